```python
import math
import jax, jax.numpy as jnp
from jax import lax
import numpy as np

D_MODEL = 1024
BATCH = 8
SEQ = 8192
DEPTH = 2

CHUNK = 64
DN_HEADS = 8
DN_DK = 128
DN_DV = 128
DN_QK_W = DN_HEADS * DN_DK
DN_V_W = DN_HEADS * DN_DV
CONV_K = 4
SGU_BLOCK = 128
SGU_GROUPS = 8
SGU_GROUP_DIM = 128
SGU_W = SGU_GROUPS * SGU_GROUP_DIM
FFN_HIDDEN = int(math.ceil(8 * D_MODEL / 3 / 256)) * 256
DEEPNORM_ALPHA = (2 * DEPTH) ** 0.25
DEEPNORM_BETA = (8 * DEPTH) ** -0.25
IN_SIZES = (DN_QK_W, DN_QK_W, DN_V_W,
            DN_V_W,
            DN_HEADS, DN_HEADS,
            SGU_W, SGU_W,
            D_MODEL, D_MODEL)
N_IN = sum(IN_SIZES)
LN_EPS = 1e-5
RMS_EPS = 1e-6

kernel_name = "hybrid_deltanet_sgu_deepnorm"


def layer_norm(x, g, b):
    xf = x.astype(jnp.float32)
    mu = jnp.mean(xf, -1, keepdims=True)
    var = jnp.mean(jnp.square(xf - mu), -1, keepdims=True)
    return ((xf - mu) * lax.rsqrt(var + LN_EPS) * g + b).astype(x.dtype)


def l2norm(x):
    xf = x.astype(jnp.float32)
    return xf * lax.rsqrt(jnp.sum(xf * xf, -1, keepdims=True) + RMS_EPS)


def causal_depthwise_conv(x, w):
    K, C = w.shape
    return lax.conv_general_dilated(
        x, w[:, None, :], window_strides=(1,), padding=[(K - 1, 0)],
        dimension_numbers=("NWC", "WIO", "NWC"), feature_group_count=C)


def gated_delta_rule(q, k, v, g, beta):
    f32 = jnp.float32
    B, S, H, dk = q.shape
    dv = v.shape[-1]
    N = S // CHUNK

    def chunks(t):
        t = t.astype(f32).reshape((B, N, CHUNK, H) + t.shape[3:])
        return jnp.moveaxis(t, 3, 1)

    q, k, v, g, beta = (chunks(t) for t in (q, k, v, g, beta))
    q = q * (dk ** -0.5)
    g = jnp.cumsum(g, axis=-1)
    idx = jnp.arange(CHUNK)
    tril = idx[:, None] >= idx[None, :]
    strict = idx[:, None] > idx[None, :]
    decay = jnp.exp(jnp.where(tril, g[..., :, None] - g[..., None, :], -jnp.inf))
    k_beta = k * beta[..., None]
    lower = jnp.einsum("bhnid,bhnjd->bhnij", k_beta, k) * decay
    lower = jnp.where(strict, lower, 0.0) + jnp.eye(CHUNK, dtype=f32)
    rhs = jnp.concatenate([v * beta[..., None], k_beta * jnp.exp(g)[..., None]], -1)
    sol = lax.linalg.triangular_solve(lower, rhs, left_side=True, lower=True,
                                      unit_diagonal=True)
    u_c, w_c = sol[..., :dv], sol[..., dv:]
    attn = jnp.einsum("bhnid,bhnjd->bhnij", q, k) * decay
    q_g = q * jnp.exp(g)[..., None]
    g_last = g[..., -1]
    k_dec = k * jnp.exp(g_last[..., None] - g)[..., None]

    def step(state, xs):
        qg_i, kd_i, u_i, w_i, a_i, gl_i = xs
        v_new = u_i - jnp.einsum("bhck,bhkv->bhcv", w_i, state)
        o_i = (jnp.einsum("bhck,bhkv->bhcv", qg_i, state)
               + jnp.einsum("bhij,bhjv->bhiv", a_i, v_new))
        state = (state * jnp.exp(gl_i)[..., None, None]
                 + jnp.einsum("bhck,bhcv->bhkv", kd_i, v_new))
        return state, o_i

    xs = tuple(jnp.moveaxis(t, 2, 0) for t in (q_g, k_dec, u_c, w_c, attn, g_last))
    state0 = jnp.zeros((B, H, dk, dv), f32)
    _, o = lax.scan(step, state0, xs)
    return jnp.transpose(o, (1, 0, 3, 2, 4)).reshape(B, S, H, dv)


def hybrid_mixer(x, w_in, conv_w, a_log, dt_bias, o_norm_w, sgu_ln_g, sgu_ln_b,
                 w_s, b_s, w_pa, w_pb, w_o):
    f32 = jnp.float32
    B, S, _ = x.shape
    proj = x @ w_in
    cuts = np.cumsum(IN_SIZES)[:-1].tolist()
    q, k, v, z, b_logit, a_logit, u, vg, ga, gb = jnp.split(proj, cuts, axis=-1)

    qkv = jax.nn.silu(causal_depthwise_conv(jnp.concatenate([q, k, v], -1), conv_w))
    q, k, v = jnp.split(qkv, [DN_QK_W, 2 * DN_QK_W], axis=-1)
    q = l2norm(q.reshape(B, S, DN_HEADS, DN_DK))
    k = l2norm(k.reshape(B, S, DN_HEADS, DN_DK))
    v = v.reshape(B, S, DN_HEADS, DN_DV)
    beta = jax.nn.sigmoid(b_logit.astype(f32))
    g = -jnp.exp(a_log.astype(f32)) * jax.nn.softplus(a_logit.astype(f32) + dt_bias)
    o = gated_delta_rule(q, k, v, g, beta)
    zf = z.reshape(B, S, DN_HEADS, DN_DV).astype(f32)
    o = o * lax.rsqrt(jnp.mean(o * o, -1, keepdims=True) + RMS_EPS) * o_norm_w * jax.nn.silu(zf)
    y_a = o.reshape(B, S, DN_V_W).astype(x.dtype)

    u = jax.nn.gelu(u, approximate=False)
    vg = layer_norm(jax.nn.gelu(vg, approximate=False), sgu_ln_g, sgu_ln_b)
    nb = S // SGU_BLOCK
    vb = vg.reshape(B, nb, SGU_BLOCK, SGU_GROUPS, SGU_GROUP_DIM)
    ch = jnp.arange(SGU_BLOCK) // CHUNK
    ws = jnp.where(ch[:, None] >= ch[None, :], w_s, 0.0)
    sp = jnp.einsum("gij,bnjgc->bnigc", ws, vb) + b_s.T[None, None, :, :, None]
    y_b = u * sp.reshape(B, S, SGU_W)

    m = jax.nn.sigmoid(ga) * (y_a @ w_pa) + jax.nn.sigmoid(gb) * (y_b @ w_pb)
    return m @ w_o


def swiglu(x, w_gate, w_up, w_down):
    return (jax.nn.silu(x @ w_gate) * (x @ w_up)) @ w_down


def _fwd_setup_inputs(seed: int = 0) -> dict:
    key = jax.random.key(seed)
    ks = jax.random.split(key, 24)
    f32 = jnp.float32
    L = DEPTH
    nrm = lambda k, shape, s: jax.random.normal(k, shape, f32) * s
    dt = jnp.exp(jax.random.uniform(ks[4], (L, DN_HEADS), f32,
                                    math.log(1e-3), math.log(1e-1)))
    return {
        "x": jax.random.normal(ks[0], (BATCH, SEQ, D_MODEL), f32),
        "w_in": nrm(ks[1], (L, D_MODEL, N_IN), D_MODEL ** -0.5),
        "conv_w": nrm(ks[2], (L, CONV_K, 2 * DN_QK_W + DN_V_W), CONV_K ** -0.5),
        "a_log": jnp.log(jax.random.uniform(ks[3], (L, DN_HEADS), f32, 1.0, 16.0)),
        "dt_bias": dt + jnp.log(-jnp.expm1(-dt)),
        "o_norm_w": 1.0 + nrm(ks[5], (L, DN_DV), 0.02),
        "sgu_ln_g": 1.0 + nrm(ks[6], (L, SGU_W), 0.02),
        "sgu_ln_b": nrm(ks[7], (L, SGU_W), 0.02),
        "w_s": nrm(ks[8], (L, SGU_GROUPS, SGU_BLOCK, SGU_BLOCK), SGU_BLOCK ** -0.5),
        "b_s": 1.0 + nrm(ks[9], (L, SGU_GROUPS, SGU_BLOCK), 0.02),
        "w_pa": nrm(ks[10], (L, DN_V_W, D_MODEL), DN_V_W ** -0.5 * DEEPNORM_BETA),
        "w_pb": nrm(ks[11], (L, SGU_W, D_MODEL), SGU_W ** -0.5 * DEEPNORM_BETA),
        "w_o": nrm(ks[12], (L, D_MODEL, D_MODEL), D_MODEL ** -0.5 * DEEPNORM_BETA),
        "ln1_g": 1.0 + nrm(ks[13], (L, D_MODEL), 0.02),
        "ln1_b": nrm(ks[14], (L, D_MODEL), 0.02),
        "w_ffn_gate": nrm(ks[15], (L, D_MODEL, FFN_HIDDEN), D_MODEL ** -0.5),
        "w_ffn_up": nrm(ks[16], (L, D_MODEL, FFN_HIDDEN), D_MODEL ** -0.5),
        "w_ffn_down": nrm(ks[17], (L, FFN_HIDDEN, D_MODEL), FFN_HIDDEN ** -0.5 * DEEPNORM_BETA),
        "ln2_g": 1.0 + nrm(ks[18], (L, D_MODEL), 0.02),
        "ln2_b": nrm(ks[19], (L, D_MODEL), 0.02),
    }


def _fwd_reference(x, w_in, conv_w, a_log, dt_bias, o_norm_w, sgu_ln_g, sgu_ln_b, w_s, b_s,
              w_pa, w_pb, w_o, ln1_g, ln1_b, w_ffn_gate, w_ffn_up, w_ffn_down,
              ln2_g, ln2_b):
    for l in range(DEPTH):
        mix = hybrid_mixer(x, w_in[l], conv_w[l], a_log[l], dt_bias[l], o_norm_w[l],
                           sgu_ln_g[l], sgu_ln_b[l], w_s[l], b_s[l],
                           w_pa[l], w_pb[l], w_o[l])
        x = layer_norm(DEEPNORM_ALPHA * x + mix, ln1_g[l], ln1_b[l])
        ffn = swiglu(x, w_ffn_gate[l], w_ffn_up[l], w_ffn_down[l])
        x = layer_norm(DEEPNORM_ALPHA * x + ffn, ln2_g[l], ln2_b[l])
    return x


import jax as _jax
import jax.numpy as _jnp

TWIN_FORMAT = 'train_step'
FWD_PARAMS = ['x', 'w_in', 'conv_w', 'a_log', 'dt_bias', 'o_norm_w', 'sgu_ln_g', 'sgu_ln_b', 'w_s', 'b_s', 'w_pa', 'w_pb', 'w_o', 'ln1_g', 'ln1_b', 'w_ffn_gate', 'w_ffn_up', 'w_ffn_down', 'ln2_g', 'ln2_b']
TWIN_WEIGHTS = ['w_in', 'conv_w', 'a_log', 'dt_bias', 'o_norm_w', 'sgu_ln_g', 'sgu_ln_b', 'w_s', 'b_s', 'w_pa', 'w_pb', 'w_o', 'ln1_g', 'ln1_b', 'w_ffn_gate', 'w_ffn_up', 'w_ffn_down', 'ln2_g', 'ln2_b']
TWIN_DIFF_INPUT = 'x'
TWIN_INPUTS = ['x', 'w_in', 'conv_w', 'a_log', 'dt_bias', 'o_norm_w', 'sgu_ln_g', 'sgu_ln_b', 'w_s', 'b_s', 'w_pa', 'w_pb', 'w_o', 'ln1_g', 'ln1_b', 'w_ffn_gate', 'w_ffn_up', 'w_ffn_down', 'ln2_g', 'ln2_b', 'loss_target', 'm_w_in', 'm_conv_w', 'm_a_log', 'm_dt_bias', 'm_o_norm_w', 'm_sgu_ln_g', 'm_sgu_ln_b', 'm_w_s', 'm_b_s', 'm_w_pa', 'm_w_pb', 'm_w_o', 'm_ln1_g', 'm_ln1_b', 'm_w_ffn_gate', 'm_w_ffn_up', 'm_w_ffn_down', 'm_ln2_g', 'm_ln2_b', 'v_w_in', 'v_conv_w', 'v_a_log', 'v_dt_bias', 'v_o_norm_w', 'v_sgu_ln_g', 'v_sgu_ln_b', 'v_w_s', 'v_b_s', 'v_w_pa', 'v_w_pb', 'v_w_o', 'v_ln1_g', 'v_ln1_b', 'v_w_ffn_gate', 'v_w_ffn_up', 'v_w_ffn_down', 'v_ln2_g', 'v_ln2_b']
TWIN_OUTPUTS = ['loss', 'grad_x', 'grad_w_in', 'grad_conv_w', 'grad_a_log', 'grad_dt_bias', 'grad_o_norm_w', 'grad_sgu_ln_g', 'grad_sgu_ln_b', 'grad_w_s', 'grad_b_s', 'grad_w_pa', 'grad_w_pb', 'grad_w_o', 'grad_ln1_g', 'grad_ln1_b', 'grad_w_ffn_gate', 'grad_w_ffn_up', 'grad_w_ffn_down', 'grad_ln2_g', 'grad_ln2_b', 'delta_w_in', 'delta_conv_w', 'delta_a_log', 'delta_dt_bias', 'delta_o_norm_w', 'delta_sgu_ln_g', 'delta_sgu_ln_b', 'delta_w_s', 'delta_b_s', 'delta_w_pa', 'delta_w_pb', 'delta_w_o', 'delta_ln1_g', 'delta_ln1_b', 'delta_w_ffn_gate', 'delta_w_ffn_up', 'delta_w_ffn_down', 'delta_ln2_g', 'delta_ln2_b', 'new_m_w_in', 'new_m_conv_w', 'new_m_a_log', 'new_m_dt_bias', 'new_m_o_norm_w', 'new_m_sgu_ln_g', 'new_m_sgu_ln_b', 'new_m_w_s', 'new_m_b_s', 'new_m_w_pa', 'new_m_w_pb', 'new_m_w_o', 'new_m_ln1_g', 'new_m_ln1_b', 'new_m_w_ffn_gate', 'new_m_w_ffn_up', 'new_m_w_ffn_down', 'new_m_ln2_g', 'new_m_ln2_b', 'new_v_w_in', 'new_v_conv_w', 'new_v_a_log', 'new_v_dt_bias', 'new_v_o_norm_w', 'new_v_sgu_ln_g', 'new_v_sgu_ln_b', 'new_v_w_s', 'new_v_b_s', 'new_v_w_pa', 'new_v_w_pb', 'new_v_w_o', 'new_v_ln1_g', 'new_v_ln1_b', 'new_v_w_ffn_gate', 'new_v_w_ffn_up', 'new_v_w_ffn_down', 'new_v_ln2_g', 'new_v_ln2_b']
TWIN_LEAF_KINDS = {'loss': 'loss', 'grad_x': 'grad_x', 'grad_w_in': 'grad_w', 'grad_conv_w': 'grad_w', 'grad_a_log': 'grad_w', 'grad_dt_bias': 'grad_w', 'grad_o_norm_w': 'grad_w', 'grad_sgu_ln_g': 'grad_w', 'grad_sgu_ln_b': 'grad_w', 'grad_w_s': 'grad_w', 'grad_b_s': 'grad_w', 'grad_w_pa': 'grad_w', 'grad_w_pb': 'grad_w', 'grad_w_o': 'grad_w', 'grad_ln1_g': 'grad_w', 'grad_ln1_b': 'grad_w', 'grad_w_ffn_gate': 'grad_w', 'grad_w_ffn_up': 'grad_w', 'grad_w_ffn_down': 'grad_w', 'grad_ln2_g': 'grad_w', 'grad_ln2_b': 'grad_w', 'delta_w_in': 'delta_w', 'delta_conv_w': 'delta_w', 'delta_a_log': 'delta_w', 'delta_dt_bias': 'delta_w', 'delta_o_norm_w': 'delta_w', 'delta_sgu_ln_g': 'delta_w', 'delta_sgu_ln_b': 'delta_w', 'delta_w_s': 'delta_w', 'delta_b_s': 'delta_w', 'delta_w_pa': 'delta_w', 'delta_w_pb': 'delta_w', 'delta_w_o': 'delta_w', 'delta_ln1_g': 'delta_w', 'delta_ln1_b': 'delta_w', 'delta_w_ffn_gate': 'delta_w', 'delta_w_ffn_up': 'delta_w', 'delta_w_ffn_down': 'delta_w', 'delta_ln2_g': 'delta_w', 'delta_ln2_b': 'delta_w', 'new_m_w_in': 'new_m', 'new_m_conv_w': 'new_m', 'new_m_a_log': 'new_m', 'new_m_dt_bias': 'new_m', 'new_m_o_norm_w': 'new_m', 'new_m_sgu_ln_g': 'new_m', 'new_m_sgu_ln_b': 'new_m', 'new_m_w_s': 'new_m', 'new_m_b_s': 'new_m', 'new_m_w_pa': 'new_m', 'new_m_w_pb': 'new_m', 'new_m_w_o': 'new_m', 'new_m_ln1_g': 'new_m', 'new_m_ln1_b': 'new_m', 'new_m_w_ffn_gate': 'new_m', 'new_m_w_ffn_up': 'new_m', 'new_m_w_ffn_down': 'new_m', 'new_m_ln2_g': 'new_m', 'new_m_ln2_b': 'new_m', 'new_v_w_in': 'new_v', 'new_v_conv_w': 'new_v', 'new_v_a_log': 'new_v', 'new_v_dt_bias': 'new_v', 'new_v_o_norm_w': 'new_v', 'new_v_sgu_ln_g': 'new_v', 'new_v_sgu_ln_b': 'new_v', 'new_v_w_s': 'new_v', 'new_v_b_s': 'new_v', 'new_v_w_pa': 'new_v', 'new_v_w_pb': 'new_v', 'new_v_w_o': 'new_v', 'new_v_ln1_g': 'new_v', 'new_v_ln1_b': 'new_v', 'new_v_w_ffn_gate': 'new_v', 'new_v_w_ffn_up': 'new_v', 'new_v_w_ffn_down': 'new_v', 'new_v_ln2_g': 'new_v', 'new_v_ln2_b': 'new_v'}


def _forward(args):
    return _fwd_reference(*[args[k] for k in FWD_PARAMS])


def _output_shape():
    def fwd():
        inp = _fwd_setup_inputs(0)
        return _fwd_reference(*[inp[k] for k in FWD_PARAMS])
    out = _jax.eval_shape(fwd)
    return out.shape, out.dtype

N_MICROBATCH = 1
ADAM_LR = 0.001
ADAM_B1 = 0.9
ADAM_B2 = 0.999
ADAM_EPS = 1e-08
ADAM_WD = 0.01
ADAM_STEP = 10
PER_EXAMPLE_BATCH_AXIS = {'x': 0, 'loss_target': 0}
SHARED_INPUTS = []
_WEIGHT_DTYPES = {'w_in': _jnp.float32, 'conv_w': _jnp.float32, 'a_log': _jnp.float32, 'dt_bias': _jnp.float32, 'o_norm_w': _jnp.float32, 'sgu_ln_g': _jnp.float32, 'sgu_ln_b': _jnp.float32, 'w_s': _jnp.float32, 'b_s': _jnp.float32, 'w_pa': _jnp.float32, 'w_pb': _jnp.float32, 'w_o': _jnp.float32, 'ln1_g': _jnp.float32, 'ln1_b': _jnp.float32, 'w_ffn_gate': _jnp.float32, 'w_ffn_up': _jnp.float32, 'w_ffn_down': _jnp.float32, 'ln2_g': _jnp.float32, 'ln2_b': _jnp.float32}
MOMENT_SCALE = {'w_in': 1.355132e-02, 'conv_w': 1.190273e-02, 'a_log': 6.019385e-02, 'dt_bias': 5.694485e-02, 'o_norm_w': 4.798893e-02, 'sgu_ln_g': 1.417061e-02, 'sgu_ln_b': 1.421774e-02, 'w_s': 1.396595e-02, 'b_s': 1.606843e-02, 'w_pa': 3.260901e-02, 'w_pb': 5.496052e-02, 'w_o': 6.472219e-02, 'ln1_g': 2.292715e+00, 'ln1_b': 9.665265e-01, 'w_ffn_gate': 3.274757e-02, 'w_ffn_up': 3.178466e-02, 'w_ffn_down': 1.052308e-01, 'ln2_g': 4.539471e+01, 'ln2_b': 2.355944e+00}


def _to_microbatches(a, axis):
    t = _jnp.moveaxis(a, axis, 0)
    t = t.reshape((N_MICROBATCH, t.shape[0] // N_MICROBATCH) + t.shape[1:])
    return _jnp.moveaxis(t, 1, axis + 1)


def setup_inputs(seed: int = 0) -> dict:
    inp = _fwd_setup_inputs(seed)
    key = _jax.random.fold_in(_jax.random.key(seed), 7919)
    shape, _ = _output_shape()
    out = dict(inp)
    out["loss_target"] = _jax.random.normal(_jax.random.fold_in(key, 0), shape, _jnp.float32)
    for i, name in enumerate(TWIN_WEIGHTS):
        w = inp[name].astype(_jnp.float32)
        if MOMENT_SCALE is None:
            s = _jnp.sqrt(_jnp.mean(_jnp.square(w)) + 1e-30)
        else:
            s = MOMENT_SCALE[name]
        km, kv = _jax.random.split(_jax.random.fold_in(key, i + 1))
        out[name] = w
        out["m_" + name] = s * _jax.random.normal(km, w.shape, _jnp.float32)
        out["v_" + name] = (s * s) * _jax.random.uniform(kv, w.shape, _jnp.float32, 0.5, 1.5)
    if N_MICROBATCH > 1:
        for name, axis in PER_EXAMPLE_BATCH_AXIS.items():
            out[name] = _to_microbatches(out[name], axis)
    return {'x': out['x'], 'w_in': out['w_in'], 'conv_w': out['conv_w'], 'a_log': out['a_log'], 'dt_bias': out['dt_bias'], 'o_norm_w': out['o_norm_w'], 'sgu_ln_g': out['sgu_ln_g'], 'sgu_ln_b': out['sgu_ln_b'], 'w_s': out['w_s'], 'b_s': out['b_s'], 'w_pa': out['w_pa'], 'w_pb': out['w_pb'], 'w_o': out['w_o'], 'ln1_g': out['ln1_g'], 'ln1_b': out['ln1_b'], 'w_ffn_gate': out['w_ffn_gate'], 'w_ffn_up': out['w_ffn_up'], 'w_ffn_down': out['w_ffn_down'], 'ln2_g': out['ln2_g'], 'ln2_b': out['ln2_b'], 'loss_target': out['loss_target'], 'm_w_in': out['m_w_in'], 'm_conv_w': out['m_conv_w'], 'm_a_log': out['m_a_log'], 'm_dt_bias': out['m_dt_bias'], 'm_o_norm_w': out['m_o_norm_w'], 'm_sgu_ln_g': out['m_sgu_ln_g'], 'm_sgu_ln_b': out['m_sgu_ln_b'], 'm_w_s': out['m_w_s'], 'm_b_s': out['m_b_s'], 'm_w_pa': out['m_w_pa'], 'm_w_pb': out['m_w_pb'], 'm_w_o': out['m_w_o'], 'm_ln1_g': out['m_ln1_g'], 'm_ln1_b': out['m_ln1_b'], 'm_w_ffn_gate': out['m_w_ffn_gate'], 'm_w_ffn_up': out['m_w_ffn_up'], 'm_w_ffn_down': out['m_w_ffn_down'], 'm_ln2_g': out['m_ln2_g'], 'm_ln2_b': out['m_ln2_b'], 'v_w_in': out['v_w_in'], 'v_conv_w': out['v_conv_w'], 'v_a_log': out['v_a_log'], 'v_dt_bias': out['v_dt_bias'], 'v_o_norm_w': out['v_o_norm_w'], 'v_sgu_ln_g': out['v_sgu_ln_g'], 'v_sgu_ln_b': out['v_sgu_ln_b'], 'v_w_s': out['v_w_s'], 'v_b_s': out['v_b_s'], 'v_w_pa': out['v_w_pa'], 'v_w_pb': out['v_w_pb'], 'v_w_o': out['v_w_o'], 'v_ln1_g': out['v_ln1_g'], 'v_ln1_b': out['v_ln1_b'], 'v_w_ffn_gate': out['v_w_ffn_gate'], 'v_w_ffn_up': out['v_w_ffn_up'], 'v_w_ffn_down': out['v_w_ffn_down'], 'v_ln2_g': out['v_ln2_g'], 'v_ln2_b': out['v_ln2_b']}


def _loss(weights, diff, rest, loss_target):
    with _jax.named_scope("forward"):
        args = {**rest, TWIN_DIFF_INPUT: diff, **{k: w.astype(_WEIGHT_DTYPES[k]) for k, w in weights.items()}}
        y = _forward(args)
    with _jax.named_scope("loss_head"):
        err = _jnp.square(y.astype(_jnp.float32) - loss_target)
        return 0.5 * _jnp.sum(_jnp.mean(err, axis=-1)) if err.ndim else 0.5 * err


def _adamw(w, g, m, v):
    m = ADAM_B1 * m + (1.0 - ADAM_B1) * g
    v = ADAM_B2 * v + (1.0 - ADAM_B2) * _jnp.square(g)
    m_hat = m / (1.0 - ADAM_B1 ** ADAM_STEP)
    v_hat = v / (1.0 - ADAM_B2 ** ADAM_STEP)
    delta = -ADAM_LR * (m_hat / (_jnp.sqrt(v_hat) + ADAM_EPS) + ADAM_WD * w)
    return delta, m, v


def reference(x, w_in, conv_w, a_log, dt_bias, o_norm_w, sgu_ln_g, sgu_ln_b, w_s, b_s, w_pa, w_pb, w_o, ln1_g, ln1_b, w_ffn_gate, w_ffn_up, w_ffn_down, ln2_g, ln2_b, loss_target, m_w_in, m_conv_w, m_a_log, m_dt_bias, m_o_norm_w, m_sgu_ln_g, m_sgu_ln_b, m_w_s, m_b_s, m_w_pa, m_w_pb, m_w_o, m_ln1_g, m_ln1_b, m_w_ffn_gate, m_w_ffn_up, m_w_ffn_down, m_ln2_g, m_ln2_b, v_w_in, v_conv_w, v_a_log, v_dt_bias, v_o_norm_w, v_sgu_ln_g, v_sgu_ln_b, v_w_s, v_b_s, v_w_pa, v_w_pb, v_w_o, v_ln1_g, v_ln1_b, v_w_ffn_gate, v_w_ffn_up, v_w_ffn_down, v_ln2_g, v_ln2_b):
    given = dict(x=x, w_in=w_in, conv_w=conv_w, a_log=a_log, dt_bias=dt_bias, o_norm_w=o_norm_w, sgu_ln_g=sgu_ln_g, sgu_ln_b=sgu_ln_b, w_s=w_s, b_s=b_s, w_pa=w_pa, w_pb=w_pb, w_o=w_o, ln1_g=ln1_g, ln1_b=ln1_b, w_ffn_gate=w_ffn_gate, w_ffn_up=w_ffn_up, w_ffn_down=w_ffn_down, ln2_g=ln2_g, ln2_b=ln2_b, loss_target=loss_target, m_w_in=m_w_in, m_conv_w=m_conv_w, m_a_log=m_a_log, m_dt_bias=m_dt_bias, m_o_norm_w=m_o_norm_w, m_sgu_ln_g=m_sgu_ln_g, m_sgu_ln_b=m_sgu_ln_b, m_w_s=m_w_s, m_b_s=m_b_s, m_w_pa=m_w_pa, m_w_pb=m_w_pb, m_w_o=m_w_o, m_ln1_g=m_ln1_g, m_ln1_b=m_ln1_b, m_w_ffn_gate=m_w_ffn_gate, m_w_ffn_up=m_w_ffn_up, m_w_ffn_down=m_w_ffn_down, m_ln2_g=m_ln2_g, m_ln2_b=m_ln2_b, v_w_in=v_w_in, v_conv_w=v_conv_w, v_a_log=v_a_log, v_dt_bias=v_dt_bias, v_o_norm_w=v_o_norm_w, v_sgu_ln_g=v_sgu_ln_g, v_sgu_ln_b=v_sgu_ln_b, v_w_s=v_w_s, v_b_s=v_b_s, v_w_pa=v_w_pa, v_w_pb=v_w_pb, v_w_o=v_w_o, v_ln1_g=v_ln1_g, v_ln1_b=v_ln1_b, v_w_ffn_gate=v_w_ffn_gate, v_w_ffn_up=v_w_ffn_up, v_w_ffn_down=v_w_ffn_down, v_ln2_g=v_ln2_g, v_ln2_b=v_ln2_b)
    weights = {n: given[n] for n in TWIN_WEIGHTS}
    shared = {n: given[n] for n in SHARED_INPUTS}
    per_example = {n: given[n] for n in ['x']}
    grad_fn = _jax.value_and_grad(_loss, argnums=(0, 1))

    def one_microbatch(ex, loss_target):
        ex = dict(ex)
        diff = ex.pop(TWIN_DIFF_INPUT)
        return grad_fn(weights, diff, {**shared, **ex}, loss_target)

    if N_MICROBATCH == 1:
        loss, (grad_w, grad_x) = one_microbatch(per_example, given["loss_target"])
    else:
        def body(carry, xs):
            loss_sum, grad_sum = carry
            l_k, (gw_k, gx_k) = one_microbatch(xs[0], xs[1])
            with _jax.named_scope("update"):
                return (loss_sum + l_k, _jax.tree.map(_jnp.add, grad_sum, gw_k)), gx_k

        init = (_jnp.zeros((), _jnp.float32), _jax.tree.map(_jnp.zeros_like, weights))
        (loss, grad_w), grad_x = _jax.lax.scan(body, init, (per_example, given["loss_target"]))
    with _jax.named_scope("update"):
        delta_w, new_m, new_v = {}, {}, {}
        for n in TWIN_WEIGHTS:
            delta_w[n], new_m[n], new_v[n] = _adamw(weights[n], grad_w[n], given["m_" + n], given["v_" + n])
    return (loss, grad_x, *[grad_w[n] for n in TWIN_WEIGHTS], *[delta_w[n] for n in TWIN_WEIGHTS],
            *[new_m[n] for n in TWIN_WEIGHTS], *[new_v[n] for n in TWIN_WEIGHTS])
```

```python
import functools
import math

import jax
import jax.numpy as jnp
from jax import lax
from jax.experimental import pallas as pl
from jax.experimental.pallas import tpu as pltpu

F32 = jnp.float32
BF16 = jnp.bfloat16
MXU_DTYPE = jnp.bfloat16
ACT = jnp.bfloat16
HIGHEST = lax.Precision.HIGHEST

DEPTH = 2
CHUNK = 64
SGU_BLOCK = 128
CONV_K = 4
DN_DK = 128
SGU_GROUP_DIM = 128
LN_EPS = 1e-5
RMS_EPS = 1e-6
ALPHA = (2 * DEPTH) ** 0.25
ADAM_LR, ADAM_B1, ADAM_B2, ADAM_EPS, ADAM_WD, ADAM_STEP = 0.001, 0.9, 0.999, 1e-08, 0.01, 10

LANES = 128
SUBLANES = 8
VMEM_LIMIT = 52 * 2 ** 20
PACK_W = 1024
PACK_ROWS = 512
N_CHIPS = 4

NN = ((1,), (0,))
NT = ((1,), (1,))
TN = ((0,), (0,))
MESH = pl.DeviceIdType.MESH
ANY = pl.BlockSpec(memory_space=pl.ANY)


def _dot(a, b, dims=NN, prec=None):
    if prec is None:
        a = a.astype(MXU_DTYPE)
        b = b.astype(MXU_DTYPE)
    return lax.dot_general(a, b, (dims, ((), ())), preferred_element_type=F32, precision=prec)


def _cparams(sem=None):
    return pltpu.CompilerParams(dimension_semantics=sem, vmem_limit_bytes=VMEM_LIMIT)


def _tile(dim, pref, unit=LANES):
    t = (min(pref, dim) // unit) * unit
    while t >= unit:
        if dim % t == 0:
            return t
        t -= unit
    return dim


def _fold8(x):
    r, n = x.shape
    return x.reshape(r // SUBLANES, SUBLANES, n).sum(axis=0)


def _sigmoid(x):
    return 1.0 / (1.0 + jnp.exp(-x))


def _gelu(x):
    return 0.5 * x * (1.0 + lax.erf(x * (2.0 ** -0.5)))


def _gelu_grad(x):
    return 0.5 * (1.0 + lax.erf(x * (2.0 ** -0.5))) + x * jnp.exp(-0.5 * x * x) * (2.0 * math.pi) ** -0.5


def _ln_hat(h):
    mu = jnp.mean(h, axis=-1, keepdims=True)
    xc = h - mu
    var = jnp.mean(xc * xc, axis=-1, keepdims=True)
    r = lax.rsqrt(var + LN_EPS)
    return xc * r, r


def _ln_bwd(dxhat, xhat, r):
    return r * (dxhat - jnp.mean(dxhat, axis=-1, keepdims=True)
                - xhat * jnp.mean(dxhat * xhat, axis=-1, keepdims=True))


def _matmul(a, b, dims, name, out_dtype=F32, add=None, coef=1.0, tm=1024, tn=1024, tk=1024):
    if dims == NN:
        (m, k), n = a.shape, b.shape[1]
    elif dims == NT:
        (m, k), n = a.shape, b.shape[0]
    else:
        (k, m), n = a.shape, b.shape[1]
    tm, tn, tk = _tile(m, tm), _tile(n, tn), _tile(k, tk)
    nk = k // tk
    a_spec = pl.BlockSpec((tk, tm), lambda j, i, q: (q, i)) if dims == TN else pl.BlockSpec((tm, tk), lambda j, i, q: (i, q))
    b_spec = pl.BlockSpec((tn, tk), lambda j, i, q: (j, q)) if dims == NT else pl.BlockSpec((tk, tn), lambda j, i, q: (q, j))
    o_spec = pl.BlockSpec((tm, tn), lambda j, i, q: (i, j))
    has_add = add is not None

    def body(*refs):
        a_ref, b_ref = refs[0], refs[1]
        add_ref = refs[2] if has_add else None
        o_ref, acc_ref = refs[2 + has_add], refs[3 + has_add]
        q = pl.program_id(2)
        part = _dot(a_ref[...], b_ref[...], dims)

        def finish(r):
            if has_add:
                r = r + coef * add_ref[...]
            o_ref[...] = r.astype(out_dtype)

        if nk == 1:
            finish(part)
        else:
            @pl.when(q == 0)
            def _():
                acc_ref[...] = part

            @pl.when(q > 0)
            def _():
                acc_ref[...] += part

            @pl.when(q == nk - 1)
            def _():
                finish(acc_ref[...])

    ins = [a, b] + ([add] if has_add else [])
    in_specs = [a_spec, b_spec] + ([o_spec] if has_add else [])
    return pl.pallas_call(
        body, name=name, grid=(n // tn, m // tm, nk),
        in_specs=in_specs, out_specs=o_spec,
        out_shape=jax.ShapeDtypeStruct((m, n), out_dtype),
        scratch_shapes=[pltpu.VMEM((tm, tn) if nk > 1 else (SUBLANES, LANES), F32)],
        compiler_params=_cparams(("parallel", "parallel", "arbitrary")),
    )(*ins)


def _conv_taps(cur_ref, halo_ref, first):
    x = cur_ref[...]
    tb = x.shape[0]
    halo = jnp.where(first, 0.0, halo_ref[...])
    xc = jnp.concatenate([halo, x], axis=0)
    return [x] + [pltpu.roll(xc, s, 0)[SUBLANES:SUBLANES + tb] for s in range(1, CONV_K)]


def _conv_fwd(projm, conv_w, d, tb):
    t = projm.shape[0]
    heads = d // DN_DK
    hb = tb // SUBLANES

    def body(cur_ref, halo_ref, w_ref, o_ref):
        i, j = pl.program_id(0), pl.program_id(1)
        taps = _conv_taps(cur_ref, halo_ref, i == 0)
        y = taps[0] * w_ref[CONV_K - 1:CONV_K, :]
        for s in range(1, CONV_K):
            y = y + taps[s] * w_ref[CONV_K - 1 - s:CONV_K - s, :]
        act = y * _sigmoid(y)
        scale = jnp.where(j == 0, DN_DK ** -0.5, 1.0)
        for h in range(heads):
            seg = act[:, h * DN_DK:(h + 1) * DN_DK]
            r = lax.rsqrt(jnp.sum(seg * seg, axis=1, keepdims=True) + RMS_EPS) * scale
            o_ref[:, h * DN_DK:(h + 1) * DN_DK] = seg * jnp.where(j < 2, r, 1.0)

    return pl.pallas_call(
        body, name="conv_fwd", grid=(t // tb, 3),
        in_specs=[pl.BlockSpec((tb, d), lambda i, j: (i, j)),
                  pl.BlockSpec((SUBLANES, d), lambda i, j: (jnp.maximum(i * hb - 1, 0), j)),
                  pl.BlockSpec((CONV_K, d), lambda i, j: (0, j))],
        out_specs=pl.BlockSpec((tb, d), lambda i, j: (i, j)),
        out_shape=jax.ShapeDtypeStruct((t, 3 * d), F32),
        compiler_params=_cparams(("parallel", "parallel")),
    )(projm, projm, conv_w)


def _conv_bwd_dy(projm, conv_w, dqkv, d, tb):
    t = projm.shape[0]
    heads = d // DN_DK
    hb = tb // SUBLANES

    def body(cur_ref, halo_ref, w_ref, dout_ref, dy_ref, dw_ref):
        j, i = pl.program_id(0), pl.program_id(1)
        taps = _conv_taps(cur_ref, halo_ref, i == 0)
        y = taps[0] * w_ref[CONV_K - 1:CONV_K, :]
        for s in range(1, CONV_K):
            y = y + taps[s] * w_ref[CONV_K - 1 - s:CONV_K - s, :]
        sg = _sigmoid(y)
        act = y * sg
        dact = sg * (1.0 + y * (1.0 - sg))
        scale = jnp.where(j == 0, DN_DK ** -0.5, 1.0)
        for h in range(heads):
            cols = slice(h * DN_DK, (h + 1) * DN_DK)
            seg = act[:, cols]
            r = lax.rsqrt(jnp.sum(seg * seg, axis=1, keepdims=True) + RMS_EPS)
            nrm = seg * r
            dout = dout_ref[:, cols]
            dn = dout * scale
            ds = jnp.where(j < 2, r * (dn - nrm * jnp.sum(dn * nrm, axis=1, keepdims=True)), dout)
            dy_ref[:, cols] = ds * dact[:, cols]
        dy = dy_ref[...]

        @pl.when(i == 0)
        def _():
            dw_ref[...] = jnp.zeros_like(dw_ref)

        for s in range(CONV_K):
            dw_ref[CONV_K - 1 - s] += _fold8(dy * taps[s])

    return pl.pallas_call(
        body, name="conv_bwd_dy", grid=(3, t // tb),
        in_specs=[pl.BlockSpec((tb, d), lambda j, i: (i, j)),
                  pl.BlockSpec((SUBLANES, d), lambda j, i: (jnp.maximum(i * hb - 1, 0), j)),
                  pl.BlockSpec((CONV_K, d), lambda j, i: (0, j)),
                  pl.BlockSpec((tb, d), lambda j, i: (i, j))],
        out_specs=[pl.BlockSpec((tb, d), lambda j, i: (i, j)),
                   pl.BlockSpec((CONV_K, SUBLANES, d), lambda j, i: (0, 0, j))],
        out_shape=[jax.ShapeDtypeStruct((t, 3 * d), F32),
                   jax.ShapeDtypeStruct((CONV_K, SUBLANES, 3 * d), F32)],
        compiler_params=_cparams(("parallel", "arbitrary")),
    )(projm, projm, conv_w, dqkv)


def _conv_bwd_dx(dy, conv_w, dprojm, d, tb):
    t = dy.shape[0]
    hb = tb // SUBLANES
    last = t // tb - 1

    def body(cur_ref, halo_ref, w_ref, alias_ref, o_ref):
        i = pl.program_id(0)
        cur = cur_ref[...]
        halo = jnp.where(i == last, 0.0, halo_ref[...])
        dc = jnp.concatenate([cur, halo], axis=0)
        acc = cur * w_ref[CONV_K - 1:CONV_K, :]
        for s in range(1, CONV_K):
            acc = acc + pltpu.roll(dc, tb + SUBLANES - s, 0)[:tb] * w_ref[CONV_K - 1 - s:CONV_K - s, :]
        o_ref[...] = acc.astype(o_ref.dtype)

    return pl.pallas_call(
        body, name="conv_bwd_dx", grid=(t // tb, 3),
        in_specs=[pl.BlockSpec((tb, d), lambda i, j: (i, j)),
                  pl.BlockSpec((SUBLANES, d), lambda i, j: (jnp.minimum((i + 1) * hb, t // SUBLANES - 1), j)),
                  pl.BlockSpec((CONV_K, d), lambda i, j: (0, j)),
                  ANY],
        out_specs=pl.BlockSpec((tb, d), lambda i, j: (i, j)),
        out_shape=jax.ShapeDtypeStruct(dprojm.shape, dprojm.dtype),
        input_output_aliases={3: 0},
        compiler_params=_cparams(("parallel", "parallel")),
    )(dy, dy, conv_w, dprojm)


def _beta_g(ba, alog, dtb):
    beta = _sigmoid(ba[:, :LANES])
    xa = ba[:, LANES:] + dtb
    softplus = jnp.maximum(xa, 0.0) + jnp.log(1.0 + jnp.exp(-jnp.abs(xa)))
    ea = jnp.exp(alog)
    return beta, -ea * softplus, ea, _sigmoid(xa)


def _inv_correction(a):
    y = -a
    p = _dot(a, a)
    steps = int(math.log2(CHUNK)) - 1
    for it in range(steps):
        y = y + p + _dot(y, p)
        if it < steps - 1:
            p = _dot(p, p)
    return y


def _chunk_masks():
    row = lax.broadcasted_iota(jnp.int32, (CHUNK, CHUNK), 0)
    col = lax.broadcasted_iota(jnp.int32, (CHUNK, CHUNK), 1)
    return row >= col, row > col, row <= col


def _col_of(mat, lane_idx, h):
    return jnp.sum(jnp.where(lane_idx == h, mat, 0.0), axis=1, keepdims=True)


def _row_of(mat, sub_idx, h):
    return jnp.sum(jnp.where(sub_idx == h, mat, 0.0), axis=0, keepdims=True)


def _dn_fwd(qkv, ba, alog, dtb, d):
    t = qkv.shape[0]
    heads = d // DN_DK
    n_chunks = t // CHUNK

    def body(qkv_ref, ba_ref, al_ref, dt_ref, o_ref, s_ref, y_ref, state):
        @pl.when(pl.program_id(0) == 0)
        def _():
            state[...] = jnp.zeros_like(state)

        tril, strict, _ = _chunk_masks()
        beta, g, _, _ = _beta_g(ba_ref[...], al_ref[...], dt_ref[...])
        gc = _dot(jnp.where(tril, 1.0, 0.0), g, NN, HIGHEST)
        gct = gc.T
        lane = lax.broadcasted_iota(jnp.int32, (CHUNK, LANES), 1)
        sub = lax.broadcasted_iota(jnp.int32, (LANES, CHUNK), 0)
        rowc = lax.broadcasted_iota(jnp.int32, (CHUNK, 1), 0)
        for h in range(heads):
            qh = qkv_ref[:, h * DN_DK:(h + 1) * DN_DK]
            kh = qkv_ref[:, d + h * DN_DK:d + (h + 1) * DN_DK]
            vh = qkv_ref[:, 2 * d + h * DN_DK:2 * d + (h + 1) * DN_DK]
            gch, gcr, bh = _col_of(gc, lane, h), _row_of(gct, sub, h), _col_of(beta, lane, h)
            dec = jnp.where(tril, jnp.exp(gch - gcr), 0.0)
            kb = kh * bh
            a = jnp.where(strict, _dot(kb, kh, NT) * dec, 0.0)
            ycor = _inv_correction(a)
            egc = jnp.exp(gch)
            rhs = jnp.concatenate([vh * bh, kb * egc], axis=1)
            sol = rhs + _dot(ycor, rhs)
            u, w = sol[:, :DN_DK], sol[:, DN_DK:]
            p = _dot(qh, kh, NT) * dec
            s0 = state[h]
            vn = u - _dot(w, s0)
            o_ref[:, h * DN_DK:(h + 1) * DN_DK] = _dot(qh * egc, s0) + _dot(p, vn)
            gl = jnp.sum(jnp.where(rowc == CHUNK - 1, gch, 0.0), axis=0, keepdims=True)
            kd = kh * jnp.exp(gl - gch)
            s_ref[0, h] = s0
            y_ref[h] = ycor
            state[h] = s0 * jnp.exp(gl) + _dot(kd, vn, TN)

    return pl.pallas_call(
        body, name="dn_fwd", grid=(n_chunks,),
        in_specs=[pl.BlockSpec((CHUNK, 3 * d), lambda i: (i, 0)),
                  pl.BlockSpec((CHUNK, 2 * LANES), lambda i: (i, 0)),
                  pl.BlockSpec((1, LANES), lambda i: (0, 0)),
                  pl.BlockSpec((1, LANES), lambda i: (0, 0))],
        out_specs=[pl.BlockSpec((CHUNK, d), lambda i: (i, 0)),
                   pl.BlockSpec((1, heads, DN_DK, DN_DK), lambda i: (i, 0, 0, 0)),
                   pl.BlockSpec((heads, CHUNK, CHUNK), lambda i: (0, i, 0))],
        out_shape=[jax.ShapeDtypeStruct((t, d), F32),
                   jax.ShapeDtypeStruct((n_chunks, heads, DN_DK, DN_DK), F32),
                   jax.ShapeDtypeStruct((heads, t, CHUNK), F32)],
        scratch_shapes=[pltpu.VMEM((heads, DN_DK, DN_DK), F32)],
        compiler_params=_cparams(("arbitrary",)),
    )(qkv, ba, alog, dtb)


def _dn_bwd(qkv, ba, alog, dtb, dout, states, ycors, d):
    t = qkv.shape[0]
    heads = d // DN_DK
    n_chunks = t // CHUNK
    rev = lambda i: n_chunks - 1 - i

    def body(qkv_ref, ba_ref, al_ref, dt_ref, do_ref, s_ref, y_ref,
             dqkv_ref, dba_ref, dal_ref, ddt_ref, dstate):
        @pl.when(pl.program_id(0) == 0)
        def _():
            dstate[...] = jnp.zeros_like(dstate)
            dal_ref[...] = jnp.zeros_like(dal_ref)
            ddt_ref[...] = jnp.zeros_like(ddt_ref)

        tril, strict, triu = _chunk_masks()
        beta, g, ea, sig_a = _beta_g(ba_ref[...], al_ref[...], dt_ref[...])
        gc = _dot(jnp.where(tril, 1.0, 0.0), g, NN, HIGHEST)
        gct = gc.T
        lane = lax.broadcasted_iota(jnp.int32, (CHUNK, LANES), 1)
        sub = lax.broadcasted_iota(jnp.int32, (LANES, CHUNK), 0)
        rowc = lax.broadcasted_iota(jnp.int32, (CHUNK, 1), 0)
        ones = jnp.ones((CHUNK, LANES), F32)
        dgc_all = jnp.zeros((CHUNK, LANES), F32)
        dbeta_all = jnp.zeros((CHUNK, LANES), F32)
        for h in range(heads):
            qc = slice(h * DN_DK, (h + 1) * DN_DK)
            kc = slice(d + h * DN_DK, d + (h + 1) * DN_DK)
            vc = slice(2 * d + h * DN_DK, 2 * d + (h + 1) * DN_DK)
            qh, kh, vh, doh = qkv_ref[:, qc], qkv_ref[:, kc], qkv_ref[:, vc], do_ref[:, qc]
            gch, gcr, bh = _col_of(gc, lane, h), _row_of(gct, sub, h), _col_of(beta, lane, h)
            dec = jnp.where(tril, jnp.exp(gch - gcr), 0.0)
            kb = kh * bh
            a = jnp.where(strict, _dot(kb, kh, NT) * dec, 0.0)
            ycor = y_ref[h]
            egc = jnp.exp(gch)
            vb, kbg = vh * bh, kb * egc
            rhs = jnp.concatenate([vb, kbg], axis=1)
            sol = rhs + _dot(ycor, rhs)
            u, w = sol[:, :DN_DK], sol[:, DN_DK:]
            p = _dot(qh, kh, NT) * dec
            s0, dsn = s_ref[0, h], dstate[h]
            vn = u - _dot(w, s0)
            gl = jnp.sum(jnp.where(rowc == CHUNK - 1, gch, 0.0), axis=0, keepdims=True)
            egl, ekd = jnp.exp(gl), jnp.exp(gl - gch)
            kd, qg = kh * ekd, qh * egc
            dvn = _dot(p, doh, TN) + _dot(kd, dsn)
            dqg = _dot(doh, s0, NT)
            dp = jnp.where(tril, _dot(doh, vn, NT), 0.0)
            dkd = _dot(vn, dsn, NT)
            dgl = jnp.sum(jnp.sum(dsn * s0, axis=1, keepdims=True), axis=0, keepdims=True) * egl
            dw = -_dot(dvn, s0, NT)
            dstate[h] = _dot(qg, doh, TN) + egl * dsn - _dot(w, dvn, TN)
            dsol = jnp.concatenate([dvn, dw], axis=1)
            drhs = dsol + _dot(ycor, dsol, TN)
            dvb, dkbg = drhs[:, :DN_DK], drhs[:, DN_DK:]
            da = jnp.where(strict, -_dot(drhs, sol, NT), 0.0)
            dma, dmp = da * dec, dp * dec
            dkb = _dot(dma, kh) + dkbg * egc
            dqkv_ref[:, qc] = _dot(dmp, kh) + dqg * egc
            dqkv_ref[:, kc] = _dot(dma, kb, TN) + _dot(dmp, qh, TN) + dkd * ekd + dkb * bh
            dqkv_ref[:, vc] = dvb * bh
            e = da * a + dp * p
            colsum = _dot(e, ones, TN, HIGHEST)
            tkd = jnp.sum(dkd * kd, axis=1, keepdims=True)
            dgc = (jnp.sum(e, axis=1, keepdims=True) + jnp.sum(dqg * qg, axis=1, keepdims=True) - tkd
                   + jnp.sum(dkbg * kbg, axis=1, keepdims=True))
            dgl = dgl + jnp.sum(tkd, axis=0, keepdims=True)
            dgc = dgc + jnp.where(rowc == CHUNK - 1, dgl, 0.0)
            dgc_all = dgc_all + jnp.where(lane == h, dgc - colsum, 0.0)
            dbeta = jnp.sum(dkb * kh, axis=1, keepdims=True) + jnp.sum(dvb * vh, axis=1, keepdims=True)
            dbeta_all = dbeta_all + jnp.where(lane == h, dbeta, 0.0)
        dg = _dot(jnp.where(triu, 1.0, 0.0), dgc_all, NN, HIGHEST)
        valid = lane < heads
        dbl = jnp.where(valid, dbeta_all * beta * (1.0 - beta), 0.0)
        dal = jnp.where(valid, -dg * ea * sig_a, 0.0)
        dba_ref[:, :LANES] = dbl.astype(dba_ref.dtype)
        dba_ref[:, LANES:] = dal.astype(dba_ref.dtype)
        dal_ref[...] += _fold8(jnp.where(valid, dg * g, 0.0))
        ddt_ref[...] += _fold8(dal)

    return pl.pallas_call(
        body, name="dn_bwd", grid=(n_chunks,),
        in_specs=[pl.BlockSpec((CHUNK, 3 * d), lambda i: (rev(i), 0)),
                  pl.BlockSpec((CHUNK, 2 * LANES), lambda i: (rev(i), 0)),
                  pl.BlockSpec((1, LANES), lambda i: (0, 0)),
                  pl.BlockSpec((1, LANES), lambda i: (0, 0)),
                  pl.BlockSpec((CHUNK, d), lambda i: (rev(i), 0)),
                  pl.BlockSpec((1, heads, DN_DK, DN_DK), lambda i: (rev(i), 0, 0, 0)),
                  pl.BlockSpec((heads, CHUNK, CHUNK), lambda i: (0, rev(i), 0))],
        out_specs=[pl.BlockSpec((CHUNK, 3 * d), lambda i: (rev(i), 0)),
                   pl.BlockSpec((CHUNK, 2 * LANES), lambda i: (rev(i), 0)),
                   pl.BlockSpec((SUBLANES, LANES), lambda i: (0, 0)),
                   pl.BlockSpec((SUBLANES, LANES), lambda i: (0, 0))],
        out_shape=[jax.ShapeDtypeStruct((t, 3 * d), F32),
                   jax.ShapeDtypeStruct((t, 2 * LANES), ACT),
                   jax.ShapeDtypeStruct((SUBLANES, LANES), F32),
                   jax.ShapeDtypeStruct((SUBLANES, LANES), F32)],
        scratch_shapes=[pltpu.VMEM((heads, DN_DK, DN_DK), F32)],
        compiler_params=_cparams(("arbitrary",)),
    )(qkv, ba, alog, dtb, dout, states, ycors)


def _sgu_mask():
    row = lax.broadcasted_iota(jnp.int32, (SGU_BLOCK, SGU_BLOCK), 0)
    col = lax.broadcasted_iota(jnp.int32, (SGU_BLOCK, SGU_BLOCK), 1)
    sh = int(math.log2(CHUNK))
    return lax.shift_right_logical(row, sh) >= lax.shift_right_logical(col, sh)


def _gate_sgu_fwd(o, projm, onw, lng, lnb, ws, bst, d):
    t = o.shape[0]
    heads, groups = d // DN_DK, d // SGU_GROUP_DIM
    tb = SGU_BLOCK
    row_spec = pl.BlockSpec((1, d), lambda i: (0, 0))

    def body(o_ref, z_ref, u_ref, v_ref, onw_ref, lng_ref, lnb_ref, ws_ref, bst_ref, ya_ref, yb_ref):
        for h in range(heads):
            cols = slice(h * DN_DK, (h + 1) * DN_DK)
            oh, zh = o_ref[:, cols], z_ref[:, cols]
            r = lax.rsqrt(jnp.mean(oh * oh, axis=1, keepdims=True) + RMS_EPS)
            ya_ref[:, cols] = (oh * r * onw_ref[:, cols] * (zh * _sigmoid(zh))).astype(ya_ref.dtype)
        xhat, _ = _ln_hat(_gelu(v_ref[...]))
        vgn = xhat * lng_ref[...] + lnb_ref[...]
        mask = _sgu_mask()
        lane = lax.broadcasted_iota(jnp.int32, (SGU_BLOCK, LANES), 1)
        bst_v = bst_ref[...]
        for gi in range(groups):
            cols = slice(gi * SGU_GROUP_DIM, (gi + 1) * SGU_GROUP_DIM)
            wsg = jnp.where(mask, ws_ref[gi], 0.0)
            sp = _dot(wsg, vgn[:, cols]) + _col_of(bst_v, lane, gi)
            yb_ref[:, cols] = (_gelu(u_ref[:, cols]) * sp).astype(yb_ref.dtype)

    return pl.pallas_call(
        body, name="gate_sgu_fwd", grid=(t // tb,),
        in_specs=[pl.BlockSpec((tb, d), lambda i: (i, 0)),
                  pl.BlockSpec((tb, d), lambda i: (i, 3)),
                  pl.BlockSpec((tb, d), lambda i: (i, 4)),
                  pl.BlockSpec((tb, d), lambda i: (i, 5)),
                  row_spec, row_spec, row_spec,
                  pl.BlockSpec((groups, SGU_BLOCK, SGU_BLOCK), lambda i: (0, 0, 0)),
                  pl.BlockSpec((SGU_BLOCK, LANES), lambda i: (0, 0))],
        out_specs=[pl.BlockSpec((tb, d), lambda i: (i, 0)), pl.BlockSpec((tb, d), lambda i: (i, 0))],
        out_shape=[jax.ShapeDtypeStruct((t, d), ACT), jax.ShapeDtypeStruct((t, d), ACT)],
        compiler_params=_cparams(("parallel",)),
    )(o, projm, projm, projm, onw, lng, lnb, ws, bst)


def _gate_sgu_bwd(dya, dyb, o, projm, onw, lng, lnb, ws, bst, dprojm, d):
    t = o.shape[0]
    heads, groups = d // DN_DK, d // SGU_GROUP_DIM
    tb = SGU_BLOCK
    row_spec = pl.BlockSpec((1, d), lambda i: (0, 0))
    acc_row = pl.BlockSpec((SUBLANES, d), lambda i: (0, 0))

    def body(dya_ref, dyb_ref, o_ref, z_ref, u_ref, v_ref, onw_ref, lng_ref, lnb_ref, ws_ref, bst_ref, alias_ref,
             do_ref, dp_ref, donw_ref, dlng_ref, dlnb_ref, dws_ref, dbst_ref):
        @pl.when(pl.program_id(0) == 0)
        def _():
            for r_ in (donw_ref, dlng_ref, dlnb_ref, dws_ref, dbst_ref):
                r_[...] = jnp.zeros_like(r_)

        donw = jnp.zeros((SUBLANES, DN_DK), F32)
        for h in range(heads):
            cols = slice(h * DN_DK, (h + 1) * DN_DK)
            oh, zh, dyah, wh = o_ref[:, cols], z_ref[:, cols], dya_ref[:, cols], onw_ref[:, cols]
            r = lax.rsqrt(jnp.mean(oh * oh, axis=1, keepdims=True) + RMS_EPS)
            on = oh * r
            sz = _sigmoid(zh)
            silu_z = zh * sz
            don = dyah * wh * silu_z
            dp_ref[:, cols] = (dyah * on * wh * (sz * (1.0 + zh * (1.0 - sz)))).astype(dp_ref.dtype)
            donw = donw + _fold8(dyah * on * silu_z)
            do_ref[:, cols] = r * (don - on * jnp.mean(don * on, axis=1, keepdims=True))
        donw_ref[...] += donw

        vgp, up = v_ref[...], u_ref[...]
        xhat, rstd = _ln_hat(_gelu(vgp))
        lng_v = lng_ref[...]
        vgn = xhat * lng_v + lnb_ref[...]
        ua = _gelu(up)
        mask = _sgu_mask()
        lane = lax.broadcasted_iota(jnp.int32, (SGU_BLOCK, LANES), 1)
        bst_v = bst_ref[...]
        dbst = jnp.zeros((SGU_BLOCK, LANES), F32)
        dvgn_parts, dua_parts = [], []
        for gi in range(groups):
            cols = slice(gi * SGU_GROUP_DIM, (gi + 1) * SGU_GROUP_DIM)
            wsg = jnp.where(mask, ws_ref[gi], 0.0)
            vg_g, dyb_g = vgn[:, cols], dyb_ref[:, cols]
            sp = _dot(wsg, vg_g) + _col_of(bst_v, lane, gi)
            dsp = dyb_g * ua[:, cols]
            dua_parts.append(dyb_g * sp)
            dws_ref[gi] += jnp.where(mask, _dot(dsp, vg_g, NT), 0.0)
            dbst = dbst + jnp.where(lane == gi, jnp.sum(dsp, axis=1, keepdims=True), 0.0)
            dvgn_parts.append(_dot(wsg, dsp, TN))
        dbst_ref[...] += dbst
        dvgn = jnp.concatenate(dvgn_parts, axis=1)
        dua = jnp.concatenate(dua_parts, axis=1)
        dlng_ref[...] += _fold8(dvgn * xhat)
        dlnb_ref[...] += _fold8(dvgn)
        dvga = _ln_bwd(dvgn * lng_v, xhat, rstd)
        dp_ref[:, d:2 * d] = (dua * _gelu_grad(up)).astype(dp_ref.dtype)
        dp_ref[:, 2 * d:] = (dvga * _gelu_grad(vgp)).astype(dp_ref.dtype)

    return pl.pallas_call(
        body, name="gate_sgu_bwd", grid=(t // tb,),
        in_specs=[pl.BlockSpec((tb, d), lambda i: (i, 0)),
                  pl.BlockSpec((tb, d), lambda i: (i, 0)),
                  pl.BlockSpec((tb, d), lambda i: (i, 0)),
                  pl.BlockSpec((tb, d), lambda i: (i, 3)),
                  pl.BlockSpec((tb, d), lambda i: (i, 4)),
                  pl.BlockSpec((tb, d), lambda i: (i, 5)),
                  row_spec, row_spec, row_spec,
                  pl.BlockSpec((groups, SGU_BLOCK, SGU_BLOCK), lambda i: (0, 0, 0)),
                  pl.BlockSpec((SGU_BLOCK, LANES), lambda i: (0, 0)),
                  ANY],
        out_specs=[pl.BlockSpec((tb, d), lambda i: (i, 0)),
                   pl.BlockSpec((tb, 3 * d), lambda i: (i, 1)),
                   pl.BlockSpec((SUBLANES, DN_DK), lambda i: (0, 0)),
                   acc_row, acc_row,
                   pl.BlockSpec((groups, SGU_BLOCK, SGU_BLOCK), lambda i: (0, 0, 0)),
                   pl.BlockSpec((SGU_BLOCK, LANES), lambda i: (0, 0))],
        out_shape=[jax.ShapeDtypeStruct((t, d), F32),
                   jax.ShapeDtypeStruct(dprojm.shape, dprojm.dtype),
                   jax.ShapeDtypeStruct((SUBLANES, DN_DK), F32),
                   jax.ShapeDtypeStruct((SUBLANES, d), F32),
                   jax.ShapeDtypeStruct((SUBLANES, d), F32),
                   jax.ShapeDtypeStruct((groups, SGU_BLOCK, SGU_BLOCK), F32),
                   jax.ShapeDtypeStruct((SGU_BLOCK, LANES), F32)],
        input_output_aliases={11: 1},
        compiler_params=_cparams(("arbitrary",)),
    )(dya, dyb, o, projm, projm, projm, onw, lng, lnb, ws, bst, dprojm)


def _mix_fwd(ya, yb, projm, x, wpa, wpb, wo, g1, b1, d, tb):
    t = x.shape[0]
    blk = pl.BlockSpec((tb, d), lambda i: (i, 0))
    wspec = pl.BlockSpec((d, d), lambda i: (0, 0))
    row_spec = pl.BlockSpec((1, d), lambda i: (0, 0))

    def body(ya_ref, yb_ref, ga_ref, gb_ref, x_ref, wpa_ref, wpb_ref, wo_ref, g_ref, b_ref,
             pa_ref, pb_ref, m_ref, h_ref, x1_ref, x1b_ref):
        pa = _dot(ya_ref[...], wpa_ref[...])
        pb = _dot(yb_ref[...], wpb_ref[...])
        m = _sigmoid(ga_ref[...]) * pa + _sigmoid(gb_ref[...]) * pb
        hres = ALPHA * x_ref[...] + _dot(m, wo_ref[...])
        xhat, _ = _ln_hat(hres)
        x1 = xhat * g_ref[...] + b_ref[...]
        pa_ref[...] = pa
        pb_ref[...] = pb
        m_ref[...] = m.astype(m_ref.dtype)
        h_ref[...] = hres
        x1_ref[...] = x1
        x1b_ref[...] = x1.astype(x1b_ref.dtype)

    f32_out = jax.ShapeDtypeStruct((t, d), F32)
    bf_out = jax.ShapeDtypeStruct((t, d), ACT)
    return pl.pallas_call(
        body, name="mix_fwd", grid=(t // tb,),
        in_specs=[blk, blk, pl.BlockSpec((tb, d), lambda i: (i, 6)), pl.BlockSpec((tb, d), lambda i: (i, 7)),
                  blk, wspec, wspec, wspec, row_spec, row_spec],
        out_specs=[blk] * 6,
        out_shape=[f32_out, f32_out, bf_out, f32_out, f32_out, bf_out],
        compiler_params=_cparams(("parallel",)),
    )(ya, yb, projm, projm, x, wpa, wpb, wo, g1, b1)


def _mix_bwd(dmix, pa, pb, projm, wpa, wpb, wo, d, tb):
    t = dmix.shape[0]
    blk = pl.BlockSpec((tb, d), lambda i: (i, 0))
    wspec = pl.BlockSpec((d, d), lambda i: (0, 0))

    def body(dmix_ref, pa_ref, pb_ref, ga_ref, gb_ref, wpa_ref, wpb_ref, wo_ref,
             dpa_ref, dpb_ref, dya_ref, dyb_ref, dg_ref):
        dm = _dot(dmix_ref[...], wo_ref[...], NT)
        sa, sb = _sigmoid(ga_ref[...]), _sigmoid(gb_ref[...])
        dpa, dpb = dm * sa, dm * sb
        dpa_ref[...] = dpa.astype(dpa_ref.dtype)
        dpb_ref[...] = dpb.astype(dpb_ref.dtype)
        dg_ref[:, :d] = (dm * pa_ref[...] * sa * (1.0 - sa)).astype(dg_ref.dtype)
        dg_ref[:, d:] = (dm * pb_ref[...] * sb * (1.0 - sb)).astype(dg_ref.dtype)
        dya_ref[...] = _dot(dpa, wpa_ref[...], NT)
        dyb_ref[...] = _dot(dpb, wpb_ref[...], NT)

    return pl.pallas_call(
        body, name="mix_bwd", grid=(t // tb,),
        in_specs=[blk, blk, blk, pl.BlockSpec((tb, d), lambda i: (i, 6)), pl.BlockSpec((tb, d), lambda i: (i, 7)),
                  wspec, wspec, wspec],
        out_specs=[blk, blk, blk, blk, pl.BlockSpec((tb, 2 * d), lambda i: (i, 3))],
        out_shape=[jax.ShapeDtypeStruct((t, d), ACT), jax.ShapeDtypeStruct((t, d), ACT),
                   jax.ShapeDtypeStruct((t, d), F32), jax.ShapeDtypeStruct((t, d), F32),
                   jax.ShapeDtypeStruct((t, 8 * d), ACT)],
        compiler_params=_cparams(("parallel",)),
    )(dmix, pa, pb, projm, projm, wpa, wpb, wo)


def _swiglu_fwd(gu, f, tb):
    t = gu.shape[0]

    def body(g_ref, u_ref, a_ref):
        gp = g_ref[...]
        a_ref[...] = (gp * _sigmoid(gp) * u_ref[...]).astype(a_ref.dtype)

    return pl.pallas_call(
        body, name="swiglu_fwd", grid=(t // tb,),
        in_specs=[pl.BlockSpec((tb, f), lambda i: (i, 0)), pl.BlockSpec((tb, f), lambda i: (i, 1))],
        out_specs=pl.BlockSpec((tb, f), lambda i: (i, 0)),
        out_shape=jax.ShapeDtypeStruct((t, f), ACT),
        compiler_params=_cparams(("parallel",)),
    )(gu, gu)


def _swiglu_bwd(da, gu, f, tb):
    t = gu.shape[0]

    def body(da_ref, g_ref, u_ref, dgu_ref):
        gp, da_v = g_ref[...], da_ref[...]
        sg = _sigmoid(gp)
        dgu_ref[:, :f] = (da_v * u_ref[...] * sg * (1.0 + gp * (1.0 - sg))).astype(dgu_ref.dtype)
        dgu_ref[:, f:] = (da_v * gp * sg).astype(dgu_ref.dtype)

    return pl.pallas_call(
        body, name="swiglu_bwd", grid=(t // tb,),
        in_specs=[pl.BlockSpec((tb, f), lambda i: (i, 0)), pl.BlockSpec((tb, f), lambda i: (i, 0)),
                  pl.BlockSpec((tb, f), lambda i: (i, 1))],
        out_specs=pl.BlockSpec((tb, 2 * f), lambda i: (i, 0)),
        out_shape=jax.ShapeDtypeStruct((t, 2 * f), ACT),
        compiler_params=_cparams(("parallel",)),
    )(da, gu, gu)


def _res_ln_fwd(x1, f, g, b, tb):
    t, d = x1.shape
    blk = pl.BlockSpec((tb, d), lambda i: (i, 0))
    row_spec = pl.BlockSpec((1, d), lambda i: (0, 0))

    def body(x_ref, f_ref, g_ref, b_ref, h_ref, y_ref, yb_ref):
        hres = ALPHA * x_ref[...] + f_ref[...]
        xhat, _ = _ln_hat(hres)
        y = xhat * g_ref[...] + b_ref[...]
        h_ref[...] = hres
        y_ref[...] = y
        yb_ref[...] = y.astype(yb_ref.dtype)

    return pl.pallas_call(
        body, name="res_ln_fwd", grid=(t // tb,),
        in_specs=[blk, blk, row_spec, row_spec], out_specs=[blk, blk, blk],
        out_shape=[jax.ShapeDtypeStruct((t, d), F32), jax.ShapeDtypeStruct((t, d), F32),
                   jax.ShapeDtypeStruct((t, d), ACT)],
        compiler_params=_cparams(("parallel",)),
    )(x1, f, g, b)


def _ln_bwd_call(dy, hres, g, tb):
    t, d = dy.shape
    blk = pl.BlockSpec((tb, d), lambda i: (i, 0))
    acc = pl.BlockSpec((SUBLANES, d), lambda i: (0, 0))

    def body(dy_ref, h_ref, g_ref, dh_ref, dhb_ref, dg_ref, db_ref):
        @pl.when(pl.program_id(0) == 0)
        def _():
            dg_ref[...] = jnp.zeros_like(dg_ref)
            db_ref[...] = jnp.zeros_like(db_ref)

        dy_v = dy_ref[...]
        xhat, r = _ln_hat(h_ref[...])
        dh = _ln_bwd(dy_v * g_ref[...], xhat, r)
        dh_ref[...] = dh
        dhb_ref[...] = dh.astype(dhb_ref.dtype)
        dg_ref[...] += _fold8(dy_v * xhat)
        db_ref[...] += _fold8(dy_v)

    return pl.pallas_call(
        body, name="ln_bwd", grid=(t // tb,),
        in_specs=[blk, blk, pl.BlockSpec((1, d), lambda i: (0, 0))],
        out_specs=[blk, blk, acc, acc],
        out_shape=[jax.ShapeDtypeStruct((t, d), F32), jax.ShapeDtypeStruct((t, d), ACT),
                   jax.ShapeDtypeStruct((SUBLANES, d), F32), jax.ShapeDtypeStruct((SUBLANES, d), F32)],
        compiler_params=_cparams(("arbitrary",)),
    )(dy, hres, g)


def _loss_head(y, target, tb):
    t, d = y.shape
    blk = pl.BlockSpec((tb, d), lambda i: (i, 0))

    def body(y_ref, t_ref, dy_ref, l_ref):
        @pl.when(pl.program_id(0) == 0)
        def _():
            l_ref[...] = jnp.zeros_like(l_ref)

        err = y_ref[...] - t_ref[...]
        dy_ref[...] = err * (1.0 / d)
        sq = _fold8(err * err)
        part = sq[:, :LANES]
        for c in range(1, d // LANES):
            part = part + sq[:, c * LANES:(c + 1) * LANES]
        l_ref[...] += part

    return pl.pallas_call(
        body, name="loss_head", grid=(t // tb,),
        in_specs=[blk, blk],
        out_specs=[blk, pl.BlockSpec((SUBLANES, LANES), lambda i: (0, 0))],
        out_shape=[jax.ShapeDtypeStruct((t, d), F32), jax.ShapeDtypeStruct((SUBLANES, LANES), F32)],
        compiler_params=_cparams(("arbitrary",)),
    )(y, target)


def _adamw(w, g, m, v):
    shape = w.shape
    cols = shape[-1]
    w2, g2, m2, v2 = (a.reshape(-1, cols) for a in (w, g, m, v))
    rows = w2.shape[0]
    tr = _tile(rows, 256, SUBLANES)
    blk = pl.BlockSpec((tr, cols), lambda i: (i, 0))

    def body(w_ref, g_ref, m_ref, v_ref, d_ref, nm_ref, nv_ref):
        g_v = g_ref[...]
        nm = ADAM_B1 * m_ref[...] + (1.0 - ADAM_B1) * g_v
        nv = ADAM_B2 * v_ref[...] + (1.0 - ADAM_B2) * (g_v * g_v)
        m_hat = nm / (1.0 - ADAM_B1 ** ADAM_STEP)
        v_hat = nv / (1.0 - ADAM_B2 ** ADAM_STEP)
        d_ref[...] = -ADAM_LR * (m_hat / (jnp.sqrt(v_hat) + ADAM_EPS) + ADAM_WD * w_ref[...])
        nm_ref[...] = nm
        nv_ref[...] = nv

    out = jax.ShapeDtypeStruct((rows, cols), F32)
    res = pl.pallas_call(
        body, name="adamw", grid=(rows // tr,),
        in_specs=[blk] * 4, out_specs=[blk] * 3, out_shape=[out] * 3,
        compiler_params=_cparams(("parallel",)),
    )(w2, g2, m2, v2)
    return tuple(r.reshape(shape) for r in res)


def _place():
    x, y, c = lax.axis_index("x"), lax.axis_index("y"), lax.axis_index("c")
    return x, y, c, [(1 - x, y), (x, 1 - y), (1 - x, 1 - y)]


def _all_gather_weights(shard):
    rows = shard.shape[0]
    half = rows // 2

    def body(x_ref, out_ref, send_sems, recv_sems, local_sem):
        x, y, c, chips = _place()
        sibling = (x, y, 1 - c)
        mine = 2 * x + y

        def part(chip_idx, core):
            return out_ref.at[chip_idx, pl.ds(pl.multiple_of(core * half, SUBLANES), half), :]

        def copy(k, src, dst, to):
            return pltpu.make_async_remote_copy(src_ref=src, dst_ref=dst, send_sem=send_sems.at[k],
                                                recv_sem=recv_sems.at[k], device_id=to, device_id_type=MESH)

        own = pltpu.make_async_copy(x_ref, out_ref.at[mine], local_sem)
        own.start()
        src_half = x_ref.at[pl.ds(pl.multiple_of(c * half, SUBLANES), half), :]
        first = [copy(j, src_half, part(mine, c), (cx, cy, c)) for j, (cx, cy) in enumerate(chips)]
        for cp in first:
            cp.start()
        passed = []
        for j, (cx, cy) in enumerate(chips):
            theirs = part(2 * cx + cy, c)
            copy(j, theirs, theirs, (cx, cy, c)).wait_recv()
            fwd = copy(3 + j, theirs, theirs, sibling)
            fwd.start()
            passed.append(fwd)
        for j, (cx, cy) in enumerate(chips):
            other = part(2 * cx + cy, 1 - c)
            copy(3 + j, other, other, sibling).wait_recv()
        for cp in first + passed:
            cp.wait_send()
        own.wait()

    return pl.pallas_call(
        body, name="all_gather_weights",
        in_specs=[ANY], out_specs=ANY,
        out_shape=jax.ShapeDtypeStruct((N_CHIPS, rows, PACK_W), shard.dtype),
        scratch_shapes=[pltpu.SemaphoreType.DMA((6,)), pltpu.SemaphoreType.DMA((6,)), pltpu.SemaphoreType.DMA],
    )(shard)


def _sibling_exchange(big, small):
    n, rows, _ = big.shape
    half = rows // 2

    def body(big_ref, small_ref, land_ref, sland_ref, send_sems, recv_sems):
        x, y, c, _ = _place()
        sibling = (x, y, 1 - c)
        src = big_ref.at[:, pl.ds(pl.multiple_of((1 - c) * half, SUBLANES), half), :]
        cps = [pltpu.make_async_remote_copy(src_ref=src, dst_ref=land_ref, send_sem=send_sems.at[0],
                                            recv_sem=recv_sems.at[0], device_id=sibling, device_id_type=MESH),
               pltpu.make_async_remote_copy(src_ref=small_ref, dst_ref=sland_ref, send_sem=send_sems.at[1],
                                            recv_sem=recv_sems.at[1], device_id=sibling, device_id_type=MESH)]
        for cp in cps:
            cp.start()
        for cp in cps:
            cp.wait()

    return pl.pallas_call(
        body, name="grad_sibling_exchange",
        in_specs=[ANY, ANY], out_specs=[ANY, ANY],
        out_shape=[jax.ShapeDtypeStruct((n, half, PACK_W), big.dtype),
                   jax.ShapeDtypeStruct(small.shape, small.dtype)],
        scratch_shapes=[pltpu.SemaphoreType.DMA((2,)), pltpu.SemaphoreType.DMA((2,))],
    )(big, small)


def _chip_exchange(big, small):
    n, half, _ = big.shape

    def body(big_ref, small_ref, land_ref, sland_ref, send_sems, recv_sems):
        x, y, c, chips = _place()
        mine = 2 * x + y
        cps = []
        for j, (cx, cy) in enumerate(chips):
            to = (cx, cy, c)
            cps.append(pltpu.make_async_remote_copy(
                src_ref=big_ref.at[2 * cx + cy], dst_ref=land_ref.at[mine], send_sem=send_sems.at[2 * j],
                recv_sem=recv_sems.at[2 * j], device_id=to, device_id_type=MESH))
            cps.append(pltpu.make_async_remote_copy(
                src_ref=small_ref, dst_ref=sland_ref.at[mine], send_sem=send_sems.at[2 * j + 1],
                recv_sem=recv_sems.at[2 * j + 1], device_id=to, device_id_type=MESH))
        for cp in cps:
            cp.start()
        for cp in cps:
            cp.wait()

    return pl.pallas_call(
        body, name="grad_chip_exchange",
        in_specs=[ANY, ANY], out_specs=[ANY, ANY],
        out_shape=[jax.ShapeDtypeStruct((n, half, PACK_W), big.dtype),
                   jax.ShapeDtypeStruct((n,) + small.shape, small.dtype)],
        scratch_shapes=[pltpu.SemaphoreType.DMA((6,)), pltpu.SemaphoreType.DMA((6,))],
    )(big, small)


def _sibling_merge(red):
    half = red.shape[0]

    def body(red_ref, out_ref, send_sem, recv_sem, local_sem):
        x, y, c, _ = _place()
        dst = out_ref.at[pl.ds(pl.multiple_of(c * half, SUBLANES), half), :]
        own = pltpu.make_async_copy(red_ref, dst, local_sem)
        own.start()
        cp = pltpu.make_async_remote_copy(src_ref=red_ref, dst_ref=dst, send_sem=send_sem, recv_sem=recv_sem,
                                          device_id=(x, y, 1 - c), device_id_type=MESH)
        cp.start()
        cp.wait()
        own.wait()

    return pl.pallas_call(
        body, name="grad_sibling_merge",
        in_specs=[ANY], out_specs=ANY,
        out_shape=jax.ShapeDtypeStruct((2 * half, PACK_W), red.dtype),
        scratch_shapes=[pltpu.SemaphoreType.DMA, pltpu.SemaphoreType.DMA, pltpu.SemaphoreType.DMA],
    )(red)


def _pair_sum(place, big, land):
    n, rows, _ = big.shape
    half = rows // 2
    nb = half // PACK_ROWS

    def body_all(place_ref, a_ref, b_ref, o_ref):
        o_ref[...] = (a_ref[...] + b_ref[...]).astype(o_ref.dtype)

    def body_own(place_ref, a_ref, b_ref, o_ref):
        o_ref[...] = a_ref[...] + b_ref[...]

    travel = pl.pallas_call(
        body_all, name="grad_pair_sum",
        grid_spec=pltpu.PrefetchScalarGridSpec(
            num_scalar_prefetch=1, grid=(n, nb),
            in_specs=[pl.BlockSpec((1, PACK_ROWS, PACK_W), lambda s, i, p: (s, p[0] * nb + i, 0)),
                      pl.BlockSpec((1, PACK_ROWS, PACK_W), lambda s, i, p: (s, i, 0))],
            out_specs=pl.BlockSpec((1, PACK_ROWS, PACK_W), lambda s, i, p: (s, i, 0))),
        out_shape=jax.ShapeDtypeStruct((n, half, PACK_W), BF16),
        compiler_params=_cparams(("parallel", "parallel")),
    )(place, big, land)
    own = pl.pallas_call(
        body_own, name="grad_pair_sum_own",
        grid_spec=pltpu.PrefetchScalarGridSpec(
            num_scalar_prefetch=1, grid=(nb,),
            in_specs=[pl.BlockSpec((1, PACK_ROWS, PACK_W), lambda i, p: (p[1], p[0] * nb + i, 0)),
                      pl.BlockSpec((1, PACK_ROWS, PACK_W), lambda i, p: (p[1], i, 0))],
            out_specs=pl.BlockSpec((1, PACK_ROWS, PACK_W), lambda i, p: (0, i, 0))),
        out_shape=jax.ShapeDtypeStruct((1, half, PACK_W), F32),
        compiler_params=_cparams(("parallel",)),
    )(place, big, land)
    return travel, own[0]


def _chip_sum(place, own, land, name):
    n, rows, _ = land.shape
    tr = _tile(rows, PACK_ROWS, SUBLANES)

    def body(place_ref, own_ref, land_ref, o_ref):
        mine = place_ref[1]
        acc = jnp.zeros(o_ref.shape, F32)
        for s in range(n):
            acc = acc + jnp.where(mine == s, own_ref[...], land_ref[s].astype(F32))
        o_ref[...] = acc

    return pl.pallas_call(
        body, name=name,
        grid_spec=pltpu.PrefetchScalarGridSpec(
            num_scalar_prefetch=1, grid=(rows // tr,),
            in_specs=[pl.BlockSpec((tr, PACK_W), lambda i, p: (i, 0)),
                      pl.BlockSpec((n, tr, PACK_W), lambda i, p: (0, i, 0))],
            out_specs=pl.BlockSpec((tr, PACK_W), lambda i, p: (i, 0))),
        out_shape=jax.ShapeDtypeStruct((rows, PACK_W), F32),
        compiler_params=_cparams(("parallel",)),
    )(place, own, land)


def _add2(a, b):
    rows = a.shape[0]
    tr = _tile(rows, PACK_ROWS, SUBLANES)
    blk = pl.BlockSpec((tr, PACK_W), lambda i: (i, 0))

    def body(a_ref, b_ref, o_ref):
        o_ref[...] = a_ref[...] + b_ref[...]

    return pl.pallas_call(
        body, name="grad_small_pair_sum", grid=(rows // tr,), in_specs=[blk, blk], out_specs=blk,
        out_shape=jax.ShapeDtypeStruct(a.shape, F32), compiler_params=_cparams(("parallel",)),
    )(a, b)


def _reduce_gradients(place, big, small):
    land, sland = _sibling_exchange(big, small)
    travel, own = _pair_sum(place, big, land)
    small_chip = _add2(small, sland)
    land2, sland2 = _chip_exchange(travel, small_chip)
    red_half = _chip_sum(place, own, land2, "grad_chip_sum")
    small_total = _chip_sum(place, small_chip, sland2, "grad_small_chip_sum")
    return _sibling_merge(red_half), small_total


_BIG = (("w_in", 2), ("w_pa", 1), ("w_pb", 1), ("w_o", 1), ("w_ffn_gate", 2), ("w_ffn_up", 2),
        ("w_ffn_down", 1))
_SMALL = ("conv_w", "a_log", "dt_bias", "o_norm_w", "sgu_ln_g", "sgu_ln_b", "w_s", "b_s",
          "ln1_g", "ln1_b", "ln2_g", "ln2_b")


def _pad_rows(flat, mult):
    rows = -(-flat.shape[-1] // (PACK_W * mult)) * mult
    pad = rows * PACK_W - flat.shape[-1]
    flat = jnp.pad(flat, [(0, 0)] * (flat.ndim - 1) + [(0, pad)])
    return flat.reshape(flat.shape[:-1] + (rows, PACK_W))


def _unshard(gathered, axis):
    if axis == 1:
        g = jnp.transpose(gathered, (1, 0, 2, 3))
        return g.reshape(g.shape[0], -1, g.shape[3])
    g = jnp.transpose(gathered, (1, 2, 0, 3))
    return g.reshape(g.shape[0], g.shape[1], -1)


def _to_shards(full, axis):
    l, r, c = full.shape
    if axis == 1:
        return jnp.transpose(full.reshape(l, N_CHIPS, r // N_CHIPS, c), (1, 0, 2, 3))
    return jnp.transpose(full.reshape(l, r, N_CHIPS, c // N_CHIPS), (2, 0, 1, 3))


def _row(v, width=None):
    v = v.reshape(1, -1).astype(F32)
    if width is not None and v.shape[1] < width:
        v = jnp.pad(v, ((0, 0), (0, width - v.shape[1])))
    return v


def _layer_consts(p, l, d):
    heads = d // DN_DK
    return dict(
        alog=_row(p["a_log"][l], LANES), dtb=_row(p["dt_bias"][l], LANES),
        onw=_row(jnp.tile(p["o_norm_w"][l], heads)),
        lng=_row(p["sgu_ln_g"][l]), lnb=_row(p["sgu_ln_b"][l]),
        ws=p["w_s"][l].astype(F32),
        bst=jnp.pad(p["b_s"][l].T, ((0, 0), (0, LANES - p["b_s"].shape[1]))),
        g1=_row(p["ln1_g"][l]), b1=_row(p["ln1_b"][l]), g2=_row(p["ln2_g"][l]), b2=_row(p["ln2_b"][l]))


def _layer_fwd(x, xb, wl, cl, d, f, tb):
    projm = _matmul(xb, wl["wm"], NN, "proj_main")
    ba = _matmul(xb, wl["wba"], NN, "proj_gates")
    qkv = _conv_fwd(projm, wl["conv"], d, tb)
    o, states, ycors = _dn_fwd(qkv, ba, cl["alog"], cl["dtb"], d)
    ya, yb = _gate_sgu_fwd(o, projm, cl["onw"], cl["lng"], cl["lnb"], cl["ws"], cl["bst"], d)
    pa, pb, m, h1, x1, x1b = _mix_fwd(ya, yb, projm, x, wl["wpa"], wl["wpb"], wl["wo"], cl["g1"], cl["b1"], d, tb)
    gu = _matmul(x1b, wl["wgu"], NN, "ffn_in")
    act = _swiglu_fwd(gu, f, tb)
    ffn = _matmul(act, wl["wd"], NN, "ffn_out", tm=512, tk=1536)
    h2, x2, x2b = _res_ln_fwd(x1, ffn, cl["g2"], cl["b2"], tb)
    saved = dict(xb=xb, projm=projm, ba=ba, qkv=qkv, o=o, states=states, ycors=ycors, ya=ya, yb=yb,
                 pa=pa, pb=pb, m=m, h1=h1, x1b=x1b, gu=gu, act=act, h2=h2)
    return x2, x2b, saved


def _layer_bwd(dx2, sv, wl, cl, d, f, tb):
    g = {}
    dh2, dh2b, dg2, db2 = _ln_bwd_call(dx2, sv["h2"], cl["g2"], tb)
    g["ln2_g"], g["ln2_b"] = dg2.sum(0), db2.sum(0)
    g["wd"] = _matmul(sv["act"], dh2b, TN, "ffn_out_dw")
    da = _matmul(dh2b, wl["wd"], NT, "ffn_out_dx")
    dgu = _swiglu_bwd(da, sv["gu"], f, tb)
    g["wgu"] = _matmul(sv["x1b"], dgu, TN, "ffn_in_dw")
    dx1 = _matmul(dgu, wl["wgu"], NT, "ffn_in_dx", add=dh2, coef=ALPHA, tk=_tile(2 * f, 1536))
    dh1, dh1b, dg1, db1 = _ln_bwd_call(dx1, sv["h1"], cl["g1"], tb)
    g["ln1_g"], g["ln1_b"] = dg1.sum(0), db1.sum(0)
    g["wo"] = _matmul(sv["m"], dh1b, TN, "wo_dw")
    dpa, dpb, dya, dyb, dprojm = _mix_bwd(dh1b, sv["pa"], sv["pb"], sv["projm"], wl["wpa"], wl["wpb"], wl["wo"], d, tb)
    g["wpa"] = _matmul(sv["ya"], dpa, TN, "wpa_dw")
    g["wpb"] = _matmul(sv["yb"], dpb, TN, "wpb_dw")
    do, dprojm, donw, dlng, dlnb, dws, dbst = _gate_sgu_bwd(
        dya, dyb, sv["o"], sv["projm"], cl["onw"], cl["lng"], cl["lnb"], cl["ws"], cl["bst"], dprojm, d)
    heads, groups = d // DN_DK, d // SGU_GROUP_DIM
    g["o_norm_w"], g["sgu_ln_g"], g["sgu_ln_b"] = donw.sum(0), dlng.sum(0), dlnb.sum(0)
    g["w_s"], g["b_s"] = dws, dbst[:, :groups].T
    dqkv, dba, dal, ddt = _dn_bwd(sv["qkv"], sv["ba"], cl["alog"], cl["dtb"], do, sv["states"], sv["ycors"], d)
    g["a_log"], g["dt_bias"] = dal.sum(0)[:heads], ddt.sum(0)[:heads]
    dy, dcw = _conv_bwd_dy(sv["projm"], wl["conv"], dqkv, d, tb)
    g["conv_w"] = dcw.sum(1)
    dprojm = _conv_bwd_dx(dy, wl["conv"], dprojm, d, tb)
    g["wm"] = _matmul(sv["xb"], dprojm, TN, "proj_main_dw")
    g["wba"] = _matmul(sv["xb"], dba, TN, "proj_gates_dw")
    dx = _matmul(dba, wl["wba"], NT, "proj_gates_dx", add=dh1, coef=ALPHA)
    dx = _matmul(dprojm, wl["wm"], NT, "proj_main_dx", add=dx)
    return dx, g


def _local_step(x, target, full, small_w):
    t, d = x.shape
    heads = d // DN_DK
    f = full["w_ffn_gate"].shape[2]
    tb = _tile(t, 256, SUBLANES)
    q4 = 4 * d
    w_in = full["w_in"]
    layers, consts = [], []
    for l in range(DEPTH):
        wba = jnp.zeros((d, 2 * LANES), w_in.dtype)
        wba = wba.at[:, :heads].set(w_in[l][:, q4:q4 + heads])
        wba = wba.at[:, LANES:LANES + heads].set(w_in[l][:, q4 + heads:q4 + 2 * heads])
        layers.append(dict(
            wm=jnp.concatenate([w_in[l][:, :q4], w_in[l][:, q4 + 2 * heads:]], axis=1), wba=wba,
            conv=full["conv_w"][l], wpa=full["w_pa"][l], wpb=full["w_pb"][l], wo=full["w_o"][l],
            wgu=jnp.concatenate([full["w_ffn_gate"][l], full["w_ffn_up"][l]], axis=1), wd=full["w_ffn_down"][l]))
        consts.append(_layer_consts(small_w, l, d))

    h, hb, saved = x, x.astype(ACT), []
    for l in range(DEPTH):
        h, hb, sv = _layer_fwd(h, hb, layers[l], consts[l], d, f, tb)
        saved.append(sv)
    dy, loss_parts = _loss_head(h, target, tb)
    grads = []
    for l in reversed(range(DEPTH)):
        dy, g = _layer_bwd(dy, saved[l], layers[l], consts[l], d, f, tb)
        grads.append(g)
    grads = grads[::-1]

    stack = lambda k: jnp.stack([g[k] for g in grads])
    gm, gba, ggu = stack("wm"), stack("wba"), stack("wgu")
    out = {
        "w_in": jnp.concatenate([gm[:, :, :q4], gba[:, :, :heads], gba[:, :, LANES:LANES + heads], gm[:, :, q4:]], axis=2),
        "w_pa": stack("wpa"), "w_pb": stack("wpb"), "w_o": stack("wo"),
        "w_ffn_gate": ggu[:, :, :f], "w_ffn_up": ggu[:, :, f:], "w_ffn_down": stack("wd")}
    for k in _SMALL:
        out[k] = stack(k)
    return loss_parts, dy, out


def kernel(x, w_in, conv_w, a_log, dt_bias, o_norm_w, sgu_ln_g, sgu_ln_b, w_s, b_s, w_pa, w_pb, w_o, ln1_g, ln1_b, w_ffn_gate, w_ffn_up, w_ffn_down, ln2_g, ln2_b, loss_target, m_w_in, m_conv_w, m_a_log, m_dt_bias, m_o_norm_w, m_sgu_ln_g, m_sgu_ln_b, m_w_s, m_b_s, m_w_pa, m_w_pb, m_w_o, m_ln1_g, m_ln1_b, m_w_ffn_gate, m_w_ffn_up, m_w_ffn_down, m_ln2_g, m_ln2_b, v_w_in, v_conv_w, v_a_log, v_dt_bias, v_o_norm_w, v_sgu_ln_g, v_sgu_ln_b, v_w_s, v_b_s, v_w_pa, v_w_pb, v_w_o, v_ln1_g, v_ln1_b, v_w_ffn_gate, v_w_ffn_up, v_w_ffn_down, v_ln2_g, v_ln2_b):
    names = ("w_in", "conv_w", "a_log", "dt_bias", "o_norm_w", "sgu_ln_g", "sgu_ln_b", "w_s", "b_s", "w_pa",
             "w_pb", "w_o", "ln1_g", "ln1_b", "w_ffn_gate", "w_ffn_up", "w_ffn_down", "ln2_g", "ln2_b")
    w = dict(zip(names, (w_in, conv_w, a_log, dt_bias, o_norm_w, sgu_ln_g, sgu_ln_b, w_s, b_s, w_pa, w_pb, w_o,
                         ln1_g, ln1_b, w_ffn_gate, w_ffn_up, w_ffn_down, ln2_g, ln2_b)))
    mom = dict(zip(names, (m_w_in, m_conv_w, m_a_log, m_dt_bias, m_o_norm_w, m_sgu_ln_g, m_sgu_ln_b, m_w_s, m_b_s,
                           m_w_pa, m_w_pb, m_w_o, m_ln1_g, m_ln1_b, m_w_ffn_gate, m_w_ffn_up, m_w_ffn_down,
                           m_ln2_g, m_ln2_b)))
    var = dict(zip(names, (v_w_in, v_conv_w, v_a_log, v_dt_bias, v_o_norm_w, v_sgu_ln_g, v_sgu_ln_b, v_w_s, v_b_s,
                           v_w_pa, v_w_pb, v_w_o, v_ln1_g, v_ln1_b, v_w_ffn_gate, v_w_ffn_up, v_w_ffn_down,
                           v_ln2_g, v_ln2_b)))
    chip = 2 * lax.axis_index("x") + lax.axis_index("y")
    place = jnp.stack([lax.axis_index("c"), chip]).astype(jnp.int32)

    conv_bits = lax.bitcast_convert_type(conv_w, BF16)
    pieces = [w[k].astype(BF16).reshape(-1) for k, _ in _BIG] + [conv_bits.reshape(-1)]
    sizes = [p.shape[0] for p in pieces]
    shard = _pad_rows(jnp.concatenate(pieces), 2 * PACK_ROWS)
    gathered = _all_gather_weights(shard).reshape(N_CHIPS, -1)
    full, off = {}, 0
    for (k, axis), n in zip(_BIG, sizes):
        full[k] = _unshard(gathered[:, off:off + n].reshape((N_CHIPS,) + w[k].shape), axis)
        off += n
    conv_g = gathered[:, off:off + sizes[-1]].reshape((N_CHIPS,) + conv_bits.shape)
    full["conv_w"] = _unshard(lax.bitcast_convert_type(conv_g, F32), 2)

    small_w = {k: w[k] for k in _SMALL if k != "conv_w"}
    loss_parts, grad_x, g = _local_step(x[0], loss_target[0], full, small_w)

    big = jnp.concatenate([_to_shards(g[k], axis).reshape(N_CHIPS, -1) for k, axis in _BIG], axis=1)
    big = _pad_rows(big, 2 * PACK_ROWS)
    small_sizes = [g[k].size for k in _SMALL]
    small = _pad_rows(jnp.concatenate([g[k].reshape(-1) for k in _SMALL]), SUBLANES)
    red, small_total = _reduce_gradients(place, big, small)
    red, small_total = red.reshape(-1), small_total.reshape(-1)
    grads, off = {}, 0
    for (k, _), n in zip(_BIG, sizes):
        grads[k] = red[off:off + n].reshape(w[k].shape)
        off += n
    off = 0
    for k, n in zip(_SMALL, small_sizes):
        grads[k] = small_total[off:off + n].reshape(g[k].shape)
        off += n
    grads["conv_w"] = lax.dynamic_index_in_dim(_to_shards(grads["conv_w"], 2), chip, 0, keepdims=False)

    delta, new_m, new_v = {}, {}, {}
    for k in [k for k, _ in _BIG] + ["conv_w"]:
        delta[k], new_m[k], new_v[k] = _adamw(w[k], grads[k], mom[k], var[k])
    rep = [k for k in _SMALL if k != "conv_w"]
    pack = lambda dct: _pad_rows(jnp.concatenate([dct[k].reshape(-1) for k in rep]), SUBLANES)
    packed = _adamw(pack(w), pack(grads), pack(mom), pack(var))
    off = 0
    for k in rep:
        n = w[k].size
        for dst, src in zip((delta, new_m, new_v), packed):
            dst[k] = src.reshape(-1)[off:off + n].reshape(w[k].shape)
        off += n

    loss = 0.5 * lax.psum(jnp.sum(loss_parts), ("x", "y", "c")) / x.shape[-1]
    return (loss, grad_x[None], *[grads[k] for k in names], *[delta[k] for k in names],
            *[new_m[k] for k in names], *[new_v[k] for k in names])
```

```python
import math

import jax
import jax.numpy as jnp
from jax import lax
from jax.experimental import pallas as pl
from jax.experimental.pallas import tpu as pltpu

F32 = jnp.float32
BF16 = jnp.bfloat16
MXU_DTYPE = jnp.bfloat16
ACT = jnp.bfloat16
HIGHEST = lax.Precision.HIGHEST

DEPTH = 2
CHUNK = 64
SGU_BLOCK = 128
CONV_K = 4
DN_DK = 128
SGU_GROUP_DIM = 128
LN_EPS = 1e-5
RMS_EPS = 1e-6
ALPHA = (2 * DEPTH) ** 0.25
ADAM_LR, ADAM_B1, ADAM_B2, ADAM_EPS, ADAM_WD, ADAM_STEP = 0.001, 0.9, 0.999, 1e-08, 0.01, 10

LANES = 128
SUBLANES = 8
VMEM_LIMIT = 52 * 2 ** 20
PACK_W = 1024
N_CHIPS = 4

NN = ((1,), (0,))
NT = ((1,), (1,))
TN = ((0,), (0,))
MESH = pl.DeviceIdType.MESH
ANY = pl.BlockSpec(memory_space=pl.ANY)


def _dot(a, b, dims=NN, prec=None):
    if prec is None:
        a = a.astype(MXU_DTYPE)
        b = b.astype(MXU_DTYPE)
    return lax.dot_general(a, b, (dims, ((), ())), preferred_element_type=F32, precision=prec)


def _cparams(sem=None):
    return pltpu.CompilerParams(dimension_semantics=sem, vmem_limit_bytes=VMEM_LIMIT)


def _tile(dim, pref, unit=LANES):
    t = (min(pref, dim) // unit) * unit
    while t >= unit:
        if dim % t == 0:
            return t
        t -= unit
    return dim


def _fold8(x):
    r, n = x.shape
    return x.reshape(r // SUBLANES, SUBLANES, n).sum(axis=0)


def _sigmoid(x):
    return 1.0 / (1.0 + jnp.exp(-x))


def _gelu(x):
    return 0.5 * x * (1.0 + lax.erf(x * (2.0 ** -0.5)))


def _gelu_grad(x):
    return 0.5 * (1.0 + lax.erf(x * (2.0 ** -0.5))) + x * jnp.exp(-0.5 * x * x) * (2.0 * math.pi) ** -0.5


def _ln_hat(h):
    mu = jnp.mean(h, axis=-1, keepdims=True)
    xc = h - mu
    var = jnp.mean(xc * xc, axis=-1, keepdims=True)
    r = lax.rsqrt(var + LN_EPS)
    return xc * r, r


def _ln_bwd(dxhat, xhat, r):
    return r * (dxhat - jnp.mean(dxhat, axis=-1, keepdims=True)
                - xhat * jnp.mean(dxhat * xhat, axis=-1, keepdims=True))


def _matmul(a, b, dims, name, out_dtype=F32, add=None, coef=1.0, tm=1024, tn=1024, tk=1024):
    if dims == NN:
        (m, k), n = a.shape, b.shape[1]
    elif dims == NT:
        (m, k), n = a.shape, b.shape[0]
    else:
        (k, m), n = a.shape, b.shape[1]
    tm, tn, tk = _tile(m, tm), _tile(n, tn), _tile(k, tk)
    nk = k // tk
    a_spec = pl.BlockSpec((tk, tm), lambda j, i, q: (q, i)) if dims == TN else pl.BlockSpec((tm, tk), lambda j, i, q: (i, q))
    b_spec = pl.BlockSpec((tn, tk), lambda j, i, q: (j, q)) if dims == NT else pl.BlockSpec((tk, tn), lambda j, i, q: (q, j))
    o_spec = pl.BlockSpec((tm, tn), lambda j, i, q: (i, j))
    has_add = add is not None

    def body(*refs):
        a_ref, b_ref = refs[0], refs[1]
        add_ref = refs[2] if has_add else None
        o_ref, acc_ref = refs[2 + has_add], refs[3 + has_add]
        q = pl.program_id(2)
        part = _dot(a_ref[...], b_ref[...], dims)

        def finish(r):
            if has_add:
                r = r + coef * add_ref[...]
            o_ref[...] = r.astype(out_dtype)

        if nk == 1:
            finish(part)
        else:
            @pl.when(q == 0)
            def _():
                acc_ref[...] = part

            @pl.when(q > 0)
            def _():
                acc_ref[...] += part

            @pl.when(q == nk - 1)
            def _():
                finish(acc_ref[...])

    ins = [a, b] + ([add] if has_add else [])
    in_specs = [a_spec, b_spec] + ([o_spec] if has_add else [])
    return pl.pallas_call(
        body, name=name, grid=(n // tn, m // tm, nk),
        in_specs=in_specs, out_specs=o_spec,
        out_shape=jax.ShapeDtypeStruct((m, n), out_dtype),
        scratch_shapes=[pltpu.VMEM((tm, tn) if nk > 1 else (SUBLANES, LANES), F32)],
        compiler_params=_cparams(("parallel", "parallel", "arbitrary")),
    )(*ins)


def _conv_taps(cur_ref, halo_ref, first):
    x = cur_ref[...]
    tb = x.shape[0]
    halo = jnp.where(first, 0.0, halo_ref[...])
    xc = jnp.concatenate([halo, x], axis=0)
    return [x] + [pltpu.roll(xc, s, 0)[SUBLANES:SUBLANES + tb] for s in range(1, CONV_K)]


def _conv_fwd(projm, conv_w, d, tb):
    t = projm.shape[0]
    heads = d // DN_DK
    hb = tb // SUBLANES

    def body(cur_ref, halo_ref, w_ref, o_ref):
        i, j = pl.program_id(0), pl.program_id(1)
        taps = _conv_taps(cur_ref, halo_ref, i == 0)
        y = taps[0] * w_ref[CONV_K - 1:CONV_K, :]
        for s in range(1, CONV_K):
            y = y + taps[s] * w_ref[CONV_K - 1 - s:CONV_K - s, :]
        act = y * _sigmoid(y)
        scale = jnp.where(j == 0, DN_DK ** -0.5, 1.0)
        for h in range(heads):
            seg = act[:, h * DN_DK:(h + 1) * DN_DK]
            r = lax.rsqrt(jnp.sum(seg * seg, axis=1, keepdims=True) + RMS_EPS) * scale
            o_ref[:, h * DN_DK:(h + 1) * DN_DK] = seg * jnp.where(j < 2, r, 1.0)

    return pl.pallas_call(
        body, name="conv_fwd", grid=(t // tb, 3),
        in_specs=[pl.BlockSpec((tb, d), lambda i, j: (i, j)),
                  pl.BlockSpec((SUBLANES, d), lambda i, j: (jnp.maximum(i * hb - 1, 0), j)),
                  pl.BlockSpec((CONV_K, d), lambda i, j: (0, j))],
        out_specs=pl.BlockSpec((tb, d), lambda i, j: (i, j)),
        out_shape=jax.ShapeDtypeStruct((t, 3 * d), F32),
        compiler_params=_cparams(("parallel", "parallel")),
    )(projm, projm, conv_w)


def _conv_bwd_dy(projm, conv_w, dqkv, d, tb):
    t = projm.shape[0]
    heads = d // DN_DK
    hb = tb // SUBLANES

    def body(cur_ref, halo_ref, w_ref, dout_ref, dy_ref, dw_ref):
        j, i = pl.program_id(0), pl.program_id(1)
        taps = _conv_taps(cur_ref, halo_ref, i == 0)
        y = taps[0] * w_ref[CONV_K - 1:CONV_K, :]
        for s in range(1, CONV_K):
            y = y + taps[s] * w_ref[CONV_K - 1 - s:CONV_K - s, :]
        sg = _sigmoid(y)
        act = y * sg
        dact = sg * (1.0 + y * (1.0 - sg))
        scale = jnp.where(j == 0, DN_DK ** -0.5, 1.0)
        for h in range(heads):
            cols = slice(h * DN_DK, (h + 1) * DN_DK)
            seg = act[:, cols]
            r = lax.rsqrt(jnp.sum(seg * seg, axis=1, keepdims=True) + RMS_EPS)
            nrm = seg * r
            dout = dout_ref[:, cols]
            dn = dout * scale
            ds = jnp.where(j < 2, r * (dn - nrm * jnp.sum(dn * nrm, axis=1, keepdims=True)), dout)
            dy_ref[:, cols] = ds * dact[:, cols]
        dy = dy_ref[...]

        @pl.when(i == 0)
        def _():
            dw_ref[...] = jnp.zeros_like(dw_ref)

        for s in range(CONV_K):
            dw_ref[CONV_K - 1 - s] += _fold8(dy * taps[s])

    return pl.pallas_call(
        body, name="conv_bwd_dy", grid=(3, t // tb),
        in_specs=[pl.BlockSpec((tb, d), lambda j, i: (i, j)),
                  pl.BlockSpec((SUBLANES, d), lambda j, i: (jnp.maximum(i * hb - 1, 0), j)),
                  pl.BlockSpec((CONV_K, d), lambda j, i: (0, j)),
                  pl.BlockSpec((tb, d), lambda j, i: (i, j))],
        out_specs=[pl.BlockSpec((tb, d), lambda j, i: (i, j)),
                   pl.BlockSpec((CONV_K, SUBLANES, d), lambda j, i: (0, 0, j))],
        out_shape=[jax.ShapeDtypeStruct((t, 3 * d), F32),
                   jax.ShapeDtypeStruct((CONV_K, SUBLANES, 3 * d), F32)],
        compiler_params=_cparams(("parallel", "arbitrary")),
    )(projm, projm, conv_w, dqkv)


def _conv_bwd_dx(dy, conv_w, dprojm, d, tb):
    t = dy.shape[0]
    hb = tb // SUBLANES
    last = t // tb - 1

    def body(cur_ref, halo_ref, w_ref, alias_ref, o_ref):
        i = pl.program_id(0)
        cur = cur_ref[...]
        halo = jnp.where(i == last, 0.0, halo_ref[...])
        dc = jnp.concatenate([cur, halo], axis=0)
        acc = cur * w_ref[CONV_K - 1:CONV_K, :]
        for s in range(1, CONV_K):
            acc = acc + pltpu.roll(dc, tb + SUBLANES - s, 0)[:tb] * w_ref[CONV_K - 1 - s:CONV_K - s, :]
        o_ref[...] = acc.astype(o_ref.dtype)

    return pl.pallas_call(
        body, name="conv_bwd_dx", grid=(t // tb, 3),
        in_specs=[pl.BlockSpec((tb, d), lambda i, j: (i, j)),
                  pl.BlockSpec((SUBLANES, d), lambda i, j: (jnp.minimum((i + 1) * hb, t // SUBLANES - 1), j)),
                  pl.BlockSpec((CONV_K, d), lambda i, j: (0, j)),
                  ANY],
        out_specs=pl.BlockSpec((tb, d), lambda i, j: (i, j)),
        out_shape=jax.ShapeDtypeStruct(dprojm.shape, dprojm.dtype),
        input_output_aliases={3: 0},
        compiler_params=_cparams(("parallel", "parallel")),
    )(dy, dy, conv_w, dprojm)


def _beta_g(ba, alog, dtb):
    beta = _sigmoid(ba[:, :LANES])
    xa = ba[:, LANES:] + dtb
    softplus = jnp.maximum(xa, 0.0) + jnp.log(1.0 + jnp.exp(-jnp.abs(xa)))
    ea = jnp.exp(alog)
    return beta, -ea * softplus, ea, _sigmoid(xa)


def _inv_corrections(mats):
    ys = [-a for a in mats]
    ps = [_dot(a, a) for a in mats]
    steps = int(math.log2(CHUNK)) - 1
    for it in range(steps):
        ys = [y + p + _dot(y, p) for y, p in zip(ys, ps)]
        if it < steps - 1:
            ps = [_dot(p, p) for p in ps]
    return ys


def _chunk_masks():
    row = lax.broadcasted_iota(jnp.int32, (CHUNK, CHUNK), 0)
    col = lax.broadcasted_iota(jnp.int32, (CHUNK, CHUNK), 1)
    return row >= col, row > col, row <= col


def _col_of(mat, lane_idx, h):
    return jnp.sum(jnp.where(lane_idx == h, mat, 0.0), axis=1, keepdims=True)


def _row_of(mat, sub_idx, h):
    return jnp.sum(jnp.where(sub_idx == h, mat, 0.0), axis=0, keepdims=True)


def _dn_fwd(qkv, ba, alog, dtb, d):
    t = qkv.shape[0]
    heads = d // DN_DK
    n_chunks = t // CHUNK

    def body(qkv_ref, ba_ref, al_ref, dt_ref, o_ref, s_ref, y_ref, state):
        @pl.when(pl.program_id(0) == 0)
        def _():
            state[...] = jnp.zeros_like(state)

        tril, strict, _ = _chunk_masks()
        beta, g, _, _ = _beta_g(ba_ref[...], al_ref[...], dt_ref[...])
        gc = _dot(jnp.where(tril, 1.0, 0.0), g, NN, HIGHEST)
        gct = gc.T
        lane = lax.broadcasted_iota(jnp.int32, (CHUNK, LANES), 1)
        sub = lax.broadcasted_iota(jnp.int32, (LANES, CHUNK), 0)
        rowc = lax.broadcasted_iota(jnp.int32, (CHUNK, 1), 0)
        hs = range(heads)
        q = [qkv_ref[:, h * DN_DK:(h + 1) * DN_DK] for h in hs]
        k = [qkv_ref[:, d + h * DN_DK:d + (h + 1) * DN_DK] for h in hs]
        v = [qkv_ref[:, 2 * d + h * DN_DK:2 * d + (h + 1) * DN_DK] for h in hs]
        s0 = [state[h] for h in hs]
        gch = [_col_of(gc, lane, h) for h in hs]
        bh = [_col_of(beta, lane, h) for h in hs]
        dec = [jnp.where(tril, jnp.exp(gch[h] - _row_of(gct, sub, h)), 0.0) for h in hs]
        egc = [jnp.exp(gch[h]) for h in hs]
        gl = [jnp.sum(jnp.where(rowc == CHUNK - 1, gch[h], 0.0), axis=0, keepdims=True) for h in hs]
        kb = [k[h] * bh[h] for h in hs]
        a = [jnp.where(strict, _dot(kb[h], k[h], NT) * dec[h], 0.0) for h in hs]
        p = [_dot(q[h], k[h], NT) * dec[h] for h in hs]
        ycor = _inv_corrections(a)
        rhs = [jnp.concatenate([v[h] * bh[h], kb[h] * egc[h]], axis=1) for h in hs]
        sol = [rhs[h] + _dot(ycor[h], rhs[h]) for h in hs]
        vn = [sol[h][:, :DN_DK] - _dot(sol[h][:, DN_DK:], s0[h]) for h in hs]
        o = [_dot(q[h] * egc[h], s0[h]) + _dot(p[h], vn[h]) for h in hs]
        s_new = [s0[h] * jnp.exp(gl[h]) + _dot(k[h] * jnp.exp(gl[h] - gch[h]), vn[h], TN) for h in hs]
        for h in hs:
            o_ref[:, h * DN_DK:(h + 1) * DN_DK] = o[h]
            s_ref[0, h] = s0[h]
            y_ref[h] = ycor[h]
            state[h] = s_new[h]

    return pl.pallas_call(
        body, name="dn_fwd", grid=(n_chunks,),
        in_specs=[pl.BlockSpec((CHUNK, 3 * d), lambda i: (i, 0)),
                  pl.BlockSpec((CHUNK, 2 * LANES), lambda i: (i, 0)),
                  pl.BlockSpec((1, LANES), lambda i: (0, 0)),
                  pl.BlockSpec((1, LANES), lambda i: (0, 0))],
        out_specs=[pl.BlockSpec((CHUNK, d), lambda i: (i, 0)),
                   pl.BlockSpec((1, heads, DN_DK, DN_DK), lambda i: (i, 0, 0, 0)),
                   pl.BlockSpec((heads, CHUNK, CHUNK), lambda i: (0, i, 0))],
        out_shape=[jax.ShapeDtypeStruct((t, d), F32),
                   jax.ShapeDtypeStruct((n_chunks, heads, DN_DK, DN_DK), F32),
                   jax.ShapeDtypeStruct((heads, t, CHUNK), F32)],
        scratch_shapes=[pltpu.VMEM((heads, DN_DK, DN_DK), F32)],
        compiler_params=_cparams(("arbitrary",)),
    )(qkv, ba, alog, dtb)


def _dn_bwd(qkv, ba, alog, dtb, dout, states, ycors, d):
    t = qkv.shape[0]
    heads = d // DN_DK
    n_chunks = t // CHUNK
    rev = lambda i: n_chunks - 1 - i

    def body(qkv_ref, ba_ref, al_ref, dt_ref, do_ref, s_ref, y_ref,
             dqkv_ref, dba_ref, dal_ref, ddt_ref, dstate):
        @pl.when(pl.program_id(0) == 0)
        def _():
            dstate[...] = jnp.zeros_like(dstate)
            dal_ref[...] = jnp.zeros_like(dal_ref)
            ddt_ref[...] = jnp.zeros_like(ddt_ref)

        tril, strict, triu = _chunk_masks()
        beta, g, ea, sig_a = _beta_g(ba_ref[...], al_ref[...], dt_ref[...])
        gc = _dot(jnp.where(tril, 1.0, 0.0), g, NN, HIGHEST)
        gct = gc.T
        lane = lax.broadcasted_iota(jnp.int32, (CHUNK, LANES), 1)
        sub = lax.broadcasted_iota(jnp.int32, (LANES, CHUNK), 0)
        rowc = lax.broadcasted_iota(jnp.int32, (CHUNK, 1), 0)
        ones = jnp.ones((CHUNK, LANES), F32)
        hs = range(heads)
        rsum = lambda x_: jnp.sum(x_, axis=1, keepdims=True)
        q = [qkv_ref[:, h * DN_DK:(h + 1) * DN_DK] for h in hs]
        k = [qkv_ref[:, d + h * DN_DK:d + (h + 1) * DN_DK] for h in hs]
        v = [qkv_ref[:, 2 * d + h * DN_DK:2 * d + (h + 1) * DN_DK] for h in hs]
        dout_h = [do_ref[:, h * DN_DK:(h + 1) * DN_DK] for h in hs]
        s0 = [s_ref[0, h] for h in hs]
        dsn = [dstate[h] for h in hs]
        ycor = [y_ref[h] for h in hs]
        gch = [_col_of(gc, lane, h) for h in hs]
        bh = [_col_of(beta, lane, h) for h in hs]
        dec = [jnp.where(tril, jnp.exp(gch[h] - _row_of(gct, sub, h)), 0.0) for h in hs]
        egc = [jnp.exp(gch[h]) for h in hs]
        gl = [jnp.sum(jnp.where(rowc == CHUNK - 1, gch[h], 0.0), axis=0, keepdims=True) for h in hs]
        egl = [jnp.exp(gl[h]) for h in hs]
        ekd = [jnp.exp(gl[h] - gch[h]) for h in hs]
        kb = [k[h] * bh[h] for h in hs]
        kd = [k[h] * ekd[h] for h in hs]
        qg = [q[h] * egc[h] for h in hs]
        kbg = [kb[h] * egc[h] for h in hs]
        a = [jnp.where(strict, _dot(kb[h], k[h], NT) * dec[h], 0.0) for h in hs]
        p = [_dot(q[h], k[h], NT) * dec[h] for h in hs]
        rhs = [jnp.concatenate([v[h] * bh[h], kbg[h]], axis=1) for h in hs]
        sol = [rhs[h] + _dot(ycor[h], rhs[h]) for h in hs]
        w = [sol[h][:, DN_DK:] for h in hs]
        vn = [sol[h][:, :DN_DK] - _dot(w[h], s0[h]) for h in hs]
        dvn = [_dot(p[h], dout_h[h], TN) + _dot(kd[h], dsn[h]) for h in hs]
        dqg = [_dot(dout_h[h], s0[h], NT) for h in hs]
        dp = [jnp.where(tril, _dot(dout_h[h], vn[h], NT), 0.0) for h in hs]
        dkd = [_dot(vn[h], dsn[h], NT) for h in hs]
        dw = [-_dot(dvn[h], s0[h], NT) for h in hs]
        ds_new = [_dot(qg[h], dout_h[h], TN) + egl[h] * dsn[h] - _dot(w[h], dvn[h], TN) for h in hs]
        dgl = [jnp.sum(rsum(dsn[h] * s0[h]), axis=0, keepdims=True) * egl[h] for h in hs]
        dsol = [jnp.concatenate([dvn[h], dw[h]], axis=1) for h in hs]
        drhs = [dsol[h] + _dot(ycor[h], dsol[h], TN) for h in hs]
        dvb = [drhs[h][:, :DN_DK] for h in hs]
        dkbg = [drhs[h][:, DN_DK:] for h in hs]
        da = [jnp.where(strict, -_dot(drhs[h], sol[h], NT), 0.0) for h in hs]
        dma = [da[h] * dec[h] for h in hs]
        dmp = [dp[h] * dec[h] for h in hs]
        dkb = [_dot(dma[h], k[h]) + dkbg[h] * egc[h] for h in hs]
        dq = [_dot(dmp[h], k[h]) + dqg[h] * egc[h] for h in hs]
        dk = [_dot(dma[h], kb[h], TN) + _dot(dmp[h], q[h], TN) + dkd[h] * ekd[h] + dkb[h] * bh[h] for h in hs]
        e = [da[h] * a[h] + dp[h] * p[h] for h in hs]
        colsum = [_dot(e[h], ones, TN, HIGHEST) for h in hs]
        tkd = [rsum(dkd[h] * kd[h]) for h in hs]
        dgc_all = jnp.zeros((CHUNK, LANES), F32)
        dbeta_all = jnp.zeros((CHUNK, LANES), F32)
        for h in hs:
            dgc = rsum(e[h]) + rsum(dqg[h] * qg[h]) - tkd[h] + rsum(dkbg[h] * kbg[h])
            dgc = dgc + jnp.where(rowc == CHUNK - 1, dgl[h] + jnp.sum(tkd[h], axis=0, keepdims=True), 0.0)
            dgc_all = dgc_all + jnp.where(lane == h, dgc - colsum[h], 0.0)
            dbeta_all = dbeta_all + jnp.where(lane == h, rsum(dkb[h] * k[h]) + rsum(dvb[h] * v[h]), 0.0)
        for h in hs:
            dstate[h] = ds_new[h]
            dqkv_ref[:, h * DN_DK:(h + 1) * DN_DK] = dq[h]
            dqkv_ref[:, d + h * DN_DK:d + (h + 1) * DN_DK] = dk[h]
            dqkv_ref[:, 2 * d + h * DN_DK:2 * d + (h + 1) * DN_DK] = dvb[h] * bh[h]
        dg = _dot(jnp.where(triu, 1.0, 0.0), dgc_all, NN, HIGHEST)
        valid = lane < heads
        dbl = jnp.where(valid, dbeta_all * beta * (1.0 - beta), 0.0)
        dal = jnp.where(valid, -dg * ea * sig_a, 0.0)
        dba_ref[:, :LANES] = dbl.astype(dba_ref.dtype)
        dba_ref[:, LANES:] = dal.astype(dba_ref.dtype)
        dal_ref[...] += _fold8(jnp.where(valid, dg * g, 0.0))
        ddt_ref[...] += _fold8(dal)

    return pl.pallas_call(
        body, name="dn_bwd", grid=(n_chunks,),
        in_specs=[pl.BlockSpec((CHUNK, 3 * d), lambda i: (rev(i), 0)),
                  pl.BlockSpec((CHUNK, 2 * LANES), lambda i: (rev(i), 0)),
                  pl.BlockSpec((1, LANES), lambda i: (0, 0)),
                  pl.BlockSpec((1, LANES), lambda i: (0, 0)),
                  pl.BlockSpec((CHUNK, d), lambda i: (rev(i), 0)),
                  pl.BlockSpec((1, heads, DN_DK, DN_DK), lambda i: (rev(i), 0, 0, 0)),
                  pl.BlockSpec((heads, CHUNK, CHUNK), lambda i: (0, rev(i), 0))],
        out_specs=[pl.BlockSpec((CHUNK, 3 * d), lambda i: (rev(i), 0)),
                   pl.BlockSpec((CHUNK, 2 * LANES), lambda i: (rev(i), 0)),
                   pl.BlockSpec((SUBLANES, LANES), lambda i: (0, 0)),
                   pl.BlockSpec((SUBLANES, LANES), lambda i: (0, 0))],
        out_shape=[jax.ShapeDtypeStruct((t, 3 * d), F32),
                   jax.ShapeDtypeStruct((t, 2 * LANES), ACT),
                   jax.ShapeDtypeStruct((SUBLANES, LANES), F32),
                   jax.ShapeDtypeStruct((SUBLANES, LANES), F32)],
        scratch_shapes=[pltpu.VMEM((heads, DN_DK, DN_DK), F32)],
        compiler_params=_cparams(("arbitrary",)),
    )(qkv, ba, alog, dtb, dout, states, ycors)


def _sgu_mask():
    row = lax.broadcasted_iota(jnp.int32, (SGU_BLOCK, SGU_BLOCK), 0)
    col = lax.broadcasted_iota(jnp.int32, (SGU_BLOCK, SGU_BLOCK), 1)
    sh = int(math.log2(CHUNK))
    return lax.shift_right_logical(row, sh) >= lax.shift_right_logical(col, sh)


def _gate_sgu_fwd(o, projm, onw, lng, lnb, ws, bst, d):
    t = o.shape[0]
    heads, groups = d // DN_DK, d // SGU_GROUP_DIM
    tb = SGU_BLOCK
    row_spec = pl.BlockSpec((1, d), lambda i: (0, 0))

    def body(o_ref, z_ref, u_ref, v_ref, onw_ref, lng_ref, lnb_ref, ws_ref, bst_ref, ya_ref, yb_ref):
        for h in range(heads):
            cols = slice(h * DN_DK, (h + 1) * DN_DK)
            oh, zh = o_ref[:, cols], z_ref[:, cols]
            r = lax.rsqrt(jnp.mean(oh * oh, axis=1, keepdims=True) + RMS_EPS)
            ya_ref[:, cols] = (oh * r * onw_ref[:, cols] * (zh * _sigmoid(zh))).astype(ya_ref.dtype)
        xhat, _ = _ln_hat(_gelu(v_ref[...]))
        vgn = xhat * lng_ref[...] + lnb_ref[...]
        mask = _sgu_mask()
        lane = lax.broadcasted_iota(jnp.int32, (SGU_BLOCK, LANES), 1)
        bst_v = bst_ref[...]
        for gi in range(groups):
            cols = slice(gi * SGU_GROUP_DIM, (gi + 1) * SGU_GROUP_DIM)
            wsg = jnp.where(mask, ws_ref[gi], 0.0)
            sp = _dot(wsg, vgn[:, cols]) + _col_of(bst_v, lane, gi)
            yb_ref[:, cols] = (_gelu(u_ref[:, cols]) * sp).astype(yb_ref.dtype)

    return pl.pallas_call(
        body, name="gate_sgu_fwd", grid=(t // tb,),
        in_specs=[pl.BlockSpec((tb, d), lambda i: (i, 0)),
                  pl.BlockSpec((tb, d), lambda i: (i, 3)),
                  pl.BlockSpec((tb, d), lambda i: (i, 4)),
                  pl.BlockSpec((tb, d), lambda i: (i, 5)),
                  row_spec, row_spec, row_spec,
                  pl.BlockSpec((groups, SGU_BLOCK, SGU_BLOCK), lambda i: (0, 0, 0)),
                  pl.BlockSpec((SGU_BLOCK, LANES), lambda i: (0, 0))],
        out_specs=[pl.BlockSpec((tb, d), lambda i: (i, 0)), pl.BlockSpec((tb, d), lambda i: (i, 0))],
        out_shape=[jax.ShapeDtypeStruct((t, d), ACT), jax.ShapeDtypeStruct((t, d), ACT)],
        compiler_params=_cparams(("parallel",)),
    )(o, projm, projm, projm, onw, lng, lnb, ws, bst)


def _gate_sgu_bwd(dya, dyb, o, projm, onw, lng, lnb, ws, bst, dprojm, d):
    t = o.shape[0]
    heads, groups = d // DN_DK, d // SGU_GROUP_DIM
    tb = SGU_BLOCK
    row_spec = pl.BlockSpec((1, d), lambda i: (0, 0))
    acc_row = pl.BlockSpec((SUBLANES, d), lambda i: (0, 0))

    def body(dya_ref, dyb_ref, o_ref, z_ref, u_ref, v_ref, onw_ref, lng_ref, lnb_ref, ws_ref, bst_ref, alias_ref,
             do_ref, dp_ref, donw_ref, dlng_ref, dlnb_ref, dws_ref, dbst_ref):
        @pl.when(pl.program_id(0) == 0)
        def _():
            for r_ in (donw_ref, dlng_ref, dlnb_ref, dws_ref, dbst_ref):
                r_[...] = jnp.zeros_like(r_)

        donw = jnp.zeros((SUBLANES, DN_DK), F32)
        for h in range(heads):
            cols = slice(h * DN_DK, (h + 1) * DN_DK)
            oh, zh, dyah, wh = o_ref[:, cols], z_ref[:, cols], dya_ref[:, cols], onw_ref[:, cols]
            r = lax.rsqrt(jnp.mean(oh * oh, axis=1, keepdims=True) + RMS_EPS)
            on = oh * r
            sz = _sigmoid(zh)
            silu_z = zh * sz
            don = dyah * wh * silu_z
            dp_ref[:, cols] = (dyah * on * wh * (sz * (1.0 + zh * (1.0 - sz)))).astype(dp_ref.dtype)
            donw = donw + _fold8(dyah * on * silu_z)
            do_ref[:, cols] = r * (don - on * jnp.mean(don * on, axis=1, keepdims=True))
        donw_ref[...] += donw

        vgp, up = v_ref[...], u_ref[...]
        xhat, rstd = _ln_hat(_gelu(vgp))
        lng_v = lng_ref[...]
        vgn = xhat * lng_v + lnb_ref[...]
        ua = _gelu(up)
        mask = _sgu_mask()
        lane = lax.broadcasted_iota(jnp.int32, (SGU_BLOCK, LANES), 1)
        bst_v = bst_ref[...]
        dbst = jnp.zeros((SGU_BLOCK, LANES), F32)
        dvgn_parts, dua_parts = [], []
        for gi in range(groups):
            cols = slice(gi * SGU_GROUP_DIM, (gi + 1) * SGU_GROUP_DIM)
            wsg = jnp.where(mask, ws_ref[gi], 0.0)
            vg_g, dyb_g = vgn[:, cols], dyb_ref[:, cols]
            sp = _dot(wsg, vg_g) + _col_of(bst_v, lane, gi)
            dsp = dyb_g * ua[:, cols]
            dua_parts.append(dyb_g * sp)
            dws_ref[gi] += jnp.where(mask, _dot(dsp, vg_g, NT), 0.0)
            dbst = dbst + jnp.where(lane == gi, jnp.sum(dsp, axis=1, keepdims=True), 0.0)
            dvgn_parts.append(_dot(wsg, dsp, TN))
        dbst_ref[...] += dbst
        dvgn = jnp.concatenate(dvgn_parts, axis=1)
        dua = jnp.concatenate(dua_parts, axis=1)
        dlng_ref[...] += _fold8(dvgn * xhat)
        dlnb_ref[...] += _fold8(dvgn)
        dvga = _ln_bwd(dvgn * lng_v, xhat, rstd)
        dp_ref[:, d:2 * d] = (dua * _gelu_grad(up)).astype(dp_ref.dtype)
        dp_ref[:, 2 * d:] = (dvga * _gelu_grad(vgp)).astype(dp_ref.dtype)

    return pl.pallas_call(
        body, name="gate_sgu_bwd", grid=(t // tb,),
        in_specs=[pl.BlockSpec((tb, d), lambda i: (i, 0)),
                  pl.BlockSpec((tb, d), lambda i: (i, 0)),
                  pl.BlockSpec((tb, d), lambda i: (i, 0)),
                  pl.BlockSpec((tb, d), lambda i: (i, 3)),
                  pl.BlockSpec((tb, d), lambda i: (i, 4)),
                  pl.BlockSpec((tb, d), lambda i: (i, 5)),
                  row_spec, row_spec, row_spec,
                  pl.BlockSpec((groups, SGU_BLOCK, SGU_BLOCK), lambda i: (0, 0, 0)),
                  pl.BlockSpec((SGU_BLOCK, LANES), lambda i: (0, 0)),
                  ANY],
        out_specs=[pl.BlockSpec((tb, d), lambda i: (i, 0)),
                   pl.BlockSpec((tb, 3 * d), lambda i: (i, 1)),
                   pl.BlockSpec((SUBLANES, DN_DK), lambda i: (0, 0)),
                   acc_row, acc_row,
                   pl.BlockSpec((groups, SGU_BLOCK, SGU_BLOCK), lambda i: (0, 0, 0)),
                   pl.BlockSpec((SGU_BLOCK, LANES), lambda i: (0, 0))],
        out_shape=[jax.ShapeDtypeStruct((t, d), F32),
                   jax.ShapeDtypeStruct(dprojm.shape, dprojm.dtype),
                   jax.ShapeDtypeStruct((SUBLANES, DN_DK), F32),
                   jax.ShapeDtypeStruct((SUBLANES, d), F32),
                   jax.ShapeDtypeStruct((SUBLANES, d), F32),
                   jax.ShapeDtypeStruct((groups, SGU_BLOCK, SGU_BLOCK), F32),
                   jax.ShapeDtypeStruct((SGU_BLOCK, LANES), F32)],
        input_output_aliases={11: 1},
        compiler_params=_cparams(("arbitrary",)),
    )(dya, dyb, o, projm, projm, projm, onw, lng, lnb, ws, bst, dprojm)


def _mix_fwd(ya, yb, projm, x, wpa, wpb, wo, g1, b1, d, tb):
    t = x.shape[0]
    blk = pl.BlockSpec((tb, d), lambda i: (i, 0))
    wspec = pl.BlockSpec((d, d), lambda i: (0, 0))
    row_spec = pl.BlockSpec((1, d), lambda i: (0, 0))

    def body(ya_ref, yb_ref, ga_ref, gb_ref, x_ref, wpa_ref, wpb_ref, wo_ref, g_ref, b_ref,
             pa_ref, pb_ref, m_ref, h_ref, x1_ref, x1b_ref):
        pa = _dot(ya_ref[...], wpa_ref[...])
        pb = _dot(yb_ref[...], wpb_ref[...])
        m = _sigmoid(ga_ref[...]) * pa + _sigmoid(gb_ref[...]) * pb
        hres = ALPHA * x_ref[...] + _dot(m, wo_ref[...])
        xhat, _ = _ln_hat(hres)
        x1 = xhat * g_ref[...] + b_ref[...]
        pa_ref[...] = pa
        pb_ref[...] = pb
        m_ref[...] = m.astype(m_ref.dtype)
        h_ref[...] = hres
        x1_ref[...] = x1
        x1b_ref[...] = x1.astype(x1b_ref.dtype)

    f32_out = jax.ShapeDtypeStruct((t, d), F32)
    bf_out = jax.ShapeDtypeStruct((t, d), ACT)
    return pl.pallas_call(
        body, name="mix_fwd", grid=(t // tb,),
        in_specs=[blk, blk, pl.BlockSpec((tb, d), lambda i: (i, 6)), pl.BlockSpec((tb, d), lambda i: (i, 7)),
                  blk, wspec, wspec, wspec, row_spec, row_spec],
        out_specs=[blk] * 6,
        out_shape=[f32_out, f32_out, bf_out, f32_out, f32_out, bf_out],
        compiler_params=_cparams(("parallel",)),
    )(ya, yb, projm, projm, x, wpa, wpb, wo, g1, b1)


def _mix_bwd(dmix, pa, pb, projm, wpa, wpb, wo, d, tb):
    t = dmix.shape[0]
    blk = pl.BlockSpec((tb, d), lambda i: (i, 0))
    wspec = pl.BlockSpec((d, d), lambda i: (0, 0))

    def body(dmix_ref, pa_ref, pb_ref, ga_ref, gb_ref, wpa_ref, wpb_ref, wo_ref,
             dpa_ref, dpb_ref, dya_ref, dyb_ref, dg_ref):
        dm = _dot(dmix_ref[...], wo_ref[...], NT)
        sa, sb = _sigmoid(ga_ref[...]), _sigmoid(gb_ref[...])
        dpa, dpb = dm * sa, dm * sb
        dpa_ref[...] = dpa.astype(dpa_ref.dtype)
        dpb_ref[...] = dpb.astype(dpb_ref.dtype)
        dg_ref[:, :d] = (dm * pa_ref[...] * sa * (1.0 - sa)).astype(dg_ref.dtype)
        dg_ref[:, d:] = (dm * pb_ref[...] * sb * (1.0 - sb)).astype(dg_ref.dtype)
        dya_ref[...] = _dot(dpa, wpa_ref[...], NT)
        dyb_ref[...] = _dot(dpb, wpb_ref[...], NT)

    return pl.pallas_call(
        body, name="mix_bwd", grid=(t // tb,),
        in_specs=[blk, blk, blk, pl.BlockSpec((tb, d), lambda i: (i, 6)), pl.BlockSpec((tb, d), lambda i: (i, 7)),
                  wspec, wspec, wspec],
        out_specs=[blk, blk, blk, blk, pl.BlockSpec((tb, 2 * d), lambda i: (i, 3))],
        out_shape=[jax.ShapeDtypeStruct((t, d), ACT), jax.ShapeDtypeStruct((t, d), ACT),
                   jax.ShapeDtypeStruct((t, d), F32), jax.ShapeDtypeStruct((t, d), F32),
                   jax.ShapeDtypeStruct((t, 8 * d), ACT)],
        compiler_params=_cparams(("parallel",)),
    )(dmix, pa, pb, projm, projm, wpa, wpb, wo)


def _swiglu_fwd(gu, f, tb):
    t = gu.shape[0]

    def body(g_ref, u_ref, a_ref):
        gp = g_ref[...]
        a_ref[...] = (gp * _sigmoid(gp) * u_ref[...]).astype(a_ref.dtype)

    return pl.pallas_call(
        body, name="swiglu_fwd", grid=(t // tb,),
        in_specs=[pl.BlockSpec((tb, f), lambda i: (i, 0)), pl.BlockSpec((tb, f), lambda i: (i, 1))],
        out_specs=pl.BlockSpec((tb, f), lambda i: (i, 0)),
        out_shape=jax.ShapeDtypeStruct((t, f), ACT),
        compiler_params=_cparams(("parallel",)),
    )(gu, gu)


def _swiglu_bwd(da, gu, f, tb):
    t = gu.shape[0]

    def body(da_ref, g_ref, u_ref, dgu_ref):
        gp, da_v = g_ref[...], da_ref[...]
        sg = _sigmoid(gp)
        dgu_ref[:, :f] = (da_v * u_ref[...] * sg * (1.0 + gp * (1.0 - sg))).astype(dgu_ref.dtype)
        dgu_ref[:, f:] = (da_v * gp * sg).astype(dgu_ref.dtype)

    return pl.pallas_call(
        body, name="swiglu_bwd", grid=(t // tb,),
        in_specs=[pl.BlockSpec((tb, f), lambda i: (i, 0)), pl.BlockSpec((tb, f), lambda i: (i, 0)),
                  pl.BlockSpec((tb, f), lambda i: (i, 1))],
        out_specs=pl.BlockSpec((tb, 2 * f), lambda i: (i, 0)),
        out_shape=jax.ShapeDtypeStruct((t, 2 * f), ACT),
        compiler_params=_cparams(("parallel",)),
    )(da, gu, gu)


def _res_ln_fwd(x1, f, g, b, tb):
    t, d = x1.shape
    blk = pl.BlockSpec((tb, d), lambda i: (i, 0))
    row_spec = pl.BlockSpec((1, d), lambda i: (0, 0))

    def body(x_ref, f_ref, g_ref, b_ref, h_ref, y_ref, yb_ref):
        hres = ALPHA * x_ref[...] + f_ref[...]
        xhat, _ = _ln_hat(hres)
        y = xhat * g_ref[...] + b_ref[...]
        h_ref[...] = hres
        y_ref[...] = y
        yb_ref[...] = y.astype(yb_ref.dtype)

    return pl.pallas_call(
        body, name="res_ln_fwd", grid=(t // tb,),
        in_specs=[blk, blk, row_spec, row_spec], out_specs=[blk, blk, blk],
        out_shape=[jax.ShapeDtypeStruct((t, d), F32), jax.ShapeDtypeStruct((t, d), F32),
                   jax.ShapeDtypeStruct((t, d), ACT)],
        compiler_params=_cparams(("parallel",)),
    )(x1, f, g, b)


def _ln_bwd_call(dy, hres, g, tb):
    t, d = dy.shape
    blk = pl.BlockSpec((tb, d), lambda i: (i, 0))
    acc = pl.BlockSpec((SUBLANES, d), lambda i: (0, 0))

    def body(dy_ref, h_ref, g_ref, dh_ref, dhb_ref, dg_ref, db_ref):
        @pl.when(pl.program_id(0) == 0)
        def _():
            dg_ref[...] = jnp.zeros_like(dg_ref)
            db_ref[...] = jnp.zeros_like(db_ref)

        dy_v = dy_ref[...]
        xhat, r = _ln_hat(h_ref[...])
        dh = _ln_bwd(dy_v * g_ref[...], xhat, r)
        dh_ref[...] = dh
        dhb_ref[...] = dh.astype(dhb_ref.dtype)
        dg_ref[...] += _fold8(dy_v * xhat)
        db_ref[...] += _fold8(dy_v)

    return pl.pallas_call(
        body, name="ln_bwd", grid=(t // tb,),
        in_specs=[blk, blk, pl.BlockSpec((1, d), lambda i: (0, 0))],
        out_specs=[blk, blk, acc, acc],
        out_shape=[jax.ShapeDtypeStruct((t, d), F32), jax.ShapeDtypeStruct((t, d), ACT),
                   jax.ShapeDtypeStruct((SUBLANES, d), F32), jax.ShapeDtypeStruct((SUBLANES, d), F32)],
        compiler_params=_cparams(("arbitrary",)),
    )(dy, hres, g)


def _loss_head(y, target, tb):
    t, d = y.shape
    blk = pl.BlockSpec((tb, d), lambda i: (i, 0))

    def body(y_ref, t_ref, dy_ref, l_ref):
        @pl.when(pl.program_id(0) == 0)
        def _():
            l_ref[...] = jnp.zeros_like(l_ref)

        err = y_ref[...] - t_ref[...]
        dy_ref[...] = err * (1.0 / d)
        sq = _fold8(err * err)
        part = sq[:, :LANES]
        for c in range(1, d // LANES):
            part = part + sq[:, c * LANES:(c + 1) * LANES]
        l_ref[...] += part

    return pl.pallas_call(
        body, name="loss_head", grid=(t // tb,),
        in_specs=[blk, blk],
        out_specs=[blk, pl.BlockSpec((SUBLANES, LANES), lambda i: (0, 0))],
        out_shape=[jax.ShapeDtypeStruct((t, d), F32), jax.ShapeDtypeStruct((SUBLANES, LANES), F32)],
        compiler_params=_cparams(("arbitrary",)),
    )(y, target)


def _adamw(w, g, m, v):
    shape = w.shape
    cols = shape[-1]
    w2, g2, m2, v2 = (a.reshape(-1, cols) for a in (w, g, m, v))
    rows = w2.shape[0]
    tr = _tile(rows, 256, SUBLANES)
    blk = pl.BlockSpec((tr, cols), lambda i: (i, 0))

    def body(w_ref, g_ref, m_ref, v_ref, d_ref, nm_ref, nv_ref):
        g_v = g_ref[...]
        nm = ADAM_B1 * m_ref[...] + (1.0 - ADAM_B1) * g_v
        nv = ADAM_B2 * v_ref[...] + (1.0 - ADAM_B2) * (g_v * g_v)
        m_hat = nm / (1.0 - ADAM_B1 ** ADAM_STEP)
        v_hat = nv / (1.0 - ADAM_B2 ** ADAM_STEP)
        d_ref[...] = -ADAM_LR * (m_hat / (jnp.sqrt(v_hat) + ADAM_EPS) + ADAM_WD * w_ref[...])
        nm_ref[...] = nm
        nv_ref[...] = nv

    out = jax.ShapeDtypeStruct((rows, cols), F32)
    res = pl.pallas_call(
        body, name="adamw", grid=(rows // tr,),
        in_specs=[blk] * 4, out_specs=[blk] * 3, out_shape=[out] * 3,
        compiler_params=_cparams(("parallel",)),
    )(w2, g2, m2, v2)
    return tuple(r.reshape(shape) for r in res)


def _place():
    x, y, c = lax.axis_index("x"), lax.axis_index("y"), lax.axis_index("c")
    return x, y, c, [(1 - x, y), (x, 1 - y), (1 - x, 1 - y)]


def _remote(src, dst, send_sems, recv_sems, k, to):
    return pltpu.make_async_remote_copy(src_ref=src, dst_ref=dst, send_sem=send_sems.at[k],
                                        recv_sem=recv_sems.at[k], device_id=to, device_id_type=MESH)


def _all_gather_weights(shards):
    n = len(shards)

    def body(*refs):
        x_refs, out_refs = refs[:n], refs[n:2 * n]
        send_sems, recv_sems, local_sems = refs[2 * n:]
        x, y, c, chips = _place()
        sibling = (x, y, 1 - c)
        mine = 2 * x + y
        own = [pltpu.make_async_copy(x_refs[t], out_refs[t].at[mine], local_sems.at[t]) for t in range(n)]
        for cp in own:
            cp.start()
        first = [_remote(x_refs[t].at[c], out_refs[t].at[mine, c], send_sems, recv_sems, 6 * t + j, (cx, cy, c))
                 for j, (cx, cy) in enumerate(chips) for t in range(n)]
        for cp in first:
            cp.start()
        passed = []
        for j, (cx, cy) in enumerate(chips):
            for t in range(n):
                theirs = out_refs[t].at[2 * cx + cy, c]
                _remote(theirs, theirs, send_sems, recv_sems, 6 * t + j, (cx, cy, c)).wait_recv()
                fwd = _remote(theirs, theirs, send_sems, recv_sems, 6 * t + 3 + j, sibling)
                fwd.start()
                passed.append(fwd)
        for j, (cx, cy) in enumerate(chips):
            for t in range(n):
                other = out_refs[t].at[2 * cx + cy, 1 - c]
                _remote(other, other, send_sems, recv_sems, 6 * t + 3 + j, sibling).wait_recv()
        for cp in first + passed:
            cp.wait_send()
        for cp in own:
            cp.wait()

    return pl.pallas_call(
        body, name="all_gather_weights",
        in_specs=[ANY] * n, out_specs=[ANY] * n,
        out_shape=[jax.ShapeDtypeStruct((N_CHIPS,) + s.shape, s.dtype) for s in shards],
        scratch_shapes=[pltpu.SemaphoreType.DMA((6 * n,)), pltpu.SemaphoreType.DMA((6 * n,)),
                        pltpu.SemaphoreType.DMA((n,))],
    )(*shards)


def _sibling_exchange(grads, small):
    n = len(grads)

    def body(*refs):
        g_refs, small_ref = refs[:n], refs[n]
        land_refs, sland_ref = refs[n + 1:2 * n + 1], refs[2 * n + 1]
        send_sems, recv_sems = refs[2 * n + 2:]
        x, y, c, _ = _place()
        sibling = (x, y, 1 - c)
        cps = [_remote(g_refs[t].at[1 - c], land_refs[t], send_sems, recv_sems, t, sibling) for t in range(n)]
        cps.append(_remote(small_ref, sland_ref, send_sems, recv_sems, n, sibling))
        for cp in cps:
            cp.start()
        for cp in cps:
            cp.wait()

    return pl.pallas_call(
        body, name="grad_sibling_exchange",
        in_specs=[ANY] * (n + 1), out_specs=[ANY] * (n + 1),
        out_shape=[jax.ShapeDtypeStruct(g.shape[1:], g.dtype) for g in grads]
        + [jax.ShapeDtypeStruct(small.shape, small.dtype)],
        scratch_shapes=[pltpu.SemaphoreType.DMA((n + 1,)), pltpu.SemaphoreType.DMA((n + 1,))],
    )(*grads, small)


def _chip_exchange(travel, small):
    n = len(travel)

    def body(*refs):
        t_refs, small_ref = refs[:n], refs[n]
        land_refs, sland_ref = refs[n + 1:2 * n + 1], refs[2 * n + 1]
        send_sems, recv_sems = refs[2 * n + 2:]
        x, y, c, chips = _place()
        mine = 2 * x + y
        cps = []
        for j, (cx, cy) in enumerate(chips):
            to = (cx, cy, c)
            for t in range(n):
                cps.append(_remote(t_refs[t].at[2 * cx + cy], land_refs[t].at[mine], send_sems, recv_sems,
                                   3 * t + j, to))
            cps.append(_remote(small_ref, sland_ref.at[mine], send_sems, recv_sems, 3 * n + j, to))
        for cp in cps:
            cp.start()
        for cp in cps:
            cp.wait()

    return pl.pallas_call(
        body, name="grad_chip_exchange",
        in_specs=[ANY] * (n + 1), out_specs=[ANY] * (n + 1),
        out_shape=[jax.ShapeDtypeStruct(g.shape, g.dtype) for g in travel]
        + [jax.ShapeDtypeStruct((N_CHIPS,) + small.shape, small.dtype)],
        scratch_shapes=[pltpu.SemaphoreType.DMA((3 * n + 3,)), pltpu.SemaphoreType.DMA((3 * n + 3,))],
    )(*travel, small)


def _sibling_merge(reduced):
    n = len(reduced)

    def body(*refs):
        r_refs, out_refs = refs[:n], refs[n:2 * n]
        send_sems, recv_sems, local_sems = refs[2 * n:]
        x, y, c, _ = _place()
        own = [pltpu.make_async_copy(r_refs[t], out_refs[t].at[c], local_sems.at[t]) for t in range(n)]
        cps = [_remote(r_refs[t], out_refs[t].at[c], send_sems, recv_sems, t, (x, y, 1 - c)) for t in range(n)]
        for cp in own + cps:
            cp.start()
        for cp in cps + own:
            cp.wait()

    return pl.pallas_call(
        body, name="grad_sibling_merge",
        in_specs=[ANY] * n, out_specs=[ANY] * n,
        out_shape=[jax.ShapeDtypeStruct((DEPTH,) + r.shape, r.dtype) for r in reduced],
        scratch_shapes=[pltpu.SemaphoreType.DMA((n,)), pltpu.SemaphoreType.DMA((n,)), pltpu.SemaphoreType.DMA((n,))],
    )(*reduced)


def _pair_sum(place, grad, land):
    _, n, r, c = grad.shape
    tr = _tile(r, 256, SUBLANES)

    def body(place_ref, a_ref, b_ref, travel_ref, own_ref):
        total = a_ref[0, 0] + b_ref[0]
        travel_ref[0] = total.astype(travel_ref.dtype)

        @pl.when(pl.program_id(1) == place_ref[1])
        def _():
            own_ref[...] = total

    return pl.pallas_call(
        body, name="grad_pair_sum",
        grid_spec=pltpu.PrefetchScalarGridSpec(
            num_scalar_prefetch=1, grid=(r // tr, n),
            in_specs=[pl.BlockSpec((1, 1, tr, c), lambda i, s, p: (p[0], s, i, 0)),
                      pl.BlockSpec((1, tr, c), lambda i, s, p: (s, i, 0))],
            out_specs=[pl.BlockSpec((1, tr, c), lambda i, s, p: (s, i, 0)),
                       pl.BlockSpec((tr, c), lambda i, s, p: (i, 0))]),
        out_shape=[jax.ShapeDtypeStruct((n, r, c), BF16), jax.ShapeDtypeStruct((r, c), F32)],
        compiler_params=_cparams(("parallel", "arbitrary")),
    )(place, grad, land)


def _chip_sum(place, own, land, name):
    n, r, c = land.shape
    tr = _tile(r, 256, SUBLANES)

    def body(place_ref, own_ref, land_ref, o_ref):
        mine = place_ref[1]
        acc = jnp.zeros(o_ref.shape, F32)
        for s in range(n):
            acc = acc + jnp.where(mine == s, own_ref[...], land_ref[s].astype(F32))
        o_ref[...] = acc

    return pl.pallas_call(
        body, name=name,
        grid_spec=pltpu.PrefetchScalarGridSpec(
            num_scalar_prefetch=1, grid=(r // tr,),
            in_specs=[pl.BlockSpec((tr, c), lambda i, p: (i, 0)),
                      pl.BlockSpec((n, tr, c), lambda i, p: (0, i, 0))],
            out_specs=pl.BlockSpec((tr, c), lambda i, p: (i, 0))),
        out_shape=jax.ShapeDtypeStruct((r, c), F32),
        compiler_params=_cparams(("parallel",)),
    )(place, own, land)


def _add2(a, b):
    rows = a.shape[0]
    tr = _tile(rows, 256, SUBLANES)
    blk = pl.BlockSpec((tr, PACK_W), lambda i: (i, 0))

    def body(a_ref, b_ref, o_ref):
        o_ref[...] = a_ref[...] + b_ref[...]

    return pl.pallas_call(
        body, name="grad_small_pair_sum", grid=(rows // tr,), in_specs=[blk, blk], out_specs=blk,
        out_shape=jax.ShapeDtypeStruct(a.shape, F32), compiler_params=_cparams(("parallel",)),
    )(a, b)


def _reduce_gradients(place, grads, small):
    *lands, sland = _sibling_exchange(grads, small)
    pairs = [_pair_sum(place, g, land) for g, land in zip(grads, lands)]
    small_chip = _add2(small, sland)
    *lands2, sland2 = _chip_exchange([p[0] for p in pairs], small_chip)
    reduced = [_chip_sum(place, p[1], land2, "grad_chip_sum") for p, land2 in zip(pairs, lands2)]
    small_total = _chip_sum(place, small_chip, sland2, "grad_small_chip_sum")
    return _sibling_merge(reduced), small_total


_BIG = (("w_in", 2), ("w_pa", 1), ("w_pb", 1), ("w_o", 1), ("w_ffn_gate", 2), ("w_ffn_up", 2),
        ("w_ffn_down", 1))
_SMALL = ("conv_w", "a_log", "dt_bias", "o_norm_w", "sgu_ln_g", "sgu_ln_b", "w_s", "b_s",
          "ln1_g", "ln1_b", "ln2_g", "ln2_b")


def _pad_rows(flat, mult):
    rows = -(-flat.shape[-1] // (PACK_W * mult)) * mult
    pad = rows * PACK_W - flat.shape[-1]
    flat = jnp.pad(flat, [(0, 0)] * (flat.ndim - 1) + [(0, pad)])
    return flat.reshape(flat.shape[:-1] + (rows, PACK_W))


def _unshard(gathered, layer, axis):
    return jnp.concatenate([gathered[s, layer] for s in range(N_CHIPS)], axis=axis - 1)


def _to_shards(full, axis):
    l, r, c = full.shape
    if axis == 1:
        return full.reshape(l, N_CHIPS, r // N_CHIPS, c)
    return jnp.transpose(full.reshape(l, r, N_CHIPS, c // N_CHIPS), (0, 2, 1, 3))


def _row(v, width=None):
    v = v.reshape(1, -1).astype(F32)
    if width is not None and v.shape[1] < width:
        v = jnp.pad(v, ((0, 0), (0, width - v.shape[1])))
    return v


def _layer_consts(p, l, d):
    heads = d // DN_DK
    return dict(
        alog=_row(p["a_log"][l], LANES), dtb=_row(p["dt_bias"][l], LANES),
        onw=_row(jnp.tile(p["o_norm_w"][l], heads)),
        lng=_row(p["sgu_ln_g"][l]), lnb=_row(p["sgu_ln_b"][l]),
        ws=p["w_s"][l].astype(F32),
        bst=jnp.pad(p["b_s"][l].T, ((0, 0), (0, LANES - p["b_s"].shape[1]))),
        g1=_row(p["ln1_g"][l]), b1=_row(p["ln1_b"][l]), g2=_row(p["ln2_g"][l]), b2=_row(p["ln2_b"][l]))


def _layer_fwd(x, xb, wl, cl, d, f, tb):
    projm = _matmul(xb, wl["wm"], NN, "proj_main")
    ba = _matmul(xb, wl["wba"], NN, "proj_gates")
    qkv = _conv_fwd(projm, wl["conv"], d, tb)
    o, states, ycors = _dn_fwd(qkv, ba, cl["alog"], cl["dtb"], d)
    ya, yb = _gate_sgu_fwd(o, projm, cl["onw"], cl["lng"], cl["lnb"], cl["ws"], cl["bst"], d)
    pa, pb, m, h1, x1, x1b = _mix_fwd(ya, yb, projm, x, wl["wpa"], wl["wpb"], wl["wo"], cl["g1"], cl["b1"], d, tb)
    gu = _matmul(x1b, wl["wgu"], NN, "ffn_in")
    act = _swiglu_fwd(gu, f, tb)
    ffn = _matmul(act, wl["wd"], NN, "ffn_out", tm=512, tk=1536)
    h2, x2, x2b = _res_ln_fwd(x1, ffn, cl["g2"], cl["b2"], tb)
    saved = dict(xb=xb, projm=projm, ba=ba, qkv=qkv, o=o, states=states, ycors=ycors, ya=ya, yb=yb,
                 pa=pa, pb=pb, m=m, h1=h1, x1b=x1b, gu=gu, act=act, h2=h2)
    return x2, x2b, saved


def _layer_bwd(dx2, sv, wl, cl, d, f, tb):
    g = {}
    dh2, dh2b, dg2, db2 = _ln_bwd_call(dx2, sv["h2"], cl["g2"], tb)
    g["ln2_g"], g["ln2_b"] = dg2.sum(0), db2.sum(0)
    g["wd"] = _matmul(sv["act"], dh2b, TN, "ffn_out_dw")
    da = _matmul(dh2b, wl["wd"], NT, "ffn_out_dx")
    dgu = _swiglu_bwd(da, sv["gu"], f, tb)
    g["wgu"] = _matmul(sv["x1b"], dgu, TN, "ffn_in_dw")
    dx1 = _matmul(dgu, wl["wgu"], NT, "ffn_in_dx", add=dh2, coef=ALPHA, tk=_tile(2 * f, 1536))
    dh1, dh1b, dg1, db1 = _ln_bwd_call(dx1, sv["h1"], cl["g1"], tb)
    g["ln1_g"], g["ln1_b"] = dg1.sum(0), db1.sum(0)
    g["wo"] = _matmul(sv["m"], dh1b, TN, "wo_dw")
    dpa, dpb, dya, dyb, dprojm = _mix_bwd(dh1b, sv["pa"], sv["pb"], sv["projm"], wl["wpa"], wl["wpb"], wl["wo"], d, tb)
    g["wpa"] = _matmul(sv["ya"], dpa, TN, "wpa_dw")
    g["wpb"] = _matmul(sv["yb"], dpb, TN, "wpb_dw")
    do, dprojm, donw, dlng, dlnb, dws, dbst = _gate_sgu_bwd(
        dya, dyb, sv["o"], sv["projm"], cl["onw"], cl["lng"], cl["lnb"], cl["ws"], cl["bst"], dprojm, d)
    heads, groups = d // DN_DK, d // SGU_GROUP_DIM
    g["o_norm_w"], g["sgu_ln_g"], g["sgu_ln_b"] = donw.sum(0), dlng.sum(0), dlnb.sum(0)
    g["w_s"], g["b_s"] = dws, dbst[:, :groups].T
    dqkv, dba, dal, ddt = _dn_bwd(sv["qkv"], sv["ba"], cl["alog"], cl["dtb"], do, sv["states"], sv["ycors"], d)
    g["a_log"], g["dt_bias"] = dal.sum(0)[:heads], ddt.sum(0)[:heads]
    dy, dcw = _conv_bwd_dy(sv["projm"], wl["conv"], dqkv, d, tb)
    g["conv_w"] = dcw.sum(1)
    dprojm = _conv_bwd_dx(dy, wl["conv"], dprojm, d, tb)
    g["wm"] = _matmul(sv["xb"], dprojm, TN, "proj_main_dw")
    g["wba"] = _matmul(sv["xb"], dba, TN, "proj_gates_dw")
    dx = _matmul(dba, wl["wba"], NT, "proj_gates_dx", add=dh1, coef=ALPHA)
    dx = _matmul(dprojm, wl["wm"], NT, "proj_main_dx", add=dx)
    return dx, g


def _local_step(x, target, full, small_w):
    t, d = x.shape
    heads = d // DN_DK
    f = full["w_ffn_gate"][0].shape[-1]
    tb = _tile(t, 256, SUBLANES)
    q4 = 4 * d
    w_in = full["w_in"]
    layers, consts = [], []
    for l in range(DEPTH):
        wba = jnp.zeros((d, 2 * LANES), w_in[l].dtype)
        wba = wba.at[:, :heads].set(w_in[l][:, q4:q4 + heads])
        wba = wba.at[:, LANES:LANES + heads].set(w_in[l][:, q4 + heads:q4 + 2 * heads])
        layers.append(dict(
            wm=jnp.concatenate([w_in[l][:, :q4], w_in[l][:, q4 + 2 * heads:]], axis=1), wba=wba,
            conv=full["conv_w"][l], wpa=full["w_pa"][l], wpb=full["w_pb"][l], wo=full["w_o"][l],
            wgu=jnp.concatenate([full["w_ffn_gate"][l], full["w_ffn_up"][l]], axis=1), wd=full["w_ffn_down"][l]))
        consts.append(_layer_consts(small_w, l, d))

    h, hb, saved = x, x.astype(ACT), []
    for l in range(DEPTH):
        h, hb, sv = _layer_fwd(h, hb, layers[l], consts[l], d, f, tb)
        saved.append(sv)
    dy, loss_parts = _loss_head(h, target, tb)
    grads = []
    for l in reversed(range(DEPTH)):
        dy, g = _layer_bwd(dy, saved[l], layers[l], consts[l], d, f, tb)
        grads.append(g)
    grads = grads[::-1]

    stack = lambda k: jnp.stack([g[k] for g in grads])
    gm, gba, ggu = stack("wm"), stack("wba"), stack("wgu")
    out = {
        "w_in": jnp.concatenate([gm[:, :, :q4], gba[:, :, :heads], gba[:, :, LANES:LANES + heads], gm[:, :, q4:]], axis=2),
        "w_pa": stack("wpa"), "w_pb": stack("wpb"), "w_o": stack("wo"),
        "w_ffn_gate": ggu[:, :, :f], "w_ffn_up": ggu[:, :, f:], "w_ffn_down": stack("wd")}
    for k in _SMALL:
        out[k] = stack(k)
    return loss_parts, dy, out


def kernel(x, w_in, conv_w, a_log, dt_bias, o_norm_w, sgu_ln_g, sgu_ln_b, w_s, b_s, w_pa, w_pb, w_o, ln1_g, ln1_b, w_ffn_gate, w_ffn_up, w_ffn_down, ln2_g, ln2_b, loss_target, m_w_in, m_conv_w, m_a_log, m_dt_bias, m_o_norm_w, m_sgu_ln_g, m_sgu_ln_b, m_w_s, m_b_s, m_w_pa, m_w_pb, m_w_o, m_ln1_g, m_ln1_b, m_w_ffn_gate, m_w_ffn_up, m_w_ffn_down, m_ln2_g, m_ln2_b, v_w_in, v_conv_w, v_a_log, v_dt_bias, v_o_norm_w, v_sgu_ln_g, v_sgu_ln_b, v_w_s, v_b_s, v_w_pa, v_w_pb, v_w_o, v_ln1_g, v_ln1_b, v_w_ffn_gate, v_w_ffn_up, v_w_ffn_down, v_ln2_g, v_ln2_b):
    names = ("w_in", "conv_w", "a_log", "dt_bias", "o_norm_w", "sgu_ln_g", "sgu_ln_b", "w_s", "b_s", "w_pa",
             "w_pb", "w_o", "ln1_g", "ln1_b", "w_ffn_gate", "w_ffn_up", "w_ffn_down", "ln2_g", "ln2_b")
    w = dict(zip(names, (w_in, conv_w, a_log, dt_bias, o_norm_w, sgu_ln_g, sgu_ln_b, w_s, b_s, w_pa, w_pb, w_o,
                         ln1_g, ln1_b, w_ffn_gate, w_ffn_up, w_ffn_down, ln2_g, ln2_b)))
    mom = dict(zip(names, (m_w_in, m_conv_w, m_a_log, m_dt_bias, m_o_norm_w, m_sgu_ln_g, m_sgu_ln_b, m_w_s, m_b_s,
                           m_w_pa, m_w_pb, m_w_o, m_ln1_g, m_ln1_b, m_w_ffn_gate, m_w_ffn_up, m_w_ffn_down,
                           m_ln2_g, m_ln2_b)))
    var = dict(zip(names, (v_w_in, v_conv_w, v_a_log, v_dt_bias, v_o_norm_w, v_sgu_ln_g, v_sgu_ln_b, v_w_s, v_b_s,
                           v_w_pa, v_w_pb, v_w_o, v_ln1_g, v_ln1_b, v_w_ffn_gate, v_w_ffn_up, v_w_ffn_down,
                           v_ln2_g, v_ln2_b)))
    chip = 2 * lax.axis_index("x") + lax.axis_index("y")
    place = jnp.stack([lax.axis_index("c"), chip]).astype(jnp.int32)

    gathered = _all_gather_weights([w[k].astype(BF16) for k, _ in _BIG] + [conv_w])
    full = {k: [_unshard(gt, l, axis) for l in range(DEPTH)] for (k, axis), gt in zip(_BIG, gathered)}
    full["conv_w"] = [_unshard(gathered[-1], l, 2) for l in range(DEPTH)]

    small_w = {k: w[k] for k in _SMALL if k != "conv_w"}
    loss_parts, grad_x, g = _local_step(x[0], loss_target[0], full, small_w)

    small_sizes = [g[k].size for k in _SMALL]
    small = _pad_rows(jnp.concatenate([g[k].reshape(-1) for k in _SMALL]), SUBLANES)
    reduced, small_total = _reduce_gradients(place, [_to_shards(g[k], axis) for k, axis in _BIG], small)
    grads = {k: r for (k, _), r in zip(_BIG, reduced)}
    small_total, off = small_total.reshape(-1), 0
    for k, n in zip(_SMALL, small_sizes):
        grads[k] = small_total[off:off + n].reshape(g[k].shape)
        off += n
    grads["conv_w"] = lax.dynamic_index_in_dim(_to_shards(grads["conv_w"], 2), chip, 1, keepdims=False)

    delta, new_m, new_v = {}, {}, {}
    for k in [k for k, _ in _BIG] + ["conv_w"]:
        delta[k], new_m[k], new_v[k] = _adamw(w[k], grads[k], mom[k], var[k])
    rep = [k for k in _SMALL if k != "conv_w"]
    pack = lambda dct: _pad_rows(jnp.concatenate([dct[k].reshape(-1) for k in rep]), SUBLANES)
    packed = _adamw(pack(w), pack(grads), pack(mom), pack(var))
    off = 0
    for k in rep:
        n = w[k].size
        for dst, src in zip((delta, new_m, new_v), packed):
            dst[k] = src.reshape(-1)[off:off + n].reshape(w[k].shape)
        off += n

    loss = 0.5 * lax.psum(jnp.sum(loss_parts), ("x", "y", "c")) / x.shape[-1]
    return (loss, grad_x[None], *[grads[k] for k in names], *[delta[k] for k in names],
            *[new_m[k] for k in names], *[new_v[k] for k in names])
```

```python
import math

import jax
import jax.numpy as jnp
from jax import lax
from jax.experimental import pallas as pl
from jax.experimental.pallas import tpu as pltpu

F32 = jnp.float32
BF16 = jnp.bfloat16
MXU_DTYPE = jnp.bfloat16
ACT = jnp.bfloat16
HIGHEST = lax.Precision.HIGHEST

DEPTH = 2
CHUNK = 64
SGU_BLOCK = 128
CONV_K = 4
DN_DK = 128
SGU_GROUP_DIM = 128
LN_EPS = 1e-5
RMS_EPS = 1e-6
ALPHA = (2 * DEPTH) ** 0.25
ADAM_LR, ADAM_B1, ADAM_B2, ADAM_EPS, ADAM_WD, ADAM_STEP = 0.001, 0.9, 0.999, 1e-08, 0.01, 10

LANES = 128
SUBLANES = 8
VMEM_LIMIT = 52 * 2 ** 20
PACK_W = 1024
N_CHIPS = 4

NN = ((1,), (0,))
NT = ((1,), (1,))
TN = ((0,), (0,))
MESH = pl.DeviceIdType.MESH
ANY = pl.BlockSpec(memory_space=pl.ANY)


def _dot(a, b, dims=NN, prec=None):
    if prec is None:
        a = a.astype(MXU_DTYPE)
        b = b.astype(MXU_DTYPE)
    return lax.dot_general(a, b, (dims, ((), ())), preferred_element_type=F32, precision=prec)


def _cparams(sem=None):
    return pltpu.CompilerParams(dimension_semantics=sem, vmem_limit_bytes=VMEM_LIMIT)


def _tile(dim, pref, unit=LANES):
    t = (min(pref, dim) // unit) * unit
    while t >= unit:
        if dim % t == 0:
            return t
        t -= unit
    return dim


def _fold8(x):
    r, n = x.shape
    return x.reshape(r // SUBLANES, SUBLANES, n).sum(axis=0)


def _sigmoid(x):
    return 1.0 / (1.0 + jnp.exp(-x))


def _gelu(x):
    return 0.5 * x * (1.0 + lax.erf(x * (2.0 ** -0.5)))


def _gelu_grad(x):
    return 0.5 * (1.0 + lax.erf(x * (2.0 ** -0.5))) + x * jnp.exp(-0.5 * x * x) * (2.0 * math.pi) ** -0.5


def _ln_hat(h):
    mu = jnp.mean(h, axis=-1, keepdims=True)
    xc = h - mu
    var = jnp.mean(xc * xc, axis=-1, keepdims=True)
    r = lax.rsqrt(var + LN_EPS)
    return xc * r, r


def _ln_bwd(dxhat, xhat, r):
    return r * (dxhat - jnp.mean(dxhat, axis=-1, keepdims=True)
                - xhat * jnp.mean(dxhat * xhat, axis=-1, keepdims=True))


MM_TILE = 1536


def _matmul(a, b, dims, name, out_dtype=F32, add=None, coef=1.0, tm=MM_TILE, tn=MM_TILE, tk=MM_TILE):
    if dims == NN:
        (m, k), n = a.shape, b.shape[1]
    elif dims == NT:
        (m, k), n = a.shape, b.shape[0]
    else:
        (k, m), n = a.shape, b.shape[1]
    tm, tn, tk = _tile(m, tm), _tile(n, tn), _tile(k, tk)
    nk = k // tk
    a_spec = pl.BlockSpec((tk, tm), lambda j, i, q: (q, i)) if dims == TN else pl.BlockSpec((tm, tk), lambda j, i, q: (i, q))
    b_spec = pl.BlockSpec((tn, tk), lambda j, i, q: (j, q)) if dims == NT else pl.BlockSpec((tk, tn), lambda j, i, q: (q, j))
    o_spec = pl.BlockSpec((tm, tn), lambda j, i, q: (i, j))
    has_add = add is not None

    def body(*refs):
        a_ref, b_ref = refs[0], refs[1]
        add_ref = refs[2] if has_add else None
        o_ref, acc_ref = refs[2 + has_add], refs[3 + has_add]
        q = pl.program_id(2)
        part = _dot(a_ref[...], b_ref[...], dims)

        def finish(r):
            if has_add:
                r = r + coef * add_ref[...]
            o_ref[...] = r.astype(out_dtype)

        if nk == 1:
            finish(part)
        else:
            @pl.when(q == 0)
            def _():
                acc_ref[...] = part

            @pl.when(q > 0)
            def _():
                acc_ref[...] += part

            @pl.when(q == nk - 1)
            def _():
                finish(acc_ref[...])

    ins = [a, b] + ([add] if has_add else [])
    in_specs = [a_spec, b_spec] + ([o_spec] if has_add else [])
    return pl.pallas_call(
        body, name=name, grid=(n // tn, m // tm, nk),
        in_specs=in_specs, out_specs=o_spec,
        out_shape=jax.ShapeDtypeStruct((m, n), out_dtype),
        scratch_shapes=[pltpu.VMEM((tm, tn) if nk > 1 else (SUBLANES, LANES), F32)],
        compiler_params=_cparams(("parallel", "parallel", "arbitrary")),
    )(*ins)


def _conv_taps(cur_ref, halo_ref, first):
    x = cur_ref[...]
    tb = x.shape[0]
    halo = jnp.where(first, 0.0, halo_ref[...])
    xc = jnp.concatenate([halo, x], axis=0)
    return [x] + [pltpu.roll(xc, s, 0)[SUBLANES:SUBLANES + tb] for s in range(1, CONV_K)]


def _conv_fwd(projm, conv_w, d, tb):
    t = projm.shape[0]
    heads = d // DN_DK
    hb = tb // SUBLANES

    def body(cur_ref, halo_ref, w_ref, o_ref):
        i, j = pl.program_id(0), pl.program_id(1)
        taps = _conv_taps(cur_ref, halo_ref, i == 0)
        y = taps[0] * w_ref[CONV_K - 1:CONV_K, :]
        for s in range(1, CONV_K):
            y = y + taps[s] * w_ref[CONV_K - 1 - s:CONV_K - s, :]
        act = y * _sigmoid(y)
        scale = jnp.where(j == 0, DN_DK ** -0.5, 1.0)
        for h in range(heads):
            seg = act[:, h * DN_DK:(h + 1) * DN_DK]
            r = lax.rsqrt(jnp.sum(seg * seg, axis=1, keepdims=True) + RMS_EPS) * scale
            o_ref[:, h * DN_DK:(h + 1) * DN_DK] = seg * jnp.where(j < 2, r, 1.0)

    return pl.pallas_call(
        body, name="conv_fwd", grid=(t // tb, 3),
        in_specs=[pl.BlockSpec((tb, d), lambda i, j: (i, j)),
                  pl.BlockSpec((SUBLANES, d), lambda i, j: (jnp.maximum(i * hb - 1, 0), j)),
                  pl.BlockSpec((CONV_K, d), lambda i, j: (0, j))],
        out_specs=pl.BlockSpec((tb, d), lambda i, j: (i, j)),
        out_shape=jax.ShapeDtypeStruct((t, 3 * d), F32),
        compiler_params=_cparams(("parallel", "parallel")),
    )(projm, projm, conv_w)


def _conv_bwd_dy(projm, conv_w, dqkv, d, tb):
    t = projm.shape[0]
    heads = d // DN_DK
    hb = tb // SUBLANES

    def body(cur_ref, halo_ref, w_ref, dout_ref, dy_ref, dw_ref):
        j, i = pl.program_id(0), pl.program_id(1)
        taps = _conv_taps(cur_ref, halo_ref, i == 0)
        y = taps[0] * w_ref[CONV_K - 1:CONV_K, :]
        for s in range(1, CONV_K):
            y = y + taps[s] * w_ref[CONV_K - 1 - s:CONV_K - s, :]
        sg = _sigmoid(y)
        act = y * sg
        dact = sg * (1.0 + y * (1.0 - sg))
        scale = jnp.where(j == 0, DN_DK ** -0.5, 1.0)
        for h in range(heads):
            cols = slice(h * DN_DK, (h + 1) * DN_DK)
            seg = act[:, cols]
            r = lax.rsqrt(jnp.sum(seg * seg, axis=1, keepdims=True) + RMS_EPS)
            nrm = seg * r
            dout = dout_ref[:, cols]
            dn = dout * scale
            ds = jnp.where(j < 2, r * (dn - nrm * jnp.sum(dn * nrm, axis=1, keepdims=True)), dout)
            dy_ref[:, cols] = ds * dact[:, cols]
        dy = dy_ref[...]

        @pl.when(i == 0)
        def _():
            dw_ref[...] = jnp.zeros_like(dw_ref)

        for s in range(CONV_K):
            dw_ref[CONV_K - 1 - s] += _fold8(dy * taps[s])

    return pl.pallas_call(
        body, name="conv_bwd_dy", grid=(3, t // tb),
        in_specs=[pl.BlockSpec((tb, d), lambda j, i: (i, j)),
                  pl.BlockSpec((SUBLANES, d), lambda j, i: (jnp.maximum(i * hb - 1, 0), j)),
                  pl.BlockSpec((CONV_K, d), lambda j, i: (0, j)),
                  pl.BlockSpec((tb, d), lambda j, i: (i, j))],
        out_specs=[pl.BlockSpec((tb, d), lambda j, i: (i, j)),
                   pl.BlockSpec((CONV_K, SUBLANES, d), lambda j, i: (0, 0, j))],
        out_shape=[jax.ShapeDtypeStruct((t, 3 * d), F32),
                   jax.ShapeDtypeStruct((CONV_K, SUBLANES, 3 * d), F32)],
        compiler_params=_cparams(("parallel", "arbitrary")),
    )(projm, projm, conv_w, dqkv)


def _conv_bwd_dx(dy, conv_w, dprojm, d, tb):
    t = dy.shape[0]
    hb = tb // SUBLANES
    last = t // tb - 1

    def body(cur_ref, halo_ref, w_ref, alias_ref, o_ref):
        i = pl.program_id(0)
        cur = cur_ref[...]
        halo = jnp.where(i == last, 0.0, halo_ref[...])
        dc = jnp.concatenate([cur, halo], axis=0)
        acc = cur * w_ref[CONV_K - 1:CONV_K, :]
        for s in range(1, CONV_K):
            acc = acc + pltpu.roll(dc, tb + SUBLANES - s, 0)[:tb] * w_ref[CONV_K - 1 - s:CONV_K - s, :]
        o_ref[...] = acc.astype(o_ref.dtype)

    return pl.pallas_call(
        body, name="conv_bwd_dx", grid=(t // tb, 3),
        in_specs=[pl.BlockSpec((tb, d), lambda i, j: (i, j)),
                  pl.BlockSpec((SUBLANES, d), lambda i, j: (jnp.minimum((i + 1) * hb, t // SUBLANES - 1), j)),
                  pl.BlockSpec((CONV_K, d), lambda i, j: (0, j)),
                  ANY],
        out_specs=pl.BlockSpec((tb, d), lambda i, j: (i, j)),
        out_shape=jax.ShapeDtypeStruct(dprojm.shape, dprojm.dtype),
        input_output_aliases={3: 0},
        compiler_params=_cparams(("parallel", "parallel")),
    )(dy, dy, conv_w, dprojm)


def _beta_g(ba, alog, dtb):
    beta = _sigmoid(ba[:, :LANES])
    xa = ba[:, LANES:] + dtb
    softplus = jnp.maximum(xa, 0.0) + jnp.log(1.0 + jnp.exp(-jnp.abs(xa)))
    ea = jnp.exp(alog)
    return beta, -ea * softplus, ea, _sigmoid(xa)


def _inv_corrections(mats):
    ys = [-a for a in mats]
    ps = [_dot(a, a) for a in mats]
    steps = int(math.log2(CHUNK)) - 1
    for it in range(steps):
        ys = [y + p + _dot(y, p) for y, p in zip(ys, ps)]
        if it < steps - 1:
            ps = [_dot(p, p) for p in ps]
    return ys


def _chunk_masks():
    row = lax.broadcasted_iota(jnp.int32, (CHUNK, CHUNK), 0)
    col = lax.broadcasted_iota(jnp.int32, (CHUNK, CHUNK), 1)
    return row >= col, row > col, row <= col


def _col_of(mat, lane_idx, h):
    return jnp.sum(jnp.where(lane_idx == h, mat, 0.0), axis=1, keepdims=True)


def _row_of(mat, sub_idx, h):
    return jnp.sum(jnp.where(sub_idx == h, mat, 0.0), axis=0, keepdims=True)


def _dn_fwd(qkv, ba, alog, dtb, d):
    t = qkv.shape[0]
    heads = d // DN_DK
    n_chunks = t // CHUNK

    def body(qkv_ref, ba_ref, al_ref, dt_ref, o_ref, s_ref, y_ref, state):
        @pl.when(pl.program_id(0) == 0)
        def _():
            state[...] = jnp.zeros_like(state)

        tril, strict, _ = _chunk_masks()
        beta, g, _, _ = _beta_g(ba_ref[...], al_ref[...], dt_ref[...])
        gc = _dot(jnp.where(tril, 1.0, 0.0), g, NN, HIGHEST)
        gct = gc.T
        lane = lax.broadcasted_iota(jnp.int32, (CHUNK, LANES), 1)
        sub = lax.broadcasted_iota(jnp.int32, (LANES, CHUNK), 0)
        rowc = lax.broadcasted_iota(jnp.int32, (CHUNK, 1), 0)
        hs = range(heads)
        q = [qkv_ref[:, h * DN_DK:(h + 1) * DN_DK] for h in hs]
        k = [qkv_ref[:, d + h * DN_DK:d + (h + 1) * DN_DK] for h in hs]
        v = [qkv_ref[:, 2 * d + h * DN_DK:2 * d + (h + 1) * DN_DK] for h in hs]
        s0 = [state[h] for h in hs]
        gch = [_col_of(gc, lane, h) for h in hs]
        bh = [_col_of(beta, lane, h) for h in hs]
        dec = [jnp.where(tril, jnp.exp(gch[h] - _row_of(gct, sub, h)), 0.0) for h in hs]
        egc = [jnp.exp(gch[h]) for h in hs]
        gl = [jnp.sum(jnp.where(rowc == CHUNK - 1, gch[h], 0.0), axis=0, keepdims=True) for h in hs]
        kb = [k[h] * bh[h] for h in hs]
        a = [jnp.where(strict, _dot(kb[h], k[h], NT) * dec[h], 0.0) for h in hs]
        p = [_dot(q[h], k[h], NT) * dec[h] for h in hs]
        ycor = _inv_corrections(a)
        rhs = [jnp.concatenate([v[h] * bh[h], kb[h] * egc[h]], axis=1) for h in hs]
        sol = [rhs[h] + _dot(ycor[h], rhs[h]) for h in hs]
        vn = [sol[h][:, :DN_DK] - _dot(sol[h][:, DN_DK:], s0[h]) for h in hs]
        o = [_dot(q[h] * egc[h], s0[h]) + _dot(p[h], vn[h]) for h in hs]
        s_new = [s0[h] * jnp.exp(gl[h]) + _dot(k[h] * jnp.exp(gl[h] - gch[h]), vn[h], TN) for h in hs]
        for h in hs:
            o_ref[:, h * DN_DK:(h + 1) * DN_DK] = o[h]
            s_ref[0, h] = s0[h]
            y_ref[h] = ycor[h]
            state[h] = s_new[h]

    return pl.pallas_call(
        body, name="dn_fwd", grid=(n_chunks,),
        in_specs=[pl.BlockSpec((CHUNK, 3 * d), lambda i: (i, 0)),
                  pl.BlockSpec((CHUNK, 2 * LANES), lambda i: (i, 0)),
                  pl.BlockSpec((1, LANES), lambda i: (0, 0)),
                  pl.BlockSpec((1, LANES), lambda i: (0, 0))],
        out_specs=[pl.BlockSpec((CHUNK, d), lambda i: (i, 0)),
                   pl.BlockSpec((1, heads, DN_DK, DN_DK), lambda i: (i, 0, 0, 0)),
                   pl.BlockSpec((heads, CHUNK, CHUNK), lambda i: (0, i, 0))],
        out_shape=[jax.ShapeDtypeStruct((t, d), F32),
                   jax.ShapeDtypeStruct((n_chunks, heads, DN_DK, DN_DK), F32),
                   jax.ShapeDtypeStruct((heads, t, CHUNK), F32)],
        scratch_shapes=[pltpu.VMEM((heads, DN_DK, DN_DK), F32)],
        compiler_params=_cparams(("arbitrary",)),
    )(qkv, ba, alog, dtb)


def _dn_bwd(qkv, ba, alog, dtb, dout, states, ycors, d):
    t = qkv.shape[0]
    heads = d // DN_DK
    n_chunks = t // CHUNK
    rev = lambda i: n_chunks - 1 - i

    def body(qkv_ref, ba_ref, al_ref, dt_ref, do_ref, s_ref, y_ref,
             dqkv_ref, dba_ref, dal_ref, ddt_ref, dstate):
        @pl.when(pl.program_id(0) == 0)
        def _():
            dstate[...] = jnp.zeros_like(dstate)
            dal_ref[...] = jnp.zeros_like(dal_ref)
            ddt_ref[...] = jnp.zeros_like(ddt_ref)

        tril, strict, triu = _chunk_masks()
        beta, g, ea, sig_a = _beta_g(ba_ref[...], al_ref[...], dt_ref[...])
        gc = _dot(jnp.where(tril, 1.0, 0.0), g, NN, HIGHEST)
        gct = gc.T
        lane = lax.broadcasted_iota(jnp.int32, (CHUNK, LANES), 1)
        sub = lax.broadcasted_iota(jnp.int32, (LANES, CHUNK), 0)
        rowc = lax.broadcasted_iota(jnp.int32, (CHUNK, 1), 0)
        ones = jnp.ones((CHUNK, LANES), F32)
        hs = range(heads)
        rsum = lambda x_: jnp.sum(x_, axis=1, keepdims=True)
        q = [qkv_ref[:, h * DN_DK:(h + 1) * DN_DK] for h in hs]
        k = [qkv_ref[:, d + h * DN_DK:d + (h + 1) * DN_DK] for h in hs]
        v = [qkv_ref[:, 2 * d + h * DN_DK:2 * d + (h + 1) * DN_DK] for h in hs]
        dout_h = [do_ref[:, h * DN_DK:(h + 1) * DN_DK] for h in hs]
        s0 = [s_ref[0, h] for h in hs]
        dsn = [dstate[h] for h in hs]
        ycor = [y_ref[h] for h in hs]
        gch = [_col_of(gc, lane, h) for h in hs]
        bh = [_col_of(beta, lane, h) for h in hs]
        dec = [jnp.where(tril, jnp.exp(gch[h] - _row_of(gct, sub, h)), 0.0) for h in hs]
        egc = [jnp.exp(gch[h]) for h in hs]
        gl = [jnp.sum(jnp.where(rowc == CHUNK - 1, gch[h], 0.0), axis=0, keepdims=True) for h in hs]
        egl = [jnp.exp(gl[h]) for h in hs]
        ekd = [jnp.exp(gl[h] - gch[h]) for h in hs]
        kb = [k[h] * bh[h] for h in hs]
        kd = [k[h] * ekd[h] for h in hs]
        qg = [q[h] * egc[h] for h in hs]
        kbg = [kb[h] * egc[h] for h in hs]
        a = [jnp.where(strict, _dot(kb[h], k[h], NT) * dec[h], 0.0) for h in hs]
        p = [_dot(q[h], k[h], NT) * dec[h] for h in hs]
        rhs = [jnp.concatenate([v[h] * bh[h], kbg[h]], axis=1) for h in hs]
        sol = [rhs[h] + _dot(ycor[h], rhs[h]) for h in hs]
        w = [sol[h][:, DN_DK:] for h in hs]
        vn = [sol[h][:, :DN_DK] - _dot(w[h], s0[h]) for h in hs]
        dvn = [_dot(p[h], dout_h[h], TN) + _dot(kd[h], dsn[h]) for h in hs]
        dqg = [_dot(dout_h[h], s0[h], NT) for h in hs]
        dp = [jnp.where(tril, _dot(dout_h[h], vn[h], NT), 0.0) for h in hs]
        dkd = [_dot(vn[h], dsn[h], NT) for h in hs]
        dw = [-_dot(dvn[h], s0[h], NT) for h in hs]
        ds_new = [_dot(qg[h], dout_h[h], TN) + egl[h] * dsn[h] - _dot(w[h], dvn[h], TN) for h in hs]
        dgl = [jnp.sum(rsum(dsn[h] * s0[h]), axis=0, keepdims=True) * egl[h] for h in hs]
        dsol = [jnp.concatenate([dvn[h], dw[h]], axis=1) for h in hs]
        drhs = [dsol[h] + _dot(ycor[h], dsol[h], TN) for h in hs]
        dvb = [drhs[h][:, :DN_DK] for h in hs]
        dkbg = [drhs[h][:, DN_DK:] for h in hs]
        da = [jnp.where(strict, -_dot(drhs[h], sol[h], NT), 0.0) for h in hs]
        dma = [da[h] * dec[h] for h in hs]
        dmp = [dp[h] * dec[h] for h in hs]
        dkb = [_dot(dma[h], k[h]) + dkbg[h] * egc[h] for h in hs]
        dq = [_dot(dmp[h], k[h]) + dqg[h] * egc[h] for h in hs]
        dk = [_dot(dma[h], kb[h], TN) + _dot(dmp[h], q[h], TN) + dkd[h] * ekd[h] + dkb[h] * bh[h] for h in hs]
        e = [da[h] * a[h] + dp[h] * p[h] for h in hs]
        colsum = [_dot(e[h], ones, TN, HIGHEST) for h in hs]
        tkd = [rsum(dkd[h] * kd[h]) for h in hs]
        dgc_all = jnp.zeros((CHUNK, LANES), F32)
        dbeta_all = jnp.zeros((CHUNK, LANES), F32)
        for h in hs:
            dgc = rsum(e[h]) + rsum(dqg[h] * qg[h]) - tkd[h] + rsum(dkbg[h] * kbg[h])
            dgc = dgc + jnp.where(rowc == CHUNK - 1, dgl[h] + jnp.sum(tkd[h], axis=0, keepdims=True), 0.0)
            dgc_all = dgc_all + jnp.where(lane == h, dgc - colsum[h], 0.0)
            dbeta_all = dbeta_all + jnp.where(lane == h, rsum(dkb[h] * k[h]) + rsum(dvb[h] * v[h]), 0.0)
        for h in hs:
            dstate[h] = ds_new[h]
            dqkv_ref[:, h * DN_DK:(h + 1) * DN_DK] = dq[h]
            dqkv_ref[:, d + h * DN_DK:d + (h + 1) * DN_DK] = dk[h]
            dqkv_ref[:, 2 * d + h * DN_DK:2 * d + (h + 1) * DN_DK] = dvb[h] * bh[h]
        dg = _dot(jnp.where(triu, 1.0, 0.0), dgc_all, NN, HIGHEST)
        valid = lane < heads
        dbl = jnp.where(valid, dbeta_all * beta * (1.0 - beta), 0.0)
        dal = jnp.where(valid, -dg * ea * sig_a, 0.0)
        dba_ref[:, :LANES] = dbl.astype(dba_ref.dtype)
        dba_ref[:, LANES:] = dal.astype(dba_ref.dtype)
        dal_ref[...] += _fold8(jnp.where(valid, dg * g, 0.0))
        ddt_ref[...] += _fold8(dal)

    return pl.pallas_call(
        body, name="dn_bwd", grid=(n_chunks,),
        in_specs=[pl.BlockSpec((CHUNK, 3 * d), lambda i: (rev(i), 0)),
                  pl.BlockSpec((CHUNK, 2 * LANES), lambda i: (rev(i), 0)),
                  pl.BlockSpec((1, LANES), lambda i: (0, 0)),
                  pl.BlockSpec((1, LANES), lambda i: (0, 0)),
                  pl.BlockSpec((CHUNK, d), lambda i: (rev(i), 0)),
                  pl.BlockSpec((1, heads, DN_DK, DN_DK), lambda i: (rev(i), 0, 0, 0)),
                  pl.BlockSpec((heads, CHUNK, CHUNK), lambda i: (0, rev(i), 0))],
        out_specs=[pl.BlockSpec((CHUNK, 3 * d), lambda i: (rev(i), 0)),
                   pl.BlockSpec((CHUNK, 2 * LANES), lambda i: (rev(i), 0)),
                   pl.BlockSpec((SUBLANES, LANES), lambda i: (0, 0)),
                   pl.BlockSpec((SUBLANES, LANES), lambda i: (0, 0))],
        out_shape=[jax.ShapeDtypeStruct((t, 3 * d), F32),
                   jax.ShapeDtypeStruct((t, 2 * LANES), ACT),
                   jax.ShapeDtypeStruct((SUBLANES, LANES), F32),
                   jax.ShapeDtypeStruct((SUBLANES, LANES), F32)],
        scratch_shapes=[pltpu.VMEM((heads, DN_DK, DN_DK), F32)],
        compiler_params=_cparams(("arbitrary",)),
    )(qkv, ba, alog, dtb, dout, states, ycors)


def _sgu_mask():
    row = lax.broadcasted_iota(jnp.int32, (SGU_BLOCK, SGU_BLOCK), 0)
    col = lax.broadcasted_iota(jnp.int32, (SGU_BLOCK, SGU_BLOCK), 1)
    sh = int(math.log2(CHUNK))
    return lax.shift_right_logical(row, sh) >= lax.shift_right_logical(col, sh)


def _gate_sgu_fwd(o, projm, onw, lng, lnb, ws, bst, d):
    t = o.shape[0]
    heads, groups = d // DN_DK, d // SGU_GROUP_DIM
    tb = SGU_BLOCK
    row_spec = pl.BlockSpec((1, d), lambda i: (0, 0))

    def body(o_ref, z_ref, u_ref, v_ref, onw_ref, lng_ref, lnb_ref, ws_ref, bst_ref, ya_ref, yb_ref):
        for h in range(heads):
            cols = slice(h * DN_DK, (h + 1) * DN_DK)
            oh, zh = o_ref[:, cols], z_ref[:, cols]
            r = lax.rsqrt(jnp.mean(oh * oh, axis=1, keepdims=True) + RMS_EPS)
            ya_ref[:, cols] = (oh * r * onw_ref[:, cols] * (zh * _sigmoid(zh))).astype(ya_ref.dtype)
        xhat, _ = _ln_hat(_gelu(v_ref[...]))
        vgn = xhat * lng_ref[...] + lnb_ref[...]
        mask = _sgu_mask()
        lane = lax.broadcasted_iota(jnp.int32, (SGU_BLOCK, LANES), 1)
        bst_v = bst_ref[...]
        for gi in range(groups):
            cols = slice(gi * SGU_GROUP_DIM, (gi + 1) * SGU_GROUP_DIM)
            wsg = jnp.where(mask, ws_ref[gi], 0.0)
            sp = _dot(wsg, vgn[:, cols]) + _col_of(bst_v, lane, gi)
            yb_ref[:, cols] = (_gelu(u_ref[:, cols]) * sp).astype(yb_ref.dtype)

    return pl.pallas_call(
        body, name="gate_sgu_fwd", grid=(t // tb,),
        in_specs=[pl.BlockSpec((tb, d), lambda i: (i, 0)),
                  pl.BlockSpec((tb, d), lambda i: (i, 3)),
                  pl.BlockSpec((tb, d), lambda i: (i, 4)),
                  pl.BlockSpec((tb, d), lambda i: (i, 5)),
                  row_spec, row_spec, row_spec,
                  pl.BlockSpec((groups, SGU_BLOCK, SGU_BLOCK), lambda i: (0, 0, 0)),
                  pl.BlockSpec((SGU_BLOCK, LANES), lambda i: (0, 0))],
        out_specs=[pl.BlockSpec((tb, d), lambda i: (i, 0)), pl.BlockSpec((tb, d), lambda i: (i, 0))],
        out_shape=[jax.ShapeDtypeStruct((t, d), ACT), jax.ShapeDtypeStruct((t, d), ACT)],
        compiler_params=_cparams(("parallel",)),
    )(o, projm, projm, projm, onw, lng, lnb, ws, bst)


def _gate_sgu_bwd(dya, dyb, o, projm, onw, lng, lnb, ws, bst, dprojm, d):
    t = o.shape[0]
    heads, groups = d // DN_DK, d // SGU_GROUP_DIM
    tb = SGU_BLOCK
    row_spec = pl.BlockSpec((1, d), lambda i: (0, 0))
    acc_row = pl.BlockSpec((SUBLANES, d), lambda i: (0, 0))

    def body(dya_ref, dyb_ref, o_ref, z_ref, u_ref, v_ref, onw_ref, lng_ref, lnb_ref, ws_ref, bst_ref, alias_ref,
             do_ref, dp_ref, donw_ref, dlng_ref, dlnb_ref, dws_ref, dbst_ref):
        @pl.when(pl.program_id(0) == 0)
        def _():
            for r_ in (donw_ref, dlng_ref, dlnb_ref, dws_ref, dbst_ref):
                r_[...] = jnp.zeros_like(r_)

        donw = jnp.zeros((SUBLANES, DN_DK), F32)
        for h in range(heads):
            cols = slice(h * DN_DK, (h + 1) * DN_DK)
            oh, zh, dyah, wh = o_ref[:, cols], z_ref[:, cols], dya_ref[:, cols], onw_ref[:, cols]
            r = lax.rsqrt(jnp.mean(oh * oh, axis=1, keepdims=True) + RMS_EPS)
            on = oh * r
            sz = _sigmoid(zh)
            silu_z = zh * sz
            don = dyah * wh * silu_z
            dp_ref[:, cols] = (dyah * on * wh * (sz * (1.0 + zh * (1.0 - sz)))).astype(dp_ref.dtype)
            donw = donw + _fold8(dyah * on * silu_z)
            do_ref[:, cols] = r * (don - on * jnp.mean(don * on, axis=1, keepdims=True))
        donw_ref[...] += donw

        vgp, up = v_ref[...], u_ref[...]
        xhat, rstd = _ln_hat(_gelu(vgp))
        lng_v = lng_ref[...]
        vgn = xhat * lng_v + lnb_ref[...]
        ua = _gelu(up)
        mask = _sgu_mask()
        lane = lax.broadcasted_iota(jnp.int32, (SGU_BLOCK, LANES), 1)
        bst_v = bst_ref[...]
        dbst = jnp.zeros((SGU_BLOCK, LANES), F32)
        dvgn_parts, dua_parts = [], []
        for gi in range(groups):
            cols = slice(gi * SGU_GROUP_DIM, (gi + 1) * SGU_GROUP_DIM)
            wsg = jnp.where(mask, ws_ref[gi], 0.0)
            vg_g, dyb_g = vgn[:, cols], dyb_ref[:, cols]
            sp = _dot(wsg, vg_g) + _col_of(bst_v, lane, gi)
            dsp = dyb_g * ua[:, cols]
            dua_parts.append(dyb_g * sp)
            dws_ref[gi] += jnp.where(mask, _dot(dsp, vg_g, NT), 0.0)
            dbst = dbst + jnp.where(lane == gi, jnp.sum(dsp, axis=1, keepdims=True), 0.0)
            dvgn_parts.append(_dot(wsg, dsp, TN))
        dbst_ref[...] += dbst
        dvgn = jnp.concatenate(dvgn_parts, axis=1)
        dua = jnp.concatenate(dua_parts, axis=1)
        dlng_ref[...] += _fold8(dvgn * xhat)
        dlnb_ref[...] += _fold8(dvgn)
        dvga = _ln_bwd(dvgn * lng_v, xhat, rstd)
        dp_ref[:, d:2 * d] = (dua * _gelu_grad(up)).astype(dp_ref.dtype)
        dp_ref[:, 2 * d:] = (dvga * _gelu_grad(vgp)).astype(dp_ref.dtype)

    return pl.pallas_call(
        body, name="gate_sgu_bwd", grid=(t // tb,),
        in_specs=[pl.BlockSpec((tb, d), lambda i: (i, 0)),
                  pl.BlockSpec((tb, d), lambda i: (i, 0)),
                  pl.BlockSpec((tb, d), lambda i: (i, 0)),
                  pl.BlockSpec((tb, d), lambda i: (i, 3)),
                  pl.BlockSpec((tb, d), lambda i: (i, 4)),
                  pl.BlockSpec((tb, d), lambda i: (i, 5)),
                  row_spec, row_spec, row_spec,
                  pl.BlockSpec((groups, SGU_BLOCK, SGU_BLOCK), lambda i: (0, 0, 0)),
                  pl.BlockSpec((SGU_BLOCK, LANES), lambda i: (0, 0)),
                  ANY],
        out_specs=[pl.BlockSpec((tb, d), lambda i: (i, 0)),
                   pl.BlockSpec((tb, 3 * d), lambda i: (i, 1)),
                   pl.BlockSpec((SUBLANES, DN_DK), lambda i: (0, 0)),
                   acc_row, acc_row,
                   pl.BlockSpec((groups, SGU_BLOCK, SGU_BLOCK), lambda i: (0, 0, 0)),
                   pl.BlockSpec((SGU_BLOCK, LANES), lambda i: (0, 0))],
        out_shape=[jax.ShapeDtypeStruct((t, d), F32),
                   jax.ShapeDtypeStruct(dprojm.shape, dprojm.dtype),
                   jax.ShapeDtypeStruct((SUBLANES, DN_DK), F32),
                   jax.ShapeDtypeStruct((SUBLANES, d), F32),
                   jax.ShapeDtypeStruct((SUBLANES, d), F32),
                   jax.ShapeDtypeStruct((groups, SGU_BLOCK, SGU_BLOCK), F32),
                   jax.ShapeDtypeStruct((SGU_BLOCK, LANES), F32)],
        input_output_aliases={11: 1},
        compiler_params=_cparams(("arbitrary",)),
    )(dya, dyb, o, projm, projm, projm, onw, lng, lnb, ws, bst, dprojm)


def _mix_fwd(ya, yb, projm, x, wpa, wpb, wo, g1, b1, d, tb):
    t = x.shape[0]
    blk = pl.BlockSpec((tb, d), lambda i: (i, 0))
    wspec = pl.BlockSpec((d, d), lambda i: (0, 0))
    row_spec = pl.BlockSpec((1, d), lambda i: (0, 0))

    def body(ya_ref, yb_ref, ga_ref, gb_ref, x_ref, wpa_ref, wpb_ref, wo_ref, g_ref, b_ref,
             pa_ref, pb_ref, m_ref, h_ref, x1_ref, x1b_ref):
        pa = _dot(ya_ref[...], wpa_ref[...])
        pb = _dot(yb_ref[...], wpb_ref[...])
        m = _sigmoid(ga_ref[...]) * pa + _sigmoid(gb_ref[...]) * pb
        hres = ALPHA * x_ref[...] + _dot(m, wo_ref[...])
        xhat, _ = _ln_hat(hres)
        x1 = xhat * g_ref[...] + b_ref[...]
        pa_ref[...] = pa
        pb_ref[...] = pb
        m_ref[...] = m.astype(m_ref.dtype)
        h_ref[...] = hres
        x1_ref[...] = x1
        x1b_ref[...] = x1.astype(x1b_ref.dtype)

    f32_out = jax.ShapeDtypeStruct((t, d), F32)
    bf_out = jax.ShapeDtypeStruct((t, d), ACT)
    return pl.pallas_call(
        body, name="mix_fwd", grid=(t // tb,),
        in_specs=[blk, blk, pl.BlockSpec((tb, d), lambda i: (i, 6)), pl.BlockSpec((tb, d), lambda i: (i, 7)),
                  blk, wspec, wspec, wspec, row_spec, row_spec],
        out_specs=[blk] * 6,
        out_shape=[f32_out, f32_out, bf_out, f32_out, f32_out, bf_out],
        compiler_params=_cparams(("parallel",)),
    )(ya, yb, projm, projm, x, wpa, wpb, wo, g1, b1)


def _mix_bwd(dmix, pa, pb, projm, wpa, wpb, wo, d, tb):
    t = dmix.shape[0]
    blk = pl.BlockSpec((tb, d), lambda i: (i, 0))
    wspec = pl.BlockSpec((d, d), lambda i: (0, 0))

    def body(dmix_ref, pa_ref, pb_ref, ga_ref, gb_ref, wpa_ref, wpb_ref, wo_ref,
             dpa_ref, dpb_ref, dya_ref, dyb_ref, dg_ref):
        dm = _dot(dmix_ref[...], wo_ref[...], NT)
        sa, sb = _sigmoid(ga_ref[...]), _sigmoid(gb_ref[...])
        dpa, dpb = dm * sa, dm * sb
        dpa_ref[...] = dpa.astype(dpa_ref.dtype)
        dpb_ref[...] = dpb.astype(dpb_ref.dtype)
        dg_ref[:, :d] = (dm * pa_ref[...] * sa * (1.0 - sa)).astype(dg_ref.dtype)
        dg_ref[:, d:] = (dm * pb_ref[...] * sb * (1.0 - sb)).astype(dg_ref.dtype)
        dya_ref[...] = _dot(dpa, wpa_ref[...], NT)
        dyb_ref[...] = _dot(dpb, wpb_ref[...], NT)

    return pl.pallas_call(
        body, name="mix_bwd", grid=(t // tb,),
        in_specs=[blk, blk, blk, pl.BlockSpec((tb, d), lambda i: (i, 6)), pl.BlockSpec((tb, d), lambda i: (i, 7)),
                  wspec, wspec, wspec],
        out_specs=[blk, blk, blk, blk, pl.BlockSpec((tb, 2 * d), lambda i: (i, 3))],
        out_shape=[jax.ShapeDtypeStruct((t, d), ACT), jax.ShapeDtypeStruct((t, d), ACT),
                   jax.ShapeDtypeStruct((t, d), F32), jax.ShapeDtypeStruct((t, d), F32),
                   jax.ShapeDtypeStruct((t, 8 * d), ACT)],
        compiler_params=_cparams(("parallel",)),
    )(dmix, pa, pb, projm, projm, wpa, wpb, wo)


def _swiglu_fwd(gu, f, tb):
    t = gu.shape[0]

    def body(g_ref, u_ref, a_ref):
        gp = g_ref[...]
        a_ref[...] = (gp * _sigmoid(gp) * u_ref[...]).astype(a_ref.dtype)

    return pl.pallas_call(
        body, name="swiglu_fwd", grid=(t // tb,),
        in_specs=[pl.BlockSpec((tb, f), lambda i: (i, 0)), pl.BlockSpec((tb, f), lambda i: (i, 1))],
        out_specs=pl.BlockSpec((tb, f), lambda i: (i, 0)),
        out_shape=jax.ShapeDtypeStruct((t, f), ACT),
        compiler_params=_cparams(("parallel",)),
    )(gu, gu)


def _swiglu_bwd(da, gu, f, tb):
    t = gu.shape[0]

    def body(da_ref, g_ref, u_ref, dgu_ref):
        gp, da_v = g_ref[...], da_ref[...]
        sg = _sigmoid(gp)
        dgu_ref[:, :f] = (da_v * u_ref[...] * sg * (1.0 + gp * (1.0 - sg))).astype(dgu_ref.dtype)
        dgu_ref[:, f:] = (da_v * gp * sg).astype(dgu_ref.dtype)

    return pl.pallas_call(
        body, name="swiglu_bwd", grid=(t // tb,),
        in_specs=[pl.BlockSpec((tb, f), lambda i: (i, 0)), pl.BlockSpec((tb, f), lambda i: (i, 0)),
                  pl.BlockSpec((tb, f), lambda i: (i, 1))],
        out_specs=pl.BlockSpec((tb, 2 * f), lambda i: (i, 0)),
        out_shape=jax.ShapeDtypeStruct((t, 2 * f), ACT),
        compiler_params=_cparams(("parallel",)),
    )(da, gu, gu)


def _res_ln_fwd(x1, f, g, b, tb):
    t, d = x1.shape
    blk = pl.BlockSpec((tb, d), lambda i: (i, 0))
    row_spec = pl.BlockSpec((1, d), lambda i: (0, 0))

    def body(x_ref, f_ref, g_ref, b_ref, h_ref, y_ref, yb_ref):
        hres = ALPHA * x_ref[...] + f_ref[...]
        xhat, _ = _ln_hat(hres)
        y = xhat * g_ref[...] + b_ref[...]
        h_ref[...] = hres
        y_ref[...] = y
        yb_ref[...] = y.astype(yb_ref.dtype)

    return pl.pallas_call(
        body, name="res_ln_fwd", grid=(t // tb,),
        in_specs=[blk, blk, row_spec, row_spec], out_specs=[blk, blk, blk],
        out_shape=[jax.ShapeDtypeStruct((t, d), F32), jax.ShapeDtypeStruct((t, d), F32),
                   jax.ShapeDtypeStruct((t, d), ACT)],
        compiler_params=_cparams(("parallel",)),
    )(x1, f, g, b)


def _ln_bwd_call(dy, hres, g, tb):
    t, d = dy.shape
    blk = pl.BlockSpec((tb, d), lambda i: (i, 0))
    acc = pl.BlockSpec((SUBLANES, d), lambda i: (0, 0))

    def body(dy_ref, h_ref, g_ref, dh_ref, dhb_ref, dg_ref, db_ref):
        @pl.when(pl.program_id(0) == 0)
        def _():
            dg_ref[...] = jnp.zeros_like(dg_ref)
            db_ref[...] = jnp.zeros_like(db_ref)

        dy_v = dy_ref[...]
        xhat, r = _ln_hat(h_ref[...])
        dh = _ln_bwd(dy_v * g_ref[...], xhat, r)
        dh_ref[...] = dh
        dhb_ref[...] = dh.astype(dhb_ref.dtype)
        dg_ref[...] += _fold8(dy_v * xhat)
        db_ref[...] += _fold8(dy_v)

    return pl.pallas_call(
        body, name="ln_bwd", grid=(t // tb,),
        in_specs=[blk, blk, pl.BlockSpec((1, d), lambda i: (0, 0))],
        out_specs=[blk, blk, acc, acc],
        out_shape=[jax.ShapeDtypeStruct((t, d), F32), jax.ShapeDtypeStruct((t, d), ACT),
                   jax.ShapeDtypeStruct((SUBLANES, d), F32), jax.ShapeDtypeStruct((SUBLANES, d), F32)],
        compiler_params=_cparams(("arbitrary",)),
    )(dy, hres, g)


def _loss_head(y, target, tb):
    t, d = y.shape
    blk = pl.BlockSpec((tb, d), lambda i: (i, 0))

    def body(y_ref, t_ref, dy_ref, l_ref):
        @pl.when(pl.program_id(0) == 0)
        def _():
            l_ref[...] = jnp.zeros_like(l_ref)

        err = y_ref[...] - t_ref[...]
        dy_ref[...] = err * (1.0 / d)
        sq = _fold8(err * err)
        part = sq[:, :LANES]
        for c in range(1, d // LANES):
            part = part + sq[:, c * LANES:(c + 1) * LANES]
        l_ref[...] += part

    return pl.pallas_call(
        body, name="loss_head", grid=(t // tb,),
        in_specs=[blk, blk],
        out_specs=[blk, pl.BlockSpec((SUBLANES, LANES), lambda i: (0, 0))],
        out_shape=[jax.ShapeDtypeStruct((t, d), F32), jax.ShapeDtypeStruct((SUBLANES, LANES), F32)],
        compiler_params=_cparams(("arbitrary",)),
    )(y, target)


def _adamw(w, g, m, v):
    shape = w.shape
    cols = shape[-1]
    w2, g2, m2, v2 = (a.reshape(-1, cols) for a in (w, g, m, v))
    rows = w2.shape[0]
    tr = _tile(rows, 256, SUBLANES)
    blk = pl.BlockSpec((tr, cols), lambda i: (i, 0))

    def body(w_ref, g_ref, m_ref, v_ref, d_ref, nm_ref, nv_ref):
        g_v = g_ref[...]
        nm = ADAM_B1 * m_ref[...] + (1.0 - ADAM_B1) * g_v
        nv = ADAM_B2 * v_ref[...] + (1.0 - ADAM_B2) * (g_v * g_v)
        m_hat = nm / (1.0 - ADAM_B1 ** ADAM_STEP)
        v_hat = nv / (1.0 - ADAM_B2 ** ADAM_STEP)
        d_ref[...] = -ADAM_LR * (m_hat / (jnp.sqrt(v_hat) + ADAM_EPS) + ADAM_WD * w_ref[...])
        nm_ref[...] = nm
        nv_ref[...] = nv

    out = jax.ShapeDtypeStruct((rows, cols), F32)
    res = pl.pallas_call(
        body, name="adamw", grid=(rows // tr,),
        in_specs=[blk] * 4, out_specs=[blk] * 3, out_shape=[out] * 3,
        compiler_params=_cparams(("parallel",)),
    )(w2, g2, m2, v2)
    return tuple(r.reshape(shape) for r in res)


def _place():
    x, y, c = lax.axis_index("x"), lax.axis_index("y"), lax.axis_index("c")
    return x, y, c, [(1 - x, y), (x, 1 - y), (1 - x, 1 - y)]


def _remote(src, dst, send_sems, recv_sems, k, to):
    return pltpu.make_async_remote_copy(src_ref=src, dst_ref=dst, send_sem=send_sems.at[k],
                                        recv_sem=recv_sems.at[k], device_id=to, device_id_type=MESH)


def _all_gather_weights(shards):
    n = len(shards)

    def body(*refs):
        x_refs, out_refs = refs[:n], refs[n:2 * n]
        send_sems, recv_sems = refs[2 * n:]
        x, y, c, chips = _place()
        sibling = (x, y, 1 - c)
        mine = 2 * x + y
        first = [_remote(x_refs[t].at[c], out_refs[t].at[mine, c], send_sems, recv_sems, 6 * t + j, (cx, cy, c))
                 for j, (cx, cy) in enumerate(chips) for t in range(n)]
        for cp in first:
            cp.start()
        passed = []
        for j, (cx, cy) in enumerate(chips):
            for t in range(n):
                theirs = out_refs[t].at[2 * cx + cy, c]
                _remote(theirs, theirs, send_sems, recv_sems, 6 * t + j, (cx, cy, c)).wait_recv()
                fwd = _remote(theirs, theirs, send_sems, recv_sems, 6 * t + 3 + j, sibling)
                fwd.start()
                passed.append(fwd)
        for j, (cx, cy) in enumerate(chips):
            for t in range(n):
                other = out_refs[t].at[2 * cx + cy, 1 - c]
                _remote(other, other, send_sems, recv_sems, 6 * t + 3 + j, sibling).wait_recv()
        for cp in first + passed:
            cp.wait_send()

    return pl.pallas_call(
        body, name="all_gather_weights",
        in_specs=[ANY] * n, out_specs=[ANY] * n,
        out_shape=[jax.ShapeDtypeStruct((N_CHIPS,) + s.shape, s.dtype) for s in shards],
        scratch_shapes=[pltpu.SemaphoreType.DMA((6 * n,)), pltpu.SemaphoreType.DMA((6 * n,))],
    )(*shards)


def _sibling_exchange(grads, small):
    n = len(grads)

    def body(*refs):
        g_refs, small_ref = refs[:n], refs[n]
        land_refs, sland_ref = refs[n + 1:2 * n + 1], refs[2 * n + 1]
        send_sems, recv_sems = refs[2 * n + 2:]
        x, y, c, _ = _place()
        sibling = (x, y, 1 - c)
        cps = [_remote(g_refs[t].at[1 - c], land_refs[t], send_sems, recv_sems, t, sibling) for t in range(n)]
        cps.append(_remote(small_ref, sland_ref, send_sems, recv_sems, n, sibling))
        for cp in cps:
            cp.start()
        for cp in cps:
            cp.wait()

    return pl.pallas_call(
        body, name="grad_sibling_exchange",
        in_specs=[ANY] * (n + 1), out_specs=[ANY] * (n + 1),
        out_shape=[jax.ShapeDtypeStruct(g.shape[1:], g.dtype) for g in grads]
        + [jax.ShapeDtypeStruct(small.shape, small.dtype)],
        scratch_shapes=[pltpu.SemaphoreType.DMA((n + 1,)), pltpu.SemaphoreType.DMA((n + 1,))],
    )(*grads, small)


def _chip_exchange(travel, small):
    n = len(travel)

    def body(*refs):
        t_refs, small_ref = refs[:n], refs[n]
        land_refs, sland_ref = refs[n + 1:2 * n + 1], refs[2 * n + 1]
        send_sems, recv_sems = refs[2 * n + 2:]
        x, y, c, chips = _place()
        mine = 2 * x + y
        cps = []
        for j, (cx, cy) in enumerate(chips):
            to = (cx, cy, c)
            for t in range(n):
                cps.append(_remote(t_refs[t].at[2 * cx + cy], land_refs[t].at[mine], send_sems, recv_sems,
                                   3 * t + j, to))
            cps.append(_remote(small_ref, sland_ref.at[mine], send_sems, recv_sems, 3 * n + j, to))
        for cp in cps:
            cp.start()
        for cp in cps:
            cp.wait()

    return pl.pallas_call(
        body, name="grad_chip_exchange",
        in_specs=[ANY] * (n + 1), out_specs=[ANY] * (n + 1),
        out_shape=[jax.ShapeDtypeStruct(g.shape, g.dtype) for g in travel]
        + [jax.ShapeDtypeStruct((N_CHIPS,) + small.shape, small.dtype)],
        scratch_shapes=[pltpu.SemaphoreType.DMA((3 * n + 3,)), pltpu.SemaphoreType.DMA((3 * n + 3,))],
    )(*travel, small)


def _sibling_merge(reduced):
    n = len(reduced)

    def body(*refs):
        r_refs, out_refs = refs[:n], refs[n:2 * n]
        send_sems, recv_sems = refs[2 * n:]
        x, y, c, _ = _place()
        cps = [_remote(r_refs[t], out_refs[t], send_sems, recv_sems, t, (x, y, 1 - c)) for t in range(n)]
        for cp in cps:
            cp.start()
        for cp in cps:
            cp.wait()

    return pl.pallas_call(
        body, name="grad_sibling_merge",
        in_specs=[ANY] * n, out_specs=[ANY] * n,
        out_shape=[jax.ShapeDtypeStruct(r.shape, r.dtype) for r in reduced],
        scratch_shapes=[pltpu.SemaphoreType.DMA((n,)), pltpu.SemaphoreType.DMA((n,))],
    )(*reduced)


def _pair_sum(place, grad, land):
    _, n, r, c = grad.shape
    tr = _tile(r, 256, SUBLANES)

    def body(place_ref, a_ref, b_ref, travel_ref, own_ref):
        total = a_ref[0, 0] + b_ref[0]
        travel_ref[0] = total.astype(travel_ref.dtype)

        @pl.when(pl.program_id(1) == place_ref[1])
        def _():
            own_ref[...] = total

    return pl.pallas_call(
        body, name="grad_pair_sum",
        grid_spec=pltpu.PrefetchScalarGridSpec(
            num_scalar_prefetch=1, grid=(r // tr, n),
            in_specs=[pl.BlockSpec((1, 1, tr, c), lambda i, s, p: (p[0], s, i, 0)),
                      pl.BlockSpec((1, tr, c), lambda i, s, p: (s, i, 0))],
            out_specs=[pl.BlockSpec((1, tr, c), lambda i, s, p: (s, i, 0)),
                       pl.BlockSpec((tr, c), lambda i, s, p: (i, 0))]),
        out_shape=[jax.ShapeDtypeStruct((n, r, c), BF16), jax.ShapeDtypeStruct((r, c), F32)],
        compiler_params=_cparams(("parallel", "arbitrary")),
    )(place, grad, land)


def _chip_sum(place, own, land, name):
    n, r, c = land.shape
    tr = _tile(r, 256, SUBLANES)

    def body(place_ref, own_ref, land_ref, o_ref):
        mine = place_ref[1]
        acc = jnp.zeros(o_ref.shape, F32)
        for s in range(n):
            acc = acc + jnp.where(mine == s, own_ref[...], land_ref[s].astype(F32))
        o_ref[...] = acc

    return pl.pallas_call(
        body, name=name,
        grid_spec=pltpu.PrefetchScalarGridSpec(
            num_scalar_prefetch=1, grid=(r // tr,),
            in_specs=[pl.BlockSpec((tr, c), lambda i, p: (i, 0)),
                      pl.BlockSpec((n, tr, c), lambda i, p: (0, i, 0))],
            out_specs=pl.BlockSpec((tr, c), lambda i, p: (i, 0))),
        out_shape=jax.ShapeDtypeStruct((r, c), F32),
        compiler_params=_cparams(("parallel",)),
    )(place, own, land)


def _add2(a, b):
    rows = a.shape[0]
    tr = _tile(rows, 256, SUBLANES)
    blk = pl.BlockSpec((tr, PACK_W), lambda i: (i, 0))

    def body(a_ref, b_ref, o_ref):
        o_ref[...] = a_ref[...] + b_ref[...]

    return pl.pallas_call(
        body, name="grad_small_pair_sum", grid=(rows // tr,), in_specs=[blk, blk], out_specs=blk,
        out_shape=jax.ShapeDtypeStruct(a.shape, F32), compiler_params=_cparams(("parallel",)),
    )(a, b)


def _reduce_gradients(place, grads, small):
    *lands, sland = _sibling_exchange(grads, small)
    pairs = [_pair_sum(place, g, land) for g, land in zip(grads, lands)]
    small_chip = _add2(small, sland)
    *lands2, sland2 = _chip_exchange([p[0] for p in pairs], small_chip)
    reduced = [_chip_sum(place, p[1], land2, "grad_chip_sum") for p, land2 in zip(pairs, lands2)]
    small_total = _chip_sum(place, small_chip, sland2, "grad_small_chip_sum")
    first_core = place[0] == 0
    merged = [jnp.stack([jnp.where(first_core, mine, other), jnp.where(first_core, other, mine)])
              for mine, other in zip(reduced, _sibling_merge(reduced))]
    return merged, small_total


_BIG = (("w_in", 2), ("w_pa", 1), ("w_pb", 1), ("w_o", 1), ("w_ffn_gate", 2), ("w_ffn_up", 2),
        ("w_ffn_down", 1))
_SMALL = ("conv_w", "a_log", "dt_bias", "o_norm_w", "sgu_ln_g", "sgu_ln_b", "w_s", "b_s",
          "ln1_g", "ln1_b", "ln2_g", "ln2_b")


def _pad_rows(flat, mult):
    rows = -(-flat.shape[-1] // (PACK_W * mult)) * mult
    pad = rows * PACK_W - flat.shape[-1]
    flat = jnp.pad(flat, [(0, 0)] * (flat.ndim - 1) + [(0, pad)])
    return flat.reshape(flat.shape[:-1] + (rows, PACK_W))


def _unshard(gathered, local, chip, layer, axis):
    parts = [jnp.where(chip == s, local[layer], gathered[s, layer]) for s in range(N_CHIPS)]
    return jnp.concatenate(parts, axis=axis - 1)


def _to_shards(full, axis):
    l, r, c = full.shape
    if axis == 1:
        return full.reshape(l, N_CHIPS, r // N_CHIPS, c)
    return jnp.transpose(full.reshape(l, r, N_CHIPS, c // N_CHIPS), (0, 2, 1, 3))


def _row(v, width=None):
    v = v.reshape(1, -1).astype(F32)
    if width is not None and v.shape[1] < width:
        v = jnp.pad(v, ((0, 0), (0, width - v.shape[1])))
    return v


def _layer_consts(p, l, d):
    heads = d // DN_DK
    return dict(
        alog=_row(p["a_log"][l], LANES), dtb=_row(p["dt_bias"][l], LANES),
        onw=_row(jnp.tile(p["o_norm_w"][l], heads)),
        lng=_row(p["sgu_ln_g"][l]), lnb=_row(p["sgu_ln_b"][l]),
        ws=p["w_s"][l].astype(F32),
        bst=jnp.pad(p["b_s"][l].T, ((0, 0), (0, LANES - p["b_s"].shape[1]))),
        g1=_row(p["ln1_g"][l]), b1=_row(p["ln1_b"][l]), g2=_row(p["ln2_g"][l]), b2=_row(p["ln2_b"][l]))


def _layer_fwd(x, xb, wl, cl, d, f, tb):
    projm = _matmul(xb, wl["wm"], NN, "proj_main")
    ba = _matmul(xb, wl["wba"], NN, "proj_gates")
    qkv = _conv_fwd(projm, wl["conv"], d, _tile(x.shape[0], 2 * tb, SUBLANES))
    o, states, ycors = _dn_fwd(qkv, ba, cl["alog"], cl["dtb"], d)
    ya, yb = _gate_sgu_fwd(o, projm, cl["onw"], cl["lng"], cl["lnb"], cl["ws"], cl["bst"], d)
    pa, pb, m, h1, x1, x1b = _mix_fwd(ya, yb, projm, x, wl["wpa"], wl["wpb"], wl["wo"], cl["g1"], cl["b1"], d, tb)
    gu = _matmul(x1b, wl["wgu"], NN, "ffn_in")
    act = _swiglu_fwd(gu, f, tb)
    ffn = _matmul(act, wl["wd"], NN, "ffn_out")
    h2, x2, x2b = _res_ln_fwd(x1, ffn, cl["g2"], cl["b2"], tb)
    saved = dict(xb=xb, projm=projm, ba=ba, qkv=qkv, o=o, states=states, ycors=ycors, ya=ya, yb=yb,
                 pa=pa, pb=pb, m=m, h1=h1, x1b=x1b, gu=gu, act=act, h2=h2)
    return x2, x2b, saved


def _layer_bwd(dx2, sv, wl, cl, d, f, tb):
    g = {}
    dh2, dh2b, dg2, db2 = _ln_bwd_call(dx2, sv["h2"], cl["g2"], tb)
    g["ln2_g"], g["ln2_b"] = dg2.sum(0), db2.sum(0)
    g["wd"] = _matmul(sv["act"], dh2b, TN, "ffn_out_dw")
    da = _matmul(dh2b, wl["wd"], NT, "ffn_out_dx")
    dgu = _swiglu_bwd(da, sv["gu"], f, tb)
    g["wgu"] = _matmul(sv["x1b"], dgu, TN, "ffn_in_dw")
    dx1 = _matmul(dgu, wl["wgu"], NT, "ffn_in_dx", add=dh2, coef=ALPHA)
    dh1, dh1b, dg1, db1 = _ln_bwd_call(dx1, sv["h1"], cl["g1"], tb)
    g["ln1_g"], g["ln1_b"] = dg1.sum(0), db1.sum(0)
    g["wo"] = _matmul(sv["m"], dh1b, TN, "wo_dw")
    dpa, dpb, dya, dyb, dprojm = _mix_bwd(dh1b, sv["pa"], sv["pb"], sv["projm"], wl["wpa"], wl["wpb"], wl["wo"], d, tb)
    g["wpa"] = _matmul(sv["ya"], dpa, TN, "wpa_dw")
    g["wpb"] = _matmul(sv["yb"], dpb, TN, "wpb_dw")
    do, dprojm, donw, dlng, dlnb, dws, dbst = _gate_sgu_bwd(
        dya, dyb, sv["o"], sv["projm"], cl["onw"], cl["lng"], cl["lnb"], cl["ws"], cl["bst"], dprojm, d)
    heads, groups = d // DN_DK, d // SGU_GROUP_DIM
    g["o_norm_w"], g["sgu_ln_g"], g["sgu_ln_b"] = donw.sum(0), dlng.sum(0), dlnb.sum(0)
    g["w_s"], g["b_s"] = dws, dbst[:, :groups].T
    dqkv, dba, dal, ddt = _dn_bwd(sv["qkv"], sv["ba"], cl["alog"], cl["dtb"], do, sv["states"], sv["ycors"], d)
    g["a_log"], g["dt_bias"] = dal.sum(0)[:heads], ddt.sum(0)[:heads]
    tbc = _tile(dx2.shape[0], 2 * tb, SUBLANES)
    dy, dcw = _conv_bwd_dy(sv["projm"], wl["conv"], dqkv, d, tbc)
    g["conv_w"] = dcw.sum(1)
    dprojm = _conv_bwd_dx(dy, wl["conv"], dprojm, d, tbc)
    g["wm"] = _matmul(sv["xb"], dprojm, TN, "proj_main_dw")
    g["wba"] = _matmul(sv["xb"], dba, TN, "proj_gates_dw")
    dx = _matmul(dba, wl["wba"], NT, "proj_gates_dx", add=dh1, coef=ALPHA)
    dx = _matmul(dprojm, wl["wm"], NT, "proj_main_dx", add=dx)
    return dx, g


def _local_step(x, target, full, small_w):
    t, d = x.shape
    heads = d // DN_DK
    f = full["w_ffn_gate"][0].shape[-1]
    tb = _tile(t, 256, SUBLANES)
    q4 = 4 * d
    w_in = full["w_in"]
    layers, consts = [], []
    for l in range(DEPTH):
        wba = jnp.zeros((d, 2 * LANES), w_in[l].dtype)
        wba = wba.at[:, :heads].set(w_in[l][:, q4:q4 + heads])
        wba = wba.at[:, LANES:LANES + heads].set(w_in[l][:, q4 + heads:q4 + 2 * heads])
        layers.append(dict(
            wm=jnp.concatenate([w_in[l][:, :q4], w_in[l][:, q4 + 2 * heads:]], axis=1), wba=wba,
            conv=full["conv_w"][l], wpa=full["w_pa"][l], wpb=full["w_pb"][l], wo=full["w_o"][l],
            wgu=jnp.concatenate([full["w_ffn_gate"][l], full["w_ffn_up"][l]], axis=1), wd=full["w_ffn_down"][l]))
        consts.append(_layer_consts(small_w, l, d))

    h, hb, saved = x, x.astype(ACT), []
    for l in range(DEPTH):
        h, hb, sv = _layer_fwd(h, hb, layers[l], consts[l], d, f, tb)
        saved.append(sv)
    dy, loss_parts = _loss_head(h, target, tb)
    grads = []
    for l in reversed(range(DEPTH)):
        dy, g = _layer_bwd(dy, saved[l], layers[l], consts[l], d, f, tb)
        grads.append(g)
    grads = grads[::-1]

    stack = lambda k: jnp.stack([g[k] for g in grads])
    gm, gba, ggu = stack("wm"), stack("wba"), stack("wgu")
    out = {
        "w_in": jnp.concatenate([gm[:, :, :q4], gba[:, :, :heads], gba[:, :, LANES:LANES + heads], gm[:, :, q4:]], axis=2),
        "w_pa": stack("wpa"), "w_pb": stack("wpb"), "w_o": stack("wo"),
        "w_ffn_gate": ggu[:, :, :f], "w_ffn_up": ggu[:, :, f:], "w_ffn_down": stack("wd")}
    for k in _SMALL:
        out[k] = stack(k)
    return loss_parts, dy, out


def kernel(x, w_in, conv_w, a_log, dt_bias, o_norm_w, sgu_ln_g, sgu_ln_b, w_s, b_s, w_pa, w_pb, w_o, ln1_g, ln1_b, w_ffn_gate, w_ffn_up, w_ffn_down, ln2_g, ln2_b, loss_target, m_w_in, m_conv_w, m_a_log, m_dt_bias, m_o_norm_w, m_sgu_ln_g, m_sgu_ln_b, m_w_s, m_b_s, m_w_pa, m_w_pb, m_w_o, m_ln1_g, m_ln1_b, m_w_ffn_gate, m_w_ffn_up, m_w_ffn_down, m_ln2_g, m_ln2_b, v_w_in, v_conv_w, v_a_log, v_dt_bias, v_o_norm_w, v_sgu_ln_g, v_sgu_ln_b, v_w_s, v_b_s, v_w_pa, v_w_pb, v_w_o, v_ln1_g, v_ln1_b, v_w_ffn_gate, v_w_ffn_up, v_w_ffn_down, v_ln2_g, v_ln2_b):
    names = ("w_in", "conv_w", "a_log", "dt_bias", "o_norm_w", "sgu_ln_g", "sgu_ln_b", "w_s", "b_s", "w_pa",
             "w_pb", "w_o", "ln1_g", "ln1_b", "w_ffn_gate", "w_ffn_up", "w_ffn_down", "ln2_g", "ln2_b")
    w = dict(zip(names, (w_in, conv_w, a_log, dt_bias, o_norm_w, sgu_ln_g, sgu_ln_b, w_s, b_s, w_pa, w_pb, w_o,
                         ln1_g, ln1_b, w_ffn_gate, w_ffn_up, w_ffn_down, ln2_g, ln2_b)))
    mom = dict(zip(names, (m_w_in, m_conv_w, m_a_log, m_dt_bias, m_o_norm_w, m_sgu_ln_g, m_sgu_ln_b, m_w_s, m_b_s,
                           m_w_pa, m_w_pb, m_w_o, m_ln1_g, m_ln1_b, m_w_ffn_gate, m_w_ffn_up, m_w_ffn_down,
                           m_ln2_g, m_ln2_b)))
    var = dict(zip(names, (v_w_in, v_conv_w, v_a_log, v_dt_bias, v_o_norm_w, v_sgu_ln_g, v_sgu_ln_b, v_w_s, v_b_s,
                           v_w_pa, v_w_pb, v_w_o, v_ln1_g, v_ln1_b, v_w_ffn_gate, v_w_ffn_up, v_w_ffn_down,
                           v_ln2_g, v_ln2_b)))
    chip = 2 * lax.axis_index("x") + lax.axis_index("y")
    place = jnp.stack([lax.axis_index("c"), chip]).astype(jnp.int32)

    local = [w[k].astype(BF16) for k, _ in _BIG] + [conv_w]
    gathered = _all_gather_weights(local)
    full = {k: [_unshard(gt, lc, chip, l, axis) for l in range(DEPTH)]
            for (k, axis), gt, lc in zip(_BIG, gathered, local)}
    full["conv_w"] = [_unshard(gathered[-1], conv_w, chip, l, 2) for l in range(DEPTH)]

    small_w = {k: w[k] for k in _SMALL if k != "conv_w"}
    loss_parts, grad_x, g = _local_step(x[0], loss_target[0], full, small_w)

    small_sizes = [g[k].size for k in _SMALL]
    small = _pad_rows(jnp.concatenate([g[k].reshape(-1) for k in _SMALL]), SUBLANES)
    reduced, small_total = _reduce_gradients(place, [_to_shards(g[k], axis) for k, axis in _BIG], small)
    grads = {k: r for (k, _), r in zip(_BIG, reduced)}
    small_total, off = small_total.reshape(-1), 0
    for k, n in zip(_SMALL, small_sizes):
        grads[k] = small_total[off:off + n].reshape(g[k].shape)
        off += n
    grads["conv_w"] = lax.dynamic_index_in_dim(_to_shards(grads["conv_w"], 2), chip, 1, keepdims=False)

    delta, new_m, new_v = {}, {}, {}
    for k in [k for k, _ in _BIG] + ["conv_w"]:
        delta[k], new_m[k], new_v[k] = _adamw(w[k], grads[k], mom[k], var[k])
    rep = [k for k in _SMALL if k != "conv_w"]
    pack = lambda dct: _pad_rows(jnp.concatenate([dct[k].reshape(-1) for k in rep]), SUBLANES)
    packed = _adamw(pack(w), pack(grads), pack(mom), pack(var))
    off = 0
    for k in rep:
        n = w[k].size
        for dst, src in zip((delta, new_m, new_v), packed):
            dst[k] = src.reshape(-1)[off:off + n].reshape(w[k].shape)
        off += n

    loss = 0.5 * lax.psum(jnp.sum(loss_parts), ("x", "y", "c")) / x.shape[-1]
    return (loss, grad_x[None], *[grads[k] for k in names], *[delta[k] for k in names],
            *[new_m[k] for k in names], *[new_v[k] for k in names])
```

```python
import math

import jax
import jax.numpy as jnp
from jax import lax
from jax.experimental import pallas as pl
from jax.experimental.pallas import tpu as pltpu

F32 = jnp.float32
BF16 = jnp.bfloat16
MXU_DTYPE = jnp.bfloat16
ACT = jnp.bfloat16
HIGHEST = lax.Precision.HIGHEST

DEPTH = 2
CHUNK = 64
DN_GROUP = 2
SGU_BLOCK = 128
CONV_K = 4
DN_DK = 128
SGU_GROUP_DIM = 128
LN_EPS = 1e-5
RMS_EPS = 1e-6
ALPHA = (2 * DEPTH) ** 0.25
ADAM_LR, ADAM_B1, ADAM_B2, ADAM_EPS, ADAM_WD, ADAM_STEP = 0.001, 0.9, 0.999, 1e-08, 0.01, 10

LANES = 128
SUBLANES = 8
VMEM_LIMIT = 52 * 2 ** 20
PACK_W = 1024
N_CHIPS = 4

NN = ((1,), (0,))
NT = ((1,), (1,))
TN = ((0,), (0,))
MESH = pl.DeviceIdType.MESH
ANY = pl.BlockSpec(memory_space=pl.ANY)


def _dot(a, b, dims=NN, prec=None):
    if prec is None:
        a = a.astype(MXU_DTYPE)
        b = b.astype(MXU_DTYPE)
    return lax.dot_general(a, b, (dims, ((), ())), preferred_element_type=F32, precision=prec)


def _cparams(sem=None):
    return pltpu.CompilerParams(dimension_semantics=sem, vmem_limit_bytes=VMEM_LIMIT)


def _tile(dim, pref, unit=LANES):
    t = (min(pref, dim) // unit) * unit
    while t >= unit:
        if dim % t == 0:
            return t
        t -= unit
    return dim


def _fold8(x):
    r, n = x.shape
    return x.reshape(r // SUBLANES, SUBLANES, n).sum(axis=0)


def _sigmoid(x):
    return 1.0 / (1.0 + jnp.exp(-x))


def _gelu(x):
    return 0.5 * x * (1.0 + lax.erf(x * (2.0 ** -0.5)))


def _gelu_grad(x):
    return 0.5 * (1.0 + lax.erf(x * (2.0 ** -0.5))) + x * jnp.exp(-0.5 * x * x) * (2.0 * math.pi) ** -0.5


def _ln_hat(h):
    mu = jnp.mean(h, axis=-1, keepdims=True)
    xc = h - mu
    var = jnp.mean(xc * xc, axis=-1, keepdims=True)
    r = lax.rsqrt(var + LN_EPS)
    return xc * r, r


def _ln_bwd(dxhat, xhat, r):
    return r * (dxhat - jnp.mean(dxhat, axis=-1, keepdims=True)
                - xhat * jnp.mean(dxhat * xhat, axis=-1, keepdims=True))


MM_TILE = 1536


def _matmul(a, b, dims, name, out_dtype=F32, add=None, coef=1.0, tm=MM_TILE, tn=MM_TILE, tk=MM_TILE):
    if dims == NN:
        (m, k), n = a.shape, b.shape[1]
    elif dims == NT:
        (m, k), n = a.shape, b.shape[0]
    else:
        (k, m), n = a.shape, b.shape[1]
    tm, tn, tk = _tile(m, tm), _tile(n, tn), _tile(k, tk)
    nk = k // tk
    a_spec = pl.BlockSpec((tk, tm), lambda j, i, q: (q, i)) if dims == TN else pl.BlockSpec((tm, tk), lambda j, i, q: (i, q))
    b_spec = pl.BlockSpec((tn, tk), lambda j, i, q: (j, q)) if dims == NT else pl.BlockSpec((tk, tn), lambda j, i, q: (q, j))
    o_spec = pl.BlockSpec((tm, tn), lambda j, i, q: (i, j))
    has_add = add is not None

    def body(*refs):
        a_ref, b_ref = refs[0], refs[1]
        add_ref = refs[2] if has_add else None
        o_ref, acc_ref = refs[2 + has_add], refs[3 + has_add]
        q = pl.program_id(2)
        part = _dot(a_ref[...], b_ref[...], dims)

        def finish(r):
            if has_add:
                r = r + coef * add_ref[...]
            o_ref[...] = r.astype(out_dtype)

        if nk == 1:
            finish(part)
        else:
            @pl.when(q == 0)
            def _():
                acc_ref[...] = part

            @pl.when(q > 0)
            def _():
                acc_ref[...] += part

            @pl.when(q == nk - 1)
            def _():
                finish(acc_ref[...])

    ins = [a, b] + ([add] if has_add else [])
    in_specs = [a_spec, b_spec] + ([o_spec] if has_add else [])
    return pl.pallas_call(
        body, name=name, grid=(n // tn, m // tm, nk),
        in_specs=in_specs, out_specs=o_spec,
        out_shape=jax.ShapeDtypeStruct((m, n), out_dtype),
        scratch_shapes=[pltpu.VMEM((tm, tn) if nk > 1 else (SUBLANES, LANES), F32)],
        compiler_params=_cparams(("parallel", "parallel", "arbitrary")),
    )(*ins)


def _conv_taps(cur_ref, halo_ref, first):
    x = cur_ref[...]
    tb = x.shape[0]
    halo = jnp.where(first, 0.0, halo_ref[...])
    xc = jnp.concatenate([halo, x], axis=0)
    return [x] + [pltpu.roll(xc, s, 0)[SUBLANES:SUBLANES + tb] for s in range(1, CONV_K)]


def _conv_fwd(projm, conv_w, d, tb):
    t = projm.shape[0]
    heads = d // DN_DK
    hb = tb // SUBLANES

    def body(cur_ref, halo_ref, w_ref, o_ref):
        i, j = pl.program_id(0), pl.program_id(1)
        taps = _conv_taps(cur_ref, halo_ref, i == 0)
        y = taps[0] * w_ref[CONV_K - 1:CONV_K, :]
        for s in range(1, CONV_K):
            y = y + taps[s] * w_ref[CONV_K - 1 - s:CONV_K - s, :]
        act = y * _sigmoid(y)
        scale = jnp.where(j == 0, DN_DK ** -0.5, 1.0)
        for h in range(heads):
            seg = act[:, h * DN_DK:(h + 1) * DN_DK]
            r = lax.rsqrt(jnp.sum(seg * seg, axis=1, keepdims=True) + RMS_EPS) * scale
            o_ref[:, h * DN_DK:(h + 1) * DN_DK] = seg * jnp.where(j < 2, r, 1.0)

    return pl.pallas_call(
        body, name="conv_fwd", grid=(t // tb, 3),
        in_specs=[pl.BlockSpec((tb, d), lambda i, j: (i, j)),
                  pl.BlockSpec((SUBLANES, d), lambda i, j: (jnp.maximum(i * hb - 1, 0), j)),
                  pl.BlockSpec((CONV_K, d), lambda i, j: (0, j))],
        out_specs=pl.BlockSpec((tb, d), lambda i, j: (i, j)),
        out_shape=jax.ShapeDtypeStruct((t, 3 * d), F32),
        compiler_params=_cparams(("parallel", "parallel")),
    )(projm, projm, conv_w)


def _conv_bwd_dy(projm, conv_w, dqkv, d, tb):
    t = projm.shape[0]
    heads = d // DN_DK
    hb = tb // SUBLANES

    def body(cur_ref, halo_ref, w_ref, dout_ref, dy_ref, dw_ref):
        j, i = pl.program_id(0), pl.program_id(1)
        taps = _conv_taps(cur_ref, halo_ref, i == 0)
        y = taps[0] * w_ref[CONV_K - 1:CONV_K, :]
        for s in range(1, CONV_K):
            y = y + taps[s] * w_ref[CONV_K - 1 - s:CONV_K - s, :]
        sg = _sigmoid(y)
        act = y * sg
        dact = sg * (1.0 + y * (1.0 - sg))
        scale = jnp.where(j == 0, DN_DK ** -0.5, 1.0)
        for h in range(heads):
            cols = slice(h * DN_DK, (h + 1) * DN_DK)
            seg = act[:, cols]
            r = lax.rsqrt(jnp.sum(seg * seg, axis=1, keepdims=True) + RMS_EPS)
            nrm = seg * r
            dout = dout_ref[:, cols]
            dn = dout * scale
            ds = jnp.where(j < 2, r * (dn - nrm * jnp.sum(dn * nrm, axis=1, keepdims=True)), dout)
            dy_ref[:, cols] = ds * dact[:, cols]
        dy = dy_ref[...]

        @pl.when(i == 0)
        def _():
            dw_ref[...] = jnp.zeros_like(dw_ref)

        for s in range(CONV_K):
            dw_ref[CONV_K - 1 - s] += _fold8(dy * taps[s])

    return pl.pallas_call(
        body, name="conv_bwd_dy", grid=(3, t // tb),
        in_specs=[pl.BlockSpec((tb, d), lambda j, i: (i, j)),
                  pl.BlockSpec((SUBLANES, d), lambda j, i: (jnp.maximum(i * hb - 1, 0), j)),
                  pl.BlockSpec((CONV_K, d), lambda j, i: (0, j)),
                  pl.BlockSpec((tb, d), lambda j, i: (i, j))],
        out_specs=[pl.BlockSpec((tb, d), lambda j, i: (i, j)),
                   pl.BlockSpec((CONV_K, SUBLANES, d), lambda j, i: (0, 0, j))],
        out_shape=[jax.ShapeDtypeStruct((t, 3 * d), F32),
                   jax.ShapeDtypeStruct((CONV_K, SUBLANES, 3 * d), F32)],
        compiler_params=_cparams(("parallel", "arbitrary")),
    )(projm, projm, conv_w, dqkv)


def _conv_bwd_dx(dy, conv_w, dprojm, d, tb):
    t = dy.shape[0]
    hb = tb // SUBLANES
    last = t // tb - 1

    def body(cur_ref, halo_ref, w_ref, alias_ref, o_ref):
        i = pl.program_id(0)
        cur = cur_ref[...]
        halo = jnp.where(i == last, 0.0, halo_ref[...])
        dc = jnp.concatenate([cur, halo], axis=0)
        acc = cur * w_ref[CONV_K - 1:CONV_K, :]
        for s in range(1, CONV_K):
            acc = acc + pltpu.roll(dc, tb + SUBLANES - s, 0)[:tb] * w_ref[CONV_K - 1 - s:CONV_K - s, :]
        o_ref[...] = acc.astype(o_ref.dtype)

    return pl.pallas_call(
        body, name="conv_bwd_dx", grid=(t // tb, 3),
        in_specs=[pl.BlockSpec((tb, d), lambda i, j: (i, j)),
                  pl.BlockSpec((SUBLANES, d), lambda i, j: (jnp.minimum((i + 1) * hb, t // SUBLANES - 1), j)),
                  pl.BlockSpec((CONV_K, d), lambda i, j: (0, j)),
                  ANY],
        out_specs=pl.BlockSpec((tb, d), lambda i, j: (i, j)),
        out_shape=jax.ShapeDtypeStruct(dprojm.shape, dprojm.dtype),
        input_output_aliases={3: 0},
        compiler_params=_cparams(("parallel", "parallel")),
    )(dy, dy, conv_w, dprojm)


def _beta_g(ba, alog, dtb):
    beta = _sigmoid(ba[:, :LANES])
    xa = ba[:, LANES:] + dtb
    softplus = jnp.maximum(xa, 0.0) + jnp.log(1.0 + jnp.exp(-jnp.abs(xa)))
    ea = jnp.exp(alog)
    return beta, -ea * softplus, ea, _sigmoid(xa)


def _inv_corrections(mats):
    ys = [-a for a in mats]
    ps = [_dot(a, a) for a in mats]
    steps = int(math.log2(CHUNK)) - 1
    for it in range(steps):
        ys = [y + p + _dot(y, p) for y, p in zip(ys, ps)]
        if it < steps - 1:
            ps = [_dot(p, p) for p in ps]
    return ys


def _chunk_masks():
    row = lax.broadcasted_iota(jnp.int32, (CHUNK, CHUNK), 0)
    col = lax.broadcasted_iota(jnp.int32, (CHUNK, CHUNK), 1)
    return row >= col, row > col, row <= col


def _col_of(mat, lane_idx, h):
    return jnp.sum(jnp.where(lane_idx == h, mat, 0.0), axis=1, keepdims=True)


def _row_of(mat, sub_idx, h):
    return jnp.sum(jnp.where(sub_idx == h, mat, 0.0), axis=0, keepdims=True)


def _dn_fwd(qkv, ba, alog, dtb, d):
    t = qkv.shape[0]
    heads = d // DN_DK
    n_chunks = t // CHUNK
    grp = DN_GROUP if n_chunks % DN_GROUP == 0 else 1
    span = grp * CHUNK

    def body(qkv_ref, ba_ref, al_ref, dt_ref, o_ref, s_ref, y_ref, state):
        @pl.when(pl.program_id(0) == 0)
        def _():
            state[...] = jnp.zeros_like(state)

        tril, strict, _ = _chunk_masks()
        beta, g, _, _ = _beta_g(ba_ref[...], al_ref[...], dt_ref[...])
        lane = lax.broadcasted_iota(jnp.int32, (CHUNK, LANES), 1)
        sub = lax.broadcasted_iota(jnp.int32, (LANES, CHUNK), 0)
        rowc = lax.broadcasted_iota(jnp.int32, (CHUNK, 1), 0)
        hs = range(heads)
        units = [(c, h) for c in range(grp) for h in hs]
        un = range(len(units))
        rows = lambda c: slice(c * CHUNK, (c + 1) * CHUNK)
        gc = [_dot(jnp.where(tril, 1.0, 0.0), g[rows(c)], NN, HIGHEST) for c in range(grp)]
        gct = [m.T for m in gc]
        q = [qkv_ref[rows(c), h * DN_DK:(h + 1) * DN_DK] for c, h in units]
        k = [qkv_ref[rows(c), d + h * DN_DK:d + (h + 1) * DN_DK] for c, h in units]
        v = [qkv_ref[rows(c), 2 * d + h * DN_DK:2 * d + (h + 1) * DN_DK] for c, h in units]
        gch = [_col_of(gc[c], lane, h) for c, h in units]
        bh = [_col_of(beta[rows(c)], lane, h) for c, h in units]
        dec = [jnp.where(tril, jnp.exp(gch[n] - _row_of(gct[c], sub, h)), 0.0) for n, (c, h) in enumerate(units)]
        egc = [jnp.exp(gch[n]) for n in un]
        gl = [jnp.sum(jnp.where(rowc == CHUNK - 1, gch[n], 0.0), axis=0, keepdims=True) for n in un]
        kb = [k[n] * bh[n] for n in un]
        a = [jnp.where(strict, _dot(kb[n], k[n], NT) * dec[n], 0.0) for n in un]
        p = [_dot(q[n], k[n], NT) * dec[n] for n in un]
        ycor = _inv_corrections(a)
        rhs = [jnp.concatenate([v[n] * bh[n], kb[n] * egc[n]], axis=1) for n in un]
        sol = [rhs[n] + _dot(ycor[n], rhs[n]) for n in un]
        qg = [q[n] * egc[n] for n in un]
        kd = [k[n] * jnp.exp(gl[n] - gch[n]) for n in un]
        egl = [jnp.exp(gl[n]) for n in un]
        s_cur, s_in, o = [state[h] for h in hs], [], []
        for c in range(grp):
            ns = [c * heads + h for h in hs]
            vn = [sol[n][:, :DN_DK] - _dot(sol[n][:, DN_DK:], s_cur[h]) for h, n in enumerate(ns)]
            o += [_dot(qg[n], s_cur[h]) + _dot(p[n], vn[h]) for h, n in enumerate(ns)]
            s_in += s_cur
            s_cur = [s_cur[h] * egl[n] + _dot(kd[n], vn[h], TN) for h, n in enumerate(ns)]
        for n, (c, h) in enumerate(units):
            o_ref[rows(c), h * DN_DK:(h + 1) * DN_DK] = o[n]
            s_ref[c, h] = s_in[n]
            y_ref[h, rows(c), :] = ycor[n]
        for h in hs:
            state[h] = s_cur[h]

    return pl.pallas_call(
        body, name="dn_fwd", grid=(n_chunks // grp,),
        in_specs=[pl.BlockSpec((span, 3 * d), lambda i: (i, 0)),
                  pl.BlockSpec((span, 2 * LANES), lambda i: (i, 0)),
                  pl.BlockSpec((1, LANES), lambda i: (0, 0)),
                  pl.BlockSpec((1, LANES), lambda i: (0, 0))],
        out_specs=[pl.BlockSpec((span, d), lambda i: (i, 0)),
                   pl.BlockSpec((grp, heads, DN_DK, DN_DK), lambda i: (i, 0, 0, 0)),
                   pl.BlockSpec((heads, span, CHUNK), lambda i: (0, i, 0))],
        out_shape=[jax.ShapeDtypeStruct((t, d), F32),
                   jax.ShapeDtypeStruct((n_chunks, heads, DN_DK, DN_DK), F32),
                   jax.ShapeDtypeStruct((heads, t, CHUNK), F32)],
        scratch_shapes=[pltpu.VMEM((heads, DN_DK, DN_DK), F32)],
        compiler_params=_cparams(("arbitrary",)),
    )(qkv, ba, alog, dtb)


def _dn_bwd(qkv, ba, alog, dtb, dout, states, ycors, d):
    t = qkv.shape[0]
    heads = d // DN_DK
    n_chunks = t // CHUNK
    grp = DN_GROUP if n_chunks % DN_GROUP == 0 else 1
    span = grp * CHUNK
    rev = lambda i: n_chunks // grp - 1 - i

    def body(qkv_ref, ba_ref, al_ref, dt_ref, do_ref, s_ref, y_ref,
             dqkv_ref, dba_ref, dal_ref, ddt_ref, dstate):
        @pl.when(pl.program_id(0) == 0)
        def _():
            dstate[...] = jnp.zeros_like(dstate)
            dal_ref[...] = jnp.zeros_like(dal_ref)
            ddt_ref[...] = jnp.zeros_like(ddt_ref)

        tril, strict, triu = _chunk_masks()
        beta, g, ea, sig_a = _beta_g(ba_ref[...], al_ref[...], dt_ref[...])
        lane = lax.broadcasted_iota(jnp.int32, (CHUNK, LANES), 1)
        sub = lax.broadcasted_iota(jnp.int32, (LANES, CHUNK), 0)
        rowc = lax.broadcasted_iota(jnp.int32, (CHUNK, 1), 0)
        ones = jnp.ones((CHUNK, LANES), F32)
        hs = range(heads)
        units = [(c, h) for c in range(grp) for h in hs]
        un = range(len(units))
        rows = lambda c: slice(c * CHUNK, (c + 1) * CHUNK)
        rsum = lambda x_: jnp.sum(x_, axis=1, keepdims=True)
        gc = [_dot(jnp.where(tril, 1.0, 0.0), g[rows(c)], NN, HIGHEST) for c in range(grp)]
        gct = [m.T for m in gc]
        q = [qkv_ref[rows(c), h * DN_DK:(h + 1) * DN_DK] for c, h in units]
        k = [qkv_ref[rows(c), d + h * DN_DK:d + (h + 1) * DN_DK] for c, h in units]
        v = [qkv_ref[rows(c), 2 * d + h * DN_DK:2 * d + (h + 1) * DN_DK] for c, h in units]
        dout_h = [do_ref[rows(c), h * DN_DK:(h + 1) * DN_DK] for c, h in units]
        s0 = [s_ref[c, h] for c, h in units]
        ycor = [y_ref[h, rows(c), :] for c, h in units]
        gch = [_col_of(gc[c], lane, h) for c, h in units]
        bh = [_col_of(beta[rows(c)], lane, h) for c, h in units]
        dec = [jnp.where(tril, jnp.exp(gch[n] - _row_of(gct[c], sub, h)), 0.0) for n, (c, h) in enumerate(units)]
        egc = [jnp.exp(gch[n]) for n in un]
        gl = [jnp.sum(jnp.where(rowc == CHUNK - 1, gch[n], 0.0), axis=0, keepdims=True) for n in un]
        egl = [jnp.exp(gl[n]) for n in un]
        ekd = [jnp.exp(gl[n] - gch[n]) for n in un]
        kb = [k[n] * bh[n] for n in un]
        kd = [k[n] * ekd[n] for n in un]
        qg = [q[n] * egc[n] for n in un]
        kbg = [kb[n] * egc[n] for n in un]
        a = [jnp.where(strict, _dot(kb[n], k[n], NT) * dec[n], 0.0) for n in un]
        p = [_dot(q[n], k[n], NT) * dec[n] for n in un]
        rhs = [jnp.concatenate([v[n] * bh[n], kbg[n]], axis=1) for n in un]
        sol = [rhs[n] + _dot(ycor[n], rhs[n]) for n in un]
        w = [sol[n][:, DN_DK:] for n in un]
        vn = [sol[n][:, :DN_DK] - _dot(w[n], s0[n]) for n in un]
        dqg = [_dot(dout_h[n], s0[n], NT) for n in un]
        dp = [jnp.where(tril, _dot(dout_h[n], vn[n], NT), 0.0) for n in un]
        pdo = [_dot(p[n], dout_h[n], TN) for n in un]
        qdo = [_dot(qg[n], dout_h[n], TN) for n in un]
        ds_cur = [dstate[h] for h in hs]
        dsn, dvn = [None] * len(units), [None] * len(units)
        for c in reversed(range(grp)):
            for h in hs:
                dsn[c * heads + h] = ds_cur[h]
            for h in hs:
                n = c * heads + h
                dvn[n] = pdo[n] + _dot(kd[n], ds_cur[h])
            ds_cur = [qdo[c * heads + h] + egl[c * heads + h] * ds_cur[h]
                      - _dot(w[c * heads + h], dvn[c * heads + h], TN) for h in hs]
        dkd = [_dot(vn[n], dsn[n], NT) for n in un]
        dw = [-_dot(dvn[n], s0[n], NT) for n in un]
        dgl = [jnp.sum(rsum(dsn[n] * s0[n]), axis=0, keepdims=True) * egl[n] for n in un]
        dsol = [jnp.concatenate([dvn[n], dw[n]], axis=1) for n in un]
        drhs = [dsol[n] + _dot(ycor[n], dsol[n], TN) for n in un]
        dvb = [drhs[n][:, :DN_DK] for n in un]
        dkbg = [drhs[n][:, DN_DK:] for n in un]
        da = [jnp.where(strict, -_dot(drhs[n], sol[n], NT), 0.0) for n in un]
        dma = [da[n] * dec[n] for n in un]
        dmp = [dp[n] * dec[n] for n in un]
        dkb = [_dot(dma[n], k[n]) + dkbg[n] * egc[n] for n in un]
        dq = [_dot(dmp[n], k[n]) + dqg[n] * egc[n] for n in un]
        dk = [_dot(dma[n], kb[n], TN) + _dot(dmp[n], q[n], TN) + dkd[n] * ekd[n] + dkb[n] * bh[n] for n in un]
        e = [da[n] * a[n] + dp[n] * p[n] for n in un]
        colsum = [_dot(e[n], ones, TN, HIGHEST) for n in un]
        tkd = [rsum(dkd[n] * kd[n]) for n in un]
        for n, (c, h) in enumerate(units):
            dqkv_ref[rows(c), h * DN_DK:(h + 1) * DN_DK] = dq[n]
            dqkv_ref[rows(c), d + h * DN_DK:d + (h + 1) * DN_DK] = dk[n]
            dqkv_ref[rows(c), 2 * d + h * DN_DK:2 * d + (h + 1) * DN_DK] = dvb[n] * bh[n]
        for h in hs:
            dstate[h] = ds_cur[h]
        valid = lane < heads
        dal_acc = jnp.zeros((SUBLANES, LANES), F32)
        ddt_acc = jnp.zeros((SUBLANES, LANES), F32)
        for c in range(grp):
            dgc_all = jnp.zeros((CHUNK, LANES), F32)
            dbeta_all = jnp.zeros((CHUNK, LANES), F32)
            for h in hs:
                n = c * heads + h
                dgc = rsum(e[n]) + rsum(dqg[n] * qg[n]) - tkd[n] + rsum(dkbg[n] * kbg[n])
                dgc = dgc + jnp.where(rowc == CHUNK - 1, dgl[n] + jnp.sum(tkd[n], axis=0, keepdims=True), 0.0)
                dgc_all = dgc_all + jnp.where(lane == h, dgc - colsum[n], 0.0)
                dbeta_all = dbeta_all + jnp.where(lane == h, rsum(dkb[n] * k[n]) + rsum(dvb[n] * v[n]), 0.0)
            dg = _dot(jnp.where(triu, 1.0, 0.0), dgc_all, NN, HIGHEST)
            beta_c = beta[rows(c)]
            dbl = jnp.where(valid, dbeta_all * beta_c * (1.0 - beta_c), 0.0)
            dal = jnp.where(valid, -dg * ea * sig_a[rows(c)], 0.0)
            dba_ref[rows(c), :LANES] = dbl.astype(dba_ref.dtype)
            dba_ref[rows(c), LANES:] = dal.astype(dba_ref.dtype)
            dal_acc = dal_acc + _fold8(jnp.where(valid, dg * g[rows(c)], 0.0))
            ddt_acc = ddt_acc + _fold8(dal)
        dal_ref[...] += dal_acc
        ddt_ref[...] += ddt_acc

    return pl.pallas_call(
        body, name="dn_bwd", grid=(n_chunks // grp,),
        in_specs=[pl.BlockSpec((span, 3 * d), lambda i: (rev(i), 0)),
                  pl.BlockSpec((span, 2 * LANES), lambda i: (rev(i), 0)),
                  pl.BlockSpec((1, LANES), lambda i: (0, 0)),
                  pl.BlockSpec((1, LANES), lambda i: (0, 0)),
                  pl.BlockSpec((span, d), lambda i: (rev(i), 0)),
                  pl.BlockSpec((grp, heads, DN_DK, DN_DK), lambda i: (rev(i), 0, 0, 0)),
                  pl.BlockSpec((heads, span, CHUNK), lambda i: (0, rev(i), 0))],
        out_specs=[pl.BlockSpec((span, 3 * d), lambda i: (rev(i), 0)),
                   pl.BlockSpec((span, 2 * LANES), lambda i: (rev(i), 0)),
                   pl.BlockSpec((SUBLANES, LANES), lambda i: (0, 0)),
                   pl.BlockSpec((SUBLANES, LANES), lambda i: (0, 0))],
        out_shape=[jax.ShapeDtypeStruct((t, 3 * d), F32),
                   jax.ShapeDtypeStruct((t, 2 * LANES), ACT),
                   jax.ShapeDtypeStruct((SUBLANES, LANES), F32),
                   jax.ShapeDtypeStruct((SUBLANES, LANES), F32)],
        scratch_shapes=[pltpu.VMEM((heads, DN_DK, DN_DK), F32)],
        compiler_params=_cparams(("arbitrary",)),
    )(qkv, ba, alog, dtb, dout, states, ycors)


def _sgu_mask():
    row = lax.broadcasted_iota(jnp.int32, (SGU_BLOCK, SGU_BLOCK), 0)
    col = lax.broadcasted_iota(jnp.int32, (SGU_BLOCK, SGU_BLOCK), 1)
    sh = int(math.log2(CHUNK))
    return lax.shift_right_logical(row, sh) >= lax.shift_right_logical(col, sh)


def _gate_sgu_fwd(o, projm, onw, lng, lnb, ws, bst, d):
    t = o.shape[0]
    heads, groups = d // DN_DK, d // SGU_GROUP_DIM
    tb = SGU_BLOCK
    row_spec = pl.BlockSpec((1, d), lambda i: (0, 0))

    def body(o_ref, z_ref, u_ref, v_ref, onw_ref, lng_ref, lnb_ref, ws_ref, bst_ref, ya_ref, yb_ref):
        for h in range(heads):
            cols = slice(h * DN_DK, (h + 1) * DN_DK)
            oh, zh = o_ref[:, cols], z_ref[:, cols]
            r = lax.rsqrt(jnp.mean(oh * oh, axis=1, keepdims=True) + RMS_EPS)
            ya_ref[:, cols] = (oh * r * onw_ref[:, cols] * (zh * _sigmoid(zh))).astype(ya_ref.dtype)
        xhat, _ = _ln_hat(_gelu(v_ref[...]))
        vgn = xhat * lng_ref[...] + lnb_ref[...]
        mask = _sgu_mask()
        lane = lax.broadcasted_iota(jnp.int32, (SGU_BLOCK, LANES), 1)
        bst_v = bst_ref[...]
        for gi in range(groups):
            cols = slice(gi * SGU_GROUP_DIM, (gi + 1) * SGU_GROUP_DIM)
            wsg = jnp.where(mask, ws_ref[gi], 0.0)
            sp = _dot(wsg, vgn[:, cols]) + _col_of(bst_v, lane, gi)
            yb_ref[:, cols] = (_gelu(u_ref[:, cols]) * sp).astype(yb_ref.dtype)

    return pl.pallas_call(
        body, name="gate_sgu_fwd", grid=(t // tb,),
        in_specs=[pl.BlockSpec((tb, d), lambda i: (i, 0)),
                  pl.BlockSpec((tb, d), lambda i: (i, 3)),
                  pl.BlockSpec((tb, d), lambda i: (i, 4)),
                  pl.BlockSpec((tb, d), lambda i: (i, 5)),
                  row_spec, row_spec, row_spec,
                  pl.BlockSpec((groups, SGU_BLOCK, SGU_BLOCK), lambda i: (0, 0, 0)),
                  pl.BlockSpec((SGU_BLOCK, LANES), lambda i: (0, 0))],
        out_specs=[pl.BlockSpec((tb, d), lambda i: (i, 0)), pl.BlockSpec((tb, d), lambda i: (i, 0))],
        out_shape=[jax.ShapeDtypeStruct((t, d), ACT), jax.ShapeDtypeStruct((t, d), ACT)],
        compiler_params=_cparams(("parallel",)),
    )(o, projm, projm, projm, onw, lng, lnb, ws, bst)


def _gate_sgu_bwd(dya, dyb, o, projm, onw, lng, lnb, ws, bst, dprojm, d):
    t = o.shape[0]
    heads, groups = d // DN_DK, d // SGU_GROUP_DIM
    tb = SGU_BLOCK
    row_spec = pl.BlockSpec((1, d), lambda i: (0, 0))
    acc_row = pl.BlockSpec((SUBLANES, d), lambda i: (0, 0))

    def body(dya_ref, dyb_ref, o_ref, z_ref, u_ref, v_ref, onw_ref, lng_ref, lnb_ref, ws_ref, bst_ref, alias_ref,
             do_ref, dp_ref, donw_ref, dlng_ref, dlnb_ref, dws_ref, dbst_ref):
        @pl.when(pl.program_id(0) == 0)
        def _():
            for r_ in (donw_ref, dlng_ref, dlnb_ref, dws_ref, dbst_ref):
                r_[...] = jnp.zeros_like(r_)

        donw = jnp.zeros((SUBLANES, DN_DK), F32)
        for h in range(heads):
            cols = slice(h * DN_DK, (h + 1) * DN_DK)
            oh, zh, dyah, wh = o_ref[:, cols], z_ref[:, cols], dya_ref[:, cols], onw_ref[:, cols]
            r = lax.rsqrt(jnp.mean(oh * oh, axis=1, keepdims=True) + RMS_EPS)
            on = oh * r
            sz = _sigmoid(zh)
            silu_z = zh * sz
            don = dyah * wh * silu_z
            dp_ref[:, cols] = (dyah * on * wh * (sz * (1.0 + zh * (1.0 - sz)))).astype(dp_ref.dtype)
            donw = donw + _fold8(dyah * on * silu_z)
            do_ref[:, cols] = r * (don - on * jnp.mean(don * on, axis=1, keepdims=True))
        donw_ref[...] += donw

        vgp, up = v_ref[...], u_ref[...]
        xhat, rstd = _ln_hat(_gelu(vgp))
        lng_v = lng_ref[...]
        vgn = xhat * lng_v + lnb_ref[...]
        ua = _gelu(up)
        mask = _sgu_mask()
        lane = lax.broadcasted_iota(jnp.int32, (SGU_BLOCK, LANES), 1)
        bst_v = bst_ref[...]
        dbst = jnp.zeros((SGU_BLOCK, LANES), F32)
        dvgn_parts, dua_parts = [], []
        for gi in range(groups):
            cols = slice(gi * SGU_GROUP_DIM, (gi + 1) * SGU_GROUP_DIM)
            wsg = jnp.where(mask, ws_ref[gi], 0.0)
            vg_g, dyb_g = vgn[:, cols], dyb_ref[:, cols]
            sp = _dot(wsg, vg_g) + _col_of(bst_v, lane, gi)
            dsp = dyb_g * ua[:, cols]
            dua_parts.append(dyb_g * sp)
            dws_ref[gi] += jnp.where(mask, _dot(dsp, vg_g, NT), 0.0)
            dbst = dbst + jnp.where(lane == gi, jnp.sum(dsp, axis=1, keepdims=True), 0.0)
            dvgn_parts.append(_dot(wsg, dsp, TN))
        dbst_ref[...] += dbst
        dvgn = jnp.concatenate(dvgn_parts, axis=1)
        dua = jnp.concatenate(dua_parts, axis=1)
        dlng_ref[...] += _fold8(dvgn * xhat)
        dlnb_ref[...] += _fold8(dvgn)
        dvga = _ln_bwd(dvgn * lng_v, xhat, rstd)
        dp_ref[:, d:2 * d] = (dua * _gelu_grad(up)).astype(dp_ref.dtype)
        dp_ref[:, 2 * d:] = (dvga * _gelu_grad(vgp)).astype(dp_ref.dtype)

    return pl.pallas_call(
        body, name="gate_sgu_bwd", grid=(t // tb,),
        in_specs=[pl.BlockSpec((tb, d), lambda i: (i, 0)),
                  pl.BlockSpec((tb, d), lambda i: (i, 0)),
                  pl.BlockSpec((tb, d), lambda i: (i, 0)),
                  pl.BlockSpec((tb, d), lambda i: (i, 3)),
                  pl.BlockSpec((tb, d), lambda i: (i, 4)),
                  pl.BlockSpec((tb, d), lambda i: (i, 5)),
                  row_spec, row_spec, row_spec,
                  pl.BlockSpec((groups, SGU_BLOCK, SGU_BLOCK), lambda i: (0, 0, 0)),
                  pl.BlockSpec((SGU_BLOCK, LANES), lambda i: (0, 0)),
                  ANY],
        out_specs=[pl.BlockSpec((tb, d), lambda i: (i, 0)),
                   pl.BlockSpec((tb, 3 * d), lambda i: (i, 1)),
                   pl.BlockSpec((SUBLANES, DN_DK), lambda i: (0, 0)),
                   acc_row, acc_row,
                   pl.BlockSpec((groups, SGU_BLOCK, SGU_BLOCK), lambda i: (0, 0, 0)),
                   pl.BlockSpec((SGU_BLOCK, LANES), lambda i: (0, 0))],
        out_shape=[jax.ShapeDtypeStruct((t, d), F32),
                   jax.ShapeDtypeStruct(dprojm.shape, dprojm.dtype),
                   jax.ShapeDtypeStruct((SUBLANES, DN_DK), F32),
                   jax.ShapeDtypeStruct((SUBLANES, d), F32),
                   jax.ShapeDtypeStruct((SUBLANES, d), F32),
                   jax.ShapeDtypeStruct((groups, SGU_BLOCK, SGU_BLOCK), F32),
                   jax.ShapeDtypeStruct((SGU_BLOCK, LANES), F32)],
        input_output_aliases={11: 1},
        compiler_params=_cparams(("arbitrary",)),
    )(dya, dyb, o, projm, projm, projm, onw, lng, lnb, ws, bst, dprojm)


def _mix_fwd(ya, yb, projm, x, wpa, wpb, wo, g1, b1, d, tb):
    t = x.shape[0]
    blk = pl.BlockSpec((tb, d), lambda i: (i, 0))
    wspec = pl.BlockSpec((d, d), lambda i: (0, 0))
    row_spec = pl.BlockSpec((1, d), lambda i: (0, 0))

    def body(ya_ref, yb_ref, ga_ref, gb_ref, x_ref, wpa_ref, wpb_ref, wo_ref, g_ref, b_ref,
             pa_ref, pb_ref, m_ref, h_ref, x1_ref, x1b_ref):
        pa = _dot(ya_ref[...], wpa_ref[...])
        pb = _dot(yb_ref[...], wpb_ref[...])
        m = _sigmoid(ga_ref[...]) * pa + _sigmoid(gb_ref[...]) * pb
        hres = ALPHA * x_ref[...] + _dot(m, wo_ref[...])
        xhat, _ = _ln_hat(hres)
        x1 = xhat * g_ref[...] + b_ref[...]
        pa_ref[...] = pa
        pb_ref[...] = pb
        m_ref[...] = m.astype(m_ref.dtype)
        h_ref[...] = hres
        x1_ref[...] = x1
        x1b_ref[...] = x1.astype(x1b_ref.dtype)

    f32_out = jax.ShapeDtypeStruct((t, d), F32)
    bf_out = jax.ShapeDtypeStruct((t, d), ACT)
    return pl.pallas_call(
        body, name="mix_fwd", grid=(t // tb,),
        in_specs=[blk, blk, pl.BlockSpec((tb, d), lambda i: (i, 6)), pl.BlockSpec((tb, d), lambda i: (i, 7)),
                  blk, wspec, wspec, wspec, row_spec, row_spec],
        out_specs=[blk] * 6,
        out_shape=[f32_out, f32_out, bf_out, f32_out, f32_out, bf_out],
        compiler_params=_cparams(("parallel",)),
    )(ya, yb, projm, projm, x, wpa, wpb, wo, g1, b1)


def _mix_bwd(dmix, pa, pb, projm, wpa, wpb, wo, d, tb):
    t = dmix.shape[0]
    blk = pl.BlockSpec((tb, d), lambda i: (i, 0))
    wspec = pl.BlockSpec((d, d), lambda i: (0, 0))

    def body(dmix_ref, pa_ref, pb_ref, ga_ref, gb_ref, wpa_ref, wpb_ref, wo_ref,
             dpa_ref, dpb_ref, dya_ref, dyb_ref, dg_ref):
        dm = _dot(dmix_ref[...], wo_ref[...], NT)
        sa, sb = _sigmoid(ga_ref[...]), _sigmoid(gb_ref[...])
        dpa, dpb = dm * sa, dm * sb
        dpa_ref[...] = dpa.astype(dpa_ref.dtype)
        dpb_ref[...] = dpb.astype(dpb_ref.dtype)
        dg_ref[:, :d] = (dm * pa_ref[...] * sa * (1.0 - sa)).astype(dg_ref.dtype)
        dg_ref[:, d:] = (dm * pb_ref[...] * sb * (1.0 - sb)).astype(dg_ref.dtype)
        dya_ref[...] = _dot(dpa, wpa_ref[...], NT)
        dyb_ref[...] = _dot(dpb, wpb_ref[...], NT)

    return pl.pallas_call(
        body, name="mix_bwd", grid=(t // tb,),
        in_specs=[blk, blk, blk, pl.BlockSpec((tb, d), lambda i: (i, 6)), pl.BlockSpec((tb, d), lambda i: (i, 7)),
                  wspec, wspec, wspec],
        out_specs=[blk, blk, blk, blk, pl.BlockSpec((tb, 2 * d), lambda i: (i, 3))],
        out_shape=[jax.ShapeDtypeStruct((t, d), ACT), jax.ShapeDtypeStruct((t, d), ACT),
                   jax.ShapeDtypeStruct((t, d), F32), jax.ShapeDtypeStruct((t, d), F32),
                   jax.ShapeDtypeStruct((t, 8 * d), ACT)],
        compiler_params=_cparams(("parallel",)),
    )(dmix, pa, pb, projm, projm, wpa, wpb, wo)


def _swiglu_fwd(gu, f, tb):
    t = gu.shape[0]

    def body(g_ref, u_ref, a_ref):
        gp = g_ref[...]
        a_ref[...] = (gp * _sigmoid(gp) * u_ref[...]).astype(a_ref.dtype)

    return pl.pallas_call(
        body, name="swiglu_fwd", grid=(t // tb,),
        in_specs=[pl.BlockSpec((tb, f), lambda i: (i, 0)), pl.BlockSpec((tb, f), lambda i: (i, 1))],
        out_specs=pl.BlockSpec((tb, f), lambda i: (i, 0)),
        out_shape=jax.ShapeDtypeStruct((t, f), ACT),
        compiler_params=_cparams(("parallel",)),
    )(gu, gu)


def _swiglu_bwd(da, gu, f, tb):
    t = gu.shape[0]

    def body(da_ref, g_ref, u_ref, dgu_ref):
        gp, da_v = g_ref[...], da_ref[...]
        sg = _sigmoid(gp)
        dgu_ref[:, :f] = (da_v * u_ref[...] * sg * (1.0 + gp * (1.0 - sg))).astype(dgu_ref.dtype)
        dgu_ref[:, f:] = (da_v * gp * sg).astype(dgu_ref.dtype)

    return pl.pallas_call(
        body, name="swiglu_bwd", grid=(t // tb,),
        in_specs=[pl.BlockSpec((tb, f), lambda i: (i, 0)), pl.BlockSpec((tb, f), lambda i: (i, 0)),
                  pl.BlockSpec((tb, f), lambda i: (i, 1))],
        out_specs=pl.BlockSpec((tb, 2 * f), lambda i: (i, 0)),
        out_shape=jax.ShapeDtypeStruct((t, 2 * f), ACT),
        compiler_params=_cparams(("parallel",)),
    )(da, gu, gu)


def _res_ln_fwd(x1, f, g, b, tb):
    t, d = x1.shape
    blk = pl.BlockSpec((tb, d), lambda i: (i, 0))
    row_spec = pl.BlockSpec((1, d), lambda i: (0, 0))

    def body(x_ref, f_ref, g_ref, b_ref, h_ref, y_ref, yb_ref):
        hres = ALPHA * x_ref[...] + f_ref[...]
        xhat, _ = _ln_hat(hres)
        y = xhat * g_ref[...] + b_ref[...]
        h_ref[...] = hres
        y_ref[...] = y
        yb_ref[...] = y.astype(yb_ref.dtype)

    return pl.pallas_call(
        body, name="res_ln_fwd", grid=(t // tb,),
        in_specs=[blk, blk, row_spec, row_spec], out_specs=[blk, blk, blk],
        out_shape=[jax.ShapeDtypeStruct((t, d), F32), jax.ShapeDtypeStruct((t, d), F32),
                   jax.ShapeDtypeStruct((t, d), ACT)],
        compiler_params=_cparams(("parallel",)),
    )(x1, f, g, b)


def _ln_bwd_call(dy, hres, g, tb):
    t, d = dy.shape
    blk = pl.BlockSpec((tb, d), lambda i: (i, 0))
    acc = pl.BlockSpec((SUBLANES, d), lambda i: (0, 0))

    def body(dy_ref, h_ref, g_ref, dh_ref, dhb_ref, dg_ref, db_ref):
        @pl.when(pl.program_id(0) == 0)
        def _():
            dg_ref[...] = jnp.zeros_like(dg_ref)
            db_ref[...] = jnp.zeros_like(db_ref)

        dy_v = dy_ref[...]
        xhat, r = _ln_hat(h_ref[...])
        dh = _ln_bwd(dy_v * g_ref[...], xhat, r)
        dh_ref[...] = dh
        dhb_ref[...] = dh.astype(dhb_ref.dtype)
        dg_ref[...] += _fold8(dy_v * xhat)
        db_ref[...] += _fold8(dy_v)

    return pl.pallas_call(
        body, name="ln_bwd", grid=(t // tb,),
        in_specs=[blk, blk, pl.BlockSpec((1, d), lambda i: (0, 0))],
        out_specs=[blk, blk, acc, acc],
        out_shape=[jax.ShapeDtypeStruct((t, d), F32), jax.ShapeDtypeStruct((t, d), ACT),
                   jax.ShapeDtypeStruct((SUBLANES, d), F32), jax.ShapeDtypeStruct((SUBLANES, d), F32)],
        compiler_params=_cparams(("arbitrary",)),
    )(dy, hres, g)


def _loss_head(y, target, tb):
    t, d = y.shape
    blk = pl.BlockSpec((tb, d), lambda i: (i, 0))

    def body(y_ref, t_ref, dy_ref, l_ref):
        @pl.when(pl.program_id(0) == 0)
        def _():
            l_ref[...] = jnp.zeros_like(l_ref)

        err = y_ref[...] - t_ref[...]
        dy_ref[...] = err * (1.0 / d)
        sq = _fold8(err * err)
        part = sq[:, :LANES]
        for c in range(1, d // LANES):
            part = part + sq[:, c * LANES:(c + 1) * LANES]
        l_ref[...] += part

    return pl.pallas_call(
        body, name="loss_head", grid=(t // tb,),
        in_specs=[blk, blk],
        out_specs=[blk, pl.BlockSpec((SUBLANES, LANES), lambda i: (0, 0))],
        out_shape=[jax.ShapeDtypeStruct((t, d), F32), jax.ShapeDtypeStruct((SUBLANES, LANES), F32)],
        compiler_params=_cparams(("arbitrary",)),
    )(y, target)


def _adamw(w, g, m, v):
    shape = w.shape
    cols = shape[-1]
    w2, g2, m2, v2 = (a.reshape(-1, cols) for a in (w, g, m, v))
    rows = w2.shape[0]
    tr = _tile(rows, 256, SUBLANES)
    blk = pl.BlockSpec((tr, cols), lambda i: (i, 0))

    def body(w_ref, g_ref, m_ref, v_ref, d_ref, nm_ref, nv_ref):
        g_v = g_ref[...]
        nm = ADAM_B1 * m_ref[...] + (1.0 - ADAM_B1) * g_v
        nv = ADAM_B2 * v_ref[...] + (1.0 - ADAM_B2) * (g_v * g_v)
        m_hat = nm / (1.0 - ADAM_B1 ** ADAM_STEP)
        v_hat = nv / (1.0 - ADAM_B2 ** ADAM_STEP)
        d_ref[...] = -ADAM_LR * (m_hat / (jnp.sqrt(v_hat) + ADAM_EPS) + ADAM_WD * w_ref[...])
        nm_ref[...] = nm
        nv_ref[...] = nv

    out = jax.ShapeDtypeStruct((rows, cols), F32)
    res = pl.pallas_call(
        body, name="adamw", grid=(rows // tr,),
        in_specs=[blk] * 4, out_specs=[blk] * 3, out_shape=[out] * 3,
        compiler_params=_cparams(("parallel",)),
    )(w2, g2, m2, v2)
    return tuple(r.reshape(shape) for r in res)


def _place():
    x, y, c = lax.axis_index("x"), lax.axis_index("y"), lax.axis_index("c")
    return x, y, c, [(1 - x, y), (x, 1 - y), (1 - x, 1 - y)]


def _remote(src, dst, send_sems, recv_sems, k, to):
    return pltpu.make_async_remote_copy(src_ref=src, dst_ref=dst, send_sem=send_sems.at[k],
                                        recv_sem=recv_sems.at[k], device_id=to, device_id_type=MESH)


def _all_gather_weights(shards):
    n = len(shards)

    def body(*refs):
        x_refs, out_refs = refs[:n], refs[n:2 * n]
        send_sems, recv_sems = refs[2 * n:]
        x, y, c, chips = _place()
        sibling = (x, y, 1 - c)
        mine = 2 * x + y
        first = [_remote(x_refs[t].at[c], out_refs[t].at[mine, c], send_sems, recv_sems, 6 * t + j, (cx, cy, c))
                 for j, (cx, cy) in enumerate(chips) for t in range(n)]
        for cp in first:
            cp.start()
        passed = []
        for j, (cx, cy) in enumerate(chips):
            for t in range(n):
                theirs = out_refs[t].at[2 * cx + cy, c]
                _remote(theirs, theirs, send_sems, recv_sems, 6 * t + j, (cx, cy, c)).wait_recv()
                fwd = _remote(theirs, theirs, send_sems, recv_sems, 6 * t + 3 + j, sibling)
                fwd.start()
                passed.append(fwd)
        for j, (cx, cy) in enumerate(chips):
            for t in range(n):
                other = out_refs[t].at[2 * cx + cy, 1 - c]
                _remote(other, other, send_sems, recv_sems, 6 * t + 3 + j, sibling).wait_recv()
        for cp in first + passed:
            cp.wait_send()

    return pl.pallas_call(
        body, name="all_gather_weights",
        in_specs=[ANY] * n, out_specs=[ANY] * n,
        out_shape=[jax.ShapeDtypeStruct((N_CHIPS,) + s.shape, s.dtype) for s in shards],
        scratch_shapes=[pltpu.SemaphoreType.DMA((6 * n,)), pltpu.SemaphoreType.DMA((6 * n,))],
    )(*shards)


def _sibling_exchange(grads, small):
    n = len(grads)

    def body(*refs):
        g_refs, small_ref = refs[:n], refs[n]
        land_refs, sland_ref = refs[n + 1:2 * n + 1], refs[2 * n + 1]
        send_sems, recv_sems = refs[2 * n + 2:]
        x, y, c, _ = _place()
        sibling = (x, y, 1 - c)
        cps = [_remote(g_refs[t].at[1 - c], land_refs[t], send_sems, recv_sems, t, sibling) for t in range(n)]
        cps.append(_remote(small_ref, sland_ref, send_sems, recv_sems, n, sibling))
        for cp in cps:
            cp.start()
        for cp in cps:
            cp.wait()

    return pl.pallas_call(
        body, name="grad_sibling_exchange",
        in_specs=[ANY] * (n + 1), out_specs=[ANY] * (n + 1),
        out_shape=[jax.ShapeDtypeStruct(g.shape[1:], g.dtype) for g in grads]
        + [jax.ShapeDtypeStruct(small.shape, small.dtype)],
        scratch_shapes=[pltpu.SemaphoreType.DMA((n + 1,)), pltpu.SemaphoreType.DMA((n + 1,))],
    )(*grads, small)


def _chip_exchange(travel, small):
    n = len(travel)

    def body(*refs):
        t_refs, small_ref = refs[:n], refs[n]
        land_refs, sland_ref = refs[n + 1:2 * n + 1], refs[2 * n + 1]
        send_sems, recv_sems = refs[2 * n + 2:]
        x, y, c, chips = _place()
        mine = 2 * x + y
        cps = []
        for j, (cx, cy) in enumerate(chips):
            to = (cx, cy, c)
            for t in range(n):
                cps.append(_remote(t_refs[t].at[2 * cx + cy], land_refs[t].at[mine], send_sems, recv_sems,
                                   3 * t + j, to))
            cps.append(_remote(small_ref, sland_ref.at[mine], send_sems, recv_sems, 3 * n + j, to))
        for cp in cps:
            cp.start()
        for cp in cps:
            cp.wait()

    return pl.pallas_call(
        body, name="grad_chip_exchange",
        in_specs=[ANY] * (n + 1), out_specs=[ANY] * (n + 1),
        out_shape=[jax.ShapeDtypeStruct(g.shape, g.dtype) for g in travel]
        + [jax.ShapeDtypeStruct((N_CHIPS,) + small.shape, small.dtype)],
        scratch_shapes=[pltpu.SemaphoreType.DMA((3 * n + 3,)), pltpu.SemaphoreType.DMA((3 * n + 3,))],
    )(*travel, small)


def _sibling_merge(reduced):
    n = len(reduced)

    def body(*refs):
        r_refs, out_refs = refs[:n], refs[n:2 * n]
        send_sems, recv_sems = refs[2 * n:]
        x, y, c, _ = _place()
        cps = [_remote(r_refs[t], out_refs[t], send_sems, recv_sems, t, (x, y, 1 - c)) for t in range(n)]
        for cp in cps:
            cp.start()
        for cp in cps:
            cp.wait()

    return pl.pallas_call(
        body, name="grad_sibling_merge",
        in_specs=[ANY] * n, out_specs=[ANY] * n,
        out_shape=[jax.ShapeDtypeStruct(r.shape, r.dtype) for r in reduced],
        scratch_shapes=[pltpu.SemaphoreType.DMA((n,)), pltpu.SemaphoreType.DMA((n,))],
    )(*reduced)


def _pair_sum(place, grad, land):
    _, n, r, c = grad.shape
    tr = _tile(r, 256, SUBLANES)

    def body(place_ref, a_ref, b_ref, travel_ref, own_ref):
        total = a_ref[0, 0] + b_ref[0]
        travel_ref[0] = total.astype(travel_ref.dtype)

        @pl.when(pl.program_id(1) == place_ref[1])
        def _():
            own_ref[...] = total

    return pl.pallas_call(
        body, name="grad_pair_sum",
        grid_spec=pltpu.PrefetchScalarGridSpec(
            num_scalar_prefetch=1, grid=(r // tr, n),
            in_specs=[pl.BlockSpec((1, 1, tr, c), lambda i, s, p: (p[0], s, i, 0)),
                      pl.BlockSpec((1, tr, c), lambda i, s, p: (s, i, 0))],
            out_specs=[pl.BlockSpec((1, tr, c), lambda i, s, p: (s, i, 0)),
                       pl.BlockSpec((tr, c), lambda i, s, p: (i, 0))]),
        out_shape=[jax.ShapeDtypeStruct((n, r, c), BF16), jax.ShapeDtypeStruct((r, c), F32)],
        compiler_params=_cparams(("parallel", "arbitrary")),
    )(place, grad, land)


def _chip_sum(place, own, land, name):
    n, r, c = land.shape
    tr = _tile(r, 256, SUBLANES)

    def body(place_ref, own_ref, land_ref, o_ref):
        mine = place_ref[1]
        acc = jnp.zeros(o_ref.shape, F32)
        for s in range(n):
            acc = acc + jnp.where(mine == s, own_ref[...], land_ref[s].astype(F32))
        o_ref[...] = acc

    return pl.pallas_call(
        body, name=name,
        grid_spec=pltpu.PrefetchScalarGridSpec(
            num_scalar_prefetch=1, grid=(r // tr,),
            in_specs=[pl.BlockSpec((tr, c), lambda i, p: (i, 0)),
                      pl.BlockSpec((n, tr, c), lambda i, p: (0, i, 0))],
            out_specs=pl.BlockSpec((tr, c), lambda i, p: (i, 0))),
        out_shape=jax.ShapeDtypeStruct((r, c), F32),
        compiler_params=_cparams(("parallel",)),
    )(place, own, land)


def _add2(a, b):
    rows = a.shape[0]
    tr = _tile(rows, 256, SUBLANES)
    blk = pl.BlockSpec((tr, PACK_W), lambda i: (i, 0))

    def body(a_ref, b_ref, o_ref):
        o_ref[...] = a_ref[...] + b_ref[...]

    return pl.pallas_call(
        body, name="grad_small_pair_sum", grid=(rows // tr,), in_specs=[blk, blk], out_specs=blk,
        out_shape=jax.ShapeDtypeStruct(a.shape, F32), compiler_params=_cparams(("parallel",)),
    )(a, b)


def _reduce_gradients(place, grads, small):
    *lands, sland = _sibling_exchange(grads, small)
    pairs = [_pair_sum(place, g, land) for g, land in zip(grads, lands)]
    small_chip = _add2(small, sland)
    *lands2, sland2 = _chip_exchange([p[0] for p in pairs], small_chip)
    reduced = [_chip_sum(place, p[1], land2, "grad_chip_sum") for p, land2 in zip(pairs, lands2)]
    small_total = _chip_sum(place, small_chip, sland2, "grad_small_chip_sum")
    first_core = place[0] == 0
    merged = [jnp.stack([jnp.where(first_core, mine, other), jnp.where(first_core, other, mine)])
              for mine, other in zip(reduced, _sibling_merge(reduced))]
    return merged, small_total


_BIG = (("w_in", 2), ("w_pa", 1), ("w_pb", 1), ("w_o", 1), ("w_ffn_gate", 2), ("w_ffn_up", 2),
        ("w_ffn_down", 1))
_SMALL = ("conv_w", "a_log", "dt_bias", "o_norm_w", "sgu_ln_g", "sgu_ln_b", "w_s", "b_s",
          "ln1_g", "ln1_b", "ln2_g", "ln2_b")


def _pad_rows(flat, mult):
    rows = -(-flat.shape[-1] // (PACK_W * mult)) * mult
    pad = rows * PACK_W - flat.shape[-1]
    flat = jnp.pad(flat, [(0, 0)] * (flat.ndim - 1) + [(0, pad)])
    return flat.reshape(flat.shape[:-1] + (rows, PACK_W))


def _unshard(gathered, local, chip, layer, axis):
    parts = [jnp.where(chip == s, local[layer], gathered[s, layer]) for s in range(N_CHIPS)]
    return jnp.concatenate(parts, axis=axis - 1)


def _to_shards(full, axis):
    l, r, c = full.shape
    if axis == 1:
        return full.reshape(l, N_CHIPS, r // N_CHIPS, c)
    return jnp.transpose(full.reshape(l, r, N_CHIPS, c // N_CHIPS), (0, 2, 1, 3))


def _row(v, width=None):
    v = v.reshape(1, -1).astype(F32)
    if width is not None and v.shape[1] < width:
        v = jnp.pad(v, ((0, 0), (0, width - v.shape[1])))
    return v


def _layer_consts(p, l, d):
    heads = d // DN_DK
    return dict(
        alog=_row(p["a_log"][l], LANES), dtb=_row(p["dt_bias"][l], LANES),
        onw=_row(jnp.tile(p["o_norm_w"][l], heads)),
        lng=_row(p["sgu_ln_g"][l]), lnb=_row(p["sgu_ln_b"][l]),
        ws=p["w_s"][l].astype(F32),
        bst=jnp.pad(p["b_s"][l].T, ((0, 0), (0, LANES - p["b_s"].shape[1]))),
        g1=_row(p["ln1_g"][l]), b1=_row(p["ln1_b"][l]), g2=_row(p["ln2_g"][l]), b2=_row(p["ln2_b"][l]))


def _layer_fwd(x, xb, wl, cl, d, f, tb):
    projm = _matmul(xb, wl["wm"], NN, "proj_main")
    ba = _matmul(xb, wl["wba"], NN, "proj_gates")
    qkv = _conv_fwd(projm, wl["conv"], d, _tile(x.shape[0], 2 * tb, SUBLANES))
    o, states, ycors = _dn_fwd(qkv, ba, cl["alog"], cl["dtb"], d)
    ya, yb = _gate_sgu_fwd(o, projm, cl["onw"], cl["lng"], cl["lnb"], cl["ws"], cl["bst"], d)
    pa, pb, m, h1, x1, x1b = _mix_fwd(ya, yb, projm, x, wl["wpa"], wl["wpb"], wl["wo"], cl["g1"], cl["b1"], d, tb)
    gu = _matmul(x1b, wl["wgu"], NN, "ffn_in")
    act = _swiglu_fwd(gu, f, tb)
    ffn = _matmul(act, wl["wd"], NN, "ffn_out")
    h2, x2, x2b = _res_ln_fwd(x1, ffn, cl["g2"], cl["b2"], tb)
    saved = dict(xb=xb, projm=projm, ba=ba, qkv=qkv, o=o, states=states, ycors=ycors, ya=ya, yb=yb,
                 pa=pa, pb=pb, m=m, h1=h1, x1b=x1b, gu=gu, act=act, h2=h2)
    return x2, x2b, saved


def _layer_bwd(dx2, sv, wl, cl, d, f, tb):
    g = {}
    dh2, dh2b, dg2, db2 = _ln_bwd_call(dx2, sv["h2"], cl["g2"], tb)
    g["ln2_g"], g["ln2_b"] = dg2.sum(0), db2.sum(0)
    g["wd"] = _matmul(sv["act"], dh2b, TN, "ffn_out_dw")
    da = _matmul(dh2b, wl["wd"], NT, "ffn_out_dx")
    dgu = _swiglu_bwd(da, sv["gu"], f, tb)
    g["wgu"] = _matmul(sv["x1b"], dgu, TN, "ffn_in_dw")
    dx1 = _matmul(dgu, wl["wgu"], NT, "ffn_in_dx", add=dh2, coef=ALPHA)
    dh1, dh1b, dg1, db1 = _ln_bwd_call(dx1, sv["h1"], cl["g1"], tb)
    g["ln1_g"], g["ln1_b"] = dg1.sum(0), db1.sum(0)
    g["wo"] = _matmul(sv["m"], dh1b, TN, "wo_dw")
    dpa, dpb, dya, dyb, dprojm = _mix_bwd(dh1b, sv["pa"], sv["pb"], sv["projm"], wl["wpa"], wl["wpb"], wl["wo"], d, tb)
    g["wpa"] = _matmul(sv["ya"], dpa, TN, "wpa_dw")
    g["wpb"] = _matmul(sv["yb"], dpb, TN, "wpb_dw")
    do, dprojm, donw, dlng, dlnb, dws, dbst = _gate_sgu_bwd(
        dya, dyb, sv["o"], sv["projm"], cl["onw"], cl["lng"], cl["lnb"], cl["ws"], cl["bst"], dprojm, d)
    heads, groups = d // DN_DK, d // SGU_GROUP_DIM
    g["o_norm_w"], g["sgu_ln_g"], g["sgu_ln_b"] = donw.sum(0), dlng.sum(0), dlnb.sum(0)
    g["w_s"], g["b_s"] = dws, dbst[:, :groups].T
    dqkv, dba, dal, ddt = _dn_bwd(sv["qkv"], sv["ba"], cl["alog"], cl["dtb"], do, sv["states"], sv["ycors"], d)
    g["a_log"], g["dt_bias"] = dal.sum(0)[:heads], ddt.sum(0)[:heads]
    tbc = _tile(dx2.shape[0], 2 * tb, SUBLANES)
    dy, dcw = _conv_bwd_dy(sv["projm"], wl["conv"], dqkv, d, tbc)
    g["conv_w"] = dcw.sum(1)
    dprojm = _conv_bwd_dx(dy, wl["conv"], dprojm, d, tbc)
    g["wm"] = _matmul(sv["xb"], dprojm, TN, "proj_main_dw")
    g["wba"] = _matmul(sv["xb"], dba, TN, "proj_gates_dw")
    dx = _matmul(dba, wl["wba"], NT, "proj_gates_dx", add=dh1, coef=ALPHA)
    dx = _matmul(dprojm, wl["wm"], NT, "proj_main_dx", add=dx)
    return dx, g


def _local_step(x, target, full, small_w):
    t, d = x.shape
    heads = d // DN_DK
    f = full["w_ffn_gate"][0].shape[-1]
    tb = _tile(t, 256, SUBLANES)
    q4 = 4 * d
    w_in = full["w_in"]
    layers, consts = [], []
    for l in range(DEPTH):
        wba = jnp.zeros((d, 2 * LANES), w_in[l].dtype)
        wba = wba.at[:, :heads].set(w_in[l][:, q4:q4 + heads])
        wba = wba.at[:, LANES:LANES + heads].set(w_in[l][:, q4 + heads:q4 + 2 * heads])
        layers.append(dict(
            wm=jnp.concatenate([w_in[l][:, :q4], w_in[l][:, q4 + 2 * heads:]], axis=1), wba=wba,
            conv=full["conv_w"][l], wpa=full["w_pa"][l], wpb=full["w_pb"][l], wo=full["w_o"][l],
            wgu=jnp.concatenate([full["w_ffn_gate"][l], full["w_ffn_up"][l]], axis=1), wd=full["w_ffn_down"][l]))
        consts.append(_layer_consts(small_w, l, d))

    h, hb, saved = x, x.astype(ACT), []
    for l in range(DEPTH):
        h, hb, sv = _layer_fwd(h, hb, layers[l], consts[l], d, f, tb)
        saved.append(sv)
    dy, loss_parts = _loss_head(h, target, tb)
    grads = []
    for l in reversed(range(DEPTH)):
        dy, g = _layer_bwd(dy, saved[l], layers[l], consts[l], d, f, tb)
        grads.append(g)
    grads = grads[::-1]

    stack = lambda k: jnp.stack([g[k] for g in grads])
    wsh, fs = 2 * d + heads // 2, f // N_CHIPS

    def w_in_shards(gm, gba):
        return jnp.stack([gm[:, :wsh],
                          jnp.concatenate([gm[:, wsh:q4], gba[:, :heads]], axis=1),
                          jnp.concatenate([gba[:, LANES:LANES + heads], gm[:, q4:q4 + wsh - heads]], axis=1),
                          gm[:, q4 + wsh - heads:]])

    rows = lambda k: stack(k).reshape(DEPTH, N_CHIPS, -1, grads[0][k].shape[1])
    out = {
        "w_in": jnp.stack([w_in_shards(g["wm"], g["wba"]) for g in grads]),
        "w_pa": rows("wpa"), "w_pb": rows("wpb"), "w_o": rows("wo"), "w_ffn_down": rows("wd"),
        "w_ffn_gate": jnp.stack([jnp.stack([g["wgu"][:, s * fs:(s + 1) * fs] for s in range(N_CHIPS)]) for g in grads]),
        "w_ffn_up": jnp.stack([jnp.stack([g["wgu"][:, f + s * fs:f + (s + 1) * fs] for s in range(N_CHIPS)])
                               for g in grads])}
    for k in _SMALL:
        out[k] = stack(k)
    return loss_parts, dy, out


def kernel(x, w_in, conv_w, a_log, dt_bias, o_norm_w, sgu_ln_g, sgu_ln_b, w_s, b_s, w_pa, w_pb, w_o, ln1_g, ln1_b, w_ffn_gate, w_ffn_up, w_ffn_down, ln2_g, ln2_b, loss_target, m_w_in, m_conv_w, m_a_log, m_dt_bias, m_o_norm_w, m_sgu_ln_g, m_sgu_ln_b, m_w_s, m_b_s, m_w_pa, m_w_pb, m_w_o, m_ln1_g, m_ln1_b, m_w_ffn_gate, m_w_ffn_up, m_w_ffn_down, m_ln2_g, m_ln2_b, v_w_in, v_conv_w, v_a_log, v_dt_bias, v_o_norm_w, v_sgu_ln_g, v_sgu_ln_b, v_w_s, v_b_s, v_w_pa, v_w_pb, v_w_o, v_ln1_g, v_ln1_b, v_w_ffn_gate, v_w_ffn_up, v_w_ffn_down, v_ln2_g, v_ln2_b):
    names = ("w_in", "conv_w", "a_log", "dt_bias", "o_norm_w", "sgu_ln_g", "sgu_ln_b", "w_s", "b_s", "w_pa",
             "w_pb", "w_o", "ln1_g", "ln1_b", "w_ffn_gate", "w_ffn_up", "w_ffn_down", "ln2_g", "ln2_b")
    w = dict(zip(names, (w_in, conv_w, a_log, dt_bias, o_norm_w, sgu_ln_g, sgu_ln_b, w_s, b_s, w_pa, w_pb, w_o,
                         ln1_g, ln1_b, w_ffn_gate, w_ffn_up, w_ffn_down, ln2_g, ln2_b)))
    mom = dict(zip(names, (m_w_in, m_conv_w, m_a_log, m_dt_bias, m_o_norm_w, m_sgu_ln_g, m_sgu_ln_b, m_w_s, m_b_s,
                           m_w_pa, m_w_pb, m_w_o, m_ln1_g, m_ln1_b, m_w_ffn_gate, m_w_ffn_up, m_w_ffn_down,
                           m_ln2_g, m_ln2_b)))
    var = dict(zip(names, (v_w_in, v_conv_w, v_a_log, v_dt_bias, v_o_norm_w, v_sgu_ln_g, v_sgu_ln_b, v_w_s, v_b_s,
                           v_w_pa, v_w_pb, v_w_o, v_ln1_g, v_ln1_b, v_w_ffn_gate, v_w_ffn_up, v_w_ffn_down,
                           v_ln2_g, v_ln2_b)))
    chip = 2 * lax.axis_index("x") + lax.axis_index("y")
    place = jnp.stack([lax.axis_index("c"), chip]).astype(jnp.int32)

    local = [w[k].astype(BF16) for k, _ in _BIG] + [conv_w]
    gathered = _all_gather_weights(local)
    full = {k: [_unshard(gt, lc, chip, l, axis) for l in range(DEPTH)]
            for (k, axis), gt, lc in zip(_BIG, gathered, local)}
    full["conv_w"] = [_unshard(gathered[-1], conv_w, chip, l, 2) for l in range(DEPTH)]

    small_w = {k: w[k] for k in _SMALL if k != "conv_w"}
    loss_parts, grad_x, g = _local_step(x[0], loss_target[0], full, small_w)

    small_sizes = [g[k].size for k in _SMALL]
    small = _pad_rows(jnp.concatenate([g[k].reshape(-1) for k in _SMALL]), SUBLANES)
    reduced, small_total = _reduce_gradients(place, [g[k] for k, _ in _BIG], small)
    grads = {k: r for (k, _), r in zip(_BIG, reduced)}
    small_total, off = small_total.reshape(-1), 0
    for k, n in zip(_SMALL, small_sizes):
        grads[k] = small_total[off:off + n].reshape(g[k].shape)
        off += n
    grads["conv_w"] = lax.dynamic_index_in_dim(_to_shards(grads["conv_w"], 2), chip, 1, keepdims=False)

    delta, new_m, new_v = {}, {}, {}
    for k in [k for k, _ in _BIG] + ["conv_w"]:
        delta[k], new_m[k], new_v[k] = _adamw(w[k], grads[k], mom[k], var[k])
    rep = [k for k in _SMALL if k != "conv_w"]
    pack = lambda dct: _pad_rows(jnp.concatenate([dct[k].reshape(-1) for k in rep]), SUBLANES)
    packed = _adamw(pack(w), pack(grads), pack(mom), pack(var))
    off = 0
    for k in rep:
        n = w[k].size
        for dst, src in zip((delta, new_m, new_v), packed):
            dst[k] = src.reshape(-1)[off:off + n].reshape(w[k].shape)
        off += n

    loss = 0.5 * lax.psum(jnp.sum(loss_parts), ("x", "y", "c")) / x.shape[-1]
    return (loss, grad_x[None], *[grads[k] for k in names], *[delta[k] for k in names],
            *[new_m[k] for k in names], *[new_v[k] for k in names])
```

```python
import math

import jax
import jax.numpy as jnp
from jax import lax
from jax.experimental import pallas as pl
from jax.experimental.pallas import tpu as pltpu

F32 = jnp.float32
BF16 = jnp.bfloat16
MXU_DTYPE = jnp.bfloat16
ACT = jnp.bfloat16
HIGHEST = lax.Precision.HIGHEST

DEPTH = 2
CHUNK = 64
DN_GROUP = 2
SGU_BLOCK = 128
CONV_K = 4
DN_DK = 128
SGU_GROUP_DIM = 128
LN_EPS = 1e-5
RMS_EPS = 1e-6
ALPHA = (2 * DEPTH) ** 0.25
ADAM_LR, ADAM_B1, ADAM_B2, ADAM_EPS, ADAM_WD, ADAM_STEP = 0.001, 0.9, 0.999, 1e-08, 0.01, 10

LANES = 128
SUBLANES = 8
VMEM_LIMIT = 52 * 2 ** 20
PACK_W = 1024
N_CHIPS = 4

NN = ((1,), (0,))
NT = ((1,), (1,))
TN = ((0,), (0,))
MESH = pl.DeviceIdType.MESH
ANY = pl.BlockSpec(memory_space=pl.ANY)


def _dot(a, b, dims=NN, prec=None):
    if prec is None:
        a = a.astype(MXU_DTYPE)
        b = b.astype(MXU_DTYPE)
    return lax.dot_general(a, b, (dims, ((), ())), preferred_element_type=F32, precision=prec)


def _cparams(sem=None):
    return pltpu.CompilerParams(dimension_semantics=sem, vmem_limit_bytes=VMEM_LIMIT)


def _tile(dim, pref, unit=LANES):
    t = (min(pref, dim) // unit) * unit
    while t >= unit:
        if dim % t == 0:
            return t
        t -= unit
    return dim


def _fold8(x):
    r, n = x.shape
    return x.reshape(r // SUBLANES, SUBLANES, n).sum(axis=0)


def _sigmoid(x):
    return 1.0 / (1.0 + jnp.exp(-x))


def _gelu(x):
    return 0.5 * x * (1.0 + lax.erf(x * (2.0 ** -0.5)))


def _gelu_grad(x):
    return 0.5 * (1.0 + lax.erf(x * (2.0 ** -0.5))) + x * jnp.exp(-0.5 * x * x) * (2.0 * math.pi) ** -0.5


def _ln_hat(h):
    mu = jnp.mean(h, axis=-1, keepdims=True)
    xc = h - mu
    var = jnp.mean(xc * xc, axis=-1, keepdims=True)
    r = lax.rsqrt(var + LN_EPS)
    return xc * r, r


def _ln_bwd(dxhat, xhat, r):
    return r * (dxhat - jnp.mean(dxhat, axis=-1, keepdims=True)
                - xhat * jnp.mean(dxhat * xhat, axis=-1, keepdims=True))


MM_TILE = 1536


def _matmul(a, b, dims, name, out_dtype=F32, add=None, coef=1.0, tm=MM_TILE, tn=MM_TILE, tk=MM_TILE):
    if dims == NN:
        (m, k), n = a.shape, b.shape[1]
    elif dims == NT:
        (m, k), n = a.shape, b.shape[0]
    else:
        (k, m), n = a.shape, b.shape[1]
    tm, tn, tk = _tile(m, tm), _tile(n, tn), _tile(k, tk)
    nk = k // tk
    a_spec = pl.BlockSpec((tk, tm), lambda j, i, q: (q, i)) if dims == TN else pl.BlockSpec((tm, tk), lambda j, i, q: (i, q))
    b_spec = pl.BlockSpec((tn, tk), lambda j, i, q: (j, q)) if dims == NT else pl.BlockSpec((tk, tn), lambda j, i, q: (q, j))
    o_spec = pl.BlockSpec((tm, tn), lambda j, i, q: (i, j))
    has_add = add is not None

    def body(*refs):
        a_ref, b_ref = refs[0], refs[1]
        add_ref = refs[2] if has_add else None
        o_ref, acc_ref = refs[2 + has_add], refs[3 + has_add]
        q = pl.program_id(2)
        part = _dot(a_ref[...], b_ref[...], dims)

        def finish(r):
            if has_add:
                r = r + coef * add_ref[...]
            o_ref[...] = r.astype(out_dtype)

        if nk == 1:
            finish(part)
        else:
            @pl.when(q == 0)
            def _():
                acc_ref[...] = part

            @pl.when(q > 0)
            def _():
                acc_ref[...] += part

            @pl.when(q == nk - 1)
            def _():
                finish(acc_ref[...])

    ins = [a, b] + ([add] if has_add else [])
    in_specs = [a_spec, b_spec] + ([o_spec] if has_add else [])
    return pl.pallas_call(
        body, name=name, grid=(n // tn, m // tm, nk),
        in_specs=in_specs, out_specs=o_spec,
        out_shape=jax.ShapeDtypeStruct((m, n), out_dtype),
        scratch_shapes=[pltpu.VMEM((tm, tn) if nk > 1 else (SUBLANES, LANES), F32)],
        compiler_params=_cparams(("parallel", "parallel", "arbitrary")),
    )(*ins)


def _conv_taps(cur_ref, halo_ref, first):
    x = cur_ref[...]
    tb = x.shape[0]
    halo = jnp.where(first, 0.0, halo_ref[...])
    xc = jnp.concatenate([halo, x], axis=0)
    return [x] + [pltpu.roll(xc, s, 0)[SUBLANES:SUBLANES + tb] for s in range(1, CONV_K)]


def _conv_fwd(projm, conv_w, d, tb):
    t = projm.shape[0]
    heads = d // DN_DK
    hb = tb // SUBLANES

    def body(cur_ref, halo_ref, w_ref, o_ref):
        i, j = pl.program_id(0), pl.program_id(1)
        taps = _conv_taps(cur_ref, halo_ref, i == 0)
        y = taps[0] * w_ref[CONV_K - 1:CONV_K, :]
        for s in range(1, CONV_K):
            y = y + taps[s] * w_ref[CONV_K - 1 - s:CONV_K - s, :]
        act = y * _sigmoid(y)
        scale = jnp.where(j == 0, DN_DK ** -0.5, 1.0)
        for h in range(heads):
            seg = act[:, h * DN_DK:(h + 1) * DN_DK]
            r = lax.rsqrt(jnp.sum(seg * seg, axis=1, keepdims=True) + RMS_EPS) * scale
            o_ref[:, h * DN_DK:(h + 1) * DN_DK] = seg * jnp.where(j < 2, r, 1.0)

    return pl.pallas_call(
        body, name="conv_fwd", grid=(t // tb, 3),
        in_specs=[pl.BlockSpec((tb, d), lambda i, j: (i, j)),
                  pl.BlockSpec((SUBLANES, d), lambda i, j: (jnp.maximum(i * hb - 1, 0), j)),
                  pl.BlockSpec((CONV_K, d), lambda i, j: (0, j))],
        out_specs=pl.BlockSpec((tb, d), lambda i, j: (i, j)),
        out_shape=jax.ShapeDtypeStruct((t, 3 * d), F32),
        compiler_params=_cparams(("parallel", "parallel")),
    )(projm, projm, conv_w)


def _conv_bwd_dy(projm, conv_w, dqkv, d, tb):
    t = projm.shape[0]
    heads = d // DN_DK
    hb = tb // SUBLANES

    def body(cur_ref, halo_ref, w_ref, dout_ref, dy_ref, dw_ref):
        j, i = pl.program_id(0), pl.program_id(1)
        taps = _conv_taps(cur_ref, halo_ref, i == 0)
        y = taps[0] * w_ref[CONV_K - 1:CONV_K, :]
        for s in range(1, CONV_K):
            y = y + taps[s] * w_ref[CONV_K - 1 - s:CONV_K - s, :]
        sg = _sigmoid(y)
        act = y * sg
        dact = sg * (1.0 + y * (1.0 - sg))
        scale = jnp.where(j == 0, DN_DK ** -0.5, 1.0)
        for h in range(heads):
            cols = slice(h * DN_DK, (h + 1) * DN_DK)
            seg = act[:, cols]
            r = lax.rsqrt(jnp.sum(seg * seg, axis=1, keepdims=True) + RMS_EPS)
            nrm = seg * r
            dout = dout_ref[:, cols]
            dn = dout * scale
            ds = jnp.where(j < 2, r * (dn - nrm * jnp.sum(dn * nrm, axis=1, keepdims=True)), dout)
            dy_ref[:, cols] = ds * dact[:, cols]
        dy = dy_ref[...]

        @pl.when(i == 0)
        def _():
            dw_ref[...] = jnp.zeros_like(dw_ref)

        for s in range(CONV_K):
            dw_ref[CONV_K - 1 - s] += _fold8(dy * taps[s])

    return pl.pallas_call(
        body, name="conv_bwd_dy", grid=(3, t // tb),
        in_specs=[pl.BlockSpec((tb, d), lambda j, i: (i, j)),
                  pl.BlockSpec((SUBLANES, d), lambda j, i: (jnp.maximum(i * hb - 1, 0), j)),
                  pl.BlockSpec((CONV_K, d), lambda j, i: (0, j)),
                  pl.BlockSpec((tb, d), lambda j, i: (i, j))],
        out_specs=[pl.BlockSpec((tb, d), lambda j, i: (i, j)),
                   pl.BlockSpec((CONV_K, SUBLANES, d), lambda j, i: (0, 0, j))],
        out_shape=[jax.ShapeDtypeStruct((t, 3 * d), F32),
                   jax.ShapeDtypeStruct((CONV_K, SUBLANES, 3 * d), F32)],
        compiler_params=_cparams(("parallel", "arbitrary")),
    )(projm, projm, conv_w, dqkv)


def _conv_bwd_dx(dy, conv_w, dprojm, d, tb):
    t = dy.shape[0]
    hb = tb // SUBLANES
    last = t // tb - 1

    def body(cur_ref, halo_ref, w_ref, alias_ref, o_ref):
        i = pl.program_id(0)
        cur = cur_ref[...]
        halo = jnp.where(i == last, 0.0, halo_ref[...])
        dc = jnp.concatenate([cur, halo], axis=0)
        acc = cur * w_ref[CONV_K - 1:CONV_K, :]
        for s in range(1, CONV_K):
            acc = acc + pltpu.roll(dc, tb + SUBLANES - s, 0)[:tb] * w_ref[CONV_K - 1 - s:CONV_K - s, :]
        o_ref[...] = acc.astype(o_ref.dtype)

    return pl.pallas_call(
        body, name="conv_bwd_dx", grid=(t // tb, 3),
        in_specs=[pl.BlockSpec((tb, d), lambda i, j: (i, j)),
                  pl.BlockSpec((SUBLANES, d), lambda i, j: (jnp.minimum((i + 1) * hb, t // SUBLANES - 1), j)),
                  pl.BlockSpec((CONV_K, d), lambda i, j: (0, j)),
                  ANY],
        out_specs=pl.BlockSpec((tb, d), lambda i, j: (i, j)),
        out_shape=jax.ShapeDtypeStruct(dprojm.shape, dprojm.dtype),
        input_output_aliases={3: 0},
        compiler_params=_cparams(("parallel", "parallel")),
    )(dy, dy, conv_w, dprojm)


def _beta_g(ba, alog, dtb):
    beta = _sigmoid(ba[:, :LANES])
    xa = ba[:, LANES:] + dtb
    softplus = jnp.maximum(xa, 0.0) + jnp.log(1.0 + jnp.exp(-jnp.abs(xa)))
    ea = jnp.exp(alog)
    return beta, -ea * softplus, ea, _sigmoid(xa)


def _inv_corrections(mats):
    ys = [-a for a in mats]
    ps = [_dot(a, a) for a in mats]
    steps = int(math.log2(CHUNK)) - 1
    for it in range(steps):
        ys = [y + p + _dot(y, p) for y, p in zip(ys, ps)]
        if it < steps - 1:
            ps = [_dot(p, p) for p in ps]
    return ys


def _chunk_masks():
    row = lax.broadcasted_iota(jnp.int32, (CHUNK, CHUNK), 0)
    col = lax.broadcasted_iota(jnp.int32, (CHUNK, CHUNK), 1)
    return row >= col, row > col, row <= col


def _col_of(mat, lane_idx, h):
    return jnp.sum(jnp.where(lane_idx == h, mat, 0.0), axis=1, keepdims=True)


def _row_of(mat, sub_idx, h):
    return jnp.sum(jnp.where(sub_idx == h, mat, 0.0), axis=0, keepdims=True)


def _carrying(compute, n_in, n_out, n_scratch, carried, steps):
    if carried is None:
        return compute
    ci, co = len(carried.inputs), len(carried.out_shapes)

    def body(*refs):
        ins, c_in = refs[:n_in], refs[n_in:n_in + ci]
        outs, c_out = refs[n_in + ci:n_in + ci + n_out], refs[n_in + ci + n_out:n_in + ci + n_out + co]
        scratch = refs[n_in + ci + n_out + co:]
        start, finish = carried.copies(c_in, c_out, scratch[n_scratch], scratch[n_scratch + 1])

        @pl.when(pl.program_id(0) == 0)
        def _():
            start()

        compute(*ins, *outs, *scratch[:n_scratch])

        @pl.when(pl.program_id(0) == steps - 1)
        def _():
            finish()

    return body


def _dn_fwd(qkv, ba, alog, dtb, d, carried=None):
    t = qkv.shape[0]
    heads = d // DN_DK
    n_chunks = t // CHUNK
    grp = DN_GROUP if n_chunks % DN_GROUP == 0 else 1
    span = grp * CHUNK
    extra = carried or _Carried([], [], 0, None)

    def compute(qkv_ref, ba_ref, al_ref, dt_ref, o_ref, s_ref, y_ref, state):
        @pl.when(pl.program_id(0) == 0)
        def _():
            state[...] = jnp.zeros_like(state)

        tril, strict, _ = _chunk_masks()
        beta, g, _, _ = _beta_g(ba_ref[...], al_ref[...], dt_ref[...])
        lane = lax.broadcasted_iota(jnp.int32, (CHUNK, LANES), 1)
        sub = lax.broadcasted_iota(jnp.int32, (LANES, CHUNK), 0)
        rowc = lax.broadcasted_iota(jnp.int32, (CHUNK, 1), 0)
        hs = range(heads)
        units = [(c, h) for c in range(grp) for h in hs]
        un = range(len(units))
        rows = lambda c: slice(c * CHUNK, (c + 1) * CHUNK)
        gc = [_dot(jnp.where(tril, 1.0, 0.0), g[rows(c)], NN, HIGHEST) for c in range(grp)]
        gct = [m.T for m in gc]
        q = [qkv_ref[rows(c), h * DN_DK:(h + 1) * DN_DK] for c, h in units]
        k = [qkv_ref[rows(c), d + h * DN_DK:d + (h + 1) * DN_DK] for c, h in units]
        v = [qkv_ref[rows(c), 2 * d + h * DN_DK:2 * d + (h + 1) * DN_DK] for c, h in units]
        gch = [_col_of(gc[c], lane, h) for c, h in units]
        bh = [_col_of(beta[rows(c)], lane, h) for c, h in units]
        dec = [jnp.where(tril, jnp.exp(gch[n] - _row_of(gct[c], sub, h)), 0.0) for n, (c, h) in enumerate(units)]
        egc = [jnp.exp(gch[n]) for n in un]
        gl = [jnp.sum(jnp.where(rowc == CHUNK - 1, gch[n], 0.0), axis=0, keepdims=True) for n in un]
        kb = [k[n] * bh[n] for n in un]
        a = [jnp.where(strict, _dot(kb[n], k[n], NT) * dec[n], 0.0) for n in un]
        p = [_dot(q[n], k[n], NT) * dec[n] for n in un]
        ycor = _inv_corrections(a)
        rhs = [jnp.concatenate([v[n] * bh[n], kb[n] * egc[n]], axis=1) for n in un]
        sol = [rhs[n] + _dot(ycor[n], rhs[n]) for n in un]
        qg = [q[n] * egc[n] for n in un]
        kd = [k[n] * jnp.exp(gl[n] - gch[n]) for n in un]
        egl = [jnp.exp(gl[n]) for n in un]
        s_cur, s_in, o = [state[h] for h in hs], [], []
        for c in range(grp):
            ns = [c * heads + h for h in hs]
            vn = [sol[n][:, :DN_DK] - _dot(sol[n][:, DN_DK:], s_cur[h]) for h, n in enumerate(ns)]
            o += [_dot(qg[n], s_cur[h]) + _dot(p[n], vn[h]) for h, n in enumerate(ns)]
            s_in += s_cur
            s_cur = [s_cur[h] * egl[n] + _dot(kd[n], vn[h], TN) for h, n in enumerate(ns)]
        for n, (c, h) in enumerate(units):
            o_ref[rows(c), h * DN_DK:(h + 1) * DN_DK] = o[n]
            s_ref[c, h] = s_in[n]
            y_ref[h, rows(c), :] = ycor[n]
        for h in hs:
            state[h] = s_cur[h]

    res = pl.pallas_call(
        _carrying(compute, 4, 3, 1, carried, n_chunks // grp),
        name="dn_fwd_carrying" if carried else "dn_fwd", grid=(n_chunks // grp,),
        in_specs=[pl.BlockSpec((span, 3 * d), lambda i: (i, 0)),
                  pl.BlockSpec((span, 2 * LANES), lambda i: (i, 0)),
                  pl.BlockSpec((1, LANES), lambda i: (0, 0)),
                  pl.BlockSpec((1, LANES), lambda i: (0, 0))] + [ANY] * len(extra.inputs),
        out_specs=[pl.BlockSpec((span, d), lambda i: (i, 0)),
                   pl.BlockSpec((grp, heads, DN_DK, DN_DK), lambda i: (i, 0, 0, 0)),
                   pl.BlockSpec((heads, span, CHUNK), lambda i: (0, i, 0))] + [ANY] * len(extra.out_shapes),
        out_shape=[jax.ShapeDtypeStruct((t, d), F32),
                   jax.ShapeDtypeStruct((n_chunks, heads, DN_DK, DN_DK), F32),
                   jax.ShapeDtypeStruct((heads, t, CHUNK), F32)] + extra.out_shapes,
        scratch_shapes=[pltpu.VMEM((heads, DN_DK, DN_DK), F32)] + (extra.scratch() if carried else []),
        compiler_params=_cparams(("arbitrary",)),
    )(qkv, ba, alog, dtb, *extra.inputs)
    return res[:3], res[3:]


def _dn_bwd(qkv, ba, alog, dtb, dout, states, ycors, d, carried=None):
    t = qkv.shape[0]
    heads = d // DN_DK
    n_chunks = t // CHUNK
    grp = DN_GROUP if n_chunks % DN_GROUP == 0 else 1
    span = grp * CHUNK
    rev = lambda i: n_chunks // grp - 1 - i
    extra = carried or _Carried([], [], 0, None)

    def compute(qkv_ref, ba_ref, al_ref, dt_ref, do_ref, s_ref, y_ref,
                dqkv_ref, dba_ref, dal_ref, ddt_ref, dstate):
        @pl.when(pl.program_id(0) == 0)
        def _():
            dstate[...] = jnp.zeros_like(dstate)
            dal_ref[...] = jnp.zeros_like(dal_ref)
            ddt_ref[...] = jnp.zeros_like(ddt_ref)

        tril, strict, triu = _chunk_masks()
        beta, g, ea, sig_a = _beta_g(ba_ref[...], al_ref[...], dt_ref[...])
        lane = lax.broadcasted_iota(jnp.int32, (CHUNK, LANES), 1)
        sub = lax.broadcasted_iota(jnp.int32, (LANES, CHUNK), 0)
        rowc = lax.broadcasted_iota(jnp.int32, (CHUNK, 1), 0)
        ones = jnp.ones((CHUNK, LANES), F32)
        hs = range(heads)
        units = [(c, h) for c in range(grp) for h in hs]
        un = range(len(units))
        rows = lambda c: slice(c * CHUNK, (c + 1) * CHUNK)
        rsum = lambda x_: jnp.sum(x_, axis=1, keepdims=True)
        gc = [_dot(jnp.where(tril, 1.0, 0.0), g[rows(c)], NN, HIGHEST) for c in range(grp)]
        gct = [m.T for m in gc]
        q = [qkv_ref[rows(c), h * DN_DK:(h + 1) * DN_DK] for c, h in units]
        k = [qkv_ref[rows(c), d + h * DN_DK:d + (h + 1) * DN_DK] for c, h in units]
        v = [qkv_ref[rows(c), 2 * d + h * DN_DK:2 * d + (h + 1) * DN_DK] for c, h in units]
        dout_h = [do_ref[rows(c), h * DN_DK:(h + 1) * DN_DK] for c, h in units]
        s0 = [s_ref[c, h] for c, h in units]
        ycor = [y_ref[h, rows(c), :] for c, h in units]
        gch = [_col_of(gc[c], lane, h) for c, h in units]
        bh = [_col_of(beta[rows(c)], lane, h) for c, h in units]
        dec = [jnp.where(tril, jnp.exp(gch[n] - _row_of(gct[c], sub, h)), 0.0) for n, (c, h) in enumerate(units)]
        egc = [jnp.exp(gch[n]) for n in un]
        gl = [jnp.sum(jnp.where(rowc == CHUNK - 1, gch[n], 0.0), axis=0, keepdims=True) for n in un]
        egl = [jnp.exp(gl[n]) for n in un]
        ekd = [jnp.exp(gl[n] - gch[n]) for n in un]
        kb = [k[n] * bh[n] for n in un]
        kd = [k[n] * ekd[n] for n in un]
        qg = [q[n] * egc[n] for n in un]
        kbg = [kb[n] * egc[n] for n in un]
        a = [jnp.where(strict, _dot(kb[n], k[n], NT) * dec[n], 0.0) for n in un]
        p = [_dot(q[n], k[n], NT) * dec[n] for n in un]
        rhs = [jnp.concatenate([v[n] * bh[n], kbg[n]], axis=1) for n in un]
        sol = [rhs[n] + _dot(ycor[n], rhs[n]) for n in un]
        w = [sol[n][:, DN_DK:] for n in un]
        vn = [sol[n][:, :DN_DK] - _dot(w[n], s0[n]) for n in un]
        dqg = [_dot(dout_h[n], s0[n], NT) for n in un]
        dp = [jnp.where(tril, _dot(dout_h[n], vn[n], NT), 0.0) for n in un]
        pdo = [_dot(p[n], dout_h[n], TN) for n in un]
        qdo = [_dot(qg[n], dout_h[n], TN) for n in un]
        ds_cur = [dstate[h] for h in hs]
        dsn, dvn = [None] * len(units), [None] * len(units)
        for c in reversed(range(grp)):
            for h in hs:
                dsn[c * heads + h] = ds_cur[h]
            for h in hs:
                n = c * heads + h
                dvn[n] = pdo[n] + _dot(kd[n], ds_cur[h])
            ds_cur = [qdo[c * heads + h] + egl[c * heads + h] * ds_cur[h]
                      - _dot(w[c * heads + h], dvn[c * heads + h], TN) for h in hs]
        dkd = [_dot(vn[n], dsn[n], NT) for n in un]
        dw = [-_dot(dvn[n], s0[n], NT) for n in un]
        dgl = [jnp.sum(rsum(dsn[n] * s0[n]), axis=0, keepdims=True) * egl[n] for n in un]
        dsol = [jnp.concatenate([dvn[n], dw[n]], axis=1) for n in un]
        drhs = [dsol[n] + _dot(ycor[n], dsol[n], TN) for n in un]
        dvb = [drhs[n][:, :DN_DK] for n in un]
        dkbg = [drhs[n][:, DN_DK:] for n in un]
        da = [jnp.where(strict, -_dot(drhs[n], sol[n], NT), 0.0) for n in un]
        dma = [da[n] * dec[n] for n in un]
        dmp = [dp[n] * dec[n] for n in un]
        dkb = [_dot(dma[n], k[n]) + dkbg[n] * egc[n] for n in un]
        dq = [_dot(dmp[n], k[n]) + dqg[n] * egc[n] for n in un]
        dk = [_dot(dma[n], kb[n], TN) + _dot(dmp[n], q[n], TN) + dkd[n] * ekd[n] + dkb[n] * bh[n] for n in un]
        e = [da[n] * a[n] + dp[n] * p[n] for n in un]
        colsum = [_dot(e[n], ones, TN, HIGHEST) for n in un]
        tkd = [rsum(dkd[n] * kd[n]) for n in un]
        for n, (c, h) in enumerate(units):
            dqkv_ref[rows(c), h * DN_DK:(h + 1) * DN_DK] = dq[n]
            dqkv_ref[rows(c), d + h * DN_DK:d + (h + 1) * DN_DK] = dk[n]
            dqkv_ref[rows(c), 2 * d + h * DN_DK:2 * d + (h + 1) * DN_DK] = dvb[n] * bh[n]
        for h in hs:
            dstate[h] = ds_cur[h]
        valid = lane < heads
        dal_acc = jnp.zeros((SUBLANES, LANES), F32)
        ddt_acc = jnp.zeros((SUBLANES, LANES), F32)
        for c in range(grp):
            dgc_all = jnp.zeros((CHUNK, LANES), F32)
            dbeta_all = jnp.zeros((CHUNK, LANES), F32)
            for h in hs:
                n = c * heads + h
                dgc = rsum(e[n]) + rsum(dqg[n] * qg[n]) - tkd[n] + rsum(dkbg[n] * kbg[n])
                dgc = dgc + jnp.where(rowc == CHUNK - 1, dgl[n] + jnp.sum(tkd[n], axis=0, keepdims=True), 0.0)
                dgc_all = dgc_all + jnp.where(lane == h, dgc - colsum[n], 0.0)
                dbeta_all = dbeta_all + jnp.where(lane == h, rsum(dkb[n] * k[n]) + rsum(dvb[n] * v[n]), 0.0)
            dg = _dot(jnp.where(triu, 1.0, 0.0), dgc_all, NN, HIGHEST)
            beta_c = beta[rows(c)]
            dbl = jnp.where(valid, dbeta_all * beta_c * (1.0 - beta_c), 0.0)
            dal = jnp.where(valid, -dg * ea * sig_a[rows(c)], 0.0)
            dba_ref[rows(c), :LANES] = dbl.astype(dba_ref.dtype)
            dba_ref[rows(c), LANES:] = dal.astype(dba_ref.dtype)
            dal_acc = dal_acc + _fold8(jnp.where(valid, dg * g[rows(c)], 0.0))
            ddt_acc = ddt_acc + _fold8(dal)
        dal_ref[...] += dal_acc
        ddt_ref[...] += ddt_acc

    res = pl.pallas_call(
        _carrying(compute, 7, 4, 1, carried, n_chunks // grp),
        name="dn_bwd_carrying" if carried else "dn_bwd", grid=(n_chunks // grp,),
        in_specs=[pl.BlockSpec((span, 3 * d), lambda i: (rev(i), 0)),
                  pl.BlockSpec((span, 2 * LANES), lambda i: (rev(i), 0)),
                  pl.BlockSpec((1, LANES), lambda i: (0, 0)),
                  pl.BlockSpec((1, LANES), lambda i: (0, 0)),
                  pl.BlockSpec((span, d), lambda i: (rev(i), 0)),
                  pl.BlockSpec((grp, heads, DN_DK, DN_DK), lambda i: (rev(i), 0, 0, 0)),
                  pl.BlockSpec((heads, span, CHUNK), lambda i: (0, rev(i), 0))] + [ANY] * len(extra.inputs),
        out_specs=[pl.BlockSpec((span, 3 * d), lambda i: (rev(i), 0)),
                   pl.BlockSpec((span, 2 * LANES), lambda i: (rev(i), 0)),
                   pl.BlockSpec((SUBLANES, LANES), lambda i: (0, 0)),
                   pl.BlockSpec((SUBLANES, LANES), lambda i: (0, 0))] + [ANY] * len(extra.out_shapes),
        out_shape=[jax.ShapeDtypeStruct((t, 3 * d), F32),
                   jax.ShapeDtypeStruct((t, 2 * LANES), ACT),
                   jax.ShapeDtypeStruct((SUBLANES, LANES), F32),
                   jax.ShapeDtypeStruct((SUBLANES, LANES), F32)] + extra.out_shapes,
        scratch_shapes=[pltpu.VMEM((heads, DN_DK, DN_DK), F32)] + (extra.scratch() if carried else []),
        compiler_params=_cparams(("arbitrary",)),
    )(qkv, ba, alog, dtb, dout, states, ycors, *extra.inputs)
    return res[:4], res[4:]


def _sgu_mask():
    row = lax.broadcasted_iota(jnp.int32, (SGU_BLOCK, SGU_BLOCK), 0)
    col = lax.broadcasted_iota(jnp.int32, (SGU_BLOCK, SGU_BLOCK), 1)
    sh = int(math.log2(CHUNK))
    return lax.shift_right_logical(row, sh) >= lax.shift_right_logical(col, sh)


def _gate_sgu_fwd(o, projm, onw, lng, lnb, ws, bst, d):
    t = o.shape[0]
    heads, groups = d // DN_DK, d // SGU_GROUP_DIM
    tb = SGU_BLOCK
    row_spec = pl.BlockSpec((1, d), lambda i: (0, 0))

    def body(o_ref, z_ref, u_ref, v_ref, onw_ref, lng_ref, lnb_ref, ws_ref, bst_ref, ya_ref, yb_ref):
        for h in range(heads):
            cols = slice(h * DN_DK, (h + 1) * DN_DK)
            oh, zh = o_ref[:, cols], z_ref[:, cols]
            r = lax.rsqrt(jnp.mean(oh * oh, axis=1, keepdims=True) + RMS_EPS)
            ya_ref[:, cols] = (oh * r * onw_ref[:, cols] * (zh * _sigmoid(zh))).astype(ya_ref.dtype)
        xhat, _ = _ln_hat(_gelu(v_ref[...]))
        vgn = xhat * lng_ref[...] + lnb_ref[...]
        mask = _sgu_mask()
        lane = lax.broadcasted_iota(jnp.int32, (SGU_BLOCK, LANES), 1)
        bst_v = bst_ref[...]
        for gi in range(groups):
            cols = slice(gi * SGU_GROUP_DIM, (gi + 1) * SGU_GROUP_DIM)
            wsg = jnp.where(mask, ws_ref[gi], 0.0)
            sp = _dot(wsg, vgn[:, cols]) + _col_of(bst_v, lane, gi)
            yb_ref[:, cols] = (_gelu(u_ref[:, cols]) * sp).astype(yb_ref.dtype)

    return pl.pallas_call(
        body, name="gate_sgu_fwd", grid=(t // tb,),
        in_specs=[pl.BlockSpec((tb, d), lambda i: (i, 0)),
                  pl.BlockSpec((tb, d), lambda i: (i, 3)),
                  pl.BlockSpec((tb, d), lambda i: (i, 4)),
                  pl.BlockSpec((tb, d), lambda i: (i, 5)),
                  row_spec, row_spec, row_spec,
                  pl.BlockSpec((groups, SGU_BLOCK, SGU_BLOCK), lambda i: (0, 0, 0)),
                  pl.BlockSpec((SGU_BLOCK, LANES), lambda i: (0, 0))],
        out_specs=[pl.BlockSpec((tb, d), lambda i: (i, 0)), pl.BlockSpec((tb, d), lambda i: (i, 0))],
        out_shape=[jax.ShapeDtypeStruct((t, d), ACT), jax.ShapeDtypeStruct((t, d), ACT)],
        compiler_params=_cparams(("parallel",)),
    )(o, projm, projm, projm, onw, lng, lnb, ws, bst)


def _gate_sgu_bwd(dya, dyb, o, projm, onw, lng, lnb, ws, bst, dprojm, d):
    t = o.shape[0]
    heads, groups = d // DN_DK, d // SGU_GROUP_DIM
    tb = SGU_BLOCK
    row_spec = pl.BlockSpec((1, d), lambda i: (0, 0))
    acc_row = pl.BlockSpec((SUBLANES, d), lambda i: (0, 0))

    def body(dya_ref, dyb_ref, o_ref, z_ref, u_ref, v_ref, onw_ref, lng_ref, lnb_ref, ws_ref, bst_ref, alias_ref,
             do_ref, dp_ref, donw_ref, dlng_ref, dlnb_ref, dws_ref, dbst_ref):
        @pl.when(pl.program_id(0) == 0)
        def _():
            for r_ in (donw_ref, dlng_ref, dlnb_ref, dws_ref, dbst_ref):
                r_[...] = jnp.zeros_like(r_)

        donw = jnp.zeros((SUBLANES, DN_DK), F32)
        for h in range(heads):
            cols = slice(h * DN_DK, (h + 1) * DN_DK)
            oh, zh, dyah, wh = o_ref[:, cols], z_ref[:, cols], dya_ref[:, cols], onw_ref[:, cols]
            r = lax.rsqrt(jnp.mean(oh * oh, axis=1, keepdims=True) + RMS_EPS)
            on = oh * r
            sz = _sigmoid(zh)
            silu_z = zh * sz
            don = dyah * wh * silu_z
            dp_ref[:, cols] = (dyah * on * wh * (sz * (1.0 + zh * (1.0 - sz)))).astype(dp_ref.dtype)
            donw = donw + _fold8(dyah * on * silu_z)
            do_ref[:, cols] = r * (don - on * jnp.mean(don * on, axis=1, keepdims=True))
        donw_ref[...] += donw

        vgp, up = v_ref[...], u_ref[...]
        xhat, rstd = _ln_hat(_gelu(vgp))
        lng_v = lng_ref[...]
        vgn = xhat * lng_v + lnb_ref[...]
        ua = _gelu(up)
        mask = _sgu_mask()
        lane = lax.broadcasted_iota(jnp.int32, (SGU_BLOCK, LANES), 1)
        bst_v = bst_ref[...]
        dbst = jnp.zeros((SGU_BLOCK, LANES), F32)
        dvgn_parts, dua_parts = [], []
        for gi in range(groups):
            cols = slice(gi * SGU_GROUP_DIM, (gi + 1) * SGU_GROUP_DIM)
            wsg = jnp.where(mask, ws_ref[gi], 0.0)
            vg_g, dyb_g = vgn[:, cols], dyb_ref[:, cols]
            sp = _dot(wsg, vg_g) + _col_of(bst_v, lane, gi)
            dsp = dyb_g * ua[:, cols]
            dua_parts.append(dyb_g * sp)
            dws_ref[gi] += jnp.where(mask, _dot(dsp, vg_g, NT), 0.0)
            dbst = dbst + jnp.where(lane == gi, jnp.sum(dsp, axis=1, keepdims=True), 0.0)
            dvgn_parts.append(_dot(wsg, dsp, TN))
        dbst_ref[...] += dbst
        dvgn = jnp.concatenate(dvgn_parts, axis=1)
        dua = jnp.concatenate(dua_parts, axis=1)
        dlng_ref[...] += _fold8(dvgn * xhat)
        dlnb_ref[...] += _fold8(dvgn)
        dvga = _ln_bwd(dvgn * lng_v, xhat, rstd)
        dp_ref[:, d:2 * d] = (dua * _gelu_grad(up)).astype(dp_ref.dtype)
        dp_ref[:, 2 * d:] = (dvga * _gelu_grad(vgp)).astype(dp_ref.dtype)

    return pl.pallas_call(
        body, name="gate_sgu_bwd", grid=(t // tb,),
        in_specs=[pl.BlockSpec((tb, d), lambda i: (i, 0)),
                  pl.BlockSpec((tb, d), lambda i: (i, 0)),
                  pl.BlockSpec((tb, d), lambda i: (i, 0)),
                  pl.BlockSpec((tb, d), lambda i: (i, 3)),
                  pl.BlockSpec((tb, d), lambda i: (i, 4)),
                  pl.BlockSpec((tb, d), lambda i: (i, 5)),
                  row_spec, row_spec, row_spec,
                  pl.BlockSpec((groups, SGU_BLOCK, SGU_BLOCK), lambda i: (0, 0, 0)),
                  pl.BlockSpec((SGU_BLOCK, LANES), lambda i: (0, 0)),
                  ANY],
        out_specs=[pl.BlockSpec((tb, d), lambda i: (i, 0)),
                   pl.BlockSpec((tb, 3 * d), lambda i: (i, 1)),
                   pl.BlockSpec((SUBLANES, DN_DK), lambda i: (0, 0)),
                   acc_row, acc_row,
                   pl.BlockSpec((groups, SGU_BLOCK, SGU_BLOCK), lambda i: (0, 0, 0)),
                   pl.BlockSpec((SGU_BLOCK, LANES), lambda i: (0, 0))],
        out_shape=[jax.ShapeDtypeStruct((t, d), F32),
                   jax.ShapeDtypeStruct(dprojm.shape, dprojm.dtype),
                   jax.ShapeDtypeStruct((SUBLANES, DN_DK), F32),
                   jax.ShapeDtypeStruct((SUBLANES, d), F32),
                   jax.ShapeDtypeStruct((SUBLANES, d), F32),
                   jax.ShapeDtypeStruct((groups, SGU_BLOCK, SGU_BLOCK), F32),
                   jax.ShapeDtypeStruct((SGU_BLOCK, LANES), F32)],
        input_output_aliases={11: 1},
        compiler_params=_cparams(("arbitrary",)),
    )(dya, dyb, o, projm, projm, projm, onw, lng, lnb, ws, bst, dprojm)


def _mix_fwd(ya, yb, projm, x, wpa, wpb, wo, g1, b1, d, tb):
    t = x.shape[0]
    blk = pl.BlockSpec((tb, d), lambda i: (i, 0))
    wspec = pl.BlockSpec((d, d), lambda i: (0, 0))
    row_spec = pl.BlockSpec((1, d), lambda i: (0, 0))

    def body(ya_ref, yb_ref, ga_ref, gb_ref, x_ref, wpa_ref, wpb_ref, wo_ref, g_ref, b_ref,
             pa_ref, pb_ref, m_ref, h_ref, x1_ref, x1b_ref):
        pa = _dot(ya_ref[...], wpa_ref[...])
        pb = _dot(yb_ref[...], wpb_ref[...])
        m = _sigmoid(ga_ref[...]) * pa + _sigmoid(gb_ref[...]) * pb
        hres = ALPHA * x_ref[...] + _dot(m, wo_ref[...])
        xhat, _ = _ln_hat(hres)
        x1 = xhat * g_ref[...] + b_ref[...]
        pa_ref[...] = pa
        pb_ref[...] = pb
        m_ref[...] = m.astype(m_ref.dtype)
        h_ref[...] = hres
        x1_ref[...] = x1
        x1b_ref[...] = x1.astype(x1b_ref.dtype)

    f32_out = jax.ShapeDtypeStruct((t, d), F32)
    bf_out = jax.ShapeDtypeStruct((t, d), ACT)
    return pl.pallas_call(
        body, name="mix_fwd", grid=(t // tb,),
        in_specs=[blk, blk, pl.BlockSpec((tb, d), lambda i: (i, 6)), pl.BlockSpec((tb, d), lambda i: (i, 7)),
                  blk, wspec, wspec, wspec, row_spec, row_spec],
        out_specs=[blk] * 6,
        out_shape=[f32_out, f32_out, bf_out, f32_out, f32_out, bf_out],
        compiler_params=_cparams(("parallel",)),
    )(ya, yb, projm, projm, x, wpa, wpb, wo, g1, b1)


def _mix_bwd(dmix, pa, pb, projm, wpa, wpb, wo, d, tb):
    t = dmix.shape[0]
    blk = pl.BlockSpec((tb, d), lambda i: (i, 0))
    wspec = pl.BlockSpec((d, d), lambda i: (0, 0))

    def body(dmix_ref, pa_ref, pb_ref, ga_ref, gb_ref, wpa_ref, wpb_ref, wo_ref,
             dpa_ref, dpb_ref, dya_ref, dyb_ref, dg_ref):
        dm = _dot(dmix_ref[...], wo_ref[...], NT)
        sa, sb = _sigmoid(ga_ref[...]), _sigmoid(gb_ref[...])
        dpa, dpb = dm * sa, dm * sb
        dpa_ref[...] = dpa.astype(dpa_ref.dtype)
        dpb_ref[...] = dpb.astype(dpb_ref.dtype)
        dg_ref[:, :d] = (dm * pa_ref[...] * sa * (1.0 - sa)).astype(dg_ref.dtype)
        dg_ref[:, d:] = (dm * pb_ref[...] * sb * (1.0 - sb)).astype(dg_ref.dtype)
        dya_ref[...] = _dot(dpa, wpa_ref[...], NT)
        dyb_ref[...] = _dot(dpb, wpb_ref[...], NT)

    return pl.pallas_call(
        body, name="mix_bwd", grid=(t // tb,),
        in_specs=[blk, blk, blk, pl.BlockSpec((tb, d), lambda i: (i, 6)), pl.BlockSpec((tb, d), lambda i: (i, 7)),
                  wspec, wspec, wspec],
        out_specs=[blk, blk, blk, blk, pl.BlockSpec((tb, 2 * d), lambda i: (i, 3))],
        out_shape=[jax.ShapeDtypeStruct((t, d), ACT), jax.ShapeDtypeStruct((t, d), ACT),
                   jax.ShapeDtypeStruct((t, d), F32), jax.ShapeDtypeStruct((t, d), F32),
                   jax.ShapeDtypeStruct((t, 8 * d), ACT)],
        compiler_params=_cparams(("parallel",)),
    )(dmix, pa, pb, projm, projm, wpa, wpb, wo)


def _swiglu_fwd(gu, f, tb):
    t = gu.shape[0]

    def body(g_ref, u_ref, a_ref):
        gp = g_ref[...]
        a_ref[...] = (gp * _sigmoid(gp) * u_ref[...]).astype(a_ref.dtype)

    return pl.pallas_call(
        body, name="swiglu_fwd", grid=(t // tb,),
        in_specs=[pl.BlockSpec((tb, f), lambda i: (i, 0)), pl.BlockSpec((tb, f), lambda i: (i, 1))],
        out_specs=pl.BlockSpec((tb, f), lambda i: (i, 0)),
        out_shape=jax.ShapeDtypeStruct((t, f), ACT),
        compiler_params=_cparams(("parallel",)),
    )(gu, gu)


def _swiglu_bwd(da, gu, f, tb):
    t = gu.shape[0]

    def body(da_ref, g_ref, u_ref, dgu_ref):
        gp, da_v = g_ref[...], da_ref[...]
        sg = _sigmoid(gp)
        dgu_ref[:, :f] = (da_v * u_ref[...] * sg * (1.0 + gp * (1.0 - sg))).astype(dgu_ref.dtype)
        dgu_ref[:, f:] = (da_v * gp * sg).astype(dgu_ref.dtype)

    return pl.pallas_call(
        body, name="swiglu_bwd", grid=(t // tb,),
        in_specs=[pl.BlockSpec((tb, f), lambda i: (i, 0)), pl.BlockSpec((tb, f), lambda i: (i, 0)),
                  pl.BlockSpec((tb, f), lambda i: (i, 1))],
        out_specs=pl.BlockSpec((tb, 2 * f), lambda i: (i, 0)),
        out_shape=jax.ShapeDtypeStruct((t, 2 * f), ACT),
        compiler_params=_cparams(("parallel",)),
    )(da, gu, gu)


def _res_ln_fwd(x1, f, g, b, tb):
    t, d = x1.shape
    blk = pl.BlockSpec((tb, d), lambda i: (i, 0))
    row_spec = pl.BlockSpec((1, d), lambda i: (0, 0))

    def body(x_ref, f_ref, g_ref, b_ref, h_ref, y_ref, yb_ref):
        hres = ALPHA * x_ref[...] + f_ref[...]
        xhat, _ = _ln_hat(hres)
        y = xhat * g_ref[...] + b_ref[...]
        h_ref[...] = hres
        y_ref[...] = y
        yb_ref[...] = y.astype(yb_ref.dtype)

    return pl.pallas_call(
        body, name="res_ln_fwd", grid=(t // tb,),
        in_specs=[blk, blk, row_spec, row_spec], out_specs=[blk, blk, blk],
        out_shape=[jax.ShapeDtypeStruct((t, d), F32), jax.ShapeDtypeStruct((t, d), F32),
                   jax.ShapeDtypeStruct((t, d), ACT)],
        compiler_params=_cparams(("parallel",)),
    )(x1, f, g, b)


def _ln_bwd_call(dy, hres, g, tb):
    t, d = dy.shape
    blk = pl.BlockSpec((tb, d), lambda i: (i, 0))
    acc = pl.BlockSpec((SUBLANES, d), lambda i: (0, 0))

    def body(dy_ref, h_ref, g_ref, dh_ref, dhb_ref, dg_ref, db_ref):
        @pl.when(pl.program_id(0) == 0)
        def _():
            dg_ref[...] = jnp.zeros_like(dg_ref)
            db_ref[...] = jnp.zeros_like(db_ref)

        dy_v = dy_ref[...]
        xhat, r = _ln_hat(h_ref[...])
        dh = _ln_bwd(dy_v * g_ref[...], xhat, r)
        dh_ref[...] = dh
        dhb_ref[...] = dh.astype(dhb_ref.dtype)
        dg_ref[...] += _fold8(dy_v * xhat)
        db_ref[...] += _fold8(dy_v)

    return pl.pallas_call(
        body, name="ln_bwd", grid=(t // tb,),
        in_specs=[blk, blk, pl.BlockSpec((1, d), lambda i: (0, 0))],
        out_specs=[blk, blk, acc, acc],
        out_shape=[jax.ShapeDtypeStruct((t, d), F32), jax.ShapeDtypeStruct((t, d), ACT),
                   jax.ShapeDtypeStruct((SUBLANES, d), F32), jax.ShapeDtypeStruct((SUBLANES, d), F32)],
        compiler_params=_cparams(("arbitrary",)),
    )(dy, hres, g)


def _loss_head(y, target, tb):
    t, d = y.shape
    blk = pl.BlockSpec((tb, d), lambda i: (i, 0))

    def body(y_ref, t_ref, dy_ref, l_ref):
        @pl.when(pl.program_id(0) == 0)
        def _():
            l_ref[...] = jnp.zeros_like(l_ref)

        err = y_ref[...] - t_ref[...]
        dy_ref[...] = err * (1.0 / d)
        sq = _fold8(err * err)
        part = sq[:, :LANES]
        for c in range(1, d // LANES):
            part = part + sq[:, c * LANES:(c + 1) * LANES]
        l_ref[...] += part

    return pl.pallas_call(
        body, name="loss_head", grid=(t // tb,),
        in_specs=[blk, blk],
        out_specs=[blk, pl.BlockSpec((SUBLANES, LANES), lambda i: (0, 0))],
        out_shape=[jax.ShapeDtypeStruct((t, d), F32), jax.ShapeDtypeStruct((SUBLANES, LANES), F32)],
        compiler_params=_cparams(("arbitrary",)),
    )(y, target)


def _adamw(w, g, m, v):
    shape = w.shape
    cols = shape[-1]
    w2, g2, m2, v2 = (a.reshape(-1, cols) for a in (w, g, m, v))
    rows = w2.shape[0]
    tr = _tile(rows, 256, SUBLANES)
    blk = pl.BlockSpec((tr, cols), lambda i: (i, 0))

    def body(w_ref, g_ref, m_ref, v_ref, d_ref, nm_ref, nv_ref):
        g_v = g_ref[...]
        nm = ADAM_B1 * m_ref[...] + (1.0 - ADAM_B1) * g_v
        nv = ADAM_B2 * v_ref[...] + (1.0 - ADAM_B2) * (g_v * g_v)
        m_hat = nm / (1.0 - ADAM_B1 ** ADAM_STEP)
        v_hat = nv / (1.0 - ADAM_B2 ** ADAM_STEP)
        d_ref[...] = -ADAM_LR * (m_hat / (jnp.sqrt(v_hat) + ADAM_EPS) + ADAM_WD * w_ref[...])
        nm_ref[...] = nm
        nv_ref[...] = nv

    out = jax.ShapeDtypeStruct((rows, cols), F32)
    res = pl.pallas_call(
        body, name="adamw", grid=(rows // tr,),
        in_specs=[blk] * 4, out_specs=[blk] * 3, out_shape=[out] * 3,
        compiler_params=_cparams(("parallel",)),
    )(w2, g2, m2, v2)
    return tuple(r.reshape(shape) for r in res)


def _place():
    x, y, c = lax.axis_index("x"), lax.axis_index("y"), lax.axis_index("c")
    return x, y, c, [(1 - x, y), (x, 1 - y), (1 - x, 1 - y)]


def _remote(src, dst, send_sems, recv_sems, k, to):
    return pltpu.make_async_remote_copy(src_ref=src, dst_ref=dst, send_sem=send_sems.at[k],
                                        recv_sem=recv_sems.at[k], device_id=to, device_id_type=MESH)


class _Carried:
    def __init__(self, inputs, out_shapes, n_sems, copies):
        self.inputs, self.out_shapes, self.n_sems, self.copies = list(inputs), list(out_shapes), n_sems, copies

    def scratch(self):
        return [pltpu.SemaphoreType.DMA((self.n_sems,)), pltpu.SemaphoreType.DMA((self.n_sems,))]


def _run_comm(name, plan):
    n_in, n_out = len(plan.inputs), len(plan.out_shapes)

    def body(*refs):
        start, finish = plan.copies(refs[:n_in], refs[n_in:n_in + n_out], refs[-2], refs[-1])
        start()
        finish()

    return pl.pallas_call(
        body, name=name, in_specs=[ANY] * n_in, out_specs=[ANY] * n_out, out_shape=plan.out_shapes,
        scratch_shapes=plan.scratch(),
    )(*plan.inputs)


def _half_rows(rows, core):
    if rows % (4 * SUBLANES):
        return None
    return pl.ds(pl.multiple_of(core * (rows // 2), 2 * SUBLANES), rows // 2)


def _all_gather_plan(shards):
    n = len(shards)

    def copies(x_refs, out_refs, send_sems, recv_sems):
        x, y, c, chips = _place()
        sibling = (x, y, 1 - c)
        mine = 2 * x + y
        split = [_half_rows(x_refs[t].shape[0], c) is not None for t in range(n)]

        def src(t):
            return x_refs[t].at[_half_rows(x_refs[t].shape[0], c)] if split[t] else x_refs[t]

        def slot(t, chip_idx, core):
            rows = _half_rows(x_refs[t].shape[0], core)
            return out_refs[t].at[chip_idx, rows] if split[t] else out_refs[t].at[chip_idx]

        def first():
            return [_remote(src(t), slot(t, mine, c), send_sems, recv_sems, 6 * t + j, (cx, cy, c))
                    for j, (cx, cy) in enumerate(chips) for t in range(n)]

        def start():
            for cp in first():
                cp.start()

        def finish():
            passed = []
            for j, (cx, cy) in enumerate(chips):
                for t in range(n):
                    theirs = slot(t, 2 * cx + cy, c)
                    _remote(theirs, theirs, send_sems, recv_sems, 6 * t + j, (cx, cy, c)).wait_recv()
                    if split[t]:
                        fwd = _remote(theirs, theirs, send_sems, recv_sems, 6 * t + 3 + j, sibling)
                        fwd.start()
                        passed.append(fwd)
            for j, (cx, cy) in enumerate(chips):
                for t in range(n):
                    if split[t]:
                        other = slot(t, 2 * cx + cy, 1 - c)
                        _remote(other, other, send_sems, recv_sems, 6 * t + 3 + j, sibling).wait_recv()
            for cp in first() + passed:
                cp.wait_send()

        return start, finish

    return _Carried(shards, [jax.ShapeDtypeStruct((N_CHIPS,) + s.shape, s.dtype) for s in shards], 6 * n, copies)


def _sibling_exchange_plan(grads, small=None):
    n = len(grads)
    extra = [] if small is None else [small]

    def copies(in_refs, out_refs, send_sems, recv_sems):
        x, y, c, _ = _place()
        sibling = (x, y, 1 - c)

        def all_copies():
            cps = [_remote(in_refs[t].at[:, _half_rows(in_refs[t].shape[1], 1 - c), :], out_refs[t],
                           send_sems, recv_sems, t, sibling) for t in range(n)]
            if extra:
                cps.append(_remote(in_refs[n], out_refs[n], send_sems, recv_sems, n, sibling))
            return cps

        def start():
            for cp in all_copies():
                cp.start()

        def finish():
            for cp in all_copies():
                cp.wait()

        return start, finish

    shapes = [jax.ShapeDtypeStruct((g.shape[0], g.shape[1] // 2, g.shape[2]), g.dtype) for g in grads]
    shapes += [jax.ShapeDtypeStruct(s.shape, s.dtype) for s in extra]
    return _Carried(list(grads) + extra, shapes, n + 1, copies)


def _chip_exchange_plan(travel, small=None):
    n = len(travel)
    extra = [] if small is None else [small]

    def copies(in_refs, out_refs, send_sems, recv_sems):
        x, y, c, chips = _place()
        mine = 2 * x + y

        def all_copies():
            cps = []
            for j, (cx, cy) in enumerate(chips):
                to = (cx, cy, c)
                for t in range(n):
                    cps.append(_remote(in_refs[t].at[2 * cx + cy], out_refs[t].at[mine], send_sems, recv_sems,
                                       3 * t + j, to))
                if extra:
                    cps.append(_remote(in_refs[n], out_refs[n].at[mine], send_sems, recv_sems, 3 * n + j, to))
            return cps

        def start():
            for cp in all_copies():
                cp.start()

        def finish():
            for cp in all_copies():
                cp.wait()

        return start, finish

    shapes = [jax.ShapeDtypeStruct(g.shape, g.dtype) for g in travel]
    shapes += [jax.ShapeDtypeStruct((N_CHIPS,) + s.shape, s.dtype) for s in extra]
    return _Carried(list(travel) + extra, shapes, 3 * n + 3, copies)


def _sibling_merge_plan(reduced):
    n = len(reduced)

    def copies(in_refs, out_refs, send_sems, recv_sems):
        x, y, c, _ = _place()

        def all_copies():
            return [_remote(in_refs[t], out_refs[t], send_sems, recv_sems, t, (x, y, 1 - c)) for t in range(n)]

        def start():
            for cp in all_copies():
                cp.start()

        def finish():
            for cp in all_copies():
                cp.wait()

        return start, finish

    return _Carried(reduced, [jax.ShapeDtypeStruct(r.shape, r.dtype) for r in reduced], n, copies)


def _pair_sum(place, grad, land):
    n, r, c = grad.shape
    half = r // 2
    tr = _tile(half, 256, SUBLANES)
    nb = half // tr

    def body(place_ref, a_ref, b_ref, travel_ref, own_ref):
        total = a_ref[0] + b_ref[0]
        travel_ref[0] = total.astype(travel_ref.dtype)

        @pl.when(pl.program_id(1) == place_ref[1])
        def _():
            own_ref[...] = total

    return pl.pallas_call(
        body, name="grad_pair_sum",
        grid_spec=pltpu.PrefetchScalarGridSpec(
            num_scalar_prefetch=1, grid=(nb, n),
            in_specs=[pl.BlockSpec((1, tr, c), lambda i, s, p: (s, p[0] * nb + i, 0)),
                      pl.BlockSpec((1, tr, c), lambda i, s, p: (s, i, 0))],
            out_specs=[pl.BlockSpec((1, tr, c), lambda i, s, p: (s, i, 0)),
                       pl.BlockSpec((tr, c), lambda i, s, p: (i, 0))]),
        out_shape=[jax.ShapeDtypeStruct((n, half, c), BF16), jax.ShapeDtypeStruct((half, c), F32)],
        compiler_params=_cparams(("parallel", "arbitrary")),
    )(place, grad, land)


def _chip_sum(place, own, land, name):
    n, r, c = land.shape
    tr = _tile(r, 256, SUBLANES)

    def body(place_ref, own_ref, land_ref, o_ref):
        mine = place_ref[1]
        acc = jnp.zeros(o_ref.shape, F32)
        for s in range(n):
            acc = acc + jnp.where(mine == s, own_ref[...], land_ref[s].astype(F32))
        o_ref[...] = acc

    return pl.pallas_call(
        body, name=name,
        grid_spec=pltpu.PrefetchScalarGridSpec(
            num_scalar_prefetch=1, grid=(r // tr,),
            in_specs=[pl.BlockSpec((tr, c), lambda i, p: (i, 0)),
                      pl.BlockSpec((n, tr, c), lambda i, p: (0, i, 0))],
            out_specs=pl.BlockSpec((tr, c), lambda i, p: (i, 0))),
        out_shape=jax.ShapeDtypeStruct((r, c), F32),
        compiler_params=_cparams(("parallel",)),
    )(place, own, land)


def _add2(a, b):
    rows = a.shape[0]
    tr = _tile(rows, 256, SUBLANES)
    blk = pl.BlockSpec((tr, PACK_W), lambda i: (i, 0))

    def body(a_ref, b_ref, o_ref):
        o_ref[...] = a_ref[...] + b_ref[...]

    return pl.pallas_call(
        body, name="grad_small_pair_sum", grid=(rows // tr,), in_specs=[blk, blk], out_specs=blk,
        out_shape=jax.ShapeDtypeStruct(a.shape, F32), compiler_params=_cparams(("parallel",)),
    )(a, b)


def _merge_halves(place, mine, other):
    first_core = place[0] == 0
    return jnp.concatenate([jnp.where(first_core, mine, other), jnp.where(first_core, other, mine)], axis=0)


_BIG = (("w_in", 2), ("w_pa", 1), ("w_pb", 1), ("w_o", 1), ("w_ffn_gate", 2), ("w_ffn_up", 2),
        ("w_ffn_down", 1))
_SMALL = ("conv_w", "a_log", "dt_bias", "o_norm_w", "sgu_ln_g", "sgu_ln_b", "w_s", "b_s",
          "ln1_g", "ln1_b", "ln2_g", "ln2_b")


def _pad_rows(flat, mult):
    rows = -(-flat.shape[-1] // (PACK_W * mult)) * mult
    pad = rows * PACK_W - flat.shape[-1]
    flat = jnp.pad(flat, [(0, 0)] * (flat.ndim - 1) + [(0, pad)])
    return flat.reshape(flat.shape[:-1] + (rows, PACK_W))


def _unshard(gathered, local, chip, axis):
    parts = [jnp.where(chip == s, local, gathered[s]) for s in range(N_CHIPS)]
    return jnp.concatenate(parts, axis=axis - 1)


def _to_shards(full, axis):
    l, r, c = full.shape
    if axis == 1:
        return full.reshape(l, N_CHIPS, r // N_CHIPS, c)
    return jnp.transpose(full.reshape(l, r, N_CHIPS, c // N_CHIPS), (0, 2, 1, 3))


def _row(v, width=None):
    v = v.reshape(1, -1).astype(F32)
    if width is not None and v.shape[1] < width:
        v = jnp.pad(v, ((0, 0), (0, width - v.shape[1])))
    return v


def _layer_consts(p, l, d):
    heads = d // DN_DK
    return dict(
        alog=_row(p["a_log"][l], LANES), dtb=_row(p["dt_bias"][l], LANES),
        onw=_row(jnp.tile(p["o_norm_w"][l], heads)),
        lng=_row(p["sgu_ln_g"][l]), lnb=_row(p["sgu_ln_b"][l]),
        ws=p["w_s"][l].astype(F32),
        bst=jnp.pad(p["b_s"][l].T, ((0, 0), (0, LANES - p["b_s"].shape[1]))),
        g1=_row(p["ln1_g"][l]), b1=_row(p["ln1_b"][l]), g2=_row(p["ln2_g"][l]), b2=_row(p["ln2_b"][l]))


def _layer_fwd(x, xb, wl, cl, d, f, tb, carried=None):
    projm = _matmul(xb, wl["wm"], NN, "proj_main")
    ba = _matmul(xb, wl["wba"], NN, "proj_gates")
    qkv = _conv_fwd(projm, wl["conv"], d, _tile(x.shape[0], 2 * tb, SUBLANES))
    (o, states, ycors), carried_out = _dn_fwd(qkv, ba, cl["alog"], cl["dtb"], d, carried)
    ya, yb = _gate_sgu_fwd(o, projm, cl["onw"], cl["lng"], cl["lnb"], cl["ws"], cl["bst"], d)
    pa, pb, m, h1, x1, x1b = _mix_fwd(ya, yb, projm, x, wl["wpa"], wl["wpb"], wl["wo"], cl["g1"], cl["b1"], d, tb)
    gu = _matmul(x1b, wl["wgu"], NN, "ffn_in")
    act = _swiglu_fwd(gu, f, tb)
    ffn = _matmul(act, wl["wd"], NN, "ffn_out")
    h2, x2, x2b = _res_ln_fwd(x1, ffn, cl["g2"], cl["b2"], tb)
    saved = dict(xb=xb, projm=projm, ba=ba, qkv=qkv, o=o, states=states, ycors=ycors, ya=ya, yb=yb,
                 pa=pa, pb=pb, m=m, h1=h1, x1b=x1b, gu=gu, act=act, h2=h2)
    return x2, x2b, saved, carried_out


def _layer_bwd(dx2, sv, wl, cl, d, f, tb, carried=None):
    g = {}
    dh2, dh2b, dg2, db2 = _ln_bwd_call(dx2, sv["h2"], cl["g2"], tb)
    g["ln2_g"], g["ln2_b"] = dg2.sum(0), db2.sum(0)
    g["wd"] = _matmul(sv["act"], dh2b, TN, "ffn_out_dw")
    da = _matmul(dh2b, wl["wd"], NT, "ffn_out_dx")
    dgu = _swiglu_bwd(da, sv["gu"], f, tb)
    g["wgu"] = _matmul(sv["x1b"], dgu, TN, "ffn_in_dw")
    dx1 = _matmul(dgu, wl["wgu"], NT, "ffn_in_dx", add=dh2, coef=ALPHA)
    dh1, dh1b, dg1, db1 = _ln_bwd_call(dx1, sv["h1"], cl["g1"], tb)
    g["ln1_g"], g["ln1_b"] = dg1.sum(0), db1.sum(0)
    g["wo"] = _matmul(sv["m"], dh1b, TN, "wo_dw")
    dpa, dpb, dya, dyb, dprojm = _mix_bwd(dh1b, sv["pa"], sv["pb"], sv["projm"], wl["wpa"], wl["wpb"], wl["wo"], d, tb)
    g["wpa"] = _matmul(sv["ya"], dpa, TN, "wpa_dw")
    g["wpb"] = _matmul(sv["yb"], dpb, TN, "wpb_dw")
    do, dprojm, donw, dlng, dlnb, dws, dbst = _gate_sgu_bwd(
        dya, dyb, sv["o"], sv["projm"], cl["onw"], cl["lng"], cl["lnb"], cl["ws"], cl["bst"], dprojm, d)
    heads, groups = d // DN_DK, d // SGU_GROUP_DIM
    g["o_norm_w"], g["sgu_ln_g"], g["sgu_ln_b"] = donw.sum(0), dlng.sum(0), dlnb.sum(0)
    g["w_s"], g["b_s"] = dws, dbst[:, :groups].T
    (dqkv, dba, dal, ddt), carried_out = _dn_bwd(sv["qkv"], sv["ba"], cl["alog"], cl["dtb"], do, sv["states"],
                                                 sv["ycors"], d, carried)
    g["a_log"], g["dt_bias"] = dal.sum(0)[:heads], ddt.sum(0)[:heads]
    tbc = _tile(dx2.shape[0], 2 * tb, SUBLANES)
    dy, dcw = _conv_bwd_dy(sv["projm"], wl["conv"], dqkv, d, tbc)
    g["conv_w"] = dcw.sum(1)
    dprojm = _conv_bwd_dx(dy, wl["conv"], dprojm, d, tbc)
    g["wm"] = _matmul(sv["xb"], dprojm, TN, "proj_main_dw")
    g["wba"] = _matmul(sv["xb"], dba, TN, "proj_gates_dw")
    dx = _matmul(dba, wl["wba"], NT, "proj_gates_dx", add=dh1, coef=ALPHA)
    dx = _matmul(dprojm, wl["wm"], NT, "proj_main_dx", add=dx)
    return dx, g, carried_out


def _layer_weights(full, d):
    heads, q4, w_in = d // DN_DK, 4 * d, full["w_in"]
    wba = jnp.zeros((d, 2 * LANES), w_in.dtype)
    wba = wba.at[:, :heads].set(w_in[:, q4:q4 + heads])
    wba = wba.at[:, LANES:LANES + heads].set(w_in[:, q4 + heads:q4 + 2 * heads])
    return dict(
        wm=jnp.concatenate([w_in[:, :q4], w_in[:, q4 + 2 * heads:]], axis=1), wba=wba,
        conv=full["conv_w"], wpa=full["w_pa"], wpb=full["w_pb"], wo=full["w_o"],
        wgu=jnp.concatenate([full["w_ffn_gate"], full["w_ffn_up"]], axis=1), wd=full["w_ffn_down"])


def _grad_shards(g, d, f):
    heads, q4 = d // DN_DK, 4 * d
    wsh, fs = 2 * d + heads // 2, f // N_CHIPS
    gm, gba, ggu = g["wm"], g["wba"], g["wgu"]
    rows = lambda a: a.reshape(N_CHIPS, -1, a.shape[1])
    out = dict(g)
    out.update({
        "w_in": jnp.stack([gm[:, :wsh],
                           jnp.concatenate([gm[:, wsh:q4], gba[:, :heads]], axis=1),
                           jnp.concatenate([gba[:, LANES:LANES + heads], gm[:, q4:q4 + wsh - heads]], axis=1),
                           gm[:, q4 + wsh - heads:]]),
        "w_pa": rows(g["wpa"]), "w_pb": rows(g["wpb"]), "w_o": rows(g["wo"]), "w_ffn_down": rows(g["wd"]),
        "w_ffn_gate": jnp.stack([ggu[:, s * fs:(s + 1) * fs] for s in range(N_CHIPS)]),
        "w_ffn_up": jnp.stack([ggu[:, f + s * fs:f + (s + 1) * fs] for s in range(N_CHIPS)])})
    return out


def _local_step(x, target, full0, full1_of, small_w, carry_fwd=None, carry_bwd_of=None):
    t, d = x.shape
    f = full0["w_ffn_gate"].shape[-1]
    tb = _tile(t, 256, SUBLANES)
    consts = [_layer_consts(small_w, l, d) for l in range(DEPTH)]
    w0 = _layer_weights(full0, d)
    x1, x1b, sv0, got = _layer_fwd(x, x.astype(ACT), w0, consts[0], d, f, tb, carry_fwd)
    w1 = _layer_weights(full1_of(got), d)
    x2, _, sv1, _ = _layer_fwd(x1, x1b, w1, consts[1], d, f, tb)
    dy, loss_parts = _loss_head(x2, target, tb)
    dy, g1, _ = _layer_bwd(dy, sv1, w1, consts[1], d, f, tb)
    g1 = _grad_shards(g1, d, f)
    dy, g0, got = _layer_bwd(dy, sv0, w0, consts[0], d, f, tb, carry_bwd_of(g1) if carry_bwd_of else None)
    return loss_parts, dy, [_grad_shards(g0, d, f), g1], got


def kernel(x, w_in, conv_w, a_log, dt_bias, o_norm_w, sgu_ln_g, sgu_ln_b, w_s, b_s, w_pa, w_pb, w_o, ln1_g, ln1_b, w_ffn_gate, w_ffn_up, w_ffn_down, ln2_g, ln2_b, loss_target, m_w_in, m_conv_w, m_a_log, m_dt_bias, m_o_norm_w, m_sgu_ln_g, m_sgu_ln_b, m_w_s, m_b_s, m_w_pa, m_w_pb, m_w_o, m_ln1_g, m_ln1_b, m_w_ffn_gate, m_w_ffn_up, m_w_ffn_down, m_ln2_g, m_ln2_b, v_w_in, v_conv_w, v_a_log, v_dt_bias, v_o_norm_w, v_sgu_ln_g, v_sgu_ln_b, v_w_s, v_b_s, v_w_pa, v_w_pb, v_w_o, v_ln1_g, v_ln1_b, v_w_ffn_gate, v_w_ffn_up, v_w_ffn_down, v_ln2_g, v_ln2_b):
    names = ("w_in", "conv_w", "a_log", "dt_bias", "o_norm_w", "sgu_ln_g", "sgu_ln_b", "w_s", "b_s", "w_pa",
             "w_pb", "w_o", "ln1_g", "ln1_b", "w_ffn_gate", "w_ffn_up", "w_ffn_down", "ln2_g", "ln2_b")
    w = dict(zip(names, (w_in, conv_w, a_log, dt_bias, o_norm_w, sgu_ln_g, sgu_ln_b, w_s, b_s, w_pa, w_pb, w_o,
                         ln1_g, ln1_b, w_ffn_gate, w_ffn_up, w_ffn_down, ln2_g, ln2_b)))
    mom = dict(zip(names, (m_w_in, m_conv_w, m_a_log, m_dt_bias, m_o_norm_w, m_sgu_ln_g, m_sgu_ln_b, m_w_s, m_b_s,
                           m_w_pa, m_w_pb, m_w_o, m_ln1_g, m_ln1_b, m_w_ffn_gate, m_w_ffn_up, m_w_ffn_down,
                           m_ln2_g, m_ln2_b)))
    var = dict(zip(names, (v_w_in, v_conv_w, v_a_log, v_dt_bias, v_o_norm_w, v_sgu_ln_g, v_sgu_ln_b, v_w_s, v_b_s,
                           v_w_pa, v_w_pb, v_w_o, v_ln1_g, v_ln1_b, v_w_ffn_gate, v_w_ffn_up, v_w_ffn_down,
                           v_ln2_g, v_ln2_b)))
    chip = 2 * lax.axis_index("x") + lax.axis_index("y")
    place = jnp.stack([lax.axis_index("c"), chip]).astype(jnp.int32)

    big = [k for k, _ in _BIG]
    local = {k: w[k].astype(BF16) for k in big}

    def shards_of(l):
        return [local[k][l] for k in big] + [conv_w[l]]

    def full_of(l, gathered):
        sh = shards_of(l)
        full = {k: _unshard(gt, lc, chip, axis) for (k, axis), gt, lc in zip(_BIG, gathered, sh)}
        full["conv_w"] = _unshard(gathered[-1], sh[-1], chip, 2)
        return full

    pairs = {}

    def carry_bwd_of(g1):
        lands = _run_comm("grad_sibling_exchange", _sibling_exchange_plan([g1[k] for k in big]))
        pairs[1] = [_pair_sum(place, g1[k], land) for k, land in zip(big, lands)]
        return _chip_exchange_plan([p[0] for p in pairs[1]])

    full0 = full_of(0, _run_comm("all_gather_weights", _all_gather_plan(shards_of(0))))
    small_w = {k: w[k] for k in _SMALL if k != "conv_w"}
    loss_parts, grad_x, g, lands1 = _local_step(
        x[0], loss_target[0], full0, lambda got: full_of(1, got), small_w,
        carry_fwd=_all_gather_plan(shards_of(1)), carry_bwd_of=carry_bwd_of)

    red1 = [_chip_sum(place, p[1], land, "grad_chip_sum") for p, land in zip(pairs[1], lands1)]
    small_g = {k: jnp.stack([g[l][k] for l in range(DEPTH)]) for k in _SMALL}
    small_sizes = [small_g[k].size for k in _SMALL]
    small = _pad_rows(jnp.concatenate([small_g[k].reshape(-1) for k in _SMALL]), SUBLANES)
    *lands, sland = _run_comm("grad_sibling_exchange_last", _sibling_exchange_plan([g[0][k] for k in big], small))
    pairs[0] = [_pair_sum(place, g[0][k], land) for k, land in zip(big, lands)]
    small_chip = _add2(small, sland)
    *lands0, sland2 = _run_comm("grad_chip_exchange_last", _chip_exchange_plan([p[0] for p in pairs[0]], small_chip))
    red0 = [_chip_sum(place, p[1], land, "grad_chip_sum") for p, land in zip(pairs[0], lands0)]
    small_total = _chip_sum(place, small_chip, sland2, "grad_small_chip_sum")
    others = _run_comm("grad_sibling_merge", _sibling_merge_plan(red0 + red1))
    halves = [_merge_halves(place, mine, other) for mine, other in zip(red0 + red1, others)]
    grads = {k: jnp.stack([halves[i], halves[len(big) + i]]) for i, k in enumerate(big)}
    small_total, off = small_total.reshape(-1), 0
    for k, n in zip(_SMALL, small_sizes):
        grads[k] = small_total[off:off + n].reshape(small_g[k].shape)
        off += n
    grads["conv_w"] = lax.dynamic_index_in_dim(_to_shards(grads["conv_w"], 2), chip, 1, keepdims=False)

    delta, new_m, new_v = {}, {}, {}
    for k in [k for k, _ in _BIG] + ["conv_w"]:
        delta[k], new_m[k], new_v[k] = _adamw(w[k], grads[k], mom[k], var[k])
    rep = [k for k in _SMALL if k != "conv_w"]
    pack = lambda dct: _pad_rows(jnp.concatenate([dct[k].reshape(-1) for k in rep]), SUBLANES)
    packed = _adamw(pack(w), pack(grads), pack(mom), pack(var))
    off = 0
    for k in rep:
        n = w[k].size
        for dst, src in zip((delta, new_m, new_v), packed):
            dst[k] = src.reshape(-1)[off:off + n].reshape(w[k].shape)
        off += n

    loss = 0.5 * lax.psum(jnp.sum(loss_parts), ("x", "y", "c")) / x.shape[-1]
    return (loss, grad_x[None], *[grads[k] for k in names], *[delta[k] for k in names],
            *[new_m[k] for k in names], *[new_v[k] for k in names])
```

```python
import math

import jax
import jax.numpy as jnp
from jax import lax
from jax.experimental import pallas as pl
from jax.experimental.pallas import tpu as pltpu

F32 = jnp.float32
BF16 = jnp.bfloat16
MXU_DTYPE = jnp.bfloat16
ACT = jnp.bfloat16
HIGHEST = lax.Precision.HIGHEST

DEPTH = 2
CHUNK = 64
DN_GROUP = 2
SGU_BLOCK = 128
CONV_K = 4
DN_DK = 128
SGU_GROUP_DIM = 128
LN_EPS = 1e-5
RMS_EPS = 1e-6
ALPHA = (2 * DEPTH) ** 0.25
ADAM_LR, ADAM_B1, ADAM_B2, ADAM_EPS, ADAM_WD, ADAM_STEP = 0.001, 0.9, 0.999, 1e-08, 0.01, 10

LANES = 128
SUBLANES = 8
VMEM_LIMIT = 52 * 2 ** 20
PACK_W = 1024
N_CHIPS = 4

NN = ((1,), (0,))
NT = ((1,), (1,))
TN = ((0,), (0,))
MESH = pl.DeviceIdType.MESH
ANY = pl.BlockSpec(memory_space=pl.ANY)


def _dot(a, b, dims=NN, prec=None):
    if prec is None:
        a = a.astype(MXU_DTYPE)
        b = b.astype(MXU_DTYPE)
    return lax.dot_general(a, b, (dims, ((), ())), preferred_element_type=F32, precision=prec)


def _cparams(sem=None):
    return pltpu.CompilerParams(dimension_semantics=sem, vmem_limit_bytes=VMEM_LIMIT)


def _tile(dim, pref, unit=LANES):
    t = (min(pref, dim) // unit) * unit
    while t >= unit:
        if dim % t == 0:
            return t
        t -= unit
    return dim


def _fold8(x):
    r, n = x.shape
    return x.reshape(r // SUBLANES, SUBLANES, n).sum(axis=0)


def _sigmoid(x):
    return 1.0 / (1.0 + jnp.exp(-x))


def _gelu(x):
    return 0.5 * x * (1.0 + lax.erf(x * (2.0 ** -0.5)))


def _gelu_grad(x):
    return 0.5 * (1.0 + lax.erf(x * (2.0 ** -0.5))) + x * jnp.exp(-0.5 * x * x) * (2.0 * math.pi) ** -0.5


def _ln_hat(h):
    mu = jnp.mean(h, axis=-1, keepdims=True)
    xc = h - mu
    var = jnp.mean(xc * xc, axis=-1, keepdims=True)
    r = lax.rsqrt(var + LN_EPS)
    return xc * r, r


def _ln_bwd(dxhat, xhat, r):
    return r * (dxhat - jnp.mean(dxhat, axis=-1, keepdims=True)
                - xhat * jnp.mean(dxhat * xhat, axis=-1, keepdims=True))


MM_TILE = 1536


def _matmul(a, b, dims, name, out_dtype=F32, add=None, coef=1.0, tm=MM_TILE, tn=MM_TILE, tk=MM_TILE):
    if dims == NN:
        (m, k), n = a.shape, b.shape[1]
    elif dims == NT:
        (m, k), n = a.shape, b.shape[0]
    else:
        (k, m), n = a.shape, b.shape[1]
    tm, tn, tk = _tile(m, tm), _tile(n, tn), _tile(k, tk)
    nk = k // tk
    a_spec = pl.BlockSpec((tk, tm), lambda j, i, q: (q, i)) if dims == TN else pl.BlockSpec((tm, tk), lambda j, i, q: (i, q))
    b_spec = pl.BlockSpec((tn, tk), lambda j, i, q: (j, q)) if dims == NT else pl.BlockSpec((tk, tn), lambda j, i, q: (q, j))
    o_spec = pl.BlockSpec((tm, tn), lambda j, i, q: (i, j))
    has_add = add is not None

    def body(*refs):
        a_ref, b_ref = refs[0], refs[1]
        add_ref = refs[2] if has_add else None
        o_ref, acc_ref = refs[2 + has_add], refs[3 + has_add]
        q = pl.program_id(2)
        part = _dot(a_ref[...], b_ref[...], dims)

        def finish(r):
            if has_add:
                r = r + coef * add_ref[...]
            o_ref[...] = r.astype(out_dtype)

        if nk == 1:
            finish(part)
        else:
            @pl.when(q == 0)
            def _():
                acc_ref[...] = part

            @pl.when(q > 0)
            def _():
                acc_ref[...] += part

            @pl.when(q == nk - 1)
            def _():
                finish(acc_ref[...])

    ins = [a, b] + ([add] if has_add else [])
    in_specs = [a_spec, b_spec] + ([o_spec] if has_add else [])
    return pl.pallas_call(
        body, name=name, grid=(n // tn, m // tm, nk),
        in_specs=in_specs, out_specs=o_spec,
        out_shape=jax.ShapeDtypeStruct((m, n), out_dtype),
        scratch_shapes=[pltpu.VMEM((tm, tn) if nk > 1 else (SUBLANES, LANES), F32)],
        compiler_params=_cparams(("parallel", "parallel", "arbitrary")),
    )(*ins)


def _conv_taps(cur_ref, halo_ref, first):
    x = cur_ref[...]
    tb = x.shape[0]
    halo = jnp.where(first, 0.0, halo_ref[...])
    xc = jnp.concatenate([halo, x], axis=0)
    return [x] + [pltpu.roll(xc, s, 0)[SUBLANES:SUBLANES + tb] for s in range(1, CONV_K)]


def _conv_fwd(projm, conv_w, d, tb):
    t = projm.shape[0]
    heads = d // DN_DK
    hb = tb // SUBLANES

    def body(cur_ref, halo_ref, w_ref, o_ref):
        i, j = pl.program_id(0), pl.program_id(1)
        taps = _conv_taps(cur_ref, halo_ref, i == 0)
        y = taps[0] * w_ref[CONV_K - 1:CONV_K, :]
        for s in range(1, CONV_K):
            y = y + taps[s] * w_ref[CONV_K - 1 - s:CONV_K - s, :]
        act = y * _sigmoid(y)
        scale = jnp.where(j == 0, DN_DK ** -0.5, 1.0)
        for h in range(heads):
            seg = act[:, h * DN_DK:(h + 1) * DN_DK]
            r = lax.rsqrt(jnp.sum(seg * seg, axis=1, keepdims=True) + RMS_EPS) * scale
            o_ref[:, h * DN_DK:(h + 1) * DN_DK] = seg * jnp.where(j < 2, r, 1.0)

    return pl.pallas_call(
        body, name="conv_fwd", grid=(t // tb, 3),
        in_specs=[pl.BlockSpec((tb, d), lambda i, j: (i, j)),
                  pl.BlockSpec((SUBLANES, d), lambda i, j: (jnp.maximum(i * hb - 1, 0), j)),
                  pl.BlockSpec((CONV_K, d), lambda i, j: (0, j))],
        out_specs=pl.BlockSpec((tb, d), lambda i, j: (i, j)),
        out_shape=jax.ShapeDtypeStruct((t, 3 * d), F32),
        compiler_params=_cparams(("parallel", "parallel")),
    )(projm, projm, conv_w)


def _conv_bwd_dy(projm, conv_w, dqkv, d, tb):
    t = projm.shape[0]
    heads = d // DN_DK
    hb = tb // SUBLANES

    def body(cur_ref, halo_ref, w_ref, dout_ref, dy_ref, dw_ref):
        j, i = pl.program_id(0), pl.program_id(1)
        taps = _conv_taps(cur_ref, halo_ref, i == 0)
        y = taps[0] * w_ref[CONV_K - 1:CONV_K, :]
        for s in range(1, CONV_K):
            y = y + taps[s] * w_ref[CONV_K - 1 - s:CONV_K - s, :]
        sg = _sigmoid(y)
        act = y * sg
        dact = sg * (1.0 + y * (1.0 - sg))
        scale = jnp.where(j == 0, DN_DK ** -0.5, 1.0)
        for h in range(heads):
            cols = slice(h * DN_DK, (h + 1) * DN_DK)
            seg = act[:, cols]
            r = lax.rsqrt(jnp.sum(seg * seg, axis=1, keepdims=True) + RMS_EPS)
            nrm = seg * r
            dout = dout_ref[:, cols]
            dn = dout * scale
            ds = jnp.where(j < 2, r * (dn - nrm * jnp.sum(dn * nrm, axis=1, keepdims=True)), dout)
            dy_ref[:, cols] = ds * dact[:, cols]
        dy = dy_ref[...]

        @pl.when(i == 0)
        def _():
            dw_ref[...] = jnp.zeros_like(dw_ref)

        for s in range(CONV_K):
            dw_ref[CONV_K - 1 - s] += _fold8(dy * taps[s])

    return pl.pallas_call(
        body, name="conv_bwd_dy", grid=(3, t // tb),
        in_specs=[pl.BlockSpec((tb, d), lambda j, i: (i, j)),
                  pl.BlockSpec((SUBLANES, d), lambda j, i: (jnp.maximum(i * hb - 1, 0), j)),
                  pl.BlockSpec((CONV_K, d), lambda j, i: (0, j)),
                  pl.BlockSpec((tb, d), lambda j, i: (i, j))],
        out_specs=[pl.BlockSpec((tb, d), lambda j, i: (i, j)),
                   pl.BlockSpec((CONV_K, SUBLANES, d), lambda j, i: (0, 0, j))],
        out_shape=[jax.ShapeDtypeStruct((t, 3 * d), F32),
                   jax.ShapeDtypeStruct((CONV_K, SUBLANES, 3 * d), F32)],
        compiler_params=_cparams(("parallel", "arbitrary")),
    )(projm, projm, conv_w, dqkv)


def _conv_bwd_dx(dy, conv_w, dprojm, d, tb):
    t = dy.shape[0]
    hb = tb // SUBLANES
    last = t // tb - 1

    def body(cur_ref, halo_ref, w_ref, alias_ref, o_ref):
        i = pl.program_id(0)
        cur = cur_ref[...]
        halo = jnp.where(i == last, 0.0, halo_ref[...])
        dc = jnp.concatenate([cur, halo], axis=0)
        acc = cur * w_ref[CONV_K - 1:CONV_K, :]
        for s in range(1, CONV_K):
            acc = acc + pltpu.roll(dc, tb + SUBLANES - s, 0)[:tb] * w_ref[CONV_K - 1 - s:CONV_K - s, :]
        o_ref[...] = acc.astype(o_ref.dtype)

    return pl.pallas_call(
        body, name="conv_bwd_dx", grid=(t // tb, 3),
        in_specs=[pl.BlockSpec((tb, d), lambda i, j: (i, j)),
                  pl.BlockSpec((SUBLANES, d), lambda i, j: (jnp.minimum((i + 1) * hb, t // SUBLANES - 1), j)),
                  pl.BlockSpec((CONV_K, d), lambda i, j: (0, j)),
                  ANY],
        out_specs=pl.BlockSpec((tb, d), lambda i, j: (i, j)),
        out_shape=jax.ShapeDtypeStruct(dprojm.shape, dprojm.dtype),
        input_output_aliases={3: 0},
        compiler_params=_cparams(("parallel", "parallel")),
    )(dy, dy, conv_w, dprojm)


def _beta_g(ba, alog, dtb):
    beta = _sigmoid(ba[:, :LANES])
    xa = ba[:, LANES:] + dtb
    softplus = jnp.maximum(xa, 0.0) + jnp.log(1.0 + jnp.exp(-jnp.abs(xa)))
    ea = jnp.exp(alog)
    return beta, -ea * softplus, ea, _sigmoid(xa)


def _inv_corrections(mats):
    ys = [-a for a in mats]
    ps = [_dot(a, a) for a in mats]
    steps = int(math.log2(CHUNK)) - 1
    for it in range(steps):
        ys = [y + p + _dot(y, p) for y, p in zip(ys, ps)]
        if it < steps - 1:
            ps = [_dot(p, p) for p in ps]
    return ys


def _chunk_masks():
    row = lax.broadcasted_iota(jnp.int32, (CHUNK, CHUNK), 0)
    col = lax.broadcasted_iota(jnp.int32, (CHUNK, CHUNK), 1)
    return row >= col, row > col, row <= col


def _col_of(mat, lane_idx, h):
    return jnp.sum(jnp.where(lane_idx == h, mat, 0.0), axis=1, keepdims=True)


def _row_of(mat, sub_idx, h):
    return jnp.sum(jnp.where(sub_idx == h, mat, 0.0), axis=0, keepdims=True)


def _carrying(compute, n_in, n_out, n_scratch, carried, steps):
    if carried is None:
        return compute
    ci, co = len(carried.inputs), len(carried.out_shapes)

    def body(*refs):
        ins, c_in = refs[:n_in], refs[n_in:n_in + ci]
        outs, c_out = refs[n_in + ci:n_in + ci + n_out], refs[n_in + ci + n_out:n_in + ci + n_out + co]
        scratch = refs[n_in + ci + n_out + co:]
        start, finish = carried.copies(c_in, c_out, scratch[n_scratch], scratch[n_scratch + 1])

        @pl.when(pl.program_id(0) == 0)
        def _():
            start()

        compute(*ins, *outs, *scratch[:n_scratch])

        @pl.when(pl.program_id(0) == steps - 1)
        def _():
            finish()

    return body


def _dn_fwd(qkv, ba, alog, dtb, d, carried=None):
    t = qkv.shape[0]
    heads = d // DN_DK
    n_chunks = t // CHUNK
    grp = DN_GROUP if n_chunks % DN_GROUP == 0 else 1
    span = grp * CHUNK
    extra = carried or _Carried([], [], 0, None)

    def compute(qkv_ref, ba_ref, al_ref, dt_ref, o_ref, s_ref, y_ref, state):
        @pl.when(pl.program_id(0) == 0)
        def _():
            state[...] = jnp.zeros_like(state)

        tril, strict, _ = _chunk_masks()
        beta, g, _, _ = _beta_g(ba_ref[...], al_ref[...], dt_ref[...])
        lane = lax.broadcasted_iota(jnp.int32, (CHUNK, LANES), 1)
        sub = lax.broadcasted_iota(jnp.int32, (LANES, CHUNK), 0)
        rowc = lax.broadcasted_iota(jnp.int32, (CHUNK, 1), 0)
        hs = range(heads)
        units = [(c, h) for c in range(grp) for h in hs]
        un = range(len(units))
        rows = lambda c: slice(c * CHUNK, (c + 1) * CHUNK)
        gc = [_dot(jnp.where(tril, 1.0, 0.0), g[rows(c)], NN, HIGHEST) for c in range(grp)]
        gct = [m.T for m in gc]
        q = [qkv_ref[rows(c), h * DN_DK:(h + 1) * DN_DK] for c, h in units]
        k = [qkv_ref[rows(c), d + h * DN_DK:d + (h + 1) * DN_DK] for c, h in units]
        v = [qkv_ref[rows(c), 2 * d + h * DN_DK:2 * d + (h + 1) * DN_DK] for c, h in units]
        gch = [_col_of(gc[c], lane, h) for c, h in units]
        bh = [_col_of(beta[rows(c)], lane, h) for c, h in units]
        dec = [jnp.where(tril, jnp.exp(gch[n] - _row_of(gct[c], sub, h)), 0.0) for n, (c, h) in enumerate(units)]
        egc = [jnp.exp(gch[n]) for n in un]
        gl = [jnp.sum(jnp.where(rowc == CHUNK - 1, gch[n], 0.0), axis=0, keepdims=True) for n in un]
        kb = [k[n] * bh[n] for n in un]
        a = [jnp.where(strict, _dot(kb[n], k[n], NT) * dec[n], 0.0) for n in un]
        p = [_dot(q[n], k[n], NT) * dec[n] for n in un]
        ycor = _inv_corrections(a)
        rhs = [jnp.concatenate([v[n] * bh[n], kb[n] * egc[n]], axis=1) for n in un]
        sol = [rhs[n] + _dot(ycor[n], rhs[n]) for n in un]
        qg = [q[n] * egc[n] for n in un]
        kd = [k[n] * jnp.exp(gl[n] - gch[n]) for n in un]
        egl = [jnp.exp(gl[n]) for n in un]
        s_cur, s_in, o = [state[h] for h in hs], [], []
        for c in range(grp):
            ns = [c * heads + h for h in hs]
            vn = [sol[n][:, :DN_DK] - _dot(sol[n][:, DN_DK:], s_cur[h]) for h, n in enumerate(ns)]
            o += [_dot(qg[n], s_cur[h]) + _dot(p[n], vn[h]) for h, n in enumerate(ns)]
            s_in += s_cur
            s_cur = [s_cur[h] * egl[n] + _dot(kd[n], vn[h], TN) for h, n in enumerate(ns)]
        for n, (c, h) in enumerate(units):
            o_ref[rows(c), h * DN_DK:(h + 1) * DN_DK] = o[n]
            s_ref[c, h] = s_in[n]
            y_ref[h, rows(c), :] = ycor[n]
        for h in hs:
            state[h] = s_cur[h]

    res = pl.pallas_call(
        _carrying(compute, 4, 3, 1, carried, n_chunks // grp),
        name="dn_fwd_carrying" if carried else "dn_fwd", grid=(n_chunks // grp,),
        in_specs=[pl.BlockSpec((span, 3 * d), lambda i: (i, 0)),
                  pl.BlockSpec((span, 2 * LANES), lambda i: (i, 0)),
                  pl.BlockSpec((1, LANES), lambda i: (0, 0)),
                  pl.BlockSpec((1, LANES), lambda i: (0, 0))] + [ANY] * len(extra.inputs),
        out_specs=[pl.BlockSpec((span, d), lambda i: (i, 0)),
                   pl.BlockSpec((grp, heads, DN_DK, DN_DK), lambda i: (i, 0, 0, 0)),
                   pl.BlockSpec((heads, span, CHUNK), lambda i: (0, i, 0))] + [ANY] * len(extra.out_shapes),
        out_shape=[jax.ShapeDtypeStruct((t, d), F32),
                   jax.ShapeDtypeStruct((n_chunks, heads, DN_DK, DN_DK), F32),
                   jax.ShapeDtypeStruct((heads, t, CHUNK), F32)] + extra.out_shapes,
        scratch_shapes=[pltpu.VMEM((heads, DN_DK, DN_DK), F32)] + (extra.scratch() if carried else []),
        compiler_params=_cparams(("arbitrary",)),
    )(qkv, ba, alog, dtb, *extra.inputs)
    return res[:3], res[3:]


def _dn_bwd(qkv, ba, alog, dtb, dout, states, ycors, d, carried=None):
    t = qkv.shape[0]
    heads = d // DN_DK
    n_chunks = t // CHUNK
    grp = DN_GROUP if n_chunks % DN_GROUP == 0 else 1
    span = grp * CHUNK
    rev = lambda i: n_chunks // grp - 1 - i
    extra = carried or _Carried([], [], 0, None)

    def compute(qkv_ref, ba_ref, al_ref, dt_ref, do_ref, s_ref, y_ref,
                dqkv_ref, dba_ref, dal_ref, ddt_ref, dstate):
        @pl.when(pl.program_id(0) == 0)
        def _():
            dstate[...] = jnp.zeros_like(dstate)
            dal_ref[...] = jnp.zeros_like(dal_ref)
            ddt_ref[...] = jnp.zeros_like(ddt_ref)

        tril, strict, triu = _chunk_masks()
        beta, g, ea, sig_a = _beta_g(ba_ref[...], al_ref[...], dt_ref[...])
        lane = lax.broadcasted_iota(jnp.int32, (CHUNK, LANES), 1)
        sub = lax.broadcasted_iota(jnp.int32, (LANES, CHUNK), 0)
        rowc = lax.broadcasted_iota(jnp.int32, (CHUNK, 1), 0)
        ones = jnp.ones((CHUNK, LANES), F32)
        hs = range(heads)
        units = [(c, h) for c in range(grp) for h in hs]
        un = range(len(units))
        rows = lambda c: slice(c * CHUNK, (c + 1) * CHUNK)
        rsum = lambda x_: jnp.sum(x_, axis=1, keepdims=True)
        gc = [_dot(jnp.where(tril, 1.0, 0.0), g[rows(c)], NN, HIGHEST) for c in range(grp)]
        gct = [m.T for m in gc]
        q = [qkv_ref[rows(c), h * DN_DK:(h + 1) * DN_DK] for c, h in units]
        k = [qkv_ref[rows(c), d + h * DN_DK:d + (h + 1) * DN_DK] for c, h in units]
        v = [qkv_ref[rows(c), 2 * d + h * DN_DK:2 * d + (h + 1) * DN_DK] for c, h in units]
        dout_h = [do_ref[rows(c), h * DN_DK:(h + 1) * DN_DK] for c, h in units]
        s0 = [s_ref[c, h] for c, h in units]
        ycor = [y_ref[h, rows(c), :] for c, h in units]
        gch = [_col_of(gc[c], lane, h) for c, h in units]
        bh = [_col_of(beta[rows(c)], lane, h) for c, h in units]
        dec = [jnp.where(tril, jnp.exp(gch[n] - _row_of(gct[c], sub, h)), 0.0) for n, (c, h) in enumerate(units)]
        egc = [jnp.exp(gch[n]) for n in un]
        gl = [jnp.sum(jnp.where(rowc == CHUNK - 1, gch[n], 0.0), axis=0, keepdims=True) for n in un]
        egl = [jnp.exp(gl[n]) for n in un]
        ekd = [jnp.exp(gl[n] - gch[n]) for n in un]
        kb = [k[n] * bh[n] for n in un]
        kd = [k[n] * ekd[n] for n in un]
        qg = [q[n] * egc[n] for n in un]
        kbg = [kb[n] * egc[n] for n in un]
        a = [jnp.where(strict, _dot(kb[n], k[n], NT) * dec[n], 0.0) for n in un]
        p = [_dot(q[n], k[n], NT) * dec[n] for n in un]
        rhs = [jnp.concatenate([v[n] * bh[n], kbg[n]], axis=1) for n in un]
        sol = [rhs[n] + _dot(ycor[n], rhs[n]) for n in un]
        w = [sol[n][:, DN_DK:] for n in un]
        vn = [sol[n][:, :DN_DK] - _dot(w[n], s0[n]) for n in un]
        dqg = [_dot(dout_h[n], s0[n], NT) for n in un]
        dp = [jnp.where(tril, _dot(dout_h[n], vn[n], NT), 0.0) for n in un]
        pdo = [_dot(p[n], dout_h[n], TN) for n in un]
        qdo = [_dot(qg[n], dout_h[n], TN) for n in un]
        ds_cur = [dstate[h] for h in hs]
        dsn, dvn = [None] * len(units), [None] * len(units)
        for c in reversed(range(grp)):
            for h in hs:
                dsn[c * heads + h] = ds_cur[h]
            for h in hs:
                n = c * heads + h
                dvn[n] = pdo[n] + _dot(kd[n], ds_cur[h])
            ds_cur = [qdo[c * heads + h] + egl[c * heads + h] * ds_cur[h]
                      - _dot(w[c * heads + h], dvn[c * heads + h], TN) for h in hs]
        dkd = [_dot(vn[n], dsn[n], NT) for n in un]
        dw = [-_dot(dvn[n], s0[n], NT) for n in un]
        dgl = [jnp.sum(rsum(dsn[n] * s0[n]), axis=0, keepdims=True) * egl[n] for n in un]
        dsol = [jnp.concatenate([dvn[n], dw[n]], axis=1) for n in un]
        drhs = [dsol[n] + _dot(ycor[n], dsol[n], TN) for n in un]
        dvb = [drhs[n][:, :DN_DK] for n in un]
        dkbg = [drhs[n][:, DN_DK:] for n in un]
        da = [jnp.where(strict, -_dot(drhs[n], sol[n], NT), 0.0) for n in un]
        dma = [da[n] * dec[n] for n in un]
        dmp = [dp[n] * dec[n] for n in un]
        dkb = [_dot(dma[n], k[n]) + dkbg[n] * egc[n] for n in un]
        dq = [_dot(dmp[n], k[n]) + dqg[n] * egc[n] for n in un]
        dk = [_dot(dma[n], kb[n], TN) + _dot(dmp[n], q[n], TN) + dkd[n] * ekd[n] + dkb[n] * bh[n] for n in un]
        e = [da[n] * a[n] + dp[n] * p[n] for n in un]
        colsum = [_dot(e[n], ones, TN, HIGHEST) for n in un]
        tkd = [rsum(dkd[n] * kd[n]) for n in un]
        for n, (c, h) in enumerate(units):
            dqkv_ref[rows(c), h * DN_DK:(h + 1) * DN_DK] = dq[n]
            dqkv_ref[rows(c), d + h * DN_DK:d + (h + 1) * DN_DK] = dk[n]
            dqkv_ref[rows(c), 2 * d + h * DN_DK:2 * d + (h + 1) * DN_DK] = dvb[n] * bh[n]
        for h in hs:
            dstate[h] = ds_cur[h]
        valid = lane < heads
        dal_acc = jnp.zeros((SUBLANES, LANES), F32)
        ddt_acc = jnp.zeros((SUBLANES, LANES), F32)
        for c in range(grp):
            dgc_all = jnp.zeros((CHUNK, LANES), F32)
            dbeta_all = jnp.zeros((CHUNK, LANES), F32)
            for h in hs:
                n = c * heads + h
                dgc = rsum(e[n]) + rsum(dqg[n] * qg[n]) - tkd[n] + rsum(dkbg[n] * kbg[n])
                dgc = dgc + jnp.where(rowc == CHUNK - 1, dgl[n] + jnp.sum(tkd[n], axis=0, keepdims=True), 0.0)
                dgc_all = dgc_all + jnp.where(lane == h, dgc - colsum[n], 0.0)
                dbeta_all = dbeta_all + jnp.where(lane == h, rsum(dkb[n] * k[n]) + rsum(dvb[n] * v[n]), 0.0)
            dg = _dot(jnp.where(triu, 1.0, 0.0), dgc_all, NN, HIGHEST)
            beta_c = beta[rows(c)]
            dbl = jnp.where(valid, dbeta_all * beta_c * (1.0 - beta_c), 0.0)
            dal = jnp.where(valid, -dg * ea * sig_a[rows(c)], 0.0)
            dba_ref[rows(c), :LANES] = dbl.astype(dba_ref.dtype)
            dba_ref[rows(c), LANES:] = dal.astype(dba_ref.dtype)
            dal_acc = dal_acc + _fold8(jnp.where(valid, dg * g[rows(c)], 0.0))
            ddt_acc = ddt_acc + _fold8(dal)
        dal_ref[...] += dal_acc
        ddt_ref[...] += ddt_acc

    res = pl.pallas_call(
        _carrying(compute, 7, 4, 1, carried, n_chunks // grp),
        name="dn_bwd_carrying" if carried else "dn_bwd", grid=(n_chunks // grp,),
        in_specs=[pl.BlockSpec((span, 3 * d), lambda i: (rev(i), 0)),
                  pl.BlockSpec((span, 2 * LANES), lambda i: (rev(i), 0)),
                  pl.BlockSpec((1, LANES), lambda i: (0, 0)),
                  pl.BlockSpec((1, LANES), lambda i: (0, 0)),
                  pl.BlockSpec((span, d), lambda i: (rev(i), 0)),
                  pl.BlockSpec((grp, heads, DN_DK, DN_DK), lambda i: (rev(i), 0, 0, 0)),
                  pl.BlockSpec((heads, span, CHUNK), lambda i: (0, rev(i), 0))] + [ANY] * len(extra.inputs),
        out_specs=[pl.BlockSpec((span, 3 * d), lambda i: (rev(i), 0)),
                   pl.BlockSpec((span, 2 * LANES), lambda i: (rev(i), 0)),
                   pl.BlockSpec((SUBLANES, LANES), lambda i: (0, 0)),
                   pl.BlockSpec((SUBLANES, LANES), lambda i: (0, 0))] + [ANY] * len(extra.out_shapes),
        out_shape=[jax.ShapeDtypeStruct((t, 3 * d), F32),
                   jax.ShapeDtypeStruct((t, 2 * LANES), ACT),
                   jax.ShapeDtypeStruct((SUBLANES, LANES), F32),
                   jax.ShapeDtypeStruct((SUBLANES, LANES), F32)] + extra.out_shapes,
        scratch_shapes=[pltpu.VMEM((heads, DN_DK, DN_DK), F32)] + (extra.scratch() if carried else []),
        compiler_params=_cparams(("arbitrary",)),
    )(qkv, ba, alog, dtb, dout, states, ycors, *extra.inputs)
    return res[:4], res[4:]


def _sgu_mask():
    row = lax.broadcasted_iota(jnp.int32, (SGU_BLOCK, SGU_BLOCK), 0)
    col = lax.broadcasted_iota(jnp.int32, (SGU_BLOCK, SGU_BLOCK), 1)
    sh = int(math.log2(CHUNK))
    return lax.shift_right_logical(row, sh) >= lax.shift_right_logical(col, sh)


def _gate_sgu_fwd(o, projm, onw, lng, lnb, ws, bst, d):
    t = o.shape[0]
    heads, groups = d // DN_DK, d // SGU_GROUP_DIM
    tb = SGU_BLOCK
    row_spec = pl.BlockSpec((1, d), lambda i: (0, 0))

    def body(o_ref, z_ref, u_ref, v_ref, onw_ref, lng_ref, lnb_ref, ws_ref, bst_ref, ya_ref, yb_ref):
        for h in range(heads):
            cols = slice(h * DN_DK, (h + 1) * DN_DK)
            oh, zh = o_ref[:, cols], z_ref[:, cols]
            r = lax.rsqrt(jnp.mean(oh * oh, axis=1, keepdims=True) + RMS_EPS)
            ya_ref[:, cols] = (oh * r * onw_ref[:, cols] * (zh * _sigmoid(zh))).astype(ya_ref.dtype)
        xhat, _ = _ln_hat(_gelu(v_ref[...]))
        vgn = xhat * lng_ref[...] + lnb_ref[...]
        mask = _sgu_mask()
        lane = lax.broadcasted_iota(jnp.int32, (SGU_BLOCK, LANES), 1)
        bst_v = bst_ref[...]
        for gi in range(groups):
            cols = slice(gi * SGU_GROUP_DIM, (gi + 1) * SGU_GROUP_DIM)
            wsg = jnp.where(mask, ws_ref[gi], 0.0)
            sp = _dot(wsg, vgn[:, cols]) + _col_of(bst_v, lane, gi)
            yb_ref[:, cols] = (_gelu(u_ref[:, cols]) * sp).astype(yb_ref.dtype)

    return pl.pallas_call(
        body, name="gate_sgu_fwd", grid=(t // tb,),
        in_specs=[pl.BlockSpec((tb, d), lambda i: (i, 0)),
                  pl.BlockSpec((tb, d), lambda i: (i, 3)),
                  pl.BlockSpec((tb, d), lambda i: (i, 4)),
                  pl.BlockSpec((tb, d), lambda i: (i, 5)),
                  row_spec, row_spec, row_spec,
                  pl.BlockSpec((groups, SGU_BLOCK, SGU_BLOCK), lambda i: (0, 0, 0)),
                  pl.BlockSpec((SGU_BLOCK, LANES), lambda i: (0, 0))],
        out_specs=[pl.BlockSpec((tb, d), lambda i: (i, 0)), pl.BlockSpec((tb, d), lambda i: (i, 0))],
        out_shape=[jax.ShapeDtypeStruct((t, d), ACT), jax.ShapeDtypeStruct((t, d), ACT)],
        compiler_params=_cparams(("parallel",)),
    )(o, projm, projm, projm, onw, lng, lnb, ws, bst)


def _gate_sgu_bwd(dya, dyb, o, projm, onw, lng, lnb, ws, bst, dprojm, d):
    t = o.shape[0]
    heads, groups = d // DN_DK, d // SGU_GROUP_DIM
    tb = SGU_BLOCK
    row_spec = pl.BlockSpec((1, d), lambda i: (0, 0))
    acc_row = pl.BlockSpec((SUBLANES, d), lambda i: (0, 0))

    def body(dya_ref, dyb_ref, o_ref, z_ref, u_ref, v_ref, onw_ref, lng_ref, lnb_ref, ws_ref, bst_ref, alias_ref,
             do_ref, dp_ref, donw_ref, dlng_ref, dlnb_ref, dws_ref, dbst_ref):
        @pl.when(pl.program_id(0) == 0)
        def _():
            for r_ in (donw_ref, dlng_ref, dlnb_ref, dws_ref, dbst_ref):
                r_[...] = jnp.zeros_like(r_)

        donw = jnp.zeros((SUBLANES, DN_DK), F32)
        for h in range(heads):
            cols = slice(h * DN_DK, (h + 1) * DN_DK)
            oh, zh, dyah, wh = o_ref[:, cols], z_ref[:, cols], dya_ref[:, cols], onw_ref[:, cols]
            r = lax.rsqrt(jnp.mean(oh * oh, axis=1, keepdims=True) + RMS_EPS)
            on = oh * r
            sz = _sigmoid(zh)
            silu_z = zh * sz
            don = dyah * wh * silu_z
            dp_ref[:, cols] = (dyah * on * wh * (sz * (1.0 + zh * (1.0 - sz)))).astype(dp_ref.dtype)
            donw = donw + _fold8(dyah * on * silu_z)
            do_ref[:, cols] = r * (don - on * jnp.mean(don * on, axis=1, keepdims=True))
        donw_ref[...] += donw

        vgp, up = v_ref[...], u_ref[...]
        xhat, rstd = _ln_hat(_gelu(vgp))
        lng_v = lng_ref[...]
        vgn = xhat * lng_v + lnb_ref[...]
        ua = _gelu(up)
        mask = _sgu_mask()
        lane = lax.broadcasted_iota(jnp.int32, (SGU_BLOCK, LANES), 1)
        bst_v = bst_ref[...]
        dbst = jnp.zeros((SGU_BLOCK, LANES), F32)
        dvgn_parts, dua_parts = [], []
        for gi in range(groups):
            cols = slice(gi * SGU_GROUP_DIM, (gi + 1) * SGU_GROUP_DIM)
            wsg = jnp.where(mask, ws_ref[gi], 0.0)
            vg_g, dyb_g = vgn[:, cols], dyb_ref[:, cols]
            sp = _dot(wsg, vg_g) + _col_of(bst_v, lane, gi)
            dsp = dyb_g * ua[:, cols]
            dua_parts.append(dyb_g * sp)
            dws_ref[gi] += jnp.where(mask, _dot(dsp, vg_g, NT), 0.0)
            dbst = dbst + jnp.where(lane == gi, jnp.sum(dsp, axis=1, keepdims=True), 0.0)
            dvgn_parts.append(_dot(wsg, dsp, TN))
        dbst_ref[...] += dbst
        dvgn = jnp.concatenate(dvgn_parts, axis=1)
        dua = jnp.concatenate(dua_parts, axis=1)
        dlng_ref[...] += _fold8(dvgn * xhat)
        dlnb_ref[...] += _fold8(dvgn)
        dvga = _ln_bwd(dvgn * lng_v, xhat, rstd)
        dp_ref[:, d:2 * d] = (dua * _gelu_grad(up)).astype(dp_ref.dtype)
        dp_ref[:, 2 * d:] = (dvga * _gelu_grad(vgp)).astype(dp_ref.dtype)

    return pl.pallas_call(
        body, name="gate_sgu_bwd", grid=(t // tb,),
        in_specs=[pl.BlockSpec((tb, d), lambda i: (i, 0)),
                  pl.BlockSpec((tb, d), lambda i: (i, 0)),
                  pl.BlockSpec((tb, d), lambda i: (i, 0)),
                  pl.BlockSpec((tb, d), lambda i: (i, 3)),
                  pl.BlockSpec((tb, d), lambda i: (i, 4)),
                  pl.BlockSpec((tb, d), lambda i: (i, 5)),
                  row_spec, row_spec, row_spec,
                  pl.BlockSpec((groups, SGU_BLOCK, SGU_BLOCK), lambda i: (0, 0, 0)),
                  pl.BlockSpec((SGU_BLOCK, LANES), lambda i: (0, 0)),
                  ANY],
        out_specs=[pl.BlockSpec((tb, d), lambda i: (i, 0)),
                   pl.BlockSpec((tb, 3 * d), lambda i: (i, 1)),
                   pl.BlockSpec((SUBLANES, DN_DK), lambda i: (0, 0)),
                   acc_row, acc_row,
                   pl.BlockSpec((groups, SGU_BLOCK, SGU_BLOCK), lambda i: (0, 0, 0)),
                   pl.BlockSpec((SGU_BLOCK, LANES), lambda i: (0, 0))],
        out_shape=[jax.ShapeDtypeStruct((t, d), F32),
                   jax.ShapeDtypeStruct(dprojm.shape, dprojm.dtype),
                   jax.ShapeDtypeStruct((SUBLANES, DN_DK), F32),
                   jax.ShapeDtypeStruct((SUBLANES, d), F32),
                   jax.ShapeDtypeStruct((SUBLANES, d), F32),
                   jax.ShapeDtypeStruct((groups, SGU_BLOCK, SGU_BLOCK), F32),
                   jax.ShapeDtypeStruct((SGU_BLOCK, LANES), F32)],
        input_output_aliases={11: 1},
        compiler_params=_cparams(("arbitrary",)),
    )(dya, dyb, o, projm, projm, projm, onw, lng, lnb, ws, bst, dprojm)


def _mix_fwd(ya, yb, projm, x, wpa, wpb, wo, g1, b1, d, tb):
    t = x.shape[0]
    blk = pl.BlockSpec((tb, d), lambda i: (i, 0))
    wspec = pl.BlockSpec((d, d), lambda i: (0, 0))
    row_spec = pl.BlockSpec((1, d), lambda i: (0, 0))

    def body(ya_ref, yb_ref, ga_ref, gb_ref, x_ref, wpa_ref, wpb_ref, wo_ref, g_ref, b_ref,
             pa_ref, pb_ref, m_ref, h_ref, x1_ref, x1b_ref):
        pa = _dot(ya_ref[...], wpa_ref[...])
        pb = _dot(yb_ref[...], wpb_ref[...])
        m = _sigmoid(ga_ref[...]) * pa + _sigmoid(gb_ref[...]) * pb
        hres = ALPHA * x_ref[...] + _dot(m, wo_ref[...])
        xhat, _ = _ln_hat(hres)
        x1 = xhat * g_ref[...] + b_ref[...]
        pa_ref[...] = pa
        pb_ref[...] = pb
        m_ref[...] = m.astype(m_ref.dtype)
        h_ref[...] = hres
        x1_ref[...] = x1
        x1b_ref[...] = x1.astype(x1b_ref.dtype)

    f32_out = jax.ShapeDtypeStruct((t, d), F32)
    bf_out = jax.ShapeDtypeStruct((t, d), ACT)
    return pl.pallas_call(
        body, name="mix_fwd", grid=(t // tb,),
        in_specs=[blk, blk, pl.BlockSpec((tb, d), lambda i: (i, 6)), pl.BlockSpec((tb, d), lambda i: (i, 7)),
                  blk, wspec, wspec, wspec, row_spec, row_spec],
        out_specs=[blk] * 6,
        out_shape=[f32_out, f32_out, bf_out, f32_out, f32_out, bf_out],
        compiler_params=_cparams(("parallel",)),
    )(ya, yb, projm, projm, x, wpa, wpb, wo, g1, b1)


def _mix_bwd(dmix, pa, pb, projm, wpa, wpb, wo, d, tb):
    t = dmix.shape[0]
    blk = pl.BlockSpec((tb, d), lambda i: (i, 0))
    wspec = pl.BlockSpec((d, d), lambda i: (0, 0))

    def body(dmix_ref, pa_ref, pb_ref, ga_ref, gb_ref, wpa_ref, wpb_ref, wo_ref,
             dpa_ref, dpb_ref, dya_ref, dyb_ref, dg_ref):
        dm = _dot(dmix_ref[...], wo_ref[...], NT)
        sa, sb = _sigmoid(ga_ref[...]), _sigmoid(gb_ref[...])
        dpa, dpb = dm * sa, dm * sb
        dpa_ref[...] = dpa.astype(dpa_ref.dtype)
        dpb_ref[...] = dpb.astype(dpb_ref.dtype)
        dg_ref[:, :d] = (dm * pa_ref[...] * sa * (1.0 - sa)).astype(dg_ref.dtype)
        dg_ref[:, d:] = (dm * pb_ref[...] * sb * (1.0 - sb)).astype(dg_ref.dtype)
        dya_ref[...] = _dot(dpa, wpa_ref[...], NT)
        dyb_ref[...] = _dot(dpb, wpb_ref[...], NT)

    return pl.pallas_call(
        body, name="mix_bwd", grid=(t // tb,),
        in_specs=[blk, blk, blk, pl.BlockSpec((tb, d), lambda i: (i, 6)), pl.BlockSpec((tb, d), lambda i: (i, 7)),
                  wspec, wspec, wspec],
        out_specs=[blk, blk, blk, blk, pl.BlockSpec((tb, 2 * d), lambda i: (i, 3))],
        out_shape=[jax.ShapeDtypeStruct((t, d), ACT), jax.ShapeDtypeStruct((t, d), ACT),
                   jax.ShapeDtypeStruct((t, d), F32), jax.ShapeDtypeStruct((t, d), F32),
                   jax.ShapeDtypeStruct((t, 8 * d), ACT)],
        compiler_params=_cparams(("parallel",)),
    )(dmix, pa, pb, projm, projm, wpa, wpb, wo)


def _ffn_tail_fwd(gu, wd, x1, g, b, tb):
    t, d = x1.shape
    f = wd.shape[0]
    fc = _tile(f, MM_TILE)
    blk = pl.BlockSpec((tb, d), lambda i: (i, 0))
    row_spec = pl.BlockSpec((1, d), lambda i: (0, 0))

    def body(gu_ref, wd_ref, x_ref, g_ref, b_ref, a_ref, h_ref, y_ref, yb_ref):
        ffn = jnp.zeros((tb, d), F32)
        for c in range(f // fc):
            gp = gu_ref[:, c * fc:(c + 1) * fc]
            act = (gp * _sigmoid(gp) * gu_ref[:, f + c * fc:f + (c + 1) * fc]).astype(a_ref.dtype)
            a_ref[:, c * fc:(c + 1) * fc] = act
            ffn = ffn + _dot(act, wd_ref[c * fc:(c + 1) * fc, :])
        hres = ALPHA * x_ref[...] + ffn
        xhat, _ = _ln_hat(hres)
        y = xhat * g_ref[...] + b_ref[...]
        h_ref[...] = hres
        y_ref[...] = y
        yb_ref[...] = y.astype(yb_ref.dtype)

    return pl.pallas_call(
        body, name="ffn_tail_fwd", grid=(t // tb,),
        in_specs=[pl.BlockSpec((tb, 2 * f), lambda i: (i, 0)), pl.BlockSpec((f, d), lambda i: (0, 0)),
                  blk, row_spec, row_spec],
        out_specs=[pl.BlockSpec((tb, f), lambda i: (i, 0)), blk, blk, blk],
        out_shape=[jax.ShapeDtypeStruct((t, f), ACT), jax.ShapeDtypeStruct((t, d), F32),
                   jax.ShapeDtypeStruct((t, d), F32), jax.ShapeDtypeStruct((t, d), ACT)],
        compiler_params=_cparams(("parallel",)),
    )(gu, wd, x1, g, b)


def _ffn_tail_bwd(dh, wd, gu, tb):
    t, d = dh.shape
    f = wd.shape[0]
    fc = _tile(f, MM_TILE)

    def body(dh_ref, wd_ref, gu_ref, dgu_ref):
        dh_v = dh_ref[...]
        for c in range(f // fc):
            da = _dot(dh_v, wd_ref[c * fc:(c + 1) * fc, :], NT)
            gp = gu_ref[:, c * fc:(c + 1) * fc]
            sg = _sigmoid(gp)
            dgu_ref[:, c * fc:(c + 1) * fc] = (
                da * gu_ref[:, f + c * fc:f + (c + 1) * fc] * sg * (1.0 + gp * (1.0 - sg))).astype(dgu_ref.dtype)
            dgu_ref[:, f + c * fc:f + (c + 1) * fc] = (da * gp * sg).astype(dgu_ref.dtype)

    return pl.pallas_call(
        body, name="ffn_tail_bwd", grid=(t // tb,),
        in_specs=[pl.BlockSpec((tb, d), lambda i: (i, 0)), pl.BlockSpec((f, d), lambda i: (0, 0)),
                  pl.BlockSpec((tb, 2 * f), lambda i: (i, 0))],
        out_specs=pl.BlockSpec((tb, 2 * f), lambda i: (i, 0)),
        out_shape=jax.ShapeDtypeStruct((t, 2 * f), ACT),
        compiler_params=_cparams(("parallel",)),
    )(dh, wd, gu)


def _ffn_head_bwd(dgu, wgu, dh2, hres, g, tb):
    t, d = dh2.shape
    f2 = wgu.shape[1]
    blk = pl.BlockSpec((tb, d), lambda i: (i, 0))
    acc = pl.BlockSpec((SUBLANES, d), lambda i: (0, 0))

    def body(dgu_ref, w_ref, dh2_ref, h_ref, g_ref, dh_ref, dhb_ref, dg_ref, db_ref):
        @pl.when(pl.program_id(0) == 0)
        def _():
            dg_ref[...] = jnp.zeros_like(dg_ref)
            db_ref[...] = jnp.zeros_like(db_ref)

        dy_v = _dot(dgu_ref[...], w_ref[...], NT) + ALPHA * dh2_ref[...]
        xhat, r = _ln_hat(h_ref[...])
        dh = _ln_bwd(dy_v * g_ref[...], xhat, r)
        dh_ref[...] = dh
        dhb_ref[...] = dh.astype(dhb_ref.dtype)
        dg_ref[...] += _fold8(dy_v * xhat)
        db_ref[...] += _fold8(dy_v)

    return pl.pallas_call(
        body, name="ffn_head_bwd", grid=(t // tb,),
        in_specs=[pl.BlockSpec((tb, f2), lambda i: (i, 0)), pl.BlockSpec((d, f2), lambda i: (0, 0)),
                  blk, blk, pl.BlockSpec((1, d), lambda i: (0, 0))],
        out_specs=[blk, blk, acc, acc],
        out_shape=[jax.ShapeDtypeStruct((t, d), F32), jax.ShapeDtypeStruct((t, d), ACT),
                   jax.ShapeDtypeStruct((SUBLANES, d), F32), jax.ShapeDtypeStruct((SUBLANES, d), F32)],
        compiler_params=_cparams(("arbitrary",)),
    )(dgu, wgu, dh2, hres, g)


def _ln_bwd_call(dy, hres, g, tb):
    t, d = dy.shape
    blk = pl.BlockSpec((tb, d), lambda i: (i, 0))
    acc = pl.BlockSpec((SUBLANES, d), lambda i: (0, 0))

    def body(dy_ref, h_ref, g_ref, dh_ref, dhb_ref, dg_ref, db_ref):
        @pl.when(pl.program_id(0) == 0)
        def _():
            dg_ref[...] = jnp.zeros_like(dg_ref)
            db_ref[...] = jnp.zeros_like(db_ref)

        dy_v = dy_ref[...]
        xhat, r = _ln_hat(h_ref[...])
        dh = _ln_bwd(dy_v * g_ref[...], xhat, r)
        dh_ref[...] = dh
        dhb_ref[...] = dh.astype(dhb_ref.dtype)
        dg_ref[...] += _fold8(dy_v * xhat)
        db_ref[...] += _fold8(dy_v)

    return pl.pallas_call(
        body, name="ln_bwd", grid=(t // tb,),
        in_specs=[blk, blk, pl.BlockSpec((1, d), lambda i: (0, 0))],
        out_specs=[blk, blk, acc, acc],
        out_shape=[jax.ShapeDtypeStruct((t, d), F32), jax.ShapeDtypeStruct((t, d), ACT),
                   jax.ShapeDtypeStruct((SUBLANES, d), F32), jax.ShapeDtypeStruct((SUBLANES, d), F32)],
        compiler_params=_cparams(("arbitrary",)),
    )(dy, hres, g)


def _loss_head(y, target, tb):
    t, d = y.shape
    blk = pl.BlockSpec((tb, d), lambda i: (i, 0))

    def body(y_ref, t_ref, dy_ref, l_ref):
        @pl.when(pl.program_id(0) == 0)
        def _():
            l_ref[...] = jnp.zeros_like(l_ref)

        err = y_ref[...] - t_ref[...]
        dy_ref[...] = err * (1.0 / d)
        sq = _fold8(err * err)
        part = sq[:, :LANES]
        for c in range(1, d // LANES):
            part = part + sq[:, c * LANES:(c + 1) * LANES]
        l_ref[...] += part

    return pl.pallas_call(
        body, name="loss_head", grid=(t // tb,),
        in_specs=[blk, blk],
        out_specs=[blk, pl.BlockSpec((SUBLANES, LANES), lambda i: (0, 0))],
        out_shape=[jax.ShapeDtypeStruct((t, d), F32), jax.ShapeDtypeStruct((SUBLANES, LANES), F32)],
        compiler_params=_cparams(("arbitrary",)),
    )(y, target)


def _adamw(w, g, m, v):
    shape = w.shape
    cols = shape[-1]
    w2, g2, m2, v2 = (a.reshape(-1, cols) for a in (w, g, m, v))
    rows = w2.shape[0]
    tr = _tile(rows, 256, SUBLANES)
    blk = pl.BlockSpec((tr, cols), lambda i: (i, 0))

    def body(w_ref, g_ref, m_ref, v_ref, d_ref, nm_ref, nv_ref):
        g_v = g_ref[...]
        nm = ADAM_B1 * m_ref[...] + (1.0 - ADAM_B1) * g_v
        nv = ADAM_B2 * v_ref[...] + (1.0 - ADAM_B2) * (g_v * g_v)
        m_hat = nm / (1.0 - ADAM_B1 ** ADAM_STEP)
        v_hat = nv / (1.0 - ADAM_B2 ** ADAM_STEP)
        d_ref[...] = -ADAM_LR * (m_hat / (jnp.sqrt(v_hat) + ADAM_EPS) + ADAM_WD * w_ref[...])
        nm_ref[...] = nm
        nv_ref[...] = nv

    out = jax.ShapeDtypeStruct((rows, cols), F32)
    res = pl.pallas_call(
        body, name="adamw", grid=(rows // tr,),
        in_specs=[blk] * 4, out_specs=[blk] * 3, out_shape=[out] * 3,
        compiler_params=_cparams(("parallel",)),
    )(w2, g2, m2, v2)
    return tuple(r.reshape(shape) for r in res)


def _place():
    x, y, c = lax.axis_index("x"), lax.axis_index("y"), lax.axis_index("c")
    return x, y, c, [(1 - x, y), (x, 1 - y), (1 - x, 1 - y)]


def _remote(src, dst, send_sems, recv_sems, k, to):
    return pltpu.make_async_remote_copy(src_ref=src, dst_ref=dst, send_sem=send_sems.at[k],
                                        recv_sem=recv_sems.at[k], device_id=to, device_id_type=MESH)


class _Carried:
    def __init__(self, inputs, out_shapes, n_sems, copies):
        self.inputs, self.out_shapes, self.n_sems, self.copies = list(inputs), list(out_shapes), n_sems, copies

    def scratch(self):
        return [pltpu.SemaphoreType.DMA((self.n_sems,)), pltpu.SemaphoreType.DMA((self.n_sems,))]


def _run_comm(name, plan):
    n_in, n_out = len(plan.inputs), len(plan.out_shapes)

    def body(*refs):
        start, finish = plan.copies(refs[:n_in], refs[n_in:n_in + n_out], refs[-2], refs[-1])
        start()
        finish()

    return pl.pallas_call(
        body, name=name, in_specs=[ANY] * n_in, out_specs=[ANY] * n_out, out_shape=plan.out_shapes,
        scratch_shapes=plan.scratch(),
    )(*plan.inputs)


def _half_rows(rows, core):
    if rows % (4 * SUBLANES):
        return None
    return pl.ds(pl.multiple_of(core * (rows // 2), 2 * SUBLANES), rows // 2)


def _all_gather_plan(shards):
    n = len(shards)

    def copies(x_refs, out_refs, send_sems, recv_sems):
        x, y, c, chips = _place()
        sibling = (x, y, 1 - c)
        mine = 2 * x + y
        split = [_half_rows(x_refs[t].shape[0], c) is not None for t in range(n)]

        def src(t):
            return x_refs[t].at[_half_rows(x_refs[t].shape[0], c)] if split[t] else x_refs[t]

        def slot(t, chip_idx, core):
            rows = _half_rows(x_refs[t].shape[0], core)
            return out_refs[t].at[chip_idx, rows] if split[t] else out_refs[t].at[chip_idx]

        def first():
            return [_remote(src(t), slot(t, mine, c), send_sems, recv_sems, 6 * t + j, (cx, cy, c))
                    for j, (cx, cy) in enumerate(chips) for t in range(n)]

        def start():
            for cp in first():
                cp.start()

        def finish():
            passed = []
            for j, (cx, cy) in enumerate(chips):
                for t in range(n):
                    theirs = slot(t, 2 * cx + cy, c)
                    _remote(theirs, theirs, send_sems, recv_sems, 6 * t + j, (cx, cy, c)).wait_recv()
                    if split[t]:
                        fwd = _remote(theirs, theirs, send_sems, recv_sems, 6 * t + 3 + j, sibling)
                        fwd.start()
                        passed.append(fwd)
            for j, (cx, cy) in enumerate(chips):
                for t in range(n):
                    if split[t]:
                        other = slot(t, 2 * cx + cy, 1 - c)
                        _remote(other, other, send_sems, recv_sems, 6 * t + 3 + j, sibling).wait_recv()
            for cp in first() + passed:
                cp.wait_send()

        return start, finish

    return _Carried(shards, [jax.ShapeDtypeStruct((N_CHIPS,) + s.shape, s.dtype) for s in shards], 6 * n, copies)


def _sibling_exchange_plan(grads, small=None):
    n = len(grads)
    extra = [] if small is None else [small]

    def copies(in_refs, out_refs, send_sems, recv_sems):
        x, y, c, _ = _place()
        sibling = (x, y, 1 - c)

        def all_copies():
            cps = [_remote(in_refs[t].at[:, _half_rows(in_refs[t].shape[1], 1 - c), :], out_refs[t],
                           send_sems, recv_sems, t, sibling) for t in range(n)]
            if extra:
                cps.append(_remote(in_refs[n], out_refs[n], send_sems, recv_sems, n, sibling))
            return cps

        def start():
            for cp in all_copies():
                cp.start()

        def finish():
            for cp in all_copies():
                cp.wait()

        return start, finish

    shapes = [jax.ShapeDtypeStruct((g.shape[0], g.shape[1] // 2, g.shape[2]), g.dtype) for g in grads]
    shapes += [jax.ShapeDtypeStruct(s.shape, s.dtype) for s in extra]
    return _Carried(list(grads) + extra, shapes, n + 1, copies)


def _chip_exchange_plan(travel, small=None):
    n = len(travel)
    extra = [] if small is None else [small]

    def copies(in_refs, out_refs, send_sems, recv_sems):
        x, y, c, chips = _place()
        mine = 2 * x + y

        def all_copies():
            cps = []
            for j, (cx, cy) in enumerate(chips):
                to = (cx, cy, c)
                for t in range(n):
                    cps.append(_remote(in_refs[t].at[2 * cx + cy], out_refs[t].at[mine], send_sems, recv_sems,
                                       3 * t + j, to))
                if extra:
                    cps.append(_remote(in_refs[n], out_refs[n].at[mine], send_sems, recv_sems, 3 * n + j, to))
            return cps

        def start():
            for cp in all_copies():
                cp.start()

        def finish():
            for cp in all_copies():
                cp.wait()

        return start, finish

    shapes = [jax.ShapeDtypeStruct(g.shape, g.dtype) for g in travel]
    shapes += [jax.ShapeDtypeStruct((N_CHIPS,) + s.shape, s.dtype) for s in extra]
    return _Carried(list(travel) + extra, shapes, 3 * n + 3, copies)


def _sibling_merge_plan(reduced):
    n = len(reduced)

    def copies(in_refs, out_refs, send_sems, recv_sems):
        x, y, c, _ = _place()

        def all_copies():
            return [_remote(in_refs[t], out_refs[t], send_sems, recv_sems, t, (x, y, 1 - c)) for t in range(n)]

        def start():
            for cp in all_copies():
                cp.start()

        def finish():
            for cp in all_copies():
                cp.wait()

        return start, finish

    return _Carried(reduced, [jax.ShapeDtypeStruct(r.shape, r.dtype) for r in reduced], n, copies)


def _pair_sum(place, grad, land):
    n, r, c = grad.shape
    half = r // 2
    tr = _tile(half, 256, SUBLANES)
    nb = half // tr

    def body(place_ref, a_ref, b_ref, travel_ref, own_ref):
        total = a_ref[0] + b_ref[0]
        travel_ref[0] = total.astype(travel_ref.dtype)

        @pl.when(pl.program_id(1) == place_ref[1])
        def _():
            own_ref[...] = total

    return pl.pallas_call(
        body, name="grad_pair_sum",
        grid_spec=pltpu.PrefetchScalarGridSpec(
            num_scalar_prefetch=1, grid=(nb, n),
            in_specs=[pl.BlockSpec((1, tr, c), lambda i, s, p: (s, p[0] * nb + i, 0)),
                      pl.BlockSpec((1, tr, c), lambda i, s, p: (s, i, 0))],
            out_specs=[pl.BlockSpec((1, tr, c), lambda i, s, p: (s, i, 0)),
                       pl.BlockSpec((tr, c), lambda i, s, p: (i, 0))]),
        out_shape=[jax.ShapeDtypeStruct((n, half, c), BF16), jax.ShapeDtypeStruct((half, c), F32)],
        compiler_params=_cparams(("parallel", "arbitrary")),
    )(place, grad, land)


def _chip_sum(place, own, land, name):
    n, r, c = land.shape
    tr = _tile(r, 256, SUBLANES)

    def body(place_ref, own_ref, land_ref, o_ref):
        mine = place_ref[1]
        acc = jnp.zeros(o_ref.shape, F32)
        for s in range(n):
            acc = acc + jnp.where(mine == s, own_ref[...], land_ref[s].astype(F32))
        o_ref[...] = acc

    return pl.pallas_call(
        body, name=name,
        grid_spec=pltpu.PrefetchScalarGridSpec(
            num_scalar_prefetch=1, grid=(r // tr,),
            in_specs=[pl.BlockSpec((tr, c), lambda i, p: (i, 0)),
                      pl.BlockSpec((n, tr, c), lambda i, p: (0, i, 0))],
            out_specs=pl.BlockSpec((tr, c), lambda i, p: (i, 0))),
        out_shape=jax.ShapeDtypeStruct((r, c), F32),
        compiler_params=_cparams(("parallel",)),
    )(place, own, land)


def _add2(a, b):
    rows = a.shape[0]
    tr = _tile(rows, 256, SUBLANES)
    blk = pl.BlockSpec((tr, PACK_W), lambda i: (i, 0))

    def body(a_ref, b_ref, o_ref):
        o_ref[...] = a_ref[...] + b_ref[...]

    return pl.pallas_call(
        body, name="grad_small_pair_sum", grid=(rows // tr,), in_specs=[blk, blk], out_specs=blk,
        out_shape=jax.ShapeDtypeStruct(a.shape, F32), compiler_params=_cparams(("parallel",)),
    )(a, b)


def _merge_halves(place, mine, other):
    first_core = place[0] == 0
    return jnp.concatenate([jnp.where(first_core, mine, other), jnp.where(first_core, other, mine)], axis=0)


_BIG = (("w_in", 2), ("w_pa", 1), ("w_pb", 1), ("w_o", 1), ("w_ffn_gate", 2), ("w_ffn_up", 2),
        ("w_ffn_down", 1))
_SMALL = ("conv_w", "a_log", "dt_bias", "o_norm_w", "sgu_ln_g", "sgu_ln_b", "w_s", "b_s",
          "ln1_g", "ln1_b", "ln2_g", "ln2_b")


def _pad_rows(flat, mult):
    rows = -(-flat.shape[-1] // (PACK_W * mult)) * mult
    pad = rows * PACK_W - flat.shape[-1]
    flat = jnp.pad(flat, [(0, 0)] * (flat.ndim - 1) + [(0, pad)])
    return flat.reshape(flat.shape[:-1] + (rows, PACK_W))


def _unshard(gathered, local, chip, axis):
    parts = [jnp.where(chip == s, local, gathered[s]) for s in range(N_CHIPS)]
    return jnp.concatenate(parts, axis=axis - 1)


def _to_shards(full, axis):
    l, r, c = full.shape
    if axis == 1:
        return full.reshape(l, N_CHIPS, r // N_CHIPS, c)
    return jnp.transpose(full.reshape(l, r, N_CHIPS, c // N_CHIPS), (0, 2, 1, 3))


def _row(v, width=None):
    v = v.reshape(1, -1).astype(F32)
    if width is not None and v.shape[1] < width:
        v = jnp.pad(v, ((0, 0), (0, width - v.shape[1])))
    return v


def _layer_consts(p, l, d):
    heads = d // DN_DK
    return dict(
        alog=_row(p["a_log"][l], LANES), dtb=_row(p["dt_bias"][l], LANES),
        onw=_row(jnp.tile(p["o_norm_w"][l], heads)),
        lng=_row(p["sgu_ln_g"][l]), lnb=_row(p["sgu_ln_b"][l]),
        ws=p["w_s"][l].astype(F32),
        bst=jnp.pad(p["b_s"][l].T, ((0, 0), (0, LANES - p["b_s"].shape[1]))),
        g1=_row(p["ln1_g"][l]), b1=_row(p["ln1_b"][l]), g2=_row(p["ln2_g"][l]), b2=_row(p["ln2_b"][l]))


def _layer_fwd(x, xb, wl, cl, d, f, tb, carried=None):
    projm = _matmul(xb, wl["wm"], NN, "proj_main")
    ba = _matmul(xb, wl["wba"], NN, "proj_gates")
    qkv = _conv_fwd(projm, wl["conv"], d, _tile(x.shape[0], 2 * tb, SUBLANES))
    (o, states, ycors), carried_out = _dn_fwd(qkv, ba, cl["alog"], cl["dtb"], d, carried)
    ya, yb = _gate_sgu_fwd(o, projm, cl["onw"], cl["lng"], cl["lnb"], cl["ws"], cl["bst"], d)
    pa, pb, m, h1, x1, x1b = _mix_fwd(ya, yb, projm, x, wl["wpa"], wl["wpb"], wl["wo"], cl["g1"], cl["b1"], d, tb)
    gu = _matmul(x1b, wl["wgu"], NN, "ffn_in")
    act, h2, x2, x2b = _ffn_tail_fwd(gu, wl["wd"], x1, cl["g2"], cl["b2"], tb)
    saved = dict(xb=xb, projm=projm, ba=ba, qkv=qkv, o=o, states=states, ycors=ycors, ya=ya, yb=yb,
                 pa=pa, pb=pb, m=m, h1=h1, x1b=x1b, gu=gu, act=act, h2=h2)
    return x2, x2b, saved, carried_out


def _layer_bwd(dx2, sv, wl, cl, d, f, tb, carried=None):
    g = {}
    dh2, dh2b, dg2, db2 = _ln_bwd_call(dx2, sv["h2"], cl["g2"], tb)
    g["ln2_g"], g["ln2_b"] = dg2.sum(0), db2.sum(0)
    g["wd"] = _matmul(sv["act"], dh2b, TN, "ffn_out_dw")
    dgu = _ffn_tail_bwd(dh2b, wl["wd"], sv["gu"], tb)
    g["wgu"] = _matmul(sv["x1b"], dgu, TN, "ffn_in_dw")
    dh1, dh1b, dg1, db1 = _ffn_head_bwd(dgu, wl["wgu"], dh2, sv["h1"], cl["g1"], tb)
    g["ln1_g"], g["ln1_b"] = dg1.sum(0), db1.sum(0)
    g["wo"] = _matmul(sv["m"], dh1b, TN, "wo_dw")
    dpa, dpb, dya, dyb, dprojm = _mix_bwd(dh1b, sv["pa"], sv["pb"], sv["projm"], wl["wpa"], wl["wpb"], wl["wo"], d, tb)
    g["wpa"] = _matmul(sv["ya"], dpa, TN, "wpa_dw")
    g["wpb"] = _matmul(sv["yb"], dpb, TN, "wpb_dw")
    do, dprojm, donw, dlng, dlnb, dws, dbst = _gate_sgu_bwd(
        dya, dyb, sv["o"], sv["projm"], cl["onw"], cl["lng"], cl["lnb"], cl["ws"], cl["bst"], dprojm, d)
    heads, groups = d // DN_DK, d // SGU_GROUP_DIM
    g["o_norm_w"], g["sgu_ln_g"], g["sgu_ln_b"] = donw.sum(0), dlng.sum(0), dlnb.sum(0)
    g["w_s"], g["b_s"] = dws, dbst[:, :groups].T
    (dqkv, dba, dal, ddt), carried_out = _dn_bwd(sv["qkv"], sv["ba"], cl["alog"], cl["dtb"], do, sv["states"],
                                                 sv["ycors"], d, carried)
    g["a_log"], g["dt_bias"] = dal.sum(0)[:heads], ddt.sum(0)[:heads]
    tbc = _tile(dx2.shape[0], 2 * tb, SUBLANES)
    dy, dcw = _conv_bwd_dy(sv["projm"], wl["conv"], dqkv, d, tbc)
    g["conv_w"] = dcw.sum(1)
    dprojm = _conv_bwd_dx(dy, wl["conv"], dprojm, d, tbc)
    g["wm"] = _matmul(sv["xb"], dprojm, TN, "proj_main_dw")
    g["wba"] = _matmul(sv["xb"], dba, TN, "proj_gates_dw")
    dx = _matmul(dba, wl["wba"], NT, "proj_gates_dx", add=dh1, coef=ALPHA)
    dx = _matmul(dprojm, wl["wm"], NT, "proj_main_dx", add=dx)
    return dx, g, carried_out


def _layer_weights(full, d):
    heads, q4, w_in = d // DN_DK, 4 * d, full["w_in"]
    wba = jnp.zeros((d, 2 * LANES), w_in.dtype)
    wba = wba.at[:, :heads].set(w_in[:, q4:q4 + heads])
    wba = wba.at[:, LANES:LANES + heads].set(w_in[:, q4 + heads:q4 + 2 * heads])
    return dict(
        wm=jnp.concatenate([w_in[:, :q4], w_in[:, q4 + 2 * heads:]], axis=1), wba=wba,
        conv=full["conv_w"], wpa=full["w_pa"], wpb=full["w_pb"], wo=full["w_o"],
        wgu=jnp.concatenate([full["w_ffn_gate"], full["w_ffn_up"]], axis=1), wd=full["w_ffn_down"])


def _grad_shards(g, d, f):
    heads, q4 = d // DN_DK, 4 * d
    wsh, fs = 2 * d + heads // 2, f // N_CHIPS
    gm, gba, ggu = g["wm"], g["wba"], g["wgu"]
    rows = lambda a: a.reshape(N_CHIPS, -1, a.shape[1])
    out = dict(g)
    out.update({
        "w_in": jnp.stack([gm[:, :wsh],
                           jnp.concatenate([gm[:, wsh:q4], gba[:, :heads]], axis=1),
                           jnp.concatenate([gba[:, LANES:LANES + heads], gm[:, q4:q4 + wsh - heads]], axis=1),
                           gm[:, q4 + wsh - heads:]]),
        "w_pa": rows(g["wpa"]), "w_pb": rows(g["wpb"]), "w_o": rows(g["wo"]), "w_ffn_down": rows(g["wd"]),
        "w_ffn_gate": jnp.stack([ggu[:, s * fs:(s + 1) * fs] for s in range(N_CHIPS)]),
        "w_ffn_up": jnp.stack([ggu[:, f + s * fs:f + (s + 1) * fs] for s in range(N_CHIPS)])})
    return out


def _local_step(x, target, full0, full1_of, small_w, carry_fwd=None, carry_bwd_of=None):
    t, d = x.shape
    f = full0["w_ffn_gate"].shape[-1]
    tb = _tile(t, 256, SUBLANES)
    consts = [_layer_consts(small_w, l, d) for l in range(DEPTH)]
    w0 = _layer_weights(full0, d)
    x1, x1b, sv0, got = _layer_fwd(x, x.astype(ACT), w0, consts[0], d, f, tb, carry_fwd)
    w1 = _layer_weights(full1_of(got), d)
    x2, _, sv1, _ = _layer_fwd(x1, x1b, w1, consts[1], d, f, tb)
    dy, loss_parts = _loss_head(x2, target, tb)
    dy, g1, _ = _layer_bwd(dy, sv1, w1, consts[1], d, f, tb)
    g1 = _grad_shards(g1, d, f)
    dy, g0, got = _layer_bwd(dy, sv0, w0, consts[0], d, f, tb, carry_bwd_of(g1) if carry_bwd_of else None)
    return loss_parts, dy, [_grad_shards(g0, d, f), g1], got


def kernel(x, w_in, conv_w, a_log, dt_bias, o_norm_w, sgu_ln_g, sgu_ln_b, w_s, b_s, w_pa, w_pb, w_o, ln1_g, ln1_b, w_ffn_gate, w_ffn_up, w_ffn_down, ln2_g, ln2_b, loss_target, m_w_in, m_conv_w, m_a_log, m_dt_bias, m_o_norm_w, m_sgu_ln_g, m_sgu_ln_b, m_w_s, m_b_s, m_w_pa, m_w_pb, m_w_o, m_ln1_g, m_ln1_b, m_w_ffn_gate, m_w_ffn_up, m_w_ffn_down, m_ln2_g, m_ln2_b, v_w_in, v_conv_w, v_a_log, v_dt_bias, v_o_norm_w, v_sgu_ln_g, v_sgu_ln_b, v_w_s, v_b_s, v_w_pa, v_w_pb, v_w_o, v_ln1_g, v_ln1_b, v_w_ffn_gate, v_w_ffn_up, v_w_ffn_down, v_ln2_g, v_ln2_b):
    names = ("w_in", "conv_w", "a_log", "dt_bias", "o_norm_w", "sgu_ln_g", "sgu_ln_b", "w_s", "b_s", "w_pa",
             "w_pb", "w_o", "ln1_g", "ln1_b", "w_ffn_gate", "w_ffn_up", "w_ffn_down", "ln2_g", "ln2_b")
    w = dict(zip(names, (w_in, conv_w, a_log, dt_bias, o_norm_w, sgu_ln_g, sgu_ln_b, w_s, b_s, w_pa, w_pb, w_o,
                         ln1_g, ln1_b, w_ffn_gate, w_ffn_up, w_ffn_down, ln2_g, ln2_b)))
    mom = dict(zip(names, (m_w_in, m_conv_w, m_a_log, m_dt_bias, m_o_norm_w, m_sgu_ln_g, m_sgu_ln_b, m_w_s, m_b_s,
                           m_w_pa, m_w_pb, m_w_o, m_ln1_g, m_ln1_b, m_w_ffn_gate, m_w_ffn_up, m_w_ffn_down,
                           m_ln2_g, m_ln2_b)))
    var = dict(zip(names, (v_w_in, v_conv_w, v_a_log, v_dt_bias, v_o_norm_w, v_sgu_ln_g, v_sgu_ln_b, v_w_s, v_b_s,
                           v_w_pa, v_w_pb, v_w_o, v_ln1_g, v_ln1_b, v_w_ffn_gate, v_w_ffn_up, v_w_ffn_down,
                           v_ln2_g, v_ln2_b)))
    chip = 2 * lax.axis_index("x") + lax.axis_index("y")
    place = jnp.stack([lax.axis_index("c"), chip]).astype(jnp.int32)

    big = [k for k, _ in _BIG]
    local = {k: w[k].astype(BF16) for k in big}

    def shards_of(l):
        return [local[k][l] for k in big] + [conv_w[l]]

    def full_of(l, gathered):
        sh = shards_of(l)
        full = {k: _unshard(gt, lc, chip, axis) for (k, axis), gt, lc in zip(_BIG, gathered, sh)}
        full["conv_w"] = _unshard(gathered[-1], sh[-1], chip, 2)
        return full

    pairs = {}

    def carry_bwd_of(g1):
        lands = _run_comm("grad_sibling_exchange", _sibling_exchange_plan([g1[k] for k in big]))
        pairs[1] = [_pair_sum(place, g1[k], land) for k, land in zip(big, lands)]
        return _chip_exchange_plan([p[0] for p in pairs[1]])

    full0 = full_of(0, _run_comm("all_gather_weights", _all_gather_plan(shards_of(0))))
    small_w = {k: w[k] for k in _SMALL if k != "conv_w"}
    loss_parts, grad_x, g, lands1 = _local_step(
        x[0], loss_target[0], full0, lambda got: full_of(1, got), small_w,
        carry_fwd=_all_gather_plan(shards_of(1)), carry_bwd_of=carry_bwd_of)

    red1 = [_chip_sum(place, p[1], land, "grad_chip_sum") for p, land in zip(pairs[1], lands1)]
    small_g = {k: jnp.stack([g[l][k] for l in range(DEPTH)]) for k in _SMALL}
    small_sizes = [small_g[k].size for k in _SMALL]
    small = _pad_rows(jnp.concatenate([small_g[k].reshape(-1) for k in _SMALL]), SUBLANES)
    *lands, sland = _run_comm("grad_sibling_exchange_last", _sibling_exchange_plan([g[0][k] for k in big], small))
    pairs[0] = [_pair_sum(place, g[0][k], land) for k, land in zip(big, lands)]
    small_chip = _add2(small, sland)
    *lands0, sland2 = _run_comm("grad_chip_exchange_last", _chip_exchange_plan([p[0] for p in pairs[0]], small_chip))
    red0 = [_chip_sum(place, p[1], land, "grad_chip_sum") for p, land in zip(pairs[0], lands0)]
    small_total = _chip_sum(place, small_chip, sland2, "grad_small_chip_sum")
    others = _run_comm("grad_sibling_merge", _sibling_merge_plan(red0 + red1))
    halves = [_merge_halves(place, mine, other) for mine, other in zip(red0 + red1, others)]
    grads = {k: jnp.stack([halves[i], halves[len(big) + i]]) for i, k in enumerate(big)}
    small_total, off = small_total.reshape(-1), 0
    for k, n in zip(_SMALL, small_sizes):
        grads[k] = small_total[off:off + n].reshape(small_g[k].shape)
        off += n
    grads["conv_w"] = lax.dynamic_index_in_dim(_to_shards(grads["conv_w"], 2), chip, 1, keepdims=False)

    delta, new_m, new_v = {}, {}, {}
    for k in [k for k, _ in _BIG] + ["conv_w"]:
        delta[k], new_m[k], new_v[k] = _adamw(w[k], grads[k], mom[k], var[k])
    rep = [k for k in _SMALL if k != "conv_w"]
    pack = lambda dct: _pad_rows(jnp.concatenate([dct[k].reshape(-1) for k in rep]), SUBLANES)
    packed = _adamw(pack(w), pack(grads), pack(mom), pack(var))
    off = 0
    for k in rep:
        n = w[k].size
        for dst, src in zip((delta, new_m, new_v), packed):
            dst[k] = src.reshape(-1)[off:off + n].reshape(w[k].shape)
        off += n

    loss = 0.5 * lax.psum(jnp.sum(loss_parts), ("x", "y", "c")) / x.shape[-1]
    return (loss, grad_x[None], *[grads[k] for k in names], *[delta[k] for k in names],
            *[new_m[k] for k in names], *[new_v[k] for k in names])
```

```python
import math

import jax
import jax.numpy as jnp
from jax import lax
from jax.experimental import pallas as pl
from jax.experimental.pallas import tpu as pltpu

F32 = jnp.float32
BF16 = jnp.bfloat16
MXU_DTYPE = jnp.bfloat16
ACT = jnp.bfloat16
HIGHEST = lax.Precision.HIGHEST

DEPTH = 2
CHUNK = 64
DN_GROUP = 2
SGU_BLOCK = 128
CONV_K = 4
DN_DK = 128
SGU_GROUP_DIM = 128
LN_EPS = 1e-5
RMS_EPS = 1e-6
ALPHA = (2 * DEPTH) ** 0.25
ADAM_LR, ADAM_B1, ADAM_B2, ADAM_EPS, ADAM_WD, ADAM_STEP = 0.001, 0.9, 0.999, 1e-08, 0.01, 10

LANES = 128
SUBLANES = 8
VMEM_LIMIT = 52 * 2 ** 20
PACK_W = 1024
N_CHIPS = 4

NN = ((1,), (0,))
NT = ((1,), (1,))
TN = ((0,), (0,))
MESH = pl.DeviceIdType.MESH
ANY = pl.BlockSpec(memory_space=pl.ANY)


def _dot(a, b, dims=NN, prec=None):
    if prec is None:
        a = a.astype(MXU_DTYPE)
        b = b.astype(MXU_DTYPE)
    return lax.dot_general(a, b, (dims, ((), ())), preferred_element_type=F32, precision=prec)


def _cparams(sem=None):
    return pltpu.CompilerParams(dimension_semantics=sem, vmem_limit_bytes=VMEM_LIMIT)


def _tile(dim, pref, unit=LANES):
    t = (min(pref, dim) // unit) * unit
    while t >= unit:
        if dim % t == 0:
            return t
        t -= unit
    return dim


def _fold8(x):
    r, n = x.shape
    return x.reshape(r // SUBLANES, SUBLANES, n).sum(axis=0)


def _sigmoid(x):
    return 1.0 / (1.0 + jnp.exp(-x))


def _gelu(x):
    return 0.5 * x * (1.0 + lax.erf(x * (2.0 ** -0.5)))


def _gelu_grad(x):
    return 0.5 * (1.0 + lax.erf(x * (2.0 ** -0.5))) + x * jnp.exp(-0.5 * x * x) * (2.0 * math.pi) ** -0.5


def _ln_hat(h):
    mu = jnp.mean(h, axis=-1, keepdims=True)
    xc = h - mu
    var = jnp.mean(xc * xc, axis=-1, keepdims=True)
    r = lax.rsqrt(var + LN_EPS)
    return xc * r, r


def _ln_bwd(dxhat, xhat, r):
    return r * (dxhat - jnp.mean(dxhat, axis=-1, keepdims=True)
                - xhat * jnp.mean(dxhat * xhat, axis=-1, keepdims=True))


MM_TILE = 1536
MM_WIDE = 2048


def _matmul(a, b, dims, name, out_dtype=F32, add=None, coef=1.0, tm=MM_TILE, tn=MM_TILE, tk=MM_TILE, carried=None):
    if dims == NN:
        (m, k), n = a.shape, b.shape[1]
    elif dims == NT:
        (m, k), n = a.shape, b.shape[0]
    else:
        (k, m), n = a.shape, b.shape[1]
    tm, tn, tk = _tile(m, tm), _tile(n, tn), _tile(k, tk)
    nk = k // tk
    a_spec = pl.BlockSpec((tk, tm), lambda j, i, q: (q, i)) if dims == TN else pl.BlockSpec((tm, tk), lambda j, i, q: (i, q))
    b_spec = pl.BlockSpec((tn, tk), lambda j, i, q: (j, q)) if dims == NT else pl.BlockSpec((tk, tn), lambda j, i, q: (q, j))
    o_spec = pl.BlockSpec((tm, tn), lambda j, i, q: (i, j))
    has_add = add is not None

    def body(*refs):
        a_ref, b_ref = refs[0], refs[1]
        add_ref = refs[2] if has_add else None
        o_ref, acc_ref = refs[2 + has_add], refs[3 + has_add]
        q = pl.program_id(2)
        part = _dot(a_ref[...], b_ref[...], dims)

        def finish(r):
            if has_add:
                r = r + coef * add_ref[...]
            o_ref[...] = r.astype(out_dtype)

        if nk == 1:
            finish(part)
        else:
            @pl.when(q == 0)
            def _():
                acc_ref[...] = part

            @pl.when(q > 0)
            def _():
                acc_ref[...] += part

            @pl.when(q == nk - 1)
            def _():
                finish(acc_ref[...])

    ins = [a, b] + ([add] if has_add else [])
    in_specs = [a_spec, b_spec] + ([o_spec] if has_add else [])
    grid = (n // tn, m // tm, nk)
    acc = pltpu.VMEM((tm, tn) if nk > 1 else (SUBLANES, LANES), F32)
    out = jax.ShapeDtypeStruct((m, n), out_dtype)
    if carried is None:
        return pl.pallas_call(
            body, name=name, grid=grid, in_specs=in_specs, out_specs=o_spec, out_shape=out, scratch_shapes=[acc],
            compiler_params=_cparams(("parallel", "parallel", "arbitrary")),
        )(*ins)
    res = pl.pallas_call(
        _carrying(body, len(ins), 1, 1, carried, grid), name=name + "_carrying", grid=grid,
        in_specs=in_specs + [ANY] * len(carried.inputs), out_specs=[o_spec] + [ANY] * len(carried.out_shapes),
        out_shape=[out] + carried.out_shapes, scratch_shapes=[acc] + carried.scratch(),
        compiler_params=_cparams(("arbitrary", "arbitrary", "arbitrary")),
    )(*ins, *carried.inputs)
    return res[0], res[1:]


def _conv_taps(cur_ref, halo_ref, first):
    x = cur_ref[...]
    tb = x.shape[0]
    halo = jnp.where(first, 0.0, halo_ref[...])
    xc = jnp.concatenate([halo, x], axis=0)
    return [x] + [pltpu.roll(xc, s, 0)[SUBLANES:SUBLANES + tb] for s in range(1, CONV_K)]


def _conv_fwd(projm, conv_w, d, tb):
    t = projm.shape[0]
    heads = d // DN_DK
    hb = tb // SUBLANES

    def body(cur_ref, halo_ref, w_ref, o_ref):
        i, j = pl.program_id(0), pl.program_id(1)
        taps = _conv_taps(cur_ref, halo_ref, i == 0)
        y = taps[0] * w_ref[CONV_K - 1:CONV_K, :]
        for s in range(1, CONV_K):
            y = y + taps[s] * w_ref[CONV_K - 1 - s:CONV_K - s, :]
        act = y * _sigmoid(y)
        scale = jnp.where(j == 0, DN_DK ** -0.5, 1.0)
        for h in range(heads):
            seg = act[:, h * DN_DK:(h + 1) * DN_DK]
            r = lax.rsqrt(jnp.sum(seg * seg, axis=1, keepdims=True) + RMS_EPS) * scale
            o_ref[:, h * DN_DK:(h + 1) * DN_DK] = seg * jnp.where(j < 2, r, 1.0)

    return pl.pallas_call(
        body, name="conv_fwd", grid=(t // tb, 3),
        in_specs=[pl.BlockSpec((tb, d), lambda i, j: (i, j)),
                  pl.BlockSpec((SUBLANES, d), lambda i, j: (jnp.maximum(i * hb - 1, 0), j)),
                  pl.BlockSpec((CONV_K, d), lambda i, j: (0, j))],
        out_specs=pl.BlockSpec((tb, d), lambda i, j: (i, j)),
        out_shape=jax.ShapeDtypeStruct((t, 3 * d), F32),
        compiler_params=_cparams(("parallel", "parallel")),
    )(projm, projm, conv_w)


def _conv_bwd_dy(projm, conv_w, dqkv, d, tb):
    t = projm.shape[0]
    heads = d // DN_DK
    hb = tb // SUBLANES

    def body(cur_ref, halo_ref, w_ref, dout_ref, dy_ref, dw_ref):
        j, i = pl.program_id(0), pl.program_id(1)
        taps = _conv_taps(cur_ref, halo_ref, i == 0)
        y = taps[0] * w_ref[CONV_K - 1:CONV_K, :]
        for s in range(1, CONV_K):
            y = y + taps[s] * w_ref[CONV_K - 1 - s:CONV_K - s, :]
        sg = _sigmoid(y)
        act = y * sg
        dact = sg * (1.0 + y * (1.0 - sg))
        scale = jnp.where(j == 0, DN_DK ** -0.5, 1.0)
        for h in range(heads):
            cols = slice(h * DN_DK, (h + 1) * DN_DK)
            seg = act[:, cols]
            r = lax.rsqrt(jnp.sum(seg * seg, axis=1, keepdims=True) + RMS_EPS)
            nrm = seg * r
            dout = dout_ref[:, cols]
            dn = dout * scale
            ds = jnp.where(j < 2, r * (dn - nrm * jnp.sum(dn * nrm, axis=1, keepdims=True)), dout)
            dy_ref[:, cols] = ds * dact[:, cols]
        dy = dy_ref[...]

        @pl.when(i == 0)
        def _():
            dw_ref[...] = jnp.zeros_like(dw_ref)

        for s in range(CONV_K):
            dw_ref[CONV_K - 1 - s] += _fold8(dy * taps[s])

    return pl.pallas_call(
        body, name="conv_bwd_dy", grid=(3, t // tb),
        in_specs=[pl.BlockSpec((tb, d), lambda j, i: (i, j)),
                  pl.BlockSpec((SUBLANES, d), lambda j, i: (jnp.maximum(i * hb - 1, 0), j)),
                  pl.BlockSpec((CONV_K, d), lambda j, i: (0, j)),
                  pl.BlockSpec((tb, d), lambda j, i: (i, j))],
        out_specs=[pl.BlockSpec((tb, d), lambda j, i: (i, j)),
                   pl.BlockSpec((CONV_K, SUBLANES, d), lambda j, i: (0, 0, j))],
        out_shape=[jax.ShapeDtypeStruct((t, 3 * d), F32),
                   jax.ShapeDtypeStruct((CONV_K, SUBLANES, 3 * d), F32)],
        compiler_params=_cparams(("parallel", "arbitrary")),
    )(projm, projm, conv_w, dqkv)


def _conv_bwd_dx(dy, conv_w, dprojm, d, tb):
    t = dy.shape[0]
    hb = tb // SUBLANES
    last = t // tb - 1

    def body(cur_ref, halo_ref, w_ref, alias_ref, o_ref):
        i = pl.program_id(0)
        cur = cur_ref[...]
        halo = jnp.where(i == last, 0.0, halo_ref[...])
        dc = jnp.concatenate([cur, halo], axis=0)
        acc = cur * w_ref[CONV_K - 1:CONV_K, :]
        for s in range(1, CONV_K):
            acc = acc + pltpu.roll(dc, tb + SUBLANES - s, 0)[:tb] * w_ref[CONV_K - 1 - s:CONV_K - s, :]
        o_ref[...] = acc.astype(o_ref.dtype)

    return pl.pallas_call(
        body, name="conv_bwd_dx", grid=(t // tb, 3),
        in_specs=[pl.BlockSpec((tb, d), lambda i, j: (i, j)),
                  pl.BlockSpec((SUBLANES, d), lambda i, j: (jnp.minimum((i + 1) * hb, t // SUBLANES - 1), j)),
                  pl.BlockSpec((CONV_K, d), lambda i, j: (0, j)),
                  ANY],
        out_specs=pl.BlockSpec((tb, d), lambda i, j: (i, j)),
        out_shape=jax.ShapeDtypeStruct(dprojm.shape, dprojm.dtype),
        input_output_aliases={3: 0},
        compiler_params=_cparams(("parallel", "parallel")),
    )(dy, dy, conv_w, dprojm)


def _beta_g(ba, alog, dtb):
    beta = _sigmoid(ba[:, :LANES])
    xa = ba[:, LANES:] + dtb
    softplus = jnp.maximum(xa, 0.0) + jnp.log(1.0 + jnp.exp(-jnp.abs(xa)))
    ea = jnp.exp(alog)
    return beta, -ea * softplus, ea, _sigmoid(xa)


def _inv_corrections(mats):
    ys = [-a for a in mats]
    ps = [_dot(a, a) for a in mats]
    steps = int(math.log2(CHUNK)) - 1
    for it in range(steps):
        ys = [y + p + _dot(y, p) for y, p in zip(ys, ps)]
        if it < steps - 1:
            ps = [_dot(p, p) for p in ps]
    return ys


def _chunk_masks():
    row = lax.broadcasted_iota(jnp.int32, (CHUNK, CHUNK), 0)
    col = lax.broadcasted_iota(jnp.int32, (CHUNK, CHUNK), 1)
    return row >= col, row > col, row <= col


def _col_of(mat, lane_idx, h):
    return jnp.sum(jnp.where(lane_idx == h, mat, 0.0), axis=1, keepdims=True)


def _row_of(mat, sub_idx, h):
    return jnp.sum(jnp.where(sub_idx == h, mat, 0.0), axis=0, keepdims=True)


def _carrying(compute, n_in, n_out, n_scratch, carried, grid):
    if carried is None:
        return compute
    ci, co = len(carried.inputs), len(carried.out_shapes)

    def body(*refs):
        ins, c_in = refs[:n_in], refs[n_in:n_in + ci]
        outs, c_out = refs[n_in + ci:n_in + ci + n_out], refs[n_in + ci + n_out:n_in + ci + n_out + co]
        scratch = refs[n_in + ci + n_out + co:]
        start, finish = carried.copies(c_in, c_out, scratch[n_scratch], scratch[n_scratch + 1])
        first, last = True, True
        for axis, steps in enumerate(grid):
            first = jnp.logical_and(first, pl.program_id(axis) == 0)
            last = jnp.logical_and(last, pl.program_id(axis) == steps - 1)

        @pl.when(first)
        def _():
            start()

        compute(*ins, *outs, *scratch[:n_scratch])

        @pl.when(last)
        def _():
            finish()

    return body


def _dn_fwd(qkv, ba, alog, dtb, d, carried=None):
    t = qkv.shape[0]
    heads = d // DN_DK
    n_chunks = t // CHUNK
    grp = DN_GROUP if n_chunks % DN_GROUP == 0 else 1
    span = grp * CHUNK
    extra = carried or _Carried([], [], 0, None)

    def compute(qkv_ref, ba_ref, al_ref, dt_ref, o_ref, s_ref, y_ref, state):
        @pl.when(pl.program_id(0) == 0)
        def _():
            state[...] = jnp.zeros_like(state)

        tril, strict, _ = _chunk_masks()
        beta, g, _, _ = _beta_g(ba_ref[...], al_ref[...], dt_ref[...])
        lane = lax.broadcasted_iota(jnp.int32, (CHUNK, LANES), 1)
        sub = lax.broadcasted_iota(jnp.int32, (LANES, CHUNK), 0)
        rowc = lax.broadcasted_iota(jnp.int32, (CHUNK, 1), 0)
        hs = range(heads)
        units = [(c, h) for c in range(grp) for h in hs]
        un = range(len(units))
        rows = lambda c: slice(c * CHUNK, (c + 1) * CHUNK)
        gc = [_dot(jnp.where(tril, 1.0, 0.0), g[rows(c)], NN, HIGHEST) for c in range(grp)]
        gct = [m.T for m in gc]
        q = [qkv_ref[rows(c), h * DN_DK:(h + 1) * DN_DK] for c, h in units]
        k = [qkv_ref[rows(c), d + h * DN_DK:d + (h + 1) * DN_DK] for c, h in units]
        v = [qkv_ref[rows(c), 2 * d + h * DN_DK:2 * d + (h + 1) * DN_DK] for c, h in units]
        gch = [_col_of(gc[c], lane, h) for c, h in units]
        bh = [_col_of(beta[rows(c)], lane, h) for c, h in units]
        dec = [jnp.where(tril, jnp.exp(gch[n] - _row_of(gct[c], sub, h)), 0.0) for n, (c, h) in enumerate(units)]
        egc = [jnp.exp(gch[n]) for n in un]
        gl = [jnp.sum(jnp.where(rowc == CHUNK - 1, gch[n], 0.0), axis=0, keepdims=True) for n in un]
        kb = [k[n] * bh[n] for n in un]
        a = [jnp.where(strict, _dot(kb[n], k[n], NT) * dec[n], 0.0) for n in un]
        p = [_dot(q[n], k[n], NT) * dec[n] for n in un]
        ycor = _inv_corrections(a)
        rhs = [jnp.concatenate([v[n] * bh[n], kb[n] * egc[n]], axis=1) for n in un]
        sol = [rhs[n] + _dot(ycor[n], rhs[n]) for n in un]
        qg = [q[n] * egc[n] for n in un]
        kd = [k[n] * jnp.exp(gl[n] - gch[n]) for n in un]
        egl = [jnp.exp(gl[n]) for n in un]
        s_cur, s_in, o = [state[h] for h in hs], [], []
        for c in range(grp):
            ns = [c * heads + h for h in hs]
            vn = [sol[n][:, :DN_DK] - _dot(sol[n][:, DN_DK:], s_cur[h]) for h, n in enumerate(ns)]
            o += [_dot(qg[n], s_cur[h]) + _dot(p[n], vn[h]) for h, n in enumerate(ns)]
            s_in += s_cur
            s_cur = [s_cur[h] * egl[n] + _dot(kd[n], vn[h], TN) for h, n in enumerate(ns)]
        for n, (c, h) in enumerate(units):
            o_ref[rows(c), h * DN_DK:(h + 1) * DN_DK] = o[n]
            s_ref[c, h] = s_in[n]
            y_ref[h, rows(c), :] = ycor[n]
        for h in hs:
            state[h] = s_cur[h]

    res = pl.pallas_call(
        _carrying(compute, 4, 3, 1, carried, (n_chunks // grp,)),
        name="dn_fwd_carrying" if carried else "dn_fwd", grid=(n_chunks // grp,),
        in_specs=[pl.BlockSpec((span, 3 * d), lambda i: (i, 0)),
                  pl.BlockSpec((span, 2 * LANES), lambda i: (i, 0)),
                  pl.BlockSpec((1, LANES), lambda i: (0, 0)),
                  pl.BlockSpec((1, LANES), lambda i: (0, 0))] + [ANY] * len(extra.inputs),
        out_specs=[pl.BlockSpec((span, d), lambda i: (i, 0)),
                   pl.BlockSpec((grp, heads, DN_DK, DN_DK), lambda i: (i, 0, 0, 0)),
                   pl.BlockSpec((heads, span, CHUNK), lambda i: (0, i, 0))] + [ANY] * len(extra.out_shapes),
        out_shape=[jax.ShapeDtypeStruct((t, d), F32),
                   jax.ShapeDtypeStruct((n_chunks, heads, DN_DK, DN_DK), F32),
                   jax.ShapeDtypeStruct((heads, t, CHUNK), F32)] + extra.out_shapes,
        scratch_shapes=[pltpu.VMEM((heads, DN_DK, DN_DK), F32)] + (extra.scratch() if carried else []),
        compiler_params=_cparams(("arbitrary",)),
    )(qkv, ba, alog, dtb, *extra.inputs)
    return res[:3], res[3:]


def _dn_bwd(qkv, ba, alog, dtb, dout, states, ycors, d, carried=None):
    t = qkv.shape[0]
    heads = d // DN_DK
    n_chunks = t // CHUNK
    grp = DN_GROUP if n_chunks % DN_GROUP == 0 else 1
    span = grp * CHUNK
    rev = lambda i: n_chunks // grp - 1 - i
    extra = carried or _Carried([], [], 0, None)

    def compute(qkv_ref, ba_ref, al_ref, dt_ref, do_ref, s_ref, y_ref,
                dqkv_ref, dba_ref, dal_ref, ddt_ref, dstate):
        @pl.when(pl.program_id(0) == 0)
        def _():
            dstate[...] = jnp.zeros_like(dstate)
            dal_ref[...] = jnp.zeros_like(dal_ref)
            ddt_ref[...] = jnp.zeros_like(ddt_ref)

        tril, strict, triu = _chunk_masks()
        beta, g, ea, sig_a = _beta_g(ba_ref[...], al_ref[...], dt_ref[...])
        lane = lax.broadcasted_iota(jnp.int32, (CHUNK, LANES), 1)
        sub = lax.broadcasted_iota(jnp.int32, (LANES, CHUNK), 0)
        rowc = lax.broadcasted_iota(jnp.int32, (CHUNK, 1), 0)
        hs = range(heads)
        units = [(c, h) for c in range(grp) for h in hs]
        un = range(len(units))
        rows = lambda c: slice(c * CHUNK, (c + 1) * CHUNK)
        rsum = lambda x_: jnp.sum(x_, axis=1, keepdims=True)
        gc = [_dot(jnp.where(tril, 1.0, 0.0), g[rows(c)], NN, HIGHEST) for c in range(grp)]
        gct = [m.T for m in gc]
        q = [qkv_ref[rows(c), h * DN_DK:(h + 1) * DN_DK] for c, h in units]
        k = [qkv_ref[rows(c), d + h * DN_DK:d + (h + 1) * DN_DK] for c, h in units]
        v = [qkv_ref[rows(c), 2 * d + h * DN_DK:2 * d + (h + 1) * DN_DK] for c, h in units]
        dout_h = [do_ref[rows(c), h * DN_DK:(h + 1) * DN_DK] for c, h in units]
        s0 = [s_ref[c, h] for c, h in units]
        ycor = [y_ref[h, rows(c), :] for c, h in units]
        gch = [_col_of(gc[c], lane, h) for c, h in units]
        bh = [_col_of(beta[rows(c)], lane, h) for c, h in units]
        dec = [jnp.where(tril, jnp.exp(gch[n] - _row_of(gct[c], sub, h)), 0.0) for n, (c, h) in enumerate(units)]
        egc = [jnp.exp(gch[n]) for n in un]
        gl = [jnp.sum(jnp.where(rowc == CHUNK - 1, gch[n], 0.0), axis=0, keepdims=True) for n in un]
        egl = [jnp.exp(gl[n]) for n in un]
        ekd = [jnp.exp(gl[n] - gch[n]) for n in un]
        kb = [k[n] * bh[n] for n in un]
        kd = [k[n] * ekd[n] for n in un]
        qg = [q[n] * egc[n] for n in un]
        kbg = [kb[n] * egc[n] for n in un]
        a = [jnp.where(strict, _dot(kb[n], k[n], NT) * dec[n], 0.0) for n in un]
        p = [_dot(q[n], k[n], NT) * dec[n] for n in un]
        rhs = [jnp.concatenate([v[n] * bh[n], kbg[n]], axis=1) for n in un]
        sol = [rhs[n] + _dot(ycor[n], rhs[n]) for n in un]
        w = [sol[n][:, DN_DK:] for n in un]
        vn = [sol[n][:, :DN_DK] - _dot(w[n], s0[n]) for n in un]
        dqg = [_dot(dout_h[n], s0[n], NT) for n in un]
        dp = [jnp.where(tril, _dot(dout_h[n], vn[n], NT), 0.0) for n in un]
        pdo = [_dot(p[n], dout_h[n], TN) for n in un]
        qdo = [_dot(qg[n], dout_h[n], TN) for n in un]
        ds_cur = [dstate[h] for h in hs]
        dsn, dvn = [None] * len(units), [None] * len(units)
        for c in reversed(range(grp)):
            for h in hs:
                dsn[c * heads + h] = ds_cur[h]
            for h in hs:
                n = c * heads + h
                dvn[n] = pdo[n] + _dot(kd[n], ds_cur[h])
            ds_cur = [qdo[c * heads + h] + egl[c * heads + h] * ds_cur[h]
                      - _dot(w[c * heads + h], dvn[c * heads + h], TN) for h in hs]
        dkd = [_dot(vn[n], dsn[n], NT) for n in un]
        dw = [-_dot(dvn[n], s0[n], NT) for n in un]
        dgl = [jnp.sum(rsum(dsn[n] * s0[n]), axis=0, keepdims=True) * egl[n] for n in un]
        dsol = [jnp.concatenate([dvn[n], dw[n]], axis=1) for n in un]
        drhs = [dsol[n] + _dot(ycor[n], dsol[n], TN) for n in un]
        dvb = [drhs[n][:, :DN_DK] for n in un]
        dkbg = [drhs[n][:, DN_DK:] for n in un]
        da = [jnp.where(strict, -_dot(drhs[n], sol[n], NT), 0.0) for n in un]
        dma = [da[n] * dec[n] for n in un]
        dmp = [dp[n] * dec[n] for n in un]
        dkb = [_dot(dma[n], k[n]) + dkbg[n] * egc[n] for n in un]
        dq = [_dot(dmp[n], k[n]) + dqg[n] * egc[n] for n in un]
        dk = [_dot(dma[n], kb[n], TN) + _dot(dmp[n], q[n], TN) + dkd[n] * ekd[n] + dkb[n] * bh[n] for n in un]
        e = [da[n] * a[n] + dp[n] * p[n] for n in un]
        colsum = [jnp.sum(e[n], axis=0, keepdims=True) for n in un]
        tkd = [rsum(dkd[n] * kd[n]) for n in un]
        for n, (c, h) in enumerate(units):
            dqkv_ref[rows(c), h * DN_DK:(h + 1) * DN_DK] = dq[n]
            dqkv_ref[rows(c), d + h * DN_DK:d + (h + 1) * DN_DK] = dk[n]
            dqkv_ref[rows(c), 2 * d + h * DN_DK:2 * d + (h + 1) * DN_DK] = dvb[n] * bh[n]
        for h in hs:
            dstate[h] = ds_cur[h]
        valid = lane < heads
        dal_acc = jnp.zeros((SUBLANES, LANES), F32)
        ddt_acc = jnp.zeros((SUBLANES, LANES), F32)
        for c in range(grp):
            dgc_all = jnp.zeros((CHUNK, LANES), F32)
            dbeta_all = jnp.zeros((CHUNK, LANES), F32)
            colsums = jnp.zeros((LANES, CHUNK), F32)
            for h in hs:
                n = c * heads + h
                dgc = rsum(e[n]) + rsum(dqg[n] * qg[n]) - tkd[n] + rsum(dkbg[n] * kbg[n])
                dgc = dgc + jnp.where(rowc == CHUNK - 1, dgl[n] + jnp.sum(tkd[n], axis=0, keepdims=True), 0.0)
                dgc_all = dgc_all + jnp.where(lane == h, dgc, 0.0)
                colsums = colsums + jnp.where(sub == h, colsum[n], 0.0)
                dbeta_all = dbeta_all + jnp.where(lane == h, rsum(dkb[n] * k[n]) + rsum(dvb[n] * v[n]), 0.0)
            dg = _dot(jnp.where(triu, 1.0, 0.0), dgc_all - colsums.T, NN, HIGHEST)
            beta_c = beta[rows(c)]
            dbl = jnp.where(valid, dbeta_all * beta_c * (1.0 - beta_c), 0.0)
            dal = jnp.where(valid, -dg * ea * sig_a[rows(c)], 0.0)
            dba_ref[rows(c), :LANES] = dbl.astype(dba_ref.dtype)
            dba_ref[rows(c), LANES:] = dal.astype(dba_ref.dtype)
            dal_acc = dal_acc + _fold8(jnp.where(valid, dg * g[rows(c)], 0.0))
            ddt_acc = ddt_acc + _fold8(dal)
        dal_ref[...] += dal_acc
        ddt_ref[...] += ddt_acc

    res = pl.pallas_call(
        _carrying(compute, 7, 4, 1, carried, (n_chunks // grp,)),
        name="dn_bwd_carrying" if carried else "dn_bwd", grid=(n_chunks // grp,),
        in_specs=[pl.BlockSpec((span, 3 * d), lambda i: (rev(i), 0)),
                  pl.BlockSpec((span, 2 * LANES), lambda i: (rev(i), 0)),
                  pl.BlockSpec((1, LANES), lambda i: (0, 0)),
                  pl.BlockSpec((1, LANES), lambda i: (0, 0)),
                  pl.BlockSpec((span, d), lambda i: (rev(i), 0)),
                  pl.BlockSpec((grp, heads, DN_DK, DN_DK), lambda i: (rev(i), 0, 0, 0)),
                  pl.BlockSpec((heads, span, CHUNK), lambda i: (0, rev(i), 0))] + [ANY] * len(extra.inputs),
        out_specs=[pl.BlockSpec((span, 3 * d), lambda i: (rev(i), 0)),
                   pl.BlockSpec((span, 2 * LANES), lambda i: (rev(i), 0)),
                   pl.BlockSpec((SUBLANES, LANES), lambda i: (0, 0)),
                   pl.BlockSpec((SUBLANES, LANES), lambda i: (0, 0))] + [ANY] * len(extra.out_shapes),
        out_shape=[jax.ShapeDtypeStruct((t, 3 * d), F32),
                   jax.ShapeDtypeStruct((t, 2 * LANES), ACT),
                   jax.ShapeDtypeStruct((SUBLANES, LANES), F32),
                   jax.ShapeDtypeStruct((SUBLANES, LANES), F32)] + extra.out_shapes,
        scratch_shapes=[pltpu.VMEM((heads, DN_DK, DN_DK), F32)] + (extra.scratch() if carried else []),
        compiler_params=_cparams(("arbitrary",)),
    )(qkv, ba, alog, dtb, dout, states, ycors, *extra.inputs)
    return res[:4], res[4:]


def _sgu_mask():
    row = lax.broadcasted_iota(jnp.int32, (SGU_BLOCK, SGU_BLOCK), 0)
    col = lax.broadcasted_iota(jnp.int32, (SGU_BLOCK, SGU_BLOCK), 1)
    sh = int(math.log2(CHUNK))
    return lax.shift_right_logical(row, sh) >= lax.shift_right_logical(col, sh)


def _gate_sgu_fwd(o, projm, onw, lng, lnb, ws, bst, d):
    t = o.shape[0]
    heads, groups = d // DN_DK, d // SGU_GROUP_DIM
    tb = SGU_BLOCK
    row_spec = pl.BlockSpec((1, d), lambda i: (0, 0))

    def body(o_ref, z_ref, u_ref, v_ref, onw_ref, lng_ref, lnb_ref, ws_ref, bst_ref, ya_ref, yb_ref):
        for h in range(heads):
            cols = slice(h * DN_DK, (h + 1) * DN_DK)
            oh, zh = o_ref[:, cols], z_ref[:, cols]
            r = lax.rsqrt(jnp.mean(oh * oh, axis=1, keepdims=True) + RMS_EPS)
            ya_ref[:, cols] = (oh * r * onw_ref[:, cols] * (zh * _sigmoid(zh))).astype(ya_ref.dtype)
        xhat, _ = _ln_hat(_gelu(v_ref[...]))
        vgn = xhat * lng_ref[...] + lnb_ref[...]
        mask = _sgu_mask()
        lane = lax.broadcasted_iota(jnp.int32, (SGU_BLOCK, LANES), 1)
        bst_v = bst_ref[...]
        for gi in range(groups):
            cols = slice(gi * SGU_GROUP_DIM, (gi + 1) * SGU_GROUP_DIM)
            wsg = jnp.where(mask, ws_ref[gi], 0.0)
            sp = _dot(wsg, vgn[:, cols]) + _col_of(bst_v, lane, gi)
            yb_ref[:, cols] = (_gelu(u_ref[:, cols]) * sp).astype(yb_ref.dtype)

    return pl.pallas_call(
        body, name="gate_sgu_fwd", grid=(t // tb,),
        in_specs=[pl.BlockSpec((tb, d), lambda i: (i, 0)),
                  pl.BlockSpec((tb, d), lambda i: (i, 3)),
                  pl.BlockSpec((tb, d), lambda i: (i, 4)),
                  pl.BlockSpec((tb, d), lambda i: (i, 5)),
                  row_spec, row_spec, row_spec,
                  pl.BlockSpec((groups, SGU_BLOCK, SGU_BLOCK), lambda i: (0, 0, 0)),
                  pl.BlockSpec((SGU_BLOCK, LANES), lambda i: (0, 0))],
        out_specs=[pl.BlockSpec((tb, d), lambda i: (i, 0)), pl.BlockSpec((tb, d), lambda i: (i, 0))],
        out_shape=[jax.ShapeDtypeStruct((t, d), ACT), jax.ShapeDtypeStruct((t, d), ACT)],
        compiler_params=_cparams(("parallel",)),
    )(o, projm, projm, projm, onw, lng, lnb, ws, bst)


def _gate_sgu_bwd(dya, dyb, o, projm, onw, lng, lnb, ws, bst, dprojm, d):
    t = o.shape[0]
    heads, groups = d // DN_DK, d // SGU_GROUP_DIM
    tb = SGU_BLOCK
    row_spec = pl.BlockSpec((1, d), lambda i: (0, 0))
    acc_row = pl.BlockSpec((SUBLANES, d), lambda i: (0, 0))

    def body(dya_ref, dyb_ref, o_ref, z_ref, u_ref, v_ref, onw_ref, lng_ref, lnb_ref, ws_ref, bst_ref, alias_ref,
             do_ref, dp_ref, donw_ref, dlng_ref, dlnb_ref, dws_ref, dbst_ref):
        @pl.when(pl.program_id(0) == 0)
        def _():
            for r_ in (donw_ref, dlng_ref, dlnb_ref, dws_ref, dbst_ref):
                r_[...] = jnp.zeros_like(r_)

        donw = jnp.zeros((SUBLANES, DN_DK), F32)
        for h in range(heads):
            cols = slice(h * DN_DK, (h + 1) * DN_DK)
            oh, zh, dyah, wh = o_ref[:, cols], z_ref[:, cols], dya_ref[:, cols], onw_ref[:, cols]
            r = lax.rsqrt(jnp.mean(oh * oh, axis=1, keepdims=True) + RMS_EPS)
            on = oh * r
            sz = _sigmoid(zh)
            silu_z = zh * sz
            don = dyah * wh * silu_z
            dp_ref[:, cols] = (dyah * on * wh * (sz * (1.0 + zh * (1.0 - sz)))).astype(dp_ref.dtype)
            donw = donw + _fold8(dyah * on * silu_z)
            do_ref[:, cols] = r * (don - on * jnp.mean(don * on, axis=1, keepdims=True))
        donw_ref[...] += donw

        vgp, up = v_ref[...], u_ref[...]
        xhat, rstd = _ln_hat(_gelu(vgp))
        lng_v = lng_ref[...]
        vgn = xhat * lng_v + lnb_ref[...]
        ua = _gelu(up)
        mask = _sgu_mask()
        lane = lax.broadcasted_iota(jnp.int32, (SGU_BLOCK, LANES), 1)
        bst_v = bst_ref[...]
        dbst = jnp.zeros((SGU_BLOCK, LANES), F32)
        dvgn_parts, dua_parts = [], []
        for gi in range(groups):
            cols = slice(gi * SGU_GROUP_DIM, (gi + 1) * SGU_GROUP_DIM)
            wsg = jnp.where(mask, ws_ref[gi], 0.0)
            vg_g, dyb_g = vgn[:, cols], dyb_ref[:, cols]
            sp = _dot(wsg, vg_g) + _col_of(bst_v, lane, gi)
            dsp = dyb_g * ua[:, cols]
            dua_parts.append(dyb_g * sp)
            dws_ref[gi] += jnp.where(mask, _dot(dsp, vg_g, NT), 0.0)
            dbst = dbst + jnp.where(lane == gi, jnp.sum(dsp, axis=1, keepdims=True), 0.0)
            dvgn_parts.append(_dot(wsg, dsp, TN))
        dbst_ref[...] += dbst
        dvgn = jnp.concatenate(dvgn_parts, axis=1)
        dua = jnp.concatenate(dua_parts, axis=1)
        dlng_ref[...] += _fold8(dvgn * xhat)
        dlnb_ref[...] += _fold8(dvgn)
        dvga = _ln_bwd(dvgn * lng_v, xhat, rstd)
        dp_ref[:, d:2 * d] = (dua * _gelu_grad(up)).astype(dp_ref.dtype)
        dp_ref[:, 2 * d:] = (dvga * _gelu_grad(vgp)).astype(dp_ref.dtype)

    return pl.pallas_call(
        body, name="gate_sgu_bwd", grid=(t // tb,),
        in_specs=[pl.BlockSpec((tb, d), lambda i: (i, 0)),
                  pl.BlockSpec((tb, d), lambda i: (i, 0)),
                  pl.BlockSpec((tb, d), lambda i: (i, 0)),
                  pl.BlockSpec((tb, d), lambda i: (i, 3)),
                  pl.BlockSpec((tb, d), lambda i: (i, 4)),
                  pl.BlockSpec((tb, d), lambda i: (i, 5)),
                  row_spec, row_spec, row_spec,
                  pl.BlockSpec((groups, SGU_BLOCK, SGU_BLOCK), lambda i: (0, 0, 0)),
                  pl.BlockSpec((SGU_BLOCK, LANES), lambda i: (0, 0)),
                  ANY],
        out_specs=[pl.BlockSpec((tb, d), lambda i: (i, 0)),
                   pl.BlockSpec((tb, 3 * d), lambda i: (i, 1)),
                   pl.BlockSpec((SUBLANES, DN_DK), lambda i: (0, 0)),
                   acc_row, acc_row,
                   pl.BlockSpec((groups, SGU_BLOCK, SGU_BLOCK), lambda i: (0, 0, 0)),
                   pl.BlockSpec((SGU_BLOCK, LANES), lambda i: (0, 0))],
        out_shape=[jax.ShapeDtypeStruct((t, d), F32),
                   jax.ShapeDtypeStruct(dprojm.shape, dprojm.dtype),
                   jax.ShapeDtypeStruct((SUBLANES, DN_DK), F32),
                   jax.ShapeDtypeStruct((SUBLANES, d), F32),
                   jax.ShapeDtypeStruct((SUBLANES, d), F32),
                   jax.ShapeDtypeStruct((groups, SGU_BLOCK, SGU_BLOCK), F32),
                   jax.ShapeDtypeStruct((SGU_BLOCK, LANES), F32)],
        input_output_aliases={11: 1},
        compiler_params=_cparams(("arbitrary",)),
    )(dya, dyb, o, projm, projm, projm, onw, lng, lnb, ws, bst, dprojm)


def _mix_fwd(ya, yb, projm, x, wpa, wpb, wo, g1, b1, d, tb):
    t = x.shape[0]
    blk = pl.BlockSpec((tb, d), lambda i: (i, 0))
    wspec = pl.BlockSpec((d, d), lambda i: (0, 0))
    row_spec = pl.BlockSpec((1, d), lambda i: (0, 0))

    def body(ya_ref, yb_ref, ga_ref, gb_ref, x_ref, wpa_ref, wpb_ref, wo_ref, g_ref, b_ref,
             pa_ref, pb_ref, m_ref, h_ref, x1_ref, x1b_ref):
        pa = _dot(ya_ref[...], wpa_ref[...])
        pb = _dot(yb_ref[...], wpb_ref[...])
        m = _sigmoid(ga_ref[...]) * pa + _sigmoid(gb_ref[...]) * pb
        hres = ALPHA * x_ref[...] + _dot(m, wo_ref[...])
        xhat, _ = _ln_hat(hres)
        x1 = xhat * g_ref[...] + b_ref[...]
        pa_ref[...] = pa
        pb_ref[...] = pb
        m_ref[...] = m.astype(m_ref.dtype)
        h_ref[...] = hres
        x1_ref[...] = x1
        x1b_ref[...] = x1.astype(x1b_ref.dtype)

    f32_out = jax.ShapeDtypeStruct((t, d), F32)
    bf_out = jax.ShapeDtypeStruct((t, d), ACT)
    return pl.pallas_call(
        body, name="mix_fwd", grid=(t // tb,),
        in_specs=[blk, blk, pl.BlockSpec((tb, d), lambda i: (i, 6)), pl.BlockSpec((tb, d), lambda i: (i, 7)),
                  blk, wspec, wspec, wspec, row_spec, row_spec],
        out_specs=[blk] * 6,
        out_shape=[f32_out, f32_out, bf_out, f32_out, f32_out, bf_out],
        compiler_params=_cparams(("parallel",)),
    )(ya, yb, projm, projm, x, wpa, wpb, wo, g1, b1)


def _mix_bwd(dmix, pa, pb, projm, wpa, wpb, wo, d, tb):
    t = dmix.shape[0]
    blk = pl.BlockSpec((tb, d), lambda i: (i, 0))
    wspec = pl.BlockSpec((d, d), lambda i: (0, 0))

    def body(dmix_ref, pa_ref, pb_ref, ga_ref, gb_ref, wpa_ref, wpb_ref, wo_ref,
             dpa_ref, dpb_ref, dya_ref, dyb_ref, dg_ref):
        dm = _dot(dmix_ref[...], wo_ref[...], NT)
        sa, sb = _sigmoid(ga_ref[...]), _sigmoid(gb_ref[...])
        dpa, dpb = dm * sa, dm * sb
        dpa_ref[...] = dpa.astype(dpa_ref.dtype)
        dpb_ref[...] = dpb.astype(dpb_ref.dtype)
        dg_ref[:, :d] = (dm * pa_ref[...] * sa * (1.0 - sa)).astype(dg_ref.dtype)
        dg_ref[:, d:] = (dm * pb_ref[...] * sb * (1.0 - sb)).astype(dg_ref.dtype)
        dya_ref[...] = _dot(dpa, wpa_ref[...], NT)
        dyb_ref[...] = _dot(dpb, wpb_ref[...], NT)

    return pl.pallas_call(
        body, name="mix_bwd", grid=(t // tb,),
        in_specs=[blk, blk, blk, pl.BlockSpec((tb, d), lambda i: (i, 6)), pl.BlockSpec((tb, d), lambda i: (i, 7)),
                  wspec, wspec, wspec],
        out_specs=[blk, blk, blk, blk, pl.BlockSpec((tb, 2 * d), lambda i: (i, 3))],
        out_shape=[jax.ShapeDtypeStruct((t, d), ACT), jax.ShapeDtypeStruct((t, d), ACT),
                   jax.ShapeDtypeStruct((t, d), F32), jax.ShapeDtypeStruct((t, d), F32),
                   jax.ShapeDtypeStruct((t, 8 * d), ACT)],
        compiler_params=_cparams(("parallel",)),
    )(dmix, pa, pb, projm, projm, wpa, wpb, wo)


def _ffn_tail_fwd(gu, wd, x1, g, b, tb):
    t, d = x1.shape
    f = wd.shape[0]
    fc = _tile(f, MM_TILE)
    blk = pl.BlockSpec((tb, d), lambda i: (i, 0))
    row_spec = pl.BlockSpec((1, d), lambda i: (0, 0))

    def body(gu_ref, wd_ref, x_ref, g_ref, b_ref, a_ref, h_ref, y_ref, yb_ref):
        ffn = jnp.zeros((tb, d), F32)
        for c in range(f // fc):
            gp = gu_ref[:, c * fc:(c + 1) * fc]
            act = (gp * _sigmoid(gp) * gu_ref[:, f + c * fc:f + (c + 1) * fc]).astype(a_ref.dtype)
            a_ref[:, c * fc:(c + 1) * fc] = act
            ffn = ffn + _dot(act, wd_ref[c * fc:(c + 1) * fc, :])
        hres = ALPHA * x_ref[...] + ffn
        xhat, _ = _ln_hat(hres)
        y = xhat * g_ref[...] + b_ref[...]
        h_ref[...] = hres
        y_ref[...] = y
        yb_ref[...] = y.astype(yb_ref.dtype)

    return pl.pallas_call(
        body, name="ffn_tail_fwd", grid=(t // tb,),
        in_specs=[pl.BlockSpec((tb, 2 * f), lambda i: (i, 0)), pl.BlockSpec((f, d), lambda i: (0, 0)),
                  blk, row_spec, row_spec],
        out_specs=[pl.BlockSpec((tb, f), lambda i: (i, 0)), blk, blk, blk],
        out_shape=[jax.ShapeDtypeStruct((t, f), ACT), jax.ShapeDtypeStruct((t, d), F32),
                   jax.ShapeDtypeStruct((t, d), F32), jax.ShapeDtypeStruct((t, d), ACT)],
        compiler_params=_cparams(("parallel",)),
    )(gu, wd, x1, g, b)


def _ffn_tail_bwd(dh, wd, gu, tb):
    t, d = dh.shape
    f = wd.shape[0]
    fc = _tile(f, MM_TILE)

    def body(dh_ref, wd_ref, gu_ref, dgu_ref):
        dh_v = dh_ref[...]
        for c in range(f // fc):
            da = _dot(dh_v, wd_ref[c * fc:(c + 1) * fc, :], NT)
            gp = gu_ref[:, c * fc:(c + 1) * fc]
            sg = _sigmoid(gp)
            dgu_ref[:, c * fc:(c + 1) * fc] = (
                da * gu_ref[:, f + c * fc:f + (c + 1) * fc] * sg * (1.0 + gp * (1.0 - sg))).astype(dgu_ref.dtype)
            dgu_ref[:, f + c * fc:f + (c + 1) * fc] = (da * gp * sg).astype(dgu_ref.dtype)

    return pl.pallas_call(
        body, name="ffn_tail_bwd", grid=(t // tb,),
        in_specs=[pl.BlockSpec((tb, d), lambda i: (i, 0)), pl.BlockSpec((f, d), lambda i: (0, 0)),
                  pl.BlockSpec((tb, 2 * f), lambda i: (i, 0))],
        out_specs=pl.BlockSpec((tb, 2 * f), lambda i: (i, 0)),
        out_shape=jax.ShapeDtypeStruct((t, 2 * f), ACT),
        compiler_params=_cparams(("parallel",)),
    )(dh, wd, gu)


def _ffn_head_bwd(dgu, wgu, dh2, hres, g, tb):
    t, d = dh2.shape
    f2 = wgu.shape[1]
    blk = pl.BlockSpec((tb, d), lambda i: (i, 0))
    acc = pl.BlockSpec((SUBLANES, d), lambda i: (0, 0))

    def body(dgu_ref, w_ref, dh2_ref, h_ref, g_ref, dh_ref, dhb_ref, dg_ref, db_ref):
        @pl.when(pl.program_id(0) == 0)
        def _():
            dg_ref[...] = jnp.zeros_like(dg_ref)
            db_ref[...] = jnp.zeros_like(db_ref)

        dy_v = _dot(dgu_ref[...], w_ref[...], NT) + ALPHA * dh2_ref[...]
        xhat, r = _ln_hat(h_ref[...])
        dh = _ln_bwd(dy_v * g_ref[...], xhat, r)
        dh_ref[...] = dh
        dhb_ref[...] = dh.astype(dhb_ref.dtype)
        dg_ref[...] += _fold8(dy_v * xhat)
        db_ref[...] += _fold8(dy_v)

    return pl.pallas_call(
        body, name="ffn_head_bwd", grid=(t // tb,),
        in_specs=[pl.BlockSpec((tb, f2), lambda i: (i, 0)), pl.BlockSpec((d, f2), lambda i: (0, 0)),
                  blk, blk, pl.BlockSpec((1, d), lambda i: (0, 0))],
        out_specs=[blk, blk, acc, acc],
        out_shape=[jax.ShapeDtypeStruct((t, d), F32), jax.ShapeDtypeStruct((t, d), ACT),
                   jax.ShapeDtypeStruct((SUBLANES, d), F32), jax.ShapeDtypeStruct((SUBLANES, d), F32)],
        compiler_params=_cparams(("arbitrary",)),
    )(dgu, wgu, dh2, hres, g)


def _ln_bwd_call(dy, hres, g, tb):
    t, d = dy.shape
    blk = pl.BlockSpec((tb, d), lambda i: (i, 0))
    acc = pl.BlockSpec((SUBLANES, d), lambda i: (0, 0))

    def body(dy_ref, h_ref, g_ref, dh_ref, dhb_ref, dg_ref, db_ref):
        @pl.when(pl.program_id(0) == 0)
        def _():
            dg_ref[...] = jnp.zeros_like(dg_ref)
            db_ref[...] = jnp.zeros_like(db_ref)

        dy_v = dy_ref[...]
        xhat, r = _ln_hat(h_ref[...])
        dh = _ln_bwd(dy_v * g_ref[...], xhat, r)
        dh_ref[...] = dh
        dhb_ref[...] = dh.astype(dhb_ref.dtype)
        dg_ref[...] += _fold8(dy_v * xhat)
        db_ref[...] += _fold8(dy_v)

    return pl.pallas_call(
        body, name="ln_bwd", grid=(t // tb,),
        in_specs=[blk, blk, pl.BlockSpec((1, d), lambda i: (0, 0))],
        out_specs=[blk, blk, acc, acc],
        out_shape=[jax.ShapeDtypeStruct((t, d), F32), jax.ShapeDtypeStruct((t, d), ACT),
                   jax.ShapeDtypeStruct((SUBLANES, d), F32), jax.ShapeDtypeStruct((SUBLANES, d), F32)],
        compiler_params=_cparams(("arbitrary",)),
    )(dy, hres, g)


def _loss_head(y, target, tb):
    t, d = y.shape
    blk = pl.BlockSpec((tb, d), lambda i: (i, 0))

    def body(y_ref, t_ref, dy_ref, l_ref):
        @pl.when(pl.program_id(0) == 0)
        def _():
            l_ref[...] = jnp.zeros_like(l_ref)

        err = y_ref[...] - t_ref[...]
        dy_ref[...] = err * (1.0 / d)
        sq = _fold8(err * err)
        part = sq[:, :LANES]
        for c in range(1, d // LANES):
            part = part + sq[:, c * LANES:(c + 1) * LANES]
        l_ref[...] += part

    return pl.pallas_call(
        body, name="loss_head", grid=(t // tb,),
        in_specs=[blk, blk],
        out_specs=[blk, pl.BlockSpec((SUBLANES, LANES), lambda i: (0, 0))],
        out_shape=[jax.ShapeDtypeStruct((t, d), F32), jax.ShapeDtypeStruct((SUBLANES, LANES), F32)],
        compiler_params=_cparams(("arbitrary",)),
    )(y, target)


def _adamw(w, g, m, v):
    shape = w.shape
    cols = shape[-1]
    w2, g2, m2, v2 = (a.reshape(-1, cols) for a in (w, g, m, v))
    rows = w2.shape[0]
    tr = _tile(rows, 256, SUBLANES)
    blk = pl.BlockSpec((tr, cols), lambda i: (i, 0))

    def body(w_ref, g_ref, m_ref, v_ref, d_ref, nm_ref, nv_ref):
        g_v = g_ref[...]
        nm = ADAM_B1 * m_ref[...] + (1.0 - ADAM_B1) * g_v
        nv = ADAM_B2 * v_ref[...] + (1.0 - ADAM_B2) * (g_v * g_v)
        m_hat = nm / (1.0 - ADAM_B1 ** ADAM_STEP)
        v_hat = nv / (1.0 - ADAM_B2 ** ADAM_STEP)
        d_ref[...] = -ADAM_LR * (m_hat / (jnp.sqrt(v_hat) + ADAM_EPS) + ADAM_WD * w_ref[...])
        nm_ref[...] = nm
        nv_ref[...] = nv

    out = jax.ShapeDtypeStruct((rows, cols), F32)
    res = pl.pallas_call(
        body, name="adamw", grid=(rows // tr,),
        in_specs=[blk] * 4, out_specs=[blk] * 3, out_shape=[out] * 3,
        compiler_params=_cparams(("parallel",)),
    )(w2, g2, m2, v2)
    return tuple(r.reshape(shape) for r in res)


def _place():
    x, y, c = lax.axis_index("x"), lax.axis_index("y"), lax.axis_index("c")
    return x, y, c, [(1 - x, y), (x, 1 - y), (1 - x, 1 - y)]


def _remote(src, dst, send_sems, recv_sems, k, to):
    return pltpu.make_async_remote_copy(src_ref=src, dst_ref=dst, send_sem=send_sems.at[k],
                                        recv_sem=recv_sems.at[k], device_id=to, device_id_type=MESH)


class _Carried:
    def __init__(self, inputs, out_shapes, n_sems, copies):
        self.inputs, self.out_shapes, self.n_sems, self.copies = list(inputs), list(out_shapes), n_sems, copies

    def scratch(self):
        return [pltpu.SemaphoreType.DMA((self.n_sems,)), pltpu.SemaphoreType.DMA((self.n_sems,))]


def _run_comm(name, plan):
    n_in, n_out = len(plan.inputs), len(plan.out_shapes)

    def body(*refs):
        start, finish = plan.copies(refs[:n_in], refs[n_in:n_in + n_out], refs[-2], refs[-1])
        start()
        finish()

    return pl.pallas_call(
        body, name=name, in_specs=[ANY] * n_in, out_specs=[ANY] * n_out, out_shape=plan.out_shapes,
        scratch_shapes=plan.scratch(),
    )(*plan.inputs)


def _half_rows(rows, core):
    if rows % (4 * SUBLANES):
        return None
    return pl.ds(pl.multiple_of(core * (rows // 2), 2 * SUBLANES), rows // 2)


def _all_gather_plan(shards):
    n = len(shards)

    def copies(x_refs, out_refs, send_sems, recv_sems):
        x, y, c, chips = _place()
        sibling = (x, y, 1 - c)
        mine = 2 * x + y
        split = [_half_rows(x_refs[t].shape[0], c) is not None for t in range(n)]

        def src(t):
            return x_refs[t].at[_half_rows(x_refs[t].shape[0], c)] if split[t] else x_refs[t]

        def slot(t, chip_idx, core):
            rows = _half_rows(x_refs[t].shape[0], core)
            return out_refs[t].at[chip_idx, rows] if split[t] else out_refs[t].at[chip_idx]

        def first():
            return [_remote(src(t), slot(t, mine, c), send_sems, recv_sems, 6 * t + j, (cx, cy, c))
                    for j, (cx, cy) in enumerate(chips) for t in range(n)]

        def start():
            for cp in first():
                cp.start()

        def finish():
            passed = []
            for j, (cx, cy) in enumerate(chips):
                for t in range(n):
                    theirs = slot(t, 2 * cx + cy, c)
                    _remote(theirs, theirs, send_sems, recv_sems, 6 * t + j, (cx, cy, c)).wait_recv()
                    if split[t]:
                        fwd = _remote(theirs, theirs, send_sems, recv_sems, 6 * t + 3 + j, sibling)
                        fwd.start()
                        passed.append(fwd)
            for j, (cx, cy) in enumerate(chips):
                for t in range(n):
                    if split[t]:
                        other = slot(t, 2 * cx + cy, 1 - c)
                        _remote(other, other, send_sems, recv_sems, 6 * t + 3 + j, sibling).wait_recv()
            for cp in first() + passed:
                cp.wait_send()

        return start, finish

    return _Carried(shards, [jax.ShapeDtypeStruct((N_CHIPS,) + s.shape, s.dtype) for s in shards], 6 * n, copies)


def _sibling_exchange_plan(grads, small=None):
    n = len(grads)
    extra = [] if small is None else [small]

    def copies(in_refs, out_refs, send_sems, recv_sems):
        x, y, c, _ = _place()
        sibling = (x, y, 1 - c)

        def all_copies():
            cps = [_remote(in_refs[t].at[:, _half_rows(in_refs[t].shape[1], 1 - c), :], out_refs[t],
                           send_sems, recv_sems, t, sibling) for t in range(n)]
            if extra:
                cps.append(_remote(in_refs[n], out_refs[n], send_sems, recv_sems, n, sibling))
            return cps

        def start():
            for cp in all_copies():
                cp.start()

        def finish():
            for cp in all_copies():
                cp.wait()

        return start, finish

    shapes = [jax.ShapeDtypeStruct((g.shape[0], g.shape[1] // 2, g.shape[2]), g.dtype) for g in grads]
    shapes += [jax.ShapeDtypeStruct(s.shape, s.dtype) for s in extra]
    return _Carried(list(grads) + extra, shapes, n + 1, copies)


def _chip_exchange_plan(travel, small=None):
    n = len(travel)
    extra = [] if small is None else [small]

    def copies(in_refs, out_refs, send_sems, recv_sems):
        x, y, c, chips = _place()
        mine = 2 * x + y

        def all_copies():
            cps = []
            for j, (cx, cy) in enumerate(chips):
                to = (cx, cy, c)
                for t in range(n):
                    cps.append(_remote(in_refs[t].at[2 * cx + cy], out_refs[t].at[mine], send_sems, recv_sems,
                                       3 * t + j, to))
                if extra:
                    cps.append(_remote(in_refs[n], out_refs[n].at[mine], send_sems, recv_sems, 3 * n + j, to))
            return cps

        def start():
            for cp in all_copies():
                cp.start()

        def finish():
            for cp in all_copies():
                cp.wait()

        return start, finish

    shapes = [jax.ShapeDtypeStruct(g.shape, g.dtype) for g in travel]
    shapes += [jax.ShapeDtypeStruct((N_CHIPS,) + s.shape, s.dtype) for s in extra]
    return _Carried(list(travel) + extra, shapes, 3 * n + 3, copies)


def _sibling_merge_plan(reduced):
    n = len(reduced)

    def copies(in_refs, out_refs, send_sems, recv_sems):
        x, y, c, _ = _place()

        def all_copies():
            return [_remote(in_refs[t], out_refs[t], send_sems, recv_sems, t, (x, y, 1 - c)) for t in range(n)]

        def start():
            for cp in all_copies():
                cp.start()

        def finish():
            for cp in all_copies():
                cp.wait()

        return start, finish

    return _Carried(reduced, [jax.ShapeDtypeStruct(r.shape, r.dtype) for r in reduced], n, copies)


def _pair_sum(place, grad, land):
    n, r, c = grad.shape
    half = r // 2
    tr = _tile(half, 256, SUBLANES)
    nb = half // tr

    def body(place_ref, a_ref, b_ref, travel_ref, own_ref):
        total = a_ref[0] + b_ref[0]
        travel_ref[0] = total.astype(travel_ref.dtype)

        @pl.when(pl.program_id(1) == place_ref[1])
        def _():
            own_ref[...] = total

    return pl.pallas_call(
        body, name="grad_pair_sum",
        grid_spec=pltpu.PrefetchScalarGridSpec(
            num_scalar_prefetch=1, grid=(nb, n),
            in_specs=[pl.BlockSpec((1, tr, c), lambda i, s, p: (s, p[0] * nb + i, 0)),
                      pl.BlockSpec((1, tr, c), lambda i, s, p: (s, i, 0))],
            out_specs=[pl.BlockSpec((1, tr, c), lambda i, s, p: (s, i, 0)),
                       pl.BlockSpec((tr, c), lambda i, s, p: (i, 0))]),
        out_shape=[jax.ShapeDtypeStruct((n, half, c), BF16), jax.ShapeDtypeStruct((half, c), F32)],
        compiler_params=_cparams(("parallel", "arbitrary")),
    )(place, grad, land)


def _chip_sum(place, own, land, name):
    n, r, c = land.shape
    tr = _tile(r, 256, SUBLANES)

    def body(place_ref, own_ref, land_ref, o_ref):
        mine = place_ref[1]
        acc = jnp.zeros(o_ref.shape, F32)
        for s in range(n):
            acc = acc + jnp.where(mine == s, own_ref[...], land_ref[s].astype(F32))
        o_ref[...] = acc

    return pl.pallas_call(
        body, name=name,
        grid_spec=pltpu.PrefetchScalarGridSpec(
            num_scalar_prefetch=1, grid=(r // tr,),
            in_specs=[pl.BlockSpec((tr, c), lambda i, p: (i, 0)),
                      pl.BlockSpec((n, tr, c), lambda i, p: (0, i, 0))],
            out_specs=pl.BlockSpec((tr, c), lambda i, p: (i, 0))),
        out_shape=jax.ShapeDtypeStruct((r, c), F32),
        compiler_params=_cparams(("parallel",)),
    )(place, own, land)


def _add2(a, b):
    rows = a.shape[0]
    tr = _tile(rows, 256, SUBLANES)
    blk = pl.BlockSpec((tr, PACK_W), lambda i: (i, 0))

    def body(a_ref, b_ref, o_ref):
        o_ref[...] = a_ref[...] + b_ref[...]

    return pl.pallas_call(
        body, name="grad_small_pair_sum", grid=(rows // tr,), in_specs=[blk, blk], out_specs=blk,
        out_shape=jax.ShapeDtypeStruct(a.shape, F32), compiler_params=_cparams(("parallel",)),
    )(a, b)


def _merge_halves(place, mine, other):
    first_core = place[0] == 0
    return jnp.concatenate([jnp.where(first_core, mine, other), jnp.where(first_core, other, mine)], axis=0)


_BIG = (("w_in", 2), ("w_pa", 1), ("w_pb", 1), ("w_o", 1), ("w_ffn_gate", 2), ("w_ffn_up", 2),
        ("w_ffn_down", 1))
_SMALL = ("conv_w", "a_log", "dt_bias", "o_norm_w", "sgu_ln_g", "sgu_ln_b", "w_s", "b_s",
          "ln1_g", "ln1_b", "ln2_g", "ln2_b")


def _pad_rows(flat, mult):
    rows = -(-flat.shape[-1] // (PACK_W * mult)) * mult
    pad = rows * PACK_W - flat.shape[-1]
    flat = jnp.pad(flat, [(0, 0)] * (flat.ndim - 1) + [(0, pad)])
    return flat.reshape(flat.shape[:-1] + (rows, PACK_W))


def _unshard(gathered, local, chip, axis):
    parts = [jnp.where(chip == s, local, gathered[s]) for s in range(N_CHIPS)]
    return jnp.concatenate(parts, axis=axis - 1)


def _to_shards(full, axis):
    l, r, c = full.shape
    if axis == 1:
        return full.reshape(l, N_CHIPS, r // N_CHIPS, c)
    return jnp.transpose(full.reshape(l, r, N_CHIPS, c // N_CHIPS), (0, 2, 1, 3))


def _row(v, width=None):
    v = v.reshape(1, -1).astype(F32)
    if width is not None and v.shape[1] < width:
        v = jnp.pad(v, ((0, 0), (0, width - v.shape[1])))
    return v


def _layer_consts(p, l, d):
    heads = d // DN_DK
    return dict(
        alog=_row(p["a_log"][l], LANES), dtb=_row(p["dt_bias"][l], LANES),
        onw=_row(jnp.tile(p["o_norm_w"][l], heads)),
        lng=_row(p["sgu_ln_g"][l]), lnb=_row(p["sgu_ln_b"][l]),
        ws=p["w_s"][l].astype(F32),
        bst=jnp.pad(p["b_s"][l].T, ((0, 0), (0, LANES - p["b_s"].shape[1]))),
        g1=_row(p["ln1_g"][l]), b1=_row(p["ln1_b"][l]), g2=_row(p["ln2_g"][l]), b2=_row(p["ln2_b"][l]))


class _NoComm:
    def with_proj_main(self):
        return None

    def after_proj_main(self, got):
        pass

    def weights(self, full):
        return full

    def with_dn_fwd(self):
        return None

    def after_dn_fwd(self, got):
        pass

    def with_ffn_in_dw(self):
        return None

    def after_ffn_in_dw(self, got):
        pass

    def after_branch_grads(self, g):
        pass

    def with_dn_bwd(self):
        return None

    def after_dn_bwd(self, got):
        pass

    def with_proj_main_dw(self):
        return None

    def after_proj_main_dw(self, got):
        pass


def _carry(carried, after, call, *args, **kw):
    if carried is None:
        return call(*args, **kw)
    out, got = call(*args, carried=carried, **kw)
    after(got)
    return out


def _in_proj_weights(w_in, d):
    heads, q4 = d // DN_DK, 4 * d
    wba = jnp.zeros((d, 2 * LANES), w_in.dtype)
    wba = wba.at[:, :heads].set(w_in[:, q4:q4 + heads])
    wba = wba.at[:, LANES:LANES + heads].set(w_in[:, q4 + heads:q4 + 2 * heads])
    return jnp.concatenate([w_in[:, :q4], w_in[:, q4 + 2 * heads:]], axis=1), wba


def _layer_fwd(x, xb, full, cl, d, tb, comm):
    wm, wba = _in_proj_weights(full["w_in"], d)
    projm = _carry(comm.with_proj_main(), comm.after_proj_main, _matmul, xb, wm, NN, "proj_main", tn=MM_WIDE)
    full = comm.weights(full)
    wl = dict(wm=wm, wba=wba, conv=full["conv_w"], wpa=full["w_pa"], wpb=full["w_pb"], wo=full["w_o"],
              wgu=jnp.concatenate([full["w_ffn_gate"], full["w_ffn_up"]], axis=1), wd=full["w_ffn_down"])
    ba = _matmul(xb, wba, NN, "proj_gates")
    qkv = _conv_fwd(projm, wl["conv"], d, _tile(x.shape[0], 2 * tb, SUBLANES))
    (o, states, ycors), got = _dn_fwd(qkv, ba, cl["alog"], cl["dtb"], d, comm.with_dn_fwd())
    comm.after_dn_fwd(got)
    ya, yb = _gate_sgu_fwd(o, projm, cl["onw"], cl["lng"], cl["lnb"], cl["ws"], cl["bst"], d)
    pa, pb, m, h1, x1, x1b = _mix_fwd(ya, yb, projm, x, wl["wpa"], wl["wpb"], wl["wo"], cl["g1"], cl["b1"], d, tb)
    gu = _matmul(x1b, wl["wgu"], NN, "ffn_in")
    act, h2, x2, x2b = _ffn_tail_fwd(gu, wl["wd"], x1, cl["g2"], cl["b2"], tb)
    saved = dict(xb=xb, projm=projm, ba=ba, qkv=qkv, o=o, states=states, ycors=ycors, ya=ya, yb=yb,
                 pa=pa, pb=pb, m=m, h1=h1, x1b=x1b, gu=gu, act=act, h2=h2)
    return x2, x2b, saved, wl


def _layer_bwd(dx2, sv, wl, cl, d, tb, comm):
    g = {}
    dh2, dh2b, dg2, db2 = _ln_bwd_call(dx2, sv["h2"], cl["g2"], tb)
    g["ln2_g"], g["ln2_b"] = dg2.sum(0), db2.sum(0)
    g["wd"] = _matmul(sv["act"], dh2b, TN, "ffn_out_dw")
    dgu = _ffn_tail_bwd(dh2b, wl["wd"], sv["gu"], tb)
    g["wgu"] = _carry(comm.with_ffn_in_dw(), comm.after_ffn_in_dw, _matmul, sv["x1b"], dgu, TN, "ffn_in_dw")
    dh1, dh1b, dg1, db1 = _ffn_head_bwd(dgu, wl["wgu"], dh2, sv["h1"], cl["g1"], tb)
    g["ln1_g"], g["ln1_b"] = dg1.sum(0), db1.sum(0)
    g["wo"] = _matmul(sv["m"], dh1b, TN, "wo_dw")
    dpa, dpb, dya, dyb, dprojm = _mix_bwd(dh1b, sv["pa"], sv["pb"], sv["projm"], wl["wpa"], wl["wpb"], wl["wo"], d, tb)
    g["wpa"] = _matmul(sv["ya"], dpa, TN, "wpa_dw")
    g["wpb"] = _matmul(sv["yb"], dpb, TN, "wpb_dw")
    comm.after_branch_grads(g)
    do, dprojm, donw, dlng, dlnb, dws, dbst = _gate_sgu_bwd(
        dya, dyb, sv["o"], sv["projm"], cl["onw"], cl["lng"], cl["lnb"], cl["ws"], cl["bst"], dprojm, d)
    heads, groups = d // DN_DK, d // SGU_GROUP_DIM
    g["o_norm_w"], g["sgu_ln_g"], g["sgu_ln_b"] = donw.sum(0), dlng.sum(0), dlnb.sum(0)
    g["w_s"], g["b_s"] = dws, dbst[:, :groups].T
    (dqkv, dba, dal, ddt), got = _dn_bwd(sv["qkv"], sv["ba"], cl["alog"], cl["dtb"], do, sv["states"],
                                         sv["ycors"], d, comm.with_dn_bwd())
    comm.after_dn_bwd(got)
    g["a_log"], g["dt_bias"] = dal.sum(0)[:heads], ddt.sum(0)[:heads]
    tbc = _tile(dx2.shape[0], 2 * tb, SUBLANES)
    dy, dcw = _conv_bwd_dy(sv["projm"], wl["conv"], dqkv, d, tbc)
    g["conv_w"] = dcw.sum(1)
    dprojm = _conv_bwd_dx(dy, wl["conv"], dprojm, d, tbc)
    g["wm"] = _carry(comm.with_proj_main_dw(), comm.after_proj_main_dw, _matmul, sv["xb"], dprojm, TN,
                     "proj_main_dw", tn=MM_WIDE)
    g["wba"] = _matmul(sv["xb"], dba, TN, "proj_gates_dw")
    dx = _matmul(dba, wl["wba"], NT, "proj_gates_dx", add=dh1, coef=ALPHA)
    dx = _matmul(dprojm, wl["wm"], NT, "proj_main_dx", add=dx, tk=MM_WIDE)
    return dx, g


_BRANCH = ("w_pa", "w_pb", "w_o", "w_ffn_gate", "w_ffn_up", "w_ffn_down")


def _grad_shards(g, d, keys):
    heads, q4 = d // DN_DK, 4 * d
    rows = lambda a: a.reshape(N_CHIPS, -1, a.shape[1])
    out = {}
    if "w_in" in keys:
        gm, gba, wsh = g["wm"], g["wba"], 2 * d + heads // 2
        out["w_in"] = jnp.stack([gm[:, :wsh],
                                 jnp.concatenate([gm[:, wsh:q4], gba[:, :heads]], axis=1),
                                 jnp.concatenate([gba[:, LANES:LANES + heads], gm[:, q4:q4 + wsh - heads]], axis=1),
                                 gm[:, q4 + wsh - heads:]])
    if "w_pa" in keys:
        ggu = g["wgu"]
        f = ggu.shape[1] // 2
        fs = f // N_CHIPS
        out.update({
            "w_pa": rows(g["wpa"]), "w_pb": rows(g["wpb"]), "w_o": rows(g["wo"]), "w_ffn_down": rows(g["wd"]),
            "w_ffn_gate": jnp.stack([ggu[:, s * fs:(s + 1) * fs] for s in range(N_CHIPS)]),
            "w_ffn_up": jnp.stack([ggu[:, f + s * fs:f + (s + 1) * fs] for s in range(N_CHIPS)])})
    return out


def _local_step(x, target, full0, full1_of, small_w, comm0=None):
    t, d = x.shape
    tb = _tile(t, 256, SUBLANES)
    comm0 = comm0 or _NoComm()
    consts = [_layer_consts(small_w, l, d) for l in range(DEPTH)]
    x1, x1b, sv0, w0 = _layer_fwd(x, x.astype(ACT), full0, consts[0], d, tb, comm0)
    x2, _, sv1, w1 = _layer_fwd(x1, x1b, full1_of(), consts[1], d, tb, _NoComm())
    dy, loss_parts = _loss_head(x2, target, tb)
    dy, g1 = _layer_bwd(dy, sv1, w1, consts[1], d, tb, _NoComm())
    comm0.layer1_grads = g1
    dy, g0 = _layer_bwd(dy, sv0, w0, consts[0], d, tb, comm0)
    return loss_parts, dy, [g0, g1]


def kernel(x, w_in, conv_w, a_log, dt_bias, o_norm_w, sgu_ln_g, sgu_ln_b, w_s, b_s, w_pa, w_pb, w_o, ln1_g, ln1_b, w_ffn_gate, w_ffn_up, w_ffn_down, ln2_g, ln2_b, loss_target, m_w_in, m_conv_w, m_a_log, m_dt_bias, m_o_norm_w, m_sgu_ln_g, m_sgu_ln_b, m_w_s, m_b_s, m_w_pa, m_w_pb, m_w_o, m_ln1_g, m_ln1_b, m_w_ffn_gate, m_w_ffn_up, m_w_ffn_down, m_ln2_g, m_ln2_b, v_w_in, v_conv_w, v_a_log, v_dt_bias, v_o_norm_w, v_sgu_ln_g, v_sgu_ln_b, v_w_s, v_b_s, v_w_pa, v_w_pb, v_w_o, v_ln1_g, v_ln1_b, v_w_ffn_gate, v_w_ffn_up, v_w_ffn_down, v_ln2_g, v_ln2_b):
    names = ("w_in", "conv_w", "a_log", "dt_bias", "o_norm_w", "sgu_ln_g", "sgu_ln_b", "w_s", "b_s", "w_pa",
             "w_pb", "w_o", "ln1_g", "ln1_b", "w_ffn_gate", "w_ffn_up", "w_ffn_down", "ln2_g", "ln2_b")
    w = dict(zip(names, (w_in, conv_w, a_log, dt_bias, o_norm_w, sgu_ln_g, sgu_ln_b, w_s, b_s, w_pa, w_pb, w_o,
                         ln1_g, ln1_b, w_ffn_gate, w_ffn_up, w_ffn_down, ln2_g, ln2_b)))
    mom = dict(zip(names, (m_w_in, m_conv_w, m_a_log, m_dt_bias, m_o_norm_w, m_sgu_ln_g, m_sgu_ln_b, m_w_s, m_b_s,
                           m_w_pa, m_w_pb, m_w_o, m_ln1_g, m_ln1_b, m_w_ffn_gate, m_w_ffn_up, m_w_ffn_down,
                           m_ln2_g, m_ln2_b)))
    var = dict(zip(names, (v_w_in, v_conv_w, v_a_log, v_dt_bias, v_o_norm_w, v_sgu_ln_g, v_sgu_ln_b, v_w_s, v_b_s,
                           v_w_pa, v_w_pb, v_w_o, v_ln1_g, v_ln1_b, v_w_ffn_gate, v_w_ffn_up, v_w_ffn_down,
                           v_ln2_g, v_ln2_b)))
    chip = 2 * lax.axis_index("x") + lax.axis_index("y")
    place = jnp.stack([lax.axis_index("c"), chip]).astype(jnp.int32)

    big = [k for k, _ in _BIG]
    axis_of = dict(_BIG)
    local = {k: w[k].astype(BF16) for k in big}
    local["conv_w"] = conv_w

    def gather_plan(l, keys):
        return _all_gather_plan([local[k][l] for k in keys])

    def full_of(l, keys, gathered):
        return {k: _unshard(gt, local[k][l], chip, axis_of.get(k, 2)) for k, gt in zip(keys, gathered)}

    def pair_sums(grads_l, keys, lands):
        return [_pair_sum(place, grads_l[k], land) for k, land in zip(keys, lands)]

    def chip_sums(pairs, lands):
        return [_chip_sum(place, p[1], land, "grad_chip_sum") for p, land in zip(pairs, lands)]

    class Layer0Comm(_NoComm):
        def with_proj_main(self):
            return gather_plan(0, _BRANCH)

        def after_proj_main(self, got):
            self.rest = full_of(0, _BRANCH, got)

        def weights(self, full):
            return {**full, **self.rest}

        def with_dn_fwd(self):
            return gather_plan(1, big + ["conv_w"])

        def after_dn_fwd(self, got):
            self.full1 = full_of(1, big + ["conv_w"], got)

        def with_ffn_in_dw(self):
            self.g1 = _grad_shards(self.layer1_grads, x.shape[-1], big)
            return _sibling_exchange_plan([self.g1[k] for k in big])

        def after_ffn_in_dw(self, got):
            self.pairs1 = pair_sums(self.g1, big, got)

        def with_dn_bwd(self):
            return _chip_exchange_plan([p[0] for p in self.pairs1])

        def after_dn_bwd(self, got):
            self.red1 = chip_sums(self.pairs1, got)

        def after_branch_grads(self, g0):
            shards = _grad_shards(g0, x.shape[-1], _BRANCH)
            lands = _run_comm("grad_sibling_exchange", _sibling_exchange_plan([shards[k] for k in _BRANCH]))
            self.pairs0 = pair_sums(shards, _BRANCH, lands)

        def with_proj_main_dw(self):
            return _chip_exchange_plan([p[0] for p in self.pairs0])

        def after_proj_main_dw(self, got):
            self.red0 = chip_sums(self.pairs0, got)

    comm = Layer0Comm()
    first = ["w_in", "conv_w"]
    full0 = full_of(0, first, _run_comm("all_gather_weights", gather_plan(0, first)))
    small_w = {k: w[k] for k in _SMALL if k != "conv_w"}
    loss_parts, grad_x, g = _local_step(x[0], loss_target[0], full0, lambda: comm.full1, small_w, comm)

    g_in = _grad_shards(g[0], x.shape[-1], ["w_in"])["w_in"]
    small_g = {k: jnp.stack([g[l][k] for l in range(DEPTH)]) for k in _SMALL}
    small_sizes = [small_g[k].size for k in _SMALL]
    small = _pad_rows(jnp.concatenate([small_g[k].reshape(-1) for k in _SMALL]), SUBLANES)
    land, sland = _run_comm("grad_sibling_exchange_last", _sibling_exchange_plan([g_in], small))
    pair_in = _pair_sum(place, g_in, land)
    small_chip = _add2(small, sland)
    land2, sland2 = _run_comm("grad_chip_exchange_last", _chip_exchange_plan([pair_in[0]], small_chip))
    red_in = _chip_sum(place, pair_in[1], land2, "grad_chip_sum")
    small_total = _chip_sum(place, small_chip, sland2, "grad_small_chip_sum")
    reduced = [red_in] + comm.red0 + comm.red1
    others = _run_comm("grad_sibling_merge", _sibling_merge_plan(reduced))
    halves = [_merge_halves(place, mine, other) for mine, other in zip(reduced, others)]
    grads = {k: jnp.stack([halves[i], halves[len(big) + i]]) for i, k in enumerate(big)}
    small_total, off = small_total.reshape(-1), 0
    for k, n in zip(_SMALL, small_sizes):
        grads[k] = small_total[off:off + n].reshape(small_g[k].shape)
        off += n
    grads["conv_w"] = lax.dynamic_index_in_dim(_to_shards(grads["conv_w"], 2), chip, 1, keepdims=False)

    delta, new_m, new_v = {}, {}, {}
    for k in [k for k, _ in _BIG] + ["conv_w"]:
        delta[k], new_m[k], new_v[k] = _adamw(w[k], grads[k], mom[k], var[k])
    rep = [k for k in _SMALL if k != "conv_w"]
    pack = lambda dct: _pad_rows(jnp.concatenate([dct[k].reshape(-1) for k in rep]), SUBLANES)
    packed = _adamw(pack(w), pack(grads), pack(mom), pack(var))
    off = 0
    for k in rep:
        n = w[k].size
        for dst, src in zip((delta, new_m, new_v), packed):
            dst[k] = src.reshape(-1)[off:off + n].reshape(w[k].shape)
        off += n

    loss = 0.5 * lax.psum(jnp.sum(loss_parts), ("x", "y", "c")) / x.shape[-1]
    return (loss, grad_x[None], *[grads[k] for k in names], *[delta[k] for k in names],
            *[new_m[k] for k in names], *[new_v[k] for k in names])
```

```python
import math

import jax
import jax.numpy as jnp
from jax import lax
from jax.experimental import pallas as pl
from jax.experimental.pallas import tpu as pltpu

F32 = jnp.float32
BF16 = jnp.bfloat16
MXU_DTYPE = jnp.bfloat16
ACT = jnp.bfloat16
HIGHEST = lax.Precision.HIGHEST

DEPTH = 2
CHUNK = 64
DN_GROUP = 2
SGU_BLOCK = 128
CONV_K = 4
DN_DK = 128
SGU_GROUP_DIM = 128
LN_EPS = 1e-5
RMS_EPS = 1e-6
ALPHA = (2 * DEPTH) ** 0.25
ADAM_LR, ADAM_B1, ADAM_B2, ADAM_EPS, ADAM_WD, ADAM_STEP = 0.001, 0.9, 0.999, 1e-08, 0.01, 10

LANES = 128
SUBLANES = 8
VMEM_LIMIT = 52 * 2 ** 20
PACK_W = 1024
N_CHIPS = 4

NN = ((1,), (0,))
NT = ((1,), (1,))
TN = ((0,), (0,))
MESH = pl.DeviceIdType.MESH
ANY = pl.BlockSpec(memory_space=pl.ANY)


def _dot(a, b, dims=NN, prec=None):
    if prec is None:
        a = a.astype(MXU_DTYPE)
        b = b.astype(MXU_DTYPE)
    return lax.dot_general(a, b, (dims, ((), ())), preferred_element_type=F32, precision=prec)


def _cparams(sem=None):
    return pltpu.CompilerParams(dimension_semantics=sem, vmem_limit_bytes=VMEM_LIMIT)


def _tile(dim, pref, unit=LANES):
    t = (min(pref, dim) // unit) * unit
    while t >= unit:
        if dim % t == 0:
            return t
        t -= unit
    return dim


def _fold8(x):
    r, n = x.shape
    return x.reshape(r // SUBLANES, SUBLANES, n).sum(axis=0)


def _sigmoid(x):
    return 1.0 / (1.0 + jnp.exp(-x))


def _gelu(x):
    return 0.5 * x * (1.0 + lax.erf(x * (2.0 ** -0.5)))


def _gelu_grad(x):
    return 0.5 * (1.0 + lax.erf(x * (2.0 ** -0.5))) + x * jnp.exp(-0.5 * x * x) * (2.0 * math.pi) ** -0.5


def _ln_hat(h):
    mu = jnp.mean(h, axis=-1, keepdims=True)
    xc = h - mu
    var = jnp.mean(xc * xc, axis=-1, keepdims=True)
    r = lax.rsqrt(var + LN_EPS)
    return xc * r, r


def _ln_bwd(dxhat, xhat, r):
    return r * (dxhat - jnp.mean(dxhat, axis=-1, keepdims=True)
                - xhat * jnp.mean(dxhat * xhat, axis=-1, keepdims=True))


MM_TILE = 1536
MM_WIDE = 2048


def _matmul(a, b, dims, name, out_dtype=F32, add=None, coef=1.0, tm=MM_TILE, tn=MM_TILE, tk=MM_TILE, carried=None):
    if dims == NN:
        (m, k), n = a.shape, b.shape[1]
    elif dims == NT:
        (m, k), n = a.shape, b.shape[0]
    else:
        (k, m), n = a.shape, b.shape[1]
    tm, tn, tk = _tile(m, tm), _tile(n, tn), _tile(k, tk)
    nk = k // tk
    a_spec = pl.BlockSpec((tk, tm), lambda j, i, q: (q, i)) if dims == TN else pl.BlockSpec((tm, tk), lambda j, i, q: (i, q))
    b_spec = pl.BlockSpec((tn, tk), lambda j, i, q: (j, q)) if dims == NT else pl.BlockSpec((tk, tn), lambda j, i, q: (q, j))
    o_spec = pl.BlockSpec((tm, tn), lambda j, i, q: (i, j))
    has_add = add is not None

    def body(*refs):
        a_ref, b_ref = refs[0], refs[1]
        add_ref = refs[2] if has_add else None
        o_ref, acc_ref = refs[2 + has_add], refs[3 + has_add]
        q = pl.program_id(2)
        part = _dot(a_ref[...], b_ref[...], dims)

        def finish(r):
            if has_add:
                r = r + coef * add_ref[...]
            o_ref[...] = r.astype(out_dtype)

        if nk == 1:
            finish(part)
        else:
            @pl.when(q == 0)
            def _():
                acc_ref[...] = part

            @pl.when(q > 0)
            def _():
                acc_ref[...] += part

            @pl.when(q == nk - 1)
            def _():
                finish(acc_ref[...])

    ins = [a, b] + ([add] if has_add else [])
    in_specs = [a_spec, b_spec] + ([o_spec] if has_add else [])
    grid = (n // tn, m // tm, nk)
    acc = pltpu.VMEM((tm, tn) if nk > 1 else (SUBLANES, LANES), F32)
    out = jax.ShapeDtypeStruct((m, n), out_dtype)
    if carried is None:
        return pl.pallas_call(
            body, name=name, grid=grid, in_specs=in_specs, out_specs=o_spec, out_shape=out, scratch_shapes=[acc],
            compiler_params=_cparams(("parallel", "parallel", "arbitrary")),
        )(*ins)
    res = pl.pallas_call(
        _carrying(body, len(ins), 1, 1, carried, grid), name=name + "_carrying", grid=grid,
        in_specs=in_specs + [ANY] * len(carried.inputs), out_specs=[o_spec] + [ANY] * len(carried.out_shapes),
        out_shape=[out] + carried.out_shapes, scratch_shapes=[acc] + carried.scratch(),
        compiler_params=_cparams(("arbitrary", "arbitrary", "arbitrary")),
    )(*ins, *carried.inputs)
    return res[0], res[1:]


def _conv_taps(cur_ref, halo_ref, first):
    x = cur_ref[...]
    tb = x.shape[0]
    halo = jnp.where(first, 0.0, halo_ref[...])
    xc = jnp.concatenate([halo, x], axis=0)
    return [x] + [pltpu.roll(xc, s, 0)[SUBLANES:SUBLANES + tb] for s in range(1, CONV_K)]


def _conv_fwd(projm, conv_w, d, tb):
    t = projm.shape[0]
    heads = d // DN_DK
    hb = tb // SUBLANES

    def body(cur_ref, halo_ref, w_ref, o_ref):
        i, j = pl.program_id(0), pl.program_id(1)
        taps = _conv_taps(cur_ref, halo_ref, i == 0)
        y = taps[0] * w_ref[CONV_K - 1:CONV_K, :]
        for s in range(1, CONV_K):
            y = y + taps[s] * w_ref[CONV_K - 1 - s:CONV_K - s, :]
        act = y * _sigmoid(y)
        scale = jnp.where(j == 0, DN_DK ** -0.5, 1.0)
        for h in range(heads):
            seg = act[:, h * DN_DK:(h + 1) * DN_DK]
            r = lax.rsqrt(jnp.sum(seg * seg, axis=1, keepdims=True) + RMS_EPS) * scale
            o_ref[:, h * DN_DK:(h + 1) * DN_DK] = seg * jnp.where(j < 2, r, 1.0)

    return pl.pallas_call(
        body, name="conv_fwd", grid=(t // tb, 3),
        in_specs=[pl.BlockSpec((tb, d), lambda i, j: (i, j)),
                  pl.BlockSpec((SUBLANES, d), lambda i, j: (jnp.maximum(i * hb - 1, 0), j)),
                  pl.BlockSpec((CONV_K, d), lambda i, j: (0, j))],
        out_specs=pl.BlockSpec((tb, d), lambda i, j: (i, j)),
        out_shape=jax.ShapeDtypeStruct((t, 3 * d), F32),
        compiler_params=_cparams(("parallel", "parallel")),
    )(projm, projm, conv_w)


def _conv_bwd_dy(projm, conv_w, dqkv, d, tb):
    t = projm.shape[0]
    heads = d // DN_DK
    hb = tb // SUBLANES

    def body(cur_ref, halo_ref, w_ref, dout_ref, dy_ref, dw_ref):
        j, i = pl.program_id(0), pl.program_id(1)
        taps = _conv_taps(cur_ref, halo_ref, i == 0)
        y = taps[0] * w_ref[CONV_K - 1:CONV_K, :]
        for s in range(1, CONV_K):
            y = y + taps[s] * w_ref[CONV_K - 1 - s:CONV_K - s, :]
        sg = _sigmoid(y)
        act = y * sg
        dact = sg * (1.0 + y * (1.0 - sg))
        scale = jnp.where(j == 0, DN_DK ** -0.5, 1.0)
        for h in range(heads):
            cols = slice(h * DN_DK, (h + 1) * DN_DK)
            seg = act[:, cols]
            r = lax.rsqrt(jnp.sum(seg * seg, axis=1, keepdims=True) + RMS_EPS)
            nrm = seg * r
            dout = dout_ref[:, cols]
            dn = dout * scale
            ds = jnp.where(j < 2, r * (dn - nrm * jnp.sum(dn * nrm, axis=1, keepdims=True)), dout)
            dy_ref[:, cols] = ds * dact[:, cols]
        dy = dy_ref[...]

        @pl.when(i == 0)
        def _():
            dw_ref[...] = jnp.zeros_like(dw_ref)

        for s in range(CONV_K):
            dw_ref[CONV_K - 1 - s] += _fold8(dy * taps[s])

    return pl.pallas_call(
        body, name="conv_bwd_dy", grid=(3, t // tb),
        in_specs=[pl.BlockSpec((tb, d), lambda j, i: (i, j)),
                  pl.BlockSpec((SUBLANES, d), lambda j, i: (jnp.maximum(i * hb - 1, 0), j)),
                  pl.BlockSpec((CONV_K, d), lambda j, i: (0, j)),
                  pl.BlockSpec((tb, d), lambda j, i: (i, j))],
        out_specs=[pl.BlockSpec((tb, d), lambda j, i: (i, j)),
                   pl.BlockSpec((CONV_K, SUBLANES, d), lambda j, i: (0, 0, j))],
        out_shape=[jax.ShapeDtypeStruct((t, 3 * d), F32),
                   jax.ShapeDtypeStruct((CONV_K, SUBLANES, 3 * d), F32)],
        compiler_params=_cparams(("parallel", "arbitrary")),
    )(projm, projm, conv_w, dqkv)


def _conv_bwd_dx(dy, conv_w, dprojm, d, tb):
    t = dy.shape[0]
    hb = tb // SUBLANES
    last = t // tb - 1

    def body(cur_ref, halo_ref, w_ref, alias_ref, o_ref):
        i = pl.program_id(0)
        cur = cur_ref[...]
        halo = jnp.where(i == last, 0.0, halo_ref[...])
        dc = jnp.concatenate([cur, halo], axis=0)
        acc = cur * w_ref[CONV_K - 1:CONV_K, :]
        for s in range(1, CONV_K):
            acc = acc + pltpu.roll(dc, tb + SUBLANES - s, 0)[:tb] * w_ref[CONV_K - 1 - s:CONV_K - s, :]
        o_ref[...] = acc.astype(o_ref.dtype)

    return pl.pallas_call(
        body, name="conv_bwd_dx", grid=(t // tb, 3),
        in_specs=[pl.BlockSpec((tb, d), lambda i, j: (i, j)),
                  pl.BlockSpec((SUBLANES, d), lambda i, j: (jnp.minimum((i + 1) * hb, t // SUBLANES - 1), j)),
                  pl.BlockSpec((CONV_K, d), lambda i, j: (0, j)),
                  ANY],
        out_specs=pl.BlockSpec((tb, d), lambda i, j: (i, j)),
        out_shape=jax.ShapeDtypeStruct(dprojm.shape, dprojm.dtype),
        input_output_aliases={3: 0},
        compiler_params=_cparams(("parallel", "parallel")),
    )(dy, dy, conv_w, dprojm)


def _beta_g(ba, alog, dtb):
    beta = _sigmoid(ba[:, :LANES])
    xa = ba[:, LANES:] + dtb
    softplus = jnp.maximum(xa, 0.0) + jnp.log(1.0 + jnp.exp(-jnp.abs(xa)))
    ea = jnp.exp(alog)
    return beta, -ea * softplus, ea, _sigmoid(xa)


def _inv_corrections(mats):
    ys = [-a for a in mats]
    ps = [_dot(a, a) for a in mats]
    steps = int(math.log2(CHUNK)) - 1
    for it in range(steps):
        ys = [y + p + _dot(y, p) for y, p in zip(ys, ps)]
        if it < steps - 1:
            ps = [_dot(p, p) for p in ps]
    return ys


def _chunk_masks():
    row = lax.broadcasted_iota(jnp.int32, (CHUNK, CHUNK), 0)
    col = lax.broadcasted_iota(jnp.int32, (CHUNK, CHUNK), 1)
    return row >= col, row > col, row <= col


def _col_of(mat, lane_idx, h):
    return jnp.sum(jnp.where(lane_idx == h, mat, 0.0), axis=1, keepdims=True)


def _row_of(mat, sub_idx, h):
    return jnp.sum(jnp.where(sub_idx == h, mat, 0.0), axis=0, keepdims=True)


def _carrying(compute, n_in, n_out, n_scratch, carried, grid):
    if carried is None:
        return compute
    ci, co = len(carried.inputs), len(carried.out_shapes)

    def body(*refs):
        ins, c_in = refs[:n_in], refs[n_in:n_in + ci]
        outs, c_out = refs[n_in + ci:n_in + ci + n_out], refs[n_in + ci + n_out:n_in + ci + n_out + co]
        scratch = refs[n_in + ci + n_out + co:]
        start, finish = carried.copies(c_in, c_out, scratch[n_scratch], scratch[n_scratch + 1])
        first, last = True, True
        for axis, steps in enumerate(grid):
            first = jnp.logical_and(first, pl.program_id(axis) == 0)
            last = jnp.logical_and(last, pl.program_id(axis) == steps - 1)

        @pl.when(first)
        def _():
            start()

        compute(*ins, *outs, *scratch[:n_scratch])

        @pl.when(last)
        def _():
            finish()

    return body


def _dn_fwd(qkv, ba, alog, dtb, d, carried=None):
    t = qkv.shape[0]
    heads = d // DN_DK
    n_chunks = t // CHUNK
    grp = DN_GROUP if n_chunks % DN_GROUP == 0 else 1
    span = grp * CHUNK
    extra = carried or _Carried([], [], 0, None)

    def compute(qkv_ref, ba_ref, al_ref, dt_ref, o_ref, s_ref, y_ref, state):
        @pl.when(pl.program_id(0) == 0)
        def _():
            state[...] = jnp.zeros_like(state)

        tril, strict, _ = _chunk_masks()
        beta, g, _, _ = _beta_g(ba_ref[...], al_ref[...], dt_ref[...])
        lane = lax.broadcasted_iota(jnp.int32, (CHUNK, LANES), 1)
        sub = lax.broadcasted_iota(jnp.int32, (LANES, CHUNK), 0)
        rowc = lax.broadcasted_iota(jnp.int32, (CHUNK, 1), 0)
        hs = range(heads)
        units = [(c, h) for c in range(grp) for h in hs]
        un = range(len(units))
        rows = lambda c: slice(c * CHUNK, (c + 1) * CHUNK)
        gc = [_dot(jnp.where(tril, 1.0, 0.0), g[rows(c)], NN, HIGHEST) for c in range(grp)]
        gct = [m.T for m in gc]
        q = [qkv_ref[rows(c), h * DN_DK:(h + 1) * DN_DK] for c, h in units]
        k = [qkv_ref[rows(c), d + h * DN_DK:d + (h + 1) * DN_DK] for c, h in units]
        v = [qkv_ref[rows(c), 2 * d + h * DN_DK:2 * d + (h + 1) * DN_DK] for c, h in units]
        gch = [_col_of(gc[c], lane, h) for c, h in units]
        bh = [_col_of(beta[rows(c)], lane, h) for c, h in units]
        dec = [jnp.where(tril, jnp.exp(gch[n] - _row_of(gct[c], sub, h)), 0.0) for n, (c, h) in enumerate(units)]
        egc = [jnp.exp(gch[n]) for n in un]
        gl = [jnp.sum(jnp.where(rowc == CHUNK - 1, gch[n], 0.0), axis=0, keepdims=True) for n in un]
        kb = [k[n] * bh[n] for n in un]
        a = [jnp.where(strict, _dot(kb[n], k[n], NT) * dec[n], 0.0) for n in un]
        p = [_dot(q[n], k[n], NT) * dec[n] for n in un]
        ycor = _inv_corrections(a)
        rhs = [jnp.concatenate([v[n] * bh[n], kb[n] * egc[n]], axis=1) for n in un]
        sol = [rhs[n] + _dot(ycor[n], rhs[n]) for n in un]
        qg = [q[n] * egc[n] for n in un]
        kd = [k[n] * jnp.exp(gl[n] - gch[n]) for n in un]
        egl = [jnp.exp(gl[n]) for n in un]
        s_cur, s_in, o = [state[h] for h in hs], [], []
        for c in range(grp):
            ns = [c * heads + h for h in hs]
            vn = [sol[n][:, :DN_DK] - _dot(sol[n][:, DN_DK:], s_cur[h]) for h, n in enumerate(ns)]
            o += [_dot(qg[n], s_cur[h]) + _dot(p[n], vn[h]) for h, n in enumerate(ns)]
            s_in += s_cur
            s_cur = [s_cur[h] * egl[n] + _dot(kd[n], vn[h], TN) for h, n in enumerate(ns)]
        for n, (c, h) in enumerate(units):
            o_ref[rows(c), h * DN_DK:(h + 1) * DN_DK] = o[n]
            s_ref[c, h] = s_in[n]
            y_ref[h, rows(c), :] = ycor[n]
        for h in hs:
            state[h] = s_cur[h]

    res = pl.pallas_call(
        _carrying(compute, 4, 3, 1, carried, (n_chunks // grp,)),
        name="dn_fwd_carrying" if carried else "dn_fwd", grid=(n_chunks // grp,),
        in_specs=[pl.BlockSpec((span, 3 * d), lambda i: (i, 0)),
                  pl.BlockSpec((span, 2 * LANES), lambda i: (i, 0)),
                  pl.BlockSpec((1, LANES), lambda i: (0, 0)),
                  pl.BlockSpec((1, LANES), lambda i: (0, 0))] + [ANY] * len(extra.inputs),
        out_specs=[pl.BlockSpec((span, d), lambda i: (i, 0)),
                   pl.BlockSpec((grp, heads, DN_DK, DN_DK), lambda i: (i, 0, 0, 0)),
                   pl.BlockSpec((heads, span, CHUNK), lambda i: (0, i, 0))] + [ANY] * len(extra.out_shapes),
        out_shape=[jax.ShapeDtypeStruct((t, d), F32),
                   jax.ShapeDtypeStruct((n_chunks, heads, DN_DK, DN_DK), F32),
                   jax.ShapeDtypeStruct((heads, t, CHUNK), F32)] + extra.out_shapes,
        scratch_shapes=[pltpu.VMEM((heads, DN_DK, DN_DK), F32)] + (extra.scratch() if carried else []),
        compiler_params=_cparams(("arbitrary",)),
    )(qkv, ba, alog, dtb, *extra.inputs)
    return res[:3], res[3:]


def _dn_bwd(qkv, ba, alog, dtb, dout, states, ycors, d, carried=None):
    t = qkv.shape[0]
    heads = d // DN_DK
    n_chunks = t // CHUNK
    grp = DN_GROUP if n_chunks % DN_GROUP == 0 else 1
    span = grp * CHUNK
    rev = lambda i: n_chunks // grp - 1 - i
    extra = carried or _Carried([], [], 0, None)

    def compute(qkv_ref, ba_ref, al_ref, dt_ref, do_ref, s_ref, y_ref,
                dqkv_ref, dba_ref, dal_ref, ddt_ref, dstate):
        @pl.when(pl.program_id(0) == 0)
        def _():
            dstate[...] = jnp.zeros_like(dstate)
            dal_ref[...] = jnp.zeros_like(dal_ref)
            ddt_ref[...] = jnp.zeros_like(ddt_ref)

        tril, strict, triu = _chunk_masks()
        beta, g, ea, sig_a = _beta_g(ba_ref[...], al_ref[...], dt_ref[...])
        lane = lax.broadcasted_iota(jnp.int32, (CHUNK, LANES), 1)
        sub = lax.broadcasted_iota(jnp.int32, (LANES, CHUNK), 0)
        rowc = lax.broadcasted_iota(jnp.int32, (CHUNK, 1), 0)
        hs = range(heads)
        units = [(c, h) for c in range(grp) for h in hs]
        un = range(len(units))
        rows = lambda c: slice(c * CHUNK, (c + 1) * CHUNK)
        rsum = lambda x_: jnp.sum(x_, axis=1, keepdims=True)
        gc = [_dot(jnp.where(tril, 1.0, 0.0), g[rows(c)], NN, HIGHEST) for c in range(grp)]
        gct = [m.T for m in gc]
        q = [qkv_ref[rows(c), h * DN_DK:(h + 1) * DN_DK] for c, h in units]
        k = [qkv_ref[rows(c), d + h * DN_DK:d + (h + 1) * DN_DK] for c, h in units]
        v = [qkv_ref[rows(c), 2 * d + h * DN_DK:2 * d + (h + 1) * DN_DK] for c, h in units]
        dout_h = [do_ref[rows(c), h * DN_DK:(h + 1) * DN_DK] for c, h in units]
        s0 = [s_ref[c, h] for c, h in units]
        ycor = [y_ref[h, rows(c), :] for c, h in units]
        gch = [_col_of(gc[c], lane, h) for c, h in units]
        bh = [_col_of(beta[rows(c)], lane, h) for c, h in units]
        dec = [jnp.where(tril, jnp.exp(gch[n] - _row_of(gct[c], sub, h)), 0.0) for n, (c, h) in enumerate(units)]
        egc = [jnp.exp(gch[n]) for n in un]
        gl = [jnp.sum(jnp.where(rowc == CHUNK - 1, gch[n], 0.0), axis=0, keepdims=True) for n in un]
        egl = [jnp.exp(gl[n]) for n in un]
        ekd = [jnp.exp(gl[n] - gch[n]) for n in un]
        kb = [k[n] * bh[n] for n in un]
        kd = [k[n] * ekd[n] for n in un]
        qg = [q[n] * egc[n] for n in un]
        kbg = [kb[n] * egc[n] for n in un]
        a = [jnp.where(strict, _dot(kb[n], k[n], NT) * dec[n], 0.0) for n in un]
        p = [_dot(q[n], k[n], NT) * dec[n] for n in un]
        rhs = [jnp.concatenate([v[n] * bh[n], kbg[n]], axis=1) for n in un]
        sol = [rhs[n] + _dot(ycor[n], rhs[n]) for n in un]
        w = [sol[n][:, DN_DK:] for n in un]
        vn = [sol[n][:, :DN_DK] - _dot(w[n], s0[n]) for n in un]
        dqg = [_dot(dout_h[n], s0[n], NT) for n in un]
        dp = [jnp.where(tril, _dot(dout_h[n], vn[n], NT), 0.0) for n in un]
        pdo = [_dot(p[n], dout_h[n], TN) for n in un]
        qdo = [_dot(qg[n], dout_h[n], TN) for n in un]
        ds_cur = [dstate[h] for h in hs]
        dsn, dvn = [None] * len(units), [None] * len(units)
        for c in reversed(range(grp)):
            for h in hs:
                dsn[c * heads + h] = ds_cur[h]
            for h in hs:
                n = c * heads + h
                dvn[n] = pdo[n] + _dot(kd[n], ds_cur[h])
            ds_cur = [qdo[c * heads + h] + egl[c * heads + h] * ds_cur[h]
                      - _dot(w[c * heads + h], dvn[c * heads + h], TN) for h in hs]
        dkd = [_dot(vn[n], dsn[n], NT) for n in un]
        dw = [-_dot(dvn[n], s0[n], NT) for n in un]
        dgl = [jnp.sum(rsum(dsn[n] * s0[n]), axis=0, keepdims=True) * egl[n] for n in un]
        dsol = [jnp.concatenate([dvn[n], dw[n]], axis=1) for n in un]
        drhs = [dsol[n] + _dot(ycor[n], dsol[n], TN) for n in un]
        dvb = [drhs[n][:, :DN_DK] for n in un]
        dkbg = [drhs[n][:, DN_DK:] for n in un]
        da = [jnp.where(strict, -_dot(drhs[n], sol[n], NT), 0.0) for n in un]
        dma = [da[n] * dec[n] for n in un]
        dmp = [dp[n] * dec[n] for n in un]
        dkb = [_dot(dma[n], k[n]) + dkbg[n] * egc[n] for n in un]
        dq = [_dot(dmp[n], k[n]) + dqg[n] * egc[n] for n in un]
        dk = [_dot(dma[n], kb[n], TN) + _dot(dmp[n], q[n], TN) + dkd[n] * ekd[n] + dkb[n] * bh[n] for n in un]
        e = [da[n] * a[n] + dp[n] * p[n] for n in un]
        colsum = [jnp.sum(e[n], axis=0, keepdims=True) for n in un]
        tkd = [rsum(dkd[n] * kd[n]) for n in un]
        for n, (c, h) in enumerate(units):
            dqkv_ref[rows(c), h * DN_DK:(h + 1) * DN_DK] = dq[n]
            dqkv_ref[rows(c), d + h * DN_DK:d + (h + 1) * DN_DK] = dk[n]
            dqkv_ref[rows(c), 2 * d + h * DN_DK:2 * d + (h + 1) * DN_DK] = dvb[n] * bh[n]
        for h in hs:
            dstate[h] = ds_cur[h]
        valid = lane < heads
        dal_acc = jnp.zeros((SUBLANES, LANES), F32)
        ddt_acc = jnp.zeros((SUBLANES, LANES), F32)
        for c in range(grp):
            dgc_all = jnp.zeros((CHUNK, LANES), F32)
            dbeta_all = jnp.zeros((CHUNK, LANES), F32)
            colsums = jnp.zeros((LANES, CHUNK), F32)
            for h in hs:
                n = c * heads + h
                dgc = rsum(e[n]) + rsum(dqg[n] * qg[n]) - tkd[n] + rsum(dkbg[n] * kbg[n])
                dgc = dgc + jnp.where(rowc == CHUNK - 1, dgl[n] + jnp.sum(tkd[n], axis=0, keepdims=True), 0.0)
                dgc_all = dgc_all + jnp.where(lane == h, dgc, 0.0)
                colsums = colsums + jnp.where(sub == h, colsum[n], 0.0)
                dbeta_all = dbeta_all + jnp.where(lane == h, rsum(dkb[n] * k[n]) + rsum(dvb[n] * v[n]), 0.0)
            dg = _dot(jnp.where(triu, 1.0, 0.0), dgc_all - colsums.T, NN, HIGHEST)
            beta_c = beta[rows(c)]
            dbl = jnp.where(valid, dbeta_all * beta_c * (1.0 - beta_c), 0.0)
            dal = jnp.where(valid, -dg * ea * sig_a[rows(c)], 0.0)
            dba_ref[rows(c), :LANES] = dbl.astype(dba_ref.dtype)
            dba_ref[rows(c), LANES:] = dal.astype(dba_ref.dtype)
            dal_acc = dal_acc + _fold8(jnp.where(valid, dg * g[rows(c)], 0.0))
            ddt_acc = ddt_acc + _fold8(dal)
        dal_ref[...] += dal_acc
        ddt_ref[...] += ddt_acc

    res = pl.pallas_call(
        _carrying(compute, 7, 4, 1, carried, (n_chunks // grp,)),
        name="dn_bwd_carrying" if carried else "dn_bwd", grid=(n_chunks // grp,),
        in_specs=[pl.BlockSpec((span, 3 * d), lambda i: (rev(i), 0)),
                  pl.BlockSpec((span, 2 * LANES), lambda i: (rev(i), 0)),
                  pl.BlockSpec((1, LANES), lambda i: (0, 0)),
                  pl.BlockSpec((1, LANES), lambda i: (0, 0)),
                  pl.BlockSpec((span, d), lambda i: (rev(i), 0)),
                  pl.BlockSpec((grp, heads, DN_DK, DN_DK), lambda i: (rev(i), 0, 0, 0)),
                  pl.BlockSpec((heads, span, CHUNK), lambda i: (0, rev(i), 0))] + [ANY] * len(extra.inputs),
        out_specs=[pl.BlockSpec((span, 3 * d), lambda i: (rev(i), 0)),
                   pl.BlockSpec((span, 2 * LANES), lambda i: (rev(i), 0)),
                   pl.BlockSpec((SUBLANES, LANES), lambda i: (0, 0)),
                   pl.BlockSpec((SUBLANES, LANES), lambda i: (0, 0))] + [ANY] * len(extra.out_shapes),
        out_shape=[jax.ShapeDtypeStruct((t, 3 * d), F32),
                   jax.ShapeDtypeStruct((t, 2 * LANES), ACT),
                   jax.ShapeDtypeStruct((SUBLANES, LANES), F32),
                   jax.ShapeDtypeStruct((SUBLANES, LANES), F32)] + extra.out_shapes,
        scratch_shapes=[pltpu.VMEM((heads, DN_DK, DN_DK), F32)] + (extra.scratch() if carried else []),
        compiler_params=_cparams(("arbitrary",)),
    )(qkv, ba, alog, dtb, dout, states, ycors, *extra.inputs)
    return res[:4], res[4:]


def _sgu_mask():
    row = lax.broadcasted_iota(jnp.int32, (SGU_BLOCK, SGU_BLOCK), 0)
    col = lax.broadcasted_iota(jnp.int32, (SGU_BLOCK, SGU_BLOCK), 1)
    sh = int(math.log2(CHUNK))
    return lax.shift_right_logical(row, sh) >= lax.shift_right_logical(col, sh)


def _gate_sgu_fwd(o, projm, onw, lng, lnb, ws, bst, d):
    t = o.shape[0]
    heads, groups = d // DN_DK, d // SGU_GROUP_DIM
    tb = SGU_BLOCK
    row_spec = pl.BlockSpec((1, d), lambda i: (0, 0))

    def body(o_ref, z_ref, u_ref, v_ref, onw_ref, lng_ref, lnb_ref, ws_ref, bst_ref, ya_ref, yb_ref):
        for h in range(heads):
            cols = slice(h * DN_DK, (h + 1) * DN_DK)
            oh, zh = o_ref[:, cols], z_ref[:, cols]
            r = lax.rsqrt(jnp.mean(oh * oh, axis=1, keepdims=True) + RMS_EPS)
            ya_ref[:, cols] = (oh * r * onw_ref[:, cols] * (zh * _sigmoid(zh))).astype(ya_ref.dtype)
        xhat, _ = _ln_hat(_gelu(v_ref[...]))
        vgn = xhat * lng_ref[...] + lnb_ref[...]
        mask = _sgu_mask()
        lane = lax.broadcasted_iota(jnp.int32, (SGU_BLOCK, LANES), 1)
        bst_v = bst_ref[...]
        for gi in range(groups):
            cols = slice(gi * SGU_GROUP_DIM, (gi + 1) * SGU_GROUP_DIM)
            wsg = jnp.where(mask, ws_ref[gi], 0.0)
            sp = _dot(wsg, vgn[:, cols]) + _col_of(bst_v, lane, gi)
            yb_ref[:, cols] = (_gelu(u_ref[:, cols]) * sp).astype(yb_ref.dtype)

    return pl.pallas_call(
        body, name="gate_sgu_fwd", grid=(t // tb,),
        in_specs=[pl.BlockSpec((tb, d), lambda i: (i, 0)),
                  pl.BlockSpec((tb, d), lambda i: (i, 3)),
                  pl.BlockSpec((tb, d), lambda i: (i, 4)),
                  pl.BlockSpec((tb, d), lambda i: (i, 5)),
                  row_spec, row_spec, row_spec,
                  pl.BlockSpec((groups, SGU_BLOCK, SGU_BLOCK), lambda i: (0, 0, 0)),
                  pl.BlockSpec((SGU_BLOCK, LANES), lambda i: (0, 0))],
        out_specs=[pl.BlockSpec((tb, d), lambda i: (i, 0)), pl.BlockSpec((tb, d), lambda i: (i, 0))],
        out_shape=[jax.ShapeDtypeStruct((t, d), ACT), jax.ShapeDtypeStruct((t, d), ACT)],
        compiler_params=_cparams(("parallel",)),
    )(o, projm, projm, projm, onw, lng, lnb, ws, bst)


def _gate_sgu_bwd(dya, dyb, o, projm, onw, lng, lnb, ws, bst, dprojm, d):
    t = o.shape[0]
    heads, groups = d // DN_DK, d // SGU_GROUP_DIM
    tb = SGU_BLOCK
    row_spec = pl.BlockSpec((1, d), lambda i: (0, 0))
    acc_row = pl.BlockSpec((SUBLANES, d), lambda i: (0, 0))

    def body(dya_ref, dyb_ref, o_ref, z_ref, u_ref, v_ref, onw_ref, lng_ref, lnb_ref, ws_ref, bst_ref, alias_ref,
             do_ref, dp_ref, donw_ref, dlng_ref, dlnb_ref, dws_ref, dbst_ref):
        @pl.when(pl.program_id(0) == 0)
        def _():
            for r_ in (donw_ref, dlng_ref, dlnb_ref, dws_ref, dbst_ref):
                r_[...] = jnp.zeros_like(r_)

        donw = jnp.zeros((SUBLANES, DN_DK), F32)
        for h in range(heads):
            cols = slice(h * DN_DK, (h + 1) * DN_DK)
            oh, zh, dyah, wh = o_ref[:, cols], z_ref[:, cols], dya_ref[:, cols], onw_ref[:, cols]
            r = lax.rsqrt(jnp.mean(oh * oh, axis=1, keepdims=True) + RMS_EPS)
            on = oh * r
            sz = _sigmoid(zh)
            silu_z = zh * sz
            don = dyah * wh * silu_z
            dp_ref[:, cols] = (dyah * on * wh * (sz * (1.0 + zh * (1.0 - sz)))).astype(dp_ref.dtype)
            donw = donw + _fold8(dyah * on * silu_z)
            do_ref[:, cols] = r * (don - on * jnp.mean(don * on, axis=1, keepdims=True))
        donw_ref[...] += donw

        vgp, up = v_ref[...], u_ref[...]
        xhat, rstd = _ln_hat(_gelu(vgp))
        lng_v = lng_ref[...]
        vgn = xhat * lng_v + lnb_ref[...]
        ua = _gelu(up)
        mask = _sgu_mask()
        lane = lax.broadcasted_iota(jnp.int32, (SGU_BLOCK, LANES), 1)
        bst_v = bst_ref[...]
        dbst = jnp.zeros((SGU_BLOCK, LANES), F32)
        dvgn_parts, dua_parts = [], []
        for gi in range(groups):
            cols = slice(gi * SGU_GROUP_DIM, (gi + 1) * SGU_GROUP_DIM)
            wsg = jnp.where(mask, ws_ref[gi], 0.0)
            vg_g, dyb_g = vgn[:, cols], dyb_ref[:, cols]
            sp = _dot(wsg, vg_g) + _col_of(bst_v, lane, gi)
            dsp = dyb_g * ua[:, cols]
            dua_parts.append(dyb_g * sp)
            dws_ref[gi] += jnp.where(mask, _dot(dsp, vg_g, NT), 0.0)
            dbst = dbst + jnp.where(lane == gi, jnp.sum(dsp, axis=1, keepdims=True), 0.0)
            dvgn_parts.append(_dot(wsg, dsp, TN))
        dbst_ref[...] += dbst
        dvgn = jnp.concatenate(dvgn_parts, axis=1)
        dua = jnp.concatenate(dua_parts, axis=1)
        dlng_ref[...] += _fold8(dvgn * xhat)
        dlnb_ref[...] += _fold8(dvgn)
        dvga = _ln_bwd(dvgn * lng_v, xhat, rstd)
        dp_ref[:, d:2 * d] = (dua * _gelu_grad(up)).astype(dp_ref.dtype)
        dp_ref[:, 2 * d:] = (dvga * _gelu_grad(vgp)).astype(dp_ref.dtype)

    return pl.pallas_call(
        body, name="gate_sgu_bwd", grid=(t // tb,),
        in_specs=[pl.BlockSpec((tb, d), lambda i: (i, 0)),
                  pl.BlockSpec((tb, d), lambda i: (i, 0)),
                  pl.BlockSpec((tb, d), lambda i: (i, 0)),
                  pl.BlockSpec((tb, d), lambda i: (i, 3)),
                  pl.BlockSpec((tb, d), lambda i: (i, 4)),
                  pl.BlockSpec((tb, d), lambda i: (i, 5)),
                  row_spec, row_spec, row_spec,
                  pl.BlockSpec((groups, SGU_BLOCK, SGU_BLOCK), lambda i: (0, 0, 0)),
                  pl.BlockSpec((SGU_BLOCK, LANES), lambda i: (0, 0)),
                  ANY],
        out_specs=[pl.BlockSpec((tb, d), lambda i: (i, 0)),
                   pl.BlockSpec((tb, 3 * d), lambda i: (i, 1)),
                   pl.BlockSpec((SUBLANES, DN_DK), lambda i: (0, 0)),
                   acc_row, acc_row,
                   pl.BlockSpec((groups, SGU_BLOCK, SGU_BLOCK), lambda i: (0, 0, 0)),
                   pl.BlockSpec((SGU_BLOCK, LANES), lambda i: (0, 0))],
        out_shape=[jax.ShapeDtypeStruct((t, d), F32),
                   jax.ShapeDtypeStruct(dprojm.shape, dprojm.dtype),
                   jax.ShapeDtypeStruct((SUBLANES, DN_DK), F32),
                   jax.ShapeDtypeStruct((SUBLANES, d), F32),
                   jax.ShapeDtypeStruct((SUBLANES, d), F32),
                   jax.ShapeDtypeStruct((groups, SGU_BLOCK, SGU_BLOCK), F32),
                   jax.ShapeDtypeStruct((SGU_BLOCK, LANES), F32)],
        input_output_aliases={11: 1},
        compiler_params=_cparams(("arbitrary",)),
    )(dya, dyb, o, projm, projm, projm, onw, lng, lnb, ws, bst, dprojm)


def _mix_fwd(ya, yb, projm, x, wpa, wpb, wo, g1, b1, d, tb):
    t = x.shape[0]
    blk = pl.BlockSpec((tb, d), lambda i: (i, 0))
    wspec = pl.BlockSpec((d, d), lambda i: (0, 0))
    row_spec = pl.BlockSpec((1, d), lambda i: (0, 0))

    def body(ya_ref, yb_ref, ga_ref, gb_ref, x_ref, wpa_ref, wpb_ref, wo_ref, g_ref, b_ref,
             pa_ref, pb_ref, m_ref, h_ref, x1_ref, x1b_ref):
        pa = _dot(ya_ref[...], wpa_ref[...])
        pb = _dot(yb_ref[...], wpb_ref[...])
        m = _sigmoid(ga_ref[...]) * pa + _sigmoid(gb_ref[...]) * pb
        hres = ALPHA * x_ref[...] + _dot(m, wo_ref[...])
        xhat, _ = _ln_hat(hres)
        x1 = xhat * g_ref[...] + b_ref[...]
        pa_ref[...] = pa
        pb_ref[...] = pb
        m_ref[...] = m.astype(m_ref.dtype)
        h_ref[...] = hres
        x1_ref[...] = x1
        x1b_ref[...] = x1.astype(x1b_ref.dtype)

    f32_out = jax.ShapeDtypeStruct((t, d), F32)
    bf_out = jax.ShapeDtypeStruct((t, d), ACT)
    return pl.pallas_call(
        body, name="mix_fwd", grid=(t // tb,),
        in_specs=[blk, blk, pl.BlockSpec((tb, d), lambda i: (i, 6)), pl.BlockSpec((tb, d), lambda i: (i, 7)),
                  blk, wspec, wspec, wspec, row_spec, row_spec],
        out_specs=[blk] * 6,
        out_shape=[f32_out, f32_out, bf_out, f32_out, f32_out, bf_out],
        compiler_params=_cparams(("parallel",)),
    )(ya, yb, projm, projm, x, wpa, wpb, wo, g1, b1)


def _mix_bwd(dmix, pa, pb, projm, wpa, wpb, wo, d, tb):
    t = dmix.shape[0]
    blk = pl.BlockSpec((tb, d), lambda i: (i, 0))
    wspec = pl.BlockSpec((d, d), lambda i: (0, 0))

    def body(dmix_ref, pa_ref, pb_ref, ga_ref, gb_ref, wpa_ref, wpb_ref, wo_ref,
             dpa_ref, dpb_ref, dya_ref, dyb_ref, dg_ref):
        dm = _dot(dmix_ref[...], wo_ref[...], NT)
        sa, sb = _sigmoid(ga_ref[...]), _sigmoid(gb_ref[...])
        dpa, dpb = dm * sa, dm * sb
        dpa_ref[...] = dpa.astype(dpa_ref.dtype)
        dpb_ref[...] = dpb.astype(dpb_ref.dtype)
        dg_ref[:, :d] = (dm * pa_ref[...] * sa * (1.0 - sa)).astype(dg_ref.dtype)
        dg_ref[:, d:] = (dm * pb_ref[...] * sb * (1.0 - sb)).astype(dg_ref.dtype)
        dya_ref[...] = _dot(dpa, wpa_ref[...], NT)
        dyb_ref[...] = _dot(dpb, wpb_ref[...], NT)

    return pl.pallas_call(
        body, name="mix_bwd", grid=(t // tb,),
        in_specs=[blk, blk, blk, pl.BlockSpec((tb, d), lambda i: (i, 6)), pl.BlockSpec((tb, d), lambda i: (i, 7)),
                  wspec, wspec, wspec],
        out_specs=[blk, blk, blk, blk, pl.BlockSpec((tb, 2 * d), lambda i: (i, 3))],
        out_shape=[jax.ShapeDtypeStruct((t, d), ACT), jax.ShapeDtypeStruct((t, d), ACT),
                   jax.ShapeDtypeStruct((t, d), F32), jax.ShapeDtypeStruct((t, d), F32),
                   jax.ShapeDtypeStruct((t, 8 * d), ACT)],
        compiler_params=_cparams(("parallel",)),
    )(dmix, pa, pb, projm, projm, wpa, wpb, wo)


def _ffn_tail_fwd(gu, wd, x1, g, b, tb):
    t, d = x1.shape
    f = wd.shape[0]
    fc = _tile(f, MM_TILE)
    blk = pl.BlockSpec((tb, d), lambda i: (i, 0))
    row_spec = pl.BlockSpec((1, d), lambda i: (0, 0))

    def body(gu_ref, wd_ref, x_ref, g_ref, b_ref, a_ref, h_ref, y_ref, yb_ref):
        ffn = jnp.zeros((tb, d), F32)
        for c in range(f // fc):
            gp = gu_ref[:, c * fc:(c + 1) * fc]
            act = (gp * _sigmoid(gp) * gu_ref[:, f + c * fc:f + (c + 1) * fc]).astype(a_ref.dtype)
            a_ref[:, c * fc:(c + 1) * fc] = act
            ffn = ffn + _dot(act, wd_ref[c * fc:(c + 1) * fc, :])
        hres = ALPHA * x_ref[...] + ffn
        xhat, _ = _ln_hat(hres)
        y = xhat * g_ref[...] + b_ref[...]
        h_ref[...] = hres
        y_ref[...] = y
        yb_ref[...] = y.astype(yb_ref.dtype)

    return pl.pallas_call(
        body, name="ffn_tail_fwd", grid=(t // tb,),
        in_specs=[pl.BlockSpec((tb, 2 * f), lambda i: (i, 0)), pl.BlockSpec((f, d), lambda i: (0, 0)),
                  blk, row_spec, row_spec],
        out_specs=[pl.BlockSpec((tb, f), lambda i: (i, 0)), blk, blk, blk],
        out_shape=[jax.ShapeDtypeStruct((t, f), ACT), jax.ShapeDtypeStruct((t, d), F32),
                   jax.ShapeDtypeStruct((t, d), F32), jax.ShapeDtypeStruct((t, d), ACT)],
        compiler_params=_cparams(("parallel",)),
    )(gu, wd, x1, g, b)


def _ffn_tail_bwd(dh, wd, gu, tb):
    t, d = dh.shape
    f = wd.shape[0]
    fc = _tile(f, MM_TILE)

    def body(dh_ref, wd_ref, gu_ref, dgu_ref):
        dh_v = dh_ref[...]
        for c in range(f // fc):
            da = _dot(dh_v, wd_ref[c * fc:(c + 1) * fc, :], NT)
            gp = gu_ref[:, c * fc:(c + 1) * fc]
            sg = _sigmoid(gp)
            dgu_ref[:, c * fc:(c + 1) * fc] = (
                da * gu_ref[:, f + c * fc:f + (c + 1) * fc] * sg * (1.0 + gp * (1.0 - sg))).astype(dgu_ref.dtype)
            dgu_ref[:, f + c * fc:f + (c + 1) * fc] = (da * gp * sg).astype(dgu_ref.dtype)

    return pl.pallas_call(
        body, name="ffn_tail_bwd", grid=(t // tb,),
        in_specs=[pl.BlockSpec((tb, d), lambda i: (i, 0)), pl.BlockSpec((f, d), lambda i: (0, 0)),
                  pl.BlockSpec((tb, 2 * f), lambda i: (i, 0))],
        out_specs=pl.BlockSpec((tb, 2 * f), lambda i: (i, 0)),
        out_shape=jax.ShapeDtypeStruct((t, 2 * f), ACT),
        compiler_params=_cparams(("parallel",)),
    )(dh, wd, gu)


def _ffn_head_bwd(dgu, wgu, dh2, hres, g, tb):
    t, d = dh2.shape
    f2 = wgu.shape[1]
    blk = pl.BlockSpec((tb, d), lambda i: (i, 0))
    acc = pl.BlockSpec((SUBLANES, d), lambda i: (0, 0))

    def body(dgu_ref, w_ref, dh2_ref, h_ref, g_ref, dh_ref, dhb_ref, dg_ref, db_ref):
        @pl.when(pl.program_id(0) == 0)
        def _():
            dg_ref[...] = jnp.zeros_like(dg_ref)
            db_ref[...] = jnp.zeros_like(db_ref)

        dy_v = _dot(dgu_ref[...], w_ref[...], NT) + ALPHA * dh2_ref[...]
        xhat, r = _ln_hat(h_ref[...])
        dh = _ln_bwd(dy_v * g_ref[...], xhat, r)
        dh_ref[...] = dh
        dhb_ref[...] = dh.astype(dhb_ref.dtype)
        dg_ref[...] += _fold8(dy_v * xhat)
        db_ref[...] += _fold8(dy_v)

    return pl.pallas_call(
        body, name="ffn_head_bwd", grid=(t // tb,),
        in_specs=[pl.BlockSpec((tb, f2), lambda i: (i, 0)), pl.BlockSpec((d, f2), lambda i: (0, 0)),
                  blk, blk, pl.BlockSpec((1, d), lambda i: (0, 0))],
        out_specs=[blk, blk, acc, acc],
        out_shape=[jax.ShapeDtypeStruct((t, d), F32), jax.ShapeDtypeStruct((t, d), ACT),
                   jax.ShapeDtypeStruct((SUBLANES, d), F32), jax.ShapeDtypeStruct((SUBLANES, d), F32)],
        compiler_params=_cparams(("arbitrary",)),
    )(dgu, wgu, dh2, hres, g)


def _ln_bwd_call(dy, hres, g, tb):
    t, d = dy.shape
    blk = pl.BlockSpec((tb, d), lambda i: (i, 0))
    acc = pl.BlockSpec((SUBLANES, d), lambda i: (0, 0))

    def body(dy_ref, h_ref, g_ref, dh_ref, dhb_ref, dg_ref, db_ref):
        @pl.when(pl.program_id(0) == 0)
        def _():
            dg_ref[...] = jnp.zeros_like(dg_ref)
            db_ref[...] = jnp.zeros_like(db_ref)

        dy_v = dy_ref[...]
        xhat, r = _ln_hat(h_ref[...])
        dh = _ln_bwd(dy_v * g_ref[...], xhat, r)
        dh_ref[...] = dh
        dhb_ref[...] = dh.astype(dhb_ref.dtype)
        dg_ref[...] += _fold8(dy_v * xhat)
        db_ref[...] += _fold8(dy_v)

    return pl.pallas_call(
        body, name="ln_bwd", grid=(t // tb,),
        in_specs=[blk, blk, pl.BlockSpec((1, d), lambda i: (0, 0))],
        out_specs=[blk, blk, acc, acc],
        out_shape=[jax.ShapeDtypeStruct((t, d), F32), jax.ShapeDtypeStruct((t, d), ACT),
                   jax.ShapeDtypeStruct((SUBLANES, d), F32), jax.ShapeDtypeStruct((SUBLANES, d), F32)],
        compiler_params=_cparams(("arbitrary",)),
    )(dy, hres, g)


def _loss_head(y, target, tb):
    t, d = y.shape
    blk = pl.BlockSpec((tb, d), lambda i: (i, 0))

    def body(y_ref, t_ref, dy_ref, l_ref):
        @pl.when(pl.program_id(0) == 0)
        def _():
            l_ref[...] = jnp.zeros_like(l_ref)

        err = y_ref[...] - t_ref[...]
        dy_ref[...] = err * (1.0 / d)
        sq = _fold8(err * err)
        part = sq[:, :LANES]
        for c in range(1, d // LANES):
            part = part + sq[:, c * LANES:(c + 1) * LANES]
        l_ref[...] += part

    return pl.pallas_call(
        body, name="loss_head", grid=(t // tb,),
        in_specs=[blk, blk],
        out_specs=[blk, pl.BlockSpec((SUBLANES, LANES), lambda i: (0, 0))],
        out_shape=[jax.ShapeDtypeStruct((t, d), F32), jax.ShapeDtypeStruct((SUBLANES, LANES), F32)],
        compiler_params=_cparams(("arbitrary",)),
    )(y, target)


def _adamw(w, g, m, v):
    shape = w.shape
    cols = shape[-1]
    w2, g2, m2, v2 = (a.reshape(-1, cols) for a in (w, g, m, v))
    rows = w2.shape[0]
    tr = _tile(rows, 256, SUBLANES)
    blk = pl.BlockSpec((tr, cols), lambda i: (i, 0))

    def body(w_ref, g_ref, m_ref, v_ref, d_ref, nm_ref, nv_ref):
        g_v = g_ref[...]
        nm = ADAM_B1 * m_ref[...] + (1.0 - ADAM_B1) * g_v
        nv = ADAM_B2 * v_ref[...] + (1.0 - ADAM_B2) * (g_v * g_v)
        m_hat = nm / (1.0 - ADAM_B1 ** ADAM_STEP)
        v_hat = nv / (1.0 - ADAM_B2 ** ADAM_STEP)
        d_ref[...] = -ADAM_LR * (m_hat / (jnp.sqrt(v_hat) + ADAM_EPS) + ADAM_WD * w_ref[...])
        nm_ref[...] = nm
        nv_ref[...] = nv

    out = jax.ShapeDtypeStruct((rows, cols), F32)
    res = pl.pallas_call(
        body, name="adamw", grid=(rows // tr,),
        in_specs=[blk] * 4, out_specs=[blk] * 3, out_shape=[out] * 3,
        compiler_params=_cparams(("parallel",)),
    )(w2, g2, m2, v2)
    return tuple(r.reshape(shape) for r in res)


def _place():
    x, y, c = lax.axis_index("x"), lax.axis_index("y"), lax.axis_index("c")
    return x, y, c, [(1 - x, y), (x, 1 - y), (1 - x, 1 - y)]


def _remote(src, dst, send_sems, recv_sems, k, to):
    return pltpu.make_async_remote_copy(src_ref=src, dst_ref=dst, send_sem=send_sems.at[k],
                                        recv_sem=recv_sems.at[k], device_id=to, device_id_type=MESH)


class _Carried:
    def __init__(self, inputs, out_shapes, n_sems, copies):
        self.inputs, self.out_shapes, self.n_sems, self.copies = list(inputs), list(out_shapes), n_sems, copies

    def scratch(self):
        return [pltpu.SemaphoreType.DMA((self.n_sems,)), pltpu.SemaphoreType.DMA((self.n_sems,))]


def _run_comm(name, plan):
    n_in, n_out = len(plan.inputs), len(plan.out_shapes)

    def body(*refs):
        start, finish = plan.copies(refs[:n_in], refs[n_in:n_in + n_out], refs[-2], refs[-1])
        start()
        finish()

    return pl.pallas_call(
        body, name=name, in_specs=[ANY] * n_in, out_specs=[ANY] * n_out, out_shape=plan.out_shapes,
        scratch_shapes=plan.scratch(),
    )(*plan.inputs)


def _half_rows(rows, core):
    if rows % (4 * SUBLANES):
        return None
    return pl.ds(pl.multiple_of(core * (rows // 2), 2 * SUBLANES), rows // 2)


def _all_gather_plan(shards):
    n = len(shards)

    def copies(x_refs, out_refs, send_sems, recv_sems):
        x, y, c, chips = _place()
        sibling = (x, y, 1 - c)
        mine = 2 * x + y
        split = [_half_rows(x_refs[t].shape[0], c) is not None for t in range(n)]

        def src(t):
            return x_refs[t].at[_half_rows(x_refs[t].shape[0], c)] if split[t] else x_refs[t]

        def slot(t, chip_idx, core):
            rows = _half_rows(x_refs[t].shape[0], core)
            return out_refs[t].at[chip_idx, rows] if split[t] else out_refs[t].at[chip_idx]

        def first():
            return [_remote(src(t), slot(t, mine, c), send_sems, recv_sems, 6 * t + j, (cx, cy, c))
                    for j, (cx, cy) in enumerate(chips) for t in range(n)]

        def start():
            for cp in first():
                cp.start()

        def finish():
            passed = []
            for j, (cx, cy) in enumerate(chips):
                for t in range(n):
                    theirs = slot(t, 2 * cx + cy, c)
                    _remote(theirs, theirs, send_sems, recv_sems, 6 * t + j, (cx, cy, c)).wait_recv()
                    if split[t]:
                        fwd = _remote(theirs, theirs, send_sems, recv_sems, 6 * t + 3 + j, sibling)
                        fwd.start()
                        passed.append(fwd)
            for j, (cx, cy) in enumerate(chips):
                for t in range(n):
                    if split[t]:
                        other = slot(t, 2 * cx + cy, 1 - c)
                        _remote(other, other, send_sems, recv_sems, 6 * t + 3 + j, sibling).wait_recv()
            for cp in first() + passed:
                cp.wait_send()

        return start, finish

    return _Carried(shards, [jax.ShapeDtypeStruct((N_CHIPS,) + s.shape, s.dtype) for s in shards], 6 * n, copies)


def _sibling_exchange_plan(grads, small=None):
    n = len(grads)
    extra = [] if small is None else [small]

    def copies(in_refs, out_refs, send_sems, recv_sems):
        x, y, c, _ = _place()
        sibling = (x, y, 1 - c)

        def all_copies():
            cps = [_remote(in_refs[t].at[:, _half_rows(in_refs[t].shape[1], 1 - c), :], out_refs[t],
                           send_sems, recv_sems, t, sibling) for t in range(n)]
            if extra:
                cps.append(_remote(in_refs[n], out_refs[n], send_sems, recv_sems, n, sibling))
            return cps

        def start():
            for cp in all_copies():
                cp.start()

        def finish():
            for cp in all_copies():
                cp.wait()

        return start, finish

    shapes = [jax.ShapeDtypeStruct((g.shape[0], g.shape[1] // 2, g.shape[2]), g.dtype) for g in grads]
    shapes += [jax.ShapeDtypeStruct(s.shape, s.dtype) for s in extra]
    return _Carried(list(grads) + extra, shapes, n + 1, copies)


def _chip_exchange_plan(travel, small=None):
    n = len(travel)
    extra = [] if small is None else [small]

    def copies(in_refs, out_refs, send_sems, recv_sems):
        x, y, c, chips = _place()
        mine = 2 * x + y

        def all_copies():
            cps = []
            for j, (cx, cy) in enumerate(chips):
                to = (cx, cy, c)
                for t in range(n):
                    cps.append(_remote(in_refs[t].at[2 * cx + cy], out_refs[t].at[mine], send_sems, recv_sems,
                                       3 * t + j, to))
                if extra:
                    cps.append(_remote(in_refs[n], out_refs[n].at[mine], send_sems, recv_sems, 3 * n + j, to))
            return cps

        def start():
            for cp in all_copies():
                cp.start()

        def finish():
            for cp in all_copies():
                cp.wait()

        return start, finish

    shapes = [jax.ShapeDtypeStruct(g.shape, g.dtype) for g in travel]
    shapes += [jax.ShapeDtypeStruct((N_CHIPS,) + s.shape, s.dtype) for s in extra]
    return _Carried(list(travel) + extra, shapes, 3 * n + 3, copies)


def _sibling_merge_plan(reduced):
    n = len(reduced)

    def copies(in_refs, out_refs, send_sems, recv_sems):
        x, y, c, _ = _place()

        def all_copies():
            return [_remote(in_refs[t], out_refs[t], send_sems, recv_sems, t, (x, y, 1 - c)) for t in range(n)]

        def start():
            for cp in all_copies():
                cp.start()

        def finish():
            for cp in all_copies():
                cp.wait()

        return start, finish

    return _Carried(reduced, [jax.ShapeDtypeStruct(r.shape, r.dtype) for r in reduced], n, copies)


def _pair_sum(place, grad, land):
    n, r, c = grad.shape
    half = r // 2
    tr = _tile(half, 256, SUBLANES)
    nb = half // tr

    def body(place_ref, a_ref, b_ref, travel_ref, own_ref):
        total = a_ref[0] + b_ref[0]
        travel_ref[0] = total.astype(travel_ref.dtype)

        @pl.when(pl.program_id(1) == place_ref[1])
        def _():
            own_ref[...] = total

    return pl.pallas_call(
        body, name="grad_pair_sum",
        grid_spec=pltpu.PrefetchScalarGridSpec(
            num_scalar_prefetch=1, grid=(nb, n),
            in_specs=[pl.BlockSpec((1, tr, c), lambda i, s, p: (s, p[0] * nb + i, 0)),
                      pl.BlockSpec((1, tr, c), lambda i, s, p: (s, i, 0))],
            out_specs=[pl.BlockSpec((1, tr, c), lambda i, s, p: (s, i, 0)),
                       pl.BlockSpec((tr, c), lambda i, s, p: (i, 0))]),
        out_shape=[jax.ShapeDtypeStruct((n, half, c), BF16), jax.ShapeDtypeStruct((half, c), F32)],
        compiler_params=_cparams(("parallel", "arbitrary")),
    )(place, grad, land)


def _chip_sum(place, own, land, name):
    n, r, c = land.shape
    tr = _tile(r, 256, SUBLANES)

    def body(place_ref, own_ref, land_ref, o_ref):
        mine = place_ref[1]
        acc = jnp.zeros(o_ref.shape, F32)
        for s in range(n):
            acc = acc + jnp.where(mine == s, own_ref[...], land_ref[s].astype(F32))
        o_ref[...] = acc

    return pl.pallas_call(
        body, name=name,
        grid_spec=pltpu.PrefetchScalarGridSpec(
            num_scalar_prefetch=1, grid=(r // tr,),
            in_specs=[pl.BlockSpec((tr, c), lambda i, p: (i, 0)),
                      pl.BlockSpec((n, tr, c), lambda i, p: (0, i, 0))],
            out_specs=pl.BlockSpec((tr, c), lambda i, p: (i, 0))),
        out_shape=jax.ShapeDtypeStruct((r, c), F32),
        compiler_params=_cparams(("parallel",)),
    )(place, own, land)


def _add2(a, b):
    rows = a.shape[0]
    tr = _tile(rows, 256, SUBLANES)
    blk = pl.BlockSpec((tr, PACK_W), lambda i: (i, 0))

    def body(a_ref, b_ref, o_ref):
        o_ref[...] = a_ref[...] + b_ref[...]

    return pl.pallas_call(
        body, name="grad_small_pair_sum", grid=(rows // tr,), in_specs=[blk, blk], out_specs=blk,
        out_shape=jax.ShapeDtypeStruct(a.shape, F32), compiler_params=_cparams(("parallel",)),
    )(a, b)


def _merge_halves(place, mine, other):
    first_core = place[0] == 0
    return jnp.concatenate([jnp.where(first_core, mine, other), jnp.where(first_core, other, mine)], axis=0)


_BIG = (("w_in", 2), ("w_pa", 1), ("w_pb", 1), ("w_o", 1), ("w_ffn_gate", 2), ("w_ffn_up", 2),
        ("w_ffn_down", 1))
_SMALL = ("conv_w", "a_log", "dt_bias", "o_norm_w", "sgu_ln_g", "sgu_ln_b", "w_s", "b_s",
          "ln1_g", "ln1_b", "ln2_g", "ln2_b")


def _pad_rows(flat, mult):
    rows = -(-flat.shape[-1] // (PACK_W * mult)) * mult
    pad = rows * PACK_W - flat.shape[-1]
    flat = jnp.pad(flat, [(0, 0)] * (flat.ndim - 1) + [(0, pad)])
    return flat.reshape(flat.shape[:-1] + (rows, PACK_W))


def _unshard(gathered, local, chip, axis):
    parts = [jnp.where(chip == s, local, gathered[s]) for s in range(N_CHIPS)]
    return jnp.concatenate(parts, axis=axis - 1)


def _to_shards(full, axis):
    l, r, c = full.shape
    if axis == 1:
        return full.reshape(l, N_CHIPS, r // N_CHIPS, c)
    return jnp.transpose(full.reshape(l, r, N_CHIPS, c // N_CHIPS), (0, 2, 1, 3))


def _row(v, width=None):
    v = v.reshape(1, -1).astype(F32)
    if width is not None and v.shape[1] < width:
        v = jnp.pad(v, ((0, 0), (0, width - v.shape[1])))
    return v


def _layer_consts(p, l, d):
    heads = d // DN_DK
    return dict(
        alog=_row(p["a_log"][l], LANES), dtb=_row(p["dt_bias"][l], LANES),
        onw=_row(jnp.tile(p["o_norm_w"][l], heads)),
        lng=_row(p["sgu_ln_g"][l]), lnb=_row(p["sgu_ln_b"][l]),
        ws=p["w_s"][l].astype(F32),
        bst=jnp.pad(p["b_s"][l].T, ((0, 0), (0, LANES - p["b_s"].shape[1]))),
        g1=_row(p["ln1_g"][l]), b1=_row(p["ln1_b"][l]), g2=_row(p["ln2_g"][l]), b2=_row(p["ln2_b"][l]))


class _NoComm:
    def with_proj_main(self):
        return None

    def after_proj_main(self, got):
        pass

    def weights(self, full):
        return full

    def with_dn_fwd(self):
        return None

    def after_dn_fwd(self, got):
        pass

    def with_ffn_in_dw(self):
        return None

    def after_ffn_in_dw(self, got):
        pass

    def after_branch_grads(self, g):
        pass

    def with_dn_bwd(self):
        return None

    def after_dn_bwd(self, got):
        pass

    def with_proj_main_dw(self):
        return None

    def after_proj_main_dw(self, got):
        pass

    def with_ffn_in(self):
        return None

    def after_ffn_in(self, got):
        pass

    def after_all_grads(self, g):
        pass

    def with_proj_main_dx(self):
        return None

    def after_proj_main_dx(self, got):
        pass


def _carry(carried, after, call, *args, **kw):
    if carried is None:
        return call(*args, **kw)
    out, got = call(*args, carried=carried, **kw)
    after(got)
    return out


def _in_proj_weights(w_in, d):
    heads, q4 = d // DN_DK, 4 * d
    wba = jnp.zeros((d, 2 * LANES), w_in.dtype)
    wba = wba.at[:, :heads].set(w_in[:, q4:q4 + heads])
    wba = wba.at[:, LANES:LANES + heads].set(w_in[:, q4 + heads:q4 + 2 * heads])
    return jnp.concatenate([w_in[:, :q4], w_in[:, q4 + 2 * heads:]], axis=1), wba


def _layer_fwd(x, xb, full, cl, d, tb, comm):
    wm, wba = _in_proj_weights(full["w_in"], d)
    projm = _carry(comm.with_proj_main(), comm.after_proj_main, _matmul, xb, wm, NN, "proj_main", tn=MM_WIDE)
    full = comm.weights(full)
    wl = dict(wm=wm, wba=wba, conv=full["conv_w"], wpa=full["w_pa"], wpb=full["w_pb"], wo=full["w_o"],
              wgu=jnp.concatenate([full["w_ffn_gate"], full["w_ffn_up"]], axis=1), wd=full["w_ffn_down"])
    ba = _matmul(xb, wba, NN, "proj_gates")
    qkv = _conv_fwd(projm, wl["conv"], d, _tile(x.shape[0], 2 * tb, SUBLANES))
    (o, states, ycors), got = _dn_fwd(qkv, ba, cl["alog"], cl["dtb"], d, comm.with_dn_fwd())
    comm.after_dn_fwd(got)
    ya, yb = _gate_sgu_fwd(o, projm, cl["onw"], cl["lng"], cl["lnb"], cl["ws"], cl["bst"], d)
    pa, pb, m, h1, x1, x1b = _mix_fwd(ya, yb, projm, x, wl["wpa"], wl["wpb"], wl["wo"], cl["g1"], cl["b1"], d, tb)
    gu = _carry(comm.with_ffn_in(), comm.after_ffn_in, _matmul, x1b, wl["wgu"], NN, "ffn_in")
    act, h2, x2, x2b = _ffn_tail_fwd(gu, wl["wd"], x1, cl["g2"], cl["b2"], tb)
    saved = dict(xb=xb, projm=projm, ba=ba, qkv=qkv, o=o, states=states, ycors=ycors, ya=ya, yb=yb,
                 pa=pa, pb=pb, m=m, h1=h1, x1b=x1b, gu=gu, act=act, h2=h2)
    return x2, x2b, saved, wl


def _layer_bwd(dx2, sv, wl, cl, d, tb, comm):
    g = {}
    dh2, dh2b, dg2, db2 = _ln_bwd_call(dx2, sv["h2"], cl["g2"], tb)
    g["ln2_g"], g["ln2_b"] = dg2.sum(0), db2.sum(0)
    g["wd"] = _matmul(sv["act"], dh2b, TN, "ffn_out_dw")
    dgu = _ffn_tail_bwd(dh2b, wl["wd"], sv["gu"], tb)
    g["wgu"] = _carry(comm.with_ffn_in_dw(), comm.after_ffn_in_dw, _matmul, sv["x1b"], dgu, TN, "ffn_in_dw")
    dh1, dh1b, dg1, db1 = _ffn_head_bwd(dgu, wl["wgu"], dh2, sv["h1"], cl["g1"], tb)
    g["ln1_g"], g["ln1_b"] = dg1.sum(0), db1.sum(0)
    g["wo"] = _matmul(sv["m"], dh1b, TN, "wo_dw")
    dpa, dpb, dya, dyb, dprojm = _mix_bwd(dh1b, sv["pa"], sv["pb"], sv["projm"], wl["wpa"], wl["wpb"], wl["wo"], d, tb)
    g["wpa"] = _matmul(sv["ya"], dpa, TN, "wpa_dw")
    g["wpb"] = _matmul(sv["yb"], dpb, TN, "wpb_dw")
    comm.after_branch_grads(g)
    do, dprojm, donw, dlng, dlnb, dws, dbst = _gate_sgu_bwd(
        dya, dyb, sv["o"], sv["projm"], cl["onw"], cl["lng"], cl["lnb"], cl["ws"], cl["bst"], dprojm, d)
    heads, groups = d // DN_DK, d // SGU_GROUP_DIM
    g["o_norm_w"], g["sgu_ln_g"], g["sgu_ln_b"] = donw.sum(0), dlng.sum(0), dlnb.sum(0)
    g["w_s"], g["b_s"] = dws, dbst[:, :groups].T
    (dqkv, dba, dal, ddt), got = _dn_bwd(sv["qkv"], sv["ba"], cl["alog"], cl["dtb"], do, sv["states"],
                                         sv["ycors"], d, comm.with_dn_bwd())
    comm.after_dn_bwd(got)
    g["a_log"], g["dt_bias"] = dal.sum(0)[:heads], ddt.sum(0)[:heads]
    tbc = _tile(dx2.shape[0], 2 * tb, SUBLANES)
    dy, dcw = _conv_bwd_dy(sv["projm"], wl["conv"], dqkv, d, tbc)
    g["conv_w"] = dcw.sum(1)
    dprojm = _conv_bwd_dx(dy, wl["conv"], dprojm, d, tbc)
    g["wm"] = _carry(comm.with_proj_main_dw(), comm.after_proj_main_dw, _matmul, sv["xb"], dprojm, TN,
                     "proj_main_dw", tn=MM_WIDE)
    g["wba"] = _matmul(sv["xb"], dba, TN, "proj_gates_dw")
    dx = _matmul(dba, wl["wba"], NT, "proj_gates_dx", add=dh1, coef=ALPHA)
    comm.after_all_grads(g)
    dx = _carry(comm.with_proj_main_dx(), comm.after_proj_main_dx, _matmul, dprojm, wl["wm"], NT, "proj_main_dx",
                add=dx, tk=MM_WIDE)
    return dx, g


_BRANCH = ("w_pa", "w_pb", "w_o", "w_ffn_gate", "w_ffn_up", "w_ffn_down")


def _grad_shards(g, d, keys):
    heads, q4 = d // DN_DK, 4 * d
    rows = lambda a: a.reshape(N_CHIPS, -1, a.shape[1])
    out = {}
    if "w_in" in keys:
        gm, gba, wsh = g["wm"], g["wba"], 2 * d + heads // 2
        out["w_in"] = jnp.stack([gm[:, :wsh],
                                 jnp.concatenate([gm[:, wsh:q4], gba[:, :heads]], axis=1),
                                 jnp.concatenate([gba[:, LANES:LANES + heads], gm[:, q4:q4 + wsh - heads]], axis=1),
                                 gm[:, q4 + wsh - heads:]])
    if "w_pa" in keys:
        ggu = g["wgu"]
        f = ggu.shape[1] // 2
        fs = f // N_CHIPS
        out.update({
            "w_pa": rows(g["wpa"]), "w_pb": rows(g["wpb"]), "w_o": rows(g["wo"]), "w_ffn_down": rows(g["wd"]),
            "w_ffn_gate": jnp.stack([ggu[:, s * fs:(s + 1) * fs] for s in range(N_CHIPS)]),
            "w_ffn_up": jnp.stack([ggu[:, f + s * fs:f + (s + 1) * fs] for s in range(N_CHIPS)])})
    return out


def _local_step(x, target, full0, full1_of, small_w, comm0=None):
    t, d = x.shape
    tb = _tile(t, 256, SUBLANES)
    comm0 = comm0 or _NoComm()
    consts = [_layer_consts(small_w, l, d) for l in range(DEPTH)]
    x1, x1b, sv0, w0 = _layer_fwd(x, x.astype(ACT), full0, consts[0], d, tb, comm0)
    x2, _, sv1, w1 = _layer_fwd(x1, x1b, full1_of(), consts[1], d, tb, _NoComm())
    dy, loss_parts = _loss_head(x2, target, tb)
    dy, g1 = _layer_bwd(dy, sv1, w1, consts[1], d, tb, _NoComm())
    comm0.layer1_grads = g1
    dy, g0 = _layer_bwd(dy, sv0, w0, consts[0], d, tb, comm0)
    return loss_parts, dy, [g0, g1]


def kernel(x, w_in, conv_w, a_log, dt_bias, o_norm_w, sgu_ln_g, sgu_ln_b, w_s, b_s, w_pa, w_pb, w_o, ln1_g, ln1_b, w_ffn_gate, w_ffn_up, w_ffn_down, ln2_g, ln2_b, loss_target, m_w_in, m_conv_w, m_a_log, m_dt_bias, m_o_norm_w, m_sgu_ln_g, m_sgu_ln_b, m_w_s, m_b_s, m_w_pa, m_w_pb, m_w_o, m_ln1_g, m_ln1_b, m_w_ffn_gate, m_w_ffn_up, m_w_ffn_down, m_ln2_g, m_ln2_b, v_w_in, v_conv_w, v_a_log, v_dt_bias, v_o_norm_w, v_sgu_ln_g, v_sgu_ln_b, v_w_s, v_b_s, v_w_pa, v_w_pb, v_w_o, v_ln1_g, v_ln1_b, v_w_ffn_gate, v_w_ffn_up, v_w_ffn_down, v_ln2_g, v_ln2_b):
    names = ("w_in", "conv_w", "a_log", "dt_bias", "o_norm_w", "sgu_ln_g", "sgu_ln_b", "w_s", "b_s", "w_pa",
             "w_pb", "w_o", "ln1_g", "ln1_b", "w_ffn_gate", "w_ffn_up", "w_ffn_down", "ln2_g", "ln2_b")
    w = dict(zip(names, (w_in, conv_w, a_log, dt_bias, o_norm_w, sgu_ln_g, sgu_ln_b, w_s, b_s, w_pa, w_pb, w_o,
                         ln1_g, ln1_b, w_ffn_gate, w_ffn_up, w_ffn_down, ln2_g, ln2_b)))
    mom = dict(zip(names, (m_w_in, m_conv_w, m_a_log, m_dt_bias, m_o_norm_w, m_sgu_ln_g, m_sgu_ln_b, m_w_s, m_b_s,
                           m_w_pa, m_w_pb, m_w_o, m_ln1_g, m_ln1_b, m_w_ffn_gate, m_w_ffn_up, m_w_ffn_down,
                           m_ln2_g, m_ln2_b)))
    var = dict(zip(names, (v_w_in, v_conv_w, v_a_log, v_dt_bias, v_o_norm_w, v_sgu_ln_g, v_sgu_ln_b, v_w_s, v_b_s,
                           v_w_pa, v_w_pb, v_w_o, v_ln1_g, v_ln1_b, v_w_ffn_gate, v_w_ffn_up, v_w_ffn_down,
                           v_ln2_g, v_ln2_b)))
    chip = 2 * lax.axis_index("x") + lax.axis_index("y")
    place = jnp.stack([lax.axis_index("c"), chip]).astype(jnp.int32)

    big = [k for k, _ in _BIG]
    axis_of = dict(_BIG)
    local = {k: w[k].astype(BF16) for k in big}
    local["conv_w"] = conv_w

    def gather_plan(l, keys):
        return _all_gather_plan([local[k][l] for k in keys])

    def full_of(l, keys, gathered):
        return {k: _unshard(gt, local[k][l], chip, axis_of.get(k, 2)) for k, gt in zip(keys, gathered)}

    def pair_sums(grads_l, keys, lands):
        return [_pair_sum(place, grads_l[k], land) for k, land in zip(keys, lands)]

    def chip_sums(pairs, lands):
        return [_chip_sum(place, p[1], land, "grad_chip_sum") for p, land in zip(pairs, lands)]

    class Layer0Comm(_NoComm):
        def with_proj_main(self):
            return gather_plan(0, _BRANCH)

        def after_proj_main(self, got):
            self.rest = full_of(0, _BRANCH, got)

        def weights(self, full):
            return {**full, **self.rest}

        def with_dn_fwd(self):
            return gather_plan(1, mixer)

        def after_dn_fwd(self, got):
            self.full1 = full_of(1, mixer, got)

        def with_ffn_in(self):
            return gather_plan(1, ffn)

        def after_ffn_in(self, got):
            self.full1.update(full_of(1, ffn, got))

        def with_ffn_in_dw(self):
            self.g1 = _grad_shards(self.layer1_grads, x.shape[-1], big)
            return _sibling_exchange_plan([self.g1[k] for k in big])

        def after_ffn_in_dw(self, got):
            self.pairs1 = pair_sums(self.g1, big, got)

        def with_dn_bwd(self):
            return _chip_exchange_plan([p[0] for p in self.pairs1])

        def after_dn_bwd(self, got):
            self.red1 = chip_sums(self.pairs1, got)

        def after_branch_grads(self, g0):
            shards = _grad_shards(g0, x.shape[-1], _BRANCH)
            lands = _run_comm("grad_sibling_exchange", _sibling_exchange_plan([shards[k] for k in _BRANCH]))
            self.pairs0 = pair_sums(shards, _BRANCH, lands)

        def with_proj_main_dw(self):
            return _chip_exchange_plan([p[0] for p in self.pairs0])

        def after_proj_main_dw(self, got):
            self.red0 = chip_sums(self.pairs0, got)

        def after_all_grads(self, g0):
            g_in = _grad_shards(g0, x.shape[-1], ["w_in"])["w_in"]
            self.small_g = {k: jnp.stack([g0[k], self.layer1_grads[k]]) for k in _SMALL}
            small = _pad_rows(jnp.concatenate([self.small_g[k].reshape(-1) for k in _SMALL]), SUBLANES)
            land, sland = _run_comm("grad_sibling_exchange_last", _sibling_exchange_plan([g_in], small))
            self.pair_in = _pair_sum(place, g_in, land)
            self.small_chip = _add2(small, sland)

        def with_proj_main_dx(self):
            return _chip_exchange_plan([self.pair_in[0]], self.small_chip)

        def after_proj_main_dx(self, got):
            self.red_in = _chip_sum(place, self.pair_in[1], got[0], "grad_chip_sum")
            self.small_total = _chip_sum(place, self.small_chip, got[1], "grad_small_chip_sum")

    comm = Layer0Comm()
    first, mixer, ffn = ["w_in", "conv_w"], ["w_in", "conv_w", "w_pa", "w_pb", "w_o"], list(_BRANCH[3:])
    full0 = full_of(0, first, _run_comm("all_gather_weights", gather_plan(0, first)))
    small_w = {k: w[k] for k in _SMALL if k != "conv_w"}
    loss_parts, grad_x, g = _local_step(x[0], loss_target[0], full0, lambda: comm.full1, small_w, comm)

    reduced = [comm.red_in] + comm.red0 + comm.red1
    others = _run_comm("grad_sibling_merge", _sibling_merge_plan(reduced))
    halves = [_merge_halves(place, mine, other) for mine, other in zip(reduced, others)]
    grads = {k: jnp.stack([halves[i], halves[len(big) + i]]) for i, k in enumerate(big)}
    small_total, off = comm.small_total.reshape(-1), 0
    for k in _SMALL:
        n = comm.small_g[k].size
        grads[k] = small_total[off:off + n].reshape(comm.small_g[k].shape)
        off += n
    grads["conv_w"] = lax.dynamic_index_in_dim(_to_shards(grads["conv_w"], 2), chip, 1, keepdims=False)

    delta, new_m, new_v = {}, {}, {}
    for k in [k for k, _ in _BIG] + ["conv_w"]:
        delta[k], new_m[k], new_v[k] = _adamw(w[k], grads[k], mom[k], var[k])
    rep = [k for k in _SMALL if k != "conv_w"]
    pack = lambda dct: _pad_rows(jnp.concatenate([dct[k].reshape(-1) for k in rep]), SUBLANES)
    packed = _adamw(pack(w), pack(grads), pack(mom), pack(var))
    off = 0
    for k in rep:
        n = w[k].size
        for dst, src in zip((delta, new_m, new_v), packed):
            dst[k] = src.reshape(-1)[off:off + n].reshape(w[k].shape)
        off += n

    loss = 0.5 * lax.psum(jnp.sum(loss_parts), ("x", "y", "c")) / x.shape[-1]
    return (loss, grad_x[None], *[grads[k] for k in names], *[delta[k] for k in names],
            *[new_m[k] for k in names], *[new_v[k] for k in names])
```

```python
import math

import jax
import jax.numpy as jnp
from jax import lax
from jax.experimental import pallas as pl
from jax.experimental.pallas import tpu as pltpu

F32 = jnp.float32
BF16 = jnp.bfloat16
MXU_DTYPE = jnp.bfloat16
ACT = jnp.bfloat16
HIGHEST = lax.Precision.HIGHEST

DEPTH = 2
CHUNK = 64
DN_GROUP = 2
SGU_BLOCK = 128
CONV_K = 4
DN_DK = 128
SGU_GROUP_DIM = 128
LN_EPS = 1e-5
RMS_EPS = 1e-6
ALPHA = (2 * DEPTH) ** 0.25
ADAM_LR, ADAM_B1, ADAM_B2, ADAM_EPS, ADAM_WD, ADAM_STEP = 0.001, 0.9, 0.999, 1e-08, 0.01, 10

LANES = 128
SUBLANES = 8
VMEM_LIMIT = 52 * 2 ** 20
N_CHIPS = 4

NN = ((1,), (0,))
NT = ((1,), (1,))
TN = ((0,), (0,))
MESH = pl.DeviceIdType.MESH
ANY = pl.BlockSpec(memory_space=pl.ANY)


def _dot(a, b, dims=NN, prec=None):
    if prec is None:
        a = a.astype(MXU_DTYPE)
        b = b.astype(MXU_DTYPE)
    return lax.dot_general(a, b, (dims, ((), ())), preferred_element_type=F32, precision=prec)


def _cparams(sem=None):
    return pltpu.CompilerParams(dimension_semantics=sem, vmem_limit_bytes=VMEM_LIMIT)


def _tile(dim, pref, unit=LANES):
    t = (min(pref, dim) // unit) * unit
    while t >= unit:
        if dim % t == 0:
            return t
        t -= unit
    return dim


def _fold8(x):
    r, n = x.shape
    return x.reshape(r // SUBLANES, SUBLANES, n).sum(axis=0)


def _sigmoid(x):
    return 1.0 / (1.0 + jnp.exp(-x))


def _gelu(x):
    return 0.5 * x * (1.0 + lax.erf(x * (2.0 ** -0.5)))


def _gelu_grad(x):
    return 0.5 * (1.0 + lax.erf(x * (2.0 ** -0.5))) + x * jnp.exp(-0.5 * x * x) * (2.0 * math.pi) ** -0.5


def _ln_hat(h):
    mu = jnp.mean(h, axis=-1, keepdims=True)
    xc = h - mu
    var = jnp.mean(xc * xc, axis=-1, keepdims=True)
    r = lax.rsqrt(var + LN_EPS)
    return xc * r, r


def _ln_bwd(dxhat, xhat, r):
    return r * (dxhat - jnp.mean(dxhat, axis=-1, keepdims=True)
                - xhat * jnp.mean(dxhat * xhat, axis=-1, keepdims=True))


MM_TILE = 1536
MM_WIDE = 2048


def _matmul(a, b, dims, name, out_dtype=F32, add=None, coef=1.0, tm=MM_TILE, tn=MM_TILE, tk=MM_TILE, carried=None):
    if dims == NN:
        (m, k), n = a.shape, b.shape[1]
    elif dims == NT:
        (m, k), n = a.shape, b.shape[0]
    else:
        (k, m), n = a.shape, b.shape[1]
    tm, tn, tk = _tile(m, tm), _tile(n, tn), _tile(k, tk)
    nk = k // tk
    a_spec = pl.BlockSpec((tk, tm), lambda j, i, q: (q, i)) if dims == TN else pl.BlockSpec((tm, tk), lambda j, i, q: (i, q))
    b_spec = pl.BlockSpec((tn, tk), lambda j, i, q: (j, q)) if dims == NT else pl.BlockSpec((tk, tn), lambda j, i, q: (q, j))
    o_spec = pl.BlockSpec((tm, tn), lambda j, i, q: (i, j))
    has_add = add is not None

    def body(*refs):
        a_ref, b_ref = refs[0], refs[1]
        add_ref = refs[2] if has_add else None
        o_ref, acc_ref = refs[2 + has_add], refs[3 + has_add]
        q = pl.program_id(2)
        part = _dot(a_ref[...], b_ref[...], dims)

        def finish(r):
            if has_add:
                r = r + coef * add_ref[...]
            o_ref[...] = r.astype(out_dtype)

        if nk == 1:
            finish(part)
        else:
            @pl.when(q == 0)
            def _():
                acc_ref[...] = part

            @pl.when(q > 0)
            def _():
                acc_ref[...] += part

            @pl.when(q == nk - 1)
            def _():
                finish(acc_ref[...])

    ins = [a, b] + ([add] if has_add else [])
    in_specs = [a_spec, b_spec] + ([o_spec] if has_add else [])
    grid = (n // tn, m // tm, nk)
    acc = pltpu.VMEM((tm, tn) if nk > 1 else (SUBLANES, LANES), F32)
    out = jax.ShapeDtypeStruct((m, n), out_dtype)
    if carried is None:
        return pl.pallas_call(
            body, name=name, grid=grid, in_specs=in_specs, out_specs=o_spec, out_shape=out, scratch_shapes=[acc],
            compiler_params=_cparams(("parallel", "parallel", "arbitrary")),
        )(*ins)
    res = pl.pallas_call(
        _carrying(body, len(ins), 1, 1, carried, grid), name=name + "_carrying", grid=grid,
        in_specs=in_specs + [ANY] * len(carried.inputs), out_specs=[o_spec] + [ANY] * len(carried.out_shapes),
        out_shape=[out] + carried.out_shapes, scratch_shapes=[acc] + carried.scratch(),
        compiler_params=_cparams(("arbitrary", "arbitrary", "arbitrary")),
    )(*ins, *carried.inputs)
    return res[0], res[1:]


def _conv_taps(cur_ref, halo_ref, first):
    x = cur_ref[...]
    tb = x.shape[0]
    halo = jnp.where(first, 0.0, halo_ref[...])
    xc = jnp.concatenate([halo, x], axis=0)
    return [x] + [pltpu.roll(xc, s, 0)[SUBLANES:SUBLANES + tb] for s in range(1, CONV_K)]


def _conv_fwd(projm, conv_w, d, tb):
    t = projm.shape[0]
    heads = d // DN_DK
    hb = tb // SUBLANES

    def body(cur_ref, halo_ref, w_ref, o_ref):
        i, j = pl.program_id(0), pl.program_id(1)
        taps = _conv_taps(cur_ref, halo_ref, i == 0)
        y = taps[0] * w_ref[CONV_K - 1:CONV_K, :]
        for s in range(1, CONV_K):
            y = y + taps[s] * w_ref[CONV_K - 1 - s:CONV_K - s, :]
        act = y * _sigmoid(y)
        scale = jnp.where(j == 0, DN_DK ** -0.5, 1.0)
        for h in range(heads):
            seg = act[:, h * DN_DK:(h + 1) * DN_DK]
            r = lax.rsqrt(jnp.sum(seg * seg, axis=1, keepdims=True) + RMS_EPS) * scale
            o_ref[:, h * DN_DK:(h + 1) * DN_DK] = seg * jnp.where(j < 2, r, 1.0)

    return pl.pallas_call(
        body, name="conv_fwd", grid=(t // tb, 3),
        in_specs=[pl.BlockSpec((tb, d), lambda i, j: (i, j)),
                  pl.BlockSpec((SUBLANES, d), lambda i, j: (jnp.maximum(i * hb - 1, 0), j)),
                  pl.BlockSpec((CONV_K, d), lambda i, j: (0, j))],
        out_specs=pl.BlockSpec((tb, d), lambda i, j: (i, j)),
        out_shape=jax.ShapeDtypeStruct((t, 3 * d), F32),
        compiler_params=_cparams(("parallel", "parallel")),
    )(projm, projm, conv_w)


def _conv_bwd_dy(projm, conv_w, dqkv, d, tb):
    t = projm.shape[0]
    heads = d // DN_DK
    hb = tb // SUBLANES

    def body(cur_ref, halo_ref, w_ref, dout_ref, dy_ref, dw_ref):
        j, i = pl.program_id(0), pl.program_id(1)
        taps = _conv_taps(cur_ref, halo_ref, i == 0)
        y = taps[0] * w_ref[CONV_K - 1:CONV_K, :]
        for s in range(1, CONV_K):
            y = y + taps[s] * w_ref[CONV_K - 1 - s:CONV_K - s, :]
        sg = _sigmoid(y)
        act = y * sg
        dact = sg * (1.0 + y * (1.0 - sg))
        scale = jnp.where(j == 0, DN_DK ** -0.5, 1.0)
        for h in range(heads):
            cols = slice(h * DN_DK, (h + 1) * DN_DK)
            seg = act[:, cols]
            r = lax.rsqrt(jnp.sum(seg * seg, axis=1, keepdims=True) + RMS_EPS)
            nrm = seg * r
            dout = dout_ref[:, cols]
            dn = dout * scale
            ds = jnp.where(j < 2, r * (dn - nrm * jnp.sum(dn * nrm, axis=1, keepdims=True)), dout)
            dy_ref[:, cols] = ds * dact[:, cols]
        dy = dy_ref[...]

        @pl.when(i == 0)
        def _():
            dw_ref[...] = jnp.zeros_like(dw_ref)

        for s in range(CONV_K):
            dw_ref[CONV_K - 1 - s] += _fold8(dy * taps[s])

    return pl.pallas_call(
        body, name="conv_bwd_dy", grid=(3, t // tb),
        in_specs=[pl.BlockSpec((tb, d), lambda j, i: (i, j)),
                  pl.BlockSpec((SUBLANES, d), lambda j, i: (jnp.maximum(i * hb - 1, 0), j)),
                  pl.BlockSpec((CONV_K, d), lambda j, i: (0, j)),
                  pl.BlockSpec((tb, d), lambda j, i: (i, j))],
        out_specs=[pl.BlockSpec((tb, d), lambda j, i: (i, j)),
                   pl.BlockSpec((CONV_K, SUBLANES, d), lambda j, i: (0, 0, j))],
        out_shape=[jax.ShapeDtypeStruct((t, 3 * d), F32),
                   jax.ShapeDtypeStruct((CONV_K, SUBLANES, 3 * d), F32)],
        compiler_params=_cparams(("parallel", "arbitrary")),
    )(projm, projm, conv_w, dqkv)


def _conv_bwd_dx(dy, conv_w, dprojm, d, tb):
    t = dy.shape[0]
    hb = tb // SUBLANES
    last = t // tb - 1

    def body(cur_ref, halo_ref, w_ref, alias_ref, o_ref):
        i = pl.program_id(0)
        cur = cur_ref[...]
        halo = jnp.where(i == last, 0.0, halo_ref[...])
        dc = jnp.concatenate([cur, halo], axis=0)
        acc = cur * w_ref[CONV_K - 1:CONV_K, :]
        for s in range(1, CONV_K):
            acc = acc + pltpu.roll(dc, tb + SUBLANES - s, 0)[:tb] * w_ref[CONV_K - 1 - s:CONV_K - s, :]
        o_ref[...] = acc.astype(o_ref.dtype)

    return pl.pallas_call(
        body, name="conv_bwd_dx", grid=(t // tb, 3),
        in_specs=[pl.BlockSpec((tb, d), lambda i, j: (i, j)),
                  pl.BlockSpec((SUBLANES, d), lambda i, j: (jnp.minimum((i + 1) * hb, t // SUBLANES - 1), j)),
                  pl.BlockSpec((CONV_K, d), lambda i, j: (0, j)),
                  ANY],
        out_specs=pl.BlockSpec((tb, d), lambda i, j: (i, j)),
        out_shape=jax.ShapeDtypeStruct(dprojm.shape, dprojm.dtype),
        input_output_aliases={3: 0},
        compiler_params=_cparams(("parallel", "parallel")),
    )(dy, dy, conv_w, dprojm)


def _beta_g(ba, alog, dtb):
    beta = _sigmoid(ba[:, :LANES])
    xa = ba[:, LANES:] + dtb
    softplus = jnp.maximum(xa, 0.0) + jnp.log(1.0 + jnp.exp(-jnp.abs(xa)))
    ea = jnp.exp(alog)
    return beta, -ea * softplus, ea, _sigmoid(xa)


def _inv_corrections(mats):
    ys = [-a for a in mats]
    ps = [_dot(a, a) for a in mats]
    steps = int(math.log2(CHUNK)) - 1
    for it in range(steps):
        ys = [y + p + _dot(y, p) for y, p in zip(ys, ps)]
        if it < steps - 1:
            ps = [_dot(p, p) for p in ps]
    return ys


def _chunk_masks():
    row = lax.broadcasted_iota(jnp.int32, (CHUNK, CHUNK), 0)
    col = lax.broadcasted_iota(jnp.int32, (CHUNK, CHUNK), 1)
    return row >= col, row > col, row <= col


def _col_of(mat, lane_idx, h):
    return jnp.sum(jnp.where(lane_idx == h, mat, 0.0), axis=1, keepdims=True)


def _row_of(mat, sub_idx, h):
    return jnp.sum(jnp.where(sub_idx == h, mat, 0.0), axis=0, keepdims=True)


def _carrying(compute, n_in, n_out, n_scratch, carried, grid):
    if carried is None:
        return compute
    ci, co = len(carried.inputs), len(carried.out_shapes)

    def body(*refs):
        ins, c_in = refs[:n_in], refs[n_in:n_in + ci]
        outs, c_out = refs[n_in + ci:n_in + ci + n_out], refs[n_in + ci + n_out:n_in + ci + n_out + co]
        scratch = refs[n_in + ci + n_out + co:]
        start, finish = carried.copies(c_in, c_out, scratch[n_scratch], scratch[n_scratch + 1])
        first, last = True, True
        for axis, steps in enumerate(grid):
            first = jnp.logical_and(first, pl.program_id(axis) == 0)
            last = jnp.logical_and(last, pl.program_id(axis) == steps - 1)

        @pl.when(first)
        def _():
            start()

        compute(*ins, *outs, *scratch[:n_scratch])

        @pl.when(last)
        def _():
            finish()

    return body


def _dn_fwd(qkv, ba, alog, dtb, d, carried=None):
    t = qkv.shape[0]
    heads = d // DN_DK
    n_chunks = t // CHUNK
    grp = DN_GROUP if n_chunks % DN_GROUP == 0 else 1
    span = grp * CHUNK
    extra = carried or _Carried([], [], 0, None)

    def compute(qkv_ref, ba_ref, al_ref, dt_ref, o_ref, s_ref, y_ref, state):
        @pl.when(pl.program_id(0) == 0)
        def _():
            state[...] = jnp.zeros_like(state)

        tril, strict, _ = _chunk_masks()
        beta, g, _, _ = _beta_g(ba_ref[...], al_ref[...], dt_ref[...])
        lane = lax.broadcasted_iota(jnp.int32, (CHUNK, LANES), 1)
        sub = lax.broadcasted_iota(jnp.int32, (LANES, CHUNK), 0)
        rowc = lax.broadcasted_iota(jnp.int32, (CHUNK, 1), 0)
        hs = range(heads)
        units = [(c, h) for c in range(grp) for h in hs]
        un = range(len(units))
        rows = lambda c: slice(c * CHUNK, (c + 1) * CHUNK)
        gc = [_dot(jnp.where(tril, 1.0, 0.0), g[rows(c)], NN, HIGHEST) for c in range(grp)]
        gct = [m.T for m in gc]
        q = [qkv_ref[rows(c), h * DN_DK:(h + 1) * DN_DK] for c, h in units]
        k = [qkv_ref[rows(c), d + h * DN_DK:d + (h + 1) * DN_DK] for c, h in units]
        v = [qkv_ref[rows(c), 2 * d + h * DN_DK:2 * d + (h + 1) * DN_DK] for c, h in units]
        gch = [_col_of(gc[c], lane, h) for c, h in units]
        bh = [_col_of(beta[rows(c)], lane, h) for c, h in units]
        dec = [jnp.where(tril, jnp.exp(gch[n] - _row_of(gct[c], sub, h)), 0.0) for n, (c, h) in enumerate(units)]
        egc = [jnp.exp(gch[n]) for n in un]
        gl = [jnp.sum(jnp.where(rowc == CHUNK - 1, gch[n], 0.0), axis=0, keepdims=True) for n in un]
        kb = [k[n] * bh[n] for n in un]
        a = [jnp.where(strict, _dot(kb[n], k[n], NT) * dec[n], 0.0) for n in un]
        p = [_dot(q[n], k[n], NT) * dec[n] for n in un]
        ycor = _inv_corrections(a)
        rhs = [jnp.concatenate([v[n] * bh[n], kb[n] * egc[n]], axis=1) for n in un]
        sol = [rhs[n] + _dot(ycor[n], rhs[n]) for n in un]
        qg = [q[n] * egc[n] for n in un]
        kd = [k[n] * jnp.exp(gl[n] - gch[n]) for n in un]
        egl = [jnp.exp(gl[n]) for n in un]
        s_cur, s_in, o = [state[h] for h in hs], [], []
        for c in range(grp):
            ns = [c * heads + h for h in hs]
            vn = [sol[n][:, :DN_DK] - _dot(sol[n][:, DN_DK:], s_cur[h]) for h, n in enumerate(ns)]
            o += [_dot(qg[n], s_cur[h]) + _dot(p[n], vn[h]) for h, n in enumerate(ns)]
            s_in += s_cur
            s_cur = [s_cur[h] * egl[n] + _dot(kd[n], vn[h], TN) for h, n in enumerate(ns)]
        for n, (c, h) in enumerate(units):
            o_ref[rows(c), h * DN_DK:(h + 1) * DN_DK] = o[n]
            s_ref[c, h] = s_in[n]
            y_ref[h, rows(c), :] = ycor[n]
        for h in hs:
            state[h] = s_cur[h]

    res = pl.pallas_call(
        _carrying(compute, 4, 3, 1, carried, (n_chunks // grp,)),
        name="dn_fwd_carrying" if carried else "dn_fwd", grid=(n_chunks // grp,),
        in_specs=[pl.BlockSpec((span, 3 * d), lambda i: (i, 0)),
                  pl.BlockSpec((span, 2 * LANES), lambda i: (i, 0)),
                  pl.BlockSpec((1, LANES), lambda i: (0, 0)),
                  pl.BlockSpec((1, LANES), lambda i: (0, 0))] + [ANY] * len(extra.inputs),
        out_specs=[pl.BlockSpec((span, d), lambda i: (i, 0)),
                   pl.BlockSpec((grp, heads, DN_DK, DN_DK), lambda i: (i, 0, 0, 0)),
                   pl.BlockSpec((heads, span, CHUNK), lambda i: (0, i, 0))] + [ANY] * len(extra.out_shapes),
        out_shape=[jax.ShapeDtypeStruct((t, d), F32),
                   jax.ShapeDtypeStruct((n_chunks, heads, DN_DK, DN_DK), F32),
                   jax.ShapeDtypeStruct((heads, t, CHUNK), F32)] + extra.out_shapes,
        scratch_shapes=[pltpu.VMEM((heads, DN_DK, DN_DK), F32)] + (extra.scratch() if carried else []),
        compiler_params=_cparams(("arbitrary",)),
    )(qkv, ba, alog, dtb, *extra.inputs)
    return res[:3], res[3:]


def _dn_bwd(qkv, ba, alog, dtb, dout, states, ycors, d, carried=None):
    t = qkv.shape[0]
    heads = d // DN_DK
    n_chunks = t // CHUNK
    grp = DN_GROUP if n_chunks % DN_GROUP == 0 else 1
    span = grp * CHUNK
    rev = lambda i: n_chunks // grp - 1 - i
    extra = carried or _Carried([], [], 0, None)

    def compute(qkv_ref, ba_ref, al_ref, dt_ref, do_ref, s_ref, y_ref,
                dqkv_ref, dba_ref, dal_ref, ddt_ref, dstate):
        @pl.when(pl.program_id(0) == 0)
        def _():
            dstate[...] = jnp.zeros_like(dstate)
            dal_ref[...] = jnp.zeros_like(dal_ref)
            ddt_ref[...] = jnp.zeros_like(ddt_ref)

        tril, strict, triu = _chunk_masks()
        beta, g, ea, sig_a = _beta_g(ba_ref[...], al_ref[...], dt_ref[...])
        lane = lax.broadcasted_iota(jnp.int32, (CHUNK, LANES), 1)
        sub = lax.broadcasted_iota(jnp.int32, (LANES, CHUNK), 0)
        rowc = lax.broadcasted_iota(jnp.int32, (CHUNK, 1), 0)
        hs = range(heads)
        units = [(c, h) for c in range(grp) for h in hs]
        un = range(len(units))
        rows = lambda c: slice(c * CHUNK, (c + 1) * CHUNK)
        rsum = lambda x_: jnp.sum(x_, axis=1, keepdims=True)
        gc = [_dot(jnp.where(tril, 1.0, 0.0), g[rows(c)], NN, HIGHEST) for c in range(grp)]
        gct = [m.T for m in gc]
        q = [qkv_ref[rows(c), h * DN_DK:(h + 1) * DN_DK] for c, h in units]
        k = [qkv_ref[rows(c), d + h * DN_DK:d + (h + 1) * DN_DK] for c, h in units]
        v = [qkv_ref[rows(c), 2 * d + h * DN_DK:2 * d + (h + 1) * DN_DK] for c, h in units]
        dout_h = [do_ref[rows(c), h * DN_DK:(h + 1) * DN_DK] for c, h in units]
        s0 = [s_ref[c, h] for c, h in units]
        ycor = [y_ref[h, rows(c), :] for c, h in units]
        gch = [_col_of(gc[c], lane, h) for c, h in units]
        bh = [_col_of(beta[rows(c)], lane, h) for c, h in units]
        dec = [jnp.where(tril, jnp.exp(gch[n] - _row_of(gct[c], sub, h)), 0.0) for n, (c, h) in enumerate(units)]
        egc = [jnp.exp(gch[n]) for n in un]
        gl = [jnp.sum(jnp.where(rowc == CHUNK - 1, gch[n], 0.0), axis=0, keepdims=True) for n in un]
        egl = [jnp.exp(gl[n]) for n in un]
        ekd = [jnp.exp(gl[n] - gch[n]) for n in un]
        kb = [k[n] * bh[n] for n in un]
        kd = [k[n] * ekd[n] for n in un]
        qg = [q[n] * egc[n] for n in un]
        kbg = [kb[n] * egc[n] for n in un]
        a = [jnp.where(strict, _dot(kb[n], k[n], NT) * dec[n], 0.0) for n in un]
        p = [_dot(q[n], k[n], NT) * dec[n] for n in un]
        rhs = [jnp.concatenate([v[n] * bh[n], kbg[n]], axis=1) for n in un]
        sol = [rhs[n] + _dot(ycor[n], rhs[n]) for n in un]
        w = [sol[n][:, DN_DK:] for n in un]
        vn = [sol[n][:, :DN_DK] - _dot(w[n], s0[n]) for n in un]
        dqg = [_dot(dout_h[n], s0[n], NT) for n in un]
        dp = [jnp.where(tril, _dot(dout_h[n], vn[n], NT), 0.0) for n in un]
        pdo = [_dot(p[n], dout_h[n], TN) for n in un]
        qdo = [_dot(qg[n], dout_h[n], TN) for n in un]
        ds_cur = [dstate[h] for h in hs]
        dsn, dvn = [None] * len(units), [None] * len(units)
        for c in reversed(range(grp)):
            for h in hs:
                dsn[c * heads + h] = ds_cur[h]
            for h in hs:
                n = c * heads + h
                dvn[n] = pdo[n] + _dot(kd[n], ds_cur[h])
            ds_cur = [qdo[c * heads + h] + egl[c * heads + h] * ds_cur[h]
                      - _dot(w[c * heads + h], dvn[c * heads + h], TN) for h in hs]
        dkd = [_dot(vn[n], dsn[n], NT) for n in un]
        dw = [-_dot(dvn[n], s0[n], NT) for n in un]
        dgl = [jnp.sum(rsum(dsn[n] * s0[n]), axis=0, keepdims=True) * egl[n] for n in un]
        dsol = [jnp.concatenate([dvn[n], dw[n]], axis=1) for n in un]
        drhs = [dsol[n] + _dot(ycor[n], dsol[n], TN) for n in un]
        dvb = [drhs[n][:, :DN_DK] for n in un]
        dkbg = [drhs[n][:, DN_DK:] for n in un]
        da = [jnp.where(strict, -_dot(drhs[n], sol[n], NT), 0.0) for n in un]
        dma = [da[n] * dec[n] for n in un]
        dmp = [dp[n] * dec[n] for n in un]
        dkb = [_dot(dma[n], k[n]) + dkbg[n] * egc[n] for n in un]
        dq = [_dot(dmp[n], k[n]) + dqg[n] * egc[n] for n in un]
        dk = [_dot(dma[n], kb[n], TN) + _dot(dmp[n], q[n], TN) + dkd[n] * ekd[n] + dkb[n] * bh[n] for n in un]
        e = [da[n] * a[n] + dp[n] * p[n] for n in un]
        colsum = [jnp.sum(e[n], axis=0, keepdims=True) for n in un]
        tkd = [rsum(dkd[n] * kd[n]) for n in un]
        for n, (c, h) in enumerate(units):
            dqkv_ref[rows(c), h * DN_DK:(h + 1) * DN_DK] = dq[n]
            dqkv_ref[rows(c), d + h * DN_DK:d + (h + 1) * DN_DK] = dk[n]
            dqkv_ref[rows(c), 2 * d + h * DN_DK:2 * d + (h + 1) * DN_DK] = dvb[n] * bh[n]
        for h in hs:
            dstate[h] = ds_cur[h]
        valid = lane < heads
        dal_acc = jnp.zeros((SUBLANES, LANES), F32)
        ddt_acc = jnp.zeros((SUBLANES, LANES), F32)
        for c in range(grp):
            dgc_all = jnp.zeros((CHUNK, LANES), F32)
            dbeta_all = jnp.zeros((CHUNK, LANES), F32)
            colsums = jnp.zeros((LANES, CHUNK), F32)
            for h in hs:
                n = c * heads + h
                dgc = rsum(e[n]) + rsum(dqg[n] * qg[n]) - tkd[n] + rsum(dkbg[n] * kbg[n])
                dgc = dgc + jnp.where(rowc == CHUNK - 1, dgl[n] + jnp.sum(tkd[n], axis=0, keepdims=True), 0.0)
                dgc_all = dgc_all + jnp.where(lane == h, dgc, 0.0)
                colsums = colsums + jnp.where(sub == h, colsum[n], 0.0)
                dbeta_all = dbeta_all + jnp.where(lane == h, rsum(dkb[n] * k[n]) + rsum(dvb[n] * v[n]), 0.0)
            dg = _dot(jnp.where(triu, 1.0, 0.0), dgc_all - colsums.T, NN, HIGHEST)
            beta_c = beta[rows(c)]
            dbl = jnp.where(valid, dbeta_all * beta_c * (1.0 - beta_c), 0.0)
            dal = jnp.where(valid, -dg * ea * sig_a[rows(c)], 0.0)
            dba_ref[rows(c), :LANES] = dbl.astype(dba_ref.dtype)
            dba_ref[rows(c), LANES:] = dal.astype(dba_ref.dtype)
            dal_acc = dal_acc + _fold8(jnp.where(valid, dg * g[rows(c)], 0.0))
            ddt_acc = ddt_acc + _fold8(dal)
        dal_ref[...] += dal_acc
        ddt_ref[...] += ddt_acc

    res = pl.pallas_call(
        _carrying(compute, 7, 4, 1, carried, (n_chunks // grp,)),
        name="dn_bwd_carrying" if carried else "dn_bwd", grid=(n_chunks // grp,),
        in_specs=[pl.BlockSpec((span, 3 * d), lambda i: (rev(i), 0)),
                  pl.BlockSpec((span, 2 * LANES), lambda i: (rev(i), 0)),
                  pl.BlockSpec((1, LANES), lambda i: (0, 0)),
                  pl.BlockSpec((1, LANES), lambda i: (0, 0)),
                  pl.BlockSpec((span, d), lambda i: (rev(i), 0)),
                  pl.BlockSpec((grp, heads, DN_DK, DN_DK), lambda i: (rev(i), 0, 0, 0)),
                  pl.BlockSpec((heads, span, CHUNK), lambda i: (0, rev(i), 0))] + [ANY] * len(extra.inputs),
        out_specs=[pl.BlockSpec((span, 3 * d), lambda i: (rev(i), 0)),
                   pl.BlockSpec((span, 2 * LANES), lambda i: (rev(i), 0)),
                   pl.BlockSpec((SUBLANES, LANES), lambda i: (0, 0)),
                   pl.BlockSpec((SUBLANES, LANES), lambda i: (0, 0))] + [ANY] * len(extra.out_shapes),
        out_shape=[jax.ShapeDtypeStruct((t, 3 * d), F32),
                   jax.ShapeDtypeStruct((t, 2 * LANES), ACT),
                   jax.ShapeDtypeStruct((SUBLANES, LANES), F32),
                   jax.ShapeDtypeStruct((SUBLANES, LANES), F32)] + extra.out_shapes,
        scratch_shapes=[pltpu.VMEM((heads, DN_DK, DN_DK), F32)] + (extra.scratch() if carried else []),
        compiler_params=_cparams(("arbitrary",)),
    )(qkv, ba, alog, dtb, dout, states, ycors, *extra.inputs)
    return res[:4], res[4:]


def _sgu_mask():
    row = lax.broadcasted_iota(jnp.int32, (SGU_BLOCK, SGU_BLOCK), 0)
    col = lax.broadcasted_iota(jnp.int32, (SGU_BLOCK, SGU_BLOCK), 1)
    sh = int(math.log2(CHUNK))
    return lax.shift_right_logical(row, sh) >= lax.shift_right_logical(col, sh)


def _gate_sgu_fwd(o, projm, onw, lng, lnb, ws, bst, d):
    t = o.shape[0]
    heads, groups = d // DN_DK, d // SGU_GROUP_DIM
    tb = SGU_BLOCK
    row_spec = pl.BlockSpec((1, d), lambda i: (0, 0))

    def body(o_ref, z_ref, u_ref, v_ref, onw_ref, lng_ref, lnb_ref, ws_ref, bst_ref, ya_ref, yb_ref):
        for h in range(heads):
            cols = slice(h * DN_DK, (h + 1) * DN_DK)
            oh, zh = o_ref[:, cols], z_ref[:, cols]
            r = lax.rsqrt(jnp.mean(oh * oh, axis=1, keepdims=True) + RMS_EPS)
            ya_ref[:, cols] = (oh * r * onw_ref[:, cols] * (zh * _sigmoid(zh))).astype(ya_ref.dtype)
        xhat, _ = _ln_hat(_gelu(v_ref[...]))
        vgn = xhat * lng_ref[...] + lnb_ref[...]
        mask = _sgu_mask()
        lane = lax.broadcasted_iota(jnp.int32, (SGU_BLOCK, LANES), 1)
        bst_v = bst_ref[...]
        for gi in range(groups):
            cols = slice(gi * SGU_GROUP_DIM, (gi + 1) * SGU_GROUP_DIM)
            wsg = jnp.where(mask, ws_ref[gi], 0.0)
            sp = _dot(wsg, vgn[:, cols]) + _col_of(bst_v, lane, gi)
            yb_ref[:, cols] = (_gelu(u_ref[:, cols]) * sp).astype(yb_ref.dtype)

    return pl.pallas_call(
        body, name="gate_sgu_fwd", grid=(t // tb,),
        in_specs=[pl.BlockSpec((tb, d), lambda i: (i, 0)),
                  pl.BlockSpec((tb, d), lambda i: (i, 3)),
                  pl.BlockSpec((tb, d), lambda i: (i, 4)),
                  pl.BlockSpec((tb, d), lambda i: (i, 5)),
                  row_spec, row_spec, row_spec,
                  pl.BlockSpec((groups, SGU_BLOCK, SGU_BLOCK), lambda i: (0, 0, 0)),
                  pl.BlockSpec((SGU_BLOCK, LANES), lambda i: (0, 0))],
        out_specs=[pl.BlockSpec((tb, d), lambda i: (i, 0)), pl.BlockSpec((tb, d), lambda i: (i, 0))],
        out_shape=[jax.ShapeDtypeStruct((t, d), ACT), jax.ShapeDtypeStruct((t, d), ACT)],
        compiler_params=_cparams(("parallel",)),
    )(o, projm, projm, projm, onw, lng, lnb, ws, bst)


def _gate_sgu_bwd(dya, dyb, o, projm, onw, lng, lnb, ws, bst, dprojm, d):
    t = o.shape[0]
    heads, groups = d // DN_DK, d // SGU_GROUP_DIM
    tb = SGU_BLOCK
    row_spec = pl.BlockSpec((1, d), lambda i: (0, 0))
    acc_row = pl.BlockSpec((SUBLANES, d), lambda i: (0, 0))

    def body(dya_ref, dyb_ref, o_ref, z_ref, u_ref, v_ref, onw_ref, lng_ref, lnb_ref, ws_ref, bst_ref, alias_ref,
             do_ref, dp_ref, donw_ref, dlng_ref, dlnb_ref, dws_ref, dbst_ref):
        @pl.when(pl.program_id(0) == 0)
        def _():
            for r_ in (donw_ref, dlng_ref, dlnb_ref, dws_ref, dbst_ref):
                r_[...] = jnp.zeros_like(r_)

        donw = jnp.zeros((SUBLANES, DN_DK), F32)
        for h in range(heads):
            cols = slice(h * DN_DK, (h + 1) * DN_DK)
            oh, zh, dyah, wh = o_ref[:, cols], z_ref[:, cols], dya_ref[:, cols], onw_ref[:, cols]
            r = lax.rsqrt(jnp.mean(oh * oh, axis=1, keepdims=True) + RMS_EPS)
            on = oh * r
            sz = _sigmoid(zh)
            silu_z = zh * sz
            don = dyah * wh * silu_z
            dp_ref[:, cols] = (dyah * on * wh * (sz * (1.0 + zh * (1.0 - sz)))).astype(dp_ref.dtype)
            donw = donw + _fold8(dyah * on * silu_z)
            do_ref[:, cols] = r * (don - on * jnp.mean(don * on, axis=1, keepdims=True))
        donw_ref[...] += donw

        vgp, up = v_ref[...], u_ref[...]
        xhat, rstd = _ln_hat(_gelu(vgp))
        lng_v = lng_ref[...]
        vgn = xhat * lng_v + lnb_ref[...]
        ua = _gelu(up)
        mask = _sgu_mask()
        lane = lax.broadcasted_iota(jnp.int32, (SGU_BLOCK, LANES), 1)
        bst_v = bst_ref[...]
        dbst = jnp.zeros((SGU_BLOCK, LANES), F32)
        dvgn_parts, dua_parts = [], []
        for gi in range(groups):
            cols = slice(gi * SGU_GROUP_DIM, (gi + 1) * SGU_GROUP_DIM)
            wsg = jnp.where(mask, ws_ref[gi], 0.0)
            vg_g, dyb_g = vgn[:, cols], dyb_ref[:, cols]
            sp = _dot(wsg, vg_g) + _col_of(bst_v, lane, gi)
            dsp = dyb_g * ua[:, cols]
            dua_parts.append(dyb_g * sp)
            dws_ref[gi] += jnp.where(mask, _dot(dsp, vg_g, NT), 0.0)
            dbst = dbst + jnp.where(lane == gi, jnp.sum(dsp, axis=1, keepdims=True), 0.0)
            dvgn_parts.append(_dot(wsg, dsp, TN))
        dbst_ref[...] += dbst
        dvgn = jnp.concatenate(dvgn_parts, axis=1)
        dua = jnp.concatenate(dua_parts, axis=1)
        dlng_ref[...] += _fold8(dvgn * xhat)
        dlnb_ref[...] += _fold8(dvgn)
        dvga = _ln_bwd(dvgn * lng_v, xhat, rstd)
        dp_ref[:, d:2 * d] = (dua * _gelu_grad(up)).astype(dp_ref.dtype)
        dp_ref[:, 2 * d:] = (dvga * _gelu_grad(vgp)).astype(dp_ref.dtype)

    return pl.pallas_call(
        body, name="gate_sgu_bwd", grid=(t // tb,),
        in_specs=[pl.BlockSpec((tb, d), lambda i: (i, 0)),
                  pl.BlockSpec((tb, d), lambda i: (i, 0)),
                  pl.BlockSpec((tb, d), lambda i: (i, 0)),
                  pl.BlockSpec((tb, d), lambda i: (i, 3)),
                  pl.BlockSpec((tb, d), lambda i: (i, 4)),
                  pl.BlockSpec((tb, d), lambda i: (i, 5)),
                  row_spec, row_spec, row_spec,
                  pl.BlockSpec((groups, SGU_BLOCK, SGU_BLOCK), lambda i: (0, 0, 0)),
                  pl.BlockSpec((SGU_BLOCK, LANES), lambda i: (0, 0)),
                  ANY],
        out_specs=[pl.BlockSpec((tb, d), lambda i: (i, 0)),
                   pl.BlockSpec((tb, 3 * d), lambda i: (i, 1)),
                   pl.BlockSpec((SUBLANES, DN_DK), lambda i: (0, 0)),
                   acc_row, acc_row,
                   pl.BlockSpec((groups, SGU_BLOCK, SGU_BLOCK), lambda i: (0, 0, 0)),
                   pl.BlockSpec((SGU_BLOCK, LANES), lambda i: (0, 0))],
        out_shape=[jax.ShapeDtypeStruct((t, d), F32),
                   jax.ShapeDtypeStruct(dprojm.shape, dprojm.dtype),
                   jax.ShapeDtypeStruct((SUBLANES, DN_DK), F32),
                   jax.ShapeDtypeStruct((SUBLANES, d), F32),
                   jax.ShapeDtypeStruct((SUBLANES, d), F32),
                   jax.ShapeDtypeStruct((groups, SGU_BLOCK, SGU_BLOCK), F32),
                   jax.ShapeDtypeStruct((SGU_BLOCK, LANES), F32)],
        input_output_aliases={11: 1},
        compiler_params=_cparams(("arbitrary",)),
    )(dya, dyb, o, projm, projm, projm, onw, lng, lnb, ws, bst, dprojm)


def _mix_fwd(ya, yb, projm, x, wpa, wpb, wo, g1, b1, d, tb):
    t = x.shape[0]
    blk = pl.BlockSpec((tb, d), lambda i: (i, 0))
    wspec = pl.BlockSpec((d, d), lambda i: (0, 0))
    row_spec = pl.BlockSpec((1, d), lambda i: (0, 0))

    def body(ya_ref, yb_ref, ga_ref, gb_ref, x_ref, wpa_ref, wpb_ref, wo_ref, g_ref, b_ref,
             pa_ref, pb_ref, m_ref, h_ref, x1_ref, x1b_ref):
        pa = _dot(ya_ref[...], wpa_ref[...])
        pb = _dot(yb_ref[...], wpb_ref[...])
        m = _sigmoid(ga_ref[...]) * pa + _sigmoid(gb_ref[...]) * pb
        hres = ALPHA * x_ref[...] + _dot(m, wo_ref[...])
        xhat, _ = _ln_hat(hres)
        x1 = xhat * g_ref[...] + b_ref[...]
        pa_ref[...] = pa
        pb_ref[...] = pb
        m_ref[...] = m.astype(m_ref.dtype)
        h_ref[...] = hres
        x1_ref[...] = x1
        x1b_ref[...] = x1.astype(x1b_ref.dtype)

    f32_out = jax.ShapeDtypeStruct((t, d), F32)
    bf_out = jax.ShapeDtypeStruct((t, d), ACT)
    return pl.pallas_call(
        body, name="mix_fwd", grid=(t // tb,),
        in_specs=[blk, blk, pl.BlockSpec((tb, d), lambda i: (i, 6)), pl.BlockSpec((tb, d), lambda i: (i, 7)),
                  blk, wspec, wspec, wspec, row_spec, row_spec],
        out_specs=[blk] * 6,
        out_shape=[f32_out, f32_out, bf_out, f32_out, f32_out, bf_out],
        compiler_params=_cparams(("parallel",)),
    )(ya, yb, projm, projm, x, wpa, wpb, wo, g1, b1)


def _mix_bwd(dmix, pa, pb, projm, wpa, wpb, wo, d, tb):
    t = dmix.shape[0]
    blk = pl.BlockSpec((tb, d), lambda i: (i, 0))
    wspec = pl.BlockSpec((d, d), lambda i: (0, 0))

    def body(dmix_ref, pa_ref, pb_ref, ga_ref, gb_ref, wpa_ref, wpb_ref, wo_ref,
             dpa_ref, dpb_ref, dya_ref, dyb_ref, dg_ref):
        dm = _dot(dmix_ref[...], wo_ref[...], NT)
        sa, sb = _sigmoid(ga_ref[...]), _sigmoid(gb_ref[...])
        dpa, dpb = dm * sa, dm * sb
        dpa_ref[...] = dpa.astype(dpa_ref.dtype)
        dpb_ref[...] = dpb.astype(dpb_ref.dtype)
        dg_ref[:, :d] = (dm * pa_ref[...] * sa * (1.0 - sa)).astype(dg_ref.dtype)
        dg_ref[:, d:] = (dm * pb_ref[...] * sb * (1.0 - sb)).astype(dg_ref.dtype)
        dya_ref[...] = _dot(dpa, wpa_ref[...], NT)
        dyb_ref[...] = _dot(dpb, wpb_ref[...], NT)

    return pl.pallas_call(
        body, name="mix_bwd", grid=(t // tb,),
        in_specs=[blk, blk, blk, pl.BlockSpec((tb, d), lambda i: (i, 6)), pl.BlockSpec((tb, d), lambda i: (i, 7)),
                  wspec, wspec, wspec],
        out_specs=[blk, blk, blk, blk, pl.BlockSpec((tb, 2 * d), lambda i: (i, 3))],
        out_shape=[jax.ShapeDtypeStruct((t, d), ACT), jax.ShapeDtypeStruct((t, d), ACT),
                   jax.ShapeDtypeStruct((t, d), F32), jax.ShapeDtypeStruct((t, d), F32),
                   jax.ShapeDtypeStruct((t, 8 * d), ACT)],
        compiler_params=_cparams(("parallel",)),
    )(dmix, pa, pb, projm, projm, wpa, wpb, wo)


def _ffn_tail_fwd(gu, wd, x1, g, b, tb):
    t, d = x1.shape
    f = wd.shape[0]
    fc = _tile(f, MM_TILE)
    blk = pl.BlockSpec((tb, d), lambda i: (i, 0))
    row_spec = pl.BlockSpec((1, d), lambda i: (0, 0))

    def body(gu_ref, wd_ref, x_ref, g_ref, b_ref, a_ref, h_ref, y_ref, yb_ref):
        ffn = jnp.zeros((tb, d), F32)
        for c in range(f // fc):
            gp = gu_ref[:, c * fc:(c + 1) * fc]
            act = (gp * _sigmoid(gp) * gu_ref[:, f + c * fc:f + (c + 1) * fc]).astype(a_ref.dtype)
            a_ref[:, c * fc:(c + 1) * fc] = act
            ffn = ffn + _dot(act, wd_ref[c * fc:(c + 1) * fc, :])
        hres = ALPHA * x_ref[...] + ffn
        xhat, _ = _ln_hat(hres)
        y = xhat * g_ref[...] + b_ref[...]
        h_ref[...] = hres
        y_ref[...] = y
        yb_ref[...] = y.astype(yb_ref.dtype)

    return pl.pallas_call(
        body, name="ffn_tail_fwd", grid=(t // tb,),
        in_specs=[pl.BlockSpec((tb, 2 * f), lambda i: (i, 0)), pl.BlockSpec((f, d), lambda i: (0, 0)),
                  blk, row_spec, row_spec],
        out_specs=[pl.BlockSpec((tb, f), lambda i: (i, 0)), blk, blk, blk],
        out_shape=[jax.ShapeDtypeStruct((t, f), ACT), jax.ShapeDtypeStruct((t, d), F32),
                   jax.ShapeDtypeStruct((t, d), F32), jax.ShapeDtypeStruct((t, d), ACT)],
        compiler_params=_cparams(("parallel",)),
    )(gu, wd, x1, g, b)


def _ffn_tail_bwd(dh, wd, gu, tb):
    t, d = dh.shape
    f = wd.shape[0]
    fc = _tile(f, MM_TILE)

    def body(dh_ref, wd_ref, gu_ref, dgu_ref):
        dh_v = dh_ref[...]
        for c in range(f // fc):
            da = _dot(dh_v, wd_ref[c * fc:(c + 1) * fc, :], NT)
            gp = gu_ref[:, c * fc:(c + 1) * fc]
            sg = _sigmoid(gp)
            dgu_ref[:, c * fc:(c + 1) * fc] = (
                da * gu_ref[:, f + c * fc:f + (c + 1) * fc] * sg * (1.0 + gp * (1.0 - sg))).astype(dgu_ref.dtype)
            dgu_ref[:, f + c * fc:f + (c + 1) * fc] = (da * gp * sg).astype(dgu_ref.dtype)

    return pl.pallas_call(
        body, name="ffn_tail_bwd", grid=(t // tb,),
        in_specs=[pl.BlockSpec((tb, d), lambda i: (i, 0)), pl.BlockSpec((f, d), lambda i: (0, 0)),
                  pl.BlockSpec((tb, 2 * f), lambda i: (i, 0))],
        out_specs=pl.BlockSpec((tb, 2 * f), lambda i: (i, 0)),
        out_shape=jax.ShapeDtypeStruct((t, 2 * f), ACT),
        compiler_params=_cparams(("parallel",)),
    )(dh, wd, gu)


def _ffn_head_bwd(dgu, wgu, dh2, hres, g, tb):
    t, d = dh2.shape
    f2 = wgu.shape[1]
    blk = pl.BlockSpec((tb, d), lambda i: (i, 0))
    acc = pl.BlockSpec((SUBLANES, d), lambda i: (0, 0))

    def body(dgu_ref, w_ref, dh2_ref, h_ref, g_ref, dh_ref, dhb_ref, dg_ref, db_ref):
        @pl.when(pl.program_id(0) == 0)
        def _():
            dg_ref[...] = jnp.zeros_like(dg_ref)
            db_ref[...] = jnp.zeros_like(db_ref)

        dy_v = _dot(dgu_ref[...], w_ref[...], NT) + ALPHA * dh2_ref[...]
        xhat, r = _ln_hat(h_ref[...])
        dh = _ln_bwd(dy_v * g_ref[...], xhat, r)
        dh_ref[...] = dh
        dhb_ref[...] = dh.astype(dhb_ref.dtype)
        dg_ref[...] += _fold8(dy_v * xhat)
        db_ref[...] += _fold8(dy_v)

    return pl.pallas_call(
        body, name="ffn_head_bwd", grid=(t // tb,),
        in_specs=[pl.BlockSpec((tb, f2), lambda i: (i, 0)), pl.BlockSpec((d, f2), lambda i: (0, 0)),
                  blk, blk, pl.BlockSpec((1, d), lambda i: (0, 0))],
        out_specs=[blk, blk, acc, acc],
        out_shape=[jax.ShapeDtypeStruct((t, d), F32), jax.ShapeDtypeStruct((t, d), ACT),
                   jax.ShapeDtypeStruct((SUBLANES, d), F32), jax.ShapeDtypeStruct((SUBLANES, d), F32)],
        compiler_params=_cparams(("arbitrary",)),
    )(dgu, wgu, dh2, hres, g)


def _ln_bwd_call(dy, hres, g, tb):
    t, d = dy.shape
    blk = pl.BlockSpec((tb, d), lambda i: (i, 0))
    acc = pl.BlockSpec((SUBLANES, d), lambda i: (0, 0))

    def body(dy_ref, h_ref, g_ref, dh_ref, dhb_ref, dg_ref, db_ref):
        @pl.when(pl.program_id(0) == 0)
        def _():
            dg_ref[...] = jnp.zeros_like(dg_ref)
            db_ref[...] = jnp.zeros_like(db_ref)

        dy_v = dy_ref[...]
        xhat, r = _ln_hat(h_ref[...])
        dh = _ln_bwd(dy_v * g_ref[...], xhat, r)
        dh_ref[...] = dh
        dhb_ref[...] = dh.astype(dhb_ref.dtype)
        dg_ref[...] += _fold8(dy_v * xhat)
        db_ref[...] += _fold8(dy_v)

    return pl.pallas_call(
        body, name="ln_bwd", grid=(t // tb,),
        in_specs=[blk, blk, pl.BlockSpec((1, d), lambda i: (0, 0))],
        out_specs=[blk, blk, acc, acc],
        out_shape=[jax.ShapeDtypeStruct((t, d), F32), jax.ShapeDtypeStruct((t, d), ACT),
                   jax.ShapeDtypeStruct((SUBLANES, d), F32), jax.ShapeDtypeStruct((SUBLANES, d), F32)],
        compiler_params=_cparams(("arbitrary",)),
    )(dy, hres, g)


def _loss_ln_bwd(y, target, hres, g, tb):
    t, d = y.shape
    blk = pl.BlockSpec((tb, d), lambda i: (i, 0))
    acc = pl.BlockSpec((SUBLANES, d), lambda i: (0, 0))

    def body(y_ref, t_ref, h_ref, g_ref, dh_ref, dhb_ref, dg_ref, db_ref, l_ref):
        @pl.when(pl.program_id(0) == 0)
        def _():
            for r_ in (dg_ref, db_ref, l_ref):
                r_[...] = jnp.zeros_like(r_)

        err = y_ref[...] - t_ref[...]
        dy_v = err * (1.0 / d)
        sq = _fold8(err * err)
        part = sq[:, :LANES]
        for c in range(1, d // LANES):
            part = part + sq[:, c * LANES:(c + 1) * LANES]
        l_ref[...] += part
        xhat, r = _ln_hat(h_ref[...])
        dh = _ln_bwd(dy_v * g_ref[...], xhat, r)
        dh_ref[...] = dh
        dhb_ref[...] = dh.astype(dhb_ref.dtype)
        dg_ref[...] += _fold8(dy_v * xhat)
        db_ref[...] += _fold8(dy_v)

    res = pl.pallas_call(
        body, name="loss_ln_bwd", grid=(t // tb,),
        in_specs=[blk, blk, blk, pl.BlockSpec((1, d), lambda i: (0, 0))],
        out_specs=[blk, blk, acc, acc, pl.BlockSpec((SUBLANES, LANES), lambda i: (0, 0))],
        out_shape=[jax.ShapeDtypeStruct((t, d), F32), jax.ShapeDtypeStruct((t, d), ACT),
                   jax.ShapeDtypeStruct((SUBLANES, d), F32), jax.ShapeDtypeStruct((SUBLANES, d), F32),
                   jax.ShapeDtypeStruct((SUBLANES, LANES), F32)],
        compiler_params=_cparams(("arbitrary",)),
    )(y, target, hres, g)
    return res[:4], res[4]


def _adamw(w, g, m, v):
    shape = w.shape
    cols = shape[-1]
    w2, g2, m2, v2 = (a.reshape(-1, cols) for a in (w, g, m, v))
    rows = w2.shape[0]
    tr = _tile(rows, 256, SUBLANES)
    blk = pl.BlockSpec((tr, cols), lambda i: (i, 0))

    def body(w_ref, g_ref, m_ref, v_ref, d_ref, nm_ref, nv_ref):
        g_v = g_ref[...]
        nm = ADAM_B1 * m_ref[...] + (1.0 - ADAM_B1) * g_v
        nv = ADAM_B2 * v_ref[...] + (1.0 - ADAM_B2) * (g_v * g_v)
        m_hat = nm / (1.0 - ADAM_B1 ** ADAM_STEP)
        v_hat = nv / (1.0 - ADAM_B2 ** ADAM_STEP)
        d_ref[...] = -ADAM_LR * (m_hat / (jnp.sqrt(v_hat) + ADAM_EPS) + ADAM_WD * w_ref[...])
        nm_ref[...] = nm
        nv_ref[...] = nv

    out = jax.ShapeDtypeStruct((rows, cols), F32)
    res = pl.pallas_call(
        body, name="adamw", grid=(rows // tr,),
        in_specs=[blk] * 4, out_specs=[blk] * 3, out_shape=[out] * 3,
        compiler_params=_cparams(("parallel",)),
    )(w2, g2, m2, v2)
    return tuple(r.reshape(shape) for r in res)


def _place():
    x, y, c = lax.axis_index("x"), lax.axis_index("y"), lax.axis_index("c")
    return x, y, c, [(1 - x, y), (x, 1 - y), (1 - x, 1 - y)]


def _remote(src, dst, send_sems, recv_sems, k, to):
    return pltpu.make_async_remote_copy(src_ref=src, dst_ref=dst, send_sem=send_sems.at[k],
                                        recv_sem=recv_sems.at[k], device_id=to, device_id_type=MESH)


class _Carried:
    def __init__(self, inputs, out_shapes, n_sems, copies):
        self.inputs, self.out_shapes, self.n_sems, self.copies = list(inputs), list(out_shapes), n_sems, copies

    def scratch(self):
        return [pltpu.SemaphoreType.DMA((self.n_sems,)), pltpu.SemaphoreType.DMA((self.n_sems,))]


def _run_comm(name, plan):
    n_in, n_out = len(plan.inputs), len(plan.out_shapes)

    def body(*refs):
        start, finish = plan.copies(refs[:n_in], refs[n_in:n_in + n_out], refs[-2], refs[-1])
        start()
        finish()

    return pl.pallas_call(
        body, name=name, in_specs=[ANY] * n_in, out_specs=[ANY] * n_out, out_shape=plan.out_shapes,
        scratch_shapes=plan.scratch(),
    )(*plan.inputs)


def _half_rows(rows, core):
    if rows % (4 * SUBLANES):
        return None
    return pl.ds(pl.multiple_of(core * (rows // 2), 2 * SUBLANES), rows // 2)


def _all_gather_plan(shards):
    n = len(shards)

    def copies(x_refs, out_refs, send_sems, recv_sems):
        x, y, c, chips = _place()
        sibling = (x, y, 1 - c)
        mine = 2 * x + y
        split = [_half_rows(x_refs[t].shape[0], c) is not None for t in range(n)]

        def src(t):
            return x_refs[t].at[_half_rows(x_refs[t].shape[0], c)] if split[t] else x_refs[t]

        def slot(t, chip_idx, core):
            rows = _half_rows(x_refs[t].shape[0], core)
            return out_refs[t].at[chip_idx, rows] if split[t] else out_refs[t].at[chip_idx]

        def first():
            return [_remote(src(t), slot(t, mine, c), send_sems, recv_sems, 6 * t + j, (cx, cy, c))
                    for j, (cx, cy) in enumerate(chips) for t in range(n)]

        def start():
            for cp in first():
                cp.start()

        def finish():
            passed = []
            for j, (cx, cy) in enumerate(chips):
                for t in range(n):
                    theirs = slot(t, 2 * cx + cy, c)
                    _remote(theirs, theirs, send_sems, recv_sems, 6 * t + j, (cx, cy, c)).wait_recv()
                    if split[t]:
                        fwd = _remote(theirs, theirs, send_sems, recv_sems, 6 * t + 3 + j, sibling)
                        fwd.start()
                        passed.append(fwd)
            for j, (cx, cy) in enumerate(chips):
                for t in range(n):
                    if split[t]:
                        other = slot(t, 2 * cx + cy, 1 - c)
                        _remote(other, other, send_sems, recv_sems, 6 * t + 3 + j, sibling).wait_recv()
            for cp in first() + passed:
                cp.wait_send()

        return start, finish

    return _Carried(shards, [jax.ShapeDtypeStruct((N_CHIPS,) + s.shape, s.dtype) for s in shards], 6 * n, copies)


def _sibling_exchange_plan(grads, small=None):
    n = len(grads)
    extra = [] if small is None else [small]

    def copies(in_refs, out_refs, send_sems, recv_sems):
        x, y, c, _ = _place()
        sibling = (x, y, 1 - c)

        def all_copies():
            cps = [_remote(in_refs[t].at[:, _half_rows(in_refs[t].shape[1], 1 - c), :], out_refs[t],
                           send_sems, recv_sems, t, sibling) for t in range(n)]
            if extra:
                cps.append(_remote(in_refs[n], out_refs[n], send_sems, recv_sems, n, sibling))
            return cps

        def start():
            for cp in all_copies():
                cp.start()

        def finish():
            for cp in all_copies():
                cp.wait()

        return start, finish

    shapes = [jax.ShapeDtypeStruct((g.shape[0], g.shape[1] // 2, g.shape[2]), g.dtype) for g in grads]
    shapes += [jax.ShapeDtypeStruct(s.shape, s.dtype) for s in extra]
    return _Carried(list(grads) + extra, shapes, n + 1, copies)


def _chip_exchange_plan(travel, small=None):
    n = len(travel)
    extra = [] if small is None else [small]

    def copies(in_refs, out_refs, send_sems, recv_sems):
        x, y, c, chips = _place()
        mine = 2 * x + y

        def all_copies():
            cps = []
            for j, (cx, cy) in enumerate(chips):
                to = (cx, cy, c)
                for t in range(n):
                    cps.append(_remote(in_refs[t].at[2 * cx + cy], out_refs[t].at[mine], send_sems, recv_sems,
                                       3 * t + j, to))
                if extra:
                    cps.append(_remote(in_refs[n], out_refs[n].at[mine], send_sems, recv_sems, 3 * n + j, to))
            return cps

        def start():
            for cp in all_copies():
                cp.start()

        def finish():
            for cp in all_copies():
                cp.wait()

        return start, finish

    shapes = [jax.ShapeDtypeStruct(g.shape, g.dtype) for g in travel]
    shapes += [jax.ShapeDtypeStruct((N_CHIPS,) + s.shape, s.dtype) for s in extra]
    return _Carried(list(travel) + extra, shapes, 3 * n + 3, copies)


def _sibling_merge_plan(reduced):
    n = len(reduced)

    def copies(in_refs, out_refs, send_sems, recv_sems):
        x, y, c, _ = _place()

        def all_copies():
            return [_remote(in_refs[t], out_refs[t], send_sems, recv_sems, t, (x, y, 1 - c)) for t in range(n)]

        def start():
            for cp in all_copies():
                cp.start()

        def finish():
            for cp in all_copies():
                cp.wait()

        return start, finish

    return _Carried(reduced, [jax.ShapeDtypeStruct(r.shape, r.dtype) for r in reduced], n, copies)


def _pair_sum(place, grad, land):
    n, r, c = grad.shape
    half = r // 2
    tr = _tile(half, 256, SUBLANES)
    nb = half // tr

    def body(place_ref, a_ref, b_ref, travel_ref, own_ref):
        total = a_ref[0] + b_ref[0]
        travel_ref[0] = total.astype(travel_ref.dtype)

        @pl.when(pl.program_id(1) == place_ref[1])
        def _():
            own_ref[...] = total

    return pl.pallas_call(
        body, name="grad_pair_sum",
        grid_spec=pltpu.PrefetchScalarGridSpec(
            num_scalar_prefetch=1, grid=(nb, n),
            in_specs=[pl.BlockSpec((1, tr, c), lambda i, s, p: (s, p[0] * nb + i, 0)),
                      pl.BlockSpec((1, tr, c), lambda i, s, p: (s, i, 0))],
            out_specs=[pl.BlockSpec((1, tr, c), lambda i, s, p: (s, i, 0)),
                       pl.BlockSpec((tr, c), lambda i, s, p: (i, 0))]),
        out_shape=[jax.ShapeDtypeStruct((n, half, c), BF16), jax.ShapeDtypeStruct((half, c), F32)],
        compiler_params=_cparams(("parallel", "arbitrary")),
    )(place, grad, land)


def _chip_sum(place, own, land, name):
    n, r, c = land.shape
    tr = _tile(r, 256, SUBLANES)

    def body(place_ref, own_ref, land_ref, o_ref):
        mine = place_ref[1]
        acc = jnp.zeros(o_ref.shape, F32)
        for s in range(n):
            acc = acc + jnp.where(mine == s, own_ref[...], land_ref[s].astype(F32))
        o_ref[...] = acc

    return pl.pallas_call(
        body, name=name,
        grid_spec=pltpu.PrefetchScalarGridSpec(
            num_scalar_prefetch=1, grid=(r // tr,),
            in_specs=[pl.BlockSpec((tr, c), lambda i, p: (i, 0)),
                      pl.BlockSpec((n, tr, c), lambda i, p: (0, i, 0))],
            out_specs=pl.BlockSpec((tr, c), lambda i, p: (i, 0))),
        out_shape=jax.ShapeDtypeStruct((r, c), F32),
        compiler_params=_cparams(("parallel",)),
    )(place, own, land)


def _add2(a, b):
    rows = a.shape[0]
    tr = _tile(rows, 256, SUBLANES)
    blk = pl.BlockSpec((tr, a.shape[1]), lambda i: (i, 0))

    def body(a_ref, b_ref, o_ref):
        o_ref[...] = a_ref[...] + b_ref[...]

    return pl.pallas_call(
        body, name="grad_small_pair_sum", grid=(rows // tr,), in_specs=[blk, blk], out_specs=blk,
        out_shape=jax.ShapeDtypeStruct(a.shape, F32), compiler_params=_cparams(("parallel",)),
    )(a, b)


def _merge_halves(place, mine, other):
    first_core = place[0] == 0
    return jnp.concatenate([jnp.where(first_core, mine, other), jnp.where(first_core, other, mine)], axis=0)


_BIG = (("w_in", 2), ("w_pa", 1), ("w_pb", 1), ("w_o", 1), ("w_ffn_gate", 2), ("w_ffn_up", 2),
        ("w_ffn_down", 1))
_SMALL = ("conv_w", "a_log", "dt_bias", "o_norm_w", "sgu_ln_g", "sgu_ln_b", "w_s", "b_s",
          "ln1_g", "ln1_b", "ln2_g", "ln2_b")


def _pack_small(arrays):
    pieces = []
    for a in arrays:
        if a.shape[-1] % LANES == 0:
            a2 = a.reshape(-1, LANES)
        else:
            a2 = jnp.pad(a.reshape(-1, a.shape[-1]), ((0, 0), (0, LANES - a.shape[-1])))
        pieces.append(jnp.pad(a2, ((0, -a2.shape[0] % SUBLANES), (0, 0))))
    return jnp.concatenate(pieces, axis=0)


def _unpack_small(buf, like):
    out, off = [], 0
    for a in like:
        if a.shape[-1] % LANES == 0:
            rows = a.size // LANES
            out.append(buf[off:off + rows].reshape(a.shape))
        else:
            rows = a.size // a.shape[-1]
            out.append(buf[off:off + rows, :a.shape[-1]].reshape(a.shape))
        off += -(-rows // SUBLANES) * SUBLANES
    return out


def _unshard(gathered, local, chip, axis):
    parts = [jnp.where(chip == s, local, gathered[s]) for s in range(N_CHIPS)]
    return jnp.concatenate(parts, axis=axis - 1)


def _to_shards(full, axis):
    l, r, c = full.shape
    if axis == 1:
        return full.reshape(l, N_CHIPS, r // N_CHIPS, c)
    return jnp.transpose(full.reshape(l, r, N_CHIPS, c // N_CHIPS), (0, 2, 1, 3))


def _row(v, width=None):
    v = v.reshape(1, -1).astype(F32)
    if width is not None and v.shape[1] < width:
        v = jnp.pad(v, ((0, 0), (0, width - v.shape[1])))
    return v


def _layer_consts(p, l, d):
    heads = d // DN_DK
    return dict(
        alog=_row(p["a_log"][l], LANES), dtb=_row(p["dt_bias"][l], LANES),
        onw=_row(jnp.tile(p["o_norm_w"][l], heads)),
        lng=_row(p["sgu_ln_g"][l]), lnb=_row(p["sgu_ln_b"][l]),
        ws=p["w_s"][l].astype(F32),
        bst=jnp.pad(p["b_s"][l].T, ((0, 0), (0, LANES - p["b_s"].shape[1]))),
        g1=_row(p["ln1_g"][l]), b1=_row(p["ln1_b"][l]), g2=_row(p["ln2_g"][l]), b2=_row(p["ln2_b"][l]))


class _NoComm:
    def with_proj_main(self):
        return None

    def after_proj_main(self, got):
        pass

    def weights(self, full):
        return full

    def with_dn_fwd(self):
        return None

    def after_dn_fwd(self, got):
        pass

    def with_ffn_in_dw(self):
        return None

    def after_ffn_in_dw(self, got):
        pass

    def after_branch_grads(self, g):
        pass

    def with_dn_bwd(self):
        return None

    def after_dn_bwd(self, got):
        pass

    def with_proj_main_dw(self):
        return None

    def after_proj_main_dw(self, got):
        pass

    def with_ffn_in(self):
        return None

    def after_ffn_in(self, got):
        pass

    def after_all_grads(self, g):
        pass

    def with_proj_main_dx(self):
        return None

    def after_proj_main_dx(self, got):
        pass


def _carry(carried, after, call, *args, **kw):
    if carried is None:
        return call(*args, **kw)
    out, got = call(*args, carried=carried, **kw)
    after(got)
    return out


def _in_proj_weights(w_in, d):
    heads, q4 = d // DN_DK, 4 * d
    wba = jnp.zeros((d, 2 * LANES), w_in.dtype)
    wba = wba.at[:, :heads].set(w_in[:, q4:q4 + heads])
    wba = wba.at[:, LANES:LANES + heads].set(w_in[:, q4 + heads:q4 + 2 * heads])
    return jnp.concatenate([w_in[:, :q4], w_in[:, q4 + 2 * heads:]], axis=1), wba


def _layer_fwd(x, xb, full, cl, d, tb, comm):
    wm, wba = _in_proj_weights(full["w_in"], d)
    projm = _carry(comm.with_proj_main(), comm.after_proj_main, _matmul, xb, wm, NN, "proj_main", tn=MM_WIDE)
    full = comm.weights(full)
    wl = dict(wm=wm, wba=wba, conv=full["conv_w"], wpa=full["w_pa"], wpb=full["w_pb"], wo=full["w_o"],
              wgu=jnp.concatenate([full["w_ffn_gate"], full["w_ffn_up"]], axis=1), wd=full["w_ffn_down"])
    ba = _matmul(xb, wba, NN, "proj_gates")
    qkv = _conv_fwd(projm, wl["conv"], d, _tile(x.shape[0], 2 * tb, SUBLANES))
    (o, states, ycors), got = _dn_fwd(qkv, ba, cl["alog"], cl["dtb"], d, comm.with_dn_fwd())
    comm.after_dn_fwd(got)
    ya, yb = _gate_sgu_fwd(o, projm, cl["onw"], cl["lng"], cl["lnb"], cl["ws"], cl["bst"], d)
    pa, pb, m, h1, x1, x1b = _mix_fwd(ya, yb, projm, x, wl["wpa"], wl["wpb"], wl["wo"], cl["g1"], cl["b1"], d, tb)
    gu = _carry(comm.with_ffn_in(), comm.after_ffn_in, _matmul, x1b, wl["wgu"], NN, "ffn_in")
    act, h2, x2, x2b = _ffn_tail_fwd(gu, wl["wd"], x1, cl["g2"], cl["b2"], tb)
    saved = dict(xb=xb, projm=projm, ba=ba, qkv=qkv, o=o, states=states, ycors=ycors, ya=ya, yb=yb,
                 pa=pa, pb=pb, m=m, h1=h1, x1b=x1b, gu=gu, act=act, h2=h2)
    return x2, x2b, saved, wl


def _layer_bwd(dx2, sv, wl, cl, d, tb, comm, ln2_bwd=None):
    g = {}
    dh2, dh2b, dg2, db2 = ln2_bwd or _ln_bwd_call(dx2, sv["h2"], cl["g2"], tb)
    g["ln2_g"], g["ln2_b"] = dg2.sum(0), db2.sum(0)
    g["wd"] = _matmul(sv["act"], dh2b, TN, "ffn_out_dw")
    dgu = _ffn_tail_bwd(dh2b, wl["wd"], sv["gu"], tb)
    g["wgu"] = _carry(comm.with_ffn_in_dw(), comm.after_ffn_in_dw, _matmul, sv["x1b"], dgu, TN, "ffn_in_dw")
    dh1, dh1b, dg1, db1 = _ffn_head_bwd(dgu, wl["wgu"], dh2, sv["h1"], cl["g1"], tb)
    g["ln1_g"], g["ln1_b"] = dg1.sum(0), db1.sum(0)
    g["wo"] = _matmul(sv["m"], dh1b, TN, "wo_dw")
    dpa, dpb, dya, dyb, dprojm = _mix_bwd(dh1b, sv["pa"], sv["pb"], sv["projm"], wl["wpa"], wl["wpb"], wl["wo"], d, tb)
    g["wpa"] = _matmul(sv["ya"], dpa, TN, "wpa_dw")
    g["wpb"] = _matmul(sv["yb"], dpb, TN, "wpb_dw")
    comm.after_branch_grads(g)
    do, dprojm, donw, dlng, dlnb, dws, dbst = _gate_sgu_bwd(
        dya, dyb, sv["o"], sv["projm"], cl["onw"], cl["lng"], cl["lnb"], cl["ws"], cl["bst"], dprojm, d)
    heads, groups = d // DN_DK, d // SGU_GROUP_DIM
    g["o_norm_w"], g["sgu_ln_g"], g["sgu_ln_b"] = donw.sum(0), dlng.sum(0), dlnb.sum(0)
    g["w_s"], g["b_s"] = dws, dbst[:, :groups].T
    (dqkv, dba, dal, ddt), got = _dn_bwd(sv["qkv"], sv["ba"], cl["alog"], cl["dtb"], do, sv["states"],
                                         sv["ycors"], d, comm.with_dn_bwd())
    comm.after_dn_bwd(got)
    g["a_log"], g["dt_bias"] = dal.sum(0)[:heads], ddt.sum(0)[:heads]
    tbc = _tile(sv["xb"].shape[0], 2 * tb, SUBLANES)
    dy, dcw = _conv_bwd_dy(sv["projm"], wl["conv"], dqkv, d, tbc)
    g["conv_w"] = dcw.sum(1)
    dprojm = _conv_bwd_dx(dy, wl["conv"], dprojm, d, tbc)
    g["wm"] = _carry(comm.with_proj_main_dw(), comm.after_proj_main_dw, _matmul, sv["xb"], dprojm, TN,
                     "proj_main_dw", tn=MM_WIDE)
    g["wba"] = _matmul(sv["xb"], dba, TN, "proj_gates_dw")
    dx = _matmul(dba, wl["wba"], NT, "proj_gates_dx", add=dh1, coef=ALPHA)
    comm.after_all_grads(g)
    dx = _carry(comm.with_proj_main_dx(), comm.after_proj_main_dx, _matmul, dprojm, wl["wm"], NT, "proj_main_dx",
                add=dx, tk=MM_WIDE)
    return dx, g


_BRANCH = ("w_pa", "w_pb", "w_o", "w_ffn_gate", "w_ffn_up", "w_ffn_down")


def _grad_shards(g, d, keys):
    heads, q4 = d // DN_DK, 4 * d
    rows = lambda a: a.reshape(N_CHIPS, -1, a.shape[1])
    out = {}
    if "w_in" in keys:
        gm, gba, wsh = g["wm"], g["wba"], 2 * d + heads // 2
        out["w_in"] = jnp.stack([gm[:, :wsh],
                                 jnp.concatenate([gm[:, wsh:q4], gba[:, :heads]], axis=1),
                                 jnp.concatenate([gba[:, LANES:LANES + heads], gm[:, q4:q4 + wsh - heads]], axis=1),
                                 gm[:, q4 + wsh - heads:]])
    if "w_pa" in keys:
        ggu = g["wgu"]
        f = ggu.shape[1] // 2
        fs = f // N_CHIPS
        out.update({
            "w_pa": rows(g["wpa"]), "w_pb": rows(g["wpb"]), "w_o": rows(g["wo"]), "w_ffn_down": rows(g["wd"]),
            "w_ffn_gate": jnp.stack([ggu[:, s * fs:(s + 1) * fs] for s in range(N_CHIPS)]),
            "w_ffn_up": jnp.stack([ggu[:, f + s * fs:f + (s + 1) * fs] for s in range(N_CHIPS)])})
    return out


def _local_step(x, target, full0, full1_of, small_w, comm0=None):
    t, d = x.shape
    tb = _tile(t, 256, SUBLANES)
    comm0 = comm0 or _NoComm()
    consts = [_layer_consts(small_w, l, d) for l in range(DEPTH)]
    x1, x1b, sv0, w0 = _layer_fwd(x, x.astype(ACT), full0, consts[0], d, tb, comm0)
    x2, _, sv1, w1 = _layer_fwd(x1, x1b, full1_of(), consts[1], d, tb, _NoComm())
    ln2_bwd, loss_parts = _loss_ln_bwd(x2, target, sv1["h2"], consts[1]["g2"], tb)
    dy, g1 = _layer_bwd(None, sv1, w1, consts[1], d, tb, _NoComm(), ln2_bwd)
    comm0.layer1_grads = g1
    dy, g0 = _layer_bwd(dy, sv0, w0, consts[0], d, tb, comm0)
    return loss_parts, dy, [g0, g1]


def kernel(x, w_in, conv_w, a_log, dt_bias, o_norm_w, sgu_ln_g, sgu_ln_b, w_s, b_s, w_pa, w_pb, w_o, ln1_g, ln1_b, w_ffn_gate, w_ffn_up, w_ffn_down, ln2_g, ln2_b, loss_target, m_w_in, m_conv_w, m_a_log, m_dt_bias, m_o_norm_w, m_sgu_ln_g, m_sgu_ln_b, m_w_s, m_b_s, m_w_pa, m_w_pb, m_w_o, m_ln1_g, m_ln1_b, m_w_ffn_gate, m_w_ffn_up, m_w_ffn_down, m_ln2_g, m_ln2_b, v_w_in, v_conv_w, v_a_log, v_dt_bias, v_o_norm_w, v_sgu_ln_g, v_sgu_ln_b, v_w_s, v_b_s, v_w_pa, v_w_pb, v_w_o, v_ln1_g, v_ln1_b, v_w_ffn_gate, v_w_ffn_up, v_w_ffn_down, v_ln2_g, v_ln2_b):
    names = ("w_in", "conv_w", "a_log", "dt_bias", "o_norm_w", "sgu_ln_g", "sgu_ln_b", "w_s", "b_s", "w_pa",
             "w_pb", "w_o", "ln1_g", "ln1_b", "w_ffn_gate", "w_ffn_up", "w_ffn_down", "ln2_g", "ln2_b")
    w = dict(zip(names, (w_in, conv_w, a_log, dt_bias, o_norm_w, sgu_ln_g, sgu_ln_b, w_s, b_s, w_pa, w_pb, w_o,
                         ln1_g, ln1_b, w_ffn_gate, w_ffn_up, w_ffn_down, ln2_g, ln2_b)))
    mom = dict(zip(names, (m_w_in, m_conv_w, m_a_log, m_dt_bias, m_o_norm_w, m_sgu_ln_g, m_sgu_ln_b, m_w_s, m_b_s,
                           m_w_pa, m_w_pb, m_w_o, m_ln1_g, m_ln1_b, m_w_ffn_gate, m_w_ffn_up, m_w_ffn_down,
                           m_ln2_g, m_ln2_b)))
    var = dict(zip(names, (v_w_in, v_conv_w, v_a_log, v_dt_bias, v_o_norm_w, v_sgu_ln_g, v_sgu_ln_b, v_w_s, v_b_s,
                           v_w_pa, v_w_pb, v_w_o, v_ln1_g, v_ln1_b, v_w_ffn_gate, v_w_ffn_up, v_w_ffn_down,
                           v_ln2_g, v_ln2_b)))
    chip = 2 * lax.axis_index("x") + lax.axis_index("y")
    place = jnp.stack([lax.axis_index("c"), chip]).astype(jnp.int32)

    big = [k for k, _ in _BIG]
    axis_of = dict(_BIG)
    local = {k: w[k].astype(BF16) for k in big}
    local["conv_w"] = conv_w

    def gather_plan(l, keys):
        return _all_gather_plan([local[k][l] for k in keys])

    def full_of(l, keys, gathered):
        return {k: _unshard(gt, local[k][l], chip, axis_of.get(k, 2)) for k, gt in zip(keys, gathered)}

    def pair_sums(grads_l, keys, lands):
        return [_pair_sum(place, grads_l[k], land) for k, land in zip(keys, lands)]

    def chip_sums(pairs, lands):
        return [_chip_sum(place, p[1], land, "grad_chip_sum") for p, land in zip(pairs, lands)]

    class Layer0Comm(_NoComm):
        def with_proj_main(self):
            return gather_plan(0, _BRANCH)

        def after_proj_main(self, got):
            self.rest = full_of(0, _BRANCH, got)

        def weights(self, full):
            return {**full, **self.rest}

        def with_dn_fwd(self):
            return gather_plan(1, mixer)

        def after_dn_fwd(self, got):
            self.full1 = full_of(1, mixer, got)

        def with_ffn_in(self):
            return gather_plan(1, ffn)

        def after_ffn_in(self, got):
            self.full1.update(full_of(1, ffn, got))

        def with_ffn_in_dw(self):
            self.g1 = _grad_shards(self.layer1_grads, x.shape[-1], big)
            return _sibling_exchange_plan([self.g1[k] for k in big])

        def after_ffn_in_dw(self, got):
            self.pairs1 = pair_sums(self.g1, big, got)

        def with_dn_bwd(self):
            return _chip_exchange_plan([p[0] for p in self.pairs1])

        def after_dn_bwd(self, got):
            self.red1 = chip_sums(self.pairs1, got)

        def after_branch_grads(self, g0):
            shards = _grad_shards(g0, x.shape[-1], _BRANCH)
            lands = _run_comm("grad_sibling_exchange", _sibling_exchange_plan([shards[k] for k in _BRANCH]))
            self.pairs0 = pair_sums(shards, _BRANCH, lands)

        def with_proj_main_dw(self):
            return _chip_exchange_plan([p[0] for p in self.pairs0])

        def after_proj_main_dw(self, got):
            self.red0 = chip_sums(self.pairs0, got)

        def after_all_grads(self, g0):
            g_in = _grad_shards(g0, x.shape[-1], ["w_in"])["w_in"]
            self.small_g = {k: jnp.stack([g0[k], self.layer1_grads[k]]) for k in _SMALL}
            small = _pack_small([self.small_g[k] for k in _SMALL])
            land, sland = _run_comm("grad_sibling_exchange_last", _sibling_exchange_plan([g_in], small))
            self.pair_in = _pair_sum(place, g_in, land)
            self.small_chip = _add2(small, sland)

        def with_proj_main_dx(self):
            return _chip_exchange_plan([self.pair_in[0]], self.small_chip)

        def after_proj_main_dx(self, got):
            self.red_in = _chip_sum(place, self.pair_in[1], got[0], "grad_chip_sum")
            self.small_total = _chip_sum(place, self.small_chip, got[1], "grad_small_chip_sum")

    comm = Layer0Comm()
    first, mixer, ffn = ["w_in", "conv_w"], ["w_in", "conv_w", "w_pa", "w_pb", "w_o"], list(_BRANCH[3:])
    full0 = full_of(0, first, _run_comm("all_gather_weights", gather_plan(0, first)))
    small_w = {k: w[k] for k in _SMALL if k != "conv_w"}
    loss_parts, grad_x, g = _local_step(x[0], loss_target[0], full0, lambda: comm.full1, small_w, comm)

    reduced = [comm.red_in] + comm.red0 + comm.red1
    others = _run_comm("grad_sibling_merge", _sibling_merge_plan(reduced))
    halves = [_merge_halves(place, mine, other) for mine, other in zip(reduced, others)]
    grads = {k: jnp.stack([halves[i], halves[len(big) + i]]) for i, k in enumerate(big)}
    grads.update(zip(_SMALL, _unpack_small(comm.small_total, [comm.small_g[k] for k in _SMALL])))
    grads["conv_w"] = lax.dynamic_index_in_dim(_to_shards(grads["conv_w"], 2), chip, 1, keepdims=False)

    delta, new_m, new_v = {}, {}, {}
    for k in [k for k, _ in _BIG] + ["conv_w"]:
        delta[k], new_m[k], new_v[k] = _adamw(w[k], grads[k], mom[k], var[k])
    rep = [k for k in _SMALL if k != "conv_w"]
    pack = lambda dct: _pack_small([dct[k] for k in rep])
    packed = _adamw(pack(w), pack(grads), pack(mom), pack(var))
    for dst, src in zip((delta, new_m, new_v), packed):
        dst.update(zip(rep, _unpack_small(src, [w[k] for k in rep])))

    loss = 0.5 * lax.psum(jnp.sum(loss_parts), ("x", "y", "c")) / x.shape[-1]
    return (loss, grad_x[None], *[grads[k] for k in names], *[delta[k] for k in names],
            *[new_m[k] for k in names], *[new_v[k] for k in names])
```

```python
import math

import jax
import jax.numpy as jnp
from jax import lax
from jax.experimental import pallas as pl
from jax.experimental.pallas import tpu as pltpu

F32 = jnp.float32
BF16 = jnp.bfloat16
MXU_DTYPE = jnp.bfloat16
ACT = jnp.bfloat16
HIGHEST = lax.Precision.HIGHEST

DEPTH = 2
CHUNK = 64
DN_GROUP = 2
DN_GROUP_FWD = 4
SGU_BLOCK = 128
CONV_K = 4
DN_DK = 128
SGU_GROUP_DIM = 128
LN_EPS = 1e-5
RMS_EPS = 1e-6
ALPHA = (2 * DEPTH) ** 0.25
ADAM_LR, ADAM_B1, ADAM_B2, ADAM_EPS, ADAM_WD, ADAM_STEP = 0.001, 0.9, 0.999, 1e-08, 0.01, 10

LANES = 128
SUBLANES = 8
VMEM_LIMIT = 52 * 2 ** 20
N_CHIPS = 4

NN = ((1,), (0,))
NT = ((1,), (1,))
TN = ((0,), (0,))
MESH = pl.DeviceIdType.MESH
ANY = pl.BlockSpec(memory_space=pl.ANY)


def _dot(a, b, dims=NN, prec=None):
    if prec is None:
        a = a.astype(MXU_DTYPE)
        b = b.astype(MXU_DTYPE)
    return lax.dot_general(a, b, (dims, ((), ())), preferred_element_type=F32, precision=prec)


def _cparams(sem=None):
    return pltpu.CompilerParams(dimension_semantics=sem, vmem_limit_bytes=VMEM_LIMIT)


def _tile(dim, pref, unit=LANES):
    t = (min(pref, dim) // unit) * unit
    while t >= unit:
        if dim % t == 0:
            return t
        t -= unit
    return dim


def _fold8(x):
    r, n = x.shape
    return x.reshape(r // SUBLANES, SUBLANES, n).sum(axis=0)


def _sigmoid(x):
    return 1.0 / (1.0 + jnp.exp(-x))


def _gelu(x):
    return 0.5 * x * (1.0 + lax.erf(x * (2.0 ** -0.5)))


def _gelu_grad(x):
    return 0.5 * (1.0 + lax.erf(x * (2.0 ** -0.5))) + x * jnp.exp(-0.5 * x * x) * (2.0 * math.pi) ** -0.5


def _ln_hat(h):
    mu = jnp.mean(h, axis=-1, keepdims=True)
    xc = h - mu
    var = jnp.mean(xc * xc, axis=-1, keepdims=True)
    r = lax.rsqrt(var + LN_EPS)
    return xc * r, r


def _ln_bwd(dxhat, xhat, r):
    return r * (dxhat - jnp.mean(dxhat, axis=-1, keepdims=True)
                - xhat * jnp.mean(dxhat * xhat, axis=-1, keepdims=True))


MM_TILE = 1536
MM_WIDE = 2048


def _matmul(a, b, dims, name, out_dtype=F32, add=None, coef=1.0, tm=MM_TILE, tn=MM_TILE, tk=MM_TILE, carried=None,
            ln=None):
    if dims == NN:
        (m, k), n = a.shape, b.shape[1]
    elif dims == NT:
        (m, k), n = a.shape, b.shape[0]
    else:
        (k, m), n = a.shape, b.shape[1]
    tm, tn, tk = _tile(m, tm), _tile(n, tn), _tile(k, tk)
    nk = k // tk
    a_spec = pl.BlockSpec((tk, tm), lambda j, i, q: (q, i)) if dims == TN else pl.BlockSpec((tm, tk), lambda j, i, q: (i, q))
    b_spec = pl.BlockSpec((tn, tk), lambda j, i, q: (j, q)) if dims == NT else pl.BlockSpec((tk, tn), lambda j, i, q: (q, j))
    o_spec = pl.BlockSpec((tm, tn), lambda j, i, q: (i, j))
    has_add = add is not None
    if ln is not None:
        assert n == tn and has_add and carried is None
        return _matmul_ln_bwd(a, b, dims, name, add, coef, ln, a_spec, b_spec, o_spec, (m, n, tm, tn, nk))

    def body(*refs):
        a_ref, b_ref = refs[0], refs[1]
        add_ref = refs[2] if has_add else None
        o_ref, acc_ref = refs[2 + has_add], refs[3 + has_add]
        q = pl.program_id(2)
        part = _dot(a_ref[...], b_ref[...], dims)

        def finish(r):
            if has_add:
                r = r + coef * add_ref[...]
            o_ref[...] = r.astype(out_dtype)

        if nk == 1:
            finish(part)
        else:
            @pl.when(q == 0)
            def _():
                acc_ref[...] = part

            @pl.when(q > 0)
            def _():
                acc_ref[...] += part

            @pl.when(q == nk - 1)
            def _():
                finish(acc_ref[...])

    ins = [a, b] + ([add] if has_add else [])
    in_specs = [a_spec, b_spec] + ([o_spec] if has_add else [])
    grid = (n // tn, m // tm, nk)
    acc = pltpu.VMEM((tm, tn) if nk > 1 else (SUBLANES, LANES), F32)
    out = jax.ShapeDtypeStruct((m, n), out_dtype)
    if carried is None:
        return pl.pallas_call(
            body, name=name, grid=grid, in_specs=in_specs, out_specs=o_spec, out_shape=out, scratch_shapes=[acc],
            compiler_params=_cparams(("parallel", "parallel", "arbitrary")),
        )(*ins)
    res = pl.pallas_call(
        _carrying(body, len(ins), 1, 1, carried, grid), name=name + "_carrying", grid=grid,
        in_specs=in_specs + [ANY] * len(carried.inputs), out_specs=[o_spec] + [ANY] * len(carried.out_shapes),
        out_shape=[out] + carried.out_shapes, scratch_shapes=[acc] + carried.scratch(),
        compiler_params=_cparams(("arbitrary", "arbitrary", "arbitrary")),
    )(*ins, *carried.inputs)
    return res[0], res[1:]


def _matmul_ln_bwd(a, b, dims, name, add, coef, ln, a_spec, b_spec, o_spec, sizes):
    m, n, tm, tn, nk = sizes
    hres, g = ln
    row = pl.BlockSpec((1, n), lambda j, i, q: (0, 0))
    sums = pl.BlockSpec((SUBLANES, n), lambda j, i, q: (0, 0))

    def body(a_ref, b_ref, add_ref, h_ref, g_ref, dh_ref, dhb_ref, dg_ref, db_ref, acc_ref):
        i, q = pl.program_id(1), pl.program_id(2)
        part = _dot(a_ref[...], b_ref[...], dims)

        @pl.when(jnp.logical_and(i == 0, q == 0))
        def _():
            dg_ref[...] = jnp.zeros_like(dg_ref)
            db_ref[...] = jnp.zeros_like(db_ref)

        @pl.when(q == 0)
        def _():
            acc_ref[...] = part

        @pl.when(q > 0)
        def _():
            acc_ref[...] += part

        @pl.when(q == nk - 1)
        def _():
            dy_v = acc_ref[...] + coef * add_ref[...]
            xhat, r = _ln_hat(h_ref[...])
            dh = _ln_bwd(dy_v * g_ref[...], xhat, r)
            dh_ref[...] = dh
            dhb_ref[...] = dh.astype(dhb_ref.dtype)
            dg_ref[...] += _fold8(dy_v * xhat)
            db_ref[...] += _fold8(dy_v)

    return pl.pallas_call(
        body, name=name + "_ln_bwd", grid=(1, m // tm, nk),
        in_specs=[a_spec, b_spec, o_spec, o_spec, row], out_specs=[o_spec, o_spec, sums, sums],
        out_shape=[jax.ShapeDtypeStruct((m, n), F32), jax.ShapeDtypeStruct((m, n), ACT),
                   jax.ShapeDtypeStruct((SUBLANES, n), F32), jax.ShapeDtypeStruct((SUBLANES, n), F32)],
        scratch_shapes=[pltpu.VMEM((tm, tn), F32)],
        compiler_params=_cparams(("arbitrary", "arbitrary", "arbitrary")),
    )(a, b, add, hres, g)


def _conv_taps(cur_ref, halo_ref, first):
    x = cur_ref[...]
    tb = x.shape[0]
    halo = jnp.where(first, 0.0, halo_ref[...])
    xc = jnp.concatenate([halo, x], axis=0)
    return [x] + [pltpu.roll(xc, s, 0)[SUBLANES:SUBLANES + tb] for s in range(1, CONV_K)]


def _conv_fwd(projm, conv_w, d, tb):
    t = projm.shape[0]
    heads = d // DN_DK
    hb = tb // SUBLANES

    def body(cur_ref, halo_ref, w_ref, o_ref):
        i, j = pl.program_id(0), pl.program_id(1)
        taps = _conv_taps(cur_ref, halo_ref, i == 0)
        y = taps[0] * w_ref[CONV_K - 1:CONV_K, :]
        for s in range(1, CONV_K):
            y = y + taps[s] * w_ref[CONV_K - 1 - s:CONV_K - s, :]
        act = y * _sigmoid(y)
        scale = jnp.where(j == 0, DN_DK ** -0.5, 1.0)
        for h in range(heads):
            seg = act[:, h * DN_DK:(h + 1) * DN_DK]
            r = lax.rsqrt(jnp.sum(seg * seg, axis=1, keepdims=True) + RMS_EPS) * scale
            o_ref[:, h * DN_DK:(h + 1) * DN_DK] = seg * jnp.where(j < 2, r, 1.0)

    blk = pl.BlockSpec((tb, d), lambda i, j: (i, j))
    return pl.pallas_call(
        body, name="conv_fwd", grid=(t // tb, 3),
        in_specs=[blk,
                  pl.BlockSpec((SUBLANES, d), lambda i, j: (jnp.maximum(i * hb - 1, 0), j)),
                  pl.BlockSpec((CONV_K, d), lambda i, j: (0, j))],
        out_specs=blk,
        out_shape=jax.ShapeDtypeStruct((t, 3 * d), F32),
        compiler_params=_cparams(("parallel", "parallel")),
    )(projm, projm, conv_w)


def _conv_bwd_dy(projm, conv_w, dqkv, d, tb):
    t = projm.shape[0]
    heads = d // DN_DK
    hb = tb // SUBLANES

    def body(cur_ref, halo_ref, w_ref, dout_ref, dy_ref, dw_ref):
        j, i = pl.program_id(0), pl.program_id(1)
        taps = _conv_taps(cur_ref, halo_ref, i == 0)
        y = taps[0] * w_ref[CONV_K - 1:CONV_K, :]
        for s in range(1, CONV_K):
            y = y + taps[s] * w_ref[CONV_K - 1 - s:CONV_K - s, :]
        sg = _sigmoid(y)
        act = y * sg
        dact = sg * (1.0 + y * (1.0 - sg))
        scale = jnp.where(j == 0, DN_DK ** -0.5, 1.0)
        for h in range(heads):
            cols = slice(h * DN_DK, (h + 1) * DN_DK)
            seg = act[:, cols]
            r = lax.rsqrt(jnp.sum(seg * seg, axis=1, keepdims=True) + RMS_EPS)
            nrm = seg * r
            dout = dout_ref[:, cols]
            ds = jnp.where(j < 2, (r * scale) * (dout - nrm * jnp.sum(dout * nrm, axis=1, keepdims=True)), dout)
            dy_ref[:, cols] = ds * dact[:, cols]
        dy = dy_ref[...]

        @pl.when(i == 0)
        def _():
            dw_ref[...] = jnp.zeros_like(dw_ref)

        for s in range(CONV_K):
            dw_ref[CONV_K - 1 - s] += _fold8(dy * taps[s])

    return pl.pallas_call(
        body, name="conv_bwd_dy", grid=(3, t // tb),
        in_specs=[pl.BlockSpec((tb, d), lambda j, i: (i, j)),
                  pl.BlockSpec((SUBLANES, d), lambda j, i: (jnp.maximum(i * hb - 1, 0), j)),
                  pl.BlockSpec((CONV_K, d), lambda j, i: (0, j)),
                  pl.BlockSpec((tb, d), lambda j, i: (i, j))],
        out_specs=[pl.BlockSpec((tb, d), lambda j, i: (i, j)),
                   pl.BlockSpec((CONV_K, SUBLANES, d), lambda j, i: (0, 0, j))],
        out_shape=[jax.ShapeDtypeStruct((t, 3 * d), F32),
                   jax.ShapeDtypeStruct((CONV_K, SUBLANES, 3 * d), F32)],
        compiler_params=_cparams(("parallel", "arbitrary")),
    )(projm, projm, conv_w, dqkv)


def _conv_bwd_dx(dy, conv_w, dprojm, d, tb):
    t = dy.shape[0]
    hb = tb // SUBLANES
    last = t // tb - 1

    def body(cur_ref, halo_ref, w_ref, alias_ref, o_ref):
        i = pl.program_id(0)
        cur = cur_ref[...]
        halo = jnp.where(i == last, 0.0, halo_ref[...])
        dc = jnp.concatenate([cur, halo], axis=0)
        acc = cur * w_ref[CONV_K - 1:CONV_K, :]
        for s in range(1, CONV_K):
            acc = acc + pltpu.roll(dc, tb + SUBLANES - s, 0)[:tb] * w_ref[CONV_K - 1 - s:CONV_K - s, :]
        o_ref[...] = acc.astype(o_ref.dtype)

    return pl.pallas_call(
        body, name="conv_bwd_dx", grid=(t // tb, 3),
        in_specs=[pl.BlockSpec((tb, d), lambda i, j: (i, j)),
                  pl.BlockSpec((SUBLANES, d), lambda i, j: (jnp.minimum((i + 1) * hb, t // SUBLANES - 1), j)),
                  pl.BlockSpec((CONV_K, d), lambda i, j: (0, j)),
                  ANY],
        out_specs=pl.BlockSpec((tb, d), lambda i, j: (i, j)),
        out_shape=jax.ShapeDtypeStruct(dprojm.shape, dprojm.dtype),
        input_output_aliases={3: 0},
        compiler_params=_cparams(("parallel", "parallel")),
    )(dy, dy, conv_w, dprojm)


def _beta_g(ba, alog, dtb):
    beta = _sigmoid(ba[:, :LANES])
    xa = ba[:, LANES:] + dtb
    softplus = jnp.maximum(xa, 0.0) + jnp.log(1.0 + jnp.exp(-jnp.abs(xa)))
    ea = jnp.exp(alog)
    return beta, -ea * softplus, ea, _sigmoid(xa)


def _inv_corrections(mats):
    ys = [-a for a in mats]
    ps = [_dot(a, a) for a in mats]
    steps = int(math.log2(CHUNK)) - 1
    for it in range(steps):
        ys = [y + p + _dot(y, p) for y, p in zip(ys, ps)]
        if it < steps - 1:
            ps = [_dot(p, p) for p in ps]
    return ys


def _chunk_masks():
    row = lax.broadcasted_iota(jnp.int32, (CHUNK, CHUNK), 0)
    col = lax.broadcasted_iota(jnp.int32, (CHUNK, CHUNK), 1)
    return row >= col, row > col, row <= col


def _col_of(mat, lane_idx, h):
    return jnp.sum(jnp.where(lane_idx == h, mat, 0.0), axis=1, keepdims=True)


def _row_of(mat, sub_idx, h):
    return jnp.sum(jnp.where(sub_idx == h, mat, 0.0), axis=0, keepdims=True)


def _carrying(compute, n_in, n_out, n_scratch, carried, grid):
    if carried is None:
        return compute
    ci, co = len(carried.inputs), len(carried.out_shapes)

    def body(*refs):
        ins, c_in = refs[:n_in], refs[n_in:n_in + ci]
        outs, c_out = refs[n_in + ci:n_in + ci + n_out], refs[n_in + ci + n_out:n_in + ci + n_out + co]
        scratch = refs[n_in + ci + n_out + co:]
        start, finish = carried.copies(c_in, c_out, scratch[n_scratch], scratch[n_scratch + 1])
        first, last = True, True
        for axis, steps in enumerate(grid):
            first = jnp.logical_and(first, pl.program_id(axis) == 0)
            last = jnp.logical_and(last, pl.program_id(axis) == steps - 1)

        @pl.when(first)
        def _():
            start()

        compute(*ins, *outs, *scratch[:n_scratch])

        @pl.when(last)
        def _():
            finish()

    return body


def _dn_fwd(qkv, ba, alog, dtb, d, carried=None):
    t = qkv.shape[0]
    heads = d // DN_DK
    n_chunks = t // CHUNK
    grp = DN_GROUP_FWD if n_chunks % DN_GROUP_FWD == 0 else 1
    span = grp * CHUNK
    extra = carried or _Carried([], [], 0, None)

    def compute(qkv_ref, ba_ref, al_ref, dt_ref, o_ref, s_ref, y_ref, state):
        @pl.when(pl.program_id(0) == 0)
        def _():
            state[...] = jnp.zeros_like(state)

        tril, strict, _ = _chunk_masks()
        beta, g, _, _ = _beta_g(ba_ref[...], al_ref[...], dt_ref[...])
        lane = lax.broadcasted_iota(jnp.int32, (CHUNK, LANES), 1)
        sub = lax.broadcasted_iota(jnp.int32, (LANES, CHUNK), 0)
        rowc = lax.broadcasted_iota(jnp.int32, (CHUNK, 1), 0)
        hs = range(heads)
        units = [(c, h) for c in range(grp) for h in hs]
        un = range(len(units))
        rows = lambda c: slice(c * CHUNK, (c + 1) * CHUNK)
        gc = [_dot(jnp.where(tril, 1.0, 0.0), g[rows(c)], NN, HIGHEST) for c in range(grp)]
        gct = [m.T for m in gc]
        q = [qkv_ref[rows(c), h * DN_DK:(h + 1) * DN_DK] for c, h in units]
        k = [qkv_ref[rows(c), d + h * DN_DK:d + (h + 1) * DN_DK] for c, h in units]
        v = [qkv_ref[rows(c), 2 * d + h * DN_DK:2 * d + (h + 1) * DN_DK] for c, h in units]
        gch = [_col_of(gc[c], lane, h) for c, h in units]
        bh = [_col_of(beta[rows(c)], lane, h) for c, h in units]
        dec = [jnp.where(tril, jnp.exp(gch[n] - _row_of(gct[c], sub, h)), 0.0) for n, (c, h) in enumerate(units)]
        egc = [jnp.exp(gch[n]) for n in un]
        gl = [jnp.sum(jnp.where(rowc == CHUNK - 1, gch[n], 0.0), axis=0, keepdims=True) for n in un]
        kb = [k[n] * bh[n] for n in un]
        a = [jnp.where(strict, _dot(kb[n], k[n], NT) * dec[n], 0.0) for n in un]
        p = [_dot(q[n], k[n], NT) * dec[n] for n in un]
        ycor = _inv_corrections(a)
        rhs = [jnp.concatenate([v[n] * bh[n], kb[n] * egc[n]], axis=1) for n in un]
        sol = [rhs[n] + _dot(ycor[n], rhs[n]) for n in un]
        qg = [q[n] * egc[n] for n in un]
        kd = [k[n] * jnp.exp(gl[n] - gch[n]) for n in un]
        egl = [jnp.exp(gl[n]) for n in un]
        s_cur, s_in, o = [state[h] for h in hs], [], []
        for c in range(grp):
            ns = [c * heads + h for h in hs]
            vn = [sol[n][:, :DN_DK] - _dot(sol[n][:, DN_DK:], s_cur[h]) for h, n in enumerate(ns)]
            o += [_dot(qg[n], s_cur[h]) + _dot(p[n], vn[h]) for h, n in enumerate(ns)]
            s_in += s_cur
            s_cur = [s_cur[h] * egl[n] + _dot(kd[n], vn[h], TN) for h, n in enumerate(ns)]
        for n, (c, h) in enumerate(units):
            o_ref[rows(c), h * DN_DK:(h + 1) * DN_DK] = o[n]
            s_ref[c, h] = s_in[n]
            y_ref[h, rows(c), :] = ycor[n]
        for h in hs:
            state[h] = s_cur[h]

    res = pl.pallas_call(
        _carrying(compute, 4, 3, 1, carried, (n_chunks // grp,)),
        name="dn_fwd_carrying" if carried else "dn_fwd", grid=(n_chunks // grp,),
        in_specs=[pl.BlockSpec((span, 3 * d), lambda i: (i, 0)),
                  pl.BlockSpec((span, 2 * LANES), lambda i: (i, 0)),
                  pl.BlockSpec((1, LANES), lambda i: (0, 0)),
                  pl.BlockSpec((1, LANES), lambda i: (0, 0))] + [ANY] * len(extra.inputs),
        out_specs=[pl.BlockSpec((span, d), lambda i: (i, 0)),
                   pl.BlockSpec((grp, heads, DN_DK, DN_DK), lambda i: (i, 0, 0, 0)),
                   pl.BlockSpec((heads, span, CHUNK), lambda i: (0, i, 0))] + [ANY] * len(extra.out_shapes),
        out_shape=[jax.ShapeDtypeStruct((t, d), F32),
                   jax.ShapeDtypeStruct((n_chunks, heads, DN_DK, DN_DK), F32),
                   jax.ShapeDtypeStruct((heads, t, CHUNK), F32)] + extra.out_shapes,
        scratch_shapes=[pltpu.VMEM((heads, DN_DK, DN_DK), F32)] + (extra.scratch() if carried else []),
        compiler_params=_cparams(("arbitrary",)),
    )(qkv, ba, alog, dtb, *extra.inputs)
    return res[:3], res[3:]


def _dn_bwd(qkv, ba, alog, dtb, dout, states, ycors, d, carried=None):
    t = qkv.shape[0]
    heads = d // DN_DK
    n_chunks = t // CHUNK
    grp = DN_GROUP if n_chunks % DN_GROUP == 0 else 1
    span = grp * CHUNK
    rev = lambda i: n_chunks // grp - 1 - i
    extra = carried or _Carried([], [], 0, None)

    def compute(qkv_ref, ba_ref, al_ref, dt_ref, do_ref, s_ref, y_ref,
                dqkv_ref, dba_ref, dal_ref, ddt_ref, dstate):
        @pl.when(pl.program_id(0) == 0)
        def _():
            dstate[...] = jnp.zeros_like(dstate)
            dal_ref[...] = jnp.zeros_like(dal_ref)
            ddt_ref[...] = jnp.zeros_like(ddt_ref)

        tril, strict, triu = _chunk_masks()
        beta, g, ea, sig_a = _beta_g(ba_ref[...], al_ref[...], dt_ref[...])
        lane = lax.broadcasted_iota(jnp.int32, (CHUNK, LANES), 1)
        sub = lax.broadcasted_iota(jnp.int32, (LANES, CHUNK), 0)
        rowc = lax.broadcasted_iota(jnp.int32, (CHUNK, 1), 0)
        hs = range(heads)
        units = [(c, h) for c in range(grp) for h in hs]
        un = range(len(units))
        rows = lambda c: slice(c * CHUNK, (c + 1) * CHUNK)
        rsum = lambda x_: jnp.sum(x_, axis=1, keepdims=True)
        gc = [_dot(jnp.where(tril, 1.0, 0.0), g[rows(c)], NN, HIGHEST) for c in range(grp)]
        gct = [m.T for m in gc]
        q = [qkv_ref[rows(c), h * DN_DK:(h + 1) * DN_DK] for c, h in units]
        k = [qkv_ref[rows(c), d + h * DN_DK:d + (h + 1) * DN_DK] for c, h in units]
        v = [qkv_ref[rows(c), 2 * d + h * DN_DK:2 * d + (h + 1) * DN_DK] for c, h in units]
        dout_h = [do_ref[rows(c), h * DN_DK:(h + 1) * DN_DK] for c, h in units]
        s0 = [s_ref[c, h] for c, h in units]
        ycor = [y_ref[h, rows(c), :] for c, h in units]
        gch = [_col_of(gc[c], lane, h) for c, h in units]
        bh = [_col_of(beta[rows(c)], lane, h) for c, h in units]
        dec = [jnp.where(tril, jnp.exp(gch[n] - _row_of(gct[c], sub, h)), 0.0) for n, (c, h) in enumerate(units)]
        egc = [jnp.exp(gch[n]) for n in un]
        gl = [jnp.sum(jnp.where(rowc == CHUNK - 1, gch[n], 0.0), axis=0, keepdims=True) for n in un]
        egl = [jnp.exp(gl[n]) for n in un]
        ekd = [jnp.exp(gl[n] - gch[n]) for n in un]
        kb = [k[n] * bh[n] for n in un]
        kd = [k[n] * ekd[n] for n in un]
        qg = [q[n] * egc[n] for n in un]
        kbg = [kb[n] * egc[n] for n in un]
        a = [jnp.where(strict, _dot(kb[n], k[n], NT) * dec[n], 0.0) for n in un]
        p = [_dot(q[n], k[n], NT) * dec[n] for n in un]
        rhs = [jnp.concatenate([v[n] * bh[n], kbg[n]], axis=1) for n in un]
        sol = [rhs[n] + _dot(ycor[n], rhs[n]) for n in un]
        w = [sol[n][:, DN_DK:] for n in un]
        vn = [sol[n][:, :DN_DK] - _dot(w[n], s0[n]) for n in un]
        dqg = [_dot(dout_h[n], s0[n], NT) for n in un]
        dp = [jnp.where(tril, _dot(dout_h[n], vn[n], NT), 0.0) for n in un]
        pdo = [_dot(p[n], dout_h[n], TN) for n in un]
        qdo = [_dot(qg[n], dout_h[n], TN) for n in un]
        ds_cur = [dstate[h] for h in hs]
        dsn, dvn = [None] * len(units), [None] * len(units)
        for c in reversed(range(grp)):
            for h in hs:
                dsn[c * heads + h] = ds_cur[h]
            for h in hs:
                n = c * heads + h
                dvn[n] = pdo[n] + _dot(kd[n], ds_cur[h])
            ds_cur = [qdo[c * heads + h] + egl[c * heads + h] * ds_cur[h]
                      - _dot(w[c * heads + h], dvn[c * heads + h], TN) for h in hs]
        dkd = [_dot(vn[n], dsn[n], NT) for n in un]
        dw = [-_dot(dvn[n], s0[n], NT) for n in un]
        dgl = [jnp.sum(rsum(dsn[n] * s0[n]), axis=0, keepdims=True) * egl[n] for n in un]
        dsol = [jnp.concatenate([dvn[n], dw[n]], axis=1) for n in un]
        drhs = [dsol[n] + _dot(ycor[n], dsol[n], TN) for n in un]
        dvb = [drhs[n][:, :DN_DK] for n in un]
        dkbg = [drhs[n][:, DN_DK:] for n in un]
        da = [jnp.where(strict, -_dot(drhs[n], sol[n], NT), 0.0) for n in un]
        dma = [da[n] * dec[n] for n in un]
        dmp = [dp[n] * dec[n] for n in un]
        dkb = [_dot(dma[n], k[n]) + dkbg[n] * egc[n] for n in un]
        dq = [_dot(dmp[n], k[n]) + dqg[n] * egc[n] for n in un]
        dk = [_dot(dma[n], kb[n], TN) + _dot(dmp[n], q[n], TN) + dkd[n] * ekd[n] + dkb[n] * bh[n] for n in un]
        e = [da[n] * a[n] + dp[n] * p[n] for n in un]
        colsum = [jnp.sum(e[n], axis=0, keepdims=True) for n in un]
        tkd = [rsum(dkd[n] * kd[n]) for n in un]
        for n, (c, h) in enumerate(units):
            dqkv_ref[rows(c), h * DN_DK:(h + 1) * DN_DK] = dq[n]
            dqkv_ref[rows(c), d + h * DN_DK:d + (h + 1) * DN_DK] = dk[n]
            dqkv_ref[rows(c), 2 * d + h * DN_DK:2 * d + (h + 1) * DN_DK] = dvb[n] * bh[n]
        for h in hs:
            dstate[h] = ds_cur[h]
        valid = lane < heads
        dal_acc = jnp.zeros((SUBLANES, LANES), F32)
        ddt_acc = jnp.zeros((SUBLANES, LANES), F32)
        for c in range(grp):
            dgc_all = jnp.zeros((CHUNK, LANES), F32)
            dbeta_all = jnp.zeros((CHUNK, LANES), F32)
            colsums = jnp.zeros((LANES, CHUNK), F32)
            for h in hs:
                n = c * heads + h
                dgc = rsum(e[n]) + rsum(dqg[n] * qg[n]) - tkd[n] + rsum(dkbg[n] * kbg[n])
                dgc = dgc + jnp.where(rowc == CHUNK - 1, dgl[n] + jnp.sum(tkd[n], axis=0, keepdims=True), 0.0)
                dgc_all = dgc_all + jnp.where(lane == h, dgc, 0.0)
                colsums = colsums + jnp.where(sub == h, colsum[n], 0.0)
                dbeta_all = dbeta_all + jnp.where(lane == h, rsum(dkb[n] * k[n]) + rsum(dvb[n] * v[n]), 0.0)
            dg = _dot(jnp.where(triu, 1.0, 0.0), dgc_all - colsums.T, NN, HIGHEST)
            beta_c = beta[rows(c)]
            dbl = jnp.where(valid, dbeta_all * beta_c * (1.0 - beta_c), 0.0)
            dal = jnp.where(valid, -dg * ea * sig_a[rows(c)], 0.0)
            dba_ref[rows(c), :LANES] = dbl.astype(dba_ref.dtype)
            dba_ref[rows(c), LANES:] = dal.astype(dba_ref.dtype)
            dal_acc = dal_acc + _fold8(jnp.where(valid, dg * g[rows(c)], 0.0))
            ddt_acc = ddt_acc + _fold8(dal)
        dal_ref[...] += dal_acc
        ddt_ref[...] += ddt_acc

    res = pl.pallas_call(
        _carrying(compute, 7, 4, 1, carried, (n_chunks // grp,)),
        name="dn_bwd_carrying" if carried else "dn_bwd", grid=(n_chunks // grp,),
        in_specs=[pl.BlockSpec((span, 3 * d), lambda i: (rev(i), 0)),
                  pl.BlockSpec((span, 2 * LANES), lambda i: (rev(i), 0)),
                  pl.BlockSpec((1, LANES), lambda i: (0, 0)),
                  pl.BlockSpec((1, LANES), lambda i: (0, 0)),
                  pl.BlockSpec((span, d), lambda i: (rev(i), 0)),
                  pl.BlockSpec((grp, heads, DN_DK, DN_DK), lambda i: (rev(i), 0, 0, 0)),
                  pl.BlockSpec((heads, span, CHUNK), lambda i: (0, rev(i), 0))] + [ANY] * len(extra.inputs),
        out_specs=[pl.BlockSpec((span, 3 * d), lambda i: (rev(i), 0)),
                   pl.BlockSpec((span, 2 * LANES), lambda i: (rev(i), 0)),
                   pl.BlockSpec((SUBLANES, LANES), lambda i: (0, 0)),
                   pl.BlockSpec((SUBLANES, LANES), lambda i: (0, 0))] + [ANY] * len(extra.out_shapes),
        out_shape=[jax.ShapeDtypeStruct((t, 3 * d), F32),
                   jax.ShapeDtypeStruct((t, 2 * LANES), ACT),
                   jax.ShapeDtypeStruct((SUBLANES, LANES), F32),
                   jax.ShapeDtypeStruct((SUBLANES, LANES), F32)] + extra.out_shapes,
        scratch_shapes=[pltpu.VMEM((heads, DN_DK, DN_DK), F32)] + (extra.scratch() if carried else []),
        compiler_params=_cparams(("arbitrary",)),
    )(qkv, ba, alog, dtb, dout, states, ycors, *extra.inputs)
    return res[:4], res[4:]


def _sgu_mask():
    row = lax.broadcasted_iota(jnp.int32, (SGU_BLOCK, SGU_BLOCK), 0)
    col = lax.broadcasted_iota(jnp.int32, (SGU_BLOCK, SGU_BLOCK), 1)
    sh = int(math.log2(CHUNK))
    return lax.shift_right_logical(row, sh) >= lax.shift_right_logical(col, sh)


def _gate_sgu_fwd(o, projm, onw, lng, lnb, ws, bst, d):
    t = o.shape[0]
    heads, groups = d // DN_DK, d // SGU_GROUP_DIM
    tb = SGU_BLOCK
    row_spec = pl.BlockSpec((1, d), lambda i: (0, 0))

    def body(o_ref, z_ref, u_ref, v_ref, onw_ref, lng_ref, lnb_ref, ws_ref, bst_ref, ya_ref, yb_ref):
        for h in range(heads):
            cols = slice(h * DN_DK, (h + 1) * DN_DK)
            oh, zh = o_ref[:, cols], z_ref[:, cols]
            r = lax.rsqrt(jnp.mean(oh * oh, axis=1, keepdims=True) + RMS_EPS)
            ya_ref[:, cols] = (oh * r * onw_ref[:, cols] * (zh * _sigmoid(zh))).astype(ya_ref.dtype)
        xhat, _ = _ln_hat(_gelu(v_ref[...]))
        vgn = xhat * lng_ref[...] + lnb_ref[...]
        mask = _sgu_mask()
        lane = lax.broadcasted_iota(jnp.int32, (SGU_BLOCK, LANES), 1)
        bst_v = bst_ref[...]
        for gi in range(groups):
            cols = slice(gi * SGU_GROUP_DIM, (gi + 1) * SGU_GROUP_DIM)
            wsg = jnp.where(mask, ws_ref[gi], 0.0)
            sp = _dot(wsg, vgn[:, cols]) + _col_of(bst_v, lane, gi)
            yb_ref[:, cols] = (_gelu(u_ref[:, cols]) * sp).astype(yb_ref.dtype)

    return pl.pallas_call(
        body, name="gate_sgu_fwd", grid=(t // tb,),
        in_specs=[pl.BlockSpec((tb, d), lambda i: (i, 0)),
                  pl.BlockSpec((tb, d), lambda i: (i, 3)),
                  pl.BlockSpec((tb, d), lambda i: (i, 4)),
                  pl.BlockSpec((tb, d), lambda i: (i, 5)),
                  row_spec, row_spec, row_spec,
                  pl.BlockSpec((groups, SGU_BLOCK, SGU_BLOCK), lambda i: (0, 0, 0)),
                  pl.BlockSpec((SGU_BLOCK, LANES), lambda i: (0, 0))],
        out_specs=[pl.BlockSpec((tb, d), lambda i: (i, 0)), pl.BlockSpec((tb, d), lambda i: (i, 0))],
        out_shape=[jax.ShapeDtypeStruct((t, d), ACT), jax.ShapeDtypeStruct((t, d), ACT)],
        compiler_params=_cparams(("parallel",)),
    )(o, projm, projm, projm, onw, lng, lnb, ws, bst)


def _gate_sgu_bwd(dya, dyb, o, projm, onw, lng, lnb, ws, bst, dprojm, d):
    t = o.shape[0]
    heads, groups = d // DN_DK, d // SGU_GROUP_DIM
    tb = SGU_BLOCK
    row_spec = pl.BlockSpec((1, d), lambda i: (0, 0))
    acc_row = pl.BlockSpec((SUBLANES, d), lambda i: (0, 0))

    def body(dya_ref, dyb_ref, o_ref, z_ref, u_ref, v_ref, onw_ref, lng_ref, lnb_ref, ws_ref, bst_ref, alias_ref,
             do_ref, dp_ref, donw_ref, dlng_ref, dlnb_ref, dws_ref, dbst_ref):
        @pl.when(pl.program_id(0) == 0)
        def _():
            for r_ in (donw_ref, dlng_ref, dlnb_ref, dws_ref, dbst_ref):
                r_[...] = jnp.zeros_like(r_)

        donw = jnp.zeros((SUBLANES, DN_DK), F32)
        for h in range(heads):
            cols = slice(h * DN_DK, (h + 1) * DN_DK)
            oh, zh, dyah, wh = o_ref[:, cols], z_ref[:, cols], dya_ref[:, cols], onw_ref[:, cols]
            r = lax.rsqrt(jnp.mean(oh * oh, axis=1, keepdims=True) + RMS_EPS)
            on = oh * r
            sz = _sigmoid(zh)
            silu_z = zh * sz
            don = dyah * wh * silu_z
            dp_ref[:, cols] = (dyah * on * wh * (sz * (1.0 + zh * (1.0 - sz)))).astype(dp_ref.dtype)
            donw = donw + _fold8(dyah * on * silu_z)
            do_ref[:, cols] = r * (don - on * jnp.mean(don * on, axis=1, keepdims=True))
        donw_ref[...] += donw

        vgp, up = v_ref[...], u_ref[...]
        xhat, rstd = _ln_hat(_gelu(vgp))
        lng_v = lng_ref[...]
        vgn = xhat * lng_v + lnb_ref[...]
        ua = _gelu(up)
        mask = _sgu_mask()
        lane = lax.broadcasted_iota(jnp.int32, (SGU_BLOCK, LANES), 1)
        bst_v = bst_ref[...]
        dbst = jnp.zeros((SGU_BLOCK, LANES), F32)
        dvgn_parts, dua_parts = [], []
        for gi in range(groups):
            cols = slice(gi * SGU_GROUP_DIM, (gi + 1) * SGU_GROUP_DIM)
            wsg = jnp.where(mask, ws_ref[gi], 0.0)
            vg_g, dyb_g = vgn[:, cols], dyb_ref[:, cols]
            sp = _dot(wsg, vg_g) + _col_of(bst_v, lane, gi)
            dsp = dyb_g * ua[:, cols]
            dua_parts.append(dyb_g * sp)
            dws_ref[gi] += jnp.where(mask, _dot(dsp, vg_g, NT), 0.0)
            dbst = dbst + jnp.where(lane == gi, jnp.sum(dsp, axis=1, keepdims=True), 0.0)
            dvgn_parts.append(_dot(wsg, dsp, TN))
        dbst_ref[...] += dbst
        dvgn = jnp.concatenate(dvgn_parts, axis=1)
        dua = jnp.concatenate(dua_parts, axis=1)
        dlng_ref[...] += _fold8(dvgn * xhat)
        dlnb_ref[...] += _fold8(dvgn)
        dvga = _ln_bwd(dvgn * lng_v, xhat, rstd)
        dp_ref[:, d:2 * d] = (dua * _gelu_grad(up)).astype(dp_ref.dtype)
        dp_ref[:, 2 * d:] = (dvga * _gelu_grad(vgp)).astype(dp_ref.dtype)

    return pl.pallas_call(
        body, name="gate_sgu_bwd", grid=(t // tb,),
        in_specs=[pl.BlockSpec((tb, d), lambda i: (i, 0)),
                  pl.BlockSpec((tb, d), lambda i: (i, 0)),
                  pl.BlockSpec((tb, d), lambda i: (i, 0)),
                  pl.BlockSpec((tb, d), lambda i: (i, 3)),
                  pl.BlockSpec((tb, d), lambda i: (i, 4)),
                  pl.BlockSpec((tb, d), lambda i: (i, 5)),
                  row_spec, row_spec, row_spec,
                  pl.BlockSpec((groups, SGU_BLOCK, SGU_BLOCK), lambda i: (0, 0, 0)),
                  pl.BlockSpec((SGU_BLOCK, LANES), lambda i: (0, 0)),
                  ANY],
        out_specs=[pl.BlockSpec((tb, d), lambda i: (i, 0)),
                   pl.BlockSpec((tb, 3 * d), lambda i: (i, 1)),
                   pl.BlockSpec((SUBLANES, DN_DK), lambda i: (0, 0)),
                   acc_row, acc_row,
                   pl.BlockSpec((groups, SGU_BLOCK, SGU_BLOCK), lambda i: (0, 0, 0)),
                   pl.BlockSpec((SGU_BLOCK, LANES), lambda i: (0, 0))],
        out_shape=[jax.ShapeDtypeStruct((t, d), F32),
                   jax.ShapeDtypeStruct(dprojm.shape, dprojm.dtype),
                   jax.ShapeDtypeStruct((SUBLANES, DN_DK), F32),
                   jax.ShapeDtypeStruct((SUBLANES, d), F32),
                   jax.ShapeDtypeStruct((SUBLANES, d), F32),
                   jax.ShapeDtypeStruct((groups, SGU_BLOCK, SGU_BLOCK), F32),
                   jax.ShapeDtypeStruct((SGU_BLOCK, LANES), F32)],
        input_output_aliases={11: 1},
        compiler_params=_cparams(("arbitrary",)),
    )(dya, dyb, o, projm, projm, projm, onw, lng, lnb, ws, bst, dprojm)


def _mix_fwd(ya, yb, projm, x, wpa, wpb, wo, g1, b1, d, tb):
    t = x.shape[0]
    blk = pl.BlockSpec((tb, d), lambda i: (i, 0))
    wspec = pl.BlockSpec((d, d), lambda i: (0, 0))
    row_spec = pl.BlockSpec((1, d), lambda i: (0, 0))

    def body(ya_ref, yb_ref, ga_ref, gb_ref, x_ref, wpa_ref, wpb_ref, wo_ref, g_ref, b_ref,
             pa_ref, pb_ref, m_ref, h_ref, x1_ref, x1b_ref):
        pa = _dot(ya_ref[...], wpa_ref[...])
        pb = _dot(yb_ref[...], wpb_ref[...])
        m = _sigmoid(ga_ref[...]) * pa + _sigmoid(gb_ref[...]) * pb
        hres = ALPHA * x_ref[...] + _dot(m, wo_ref[...])
        xhat, _ = _ln_hat(hres)
        x1 = xhat * g_ref[...] + b_ref[...]
        pa_ref[...] = pa
        pb_ref[...] = pb
        m_ref[...] = m.astype(m_ref.dtype)
        h_ref[...] = hres
        x1_ref[...] = x1
        x1b_ref[...] = x1.astype(x1b_ref.dtype)

    f32_out = jax.ShapeDtypeStruct((t, d), F32)
    bf_out = jax.ShapeDtypeStruct((t, d), ACT)
    return pl.pallas_call(
        body, name="mix_fwd", grid=(t // tb,),
        in_specs=[blk, blk, pl.BlockSpec((tb, d), lambda i: (i, 6)), pl.BlockSpec((tb, d), lambda i: (i, 7)),
                  blk, wspec, wspec, wspec, row_spec, row_spec],
        out_specs=[blk] * 6,
        out_shape=[f32_out, f32_out, bf_out, f32_out, f32_out, bf_out],
        compiler_params=_cparams(("parallel",)),
    )(ya, yb, projm, projm, x, wpa, wpb, wo, g1, b1)


def _mix_bwd(dmix, pa, pb, projm, wpa, wpb, wo, d, tb):
    t = dmix.shape[0]
    blk = pl.BlockSpec((tb, d), lambda i: (i, 0))
    wspec = pl.BlockSpec((d, d), lambda i: (0, 0))

    def body(dmix_ref, pa_ref, pb_ref, ga_ref, gb_ref, wpa_ref, wpb_ref, wo_ref,
             dpa_ref, dpb_ref, dya_ref, dyb_ref, dg_ref):
        dm = _dot(dmix_ref[...], wo_ref[...], NT)
        sa, sb = _sigmoid(ga_ref[...]), _sigmoid(gb_ref[...])
        dpa, dpb = dm * sa, dm * sb
        dpa_ref[...] = dpa.astype(dpa_ref.dtype)
        dpb_ref[...] = dpb.astype(dpb_ref.dtype)
        dg_ref[:, :d] = (dm * pa_ref[...] * sa * (1.0 - sa)).astype(dg_ref.dtype)
        dg_ref[:, d:] = (dm * pb_ref[...] * sb * (1.0 - sb)).astype(dg_ref.dtype)
        dya_ref[...] = _dot(dpa, wpa_ref[...], NT)
        dyb_ref[...] = _dot(dpb, wpb_ref[...], NT)

    return pl.pallas_call(
        body, name="mix_bwd", grid=(t // tb,),
        in_specs=[blk, blk, blk, pl.BlockSpec((tb, d), lambda i: (i, 6)), pl.BlockSpec((tb, d), lambda i: (i, 7)),
                  wspec, wspec, wspec],
        out_specs=[blk, blk, blk, blk, pl.BlockSpec((tb, 2 * d), lambda i: (i, 3))],
        out_shape=[jax.ShapeDtypeStruct((t, d), ACT), jax.ShapeDtypeStruct((t, d), ACT),
                   jax.ShapeDtypeStruct((t, d), F32), jax.ShapeDtypeStruct((t, d), F32),
                   jax.ShapeDtypeStruct((t, 8 * d), ACT)],
        compiler_params=_cparams(("parallel",)),
    )(dmix, pa, pb, projm, projm, wpa, wpb, wo)


def _ffn_tail_fwd(gu, wd, x1, g, b, tb):
    t, d = x1.shape
    f = wd.shape[0]
    fc = _tile(f, MM_TILE)
    blk = pl.BlockSpec((tb, d), lambda i: (i, 0))
    row_spec = pl.BlockSpec((1, d), lambda i: (0, 0))

    def body(gu_ref, wd_ref, x_ref, g_ref, b_ref, a_ref, h_ref, y_ref, yb_ref):
        ffn = jnp.zeros((tb, d), F32)
        for c in range(f // fc):
            gp = gu_ref[:, c * fc:(c + 1) * fc]
            act = (gp * _sigmoid(gp) * gu_ref[:, f + c * fc:f + (c + 1) * fc]).astype(a_ref.dtype)
            a_ref[:, c * fc:(c + 1) * fc] = act
            ffn = ffn + _dot(act, wd_ref[c * fc:(c + 1) * fc, :])
        hres = ALPHA * x_ref[...] + ffn
        xhat, _ = _ln_hat(hres)
        y = xhat * g_ref[...] + b_ref[...]
        h_ref[...] = hres
        y_ref[...] = y
        yb_ref[...] = y.astype(yb_ref.dtype)

    return pl.pallas_call(
        body, name="ffn_tail_fwd", grid=(t // tb,),
        in_specs=[pl.BlockSpec((tb, 2 * f), lambda i: (i, 0)), pl.BlockSpec((f, d), lambda i: (0, 0)),
                  blk, row_spec, row_spec],
        out_specs=[pl.BlockSpec((tb, f), lambda i: (i, 0)), blk, blk, blk],
        out_shape=[jax.ShapeDtypeStruct((t, f), ACT), jax.ShapeDtypeStruct((t, d), F32),
                   jax.ShapeDtypeStruct((t, d), F32), jax.ShapeDtypeStruct((t, d), ACT)],
        compiler_params=_cparams(("parallel",)),
    )(gu, wd, x1, g, b)


def _ffn_tail_bwd(dh, wd, gu, tb):
    t, d = dh.shape
    f = wd.shape[0]
    fc = _tile(f, MM_TILE)

    def body(dh_ref, wd_ref, gu_ref, dgu_ref):
        dh_v = dh_ref[...]
        for c in range(f // fc):
            da = _dot(dh_v, wd_ref[c * fc:(c + 1) * fc, :], NT)
            gp = gu_ref[:, c * fc:(c + 1) * fc]
            sg = _sigmoid(gp)
            dgu_ref[:, c * fc:(c + 1) * fc] = (
                da * gu_ref[:, f + c * fc:f + (c + 1) * fc] * sg * (1.0 + gp * (1.0 - sg))).astype(dgu_ref.dtype)
            dgu_ref[:, f + c * fc:f + (c + 1) * fc] = (da * gp * sg).astype(dgu_ref.dtype)

    return pl.pallas_call(
        body, name="ffn_tail_bwd", grid=(t // tb,),
        in_specs=[pl.BlockSpec((tb, d), lambda i: (i, 0)), pl.BlockSpec((f, d), lambda i: (0, 0)),
                  pl.BlockSpec((tb, 2 * f), lambda i: (i, 0))],
        out_specs=pl.BlockSpec((tb, 2 * f), lambda i: (i, 0)),
        out_shape=jax.ShapeDtypeStruct((t, 2 * f), ACT),
        compiler_params=_cparams(("parallel",)),
    )(dh, wd, gu)


def _ffn_head_bwd(dgu, wgu, dh2, hres, g, tb):
    t, d = dh2.shape
    f2 = wgu.shape[1]
    blk = pl.BlockSpec((tb, d), lambda i: (i, 0))
    acc = pl.BlockSpec((SUBLANES, d), lambda i: (0, 0))

    def body(dgu_ref, w_ref, dh2_ref, h_ref, g_ref, dh_ref, dhb_ref, dg_ref, db_ref):
        @pl.when(pl.program_id(0) == 0)
        def _():
            dg_ref[...] = jnp.zeros_like(dg_ref)
            db_ref[...] = jnp.zeros_like(db_ref)

        dy_v = _dot(dgu_ref[...], w_ref[...], NT) + ALPHA * dh2_ref[...]
        xhat, r = _ln_hat(h_ref[...])
        dh = _ln_bwd(dy_v * g_ref[...], xhat, r)
        dh_ref[...] = dh
        dhb_ref[...] = dh.astype(dhb_ref.dtype)
        dg_ref[...] += _fold8(dy_v * xhat)
        db_ref[...] += _fold8(dy_v)

    return pl.pallas_call(
        body, name="ffn_head_bwd", grid=(t // tb,),
        in_specs=[pl.BlockSpec((tb, f2), lambda i: (i, 0)), pl.BlockSpec((d, f2), lambda i: (0, 0)),
                  blk, blk, pl.BlockSpec((1, d), lambda i: (0, 0))],
        out_specs=[blk, blk, acc, acc],
        out_shape=[jax.ShapeDtypeStruct((t, d), F32), jax.ShapeDtypeStruct((t, d), ACT),
                   jax.ShapeDtypeStruct((SUBLANES, d), F32), jax.ShapeDtypeStruct((SUBLANES, d), F32)],
        compiler_params=_cparams(("arbitrary",)),
    )(dgu, wgu, dh2, hres, g)


def _loss_ln_bwd(y, target, hres, g, tb):
    t, d = y.shape
    blk = pl.BlockSpec((tb, d), lambda i: (i, 0))
    acc = pl.BlockSpec((SUBLANES, d), lambda i: (0, 0))

    def body(y_ref, t_ref, h_ref, g_ref, dh_ref, dhb_ref, dg_ref, db_ref, l_ref):
        @pl.when(pl.program_id(0) == 0)
        def _():
            for r_ in (dg_ref, db_ref, l_ref):
                r_[...] = jnp.zeros_like(r_)

        err = y_ref[...] - t_ref[...]
        dy_v = err * (1.0 / d)
        sq = _fold8(err * err)
        part = sq[:, :LANES]
        for c in range(1, d // LANES):
            part = part + sq[:, c * LANES:(c + 1) * LANES]
        l_ref[...] += part
        xhat, r = _ln_hat(h_ref[...])
        dh = _ln_bwd(dy_v * g_ref[...], xhat, r)
        dh_ref[...] = dh
        dhb_ref[...] = dh.astype(dhb_ref.dtype)
        dg_ref[...] += _fold8(dy_v * xhat)
        db_ref[...] += _fold8(dy_v)

    res = pl.pallas_call(
        body, name="loss_ln_bwd", grid=(t // tb,),
        in_specs=[blk, blk, blk, pl.BlockSpec((1, d), lambda i: (0, 0))],
        out_specs=[blk, blk, acc, acc, pl.BlockSpec((SUBLANES, LANES), lambda i: (0, 0))],
        out_shape=[jax.ShapeDtypeStruct((t, d), F32), jax.ShapeDtypeStruct((t, d), ACT),
                   jax.ShapeDtypeStruct((SUBLANES, d), F32), jax.ShapeDtypeStruct((SUBLANES, d), F32),
                   jax.ShapeDtypeStruct((SUBLANES, LANES), F32)],
        compiler_params=_cparams(("arbitrary",)),
    )(y, target, hres, g)
    return res[:4], res[4]


def _adamw(w, g, m, v):
    shape = w.shape
    cols = shape[-1]
    w2, g2, m2, v2 = (a.reshape(-1, cols) for a in (w, g, m, v))
    rows = w2.shape[0]
    tr = _tile(rows, 256, SUBLANES)
    blk = pl.BlockSpec((tr, cols), lambda i: (i, 0))

    def body(w_ref, g_ref, m_ref, v_ref, d_ref, nm_ref, nv_ref):
        g_v = g_ref[...]
        nm = ADAM_B1 * m_ref[...] + (1.0 - ADAM_B1) * g_v
        nv = ADAM_B2 * v_ref[...] + (1.0 - ADAM_B2) * (g_v * g_v)
        m_hat = nm / (1.0 - ADAM_B1 ** ADAM_STEP)
        v_hat = nv / (1.0 - ADAM_B2 ** ADAM_STEP)
        d_ref[...] = -ADAM_LR * (m_hat / (jnp.sqrt(v_hat) + ADAM_EPS) + ADAM_WD * w_ref[...])
        nm_ref[...] = nm
        nv_ref[...] = nv

    out = jax.ShapeDtypeStruct((rows, cols), F32)
    res = pl.pallas_call(
        body, name="adamw", grid=(rows // tr,),
        in_specs=[blk] * 4, out_specs=[blk] * 3, out_shape=[out] * 3,
        compiler_params=_cparams(("parallel",)),
    )(w2, g2, m2, v2)
    return tuple(r.reshape(shape) for r in res)


def _place():
    x, y, c = lax.axis_index("x"), lax.axis_index("y"), lax.axis_index("c")
    return x, y, c, [(1 - x, y), (x, 1 - y), (1 - x, 1 - y)]


def _remote(src, dst, send_sems, recv_sems, k, to):
    return pltpu.make_async_remote_copy(src_ref=src, dst_ref=dst, send_sem=send_sems.at[k],
                                        recv_sem=recv_sems.at[k], device_id=to, device_id_type=MESH)


class _Carried:
    def __init__(self, inputs, out_shapes, n_sems, copies):
        self.inputs, self.out_shapes, self.n_sems, self.copies = list(inputs), list(out_shapes), n_sems, copies

    def scratch(self):
        return [pltpu.SemaphoreType.DMA((self.n_sems,)), pltpu.SemaphoreType.DMA((self.n_sems,))]


def _join_plans(first, second):
    ni, no, ns = len(first.inputs), len(first.out_shapes), first.n_sems

    def copies(in_refs, out_refs, send_sems, recv_sems):
        start1, finish1 = first.copies(in_refs[:ni], out_refs[:no], send_sems, recv_sems)
        start2, finish2 = second.copies(in_refs[ni:], out_refs[no:], send_sems.at[pl.ds(ns, second.n_sems)],
                                        recv_sems.at[pl.ds(ns, second.n_sems)])

        def start():
            start1()
            start2()

        def finish():
            finish1()
            finish2()

        return start, finish

    return _Carried(first.inputs + second.inputs, first.out_shapes + second.out_shapes, ns + second.n_sems, copies)


def _run_comm(name, plan):
    n_in, n_out = len(plan.inputs), len(plan.out_shapes)

    def body(*refs):
        start, finish = plan.copies(refs[:n_in], refs[n_in:n_in + n_out], refs[-2], refs[-1])
        start()
        finish()

    return pl.pallas_call(
        body, name=name, in_specs=[ANY] * n_in, out_specs=[ANY] * n_out, out_shape=plan.out_shapes,
        scratch_shapes=plan.scratch(),
    )(*plan.inputs)


def _half_rows(rows, core):
    if rows % (4 * SUBLANES):
        return None
    return pl.ds(pl.multiple_of(core * (rows // 2), 2 * SUBLANES), rows // 2)


def _all_gather_plan(shards):
    n = len(shards)

    def copies(x_refs, out_refs, send_sems, recv_sems):
        x, y, c, chips = _place()
        sibling = (x, y, 1 - c)
        mine = 2 * x + y
        split = [_half_rows(x_refs[t].shape[0], c) is not None for t in range(n)]

        def src(t):
            return x_refs[t].at[_half_rows(x_refs[t].shape[0], c)] if split[t] else x_refs[t]

        def slot(t, chip_idx, core):
            rows = _half_rows(x_refs[t].shape[0], core)
            return out_refs[t].at[chip_idx, rows] if split[t] else out_refs[t].at[chip_idx]

        def first():
            return [_remote(src(t), slot(t, mine, c), send_sems, recv_sems, 6 * t + j, (cx, cy, c))
                    for j, (cx, cy) in enumerate(chips) for t in range(n)]

        def start():
            for cp in first():
                cp.start()

        def finish():
            passed = []
            for j, (cx, cy) in enumerate(chips):
                for t in range(n):
                    theirs = slot(t, 2 * cx + cy, c)
                    _remote(theirs, theirs, send_sems, recv_sems, 6 * t + j, (cx, cy, c)).wait_recv()
                    if split[t]:
                        fwd = _remote(theirs, theirs, send_sems, recv_sems, 6 * t + 3 + j, sibling)
                        fwd.start()
                        passed.append(fwd)
            for j, (cx, cy) in enumerate(chips):
                for t in range(n):
                    if split[t]:
                        other = slot(t, 2 * cx + cy, 1 - c)
                        _remote(other, other, send_sems, recv_sems, 6 * t + 3 + j, sibling).wait_recv()
            for cp in first() + passed:
                cp.wait_send()

        return start, finish

    return _Carried(shards, [jax.ShapeDtypeStruct((N_CHIPS,) + s.shape, s.dtype) for s in shards], 6 * n, copies)


def _sibling_exchange_plan(grads, small=None):
    n = len(grads)
    extra = [] if small is None else [small]

    def copies(in_refs, out_refs, send_sems, recv_sems):
        x, y, c, _ = _place()
        sibling = (x, y, 1 - c)

        def all_copies():
            cps = [_remote(in_refs[t].at[:, _half_rows(in_refs[t].shape[1], 1 - c), :], out_refs[t],
                           send_sems, recv_sems, t, sibling) for t in range(n)]
            if extra:
                cps.append(_remote(in_refs[n], out_refs[n], send_sems, recv_sems, n, sibling))
            return cps

        def start():
            for cp in all_copies():
                cp.start()

        def finish():
            for cp in all_copies():
                cp.wait()

        return start, finish

    shapes = [jax.ShapeDtypeStruct((g.shape[0], g.shape[1] // 2, g.shape[2]), g.dtype) for g in grads]
    shapes += [jax.ShapeDtypeStruct(s.shape, s.dtype) for s in extra]
    return _Carried(list(grads) + extra, shapes, n + 1, copies)


def _chip_exchange_plan(travel, small=None):
    n = len(travel)
    extra = [] if small is None else [small]

    def copies(in_refs, out_refs, send_sems, recv_sems):
        x, y, c, chips = _place()
        mine = 2 * x + y

        def all_copies():
            cps = []
            for j, (cx, cy) in enumerate(chips):
                to = (cx, cy, c)
                for t in range(n):
                    cps.append(_remote(in_refs[t].at[2 * cx + cy], out_refs[t].at[mine], send_sems, recv_sems,
                                       3 * t + j, to))
                if extra:
                    cps.append(_remote(in_refs[n], out_refs[n].at[mine], send_sems, recv_sems, 3 * n + j, to))
            return cps

        def start():
            for cp in all_copies():
                cp.start()

        def finish():
            for cp in all_copies():
                cp.wait()

        return start, finish

    shapes = [jax.ShapeDtypeStruct(g.shape, g.dtype) for g in travel]
    shapes += [jax.ShapeDtypeStruct((N_CHIPS,) + s.shape, s.dtype) for s in extra]
    return _Carried(list(travel) + extra, shapes, 3 * n + 3, copies)


def _sibling_merge_plan(reduced):
    n = len(reduced)

    def copies(in_refs, out_refs, send_sems, recv_sems):
        x, y, c, _ = _place()

        def all_copies():
            return [_remote(in_refs[t], out_refs[t], send_sems, recv_sems, t, (x, y, 1 - c)) for t in range(n)]

        def start():
            for cp in all_copies():
                cp.start()

        def finish():
            for cp in all_copies():
                cp.wait()

        return start, finish

    return _Carried(reduced, [jax.ShapeDtypeStruct(r.shape, r.dtype) for r in reduced], n, copies)


def _pair_sum(place, grad, land):
    n, r, c = grad.shape
    half = r // 2
    tr = _tile(half, 256, SUBLANES)
    nb = half // tr

    def body(place_ref, a_ref, b_ref, travel_ref, own_ref):
        total = a_ref[0] + b_ref[0]
        travel_ref[0] = total.astype(travel_ref.dtype)

        @pl.when(pl.program_id(1) == place_ref[1])
        def _():
            own_ref[...] = total

    return pl.pallas_call(
        body, name="grad_pair_sum",
        grid_spec=pltpu.PrefetchScalarGridSpec(
            num_scalar_prefetch=1, grid=(nb, n),
            in_specs=[pl.BlockSpec((1, tr, c), lambda i, s, p: (s, p[0] * nb + i, 0)),
                      pl.BlockSpec((1, tr, c), lambda i, s, p: (s, i, 0))],
            out_specs=[pl.BlockSpec((1, tr, c), lambda i, s, p: (s, i, 0)),
                       pl.BlockSpec((tr, c), lambda i, s, p: (i, 0))]),
        out_shape=[jax.ShapeDtypeStruct((n, half, c), BF16), jax.ShapeDtypeStruct((half, c), F32)],
        compiler_params=_cparams(("parallel", "arbitrary")),
    )(place, grad, land)


def _chip_sum(place, own, land, name):
    n, r, c = land.shape
    tr = _tile(r, 256, SUBLANES)

    def body(place_ref, own_ref, land_ref, o_ref):
        mine = place_ref[1]
        acc = jnp.zeros(o_ref.shape, F32)
        for s in range(n):
            acc = acc + jnp.where(mine == s, own_ref[...], land_ref[s].astype(F32))
        o_ref[...] = acc

    return pl.pallas_call(
        body, name=name,
        grid_spec=pltpu.PrefetchScalarGridSpec(
            num_scalar_prefetch=1, grid=(r // tr,),
            in_specs=[pl.BlockSpec((tr, c), lambda i, p: (i, 0)),
                      pl.BlockSpec((n, tr, c), lambda i, p: (0, i, 0))],
            out_specs=pl.BlockSpec((tr, c), lambda i, p: (i, 0))),
        out_shape=jax.ShapeDtypeStruct((r, c), F32),
        compiler_params=_cparams(("parallel",)),
    )(place, own, land)


def _add2(a, b):
    rows = a.shape[0]
    tr = _tile(rows, 256, SUBLANES)
    blk = pl.BlockSpec((tr, a.shape[1]), lambda i: (i, 0))

    def body(a_ref, b_ref, o_ref):
        o_ref[...] = a_ref[...] + b_ref[...]

    return pl.pallas_call(
        body, name="grad_small_pair_sum", grid=(rows // tr,), in_specs=[blk, blk], out_specs=blk,
        out_shape=jax.ShapeDtypeStruct(a.shape, F32), compiler_params=_cparams(("parallel",)),
    )(a, b)


def _merge_halves(place, mine, other):
    first_core = place[0] == 0
    return jnp.concatenate([jnp.where(first_core, mine, other), jnp.where(first_core, other, mine)], axis=0)


_BIG = (("w_in", 2), ("w_pa", 1), ("w_pb", 1), ("w_o", 1), ("w_ffn_gate", 2), ("w_ffn_up", 2),
        ("w_ffn_down", 1))
_SMALL = ("conv_w", "a_log", "dt_bias", "o_norm_w", "sgu_ln_g", "sgu_ln_b", "w_s", "b_s",
          "ln1_g", "ln1_b", "ln2_g", "ln2_b")


def _pack_small(arrays):
    pieces = []
    for a in arrays:
        if a.shape[-1] % LANES == 0:
            a2 = a.reshape(-1, LANES)
        else:
            a2 = jnp.pad(a.reshape(-1, a.shape[-1]), ((0, 0), (0, LANES - a.shape[-1])))
        pieces.append(jnp.pad(a2, ((0, -a2.shape[0] % SUBLANES), (0, 0))))
    return jnp.concatenate(pieces, axis=0)


def _unpack_small(buf, like):
    out, off = [], 0
    for a in like:
        if a.shape[-1] % LANES == 0:
            rows = a.size // LANES
            out.append(buf[off:off + rows].reshape(a.shape))
        else:
            rows = a.size // a.shape[-1]
            out.append(buf[off:off + rows, :a.shape[-1]].reshape(a.shape))
        off += -(-rows // SUBLANES) * SUBLANES
    return out


def _unshard(gathered, local, chip, axis):
    parts = [jnp.where(chip == s, local, gathered[s]) for s in range(N_CHIPS)]
    return jnp.concatenate(parts, axis=axis - 1)


def _to_shards(full, axis):
    l, r, c = full.shape
    if axis == 1:
        return full.reshape(l, N_CHIPS, r // N_CHIPS, c)
    return jnp.transpose(full.reshape(l, r, N_CHIPS, c // N_CHIPS), (0, 2, 1, 3))


def _row(v, width=None):
    v = v.reshape(1, -1).astype(F32)
    if width is not None and v.shape[1] < width:
        v = jnp.pad(v, ((0, 0), (0, width - v.shape[1])))
    return v


def _layer_consts(p, l, d):
    heads = d // DN_DK
    return dict(
        alog=_row(p["a_log"][l], LANES), dtb=_row(p["dt_bias"][l], LANES),
        onw=_row(jnp.tile(p["o_norm_w"][l], heads)),
        lng=_row(p["sgu_ln_g"][l]), lnb=_row(p["sgu_ln_b"][l]),
        ws=p["w_s"][l].astype(F32),
        bst=jnp.pad(p["b_s"][l].T, ((0, 0), (0, LANES - p["b_s"].shape[1]))),
        g1=_row(p["ln1_g"][l]), b1=_row(p["ln1_b"][l]), g2=_row(p["ln2_g"][l]), b2=_row(p["ln2_b"][l]))


class _NoComm:
    def with_proj_main(self):
        return None

    def after_proj_main(self, got):
        pass

    def weights(self, full):
        return full

    def with_dn_fwd(self):
        return None

    def after_dn_fwd(self, got):
        pass

    def with_ffn_in_dw(self):
        return None

    def after_ffn_in_dw(self, got):
        pass

    def after_branch_grads(self, g):
        pass

    def with_dn_bwd(self):
        return None

    def after_dn_bwd(self, got):
        pass

    def with_proj_main_dw(self):
        return None

    def after_proj_main_dw(self, got):
        pass

    def with_ffn_in(self):
        return None

    def after_ffn_in(self, got):
        pass

    def after_all_grads(self, g):
        pass

    def with_proj_main_dx(self):
        return None

    def after_proj_main_dx(self, got):
        pass


def _carry(carried, after, call, *args, **kw):
    if carried is None:
        return call(*args, **kw)
    out, got = call(*args, carried=carried, **kw)
    after(got)
    return out


def _in_proj_weights(w_in, d):
    heads, q4 = d // DN_DK, 4 * d
    wba = jnp.zeros((d, 2 * LANES), w_in.dtype)
    wba = wba.at[:, :heads].set(w_in[:, q4:q4 + heads])
    wba = wba.at[:, LANES:LANES + heads].set(w_in[:, q4 + heads:q4 + 2 * heads])
    return jnp.concatenate([w_in[:, :q4], w_in[:, q4 + 2 * heads:]], axis=1), wba


def _layer_fwd(x, xb, full, cl, d, tb, comm):
    wm, wba = _in_proj_weights(full["w_in"], d)
    projm = _carry(comm.with_proj_main(), comm.after_proj_main, _matmul, xb, wm, NN, "proj_main", tn=MM_WIDE)
    full = comm.weights(full)
    wl = dict(wm=wm, wba=wba, conv=full["conv_w"], wpa=full["w_pa"], wpb=full["w_pb"], wo=full["w_o"],
              wgu=jnp.concatenate([full["w_ffn_gate"], full["w_ffn_up"]], axis=1), wd=full["w_ffn_down"])
    ba = _matmul(xb, wba, NN, "proj_gates")
    qkv = _conv_fwd(projm, wl["conv"], d, _tile(x.shape[0], 2 * tb, SUBLANES))
    (o, states, ycors), got = _dn_fwd(qkv, ba, cl["alog"], cl["dtb"], d, comm.with_dn_fwd())
    comm.after_dn_fwd(got)
    ya, yb = _gate_sgu_fwd(o, projm, cl["onw"], cl["lng"], cl["lnb"], cl["ws"], cl["bst"], d)
    pa, pb, m, h1, x1, x1b = _mix_fwd(ya, yb, projm, x, wl["wpa"], wl["wpb"], wl["wo"], cl["g1"], cl["b1"], d, tb)
    gu = _carry(comm.with_ffn_in(), comm.after_ffn_in, _matmul, x1b, wl["wgu"], NN, "ffn_in")
    act, h2, x2, x2b = _ffn_tail_fwd(gu, wl["wd"], x1, cl["g2"], cl["b2"], tb)
    saved = dict(xb=xb, projm=projm, ba=ba, qkv=qkv, o=o, states=states, ycors=ycors, ya=ya, yb=yb,
                 pa=pa, pb=pb, m=m, h1=h1, x1b=x1b, gu=gu, act=act, h2=h2)
    return x2, x2b, saved, wl


def _layer_bwd(sv, wl, cl, d, tb, comm, ln2_bwd, next_ln=None):
    g = {}
    dh2, dh2b, dg2, db2 = ln2_bwd
    g["ln2_g"], g["ln2_b"] = dg2.sum(0), db2.sum(0)
    g["wd"] = _matmul(sv["act"], dh2b, TN, "ffn_out_dw")
    dgu = _ffn_tail_bwd(dh2b, wl["wd"], sv["gu"], tb)
    g["wgu"] = _carry(comm.with_ffn_in_dw(), comm.after_ffn_in_dw, _matmul, sv["x1b"], dgu, TN, "ffn_in_dw")
    dh1, dh1b, dg1, db1 = _ffn_head_bwd(dgu, wl["wgu"], dh2, sv["h1"], cl["g1"], tb)
    g["ln1_g"], g["ln1_b"] = dg1.sum(0), db1.sum(0)
    g["wo"] = _matmul(sv["m"], dh1b, TN, "wo_dw")
    dpa, dpb, dya, dyb, dprojm = _mix_bwd(dh1b, sv["pa"], sv["pb"], sv["projm"], wl["wpa"], wl["wpb"], wl["wo"], d, tb)
    g["wpa"] = _matmul(sv["ya"], dpa, TN, "wpa_dw")
    g["wpb"] = _matmul(sv["yb"], dpb, TN, "wpb_dw")
    comm.after_branch_grads(g)
    do, dprojm, donw, dlng, dlnb, dws, dbst = _gate_sgu_bwd(
        dya, dyb, sv["o"], sv["projm"], cl["onw"], cl["lng"], cl["lnb"], cl["ws"], cl["bst"], dprojm, d)
    heads, groups = d // DN_DK, d // SGU_GROUP_DIM
    g["o_norm_w"], g["sgu_ln_g"], g["sgu_ln_b"] = donw.sum(0), dlng.sum(0), dlnb.sum(0)
    g["w_s"], g["b_s"] = dws, dbst[:, :groups].T
    (dqkv, dba, dal, ddt), got = _dn_bwd(sv["qkv"], sv["ba"], cl["alog"], cl["dtb"], do, sv["states"],
                                         sv["ycors"], d, comm.with_dn_bwd())
    comm.after_dn_bwd(got)
    g["a_log"], g["dt_bias"] = dal.sum(0)[:heads], ddt.sum(0)[:heads]
    tbc = _tile(sv["xb"].shape[0], 2 * tb, SUBLANES)
    dy, dcw = _conv_bwd_dy(sv["projm"], wl["conv"], dqkv, d, tbc)
    g["conv_w"] = dcw.sum(1)
    dprojm = _conv_bwd_dx(dy, wl["conv"], dprojm, d, tbc)
    g["wm"] = _carry(comm.with_proj_main_dw(), comm.after_proj_main_dw, _matmul, sv["xb"], dprojm, TN,
                     "proj_main_dw", tn=MM_WIDE)
    g["wba"] = _matmul(sv["xb"], dba, TN, "proj_gates_dw")
    dx = _matmul(dba, wl["wba"], NT, "proj_gates_dx", add=dh1, coef=ALPHA)
    comm.after_all_grads(g)
    if next_ln is not None:
        return _matmul(dprojm, wl["wm"], NT, "proj_main_dx", add=dx, tm=MM_TILE // 3, tk=MM_WIDE, ln=next_ln), g
    dx = _carry(comm.with_proj_main_dx(), comm.after_proj_main_dx, _matmul, dprojm, wl["wm"], NT, "proj_main_dx",
                add=dx, tk=MM_WIDE)
    return dx, g


_BRANCH = ("w_pa", "w_pb", "w_o", "w_ffn_gate", "w_ffn_up", "w_ffn_down")


def _grad_shards(g, d, keys):
    heads, q4 = d // DN_DK, 4 * d
    rows = lambda a: a.reshape(N_CHIPS, -1, a.shape[1])
    out = {}
    if "w_in" in keys:
        gm, gba, wsh = g["wm"], g["wba"], 2 * d + heads // 2
        out["w_in"] = jnp.stack([gm[:, :wsh],
                                 jnp.concatenate([gm[:, wsh:q4], gba[:, :heads]], axis=1),
                                 jnp.concatenate([gba[:, LANES:LANES + heads], gm[:, q4:q4 + wsh - heads]], axis=1),
                                 gm[:, q4 + wsh - heads:]])
    if "w_pa" in keys:
        ggu = g["wgu"]
        f = ggu.shape[1] // 2
        fs = f // N_CHIPS
        out.update({
            "w_pa": rows(g["wpa"]), "w_pb": rows(g["wpb"]), "w_o": rows(g["wo"]), "w_ffn_down": rows(g["wd"]),
            "w_ffn_gate": jnp.stack([ggu[:, s * fs:(s + 1) * fs] for s in range(N_CHIPS)]),
            "w_ffn_up": jnp.stack([ggu[:, f + s * fs:f + (s + 1) * fs] for s in range(N_CHIPS)])})
    return out


def _local_step(x, target, full0, full1_of, small_w, comm0=None):
    t, d = x.shape
    tb = _tile(t, 256, SUBLANES)
    comm0 = comm0 or _NoComm()
    consts = [_layer_consts(small_w, l, d) for l in range(DEPTH)]
    x1, x1b, sv0, w0 = _layer_fwd(x, x.astype(ACT), full0, consts[0], d, tb, comm0)
    x2, _, sv1, w1 = _layer_fwd(x1, x1b, full1_of(), consts[1], d, tb, _NoComm())
    ln2_bwd, loss_parts = _loss_ln_bwd(x2, target, sv1["h2"], consts[1]["g2"], tb)
    ln2_bwd, g1 = _layer_bwd(sv1, w1, consts[1], d, tb, _NoComm(), ln2_bwd, next_ln=(sv0["h2"], consts[0]["g2"]))
    comm0.layer1_grads = g1
    grad_x, g0 = _layer_bwd(sv0, w0, consts[0], d, tb, comm0, ln2_bwd)
    return loss_parts, grad_x, [g0, g1]


def kernel(x, w_in, conv_w, a_log, dt_bias, o_norm_w, sgu_ln_g, sgu_ln_b, w_s, b_s, w_pa, w_pb, w_o, ln1_g, ln1_b, w_ffn_gate, w_ffn_up, w_ffn_down, ln2_g, ln2_b, loss_target, m_w_in, m_conv_w, m_a_log, m_dt_bias, m_o_norm_w, m_sgu_ln_g, m_sgu_ln_b, m_w_s, m_b_s, m_w_pa, m_w_pb, m_w_o, m_ln1_g, m_ln1_b, m_w_ffn_gate, m_w_ffn_up, m_w_ffn_down, m_ln2_g, m_ln2_b, v_w_in, v_conv_w, v_a_log, v_dt_bias, v_o_norm_w, v_sgu_ln_g, v_sgu_ln_b, v_w_s, v_b_s, v_w_pa, v_w_pb, v_w_o, v_ln1_g, v_ln1_b, v_w_ffn_gate, v_w_ffn_up, v_w_ffn_down, v_ln2_g, v_ln2_b):
    names = ("w_in", "conv_w", "a_log", "dt_bias", "o_norm_w", "sgu_ln_g", "sgu_ln_b", "w_s", "b_s", "w_pa",
             "w_pb", "w_o", "ln1_g", "ln1_b", "w_ffn_gate", "w_ffn_up", "w_ffn_down", "ln2_g", "ln2_b")
    w = dict(zip(names, (w_in, conv_w, a_log, dt_bias, o_norm_w, sgu_ln_g, sgu_ln_b, w_s, b_s, w_pa, w_pb, w_o,
                         ln1_g, ln1_b, w_ffn_gate, w_ffn_up, w_ffn_down, ln2_g, ln2_b)))
    mom = dict(zip(names, (m_w_in, m_conv_w, m_a_log, m_dt_bias, m_o_norm_w, m_sgu_ln_g, m_sgu_ln_b, m_w_s, m_b_s,
                           m_w_pa, m_w_pb, m_w_o, m_ln1_g, m_ln1_b, m_w_ffn_gate, m_w_ffn_up, m_w_ffn_down,
                           m_ln2_g, m_ln2_b)))
    var = dict(zip(names, (v_w_in, v_conv_w, v_a_log, v_dt_bias, v_o_norm_w, v_sgu_ln_g, v_sgu_ln_b, v_w_s, v_b_s,
                           v_w_pa, v_w_pb, v_w_o, v_ln1_g, v_ln1_b, v_w_ffn_gate, v_w_ffn_up, v_w_ffn_down,
                           v_ln2_g, v_ln2_b)))
    chip = 2 * lax.axis_index("x") + lax.axis_index("y")
    place = jnp.stack([lax.axis_index("c"), chip]).astype(jnp.int32)

    big = [k for k, _ in _BIG]
    axis_of = dict(_BIG)
    local = {k: w[k].astype(BF16) for k in big}
    local["conv_w"] = conv_w

    def gather_plan(l, keys):
        return _all_gather_plan([local[k][l] for k in keys])

    def full_of(l, keys, gathered):
        return {k: _unshard(gt, local[k][l], chip, axis_of.get(k, 2)) for k, gt in zip(keys, gathered)}

    def pair_sums(grads_l, keys, lands):
        return [_pair_sum(place, grads_l[k], land) for k, land in zip(keys, lands)]

    def chip_sums(pairs, lands):
        return [_chip_sum(place, p[1], land, "grad_chip_sum") for p, land in zip(pairs, lands)]

    class Layer0Comm(_NoComm):
        def with_proj_main(self):
            return gather_plan(0, _BRANCH)

        def after_proj_main(self, got):
            self.rest = full_of(0, _BRANCH, got)

        def weights(self, full):
            return {**full, **self.rest}

        def with_dn_fwd(self):
            return gather_plan(1, mixer)

        def after_dn_fwd(self, got):
            self.full1 = full_of(1, mixer, got)

        def with_ffn_in(self):
            return gather_plan(1, ffn)

        def after_ffn_in(self, got):
            self.full1.update(full_of(1, ffn, got))

        def with_ffn_in_dw(self):
            self.g1 = _grad_shards(self.layer1_grads, x.shape[-1], big)
            return _sibling_exchange_plan([self.g1[k] for k in big])

        def after_ffn_in_dw(self, got):
            self.pairs1 = pair_sums(self.g1, big, got)

        def with_dn_bwd(self):
            return _chip_exchange_plan([p[0] for p in self.pairs1])

        def after_dn_bwd(self, got):
            self.red1 = chip_sums(self.pairs1, got)

        def after_branch_grads(self, g0):
            shards = _grad_shards(g0, x.shape[-1], _BRANCH)
            lands = _run_comm("grad_sibling_exchange", _sibling_exchange_plan([shards[k] for k in _BRANCH]))
            self.pairs0 = pair_sums(shards, _BRANCH, lands)

        def with_proj_main_dw(self):
            return _chip_exchange_plan([p[0] for p in self.pairs0])

        def after_proj_main_dw(self, got):
            self.red0 = chip_sums(self.pairs0, got)

        def after_all_grads(self, g0):
            g_in = _grad_shards(g0, x.shape[-1], ["w_in"])["w_in"]
            self.small_g = {k: jnp.stack([g0[k], self.layer1_grads[k]]) for k in _SMALL}
            small = _pack_small([self.small_g[k] for k in _SMALL])
            land, sland = _run_comm("grad_sibling_exchange_last", _sibling_exchange_plan([g_in], small))
            self.pair_in = _pair_sum(place, g_in, land)
            self.small_chip = _add2(small, sland)

        def with_proj_main_dx(self):
            return _join_plans(_chip_exchange_plan([self.pair_in[0]], self.small_chip),
                               _sibling_merge_plan(self.red0 + self.red1))

        def after_proj_main_dx(self, got):
            self.red_in = _chip_sum(place, self.pair_in[1], got[0], "grad_chip_sum")
            self.small_total = _chip_sum(place, self.small_chip, got[1], "grad_small_chip_sum")
            self.others = got[2:]

    comm = Layer0Comm()
    first, mixer, ffn = ["w_in", "conv_w"], ["w_in", "conv_w", "w_pa", "w_pb", "w_o"], list(_BRANCH[3:])
    full0 = full_of(0, first, _run_comm("all_gather_weights", gather_plan(0, first)))
    small_w = {k: w[k] for k in _SMALL if k != "conv_w"}
    loss_parts, grad_x, g = _local_step(x[0], loss_target[0], full0, lambda: comm.full1, small_w, comm)

    reduced = [comm.red_in] + comm.red0 + comm.red1
    others = list(_run_comm("grad_sibling_merge", _sibling_merge_plan([comm.red_in]))) + list(comm.others)
    halves = [_merge_halves(place, mine, other) for mine, other in zip(reduced, others)]
    grads = {k: jnp.stack([halves[i], halves[len(big) + i]]) for i, k in enumerate(big)}
    grads.update(zip(_SMALL, _unpack_small(comm.small_total, [comm.small_g[k] for k in _SMALL])))
    grads["conv_w"] = lax.dynamic_index_in_dim(_to_shards(grads["conv_w"], 2), chip, 1, keepdims=False)

    delta, new_m, new_v = {}, {}, {}
    for k in [k for k, _ in _BIG] + ["conv_w"]:
        delta[k], new_m[k], new_v[k] = _adamw(w[k], grads[k], mom[k], var[k])
    rep = [k for k in _SMALL if k != "conv_w"]
    pack = lambda dct: _pack_small([dct[k] for k in rep])
    packed = _adamw(pack(w), pack(grads), pack(mom), pack(var))
    for dst, src in zip((delta, new_m, new_v), packed):
        dst.update(zip(rep, _unpack_small(src, [w[k] for k in rep])))

    loss = 0.5 * lax.psum(jnp.sum(loss_parts), ("x", "y", "c")) / x.shape[-1]
    return (loss, grad_x[None], *[grads[k] for k in names], *[delta[k] for k in names],
            *[new_m[k] for k in names], *[new_v[k] for k in names])
```

```python
import math

import jax
import jax.numpy as jnp
from jax import lax
from jax.experimental import pallas as pl
from jax.experimental.pallas import tpu as pltpu

F32 = jnp.float32
BF16 = jnp.bfloat16
MXU_DTYPE = jnp.bfloat16
ACT = jnp.bfloat16
HIGHEST = lax.Precision.HIGHEST

DEPTH = 2
CHUNK = 64
DN_GROUP = 2
DN_GROUP_FWD = 4
SGU_BLOCK = 128
SGU_WINDOWS = 2
CONV_K = 4
DN_DK = 128
SGU_GROUP_DIM = 128
LN_EPS = 1e-5
RMS_EPS = 1e-6
ALPHA = (2 * DEPTH) ** 0.25
ADAM_LR, ADAM_B1, ADAM_B2, ADAM_EPS, ADAM_WD, ADAM_STEP = 0.001, 0.9, 0.999, 1e-08, 0.01, 10

LANES = 128
SUBLANES = 8
VMEM_LIMIT = 52 * 2 ** 20
N_CHIPS = 4

NN = ((1,), (0,))
NT = ((1,), (1,))
TN = ((0,), (0,))
MESH = pl.DeviceIdType.MESH
ANY = pl.BlockSpec(memory_space=pl.ANY)


def _dot(a, b, dims=NN, prec=None):
    if prec is None:
        a = a.astype(MXU_DTYPE)
        b = b.astype(MXU_DTYPE)
    return lax.dot_general(a, b, (dims, ((), ())), preferred_element_type=F32, precision=prec)


def _cparams(sem=None):
    return pltpu.CompilerParams(dimension_semantics=sem, vmem_limit_bytes=VMEM_LIMIT)


def _tile(dim, pref, unit=LANES):
    t = (min(pref, dim) // unit) * unit
    while t >= unit:
        if dim % t == 0:
            return t
        t -= unit
    return dim


def _fold8(x):
    r, n = x.shape
    return x.reshape(r // SUBLANES, SUBLANES, n).sum(axis=0)


def _sigmoid(x):
    return 1.0 / (1.0 + jnp.exp(-x))


def _gelu(x):
    return 0.5 * x * (1.0 + lax.erf(x * (2.0 ** -0.5)))


def _gelu_grad(x):
    return 0.5 * (1.0 + lax.erf(x * (2.0 ** -0.5))) + x * jnp.exp(-0.5 * x * x) * (2.0 * math.pi) ** -0.5


def _ln_hat(h):
    mu = jnp.mean(h, axis=-1, keepdims=True)
    xc = h - mu
    var = jnp.mean(xc * xc, axis=-1, keepdims=True)
    r = lax.rsqrt(var + LN_EPS)
    return xc * r, r


def _ln_bwd(dxhat, xhat, r):
    return r * (dxhat - jnp.mean(dxhat, axis=-1, keepdims=True)
                - xhat * jnp.mean(dxhat * xhat, axis=-1, keepdims=True))


MM_TILE = 1536
MM_WIDE = 2048


def _matmul(a, b, dims, name, out_dtype=F32, add=None, coef=1.0, tm=MM_TILE, tn=MM_TILE, tk=MM_TILE, carried=None,
            ln=None):
    if dims == NN:
        (m, k), n = a.shape, b.shape[1]
    elif dims == NT:
        (m, k), n = a.shape, b.shape[0]
    else:
        (k, m), n = a.shape, b.shape[1]
    tm, tn, tk = _tile(m, tm), _tile(n, tn), _tile(k, tk)
    nk = k // tk
    a_spec = pl.BlockSpec((tk, tm), lambda j, i, q: (q, i)) if dims == TN else pl.BlockSpec((tm, tk), lambda j, i, q: (i, q))
    b_spec = pl.BlockSpec((tn, tk), lambda j, i, q: (j, q)) if dims == NT else pl.BlockSpec((tk, tn), lambda j, i, q: (q, j))
    o_spec = pl.BlockSpec((tm, tn), lambda j, i, q: (i, j))
    has_add = add is not None
    if ln is not None:
        assert n == tn and has_add and carried is None
        return _matmul_ln_bwd(a, b, dims, name, add, coef, ln, a_spec, b_spec, o_spec, (m, n, tm, tn, nk))

    def body(*refs):
        a_ref, b_ref = refs[0], refs[1]
        add_ref = refs[2] if has_add else None
        o_ref, acc_ref = refs[2 + has_add], refs[3 + has_add]
        q = pl.program_id(2)
        part = _dot(a_ref[...], b_ref[...], dims)

        def finish(r):
            if has_add:
                r = r + coef * add_ref[...]
            o_ref[...] = r.astype(out_dtype)

        if nk == 1:
            finish(part)
        else:
            @pl.when(q == 0)
            def _():
                acc_ref[...] = part

            @pl.when(q > 0)
            def _():
                acc_ref[...] += part

            @pl.when(q == nk - 1)
            def _():
                finish(acc_ref[...])

    ins = [a, b] + ([add] if has_add else [])
    in_specs = [a_spec, b_spec] + ([o_spec] if has_add else [])
    grid = (n // tn, m // tm, nk)
    acc = pltpu.VMEM((tm, tn) if nk > 1 else (SUBLANES, LANES), F32)
    out = jax.ShapeDtypeStruct((m, n), out_dtype)
    if carried is None:
        return pl.pallas_call(
            body, name=name, grid=grid, in_specs=in_specs, out_specs=o_spec, out_shape=out, scratch_shapes=[acc],
            compiler_params=_cparams(("parallel", "parallel", "arbitrary")),
        )(*ins)
    res = pl.pallas_call(
        _carrying(body, len(ins), 1, 1, carried, grid), name=name + "_carrying", grid=grid,
        in_specs=in_specs + [ANY] * len(carried.inputs), out_specs=[o_spec] + [ANY] * len(carried.out_shapes),
        out_shape=[out] + carried.out_shapes, scratch_shapes=[acc] + carried.scratch(),
        compiler_params=_cparams(("arbitrary", "arbitrary", "arbitrary")),
    )(*ins, *carried.inputs)
    return res[0], res[1:]


def _matmul_ln_bwd(a, b, dims, name, add, coef, ln, a_spec, b_spec, o_spec, sizes):
    m, n, tm, tn, nk = sizes
    hres, g = ln
    row = pl.BlockSpec((1, n), lambda j, i, q: (0, 0))
    sums = pl.BlockSpec((SUBLANES, n), lambda j, i, q: (0, 0))

    def body(a_ref, b_ref, add_ref, h_ref, g_ref, dh_ref, dhb_ref, dg_ref, db_ref, acc_ref):
        i, q = pl.program_id(1), pl.program_id(2)
        part = _dot(a_ref[...], b_ref[...], dims)

        @pl.when(jnp.logical_and(i == 0, q == 0))
        def _():
            dg_ref[...] = jnp.zeros_like(dg_ref)
            db_ref[...] = jnp.zeros_like(db_ref)

        @pl.when(q == 0)
        def _():
            acc_ref[...] = part

        @pl.when(q > 0)
        def _():
            acc_ref[...] += part

        @pl.when(q == nk - 1)
        def _():
            dy_v = acc_ref[...] + coef * add_ref[...]
            xhat, r = _ln_hat(h_ref[...])
            dh = _ln_bwd(dy_v * g_ref[...], xhat, r)
            dh_ref[...] = dh
            dhb_ref[...] = dh.astype(dhb_ref.dtype)
            dg_ref[...] += _fold8(dy_v * xhat)
            db_ref[...] += _fold8(dy_v)

    return pl.pallas_call(
        body, name=name + "_ln_bwd", grid=(1, m // tm, nk),
        in_specs=[a_spec, b_spec, o_spec, o_spec, row], out_specs=[o_spec, o_spec, sums, sums],
        out_shape=[jax.ShapeDtypeStruct((m, n), F32), jax.ShapeDtypeStruct((m, n), ACT),
                   jax.ShapeDtypeStruct((SUBLANES, n), F32), jax.ShapeDtypeStruct((SUBLANES, n), F32)],
        scratch_shapes=[pltpu.VMEM((tm, tn), F32)],
        compiler_params=_cparams(("arbitrary", "arbitrary", "arbitrary")),
    )(a, b, add, hres, g)


def _conv_taps(cur_ref, halo_ref, first):
    x = cur_ref[...]
    tb = x.shape[0]
    halo = jnp.where(first, 0.0, halo_ref[...])
    xc = jnp.concatenate([halo, x], axis=0)
    return [x] + [pltpu.roll(xc, s, 0)[SUBLANES:SUBLANES + tb] for s in range(1, CONV_K)]


def _conv_fwd(projm, conv_w, d, tb):
    t = projm.shape[0]
    heads = d // DN_DK
    hb = tb // SUBLANES

    def body(cur_ref, halo_ref, w_ref, o_ref):
        i, j = pl.program_id(0), pl.program_id(1)
        taps = _conv_taps(cur_ref, halo_ref, i == 0)
        y = taps[0] * w_ref[CONV_K - 1:CONV_K, :]
        for s in range(1, CONV_K):
            y = y + taps[s] * w_ref[CONV_K - 1 - s:CONV_K - s, :]
        act = y * _sigmoid(y)
        scale = jnp.where(j == 0, DN_DK ** -0.5, 1.0)
        for h in range(heads):
            seg = act[:, h * DN_DK:(h + 1) * DN_DK]
            r = lax.rsqrt(jnp.sum(seg * seg, axis=1, keepdims=True) + RMS_EPS) * scale
            o_ref[:, h * DN_DK:(h + 1) * DN_DK] = seg * jnp.where(j < 2, r, 1.0)

    blk = pl.BlockSpec((tb, d), lambda i, j: (i, j))
    return pl.pallas_call(
        body, name="conv_fwd", grid=(t // tb, 3),
        in_specs=[blk,
                  pl.BlockSpec((SUBLANES, d), lambda i, j: (jnp.maximum(i * hb - 1, 0), j)),
                  pl.BlockSpec((CONV_K, d), lambda i, j: (0, j))],
        out_specs=blk,
        out_shape=jax.ShapeDtypeStruct((t, 3 * d), F32),
        compiler_params=_cparams(("parallel", "parallel")),
    )(projm, projm, conv_w)


def _conv_bwd_dy(projm, conv_w, dqkv, d, tb):
    t = projm.shape[0]
    heads = d // DN_DK
    hb = tb // SUBLANES

    def body(cur_ref, halo_ref, w_ref, dout_ref, dy_ref, dw_ref):
        j, i = pl.program_id(0), pl.program_id(1)
        taps = _conv_taps(cur_ref, halo_ref, i == 0)
        y = taps[0] * w_ref[CONV_K - 1:CONV_K, :]
        for s in range(1, CONV_K):
            y = y + taps[s] * w_ref[CONV_K - 1 - s:CONV_K - s, :]
        sg = _sigmoid(y)
        act = y * sg
        dact = sg * (1.0 + y * (1.0 - sg))
        scale = jnp.where(j == 0, DN_DK ** -0.5, 1.0)
        for h in range(heads):
            cols = slice(h * DN_DK, (h + 1) * DN_DK)
            seg = act[:, cols]
            r = lax.rsqrt(jnp.sum(seg * seg, axis=1, keepdims=True) + RMS_EPS)
            nrm = seg * r
            dout = dout_ref[:, cols]
            ds = jnp.where(j < 2, (r * scale) * (dout - nrm * jnp.sum(dout * nrm, axis=1, keepdims=True)), dout)
            dy_ref[:, cols] = ds * dact[:, cols]
        dy = dy_ref[...]

        @pl.when(i == 0)
        def _():
            dw_ref[...] = jnp.zeros_like(dw_ref)

        for s in range(CONV_K):
            dw_ref[CONV_K - 1 - s] += _fold8(dy * taps[s])

    return pl.pallas_call(
        body, name="conv_bwd_dy", grid=(3, t // tb),
        in_specs=[pl.BlockSpec((tb, d), lambda j, i: (i, j)),
                  pl.BlockSpec((SUBLANES, d), lambda j, i: (jnp.maximum(i * hb - 1, 0), j)),
                  pl.BlockSpec((CONV_K, d), lambda j, i: (0, j)),
                  pl.BlockSpec((tb, d), lambda j, i: (i, j))],
        out_specs=[pl.BlockSpec((tb, d), lambda j, i: (i, j)),
                   pl.BlockSpec((CONV_K, SUBLANES, d), lambda j, i: (0, 0, j))],
        out_shape=[jax.ShapeDtypeStruct((t, 3 * d), F32),
                   jax.ShapeDtypeStruct((CONV_K, SUBLANES, 3 * d), F32)],
        compiler_params=_cparams(("parallel", "arbitrary")),
    )(projm, projm, conv_w, dqkv)


def _conv_bwd_dx(dy, conv_w, dprojm, d, tb):
    t = dy.shape[0]
    hb = tb // SUBLANES
    last = t // tb - 1

    def body(cur_ref, halo_ref, w_ref, alias_ref, o_ref):
        i = pl.program_id(0)
        cur = cur_ref[...]
        halo = jnp.where(i == last, 0.0, halo_ref[...])
        dc = jnp.concatenate([cur, halo], axis=0)
        acc = cur * w_ref[CONV_K - 1:CONV_K, :]
        for s in range(1, CONV_K):
            acc = acc + pltpu.roll(dc, tb + SUBLANES - s, 0)[:tb] * w_ref[CONV_K - 1 - s:CONV_K - s, :]
        o_ref[...] = acc.astype(o_ref.dtype)

    return pl.pallas_call(
        body, name="conv_bwd_dx", grid=(t // tb, 3),
        in_specs=[pl.BlockSpec((tb, d), lambda i, j: (i, j)),
                  pl.BlockSpec((SUBLANES, d), lambda i, j: (jnp.minimum((i + 1) * hb, t // SUBLANES - 1), j)),
                  pl.BlockSpec((CONV_K, d), lambda i, j: (0, j)),
                  ANY],
        out_specs=pl.BlockSpec((tb, d), lambda i, j: (i, j)),
        out_shape=jax.ShapeDtypeStruct(dprojm.shape, dprojm.dtype),
        input_output_aliases={3: 0},
        compiler_params=_cparams(("parallel", "parallel")),
    )(dy, dy, conv_w, dprojm)


def _beta_g(ba, alog, dtb):
    beta = _sigmoid(ba[:, :LANES])
    xa = ba[:, LANES:] + dtb
    softplus = jnp.maximum(xa, 0.0) + jnp.log(1.0 + jnp.exp(-jnp.abs(xa)))
    ea = jnp.exp(alog)
    return beta, -ea * softplus, ea, _sigmoid(xa)


def _inv_corrections(mats):
    ys = [-a for a in mats]
    ps = [_dot(a, a) for a in mats]
    steps = int(math.log2(CHUNK)) - 1
    for it in range(steps):
        ys = [y + p + _dot(y, p) for y, p in zip(ys, ps)]
        if it < steps - 1:
            ps = [_dot(p, p) for p in ps]
    return ys


def _chunk_masks():
    row = lax.broadcasted_iota(jnp.int32, (CHUNK, CHUNK), 0)
    col = lax.broadcasted_iota(jnp.int32, (CHUNK, CHUNK), 1)
    return row >= col, row > col, row <= col


def _col_of(mat, lane_idx, h):
    return jnp.sum(jnp.where(lane_idx == h, mat, 0.0), axis=1, keepdims=True)


def _row_of(mat, sub_idx, h):
    return jnp.sum(jnp.where(sub_idx == h, mat, 0.0), axis=0, keepdims=True)


def _carrying(compute, n_in, n_out, n_scratch, carried, grid):
    if carried is None:
        return compute
    ci, co = len(carried.inputs), len(carried.out_shapes)

    def body(*refs):
        ins, c_in = refs[:n_in], refs[n_in:n_in + ci]
        outs, c_out = refs[n_in + ci:n_in + ci + n_out], refs[n_in + ci + n_out:n_in + ci + n_out + co]
        scratch = refs[n_in + ci + n_out + co:]
        start, finish = carried.copies(c_in, c_out, scratch[n_scratch], scratch[n_scratch + 1])
        first, last = True, True
        for axis, steps in enumerate(grid):
            first = jnp.logical_and(first, pl.program_id(axis) == 0)
            last = jnp.logical_and(last, pl.program_id(axis) == steps - 1)

        @pl.when(first)
        def _():
            start()

        compute(*ins, *outs, *scratch[:n_scratch])

        @pl.when(last)
        def _():
            finish()

    return body


def _dn_fwd(qkv, ba, alog, dtb, d, carried=None):
    t = qkv.shape[0]
    heads = d // DN_DK
    n_chunks = t // CHUNK
    grp = DN_GROUP_FWD if n_chunks % DN_GROUP_FWD == 0 else 1
    span = grp * CHUNK
    extra = carried or _Carried([], [], 0, None)

    def compute(qkv_ref, ba_ref, al_ref, dt_ref, o_ref, s_ref, y_ref, state):
        @pl.when(pl.program_id(0) == 0)
        def _():
            state[...] = jnp.zeros_like(state)

        tril, strict, _ = _chunk_masks()
        beta, g, _, _ = _beta_g(ba_ref[...], al_ref[...], dt_ref[...])
        lane = lax.broadcasted_iota(jnp.int32, (CHUNK, LANES), 1)
        sub = lax.broadcasted_iota(jnp.int32, (LANES, CHUNK), 0)
        rowc = lax.broadcasted_iota(jnp.int32, (CHUNK, 1), 0)
        hs = range(heads)
        units = [(c, h) for c in range(grp) for h in hs]
        un = range(len(units))
        rows = lambda c: slice(c * CHUNK, (c + 1) * CHUNK)
        gc = [_dot(jnp.where(tril, 1.0, 0.0), g[rows(c)], NN, HIGHEST) for c in range(grp)]
        gct = [m.T for m in gc]
        q = [qkv_ref[rows(c), h * DN_DK:(h + 1) * DN_DK] for c, h in units]
        k = [qkv_ref[rows(c), d + h * DN_DK:d + (h + 1) * DN_DK] for c, h in units]
        v = [qkv_ref[rows(c), 2 * d + h * DN_DK:2 * d + (h + 1) * DN_DK] for c, h in units]
        gch = [_col_of(gc[c], lane, h) for c, h in units]
        bh = [_col_of(beta[rows(c)], lane, h) for c, h in units]
        dec = [jnp.where(tril, jnp.exp(gch[n] - _row_of(gct[c], sub, h)), 0.0) for n, (c, h) in enumerate(units)]
        egc = [jnp.exp(gch[n]) for n in un]
        gl = [jnp.sum(jnp.where(rowc == CHUNK - 1, gch[n], 0.0), axis=0, keepdims=True) for n in un]
        kb = [k[n] * bh[n] for n in un]
        a = [jnp.where(strict, _dot(kb[n], k[n], NT) * dec[n], 0.0) for n in un]
        p = [_dot(q[n], k[n], NT) * dec[n] for n in un]
        ycor = _inv_corrections(a)
        rhs = [jnp.concatenate([v[n] * bh[n], kb[n] * egc[n]], axis=1) for n in un]
        sol = [rhs[n] + _dot(ycor[n], rhs[n]) for n in un]
        qg = [q[n] * egc[n] for n in un]
        kd = [k[n] * jnp.exp(gl[n] - gch[n]) for n in un]
        egl = [jnp.exp(gl[n]) for n in un]
        s_cur, s_in, o = [state[h] for h in hs], [], []
        for c in range(grp):
            ns = [c * heads + h for h in hs]
            vn = [sol[n][:, :DN_DK] - _dot(sol[n][:, DN_DK:], s_cur[h]) for h, n in enumerate(ns)]
            o += [_dot(qg[n], s_cur[h]) + _dot(p[n], vn[h]) for h, n in enumerate(ns)]
            s_in += s_cur
            s_cur = [s_cur[h] * egl[n] + _dot(kd[n], vn[h], TN) for h, n in enumerate(ns)]
        for n, (c, h) in enumerate(units):
            o_ref[rows(c), h * DN_DK:(h + 1) * DN_DK] = o[n]
            s_ref[c, h] = s_in[n]
            y_ref[h, rows(c), :] = ycor[n]
        for h in hs:
            state[h] = s_cur[h]

    res = pl.pallas_call(
        _carrying(compute, 4, 3, 1, carried, (n_chunks // grp,)),
        name="dn_fwd_carrying" if carried else "dn_fwd", grid=(n_chunks // grp,),
        in_specs=[pl.BlockSpec((span, 3 * d), lambda i: (i, 0)),
                  pl.BlockSpec((span, 2 * LANES), lambda i: (i, 0)),
                  pl.BlockSpec((1, LANES), lambda i: (0, 0)),
                  pl.BlockSpec((1, LANES), lambda i: (0, 0))] + [ANY] * len(extra.inputs),
        out_specs=[pl.BlockSpec((span, d), lambda i: (i, 0)),
                   pl.BlockSpec((grp, heads, DN_DK, DN_DK), lambda i: (i, 0, 0, 0)),
                   pl.BlockSpec((heads, span, CHUNK), lambda i: (0, i, 0))] + [ANY] * len(extra.out_shapes),
        out_shape=[jax.ShapeDtypeStruct((t, d), F32),
                   jax.ShapeDtypeStruct((n_chunks, heads, DN_DK, DN_DK), F32),
                   jax.ShapeDtypeStruct((heads, t, CHUNK), F32)] + extra.out_shapes,
        scratch_shapes=[pltpu.VMEM((heads, DN_DK, DN_DK), F32)] + (extra.scratch() if carried else []),
        compiler_params=_cparams(("arbitrary",)),
    )(qkv, ba, alog, dtb, *extra.inputs)
    return res[:3], res[3:]


def _dn_bwd(qkv, ba, alog, dtb, dout, states, ycors, d, carried=None):
    t = qkv.shape[0]
    heads = d // DN_DK
    n_chunks = t // CHUNK
    grp = DN_GROUP if n_chunks % DN_GROUP == 0 else 1
    span = grp * CHUNK
    rev = lambda i: n_chunks // grp - 1 - i
    extra = carried or _Carried([], [], 0, None)

    def compute(qkv_ref, ba_ref, al_ref, dt_ref, do_ref, s_ref, y_ref,
                dqkv_ref, dba_ref, dal_ref, ddt_ref, dstate):
        @pl.when(pl.program_id(0) == 0)
        def _():
            dstate[...] = jnp.zeros_like(dstate)
            dal_ref[...] = jnp.zeros_like(dal_ref)
            ddt_ref[...] = jnp.zeros_like(ddt_ref)

        tril, strict, triu = _chunk_masks()
        beta, g, ea, sig_a = _beta_g(ba_ref[...], al_ref[...], dt_ref[...])
        lane = lax.broadcasted_iota(jnp.int32, (CHUNK, LANES), 1)
        sub = lax.broadcasted_iota(jnp.int32, (LANES, CHUNK), 0)
        rowc = lax.broadcasted_iota(jnp.int32, (CHUNK, 1), 0)
        hs = range(heads)
        units = [(c, h) for c in range(grp) for h in hs]
        un = range(len(units))
        rows = lambda c: slice(c * CHUNK, (c + 1) * CHUNK)
        rsum = lambda x_: jnp.sum(x_, axis=1, keepdims=True)
        gc = [_dot(jnp.where(tril, 1.0, 0.0), g[rows(c)], NN, HIGHEST) for c in range(grp)]
        gct = [m.T for m in gc]
        q = [qkv_ref[rows(c), h * DN_DK:(h + 1) * DN_DK] for c, h in units]
        k = [qkv_ref[rows(c), d + h * DN_DK:d + (h + 1) * DN_DK] for c, h in units]
        v = [qkv_ref[rows(c), 2 * d + h * DN_DK:2 * d + (h + 1) * DN_DK] for c, h in units]
        dout_h = [do_ref[rows(c), h * DN_DK:(h + 1) * DN_DK] for c, h in units]
        s0 = [s_ref[c, h] for c, h in units]
        ycor = [y_ref[h, rows(c), :] for c, h in units]
        gch = [_col_of(gc[c], lane, h) for c, h in units]
        bh = [_col_of(beta[rows(c)], lane, h) for c, h in units]
        dec = [jnp.where(tril, jnp.exp(gch[n] - _row_of(gct[c], sub, h)), 0.0) for n, (c, h) in enumerate(units)]
        egc = [jnp.exp(gch[n]) for n in un]
        gl = [jnp.sum(jnp.where(rowc == CHUNK - 1, gch[n], 0.0), axis=0, keepdims=True) for n in un]
        egl = [jnp.exp(gl[n]) for n in un]
        ekd = [jnp.exp(gl[n] - gch[n]) for n in un]
        kb = [k[n] * bh[n] for n in un]
        kd = [k[n] * ekd[n] for n in un]
        qg = [q[n] * egc[n] for n in un]
        kbg = [kb[n] * egc[n] for n in un]
        a = [jnp.where(strict, _dot(kb[n], k[n], NT) * dec[n], 0.0) for n in un]
        p = [_dot(q[n], k[n], NT) * dec[n] for n in un]
        rhs = [jnp.concatenate([v[n] * bh[n], kbg[n]], axis=1) for n in un]
        sol = [rhs[n] + _dot(ycor[n], rhs[n]) for n in un]
        w = [sol[n][:, DN_DK:] for n in un]
        vn = [sol[n][:, :DN_DK] - _dot(w[n], s0[n]) for n in un]
        dqg = [_dot(dout_h[n], s0[n], NT) for n in un]
        dp = [jnp.where(tril, _dot(dout_h[n], vn[n], NT), 0.0) for n in un]
        pdo = [_dot(p[n], dout_h[n], TN) for n in un]
        qdo = [_dot(qg[n], dout_h[n], TN) for n in un]
        ds_cur = [dstate[h] for h in hs]
        dsn, dvn = [None] * len(units), [None] * len(units)
        for c in reversed(range(grp)):
            for h in hs:
                dsn[c * heads + h] = ds_cur[h]
            for h in hs:
                n = c * heads + h
                dvn[n] = pdo[n] + _dot(kd[n], ds_cur[h])
            ds_cur = [qdo[c * heads + h] + egl[c * heads + h] * ds_cur[h]
                      - _dot(w[c * heads + h], dvn[c * heads + h], TN) for h in hs]
        dkd = [_dot(vn[n], dsn[n], NT) for n in un]
        dw = [-_dot(dvn[n], s0[n], NT) for n in un]
        dgl = [jnp.sum(rsum(dsn[n] * s0[n]), axis=0, keepdims=True) * egl[n] for n in un]
        dsol = [jnp.concatenate([dvn[n], dw[n]], axis=1) for n in un]
        drhs = [dsol[n] + _dot(ycor[n], dsol[n], TN) for n in un]
        dvb = [drhs[n][:, :DN_DK] for n in un]
        dkbg = [drhs[n][:, DN_DK:] for n in un]
        da = [jnp.where(strict, -_dot(drhs[n], sol[n], NT), 0.0) for n in un]
        dma = [da[n] * dec[n] for n in un]
        dmp = [dp[n] * dec[n] for n in un]
        dkb = [_dot(dma[n], k[n]) + dkbg[n] * egc[n] for n in un]
        dq = [_dot(dmp[n], k[n]) + dqg[n] * egc[n] for n in un]
        dk = [_dot(dma[n], kb[n], TN) + _dot(dmp[n], q[n], TN) + dkd[n] * ekd[n] + dkb[n] * bh[n] for n in un]
        e = [da[n] * a[n] + dp[n] * p[n] for n in un]
        colsum = [jnp.sum(e[n], axis=0, keepdims=True) for n in un]
        tkd = [rsum(dkd[n] * kd[n]) for n in un]
        for n, (c, h) in enumerate(units):
            dqkv_ref[rows(c), h * DN_DK:(h + 1) * DN_DK] = dq[n]
            dqkv_ref[rows(c), d + h * DN_DK:d + (h + 1) * DN_DK] = dk[n]
            dqkv_ref[rows(c), 2 * d + h * DN_DK:2 * d + (h + 1) * DN_DK] = dvb[n] * bh[n]
        for h in hs:
            dstate[h] = ds_cur[h]
        valid = lane < heads
        dal_acc = jnp.zeros((SUBLANES, LANES), F32)
        ddt_acc = jnp.zeros((SUBLANES, LANES), F32)
        for c in range(grp):
            dgc_all = jnp.zeros((CHUNK, LANES), F32)
            dbeta_all = jnp.zeros((CHUNK, LANES), F32)
            colsums = jnp.zeros((LANES, CHUNK), F32)
            for h in hs:
                n = c * heads + h
                dgc = rsum(e[n]) + rsum(dqg[n] * qg[n]) - tkd[n] + rsum(dkbg[n] * kbg[n])
                dgc = dgc + jnp.where(rowc == CHUNK - 1, dgl[n] + jnp.sum(tkd[n], axis=0, keepdims=True), 0.0)
                dgc_all = dgc_all + jnp.where(lane == h, dgc, 0.0)
                colsums = colsums + jnp.where(sub == h, colsum[n], 0.0)
                dbeta_all = dbeta_all + jnp.where(lane == h, rsum(dkb[n] * k[n]) + rsum(dvb[n] * v[n]), 0.0)
            dg = _dot(jnp.where(triu, 1.0, 0.0), dgc_all - colsums.T, NN, HIGHEST)
            beta_c = beta[rows(c)]
            dbl = jnp.where(valid, dbeta_all * beta_c * (1.0 - beta_c), 0.0)
            dal = jnp.where(valid, -dg * ea * sig_a[rows(c)], 0.0)
            dba_ref[rows(c), :LANES] = dbl.astype(dba_ref.dtype)
            dba_ref[rows(c), LANES:] = dal.astype(dba_ref.dtype)
            dal_acc = dal_acc + _fold8(jnp.where(valid, dg * g[rows(c)], 0.0))
            ddt_acc = ddt_acc + _fold8(dal)
        dal_ref[...] += dal_acc
        ddt_ref[...] += ddt_acc

    res = pl.pallas_call(
        _carrying(compute, 7, 4, 1, carried, (n_chunks // grp,)),
        name="dn_bwd_carrying" if carried else "dn_bwd", grid=(n_chunks // grp,),
        in_specs=[pl.BlockSpec((span, 3 * d), lambda i: (rev(i), 0)),
                  pl.BlockSpec((span, 2 * LANES), lambda i: (rev(i), 0)),
                  pl.BlockSpec((1, LANES), lambda i: (0, 0)),
                  pl.BlockSpec((1, LANES), lambda i: (0, 0)),
                  pl.BlockSpec((span, d), lambda i: (rev(i), 0)),
                  pl.BlockSpec((grp, heads, DN_DK, DN_DK), lambda i: (rev(i), 0, 0, 0)),
                  pl.BlockSpec((heads, span, CHUNK), lambda i: (0, rev(i), 0))] + [ANY] * len(extra.inputs),
        out_specs=[pl.BlockSpec((span, 3 * d), lambda i: (rev(i), 0)),
                   pl.BlockSpec((span, 2 * LANES), lambda i: (rev(i), 0)),
                   pl.BlockSpec((SUBLANES, LANES), lambda i: (0, 0)),
                   pl.BlockSpec((SUBLANES, LANES), lambda i: (0, 0))] + [ANY] * len(extra.out_shapes),
        out_shape=[jax.ShapeDtypeStruct((t, 3 * d), F32),
                   jax.ShapeDtypeStruct((t, 2 * LANES), ACT),
                   jax.ShapeDtypeStruct((SUBLANES, LANES), F32),
                   jax.ShapeDtypeStruct((SUBLANES, LANES), F32)] + extra.out_shapes,
        scratch_shapes=[pltpu.VMEM((heads, DN_DK, DN_DK), F32)] + (extra.scratch() if carried else []),
        compiler_params=_cparams(("arbitrary",)),
    )(qkv, ba, alog, dtb, dout, states, ycors, *extra.inputs)
    return res[:4], res[4:]


def _sgu_mask():
    row = lax.broadcasted_iota(jnp.int32, (SGU_BLOCK, SGU_BLOCK), 0)
    col = lax.broadcasted_iota(jnp.int32, (SGU_BLOCK, SGU_BLOCK), 1)
    sh = int(math.log2(CHUNK))
    return lax.shift_right_logical(row, sh) >= lax.shift_right_logical(col, sh)


def _gate_sgu_fwd(o, projm, onw, lng, lnb, ws, bst, d):
    t = o.shape[0]
    heads, groups = d // DN_DK, d // SGU_GROUP_DIM
    tb = _tile(t, SGU_WINDOWS * SGU_BLOCK, SGU_BLOCK)
    row_spec = pl.BlockSpec((1, d), lambda i: (0, 0))

    def body(o_ref, z_ref, u_ref, v_ref, onw_ref, lng_ref, lnb_ref, ws_ref, bst_ref, ya_ref, yb_ref):
        for h in range(heads):
            cols = slice(h * DN_DK, (h + 1) * DN_DK)
            oh, zh = o_ref[:, cols], z_ref[:, cols]
            r = lax.rsqrt(jnp.mean(oh * oh, axis=1, keepdims=True) + RMS_EPS)
            ya_ref[:, cols] = (oh * r * onw_ref[:, cols] * (zh * _sigmoid(zh))).astype(ya_ref.dtype)
        xhat, _ = _ln_hat(_gelu(v_ref[...]))
        vgn = xhat * lng_ref[...] + lnb_ref[...]
        mask = _sgu_mask()
        lane = lax.broadcasted_iota(jnp.int32, (SGU_BLOCK, LANES), 1)
        bst_v = bst_ref[...]
        for gi in range(groups):
            cols = slice(gi * SGU_GROUP_DIM, (gi + 1) * SGU_GROUP_DIM)
            wsg = jnp.where(mask, ws_ref[gi], 0.0)
            bias = _col_of(bst_v, lane, gi)
            for win in range(tb // SGU_BLOCK):
                rows = slice(win * SGU_BLOCK, (win + 1) * SGU_BLOCK)
                sp = _dot(wsg, vgn[rows, cols]) + bias
                yb_ref[rows, cols] = (_gelu(u_ref[rows, cols]) * sp).astype(yb_ref.dtype)

    return pl.pallas_call(
        body, name="gate_sgu_fwd", grid=(t // tb,),
        in_specs=[pl.BlockSpec((tb, d), lambda i: (i, 0)),
                  pl.BlockSpec((tb, d), lambda i: (i, 3)),
                  pl.BlockSpec((tb, d), lambda i: (i, 4)),
                  pl.BlockSpec((tb, d), lambda i: (i, 5)),
                  row_spec, row_spec, row_spec,
                  pl.BlockSpec((groups, SGU_BLOCK, SGU_BLOCK), lambda i: (0, 0, 0)),
                  pl.BlockSpec((SGU_BLOCK, LANES), lambda i: (0, 0))],
        out_specs=[pl.BlockSpec((tb, d), lambda i: (i, 0)), pl.BlockSpec((tb, d), lambda i: (i, 0))],
        out_shape=[jax.ShapeDtypeStruct((t, d), ACT), jax.ShapeDtypeStruct((t, d), ACT)],
        compiler_params=_cparams(("parallel",)),
    )(o, projm, projm, projm, onw, lng, lnb, ws, bst)


def _gate_sgu_bwd(dya, dyb, o, projm, onw, lng, lnb, ws, bst, dprojm, d):
    t = o.shape[0]
    heads, groups = d // DN_DK, d // SGU_GROUP_DIM
    tb = _tile(t, SGU_WINDOWS * SGU_BLOCK, SGU_BLOCK)
    row_spec = pl.BlockSpec((1, d), lambda i: (0, 0))
    acc_row = pl.BlockSpec((SUBLANES, d), lambda i: (0, 0))

    def body(dya_ref, dyb_ref, o_ref, z_ref, u_ref, v_ref, onw_ref, lng_ref, lnb_ref, ws_ref, bst_ref, alias_ref,
             do_ref, dp_ref, donw_ref, dlng_ref, dlnb_ref, dws_ref, dbst_ref):
        @pl.when(pl.program_id(0) == 0)
        def _():
            for r_ in (donw_ref, dlng_ref, dlnb_ref, dws_ref, dbst_ref):
                r_[...] = jnp.zeros_like(r_)

        donw = jnp.zeros((SUBLANES, DN_DK), F32)
        for h in range(heads):
            cols = slice(h * DN_DK, (h + 1) * DN_DK)
            oh, zh, dyah, wh = o_ref[:, cols], z_ref[:, cols], dya_ref[:, cols], onw_ref[:, cols]
            r = lax.rsqrt(jnp.mean(oh * oh, axis=1, keepdims=True) + RMS_EPS)
            on = oh * r
            sz = _sigmoid(zh)
            silu_z = zh * sz
            don = dyah * wh * silu_z
            dp_ref[:, cols] = (dyah * on * wh * (sz * (1.0 + zh * (1.0 - sz)))).astype(dp_ref.dtype)
            donw = donw + _fold8(dyah * on * silu_z)
            do_ref[:, cols] = r * (don - on * jnp.mean(don * on, axis=1, keepdims=True))
        donw_ref[...] += donw

        vgp, up = v_ref[...], u_ref[...]
        xhat, rstd = _ln_hat(_gelu(vgp))
        lng_v = lng_ref[...]
        vgn = xhat * lng_v + lnb_ref[...]
        ua = _gelu(up)
        mask = _sgu_mask()
        lane = lax.broadcasted_iota(jnp.int32, (SGU_BLOCK, LANES), 1)
        bst_v = bst_ref[...]
        dbst = jnp.zeros((SGU_BLOCK, LANES), F32)
        dvgn_parts, dua_parts = [], []
        for gi in range(groups):
            cols = slice(gi * SGU_GROUP_DIM, (gi + 1) * SGU_GROUP_DIM)
            wsg = jnp.where(mask, ws_ref[gi], 0.0)
            bias = _col_of(bst_v, lane, gi)
            dws = jnp.zeros((SGU_BLOCK, SGU_BLOCK), F32)
            dvgn_g, dua_g = [], []
            for win in range(tb // SGU_BLOCK):
                rows = slice(win * SGU_BLOCK, (win + 1) * SGU_BLOCK)
                vg_g, dyb_g = vgn[rows, cols], dyb_ref[rows, cols]
                sp = _dot(wsg, vg_g) + bias
                dsp = dyb_g * ua[rows, cols]
                dua_g.append(dyb_g * sp)
                dws = dws + _dot(dsp, vg_g, NT)
                dbst = dbst + jnp.where(lane == gi, jnp.sum(dsp, axis=1, keepdims=True), 0.0)
                dvgn_g.append(_dot(wsg, dsp, TN))
            dws_ref[gi] += jnp.where(mask, dws, 0.0)
            dvgn_parts.append(jnp.concatenate(dvgn_g, axis=0))
            dua_parts.append(jnp.concatenate(dua_g, axis=0))
        dbst_ref[...] += dbst
        dvgn = jnp.concatenate(dvgn_parts, axis=1)
        dua = jnp.concatenate(dua_parts, axis=1)
        dlng_ref[...] += _fold8(dvgn * xhat)
        dlnb_ref[...] += _fold8(dvgn)
        dvga = _ln_bwd(dvgn * lng_v, xhat, rstd)
        dp_ref[:, d:2 * d] = (dua * _gelu_grad(up)).astype(dp_ref.dtype)
        dp_ref[:, 2 * d:] = (dvga * _gelu_grad(vgp)).astype(dp_ref.dtype)

    return pl.pallas_call(
        body, name="gate_sgu_bwd", grid=(t // tb,),
        in_specs=[pl.BlockSpec((tb, d), lambda i: (i, 0)),
                  pl.BlockSpec((tb, d), lambda i: (i, 0)),
                  pl.BlockSpec((tb, d), lambda i: (i, 0)),
                  pl.BlockSpec((tb, d), lambda i: (i, 3)),
                  pl.BlockSpec((tb, d), lambda i: (i, 4)),
                  pl.BlockSpec((tb, d), lambda i: (i, 5)),
                  row_spec, row_spec, row_spec,
                  pl.BlockSpec((groups, SGU_BLOCK, SGU_BLOCK), lambda i: (0, 0, 0)),
                  pl.BlockSpec((SGU_BLOCK, LANES), lambda i: (0, 0)),
                  ANY],
        out_specs=[pl.BlockSpec((tb, d), lambda i: (i, 0)),
                   pl.BlockSpec((tb, 3 * d), lambda i: (i, 1)),
                   pl.BlockSpec((SUBLANES, DN_DK), lambda i: (0, 0)),
                   acc_row, acc_row,
                   pl.BlockSpec((groups, SGU_BLOCK, SGU_BLOCK), lambda i: (0, 0, 0)),
                   pl.BlockSpec((SGU_BLOCK, LANES), lambda i: (0, 0))],
        out_shape=[jax.ShapeDtypeStruct((t, d), F32),
                   jax.ShapeDtypeStruct(dprojm.shape, dprojm.dtype),
                   jax.ShapeDtypeStruct((SUBLANES, DN_DK), F32),
                   jax.ShapeDtypeStruct((SUBLANES, d), F32),
                   jax.ShapeDtypeStruct((SUBLANES, d), F32),
                   jax.ShapeDtypeStruct((groups, SGU_BLOCK, SGU_BLOCK), F32),
                   jax.ShapeDtypeStruct((SGU_BLOCK, LANES), F32)],
        input_output_aliases={11: 1},
        compiler_params=_cparams(("arbitrary",)),
    )(dya, dyb, o, projm, projm, projm, onw, lng, lnb, ws, bst, dprojm)


def _mix_fwd(ya, yb, projm, x, wpa, wpb, wo, g1, b1, d, tb):
    t = x.shape[0]
    blk = pl.BlockSpec((tb, d), lambda i: (i, 0))
    wspec = pl.BlockSpec((d, d), lambda i: (0, 0))
    row_spec = pl.BlockSpec((1, d), lambda i: (0, 0))

    def body(ya_ref, yb_ref, ga_ref, gb_ref, x_ref, wpa_ref, wpb_ref, wo_ref, g_ref, b_ref,
             pa_ref, pb_ref, m_ref, h_ref, x1_ref, x1b_ref):
        pa = _dot(ya_ref[...], wpa_ref[...])
        pb = _dot(yb_ref[...], wpb_ref[...])
        m = _sigmoid(ga_ref[...]) * pa + _sigmoid(gb_ref[...]) * pb
        hres = ALPHA * x_ref[...] + _dot(m, wo_ref[...])
        xhat, _ = _ln_hat(hres)
        x1 = xhat * g_ref[...] + b_ref[...]
        pa_ref[...] = pa
        pb_ref[...] = pb
        m_ref[...] = m.astype(m_ref.dtype)
        h_ref[...] = hres
        x1_ref[...] = x1
        x1b_ref[...] = x1.astype(x1b_ref.dtype)

    f32_out = jax.ShapeDtypeStruct((t, d), F32)
    bf_out = jax.ShapeDtypeStruct((t, d), ACT)
    return pl.pallas_call(
        body, name="mix_fwd", grid=(t // tb,),
        in_specs=[blk, blk, pl.BlockSpec((tb, d), lambda i: (i, 6)), pl.BlockSpec((tb, d), lambda i: (i, 7)),
                  blk, wspec, wspec, wspec, row_spec, row_spec],
        out_specs=[blk] * 6,
        out_shape=[f32_out, f32_out, bf_out, f32_out, f32_out, bf_out],
        compiler_params=_cparams(("parallel",)),
    )(ya, yb, projm, projm, x, wpa, wpb, wo, g1, b1)


def _mix_bwd(dmix, pa, pb, projm, wpa, wpb, wo, d, tb):
    t = dmix.shape[0]
    blk = pl.BlockSpec((tb, d), lambda i: (i, 0))
    wspec = pl.BlockSpec((d, d), lambda i: (0, 0))

    def body(dmix_ref, pa_ref, pb_ref, ga_ref, gb_ref, wpa_ref, wpb_ref, wo_ref,
             dpa_ref, dpb_ref, dya_ref, dyb_ref, dg_ref):
        dm = _dot(dmix_ref[...], wo_ref[...], NT)
        sa, sb = _sigmoid(ga_ref[...]), _sigmoid(gb_ref[...])
        dpa, dpb = dm * sa, dm * sb
        dpa_ref[...] = dpa.astype(dpa_ref.dtype)
        dpb_ref[...] = dpb.astype(dpb_ref.dtype)
        dg_ref[:, :d] = (dm * pa_ref[...] * sa * (1.0 - sa)).astype(dg_ref.dtype)
        dg_ref[:, d:] = (dm * pb_ref[...] * sb * (1.0 - sb)).astype(dg_ref.dtype)
        dya_ref[...] = _dot(dpa, wpa_ref[...], NT)
        dyb_ref[...] = _dot(dpb, wpb_ref[...], NT)

    return pl.pallas_call(
        body, name="mix_bwd", grid=(t // tb,),
        in_specs=[blk, blk, blk, pl.BlockSpec((tb, d), lambda i: (i, 6)), pl.BlockSpec((tb, d), lambda i: (i, 7)),
                  wspec, wspec, wspec],
        out_specs=[blk, blk, blk, blk, pl.BlockSpec((tb, 2 * d), lambda i: (i, 3))],
        out_shape=[jax.ShapeDtypeStruct((t, d), ACT), jax.ShapeDtypeStruct((t, d), ACT),
                   jax.ShapeDtypeStruct((t, d), F32), jax.ShapeDtypeStruct((t, d), F32),
                   jax.ShapeDtypeStruct((t, 8 * d), ACT)],
        compiler_params=_cparams(("parallel",)),
    )(dmix, pa, pb, projm, projm, wpa, wpb, wo)


def _ffn_tail_fwd(gu, wd, x1, g, b, tb):
    t, d = x1.shape
    f = wd.shape[0]
    fc = _tile(f, MM_TILE)
    blk = pl.BlockSpec((tb, d), lambda i: (i, 0))
    row_spec = pl.BlockSpec((1, d), lambda i: (0, 0))

    def body(gu_ref, wd_ref, x_ref, g_ref, b_ref, a_ref, h_ref, y_ref, yb_ref):
        ffn = jnp.zeros((tb, d), F32)
        for c in range(f // fc):
            gp = gu_ref[:, c * fc:(c + 1) * fc].astype(F32)
            act = (gp * _sigmoid(gp) * gu_ref[:, f + c * fc:f + (c + 1) * fc].astype(F32)).astype(a_ref.dtype)
            a_ref[:, c * fc:(c + 1) * fc] = act
            ffn = ffn + _dot(act, wd_ref[c * fc:(c + 1) * fc, :])
        hres = ALPHA * x_ref[...] + ffn
        xhat, _ = _ln_hat(hres)
        y = xhat * g_ref[...] + b_ref[...]
        h_ref[...] = hres
        y_ref[...] = y
        yb_ref[...] = y.astype(yb_ref.dtype)

    return pl.pallas_call(
        body, name="ffn_tail_fwd", grid=(t // tb,),
        in_specs=[pl.BlockSpec((tb, 2 * f), lambda i: (i, 0)), pl.BlockSpec((f, d), lambda i: (0, 0)),
                  blk, row_spec, row_spec],
        out_specs=[pl.BlockSpec((tb, f), lambda i: (i, 0)), blk, blk, blk],
        out_shape=[jax.ShapeDtypeStruct((t, f), ACT), jax.ShapeDtypeStruct((t, d), F32),
                   jax.ShapeDtypeStruct((t, d), F32), jax.ShapeDtypeStruct((t, d), ACT)],
        compiler_params=_cparams(("parallel",)),
    )(gu, wd, x1, g, b)


def _ffn_tail_bwd(dh, wd, gu, tb):
    t, d = dh.shape
    f = wd.shape[0]
    fc = _tile(f, MM_TILE)

    def body(dh_ref, wd_ref, gu_ref, dgu_ref):
        dh_v = dh_ref[...]
        for c in range(f // fc):
            da = _dot(dh_v, wd_ref[c * fc:(c + 1) * fc, :], NT)
            gp = gu_ref[:, c * fc:(c + 1) * fc].astype(F32)
            sg = _sigmoid(gp)
            dgu_ref[:, c * fc:(c + 1) * fc] = (
                da * gu_ref[:, f + c * fc:f + (c + 1) * fc].astype(F32) * sg * (1.0 + gp * (1.0 - sg))
            ).astype(dgu_ref.dtype)
            dgu_ref[:, f + c * fc:f + (c + 1) * fc] = (da * gp * sg).astype(dgu_ref.dtype)

    return pl.pallas_call(
        body, name="ffn_tail_bwd", grid=(t // tb,),
        in_specs=[pl.BlockSpec((tb, d), lambda i: (i, 0)), pl.BlockSpec((f, d), lambda i: (0, 0)),
                  pl.BlockSpec((tb, 2 * f), lambda i: (i, 0))],
        out_specs=pl.BlockSpec((tb, 2 * f), lambda i: (i, 0)),
        out_shape=jax.ShapeDtypeStruct((t, 2 * f), ACT),
        compiler_params=_cparams(("parallel",)),
    )(dh, wd, gu)


def _ffn_head_bwd(dgu, wgu, dh2, hres, g, tb):
    t, d = dh2.shape
    f2 = wgu.shape[1]
    blk = pl.BlockSpec((tb, d), lambda i: (i, 0))
    acc = pl.BlockSpec((SUBLANES, d), lambda i: (0, 0))

    def body(dgu_ref, w_ref, dh2_ref, h_ref, g_ref, dh_ref, dhb_ref, dg_ref, db_ref):
        @pl.when(pl.program_id(0) == 0)
        def _():
            dg_ref[...] = jnp.zeros_like(dg_ref)
            db_ref[...] = jnp.zeros_like(db_ref)

        dy_v = _dot(dgu_ref[...], w_ref[...], NT) + ALPHA * dh2_ref[...]
        xhat, r = _ln_hat(h_ref[...])
        dh = _ln_bwd(dy_v * g_ref[...], xhat, r)
        dh_ref[...] = dh
        dhb_ref[...] = dh.astype(dhb_ref.dtype)
        dg_ref[...] += _fold8(dy_v * xhat)
        db_ref[...] += _fold8(dy_v)

    return pl.pallas_call(
        body, name="ffn_head_bwd", grid=(t // tb,),
        in_specs=[pl.BlockSpec((tb, f2), lambda i: (i, 0)), pl.BlockSpec((d, f2), lambda i: (0, 0)),
                  blk, blk, pl.BlockSpec((1, d), lambda i: (0, 0))],
        out_specs=[blk, blk, acc, acc],
        out_shape=[jax.ShapeDtypeStruct((t, d), F32), jax.ShapeDtypeStruct((t, d), ACT),
                   jax.ShapeDtypeStruct((SUBLANES, d), F32), jax.ShapeDtypeStruct((SUBLANES, d), F32)],
        compiler_params=_cparams(("arbitrary",)),
    )(dgu, wgu, dh2, hres, g)


def _loss_ln_bwd(y, target, hres, g, tb):
    t, d = y.shape
    blk = pl.BlockSpec((tb, d), lambda i: (i, 0))
    acc = pl.BlockSpec((SUBLANES, d), lambda i: (0, 0))

    def body(y_ref, t_ref, h_ref, g_ref, dh_ref, dhb_ref, dg_ref, db_ref, l_ref):
        @pl.when(pl.program_id(0) == 0)
        def _():
            for r_ in (dg_ref, db_ref, l_ref):
                r_[...] = jnp.zeros_like(r_)

        err = y_ref[...] - t_ref[...]
        dy_v = err * (1.0 / d)
        sq = _fold8(err * err)
        part = sq[:, :LANES]
        for c in range(1, d // LANES):
            part = part + sq[:, c * LANES:(c + 1) * LANES]
        l_ref[...] += part
        xhat, r = _ln_hat(h_ref[...])
        dh = _ln_bwd(dy_v * g_ref[...], xhat, r)
        dh_ref[...] = dh
        dhb_ref[...] = dh.astype(dhb_ref.dtype)
        dg_ref[...] += _fold8(dy_v * xhat)
        db_ref[...] += _fold8(dy_v)

    res = pl.pallas_call(
        body, name="loss_ln_bwd", grid=(t // tb,),
        in_specs=[blk, blk, blk, pl.BlockSpec((1, d), lambda i: (0, 0))],
        out_specs=[blk, blk, acc, acc, pl.BlockSpec((SUBLANES, LANES), lambda i: (0, 0))],
        out_shape=[jax.ShapeDtypeStruct((t, d), F32), jax.ShapeDtypeStruct((t, d), ACT),
                   jax.ShapeDtypeStruct((SUBLANES, d), F32), jax.ShapeDtypeStruct((SUBLANES, d), F32),
                   jax.ShapeDtypeStruct((SUBLANES, LANES), F32)],
        compiler_params=_cparams(("arbitrary",)),
    )(y, target, hres, g)
    return res[:4], res[4]


def _adamw(w, g, m, v):
    shape = w.shape
    cols = shape[-1]
    w2, g2, m2, v2 = (a.reshape(-1, cols) for a in (w, g, m, v))
    rows = w2.shape[0]
    tr = _tile(rows, 256, SUBLANES)
    blk = pl.BlockSpec((tr, cols), lambda i: (i, 0))

    def body(w_ref, g_ref, m_ref, v_ref, d_ref, nm_ref, nv_ref):
        g_v = g_ref[...]
        nm = ADAM_B1 * m_ref[...] + (1.0 - ADAM_B1) * g_v
        nv = ADAM_B2 * v_ref[...] + (1.0 - ADAM_B2) * (g_v * g_v)
        m_hat = nm / (1.0 - ADAM_B1 ** ADAM_STEP)
        v_hat = nv / (1.0 - ADAM_B2 ** ADAM_STEP)
        d_ref[...] = -ADAM_LR * (m_hat / (jnp.sqrt(v_hat) + ADAM_EPS) + ADAM_WD * w_ref[...])
        nm_ref[...] = nm
        nv_ref[...] = nv

    out = jax.ShapeDtypeStruct((rows, cols), F32)
    res = pl.pallas_call(
        body, name="adamw", grid=(rows // tr,),
        in_specs=[blk] * 4, out_specs=[blk] * 3, out_shape=[out] * 3,
        compiler_params=_cparams(("parallel",)),
    )(w2, g2, m2, v2)
    return tuple(r.reshape(shape) for r in res)


def _place():
    x, y, c = lax.axis_index("x"), lax.axis_index("y"), lax.axis_index("c")
    return x, y, c, [(1 - x, y), (x, 1 - y), (1 - x, 1 - y)]


def _remote(src, dst, send_sems, recv_sems, k, to):
    return pltpu.make_async_remote_copy(src_ref=src, dst_ref=dst, send_sem=send_sems.at[k],
                                        recv_sem=recv_sems.at[k], device_id=to, device_id_type=MESH)


class _Carried:
    def __init__(self, inputs, out_shapes, n_sems, copies):
        self.inputs, self.out_shapes, self.n_sems, self.copies = list(inputs), list(out_shapes), n_sems, copies

    def scratch(self):
        return [pltpu.SemaphoreType.DMA((self.n_sems,)), pltpu.SemaphoreType.DMA((self.n_sems,))]


def _join_plans(first, second):
    ni, no, ns = len(first.inputs), len(first.out_shapes), first.n_sems

    def copies(in_refs, out_refs, send_sems, recv_sems):
        start1, finish1 = first.copies(in_refs[:ni], out_refs[:no], send_sems, recv_sems)
        start2, finish2 = second.copies(in_refs[ni:], out_refs[no:], send_sems.at[pl.ds(ns, second.n_sems)],
                                        recv_sems.at[pl.ds(ns, second.n_sems)])

        def start():
            start1()
            start2()

        def finish():
            finish1()
            finish2()

        return start, finish

    return _Carried(first.inputs + second.inputs, first.out_shapes + second.out_shapes, ns + second.n_sems, copies)


def _run_comm(name, plan):
    n_in, n_out = len(plan.inputs), len(plan.out_shapes)

    def body(*refs):
        start, finish = plan.copies(refs[:n_in], refs[n_in:n_in + n_out], refs[-2], refs[-1])
        start()
        finish()

    return pl.pallas_call(
        body, name=name, in_specs=[ANY] * n_in, out_specs=[ANY] * n_out, out_shape=plan.out_shapes,
        scratch_shapes=plan.scratch(),
    )(*plan.inputs)


def _half_rows(rows, core):
    if rows % (4 * SUBLANES):
        return None
    return pl.ds(pl.multiple_of(core * (rows // 2), 2 * SUBLANES), rows // 2)


def _all_gather_plan(shards):
    n = len(shards)

    def copies(x_refs, out_refs, send_sems, recv_sems):
        x, y, c, chips = _place()
        sibling = (x, y, 1 - c)
        mine = 2 * x + y
        split = [_half_rows(x_refs[t].shape[0], c) is not None for t in range(n)]

        def src(t):
            return x_refs[t].at[_half_rows(x_refs[t].shape[0], c)] if split[t] else x_refs[t]

        def slot(t, chip_idx, core):
            rows = _half_rows(x_refs[t].shape[0], core)
            return out_refs[t].at[chip_idx, rows] if split[t] else out_refs[t].at[chip_idx]

        def first():
            return [_remote(src(t), slot(t, mine, c), send_sems, recv_sems, 6 * t + j, (cx, cy, c))
                    for j, (cx, cy) in enumerate(chips) for t in range(n)]

        def start():
            for cp in first():
                cp.start()

        def finish():
            passed = []
            for j, (cx, cy) in enumerate(chips):
                for t in range(n):
                    theirs = slot(t, 2 * cx + cy, c)
                    _remote(theirs, theirs, send_sems, recv_sems, 6 * t + j, (cx, cy, c)).wait_recv()
                    if split[t]:
                        fwd = _remote(theirs, theirs, send_sems, recv_sems, 6 * t + 3 + j, sibling)
                        fwd.start()
                        passed.append(fwd)
            for j, (cx, cy) in enumerate(chips):
                for t in range(n):
                    if split[t]:
                        other = slot(t, 2 * cx + cy, 1 - c)
                        _remote(other, other, send_sems, recv_sems, 6 * t + 3 + j, sibling).wait_recv()
            for cp in first() + passed:
                cp.wait_send()

        return start, finish

    return _Carried(shards, [jax.ShapeDtypeStruct((N_CHIPS,) + s.shape, s.dtype) for s in shards], 6 * n, copies)


def _sibling_exchange_plan(grads, small=None):
    n = len(grads)
    extra = [] if small is None else [small]

    def copies(in_refs, out_refs, send_sems, recv_sems):
        x, y, c, _ = _place()
        sibling = (x, y, 1 - c)

        def all_copies():
            cps = [_remote(in_refs[t].at[:, _half_rows(in_refs[t].shape[1], 1 - c), :], out_refs[t],
                           send_sems, recv_sems, t, sibling) for t in range(n)]
            if extra:
                cps.append(_remote(in_refs[n], out_refs[n], send_sems, recv_sems, n, sibling))
            return cps

        def start():
            for cp in all_copies():
                cp.start()

        def finish():
            for cp in all_copies():
                cp.wait()

        return start, finish

    shapes = [jax.ShapeDtypeStruct((g.shape[0], g.shape[1] // 2, g.shape[2]), g.dtype) for g in grads]
    shapes += [jax.ShapeDtypeStruct(s.shape, s.dtype) for s in extra]
    return _Carried(list(grads) + extra, shapes, n + 1, copies)


def _chip_exchange_plan(travel, small=None):
    n = len(travel)
    extra = [] if small is None else [small]

    def copies(in_refs, out_refs, send_sems, recv_sems):
        x, y, c, chips = _place()
        mine = 2 * x + y

        def all_copies():
            cps = []
            for j, (cx, cy) in enumerate(chips):
                to = (cx, cy, c)
                for t in range(n):
                    cps.append(_remote(in_refs[t].at[2 * cx + cy], out_refs[t].at[mine], send_sems, recv_sems,
                                       3 * t + j, to))
                if extra:
                    cps.append(_remote(in_refs[n], out_refs[n].at[mine], send_sems, recv_sems, 3 * n + j, to))
            return cps

        def start():
            for cp in all_copies():
                cp.start()

        def finish():
            for cp in all_copies():
                cp.wait()

        return start, finish

    shapes = [jax.ShapeDtypeStruct(g.shape, g.dtype) for g in travel]
    shapes += [jax.ShapeDtypeStruct((N_CHIPS,) + s.shape, s.dtype) for s in extra]
    return _Carried(list(travel) + extra, shapes, 3 * n + 3, copies)


def _sibling_merge_plan(reduced):
    n = len(reduced)

    def copies(in_refs, out_refs, send_sems, recv_sems):
        x, y, c, _ = _place()

        def all_copies():
            return [_remote(in_refs[t], out_refs[t], send_sems, recv_sems, t, (x, y, 1 - c)) for t in range(n)]

        def start():
            for cp in all_copies():
                cp.start()

        def finish():
            for cp in all_copies():
                cp.wait()

        return start, finish

    return _Carried(reduced, [jax.ShapeDtypeStruct(r.shape, r.dtype) for r in reduced], n, copies)


def _pair_sum(place, grad, land):
    n, r, c = grad.shape
    half = r // 2
    tr = _tile(half, 256, SUBLANES)
    nb = half // tr

    def body(place_ref, a_ref, b_ref, travel_ref, own_ref):
        total = a_ref[0] + b_ref[0]
        travel_ref[0] = total.astype(travel_ref.dtype)

        @pl.when(pl.program_id(1) == place_ref[1])
        def _():
            own_ref[...] = total

    return pl.pallas_call(
        body, name="grad_pair_sum",
        grid_spec=pltpu.PrefetchScalarGridSpec(
            num_scalar_prefetch=1, grid=(nb, n),
            in_specs=[pl.BlockSpec((1, tr, c), lambda i, s, p: (s, p[0] * nb + i, 0)),
                      pl.BlockSpec((1, tr, c), lambda i, s, p: (s, i, 0))],
            out_specs=[pl.BlockSpec((1, tr, c), lambda i, s, p: (s, i, 0)),
                       pl.BlockSpec((tr, c), lambda i, s, p: (i, 0))]),
        out_shape=[jax.ShapeDtypeStruct((n, half, c), BF16), jax.ShapeDtypeStruct((half, c), F32)],
        compiler_params=_cparams(("parallel", "arbitrary")),
    )(place, grad, land)


def _chip_sum(place, own, land, name):
    n, r, c = land.shape
    tr = _tile(r, 256, SUBLANES)

    def body(place_ref, own_ref, land_ref, o_ref):
        mine = place_ref[1]
        acc = jnp.zeros(o_ref.shape, F32)
        for s in range(n):
            acc = acc + jnp.where(mine == s, own_ref[...], land_ref[s].astype(F32))
        o_ref[...] = acc

    return pl.pallas_call(
        body, name=name,
        grid_spec=pltpu.PrefetchScalarGridSpec(
            num_scalar_prefetch=1, grid=(r // tr,),
            in_specs=[pl.BlockSpec((tr, c), lambda i, p: (i, 0)),
                      pl.BlockSpec((n, tr, c), lambda i, p: (0, i, 0))],
            out_specs=pl.BlockSpec((tr, c), lambda i, p: (i, 0))),
        out_shape=jax.ShapeDtypeStruct((r, c), F32),
        compiler_params=_cparams(("parallel",)),
    )(place, own, land)


def _add2(a, b):
    rows = a.shape[0]
    tr = _tile(rows, 256, SUBLANES)
    blk = pl.BlockSpec((tr, a.shape[1]), lambda i: (i, 0))

    def body(a_ref, b_ref, o_ref):
        o_ref[...] = a_ref[...] + b_ref[...]

    return pl.pallas_call(
        body, name="grad_small_pair_sum", grid=(rows // tr,), in_specs=[blk, blk], out_specs=blk,
        out_shape=jax.ShapeDtypeStruct(a.shape, F32), compiler_params=_cparams(("parallel",)),
    )(a, b)


def _merge_halves(place, mine, other):
    first_core = place[0] == 0
    return jnp.concatenate([jnp.where(first_core, mine, other), jnp.where(first_core, other, mine)], axis=0)


_BIG = (("w_in", 2), ("w_pa", 1), ("w_pb", 1), ("w_o", 1), ("w_ffn_gate", 2), ("w_ffn_up", 2),
        ("w_ffn_down", 1))
_SMALL = ("conv_w", "a_log", "dt_bias", "o_norm_w", "sgu_ln_g", "sgu_ln_b", "w_s", "b_s",
          "ln1_g", "ln1_b", "ln2_g", "ln2_b")


def _pack_small(arrays):
    pieces = []
    for a in arrays:
        if a.shape[-1] % LANES == 0:
            a2 = a.reshape(-1, LANES)
        else:
            a2 = jnp.pad(a.reshape(-1, a.shape[-1]), ((0, 0), (0, LANES - a.shape[-1])))
        pieces.append(jnp.pad(a2, ((0, -a2.shape[0] % SUBLANES), (0, 0))))
    return jnp.concatenate(pieces, axis=0)


def _unpack_small(buf, like):
    out, off = [], 0
    for a in like:
        if a.shape[-1] % LANES == 0:
            rows = a.size // LANES
            out.append(buf[off:off + rows].reshape(a.shape))
        else:
            rows = a.size // a.shape[-1]
            out.append(buf[off:off + rows, :a.shape[-1]].reshape(a.shape))
        off += -(-rows // SUBLANES) * SUBLANES
    return out


def _unshard(gathered, local, chip, axis):
    parts = [jnp.where(chip == s, local, gathered[s]) for s in range(N_CHIPS)]
    return jnp.concatenate(parts, axis=axis - 1)


def _to_shards(full, axis):
    l, r, c = full.shape
    if axis == 1:
        return full.reshape(l, N_CHIPS, r // N_CHIPS, c)
    return jnp.transpose(full.reshape(l, r, N_CHIPS, c // N_CHIPS), (0, 2, 1, 3))


def _row(v, width=None):
    v = v.reshape(1, -1).astype(F32)
    if width is not None and v.shape[1] < width:
        v = jnp.pad(v, ((0, 0), (0, width - v.shape[1])))
    return v


def _layer_consts(p, l, d):
    heads = d // DN_DK
    return dict(
        alog=_row(p["a_log"][l], LANES), dtb=_row(p["dt_bias"][l], LANES),
        onw=_row(jnp.tile(p["o_norm_w"][l], heads)),
        lng=_row(p["sgu_ln_g"][l]), lnb=_row(p["sgu_ln_b"][l]),
        ws=p["w_s"][l].astype(F32),
        bst=jnp.pad(p["b_s"][l].T, ((0, 0), (0, LANES - p["b_s"].shape[1]))),
        g1=_row(p["ln1_g"][l]), b1=_row(p["ln1_b"][l]), g2=_row(p["ln2_g"][l]), b2=_row(p["ln2_b"][l]))


class _NoComm:
    def with_proj_main(self):
        return None

    def after_proj_main(self, got):
        pass

    def weights(self, full):
        return full

    def with_dn_fwd(self):
        return None

    def after_dn_fwd(self, got):
        pass

    def with_ffn_in_dw(self):
        return None

    def after_ffn_in_dw(self, got):
        pass

    def after_branch_grads(self, g):
        pass

    def with_dn_bwd(self):
        return None

    def after_dn_bwd(self, got):
        pass

    def with_proj_main_dw(self):
        return None

    def after_proj_main_dw(self, got):
        pass

    def with_ffn_in(self):
        return None

    def after_ffn_in(self, got):
        pass

    def after_all_grads(self, g):
        pass

    def with_proj_main_dx(self):
        return None

    def after_proj_main_dx(self, got):
        pass


def _carry(carried, after, call, *args, **kw):
    if carried is None:
        return call(*args, **kw)
    out, got = call(*args, carried=carried, **kw)
    after(got)
    return out


def _in_proj_weights(w_in, d):
    heads, q4 = d // DN_DK, 4 * d
    wba = jnp.zeros((d, 2 * LANES), w_in.dtype)
    wba = wba.at[:, :heads].set(w_in[:, q4:q4 + heads])
    wba = wba.at[:, LANES:LANES + heads].set(w_in[:, q4 + heads:q4 + 2 * heads])
    return jnp.concatenate([w_in[:, :q4], w_in[:, q4 + 2 * heads:]], axis=1), wba


def _layer_fwd(x, xb, full, cl, d, tb, comm):
    wm, wba = _in_proj_weights(full["w_in"], d)
    projm = _carry(comm.with_proj_main(), comm.after_proj_main, _matmul, xb, wm, NN, "proj_main", tn=MM_WIDE)
    full = comm.weights(full)
    wl = dict(wm=wm, wba=wba, conv=full["conv_w"], wpa=full["w_pa"], wpb=full["w_pb"], wo=full["w_o"],
              wgu=jnp.concatenate([full["w_ffn_gate"], full["w_ffn_up"]], axis=1), wd=full["w_ffn_down"])
    ba = _matmul(xb, wba, NN, "proj_gates")
    qkv = _conv_fwd(projm, wl["conv"], d, _tile(x.shape[0], 2 * tb, SUBLANES))
    (o, states, ycors), got = _dn_fwd(qkv, ba, cl["alog"], cl["dtb"], d, comm.with_dn_fwd())
    comm.after_dn_fwd(got)
    ya, yb = _gate_sgu_fwd(o, projm, cl["onw"], cl["lng"], cl["lnb"], cl["ws"], cl["bst"], d)
    pa, pb, m, h1, x1, x1b = _mix_fwd(ya, yb, projm, x, wl["wpa"], wl["wpb"], wl["wo"], cl["g1"], cl["b1"], d, tb)
    gu = _carry(comm.with_ffn_in(), comm.after_ffn_in, _matmul, x1b, wl["wgu"], NN, "ffn_in", out_dtype=ACT)
    act, h2, x2, x2b = _ffn_tail_fwd(gu, wl["wd"], x1, cl["g2"], cl["b2"], tb)
    saved = dict(xb=xb, projm=projm, ba=ba, qkv=qkv, o=o, states=states, ycors=ycors, ya=ya, yb=yb,
                 pa=pa, pb=pb, m=m, h1=h1, x1b=x1b, gu=gu, act=act, h2=h2)
    return x2, x2b, saved, wl


def _layer_bwd(sv, wl, cl, d, tb, comm, ln2_bwd, next_ln=None):
    g = {}
    dh2, dh2b, dg2, db2 = ln2_bwd
    g["ln2_g"], g["ln2_b"] = dg2.sum(0), db2.sum(0)
    g["wd"] = _matmul(sv["act"], dh2b, TN, "ffn_out_dw")
    dgu = _ffn_tail_bwd(dh2b, wl["wd"], sv["gu"], tb)
    g["wgu"] = _carry(comm.with_ffn_in_dw(), comm.after_ffn_in_dw, _matmul, sv["x1b"], dgu, TN, "ffn_in_dw")
    dh1, dh1b, dg1, db1 = _ffn_head_bwd(dgu, wl["wgu"], dh2, sv["h1"], cl["g1"], tb)
    g["ln1_g"], g["ln1_b"] = dg1.sum(0), db1.sum(0)
    g["wo"] = _matmul(sv["m"], dh1b, TN, "wo_dw")
    dpa, dpb, dya, dyb, dprojm = _mix_bwd(dh1b, sv["pa"], sv["pb"], sv["projm"], wl["wpa"], wl["wpb"], wl["wo"], d, tb)
    g["wpa"] = _matmul(sv["ya"], dpa, TN, "wpa_dw")
    g["wpb"] = _matmul(sv["yb"], dpb, TN, "wpb_dw")
    comm.after_branch_grads(g)
    do, dprojm, donw, dlng, dlnb, dws, dbst = _gate_sgu_bwd(
        dya, dyb, sv["o"], sv["projm"], cl["onw"], cl["lng"], cl["lnb"], cl["ws"], cl["bst"], dprojm, d)
    heads, groups = d // DN_DK, d // SGU_GROUP_DIM
    g["o_norm_w"], g["sgu_ln_g"], g["sgu_ln_b"] = donw.sum(0), dlng.sum(0), dlnb.sum(0)
    g["w_s"], g["b_s"] = dws, dbst[:, :groups].T
    (dqkv, dba, dal, ddt), got = _dn_bwd(sv["qkv"], sv["ba"], cl["alog"], cl["dtb"], do, sv["states"],
                                         sv["ycors"], d, comm.with_dn_bwd())
    comm.after_dn_bwd(got)
    g["a_log"], g["dt_bias"] = dal.sum(0)[:heads], ddt.sum(0)[:heads]
    tbc = _tile(sv["xb"].shape[0], 2 * tb, SUBLANES)
    dy, dcw = _conv_bwd_dy(sv["projm"], wl["conv"], dqkv, d, tbc)
    g["conv_w"] = dcw.sum(1)
    dprojm = _conv_bwd_dx(dy, wl["conv"], dprojm, d, tbc)
    g["wm"] = _carry(comm.with_proj_main_dw(), comm.after_proj_main_dw, _matmul, sv["xb"], dprojm, TN,
                     "proj_main_dw", tn=MM_WIDE)
    g["wba"] = _matmul(sv["xb"], dba, TN, "proj_gates_dw")
    dx = _matmul(dba, wl["wba"], NT, "proj_gates_dx", add=dh1, coef=ALPHA)
    comm.after_all_grads(g)
    if next_ln is not None:
        return _matmul(dprojm, wl["wm"], NT, "proj_main_dx", add=dx, tm=MM_TILE // 3, tk=MM_WIDE, ln=next_ln), g
    dx = _carry(comm.with_proj_main_dx(), comm.after_proj_main_dx, _matmul, dprojm, wl["wm"], NT, "proj_main_dx",
                add=dx, tk=MM_WIDE)
    return dx, g


_BRANCH = ("w_pa", "w_pb", "w_o", "w_ffn_gate", "w_ffn_up", "w_ffn_down")


def _grad_shards(g, d, keys):
    heads, q4 = d // DN_DK, 4 * d
    rows = lambda a: a.reshape(N_CHIPS, -1, a.shape[1])
    out = {}
    if "w_in" in keys:
        gm, gba, wsh = g["wm"], g["wba"], 2 * d + heads // 2
        out["w_in"] = jnp.stack([gm[:, :wsh],
                                 jnp.concatenate([gm[:, wsh:q4], gba[:, :heads]], axis=1),
                                 jnp.concatenate([gba[:, LANES:LANES + heads], gm[:, q4:q4 + wsh - heads]], axis=1),
                                 gm[:, q4 + wsh - heads:]])
    if "w_pa" in keys:
        ggu = g["wgu"]
        f = ggu.shape[1] // 2
        fs = f // N_CHIPS
        out.update({
            "w_pa": rows(g["wpa"]), "w_pb": rows(g["wpb"]), "w_o": rows(g["wo"]), "w_ffn_down": rows(g["wd"]),
            "w_ffn_gate": jnp.stack([ggu[:, s * fs:(s + 1) * fs] for s in range(N_CHIPS)]),
            "w_ffn_up": jnp.stack([ggu[:, f + s * fs:f + (s + 1) * fs] for s in range(N_CHIPS)])})
    return out


def _local_step(x, target, full0, full1_of, small_w, comm0=None):
    t, d = x.shape
    tb = _tile(t, 256, SUBLANES)
    comm0 = comm0 or _NoComm()
    consts = [_layer_consts(small_w, l, d) for l in range(DEPTH)]
    x1, x1b, sv0, w0 = _layer_fwd(x, x.astype(ACT), full0, consts[0], d, tb, comm0)
    x2, _, sv1, w1 = _layer_fwd(x1, x1b, full1_of(), consts[1], d, tb, _NoComm())
    ln2_bwd, loss_parts = _loss_ln_bwd(x2, target, sv1["h2"], consts[1]["g2"], tb)
    ln2_bwd, g1 = _layer_bwd(sv1, w1, consts[1], d, tb, _NoComm(), ln2_bwd, next_ln=(sv0["h2"], consts[0]["g2"]))
    comm0.layer1_grads = g1
    grad_x, g0 = _layer_bwd(sv0, w0, consts[0], d, tb, comm0, ln2_bwd)
    return loss_parts, grad_x, [g0, g1]


def kernel(x, w_in, conv_w, a_log, dt_bias, o_norm_w, sgu_ln_g, sgu_ln_b, w_s, b_s, w_pa, w_pb, w_o, ln1_g, ln1_b, w_ffn_gate, w_ffn_up, w_ffn_down, ln2_g, ln2_b, loss_target, m_w_in, m_conv_w, m_a_log, m_dt_bias, m_o_norm_w, m_sgu_ln_g, m_sgu_ln_b, m_w_s, m_b_s, m_w_pa, m_w_pb, m_w_o, m_ln1_g, m_ln1_b, m_w_ffn_gate, m_w_ffn_up, m_w_ffn_down, m_ln2_g, m_ln2_b, v_w_in, v_conv_w, v_a_log, v_dt_bias, v_o_norm_w, v_sgu_ln_g, v_sgu_ln_b, v_w_s, v_b_s, v_w_pa, v_w_pb, v_w_o, v_ln1_g, v_ln1_b, v_w_ffn_gate, v_w_ffn_up, v_w_ffn_down, v_ln2_g, v_ln2_b):
    names = ("w_in", "conv_w", "a_log", "dt_bias", "o_norm_w", "sgu_ln_g", "sgu_ln_b", "w_s", "b_s", "w_pa",
             "w_pb", "w_o", "ln1_g", "ln1_b", "w_ffn_gate", "w_ffn_up", "w_ffn_down", "ln2_g", "ln2_b")
    w = dict(zip(names, (w_in, conv_w, a_log, dt_bias, o_norm_w, sgu_ln_g, sgu_ln_b, w_s, b_s, w_pa, w_pb, w_o,
                         ln1_g, ln1_b, w_ffn_gate, w_ffn_up, w_ffn_down, ln2_g, ln2_b)))
    mom = dict(zip(names, (m_w_in, m_conv_w, m_a_log, m_dt_bias, m_o_norm_w, m_sgu_ln_g, m_sgu_ln_b, m_w_s, m_b_s,
                           m_w_pa, m_w_pb, m_w_o, m_ln1_g, m_ln1_b, m_w_ffn_gate, m_w_ffn_up, m_w_ffn_down,
                           m_ln2_g, m_ln2_b)))
    var = dict(zip(names, (v_w_in, v_conv_w, v_a_log, v_dt_bias, v_o_norm_w, v_sgu_ln_g, v_sgu_ln_b, v_w_s, v_b_s,
                           v_w_pa, v_w_pb, v_w_o, v_ln1_g, v_ln1_b, v_w_ffn_gate, v_w_ffn_up, v_w_ffn_down,
                           v_ln2_g, v_ln2_b)))
    chip = 2 * lax.axis_index("x") + lax.axis_index("y")
    place = jnp.stack([lax.axis_index("c"), chip]).astype(jnp.int32)

    big = [k for k, _ in _BIG]
    axis_of = dict(_BIG)
    local = {k: w[k].astype(BF16) for k in big}
    local["conv_w"] = conv_w

    def gather_plan(l, keys):
        return _all_gather_plan([local[k][l] for k in keys])

    def full_of(l, keys, gathered):
        return {k: _unshard(gt, local[k][l], chip, axis_of.get(k, 2)) for k, gt in zip(keys, gathered)}

    def pair_sums(grads_l, keys, lands):
        return [_pair_sum(place, grads_l[k], land) for k, land in zip(keys, lands)]

    def chip_sums(pairs, lands):
        return [_chip_sum(place, p[1], land, "grad_chip_sum") for p, land in zip(pairs, lands)]

    class Layer0Comm(_NoComm):
        def with_proj_main(self):
            return gather_plan(0, _BRANCH)

        def after_proj_main(self, got):
            self.rest = full_of(0, _BRANCH, got)

        def weights(self, full):
            return {**full, **self.rest}

        def with_dn_fwd(self):
            return gather_plan(1, mixer)

        def after_dn_fwd(self, got):
            self.full1 = full_of(1, mixer, got)

        def with_ffn_in(self):
            return gather_plan(1, ffn)

        def after_ffn_in(self, got):
            self.full1.update(full_of(1, ffn, got))

        def with_ffn_in_dw(self):
            self.g1 = _grad_shards(self.layer1_grads, x.shape[-1], big)
            return _sibling_exchange_plan([self.g1[k] for k in big])

        def after_ffn_in_dw(self, got):
            self.pairs1 = pair_sums(self.g1, big, got)

        def with_dn_bwd(self):
            return _chip_exchange_plan([p[0] for p in self.pairs1])

        def after_dn_bwd(self, got):
            self.red1 = chip_sums(self.pairs1, got)

        def after_branch_grads(self, g0):
            shards = _grad_shards(g0, x.shape[-1], _BRANCH)
            lands = _run_comm("grad_sibling_exchange", _sibling_exchange_plan([shards[k] for k in _BRANCH]))
            self.pairs0 = pair_sums(shards, _BRANCH, lands)

        def with_proj_main_dw(self):
            return _chip_exchange_plan([p[0] for p in self.pairs0])

        def after_proj_main_dw(self, got):
            self.red0 = chip_sums(self.pairs0, got)

        def after_all_grads(self, g0):
            g_in = _grad_shards(g0, x.shape[-1], ["w_in"])["w_in"]
            self.small_g = {k: jnp.stack([g0[k], self.layer1_grads[k]]) for k in _SMALL}
            small = _pack_small([self.small_g[k] for k in _SMALL])
            land, sland = _run_comm("grad_sibling_exchange_last", _sibling_exchange_plan([g_in], small))
            self.pair_in = _pair_sum(place, g_in, land)
            self.small_chip = _add2(small, sland)

        def with_proj_main_dx(self):
            return _join_plans(_chip_exchange_plan([self.pair_in[0]], self.small_chip),
                               _sibling_merge_plan(self.red0 + self.red1))

        def after_proj_main_dx(self, got):
            self.red_in = _chip_sum(place, self.pair_in[1], got[0], "grad_chip_sum")
            self.small_total = _chip_sum(place, self.small_chip, got[1], "grad_small_chip_sum")
            self.others = got[2:]

    comm = Layer0Comm()
    first, mixer, ffn = ["w_in", "conv_w"], ["w_in", "conv_w", "w_pa", "w_pb", "w_o"], list(_BRANCH[3:])
    full0 = full_of(0, first, _run_comm("all_gather_weights", gather_plan(0, first)))
    small_w = {k: w[k] for k in _SMALL if k != "conv_w"}
    loss_parts, grad_x, g = _local_step(x[0], loss_target[0], full0, lambda: comm.full1, small_w, comm)

    reduced = [comm.red_in] + comm.red0 + comm.red1
    others = list(_run_comm("grad_sibling_merge", _sibling_merge_plan([comm.red_in]))) + list(comm.others)
    halves = [_merge_halves(place, mine, other) for mine, other in zip(reduced, others)]
    grads = {k: jnp.stack([halves[i], halves[len(big) + i]]) for i, k in enumerate(big)}
    grads.update(zip(_SMALL, _unpack_small(comm.small_total, [comm.small_g[k] for k in _SMALL])))
    grads["conv_w"] = lax.dynamic_index_in_dim(_to_shards(grads["conv_w"], 2), chip, 1, keepdims=False)

    delta, new_m, new_v = {}, {}, {}
    for k in [k for k, _ in _BIG] + ["conv_w"]:
        delta[k], new_m[k], new_v[k] = _adamw(w[k], grads[k], mom[k], var[k])
    rep = [k for k in _SMALL if k != "conv_w"]
    pack = lambda dct: _pack_small([dct[k] for k in rep])
    packed = _adamw(pack(w), pack(grads), pack(mom), pack(var))
    for dst, src in zip((delta, new_m, new_v), packed):
        dst.update(zip(rep, _unpack_small(src, [w[k] for k in rep])))

    loss = 0.5 * lax.psum(jnp.sum(loss_parts), ("x", "y", "c")) / x.shape[-1]
    return (loss, grad_x[None], *[grads[k] for k in names], *[delta[k] for k in names],
            *[new_m[k] for k in names], *[new_v[k] for k in names])
```

```python
import math

import jax
import jax.numpy as jnp
from jax import lax
from jax.experimental import pallas as pl
from jax.experimental.pallas import tpu as pltpu

F32 = jnp.float32
BF16 = jnp.bfloat16
MXU_DTYPE = jnp.bfloat16
ACT = jnp.bfloat16
HIGHEST = lax.Precision.HIGHEST

DEPTH = 2
CHUNK = 64
DN_GROUP = 2
DN_GROUP_FWD = 4
SGU_BLOCK = 128
SGU_WINDOWS = 2
CONV_K = 4
DN_DK = 128
SGU_GROUP_DIM = 128
LN_EPS = 1e-5
RMS_EPS = 1e-6
ALPHA = (2 * DEPTH) ** 0.25
ADAM_LR, ADAM_B1, ADAM_B2, ADAM_EPS, ADAM_WD, ADAM_STEP = 0.001, 0.9, 0.999, 1e-08, 0.01, 10

LANES = 128
SUBLANES = 8
VMEM_LIMIT = 52 * 2 ** 20
N_CHIPS = 4

NN = ((1,), (0,))
NT = ((1,), (1,))
TN = ((0,), (0,))
MESH = pl.DeviceIdType.MESH
ANY = pl.BlockSpec(memory_space=pl.ANY)


def _dot(a, b, dims=NN, prec=None):
    if prec is None:
        a = a.astype(MXU_DTYPE)
        b = b.astype(MXU_DTYPE)
    return lax.dot_general(a, b, (dims, ((), ())), preferred_element_type=F32, precision=prec)


def _cparams(sem=None):
    return pltpu.CompilerParams(dimension_semantics=sem, vmem_limit_bytes=VMEM_LIMIT)


def _tile(dim, pref, unit=LANES):
    t = (min(pref, dim) // unit) * unit
    while t >= unit:
        if dim % t == 0:
            return t
        t -= unit
    return dim


def _fold8(x):
    r, n = x.shape
    return x.reshape(r // SUBLANES, SUBLANES, n).sum(axis=0)


def _sigmoid(x):
    return 1.0 / (1.0 + jnp.exp(-x))


def _gelu(x):
    return 0.5 * x * (1.0 + lax.erf(x * (2.0 ** -0.5)))


def _gelu_grad(x):
    return 0.5 * (1.0 + lax.erf(x * (2.0 ** -0.5))) + x * jnp.exp(-0.5 * x * x) * (2.0 * math.pi) ** -0.5


def _ln_hat(h):
    mu = jnp.mean(h, axis=-1, keepdims=True)
    xc = h - mu
    var = jnp.mean(xc * xc, axis=-1, keepdims=True)
    r = lax.rsqrt(var + LN_EPS)
    return xc * r, r


def _ln_bwd(dxhat, xhat, r):
    return r * (dxhat - jnp.mean(dxhat, axis=-1, keepdims=True)
                - xhat * jnp.mean(dxhat * xhat, axis=-1, keepdims=True))


MM_TILE = 1536
MM_WIDE = 2048


def _matmul(a, b, dims, name, out_dtype=F32, add=None, coef=1.0, tm=MM_TILE, tn=MM_TILE, tk=MM_TILE, carried=None,
            ln=None):
    if dims == NN:
        (m, k), n = a.shape, b.shape[1]
    elif dims == NT:
        (m, k), n = a.shape, b.shape[0]
    else:
        (k, m), n = a.shape, b.shape[1]
    tm, tn, tk = _tile(m, tm), _tile(n, tn), _tile(k, tk)
    nk = k // tk
    a_spec = pl.BlockSpec((tk, tm), lambda j, i, q: (q, i)) if dims == TN else pl.BlockSpec((tm, tk), lambda j, i, q: (i, q))
    b_spec = pl.BlockSpec((tn, tk), lambda j, i, q: (j, q)) if dims == NT else pl.BlockSpec((tk, tn), lambda j, i, q: (q, j))
    o_spec = pl.BlockSpec((tm, tn), lambda j, i, q: (i, j))
    has_add = add is not None
    if ln is not None:
        assert n == tn and has_add and carried is None
        return _matmul_ln_bwd(a, b, dims, name, add, coef, ln, a_spec, b_spec, o_spec, (m, n, tm, tn, nk))

    def body(*refs):
        a_ref, b_ref = refs[0], refs[1]
        add_ref = refs[2] if has_add else None
        o_ref, acc_ref = refs[2 + has_add], refs[3 + has_add]
        q = pl.program_id(2)
        part = _dot(a_ref[...], b_ref[...], dims)

        def finish(r):
            if has_add:
                r = r + coef * add_ref[...]
            o_ref[...] = r.astype(out_dtype)

        if nk == 1:
            finish(part)
        else:
            @pl.when(q == 0)
            def _():
                acc_ref[...] = part

            @pl.when(q > 0)
            def _():
                acc_ref[...] += part

            @pl.when(q == nk - 1)
            def _():
                finish(acc_ref[...])

    ins = [a, b] + ([add] if has_add else [])
    in_specs = [a_spec, b_spec] + ([o_spec] if has_add else [])
    grid = (n // tn, m // tm, nk)
    acc = pltpu.VMEM((tm, tn) if nk > 1 else (SUBLANES, LANES), F32)
    out = jax.ShapeDtypeStruct((m, n), out_dtype)
    if carried is None:
        return pl.pallas_call(
            body, name=name, grid=grid, in_specs=in_specs, out_specs=o_spec, out_shape=out, scratch_shapes=[acc],
            compiler_params=_cparams(("parallel", "parallel", "arbitrary")),
        )(*ins)
    res = pl.pallas_call(
        _carrying(body, len(ins), 1, 1, carried, grid), name=name + "_carrying", grid=grid,
        in_specs=in_specs + [ANY] * len(carried.inputs), out_specs=[o_spec] + [ANY] * len(carried.out_shapes),
        out_shape=[out] + carried.out_shapes, scratch_shapes=[acc] + carried.scratch(),
        compiler_params=_cparams(("arbitrary", "arbitrary", "arbitrary")),
    )(*ins, *carried.inputs)
    return res[0], res[1:]


def _matmul_ln_bwd(a, b, dims, name, add, coef, ln, a_spec, b_spec, o_spec, sizes):
    m, n, tm, tn, nk = sizes
    hres, g = ln
    row = pl.BlockSpec((1, n), lambda j, i, q: (0, 0))
    sums = pl.BlockSpec((SUBLANES, n), lambda j, i, q: (0, 0))

    def body(a_ref, b_ref, add_ref, h_ref, g_ref, dh_ref, dhb_ref, dg_ref, db_ref, acc_ref):
        i, q = pl.program_id(1), pl.program_id(2)
        part = _dot(a_ref[...], b_ref[...], dims)

        @pl.when(jnp.logical_and(i == 0, q == 0))
        def _():
            dg_ref[...] = jnp.zeros_like(dg_ref)
            db_ref[...] = jnp.zeros_like(db_ref)

        @pl.when(q == 0)
        def _():
            acc_ref[...] = part

        @pl.when(q > 0)
        def _():
            acc_ref[...] += part

        @pl.when(q == nk - 1)
        def _():
            dy_v = acc_ref[...] + coef * add_ref[...]
            xhat, r = _ln_hat(h_ref[...])
            dh = _ln_bwd(dy_v * g_ref[...], xhat, r)
            dh_ref[...] = dh
            dhb_ref[...] = dh.astype(dhb_ref.dtype)
            dg_ref[...] += _fold8(dy_v * xhat)
            db_ref[...] += _fold8(dy_v)

    return pl.pallas_call(
        body, name=name + "_ln_bwd", grid=(1, m // tm, nk),
        in_specs=[a_spec, b_spec, o_spec, o_spec, row], out_specs=[o_spec, o_spec, sums, sums],
        out_shape=[jax.ShapeDtypeStruct((m, n), F32), jax.ShapeDtypeStruct((m, n), ACT),
                   jax.ShapeDtypeStruct((SUBLANES, n), F32), jax.ShapeDtypeStruct((SUBLANES, n), F32)],
        scratch_shapes=[pltpu.VMEM((tm, tn), F32)],
        compiler_params=_cparams(("arbitrary", "arbitrary", "arbitrary")),
    )(a, b, add, hres, g)


def _conv_taps(cur_ref, halo_ref, first):
    x = cur_ref[...]
    tb = x.shape[0]
    halo = jnp.where(first, 0.0, halo_ref[...])
    xc = jnp.concatenate([halo, x], axis=0)
    return [x] + [pltpu.roll(xc, s, 0)[SUBLANES:SUBLANES + tb] for s in range(1, CONV_K)]


def _conv_fwd(projm, conv_w, d, tb):
    t = projm.shape[0]
    heads = d // DN_DK
    hb = tb // SUBLANES

    def body(cur_ref, halo_ref, w_ref, o_ref):
        i, j = pl.program_id(0), pl.program_id(1)
        taps = _conv_taps(cur_ref, halo_ref, i == 0)
        y = taps[0] * w_ref[CONV_K - 1:CONV_K, :]
        for s in range(1, CONV_K):
            y = y + taps[s] * w_ref[CONV_K - 1 - s:CONV_K - s, :]
        act = y * _sigmoid(y)
        scale = jnp.where(j == 0, DN_DK ** -0.5, 1.0)
        for h in range(heads):
            seg = act[:, h * DN_DK:(h + 1) * DN_DK]
            r = lax.rsqrt(jnp.sum(seg * seg, axis=1, keepdims=True) + RMS_EPS) * scale
            o_ref[:, h * DN_DK:(h + 1) * DN_DK] = seg * jnp.where(j < 2, r, 1.0)

    blk = pl.BlockSpec((tb, d), lambda i, j: (i, j))
    return pl.pallas_call(
        body, name="conv_fwd", grid=(t // tb, 3),
        in_specs=[blk,
                  pl.BlockSpec((SUBLANES, d), lambda i, j: (jnp.maximum(i * hb - 1, 0), j)),
                  pl.BlockSpec((CONV_K, d), lambda i, j: (0, j))],
        out_specs=blk,
        out_shape=jax.ShapeDtypeStruct((t, 3 * d), F32),
        compiler_params=_cparams(("parallel", "parallel")),
    )(projm, projm, conv_w)


def _conv_bwd_dy(projm, conv_w, dqkv, d, tb):
    t = projm.shape[0]
    heads = d // DN_DK
    hb = tb // SUBLANES

    def body(cur_ref, halo_ref, w_ref, dout_ref, dy_ref, dw_ref):
        j, i = pl.program_id(0), pl.program_id(1)
        taps = _conv_taps(cur_ref, halo_ref, i == 0)
        y = taps[0] * w_ref[CONV_K - 1:CONV_K, :]
        for s in range(1, CONV_K):
            y = y + taps[s] * w_ref[CONV_K - 1 - s:CONV_K - s, :]
        sg = _sigmoid(y)
        act = y * sg
        dact = sg * (1.0 + y * (1.0 - sg))
        scale = jnp.where(j == 0, DN_DK ** -0.5, 1.0)
        for h in range(heads):
            cols = slice(h * DN_DK, (h + 1) * DN_DK)
            seg = act[:, cols]
            r = lax.rsqrt(jnp.sum(seg * seg, axis=1, keepdims=True) + RMS_EPS)
            nrm = seg * r
            dout = dout_ref[:, cols]
            ds = jnp.where(j < 2, (r * scale) * (dout - nrm * jnp.sum(dout * nrm, axis=1, keepdims=True)), dout)
            dy_ref[:, cols] = ds * dact[:, cols]
        dy = dy_ref[...]

        @pl.when(i == 0)
        def _():
            dw_ref[...] = jnp.zeros_like(dw_ref)

        for s in range(CONV_K):
            dw_ref[CONV_K - 1 - s] += _fold8(dy * taps[s])

    return pl.pallas_call(
        body, name="conv_bwd_dy", grid=(3, t // tb),
        in_specs=[pl.BlockSpec((tb, d), lambda j, i: (i, j)),
                  pl.BlockSpec((SUBLANES, d), lambda j, i: (jnp.maximum(i * hb - 1, 0), j)),
                  pl.BlockSpec((CONV_K, d), lambda j, i: (0, j)),
                  pl.BlockSpec((tb, d), lambda j, i: (i, j))],
        out_specs=[pl.BlockSpec((tb, d), lambda j, i: (i, j)),
                   pl.BlockSpec((CONV_K, SUBLANES, d), lambda j, i: (0, 0, j))],
        out_shape=[jax.ShapeDtypeStruct((t, 3 * d), F32),
                   jax.ShapeDtypeStruct((CONV_K, SUBLANES, 3 * d), F32)],
        compiler_params=_cparams(("parallel", "arbitrary")),
    )(projm, projm, conv_w, dqkv)


def _conv_bwd_dx(dy, conv_w, dprojm, d, tb):
    t = dy.shape[0]
    hb = tb // SUBLANES
    last = t // tb - 1

    def body(cur_ref, halo_ref, w_ref, alias_ref, o_ref):
        i = pl.program_id(0)
        cur = cur_ref[...]
        halo = jnp.where(i == last, 0.0, halo_ref[...])
        dc = jnp.concatenate([cur, halo], axis=0)
        acc = cur * w_ref[CONV_K - 1:CONV_K, :]
        for s in range(1, CONV_K):
            acc = acc + pltpu.roll(dc, tb + SUBLANES - s, 0)[:tb] * w_ref[CONV_K - 1 - s:CONV_K - s, :]
        o_ref[...] = acc.astype(o_ref.dtype)

    return pl.pallas_call(
        body, name="conv_bwd_dx", grid=(t // tb, 3),
        in_specs=[pl.BlockSpec((tb, d), lambda i, j: (i, j)),
                  pl.BlockSpec((SUBLANES, d), lambda i, j: (jnp.minimum((i + 1) * hb, t // SUBLANES - 1), j)),
                  pl.BlockSpec((CONV_K, d), lambda i, j: (0, j)),
                  ANY],
        out_specs=pl.BlockSpec((tb, d), lambda i, j: (i, j)),
        out_shape=jax.ShapeDtypeStruct(dprojm.shape, dprojm.dtype),
        input_output_aliases={3: 0},
        compiler_params=_cparams(("parallel", "parallel")),
    )(dy, dy, conv_w, dprojm)


def _beta_g(ba, alog, dtb):
    beta = _sigmoid(ba[:, :LANES])
    xa = ba[:, LANES:] + dtb
    softplus = jnp.maximum(xa, 0.0) + jnp.log(1.0 + jnp.exp(-jnp.abs(xa)))
    ea = jnp.exp(alog)
    return beta, -ea * softplus, ea, _sigmoid(xa)


def _inv_corrections(mats):
    ys = [-a for a in mats]
    ps = [_dot(a, a) for a in mats]
    steps = int(math.log2(CHUNK)) - 1
    for it in range(steps):
        ys = [y + p + _dot(y, p) for y, p in zip(ys, ps)]
        if it < steps - 1:
            ps = [_dot(p, p) for p in ps]
    return ys


def _chunk_masks():
    row = lax.broadcasted_iota(jnp.int32, (CHUNK, CHUNK), 0)
    col = lax.broadcasted_iota(jnp.int32, (CHUNK, CHUNK), 1)
    return row >= col, row > col, row <= col


def _col_of(mat, lane_idx, h):
    return jnp.sum(jnp.where(lane_idx == h, mat, 0.0), axis=1, keepdims=True)


def _row_of(mat, sub_idx, h):
    return jnp.sum(jnp.where(sub_idx == h, mat, 0.0), axis=0, keepdims=True)


def _carrying(compute, n_in, n_out, n_scratch, carried, grid):
    if carried is None:
        return compute
    ci, co = len(carried.inputs), len(carried.out_shapes)

    def body(*refs):
        ins, c_in = refs[:n_in], refs[n_in:n_in + ci]
        outs, c_out = refs[n_in + ci:n_in + ci + n_out], refs[n_in + ci + n_out:n_in + ci + n_out + co]
        scratch = refs[n_in + ci + n_out + co:]
        start, finish = carried.copies(c_in, c_out, scratch[n_scratch], scratch[n_scratch + 1])
        first, last = True, True
        for axis, steps in enumerate(grid):
            first = jnp.logical_and(first, pl.program_id(axis) == 0)
            last = jnp.logical_and(last, pl.program_id(axis) == steps - 1)

        @pl.when(first)
        def _():
            start()

        compute(*ins, *outs, *scratch[:n_scratch])

        @pl.when(last)
        def _():
            finish()

    return body


def _dn_fwd(qkv, projm, alog, dtb, d, carried=None):
    t = qkv.shape[0]
    ba_block = 8 * d // (2 * LANES)
    heads = d // DN_DK
    n_chunks = t // CHUNK
    grp = DN_GROUP_FWD if n_chunks % DN_GROUP_FWD == 0 else 1
    span = grp * CHUNK
    extra = carried or _Carried([], [], 0, None)

    def compute(qkv_ref, ba_ref, al_ref, dt_ref, o_ref, s_ref, y_ref, state):
        @pl.when(pl.program_id(0) == 0)
        def _():
            state[...] = jnp.zeros_like(state)

        tril, strict, _ = _chunk_masks()
        beta, g, _, _ = _beta_g(ba_ref[...], al_ref[...], dt_ref[...])
        lane = lax.broadcasted_iota(jnp.int32, (CHUNK, LANES), 1)
        sub = lax.broadcasted_iota(jnp.int32, (LANES, CHUNK), 0)
        rowc = lax.broadcasted_iota(jnp.int32, (CHUNK, 1), 0)
        hs = range(heads)
        units = [(c, h) for c in range(grp) for h in hs]
        un = range(len(units))
        rows = lambda c: slice(c * CHUNK, (c + 1) * CHUNK)
        gc = [_dot(jnp.where(tril, 1.0, 0.0), g[rows(c)], NN, HIGHEST) for c in range(grp)]
        gct = [m.T for m in gc]
        q = [qkv_ref[rows(c), h * DN_DK:(h + 1) * DN_DK] for c, h in units]
        k = [qkv_ref[rows(c), d + h * DN_DK:d + (h + 1) * DN_DK] for c, h in units]
        v = [qkv_ref[rows(c), 2 * d + h * DN_DK:2 * d + (h + 1) * DN_DK] for c, h in units]
        gch = [_col_of(gc[c], lane, h) for c, h in units]
        bh = [_col_of(beta[rows(c)], lane, h) for c, h in units]
        dec = [jnp.where(tril, jnp.exp(gch[n] - _row_of(gct[c], sub, h)), 0.0) for n, (c, h) in enumerate(units)]
        egc = [jnp.exp(gch[n]) for n in un]
        gl = [jnp.sum(jnp.where(rowc == CHUNK - 1, gch[n], 0.0), axis=0, keepdims=True) for n in un]
        kb = [k[n] * bh[n] for n in un]
        a = [jnp.where(strict, _dot(kb[n], k[n], NT) * dec[n], 0.0) for n in un]
        p = [_dot(q[n], k[n], NT) * dec[n] for n in un]
        ycor = _inv_corrections(a)
        rhs = [jnp.concatenate([v[n] * bh[n], kb[n] * egc[n]], axis=1) for n in un]
        sol = [rhs[n] + _dot(ycor[n], rhs[n]) for n in un]
        qg = [q[n] * egc[n] for n in un]
        kd = [k[n] * jnp.exp(gl[n] - gch[n]) for n in un]
        egl = [jnp.exp(gl[n]) for n in un]
        s_cur, s_in, o = [state[h] for h in hs], [], []
        for c in range(grp):
            ns = [c * heads + h for h in hs]
            vn = [sol[n][:, :DN_DK] - _dot(sol[n][:, DN_DK:], s_cur[h]) for h, n in enumerate(ns)]
            o += [_dot(qg[n], s_cur[h]) + _dot(p[n], vn[h]) for h, n in enumerate(ns)]
            s_in += s_cur
            s_cur = [s_cur[h] * egl[n] + _dot(kd[n], vn[h], TN) for h, n in enumerate(ns)]
        for n, (c, h) in enumerate(units):
            o_ref[rows(c), h * DN_DK:(h + 1) * DN_DK] = o[n]
            s_ref[c, h] = s_in[n]
            y_ref[h, rows(c), :] = ycor[n]
        for h in hs:
            state[h] = s_cur[h]

    res = pl.pallas_call(
        _carrying(compute, 4, 3, 1, carried, (n_chunks // grp,)),
        name="dn_fwd_carrying" if carried else "dn_fwd", grid=(n_chunks // grp,),
        in_specs=[pl.BlockSpec((span, 3 * d), lambda i: (i, 0)),
                  pl.BlockSpec((span, 2 * LANES), lambda i: (i, ba_block)),
                  pl.BlockSpec((1, LANES), lambda i: (0, 0)),
                  pl.BlockSpec((1, LANES), lambda i: (0, 0))] + [ANY] * len(extra.inputs),
        out_specs=[pl.BlockSpec((span, d), lambda i: (i, 0)),
                   pl.BlockSpec((grp, heads, DN_DK, DN_DK), lambda i: (i, 0, 0, 0)),
                   pl.BlockSpec((heads, span, CHUNK), lambda i: (0, i, 0))] + [ANY] * len(extra.out_shapes),
        out_shape=[jax.ShapeDtypeStruct((t, d), F32),
                   jax.ShapeDtypeStruct((n_chunks, heads, DN_DK, DN_DK), F32),
                   jax.ShapeDtypeStruct((heads, t, CHUNK), F32)] + extra.out_shapes,
        scratch_shapes=[pltpu.VMEM((heads, DN_DK, DN_DK), F32)] + (extra.scratch() if carried else []),
        compiler_params=_cparams(("arbitrary",)),
    )(qkv, projm, alog, dtb, *extra.inputs)
    return res[:3], res[3:]


def _dn_bwd(qkv, projm, alog, dtb, dout, states, ycors, dprojm, d, carried=None):
    t = qkv.shape[0]
    ba_block = 8 * d // (2 * LANES)
    heads = d // DN_DK
    n_chunks = t // CHUNK
    grp = DN_GROUP if n_chunks % DN_GROUP == 0 else 1
    span = grp * CHUNK
    rev = lambda i: n_chunks // grp - 1 - i
    extra = carried or _Carried([], [], 0, None)

    def compute(qkv_ref, ba_ref, al_ref, dt_ref, do_ref, s_ref, y_ref, alias_ref,
                dqkv_ref, dba_ref, dal_ref, ddt_ref, dstate):
        @pl.when(pl.program_id(0) == 0)
        def _():
            dstate[...] = jnp.zeros_like(dstate)
            dal_ref[...] = jnp.zeros_like(dal_ref)
            ddt_ref[...] = jnp.zeros_like(ddt_ref)

        tril, strict, triu = _chunk_masks()
        beta, g, ea, sig_a = _beta_g(ba_ref[...], al_ref[...], dt_ref[...])
        lane = lax.broadcasted_iota(jnp.int32, (CHUNK, LANES), 1)
        sub = lax.broadcasted_iota(jnp.int32, (LANES, CHUNK), 0)
        rowc = lax.broadcasted_iota(jnp.int32, (CHUNK, 1), 0)
        hs = range(heads)
        units = [(c, h) for c in range(grp) for h in hs]
        un = range(len(units))
        rows = lambda c: slice(c * CHUNK, (c + 1) * CHUNK)
        rsum = lambda x_: jnp.sum(x_, axis=1, keepdims=True)
        gc = [_dot(jnp.where(tril, 1.0, 0.0), g[rows(c)], NN, HIGHEST) for c in range(grp)]
        gct = [m.T for m in gc]
        q = [qkv_ref[rows(c), h * DN_DK:(h + 1) * DN_DK] for c, h in units]
        k = [qkv_ref[rows(c), d + h * DN_DK:d + (h + 1) * DN_DK] for c, h in units]
        v = [qkv_ref[rows(c), 2 * d + h * DN_DK:2 * d + (h + 1) * DN_DK] for c, h in units]
        dout_h = [do_ref[rows(c), h * DN_DK:(h + 1) * DN_DK] for c, h in units]
        s0 = [s_ref[c, h] for c, h in units]
        ycor = [y_ref[h, rows(c), :] for c, h in units]
        gch = [_col_of(gc[c], lane, h) for c, h in units]
        bh = [_col_of(beta[rows(c)], lane, h) for c, h in units]
        dec = [jnp.where(tril, jnp.exp(gch[n] - _row_of(gct[c], sub, h)), 0.0) for n, (c, h) in enumerate(units)]
        egc = [jnp.exp(gch[n]) for n in un]
        gl = [jnp.sum(jnp.where(rowc == CHUNK - 1, gch[n], 0.0), axis=0, keepdims=True) for n in un]
        egl = [jnp.exp(gl[n]) for n in un]
        ekd = [jnp.exp(gl[n] - gch[n]) for n in un]
        kb = [k[n] * bh[n] for n in un]
        kd = [k[n] * ekd[n] for n in un]
        qg = [q[n] * egc[n] for n in un]
        kbg = [kb[n] * egc[n] for n in un]
        a = [jnp.where(strict, _dot(kb[n], k[n], NT) * dec[n], 0.0) for n in un]
        p = [_dot(q[n], k[n], NT) * dec[n] for n in un]
        rhs = [jnp.concatenate([v[n] * bh[n], kbg[n]], axis=1) for n in un]
        sol = [rhs[n] + _dot(ycor[n], rhs[n]) for n in un]
        w = [sol[n][:, DN_DK:] for n in un]
        vn = [sol[n][:, :DN_DK] - _dot(w[n], s0[n]) for n in un]
        dqg = [_dot(dout_h[n], s0[n], NT) for n in un]
        dp = [jnp.where(tril, _dot(dout_h[n], vn[n], NT), 0.0) for n in un]
        pdo = [_dot(p[n], dout_h[n], TN) for n in un]
        qdo = [_dot(qg[n], dout_h[n], TN) for n in un]
        ds_cur = [dstate[h] for h in hs]
        dsn, dvn = [None] * len(units), [None] * len(units)
        for c in reversed(range(grp)):
            for h in hs:
                dsn[c * heads + h] = ds_cur[h]
            for h in hs:
                n = c * heads + h
                dvn[n] = pdo[n] + _dot(kd[n], ds_cur[h])
            ds_cur = [qdo[c * heads + h] + egl[c * heads + h] * ds_cur[h]
                      - _dot(w[c * heads + h], dvn[c * heads + h], TN) for h in hs]
        dkd = [_dot(vn[n], dsn[n], NT) for n in un]
        dw = [-_dot(dvn[n], s0[n], NT) for n in un]
        dgl = [jnp.sum(rsum(dsn[n] * s0[n]), axis=0, keepdims=True) * egl[n] for n in un]
        dsol = [jnp.concatenate([dvn[n], dw[n]], axis=1) for n in un]
        drhs = [dsol[n] + _dot(ycor[n], dsol[n], TN) for n in un]
        dvb = [drhs[n][:, :DN_DK] for n in un]
        dkbg = [drhs[n][:, DN_DK:] for n in un]
        da = [jnp.where(strict, -_dot(drhs[n], sol[n], NT), 0.0) for n in un]
        dma = [da[n] * dec[n] for n in un]
        dmp = [dp[n] * dec[n] for n in un]
        dkb = [_dot(dma[n], k[n]) + dkbg[n] * egc[n] for n in un]
        dq = [_dot(dmp[n], k[n]) + dqg[n] * egc[n] for n in un]
        dk = [_dot(dma[n], kb[n], TN) + _dot(dmp[n], q[n], TN) + dkd[n] * ekd[n] + dkb[n] * bh[n] for n in un]
        e = [da[n] * a[n] + dp[n] * p[n] for n in un]
        colsum = [jnp.sum(e[n], axis=0, keepdims=True) for n in un]
        tkd = [rsum(dkd[n] * kd[n]) for n in un]
        for n, (c, h) in enumerate(units):
            dqkv_ref[rows(c), h * DN_DK:(h + 1) * DN_DK] = dq[n]
            dqkv_ref[rows(c), d + h * DN_DK:d + (h + 1) * DN_DK] = dk[n]
            dqkv_ref[rows(c), 2 * d + h * DN_DK:2 * d + (h + 1) * DN_DK] = dvb[n] * bh[n]
        for h in hs:
            dstate[h] = ds_cur[h]
        valid = lane < heads
        dal_acc = jnp.zeros((SUBLANES, LANES), F32)
        ddt_acc = jnp.zeros((SUBLANES, LANES), F32)
        for c in range(grp):
            dgc_all = jnp.zeros((CHUNK, LANES), F32)
            dbeta_all = jnp.zeros((CHUNK, LANES), F32)
            colsums = jnp.zeros((LANES, CHUNK), F32)
            for h in hs:
                n = c * heads + h
                dgc = rsum(e[n]) + rsum(dqg[n] * qg[n]) - tkd[n] + rsum(dkbg[n] * kbg[n])
                dgc = dgc + jnp.where(rowc == CHUNK - 1, dgl[n] + jnp.sum(tkd[n], axis=0, keepdims=True), 0.0)
                dgc_all = dgc_all + jnp.where(lane == h, dgc, 0.0)
                colsums = colsums + jnp.where(sub == h, colsum[n], 0.0)
                dbeta_all = dbeta_all + jnp.where(lane == h, rsum(dkb[n] * k[n]) + rsum(dvb[n] * v[n]), 0.0)
            dg = _dot(jnp.where(triu, 1.0, 0.0), dgc_all - colsums.T, NN, HIGHEST)
            beta_c = beta[rows(c)]
            dbl = jnp.where(valid, dbeta_all * beta_c * (1.0 - beta_c), 0.0)
            dal = jnp.where(valid, -dg * ea * sig_a[rows(c)], 0.0)
            dba_ref[rows(c), :LANES] = dbl.astype(dba_ref.dtype)
            dba_ref[rows(c), LANES:] = dal.astype(dba_ref.dtype)
            dal_acc = dal_acc + _fold8(jnp.where(valid, dg * g[rows(c)], 0.0))
            ddt_acc = ddt_acc + _fold8(dal)
        dal_ref[...] += dal_acc
        ddt_ref[...] += ddt_acc

    res = pl.pallas_call(
        _carrying(compute, 8, 4, 1, carried, (n_chunks // grp,)),
        name="dn_bwd_carrying" if carried else "dn_bwd", grid=(n_chunks // grp,),
        in_specs=[pl.BlockSpec((span, 3 * d), lambda i: (rev(i), 0)),
                  pl.BlockSpec((span, 2 * LANES), lambda i: (rev(i), ba_block)),
                  pl.BlockSpec((1, LANES), lambda i: (0, 0)),
                  pl.BlockSpec((1, LANES), lambda i: (0, 0)),
                  pl.BlockSpec((span, d), lambda i: (rev(i), 0)),
                  pl.BlockSpec((grp, heads, DN_DK, DN_DK), lambda i: (rev(i), 0, 0, 0)),
                  pl.BlockSpec((heads, span, CHUNK), lambda i: (0, rev(i), 0)),
                  ANY] + [ANY] * len(extra.inputs),
        out_specs=[pl.BlockSpec((span, 3 * d), lambda i: (rev(i), 0)),
                   pl.BlockSpec((span, 2 * LANES), lambda i: (rev(i), ba_block)),
                   pl.BlockSpec((SUBLANES, LANES), lambda i: (0, 0)),
                   pl.BlockSpec((SUBLANES, LANES), lambda i: (0, 0))] + [ANY] * len(extra.out_shapes),
        out_shape=[jax.ShapeDtypeStruct((t, 3 * d), F32),
                   jax.ShapeDtypeStruct(dprojm.shape, dprojm.dtype),
                   jax.ShapeDtypeStruct((SUBLANES, LANES), F32),
                   jax.ShapeDtypeStruct((SUBLANES, LANES), F32)] + extra.out_shapes,
        input_output_aliases={7: 1},
        scratch_shapes=[pltpu.VMEM((heads, DN_DK, DN_DK), F32)] + (extra.scratch() if carried else []),
        compiler_params=_cparams(("arbitrary",)),
    )(qkv, projm, alog, dtb, dout, states, ycors, dprojm, *extra.inputs)
    return res[:4], res[4:]


def _sgu_mask():
    row = lax.broadcasted_iota(jnp.int32, (SGU_BLOCK, SGU_BLOCK), 0)
    col = lax.broadcasted_iota(jnp.int32, (SGU_BLOCK, SGU_BLOCK), 1)
    sh = int(math.log2(CHUNK))
    return lax.shift_right_logical(row, sh) >= lax.shift_right_logical(col, sh)


def _gate_sgu_fwd(o, projm, onw, lng, lnb, ws, bst, d):
    t = o.shape[0]
    heads, groups = d // DN_DK, d // SGU_GROUP_DIM
    tb = _tile(t, SGU_WINDOWS * SGU_BLOCK, SGU_BLOCK)
    row_spec = pl.BlockSpec((1, d), lambda i: (0, 0))

    def body(o_ref, z_ref, u_ref, v_ref, onw_ref, lng_ref, lnb_ref, ws_ref, bst_ref, ya_ref, yb_ref):
        for h in range(heads):
            cols = slice(h * DN_DK, (h + 1) * DN_DK)
            oh, zh = o_ref[:, cols], z_ref[:, cols]
            r = lax.rsqrt(jnp.mean(oh * oh, axis=1, keepdims=True) + RMS_EPS)
            ya_ref[:, cols] = (oh * r * onw_ref[:, cols] * (zh * _sigmoid(zh))).astype(ya_ref.dtype)
        xhat, _ = _ln_hat(_gelu(v_ref[...]))
        vgn = xhat * lng_ref[...] + lnb_ref[...]
        mask = _sgu_mask()
        lane = lax.broadcasted_iota(jnp.int32, (SGU_BLOCK, LANES), 1)
        bst_v = bst_ref[...]
        for gi in range(groups):
            cols = slice(gi * SGU_GROUP_DIM, (gi + 1) * SGU_GROUP_DIM)
            wsg = jnp.where(mask, ws_ref[gi], 0.0)
            bias = _col_of(bst_v, lane, gi)
            for win in range(tb // SGU_BLOCK):
                rows = slice(win * SGU_BLOCK, (win + 1) * SGU_BLOCK)
                sp = _dot(wsg, vgn[rows, cols]) + bias
                yb_ref[rows, cols] = (_gelu(u_ref[rows, cols]) * sp).astype(yb_ref.dtype)

    return pl.pallas_call(
        body, name="gate_sgu_fwd", grid=(t // tb,),
        in_specs=[pl.BlockSpec((tb, d), lambda i: (i, 0)),
                  pl.BlockSpec((tb, d), lambda i: (i, 3)),
                  pl.BlockSpec((tb, d), lambda i: (i, 4)),
                  pl.BlockSpec((tb, d), lambda i: (i, 5)),
                  row_spec, row_spec, row_spec,
                  pl.BlockSpec((groups, SGU_BLOCK, SGU_BLOCK), lambda i: (0, 0, 0)),
                  pl.BlockSpec((SGU_BLOCK, LANES), lambda i: (0, 0))],
        out_specs=[pl.BlockSpec((tb, d), lambda i: (i, 0)), pl.BlockSpec((tb, d), lambda i: (i, 0))],
        out_shape=[jax.ShapeDtypeStruct((t, d), ACT), jax.ShapeDtypeStruct((t, d), ACT)],
        compiler_params=_cparams(("parallel",)),
    )(o, projm, projm, projm, onw, lng, lnb, ws, bst)


def _gate_sgu_bwd(dya, dyb, o, projm, onw, lng, lnb, ws, bst, dprojm, d):
    t = o.shape[0]
    heads, groups = d // DN_DK, d // SGU_GROUP_DIM
    tb = _tile(t, SGU_WINDOWS * SGU_BLOCK, SGU_BLOCK)
    row_spec = pl.BlockSpec((1, d), lambda i: (0, 0))
    acc_row = pl.BlockSpec((SUBLANES, d), lambda i: (0, 0))

    def body(dya_ref, dyb_ref, o_ref, z_ref, u_ref, v_ref, onw_ref, lng_ref, lnb_ref, ws_ref, bst_ref, alias_ref,
             do_ref, dp_ref, donw_ref, dlng_ref, dlnb_ref, dws_ref, dbst_ref):
        @pl.when(pl.program_id(0) == 0)
        def _():
            for r_ in (donw_ref, dlng_ref, dlnb_ref, dws_ref, dbst_ref):
                r_[...] = jnp.zeros_like(r_)

        donw = jnp.zeros((SUBLANES, DN_DK), F32)
        for h in range(heads):
            cols = slice(h * DN_DK, (h + 1) * DN_DK)
            oh, zh, dyah, wh = o_ref[:, cols], z_ref[:, cols], dya_ref[:, cols], onw_ref[:, cols]
            r = lax.rsqrt(jnp.mean(oh * oh, axis=1, keepdims=True) + RMS_EPS)
            on = oh * r
            sz = _sigmoid(zh)
            silu_z = zh * sz
            don = dyah * wh * silu_z
            dp_ref[:, cols] = (dyah * on * wh * (sz * (1.0 + zh * (1.0 - sz)))).astype(dp_ref.dtype)
            donw = donw + _fold8(dyah * on * silu_z)
            do_ref[:, cols] = r * (don - on * jnp.mean(don * on, axis=1, keepdims=True))
        donw_ref[...] += donw

        vgp, up = v_ref[...], u_ref[...]
        xhat, rstd = _ln_hat(_gelu(vgp))
        lng_v = lng_ref[...]
        vgn = xhat * lng_v + lnb_ref[...]
        ua = _gelu(up)
        mask = _sgu_mask()
        lane = lax.broadcasted_iota(jnp.int32, (SGU_BLOCK, LANES), 1)
        bst_v = bst_ref[...]
        dbst = jnp.zeros((SGU_BLOCK, LANES), F32)
        dvgn_parts, dua_parts = [], []
        for gi in range(groups):
            cols = slice(gi * SGU_GROUP_DIM, (gi + 1) * SGU_GROUP_DIM)
            wsg = jnp.where(mask, ws_ref[gi], 0.0)
            bias = _col_of(bst_v, lane, gi)
            dws = jnp.zeros((SGU_BLOCK, SGU_BLOCK), F32)
            dvgn_g, dua_g = [], []
            for win in range(tb // SGU_BLOCK):
                rows = slice(win * SGU_BLOCK, (win + 1) * SGU_BLOCK)
                vg_g, dyb_g = vgn[rows, cols], dyb_ref[rows, cols]
                sp = _dot(wsg, vg_g) + bias
                dsp = dyb_g * ua[rows, cols]
                dua_g.append(dyb_g * sp)
                dws = dws + _dot(dsp, vg_g, NT)
                dbst = dbst + jnp.where(lane == gi, jnp.sum(dsp, axis=1, keepdims=True), 0.0)
                dvgn_g.append(_dot(wsg, dsp, TN))
            dws_ref[gi] += jnp.where(mask, dws, 0.0)
            dvgn_parts.append(jnp.concatenate(dvgn_g, axis=0))
            dua_parts.append(jnp.concatenate(dua_g, axis=0))
        dbst_ref[...] += dbst
        dvgn = jnp.concatenate(dvgn_parts, axis=1)
        dua = jnp.concatenate(dua_parts, axis=1)
        dlng_ref[...] += _fold8(dvgn * xhat)
        dlnb_ref[...] += _fold8(dvgn)
        dvga = _ln_bwd(dvgn * lng_v, xhat, rstd)
        dp_ref[:, d:2 * d] = (dua * _gelu_grad(up)).astype(dp_ref.dtype)
        dp_ref[:, 2 * d:] = (dvga * _gelu_grad(vgp)).astype(dp_ref.dtype)

    return pl.pallas_call(
        body, name="gate_sgu_bwd", grid=(t // tb,),
        in_specs=[pl.BlockSpec((tb, d), lambda i: (i, 0)),
                  pl.BlockSpec((tb, d), lambda i: (i, 0)),
                  pl.BlockSpec((tb, d), lambda i: (i, 0)),
                  pl.BlockSpec((tb, d), lambda i: (i, 3)),
                  pl.BlockSpec((tb, d), lambda i: (i, 4)),
                  pl.BlockSpec((tb, d), lambda i: (i, 5)),
                  row_spec, row_spec, row_spec,
                  pl.BlockSpec((groups, SGU_BLOCK, SGU_BLOCK), lambda i: (0, 0, 0)),
                  pl.BlockSpec((SGU_BLOCK, LANES), lambda i: (0, 0)),
                  ANY],
        out_specs=[pl.BlockSpec((tb, d), lambda i: (i, 0)),
                   pl.BlockSpec((tb, 3 * d), lambda i: (i, 1)),
                   pl.BlockSpec((SUBLANES, DN_DK), lambda i: (0, 0)),
                   acc_row, acc_row,
                   pl.BlockSpec((groups, SGU_BLOCK, SGU_BLOCK), lambda i: (0, 0, 0)),
                   pl.BlockSpec((SGU_BLOCK, LANES), lambda i: (0, 0))],
        out_shape=[jax.ShapeDtypeStruct((t, d), F32),
                   jax.ShapeDtypeStruct(dprojm.shape, dprojm.dtype),
                   jax.ShapeDtypeStruct((SUBLANES, DN_DK), F32),
                   jax.ShapeDtypeStruct((SUBLANES, d), F32),
                   jax.ShapeDtypeStruct((SUBLANES, d), F32),
                   jax.ShapeDtypeStruct((groups, SGU_BLOCK, SGU_BLOCK), F32),
                   jax.ShapeDtypeStruct((SGU_BLOCK, LANES), F32)],
        input_output_aliases={11: 1},
        compiler_params=_cparams(("arbitrary",)),
    )(dya, dyb, o, projm, projm, projm, onw, lng, lnb, ws, bst, dprojm)


def _mix_fwd(ya, yb, projm, x, wpa, wpb, wo, g1, b1, d, tb):
    t = x.shape[0]
    blk = pl.BlockSpec((tb, d), lambda i: (i, 0))
    wspec = pl.BlockSpec((d, d), lambda i: (0, 0))
    row_spec = pl.BlockSpec((1, d), lambda i: (0, 0))

    def body(ya_ref, yb_ref, ga_ref, gb_ref, x_ref, wpa_ref, wpb_ref, wo_ref, g_ref, b_ref,
             pa_ref, pb_ref, m_ref, h_ref, x1_ref, x1b_ref):
        pa = _dot(ya_ref[...], wpa_ref[...])
        pb = _dot(yb_ref[...], wpb_ref[...])
        m = _sigmoid(ga_ref[...]) * pa + _sigmoid(gb_ref[...]) * pb
        hres = ALPHA * x_ref[...] + _dot(m, wo_ref[...])
        xhat, _ = _ln_hat(hres)
        x1 = xhat * g_ref[...] + b_ref[...]
        pa_ref[...] = pa
        pb_ref[...] = pb
        m_ref[...] = m.astype(m_ref.dtype)
        h_ref[...] = hres
        x1_ref[...] = x1
        x1b_ref[...] = x1.astype(x1b_ref.dtype)

    f32_out = jax.ShapeDtypeStruct((t, d), F32)
    bf_out = jax.ShapeDtypeStruct((t, d), ACT)
    return pl.pallas_call(
        body, name="mix_fwd", grid=(t // tb,),
        in_specs=[blk, blk, pl.BlockSpec((tb, d), lambda i: (i, 6)), pl.BlockSpec((tb, d), lambda i: (i, 7)),
                  blk, wspec, wspec, wspec, row_spec, row_spec],
        out_specs=[blk] * 6,
        out_shape=[f32_out, f32_out, bf_out, f32_out, f32_out, bf_out],
        compiler_params=_cparams(("parallel",)),
    )(ya, yb, projm, projm, x, wpa, wpb, wo, g1, b1)


def _mix_bwd(dmix, pa, pb, projm, wpa, wpb, wo, d, tb):
    t = dmix.shape[0]
    blk = pl.BlockSpec((tb, d), lambda i: (i, 0))
    wspec = pl.BlockSpec((d, d), lambda i: (0, 0))

    def body(dmix_ref, pa_ref, pb_ref, ga_ref, gb_ref, wpa_ref, wpb_ref, wo_ref,
             dpa_ref, dpb_ref, dya_ref, dyb_ref, dg_ref):
        dm = _dot(dmix_ref[...], wo_ref[...], NT)
        sa, sb = _sigmoid(ga_ref[...]), _sigmoid(gb_ref[...])
        dpa, dpb = dm * sa, dm * sb
        dpa_ref[...] = dpa.astype(dpa_ref.dtype)
        dpb_ref[...] = dpb.astype(dpb_ref.dtype)
        dg_ref[:, :d] = (dm * pa_ref[...] * sa * (1.0 - sa)).astype(dg_ref.dtype)
        dg_ref[:, d:] = (dm * pb_ref[...] * sb * (1.0 - sb)).astype(dg_ref.dtype)
        dya_ref[...] = _dot(dpa, wpa_ref[...], NT)
        dyb_ref[...] = _dot(dpb, wpb_ref[...], NT)

    return pl.pallas_call(
        body, name="mix_bwd", grid=(t // tb,),
        in_specs=[blk, blk, blk, pl.BlockSpec((tb, d), lambda i: (i, 6)), pl.BlockSpec((tb, d), lambda i: (i, 7)),
                  wspec, wspec, wspec],
        out_specs=[blk, blk, blk, blk, pl.BlockSpec((tb, 2 * d), lambda i: (i, 3))],
        out_shape=[jax.ShapeDtypeStruct((t, d), ACT), jax.ShapeDtypeStruct((t, d), ACT),
                   jax.ShapeDtypeStruct((t, d), F32), jax.ShapeDtypeStruct((t, d), F32),
                   jax.ShapeDtypeStruct((t, 8 * d + 2 * LANES), ACT)],
        compiler_params=_cparams(("parallel",)),
    )(dmix, pa, pb, projm, projm, wpa, wpb, wo)


def _ffn_tail_fwd(gu, wd, x1, g, b, tb):
    t, d = x1.shape
    f = wd.shape[0]
    fc = _tile(f, MM_TILE)
    blk = pl.BlockSpec((tb, d), lambda i: (i, 0))
    row_spec = pl.BlockSpec((1, d), lambda i: (0, 0))

    def body(gu_ref, wd_ref, x_ref, g_ref, b_ref, a_ref, h_ref, y_ref, yb_ref):
        ffn = jnp.zeros((tb, d), F32)
        for c in range(f // fc):
            gp = gu_ref[:, c * fc:(c + 1) * fc].astype(F32)
            act = (gp * _sigmoid(gp) * gu_ref[:, f + c * fc:f + (c + 1) * fc].astype(F32)).astype(a_ref.dtype)
            a_ref[:, c * fc:(c + 1) * fc] = act
            ffn = ffn + _dot(act, wd_ref[c * fc:(c + 1) * fc, :])
        hres = ALPHA * x_ref[...] + ffn
        xhat, _ = _ln_hat(hres)
        y = xhat * g_ref[...] + b_ref[...]
        h_ref[...] = hres
        y_ref[...] = y
        yb_ref[...] = y.astype(yb_ref.dtype)

    return pl.pallas_call(
        body, name="ffn_tail_fwd", grid=(t // tb,),
        in_specs=[pl.BlockSpec((tb, 2 * f), lambda i: (i, 0)), pl.BlockSpec((f, d), lambda i: (0, 0)),
                  blk, row_spec, row_spec],
        out_specs=[pl.BlockSpec((tb, f), lambda i: (i, 0)), blk, blk, blk],
        out_shape=[jax.ShapeDtypeStruct((t, f), ACT), jax.ShapeDtypeStruct((t, d), F32),
                   jax.ShapeDtypeStruct((t, d), F32), jax.ShapeDtypeStruct((t, d), ACT)],
        compiler_params=_cparams(("parallel",)),
    )(gu, wd, x1, g, b)


def _ffn_tail_bwd(dh, wd, gu, tb):
    t, d = dh.shape
    f = wd.shape[0]
    fc = _tile(f, MM_TILE)

    def body(dh_ref, wd_ref, gu_ref, dgu_ref):
        dh_v = dh_ref[...]
        for c in range(f // fc):
            da = _dot(dh_v, wd_ref[c * fc:(c + 1) * fc, :], NT)
            gp = gu_ref[:, c * fc:(c + 1) * fc].astype(F32)
            sg = _sigmoid(gp)
            dgu_ref[:, c * fc:(c + 1) * fc] = (
                da * gu_ref[:, f + c * fc:f + (c + 1) * fc].astype(F32) * sg * (1.0 + gp * (1.0 - sg))
            ).astype(dgu_ref.dtype)
            dgu_ref[:, f + c * fc:f + (c + 1) * fc] = (da * gp * sg).astype(dgu_ref.dtype)

    return pl.pallas_call(
        body, name="ffn_tail_bwd", grid=(t // tb,),
        in_specs=[pl.BlockSpec((tb, d), lambda i: (i, 0)), pl.BlockSpec((f, d), lambda i: (0, 0)),
                  pl.BlockSpec((tb, 2 * f), lambda i: (i, 0))],
        out_specs=pl.BlockSpec((tb, 2 * f), lambda i: (i, 0)),
        out_shape=jax.ShapeDtypeStruct((t, 2 * f), ACT),
        compiler_params=_cparams(("parallel",)),
    )(dh, wd, gu)


def _ffn_head_bwd(dgu, wgu, dh2, hres, g, tb):
    t, d = dh2.shape
    f2 = wgu.shape[1]
    blk = pl.BlockSpec((tb, d), lambda i: (i, 0))
    acc = pl.BlockSpec((SUBLANES, d), lambda i: (0, 0))

    def body(dgu_ref, w_ref, dh2_ref, h_ref, g_ref, dh_ref, dhb_ref, dg_ref, db_ref):
        @pl.when(pl.program_id(0) == 0)
        def _():
            dg_ref[...] = jnp.zeros_like(dg_ref)
            db_ref[...] = jnp.zeros_like(db_ref)

        dy_v = _dot(dgu_ref[...], w_ref[...], NT) + ALPHA * dh2_ref[...]
        xhat, r = _ln_hat(h_ref[...])
        dh = _ln_bwd(dy_v * g_ref[...], xhat, r)
        dh_ref[...] = dh
        dhb_ref[...] = dh.astype(dhb_ref.dtype)
        dg_ref[...] += _fold8(dy_v * xhat)
        db_ref[...] += _fold8(dy_v)

    return pl.pallas_call(
        body, name="ffn_head_bwd", grid=(t // tb,),
        in_specs=[pl.BlockSpec((tb, f2), lambda i: (i, 0)), pl.BlockSpec((d, f2), lambda i: (0, 0)),
                  blk, blk, pl.BlockSpec((1, d), lambda i: (0, 0))],
        out_specs=[blk, blk, acc, acc],
        out_shape=[jax.ShapeDtypeStruct((t, d), F32), jax.ShapeDtypeStruct((t, d), ACT),
                   jax.ShapeDtypeStruct((SUBLANES, d), F32), jax.ShapeDtypeStruct((SUBLANES, d), F32)],
        compiler_params=_cparams(("arbitrary",)),
    )(dgu, wgu, dh2, hres, g)


def _loss_ln_bwd(y, target, hres, g, tb):
    t, d = y.shape
    blk = pl.BlockSpec((tb, d), lambda i: (i, 0))
    acc = pl.BlockSpec((SUBLANES, d), lambda i: (0, 0))

    def body(y_ref, t_ref, h_ref, g_ref, dh_ref, dhb_ref, dg_ref, db_ref, l_ref):
        @pl.when(pl.program_id(0) == 0)
        def _():
            for r_ in (dg_ref, db_ref, l_ref):
                r_[...] = jnp.zeros_like(r_)

        err = y_ref[...] - t_ref[...]
        dy_v = err * (1.0 / d)
        sq = _fold8(err * err)
        part = sq[:, :LANES]
        for c in range(1, d // LANES):
            part = part + sq[:, c * LANES:(c + 1) * LANES]
        l_ref[...] += part
        xhat, r = _ln_hat(h_ref[...])
        dh = _ln_bwd(dy_v * g_ref[...], xhat, r)
        dh_ref[...] = dh
        dhb_ref[...] = dh.astype(dhb_ref.dtype)
        dg_ref[...] += _fold8(dy_v * xhat)
        db_ref[...] += _fold8(dy_v)

    res = pl.pallas_call(
        body, name="loss_ln_bwd", grid=(t // tb,),
        in_specs=[blk, blk, blk, pl.BlockSpec((1, d), lambda i: (0, 0))],
        out_specs=[blk, blk, acc, acc, pl.BlockSpec((SUBLANES, LANES), lambda i: (0, 0))],
        out_shape=[jax.ShapeDtypeStruct((t, d), F32), jax.ShapeDtypeStruct((t, d), ACT),
                   jax.ShapeDtypeStruct((SUBLANES, d), F32), jax.ShapeDtypeStruct((SUBLANES, d), F32),
                   jax.ShapeDtypeStruct((SUBLANES, LANES), F32)],
        compiler_params=_cparams(("arbitrary",)),
    )(y, target, hres, g)
    return res[:4], res[4]


def _adamw(w, g, m, v):
    shape = w.shape
    cols = shape[-1]
    w2, g2, m2, v2 = (a.reshape(-1, cols) for a in (w, g, m, v))
    rows = w2.shape[0]
    tr = _tile(rows, 256, SUBLANES)
    blk = pl.BlockSpec((tr, cols), lambda i: (i, 0))

    def body(w_ref, g_ref, m_ref, v_ref, d_ref, nm_ref, nv_ref):
        g_v = g_ref[...]
        nm = ADAM_B1 * m_ref[...] + (1.0 - ADAM_B1) * g_v
        nv = ADAM_B2 * v_ref[...] + (1.0 - ADAM_B2) * (g_v * g_v)
        m_hat = nm / (1.0 - ADAM_B1 ** ADAM_STEP)
        v_hat = nv / (1.0 - ADAM_B2 ** ADAM_STEP)
        d_ref[...] = -ADAM_LR * (m_hat / (jnp.sqrt(v_hat) + ADAM_EPS) + ADAM_WD * w_ref[...])
        nm_ref[...] = nm
        nv_ref[...] = nv

    out = jax.ShapeDtypeStruct((rows, cols), F32)
    res = pl.pallas_call(
        body, name="adamw", grid=(rows // tr,),
        in_specs=[blk] * 4, out_specs=[blk] * 3, out_shape=[out] * 3,
        compiler_params=_cparams(("parallel",)),
    )(w2, g2, m2, v2)
    return tuple(r.reshape(shape) for r in res)


def _place():
    x, y, c = lax.axis_index("x"), lax.axis_index("y"), lax.axis_index("c")
    return x, y, c, [(1 - x, y), (x, 1 - y), (1 - x, 1 - y)]


def _remote(src, dst, send_sems, recv_sems, k, to):
    return pltpu.make_async_remote_copy(src_ref=src, dst_ref=dst, send_sem=send_sems.at[k],
                                        recv_sem=recv_sems.at[k], device_id=to, device_id_type=MESH)


class _Carried:
    def __init__(self, inputs, out_shapes, n_sems, copies):
        self.inputs, self.out_shapes, self.n_sems, self.copies = list(inputs), list(out_shapes), n_sems, copies

    def scratch(self):
        return [pltpu.SemaphoreType.DMA((self.n_sems,)), pltpu.SemaphoreType.DMA((self.n_sems,))]


def _join_plans(first, second):
    ni, no, ns = len(first.inputs), len(first.out_shapes), first.n_sems

    def copies(in_refs, out_refs, send_sems, recv_sems):
        start1, finish1 = first.copies(in_refs[:ni], out_refs[:no], send_sems, recv_sems)
        start2, finish2 = second.copies(in_refs[ni:], out_refs[no:], send_sems.at[pl.ds(ns, second.n_sems)],
                                        recv_sems.at[pl.ds(ns, second.n_sems)])

        def start():
            start1()
            start2()

        def finish():
            finish1()
            finish2()

        return start, finish

    return _Carried(first.inputs + second.inputs, first.out_shapes + second.out_shapes, ns + second.n_sems, copies)


def _run_comm(name, plan):
    n_in, n_out = len(plan.inputs), len(plan.out_shapes)

    def body(*refs):
        start, finish = plan.copies(refs[:n_in], refs[n_in:n_in + n_out], refs[-2], refs[-1])
        start()
        finish()

    return pl.pallas_call(
        body, name=name, in_specs=[ANY] * n_in, out_specs=[ANY] * n_out, out_shape=plan.out_shapes,
        scratch_shapes=plan.scratch(),
    )(*plan.inputs)


def _half_rows(rows, core):
    if rows % (4 * SUBLANES):
        return None
    return pl.ds(pl.multiple_of(core * (rows // 2), 2 * SUBLANES), rows // 2)


def _all_gather_plan(shards):
    n = len(shards)

    def copies(x_refs, out_refs, send_sems, recv_sems):
        x, y, c, chips = _place()
        sibling = (x, y, 1 - c)
        mine = 2 * x + y
        split = [_half_rows(x_refs[t].shape[0], c) is not None for t in range(n)]

        def src(t):
            return x_refs[t].at[_half_rows(x_refs[t].shape[0], c)] if split[t] else x_refs[t]

        def slot(t, chip_idx, core):
            rows = _half_rows(x_refs[t].shape[0], core)
            return out_refs[t].at[chip_idx, rows] if split[t] else out_refs[t].at[chip_idx]

        def first():
            return [_remote(src(t), slot(t, mine, c), send_sems, recv_sems, 6 * t + j, (cx, cy, c))
                    for j, (cx, cy) in enumerate(chips) for t in range(n)]

        def start():
            for cp in first():
                cp.start()

        def finish():
            passed = []
            for j, (cx, cy) in enumerate(chips):
                for t in range(n):
                    theirs = slot(t, 2 * cx + cy, c)
                    _remote(theirs, theirs, send_sems, recv_sems, 6 * t + j, (cx, cy, c)).wait_recv()
                    if split[t]:
                        fwd = _remote(theirs, theirs, send_sems, recv_sems, 6 * t + 3 + j, sibling)
                        fwd.start()
                        passed.append(fwd)
            for j, (cx, cy) in enumerate(chips):
                for t in range(n):
                    if split[t]:
                        other = slot(t, 2 * cx + cy, 1 - c)
                        _remote(other, other, send_sems, recv_sems, 6 * t + 3 + j, sibling).wait_recv()
            for cp in first() + passed:
                cp.wait_send()

        return start, finish

    return _Carried(shards, [jax.ShapeDtypeStruct((N_CHIPS,) + s.shape, s.dtype) for s in shards], 6 * n, copies)


def _sibling_exchange_plan(grads, small=None):
    n = len(grads)
    extra = [] if small is None else [small]

    def copies(in_refs, out_refs, send_sems, recv_sems):
        x, y, c, _ = _place()
        sibling = (x, y, 1 - c)

        def all_copies():
            cps = [_remote(in_refs[t].at[:, _half_rows(in_refs[t].shape[1], 1 - c), :], out_refs[t],
                           send_sems, recv_sems, t, sibling) for t in range(n)]
            if extra:
                cps.append(_remote(in_refs[n], out_refs[n], send_sems, recv_sems, n, sibling))
            return cps

        def start():
            for cp in all_copies():
                cp.start()

        def finish():
            for cp in all_copies():
                cp.wait()

        return start, finish

    shapes = [jax.ShapeDtypeStruct((g.shape[0], g.shape[1] // 2, g.shape[2]), g.dtype) for g in grads]
    shapes += [jax.ShapeDtypeStruct(s.shape, s.dtype) for s in extra]
    return _Carried(list(grads) + extra, shapes, n + 1, copies)


def _chip_exchange_plan(travel, small=None):
    n = len(travel)
    extra = [] if small is None else [small]

    def copies(in_refs, out_refs, send_sems, recv_sems):
        x, y, c, chips = _place()
        mine = 2 * x + y

        def all_copies():
            cps = []
            for j, (cx, cy) in enumerate(chips):
                to = (cx, cy, c)
                for t in range(n):
                    cps.append(_remote(in_refs[t].at[2 * cx + cy], out_refs[t].at[mine], send_sems, recv_sems,
                                       3 * t + j, to))
                if extra:
                    cps.append(_remote(in_refs[n], out_refs[n].at[mine], send_sems, recv_sems, 3 * n + j, to))
            return cps

        def start():
            for cp in all_copies():
                cp.start()

        def finish():
            for cp in all_copies():
                cp.wait()

        return start, finish

    shapes = [jax.ShapeDtypeStruct(g.shape, g.dtype) for g in travel]
    shapes += [jax.ShapeDtypeStruct((N_CHIPS,) + s.shape, s.dtype) for s in extra]
    return _Carried(list(travel) + extra, shapes, 3 * n + 3, copies)


def _sibling_merge_plan(reduced):
    n = len(reduced)

    def copies(in_refs, out_refs, send_sems, recv_sems):
        x, y, c, _ = _place()

        def all_copies():
            return [_remote(in_refs[t], out_refs[t], send_sems, recv_sems, t, (x, y, 1 - c)) for t in range(n)]

        def start():
            for cp in all_copies():
                cp.start()

        def finish():
            for cp in all_copies():
                cp.wait()

        return start, finish

    return _Carried(reduced, [jax.ShapeDtypeStruct(r.shape, r.dtype) for r in reduced], n, copies)


def _pair_sum(place, grad, land):
    n, r, c = grad.shape
    half = r // 2
    tr = _tile(half, 256, SUBLANES)
    nb = half // tr

    def body(place_ref, a_ref, b_ref, travel_ref, own_ref):
        total = a_ref[0] + b_ref[0]
        travel_ref[0] = total.astype(travel_ref.dtype)

        @pl.when(pl.program_id(1) == place_ref[1])
        def _():
            own_ref[...] = total

    return pl.pallas_call(
        body, name="grad_pair_sum",
        grid_spec=pltpu.PrefetchScalarGridSpec(
            num_scalar_prefetch=1, grid=(nb, n),
            in_specs=[pl.BlockSpec((1, tr, c), lambda i, s, p: (s, p[0] * nb + i, 0)),
                      pl.BlockSpec((1, tr, c), lambda i, s, p: (s, i, 0))],
            out_specs=[pl.BlockSpec((1, tr, c), lambda i, s, p: (s, i, 0)),
                       pl.BlockSpec((tr, c), lambda i, s, p: (i, 0))]),
        out_shape=[jax.ShapeDtypeStruct((n, half, c), BF16), jax.ShapeDtypeStruct((half, c), F32)],
        compiler_params=_cparams(("parallel", "arbitrary")),
    )(place, grad, land)


def _chip_sum(place, own, land, name):
    n, r, c = land.shape
    tr = _tile(r, 256, SUBLANES)

    def body(place_ref, own_ref, land_ref, o_ref):
        mine = place_ref[1]
        acc = jnp.zeros(o_ref.shape, F32)
        for s in range(n):
            acc = acc + jnp.where(mine == s, own_ref[...], land_ref[s].astype(F32))
        o_ref[...] = acc

    return pl.pallas_call(
        body, name=name,
        grid_spec=pltpu.PrefetchScalarGridSpec(
            num_scalar_prefetch=1, grid=(r // tr,),
            in_specs=[pl.BlockSpec((tr, c), lambda i, p: (i, 0)),
                      pl.BlockSpec((n, tr, c), lambda i, p: (0, i, 0))],
            out_specs=pl.BlockSpec((tr, c), lambda i, p: (i, 0))),
        out_shape=jax.ShapeDtypeStruct((r, c), F32),
        compiler_params=_cparams(("parallel",)),
    )(place, own, land)


def _add2(a, b):
    rows = a.shape[0]
    tr = _tile(rows, 256, SUBLANES)
    blk = pl.BlockSpec((tr, a.shape[1]), lambda i: (i, 0))

    def body(a_ref, b_ref, o_ref):
        o_ref[...] = a_ref[...] + b_ref[...]

    return pl.pallas_call(
        body, name="grad_small_pair_sum", grid=(rows // tr,), in_specs=[blk, blk], out_specs=blk,
        out_shape=jax.ShapeDtypeStruct(a.shape, F32), compiler_params=_cparams(("parallel",)),
    )(a, b)


def _merge_halves(place, mine, other):
    first_core = place[0] == 0
    return jnp.concatenate([jnp.where(first_core, mine, other), jnp.where(first_core, other, mine)], axis=0)


_BIG = (("w_in", 2), ("w_pa", 1), ("w_pb", 1), ("w_o", 1), ("w_ffn_gate", 2), ("w_ffn_up", 2),
        ("w_ffn_down", 1))
_SMALL = ("conv_w", "a_log", "dt_bias", "o_norm_w", "sgu_ln_g", "sgu_ln_b", "w_s", "b_s",
          "ln1_g", "ln1_b", "ln2_g", "ln2_b")


def _pack_small(arrays):
    pieces = []
    for a in arrays:
        if a.shape[-1] % LANES == 0:
            a2 = a.reshape(-1, LANES)
        else:
            a2 = jnp.pad(a.reshape(-1, a.shape[-1]), ((0, 0), (0, LANES - a.shape[-1])))
        pieces.append(jnp.pad(a2, ((0, -a2.shape[0] % SUBLANES), (0, 0))))
    return jnp.concatenate(pieces, axis=0)


def _unpack_small(buf, like):
    out, off = [], 0
    for a in like:
        if a.shape[-1] % LANES == 0:
            rows = a.size // LANES
            out.append(buf[off:off + rows].reshape(a.shape))
        else:
            rows = a.size // a.shape[-1]
            out.append(buf[off:off + rows, :a.shape[-1]].reshape(a.shape))
        off += -(-rows // SUBLANES) * SUBLANES
    return out


def _unshard(gathered, local, chip, axis):
    parts = [jnp.where(chip == s, local, gathered[s]) for s in range(N_CHIPS)]
    return jnp.concatenate(parts, axis=axis - 1)


def _to_shards(full, axis):
    l, r, c = full.shape
    if axis == 1:
        return full.reshape(l, N_CHIPS, r // N_CHIPS, c)
    return jnp.transpose(full.reshape(l, r, N_CHIPS, c // N_CHIPS), (0, 2, 1, 3))


def _row(v, width=None):
    v = v.reshape(1, -1).astype(F32)
    if width is not None and v.shape[1] < width:
        v = jnp.pad(v, ((0, 0), (0, width - v.shape[1])))
    return v


def _layer_consts(p, l, d):
    heads = d // DN_DK
    return dict(
        alog=_row(p["a_log"][l], LANES), dtb=_row(p["dt_bias"][l], LANES),
        onw=_row(jnp.tile(p["o_norm_w"][l], heads)),
        lng=_row(p["sgu_ln_g"][l]), lnb=_row(p["sgu_ln_b"][l]),
        ws=p["w_s"][l].astype(F32),
        bst=jnp.pad(p["b_s"][l].T, ((0, 0), (0, LANES - p["b_s"].shape[1]))),
        g1=_row(p["ln1_g"][l]), b1=_row(p["ln1_b"][l]), g2=_row(p["ln2_g"][l]), b2=_row(p["ln2_b"][l]))


class _NoComm:
    def with_proj_main(self):
        return None

    def after_proj_main(self, got):
        pass

    def weights(self, full):
        return full

    def with_dn_fwd(self):
        return None

    def after_dn_fwd(self, got):
        pass

    def with_ffn_in_dw(self):
        return None

    def after_ffn_in_dw(self, got):
        pass

    def after_branch_grads(self, g):
        pass

    def with_dn_bwd(self):
        return None

    def after_dn_bwd(self, got):
        pass

    def with_proj_main_dw(self):
        return None

    def after_proj_main_dw(self, got):
        pass

    def with_ffn_in(self):
        return None

    def after_ffn_in(self, got):
        pass

    def after_all_grads(self, g):
        pass

    def with_proj_main_dx(self):
        return None

    def after_proj_main_dx(self, got):
        pass


def _carry(carried, after, call, *args, **kw):
    if carried is None:
        return call(*args, **kw)
    out, got = call(*args, carried=carried, **kw)
    after(got)
    return out


def _in_proj_weights(w_in, d):
    heads, q4 = d // DN_DK, 4 * d
    wba = jnp.zeros((d, 2 * LANES), w_in.dtype)
    wba = wba.at[:, :heads].set(w_in[:, q4:q4 + heads])
    wba = wba.at[:, LANES:LANES + heads].set(w_in[:, q4 + heads:q4 + 2 * heads])
    return jnp.concatenate([w_in[:, :q4], w_in[:, q4 + 2 * heads:], wba], axis=1)


def _layer_fwd(x, xb, full, cl, d, tb, comm):
    wm = _in_proj_weights(full["w_in"], d)
    projm = _carry(comm.with_proj_main(), comm.after_proj_main, _matmul, xb, wm, NN, "proj_main", tn=MM_WIDE)
    full = comm.weights(full)
    wl = dict(wm=wm, conv=full["conv_w"], wpa=full["w_pa"], wpb=full["w_pb"], wo=full["w_o"],
              wgu=jnp.concatenate([full["w_ffn_gate"], full["w_ffn_up"]], axis=1), wd=full["w_ffn_down"])
    qkv = _conv_fwd(projm, wl["conv"], d, _tile(x.shape[0], 2 * tb, SUBLANES))
    (o, states, ycors), got = _dn_fwd(qkv, projm, cl["alog"], cl["dtb"], d, comm.with_dn_fwd())
    comm.after_dn_fwd(got)
    ya, yb = _gate_sgu_fwd(o, projm, cl["onw"], cl["lng"], cl["lnb"], cl["ws"], cl["bst"], d)
    pa, pb, m, h1, x1, x1b = _mix_fwd(ya, yb, projm, x, wl["wpa"], wl["wpb"], wl["wo"], cl["g1"], cl["b1"], d, tb)
    gu = _carry(comm.with_ffn_in(), comm.after_ffn_in, _matmul, x1b, wl["wgu"], NN, "ffn_in", out_dtype=ACT)
    act, h2, x2, x2b = _ffn_tail_fwd(gu, wl["wd"], x1, cl["g2"], cl["b2"], tb)
    saved = dict(xb=xb, projm=projm, qkv=qkv, o=o, states=states, ycors=ycors, ya=ya, yb=yb,
                 pa=pa, pb=pb, m=m, h1=h1, x1b=x1b, gu=gu, act=act, h2=h2)
    return x2, x2b, saved, wl


def _layer_bwd(sv, wl, cl, d, tb, comm, ln2_bwd, next_ln=None):
    g = {}
    dh2, dh2b, dg2, db2 = ln2_bwd
    g["ln2_g"], g["ln2_b"] = dg2.sum(0), db2.sum(0)
    g["wd"] = _matmul(sv["act"], dh2b, TN, "ffn_out_dw")
    dgu = _ffn_tail_bwd(dh2b, wl["wd"], sv["gu"], tb)
    g["wgu"] = _carry(comm.with_ffn_in_dw(), comm.after_ffn_in_dw, _matmul, sv["x1b"], dgu, TN, "ffn_in_dw")
    dh1, dh1b, dg1, db1 = _ffn_head_bwd(dgu, wl["wgu"], dh2, sv["h1"], cl["g1"], tb)
    g["ln1_g"], g["ln1_b"] = dg1.sum(0), db1.sum(0)
    g["wo"] = _matmul(sv["m"], dh1b, TN, "wo_dw")
    dpa, dpb, dya, dyb, dprojm = _mix_bwd(dh1b, sv["pa"], sv["pb"], sv["projm"], wl["wpa"], wl["wpb"], wl["wo"], d, tb)
    g["wpa"] = _matmul(sv["ya"], dpa, TN, "wpa_dw")
    g["wpb"] = _matmul(sv["yb"], dpb, TN, "wpb_dw")
    comm.after_branch_grads(g)
    do, dprojm, donw, dlng, dlnb, dws, dbst = _gate_sgu_bwd(
        dya, dyb, sv["o"], sv["projm"], cl["onw"], cl["lng"], cl["lnb"], cl["ws"], cl["bst"], dprojm, d)
    heads, groups = d // DN_DK, d // SGU_GROUP_DIM
    g["o_norm_w"], g["sgu_ln_g"], g["sgu_ln_b"] = donw.sum(0), dlng.sum(0), dlnb.sum(0)
    g["w_s"], g["b_s"] = dws, dbst[:, :groups].T
    (dqkv, dprojm, dal, ddt), got = _dn_bwd(sv["qkv"], sv["projm"], cl["alog"], cl["dtb"], do, sv["states"],
                                            sv["ycors"], dprojm, d, comm.with_dn_bwd())
    comm.after_dn_bwd(got)
    g["a_log"], g["dt_bias"] = dal.sum(0)[:heads], ddt.sum(0)[:heads]
    tbc = _tile(sv["xb"].shape[0], 2 * tb, SUBLANES)
    dy, dcw = _conv_bwd_dy(sv["projm"], wl["conv"], dqkv, d, tbc)
    g["conv_w"] = dcw.sum(1)
    dprojm = _conv_bwd_dx(dy, wl["conv"], dprojm, d, tbc)
    g["wm"] = _carry(comm.with_proj_main_dw(), comm.after_proj_main_dw, _matmul, sv["xb"], dprojm, TN,
                     "proj_main_dw", tn=MM_WIDE)
    comm.after_all_grads(g)
    if next_ln is not None:
        return _matmul(dprojm, wl["wm"], NT, "proj_main_dx", add=dh1, coef=ALPHA, tm=MM_TILE // 3, tk=MM_WIDE,
                       ln=next_ln), g
    dx = _carry(comm.with_proj_main_dx(), comm.after_proj_main_dx, _matmul, dprojm, wl["wm"], NT, "proj_main_dx",
                add=dh1, coef=ALPHA, tk=MM_WIDE)
    return dx, g


_BRANCH = ("w_pa", "w_pb", "w_o", "w_ffn_gate", "w_ffn_up", "w_ffn_down")


def _grad_shards(g, d, keys):
    heads, q4 = d // DN_DK, 4 * d
    rows = lambda a: a.reshape(N_CHIPS, -1, a.shape[1])
    out = {}
    if "w_in" in keys:
        gm, gba, wsh = g["wm"][:, :8 * d], g["wm"][:, 8 * d:], 2 * d + heads // 2
        out["w_in"] = jnp.stack([gm[:, :wsh],
                                 jnp.concatenate([gm[:, wsh:q4], gba[:, :heads]], axis=1),
                                 jnp.concatenate([gba[:, LANES:LANES + heads], gm[:, q4:q4 + wsh - heads]], axis=1),
                                 gm[:, q4 + wsh - heads:]])
    if "w_pa" in keys:
        ggu = g["wgu"]
        f = ggu.shape[1] // 2
        fs = f // N_CHIPS
        out.update({
            "w_pa": rows(g["wpa"]), "w_pb": rows(g["wpb"]), "w_o": rows(g["wo"]), "w_ffn_down": rows(g["wd"]),
            "w_ffn_gate": jnp.stack([ggu[:, s * fs:(s + 1) * fs] for s in range(N_CHIPS)]),
            "w_ffn_up": jnp.stack([ggu[:, f + s * fs:f + (s + 1) * fs] for s in range(N_CHIPS)])})
    return out


def _local_step(x, target, full0, full1_of, small_w, comm0=None):
    t, d = x.shape
    tb = _tile(t, 256, SUBLANES)
    comm0 = comm0 or _NoComm()
    consts = [_layer_consts(small_w, l, d) for l in range(DEPTH)]
    x1, x1b, sv0, w0 = _layer_fwd(x, x.astype(ACT), full0, consts[0], d, tb, comm0)
    x2, _, sv1, w1 = _layer_fwd(x1, x1b, full1_of(), consts[1], d, tb, _NoComm())
    ln2_bwd, loss_parts = _loss_ln_bwd(x2, target, sv1["h2"], consts[1]["g2"], tb)
    ln2_bwd, g1 = _layer_bwd(sv1, w1, consts[1], d, tb, _NoComm(), ln2_bwd, next_ln=(sv0["h2"], consts[0]["g2"]))
    comm0.layer1_grads = g1
    grad_x, g0 = _layer_bwd(sv0, w0, consts[0], d, tb, comm0, ln2_bwd)
    return loss_parts, grad_x, [g0, g1]


def kernel(x, w_in, conv_w, a_log, dt_bias, o_norm_w, sgu_ln_g, sgu_ln_b, w_s, b_s, w_pa, w_pb, w_o, ln1_g, ln1_b, w_ffn_gate, w_ffn_up, w_ffn_down, ln2_g, ln2_b, loss_target, m_w_in, m_conv_w, m_a_log, m_dt_bias, m_o_norm_w, m_sgu_ln_g, m_sgu_ln_b, m_w_s, m_b_s, m_w_pa, m_w_pb, m_w_o, m_ln1_g, m_ln1_b, m_w_ffn_gate, m_w_ffn_up, m_w_ffn_down, m_ln2_g, m_ln2_b, v_w_in, v_conv_w, v_a_log, v_dt_bias, v_o_norm_w, v_sgu_ln_g, v_sgu_ln_b, v_w_s, v_b_s, v_w_pa, v_w_pb, v_w_o, v_ln1_g, v_ln1_b, v_w_ffn_gate, v_w_ffn_up, v_w_ffn_down, v_ln2_g, v_ln2_b):
    names = ("w_in", "conv_w", "a_log", "dt_bias", "o_norm_w", "sgu_ln_g", "sgu_ln_b", "w_s", "b_s", "w_pa",
             "w_pb", "w_o", "ln1_g", "ln1_b", "w_ffn_gate", "w_ffn_up", "w_ffn_down", "ln2_g", "ln2_b")
    w = dict(zip(names, (w_in, conv_w, a_log, dt_bias, o_norm_w, sgu_ln_g, sgu_ln_b, w_s, b_s, w_pa, w_pb, w_o,
                         ln1_g, ln1_b, w_ffn_gate, w_ffn_up, w_ffn_down, ln2_g, ln2_b)))
    mom = dict(zip(names, (m_w_in, m_conv_w, m_a_log, m_dt_bias, m_o_norm_w, m_sgu_ln_g, m_sgu_ln_b, m_w_s, m_b_s,
                           m_w_pa, m_w_pb, m_w_o, m_ln1_g, m_ln1_b, m_w_ffn_gate, m_w_ffn_up, m_w_ffn_down,
                           m_ln2_g, m_ln2_b)))
    var = dict(zip(names, (v_w_in, v_conv_w, v_a_log, v_dt_bias, v_o_norm_w, v_sgu_ln_g, v_sgu_ln_b, v_w_s, v_b_s,
                           v_w_pa, v_w_pb, v_w_o, v_ln1_g, v_ln1_b, v_w_ffn_gate, v_w_ffn_up, v_w_ffn_down,
                           v_ln2_g, v_ln2_b)))
    chip = 2 * lax.axis_index("x") + lax.axis_index("y")
    place = jnp.stack([lax.axis_index("c"), chip]).astype(jnp.int32)

    big = [k for k, _ in _BIG]
    axis_of = dict(_BIG)
    local = {k: w[k].astype(BF16) for k in big}
    local["conv_w"] = conv_w

    def gather_plan(l, keys):
        return _all_gather_plan([local[k][l] for k in keys])

    def full_of(l, keys, gathered):
        return {k: _unshard(gt, local[k][l], chip, axis_of.get(k, 2)) for k, gt in zip(keys, gathered)}

    def pair_sums(grads_l, keys, lands):
        return [_pair_sum(place, grads_l[k], land) for k, land in zip(keys, lands)]

    def chip_sums(pairs, lands):
        return [_chip_sum(place, p[1], land, "grad_chip_sum") for p, land in zip(pairs, lands)]

    class Layer0Comm(_NoComm):
        def with_proj_main(self):
            return gather_plan(0, _BRANCH)

        def after_proj_main(self, got):
            self.rest = full_of(0, _BRANCH, got)

        def weights(self, full):
            return {**full, **self.rest}

        def with_dn_fwd(self):
            return gather_plan(1, mixer)

        def after_dn_fwd(self, got):
            self.full1 = full_of(1, mixer, got)

        def with_ffn_in(self):
            return gather_plan(1, ffn)

        def after_ffn_in(self, got):
            self.full1.update(full_of(1, ffn, got))

        def with_ffn_in_dw(self):
            self.g1 = _grad_shards(self.layer1_grads, x.shape[-1], big)
            return _sibling_exchange_plan([self.g1[k] for k in big])

        def after_ffn_in_dw(self, got):
            self.pairs1 = pair_sums(self.g1, big, got)

        def with_dn_bwd(self):
            return _chip_exchange_plan([p[0] for p in self.pairs1])

        def after_dn_bwd(self, got):
            self.red1 = chip_sums(self.pairs1, got)

        def after_branch_grads(self, g0):
            shards = _grad_shards(g0, x.shape[-1], _BRANCH)
            lands = _run_comm("grad_sibling_exchange", _sibling_exchange_plan([shards[k] for k in _BRANCH]))
            self.pairs0 = pair_sums(shards, _BRANCH, lands)

        def with_proj_main_dw(self):
            return _chip_exchange_plan([p[0] for p in self.pairs0])

        def after_proj_main_dw(self, got):
            self.red0 = chip_sums(self.pairs0, got)

        def after_all_grads(self, g0):
            g_in = _grad_shards(g0, x.shape[-1], ["w_in"])["w_in"]
            self.small_g = {k: jnp.stack([g0[k], self.layer1_grads[k]]) for k in _SMALL}
            small = _pack_small([self.small_g[k] for k in _SMALL])
            land, sland = _run_comm("grad_sibling_exchange_last", _sibling_exchange_plan([g_in], small))
            self.pair_in = _pair_sum(place, g_in, land)
            self.small_chip = _add2(small, sland)

        def with_proj_main_dx(self):
            return _join_plans(_chip_exchange_plan([self.pair_in[0]], self.small_chip),
                               _sibling_merge_plan(self.red0 + self.red1))

        def after_proj_main_dx(self, got):
            self.red_in = _chip_sum(place, self.pair_in[1], got[0], "grad_chip_sum")
            self.small_total = _chip_sum(place, self.small_chip, got[1], "grad_small_chip_sum")
            self.others = got[2:]

    comm = Layer0Comm()
    first, mixer, ffn = ["w_in", "conv_w"], ["w_in", "conv_w", "w_pa", "w_pb", "w_o"], list(_BRANCH[3:])
    full0 = full_of(0, first, _run_comm("all_gather_weights", gather_plan(0, first)))
    small_w = {k: w[k] for k in _SMALL if k != "conv_w"}
    loss_parts, grad_x, g = _local_step(x[0], loss_target[0], full0, lambda: comm.full1, small_w, comm)

    reduced = [comm.red_in] + comm.red0 + comm.red1
    others = list(_run_comm("grad_sibling_merge", _sibling_merge_plan([comm.red_in]))) + list(comm.others)
    halves = [_merge_halves(place, mine, other) for mine, other in zip(reduced, others)]
    grads = {k: jnp.stack([halves[i], halves[len(big) + i]]) for i, k in enumerate(big)}
    grads.update(zip(_SMALL, _unpack_small(comm.small_total, [comm.small_g[k] for k in _SMALL])))
    grads["conv_w"] = lax.dynamic_index_in_dim(_to_shards(grads["conv_w"], 2), chip, 1, keepdims=False)

    delta, new_m, new_v = {}, {}, {}
    for k in [k for k, _ in _BIG] + ["conv_w"]:
        delta[k], new_m[k], new_v[k] = _adamw(w[k], grads[k], mom[k], var[k])
    rep = [k for k in _SMALL if k != "conv_w"]
    pack = lambda dct: _pack_small([dct[k] for k in rep])
    packed = _adamw(pack(w), pack(grads), pack(mom), pack(var))
    for dst, src in zip((delta, new_m, new_v), packed):
        dst.update(zip(rep, _unpack_small(src, [w[k] for k in rep])))

    loss = 0.5 * lax.psum(jnp.sum(loss_parts), ("x", "y", "c")) / x.shape[-1]
    return (loss, grad_x[None], *[grads[k] for k in names], *[delta[k] for k in names],
            *[new_m[k] for k in names], *[new_v[k] for k in names])
```

```python
import math

import jax
import jax.numpy as jnp
from jax import lax
from jax.experimental import pallas as pl
from jax.experimental.pallas import tpu as pltpu

F32 = jnp.float32
BF16 = jnp.bfloat16
MXU_DTYPE = jnp.bfloat16
ACT = jnp.bfloat16
HIGHEST = lax.Precision.HIGHEST

DEPTH = 2
CHUNK = 64
DN_GROUP = 2
DN_GROUP_FWD = 4
SGU_BLOCK = 128
SGU_WINDOWS = 2
CONV_K = 4
DN_DK = 128
SGU_GROUP_DIM = 128
LN_EPS = 1e-5
RMS_EPS = 1e-6
ALPHA = (2 * DEPTH) ** 0.25
ADAM_LR, ADAM_B1, ADAM_B2, ADAM_EPS, ADAM_WD, ADAM_STEP = 0.001, 0.9, 0.999, 1e-08, 0.01, 10

LANES = 128
SUBLANES = 8
VMEM_LIMIT = 52 * 2 ** 20
N_CHIPS = 4

NN = ((1,), (0,))
NT = ((1,), (1,))
TN = ((0,), (0,))
MESH = pl.DeviceIdType.MESH
ANY = pl.BlockSpec(memory_space=pl.ANY)


def _dot(a, b, dims=NN, prec=None):
    if prec is None:
        a = a.astype(MXU_DTYPE)
        b = b.astype(MXU_DTYPE)
    return lax.dot_general(a, b, (dims, ((), ())), preferred_element_type=F32, precision=prec)


def _cparams(sem=None):
    return pltpu.CompilerParams(dimension_semantics=sem, vmem_limit_bytes=VMEM_LIMIT)


def _tile(dim, pref, unit=LANES):
    t = (min(pref, dim) // unit) * unit
    while t >= unit:
        if dim % t == 0:
            return t
        t -= unit
    return dim


def _fold8(x):
    r, n = x.shape
    return x.reshape(r // SUBLANES, SUBLANES, n).sum(axis=0)


def _sigmoid(x):
    return 1.0 / (1.0 + jnp.exp(-x))


def _gelu(x):
    return 0.5 * x * (1.0 + lax.erf(x * (2.0 ** -0.5)))


def _gelu_grad(x):
    return 0.5 * (1.0 + lax.erf(x * (2.0 ** -0.5))) + x * jnp.exp(-0.5 * x * x) * (2.0 * math.pi) ** -0.5


def _ln_hat(h):
    mu = jnp.mean(h, axis=-1, keepdims=True)
    xc = h - mu
    var = jnp.mean(xc * xc, axis=-1, keepdims=True)
    r = lax.rsqrt(var + LN_EPS)
    return xc * r, r


def _ln_bwd(dxhat, xhat, r):
    return r * (dxhat - jnp.mean(dxhat, axis=-1, keepdims=True)
                - xhat * jnp.mean(dxhat * xhat, axis=-1, keepdims=True))


MM_TILE = 1536
MM_WIDE = 2048


def _matmul(a, b, dims, name, out_dtype=F32, add=None, coef=1.0, tm=MM_TILE, tn=MM_TILE, tk=MM_TILE, carried=None,
            ln=None):
    if dims == NN:
        (m, k), n = a.shape, b.shape[1]
    elif dims == NT:
        (m, k), n = a.shape, b.shape[0]
    else:
        (k, m), n = a.shape, b.shape[1]
    tm, tn, tk = _tile(m, tm), _tile(n, tn), _tile(k, tk)
    nk = k // tk
    a_spec = pl.BlockSpec((tk, tm), lambda j, i, q: (q, i)) if dims == TN else pl.BlockSpec((tm, tk), lambda j, i, q: (i, q))
    b_spec = pl.BlockSpec((tn, tk), lambda j, i, q: (j, q)) if dims == NT else pl.BlockSpec((tk, tn), lambda j, i, q: (q, j))
    o_spec = pl.BlockSpec((tm, tn), lambda j, i, q: (i, j))
    has_add = add is not None
    if ln is not None:
        assert n == tn and has_add and carried is None
        return _matmul_ln_bwd(a, b, dims, name, add, coef, ln, a_spec, b_spec, o_spec, (m, n, tm, tn, nk))

    def body(*refs):
        a_ref, b_ref = refs[0], refs[1]
        add_ref = refs[2] if has_add else None
        o_ref, acc_ref = refs[2 + has_add], refs[3 + has_add]
        q = pl.program_id(2)
        part = _dot(a_ref[...], b_ref[...], dims)

        def finish(r):
            if has_add:
                r = r + coef * add_ref[...]
            o_ref[...] = r.astype(out_dtype)

        if nk == 1:
            finish(part)
        else:
            @pl.when(q == 0)
            def _():
                acc_ref[...] = part

            @pl.when(q > 0)
            def _():
                acc_ref[...] += part

            @pl.when(q == nk - 1)
            def _():
                finish(acc_ref[...])

    ins = [a, b] + ([add] if has_add else [])
    in_specs = [a_spec, b_spec] + ([o_spec] if has_add else [])
    grid = (n // tn, m // tm, nk)
    acc = pltpu.VMEM((tm, tn) if nk > 1 else (SUBLANES, LANES), F32)
    out = jax.ShapeDtypeStruct((m, n), out_dtype)
    if carried is None:
        return pl.pallas_call(
            body, name=name, grid=grid, in_specs=in_specs, out_specs=o_spec, out_shape=out, scratch_shapes=[acc],
            compiler_params=_cparams(("parallel", "parallel", "arbitrary")),
        )(*ins)
    res = pl.pallas_call(
        _carrying(body, len(ins), 1, 1, carried, grid), name=name + "_carrying", grid=grid,
        in_specs=in_specs + [ANY] * len(carried.inputs), out_specs=[o_spec] + [ANY] * len(carried.out_shapes),
        out_shape=[out] + carried.out_shapes, scratch_shapes=[acc] + carried.scratch(),
        compiler_params=_cparams(("arbitrary", "arbitrary", "arbitrary")),
    )(*ins, *carried.inputs)
    return res[0], res[1:]


def _matmul_ln_bwd(a, b, dims, name, add, coef, ln, a_spec, b_spec, o_spec, sizes):
    m, n, tm, tn, nk = sizes
    hres, g = ln
    row = pl.BlockSpec((1, n), lambda j, i, q: (0, 0))
    sums = pl.BlockSpec((SUBLANES, n), lambda j, i, q: (0, 0))

    def body(a_ref, b_ref, add_ref, h_ref, g_ref, dh_ref, dhb_ref, dg_ref, db_ref, acc_ref):
        i, q = pl.program_id(1), pl.program_id(2)
        part = _dot(a_ref[...], b_ref[...], dims)

        @pl.when(jnp.logical_and(i == 0, q == 0))
        def _():
            dg_ref[...] = jnp.zeros_like(dg_ref)
            db_ref[...] = jnp.zeros_like(db_ref)

        @pl.when(q == 0)
        def _():
            acc_ref[...] = part

        @pl.when(q > 0)
        def _():
            acc_ref[...] += part

        @pl.when(q == nk - 1)
        def _():
            dy_v = acc_ref[...] + coef * add_ref[...]
            xhat, r = _ln_hat(h_ref[...])
            dh = _ln_bwd(dy_v * g_ref[...], xhat, r)
            dh_ref[...] = dh
            dhb_ref[...] = dh.astype(dhb_ref.dtype)
            dg_ref[...] += _fold8(dy_v * xhat)
            db_ref[...] += _fold8(dy_v)

    return pl.pallas_call(
        body, name=name + "_ln_bwd", grid=(1, m // tm, nk),
        in_specs=[a_spec, b_spec, o_spec, o_spec, row], out_specs=[o_spec, o_spec, sums, sums],
        out_shape=[jax.ShapeDtypeStruct((m, n), F32), jax.ShapeDtypeStruct((m, n), ACT),
                   jax.ShapeDtypeStruct((SUBLANES, n), F32), jax.ShapeDtypeStruct((SUBLANES, n), F32)],
        scratch_shapes=[pltpu.VMEM((tm, tn), F32)],
        compiler_params=_cparams(("arbitrary", "arbitrary", "arbitrary")),
    )(a, b, add, hres, g)


def _conv_taps(cur_ref, halo_ref, first):
    x = cur_ref[...]
    tb = x.shape[0]
    halo = jnp.where(first, 0.0, halo_ref[...])
    xc = jnp.concatenate([halo, x], axis=0)
    return [x] + [pltpu.roll(xc, s, 0)[SUBLANES:SUBLANES + tb] for s in range(1, CONV_K)]


def _conv_fwd(projm, conv_w, d, tb):
    t = projm.shape[0]
    heads = d // DN_DK
    hb = tb // SUBLANES

    def body(cur_ref, halo_ref, w_ref, o_ref):
        i, j = pl.program_id(0), pl.program_id(1)
        taps = _conv_taps(cur_ref, halo_ref, i == 0)
        y = taps[0] * w_ref[CONV_K - 1:CONV_K, :]
        for s in range(1, CONV_K):
            y = y + taps[s] * w_ref[CONV_K - 1 - s:CONV_K - s, :]
        act = y * _sigmoid(y)
        scale = jnp.where(j == 0, DN_DK ** -0.5, 1.0)
        for h in range(heads):
            seg = act[:, h * DN_DK:(h + 1) * DN_DK]
            r = lax.rsqrt(jnp.sum(seg * seg, axis=1, keepdims=True) + RMS_EPS) * scale
            o_ref[:, h * DN_DK:(h + 1) * DN_DK] = seg * jnp.where(j < 2, r, 1.0)

    blk = pl.BlockSpec((tb, d), lambda i, j: (i, j))
    return pl.pallas_call(
        body, name="conv_fwd", grid=(t // tb, 3),
        in_specs=[blk,
                  pl.BlockSpec((SUBLANES, d), lambda i, j: (jnp.maximum(i * hb - 1, 0), j)),
                  pl.BlockSpec((CONV_K, d), lambda i, j: (0, j))],
        out_specs=blk,
        out_shape=jax.ShapeDtypeStruct((t, 3 * d), F32),
        compiler_params=_cparams(("parallel", "parallel")),
    )(projm, projm, conv_w)


def _conv_bwd_dy(projm, conv_w, dqkv, d, tb):
    t = projm.shape[0]
    heads = d // DN_DK
    hb = tb // SUBLANES

    def body(cur_ref, halo_ref, w_ref, dout_ref, dy_ref, dw_ref):
        j, i = pl.program_id(0), pl.program_id(1)
        taps = _conv_taps(cur_ref, halo_ref, i == 0)
        y = taps[0] * w_ref[CONV_K - 1:CONV_K, :]
        for s in range(1, CONV_K):
            y = y + taps[s] * w_ref[CONV_K - 1 - s:CONV_K - s, :]
        sg = _sigmoid(y)
        act = y * sg
        dact = sg * (1.0 + y * (1.0 - sg))
        scale = jnp.where(j == 0, DN_DK ** -0.5, 1.0)
        for h in range(heads):
            cols = slice(h * DN_DK, (h + 1) * DN_DK)
            seg = act[:, cols]
            r = lax.rsqrt(jnp.sum(seg * seg, axis=1, keepdims=True) + RMS_EPS)
            nrm = seg * r
            dout = dout_ref[:, cols]
            ds = jnp.where(j < 2, (r * scale) * (dout - nrm * jnp.sum(dout * nrm, axis=1, keepdims=True)), dout)
            dy_ref[:, cols] = ds * dact[:, cols]
        dy = dy_ref[...]

        @pl.when(i == 0)
        def _():
            dw_ref[...] = jnp.zeros_like(dw_ref)

        for s in range(CONV_K):
            dw_ref[CONV_K - 1 - s] += _fold8(dy * taps[s])

    return pl.pallas_call(
        body, name="conv_bwd_dy", grid=(3, t // tb),
        in_specs=[pl.BlockSpec((tb, d), lambda j, i: (i, j)),
                  pl.BlockSpec((SUBLANES, d), lambda j, i: (jnp.maximum(i * hb - 1, 0), j)),
                  pl.BlockSpec((CONV_K, d), lambda j, i: (0, j)),
                  pl.BlockSpec((tb, d), lambda j, i: (i, j))],
        out_specs=[pl.BlockSpec((tb, d), lambda j, i: (i, j)),
                   pl.BlockSpec((CONV_K, SUBLANES, d), lambda j, i: (0, 0, j))],
        out_shape=[jax.ShapeDtypeStruct((t, 3 * d), F32),
                   jax.ShapeDtypeStruct((CONV_K, SUBLANES, 3 * d), F32)],
        compiler_params=_cparams(("parallel", "arbitrary")),
    )(projm, projm, conv_w, dqkv)


def _conv_bwd_dx(dy, conv_w, dprojm, d, tb):
    t = dy.shape[0]
    hb = tb // SUBLANES
    last = t // tb - 1

    def body(cur_ref, halo_ref, w_ref, alias_ref, o_ref):
        i = pl.program_id(0)
        cur = cur_ref[...]
        halo = jnp.where(i == last, 0.0, halo_ref[...])
        dc = jnp.concatenate([cur, halo], axis=0)
        acc = cur * w_ref[CONV_K - 1:CONV_K, :]
        for s in range(1, CONV_K):
            acc = acc + pltpu.roll(dc, tb + SUBLANES - s, 0)[:tb] * w_ref[CONV_K - 1 - s:CONV_K - s, :]
        o_ref[...] = acc.astype(o_ref.dtype)

    return pl.pallas_call(
        body, name="conv_bwd_dx", grid=(t // tb, 3),
        in_specs=[pl.BlockSpec((tb, d), lambda i, j: (i, j)),
                  pl.BlockSpec((SUBLANES, d), lambda i, j: (jnp.minimum((i + 1) * hb, t // SUBLANES - 1), j)),
                  pl.BlockSpec((CONV_K, d), lambda i, j: (0, j)),
                  ANY],
        out_specs=pl.BlockSpec((tb, d), lambda i, j: (i, j)),
        out_shape=jax.ShapeDtypeStruct(dprojm.shape, dprojm.dtype),
        input_output_aliases={3: 0},
        compiler_params=_cparams(("parallel", "parallel")),
    )(dy, dy, conv_w, dprojm)


def _beta_g(ba, alog, dtb):
    beta = _sigmoid(ba[:, :LANES])
    xa = ba[:, LANES:] + dtb
    softplus = jnp.maximum(xa, 0.0) + jnp.log(1.0 + jnp.exp(-jnp.abs(xa)))
    ea = jnp.exp(alog)
    return beta, -ea * softplus, ea, _sigmoid(xa)


def _inv_corrections(mats):
    ys = [-a for a in mats]
    ps = [_dot(a, a) for a in mats]
    steps = int(math.log2(CHUNK)) - 1
    for it in range(steps):
        ys = [y + p + _dot(y, p) for y, p in zip(ys, ps)]
        if it < steps - 1:
            ps = [_dot(p, p) for p in ps]
    return ys


def _chunk_masks():
    row = lax.broadcasted_iota(jnp.int32, (CHUNK, CHUNK), 0)
    col = lax.broadcasted_iota(jnp.int32, (CHUNK, CHUNK), 1)
    return row >= col, row > col, row <= col


def _col_of(mat, lane_idx, h):
    return jnp.sum(jnp.where(lane_idx == h, mat, 0.0), axis=1, keepdims=True)


def _row_of(mat, sub_idx, h):
    return jnp.sum(jnp.where(sub_idx == h, mat, 0.0), axis=0, keepdims=True)


def _phases(fns):
    return fns if len(fns) == 3 else (fns[0], lambda: None, fns[1])


def _carrying(compute, n_in, n_out, n_scratch, carried, grid):
    if carried is None:
        return compute
    ci, co = len(carried.inputs), len(carried.out_shapes)

    def body(*refs):
        ins, c_in = refs[:n_in], refs[n_in:n_in + ci]
        outs, c_out = refs[n_in + ci:n_in + ci + n_out], refs[n_in + ci + n_out:n_in + ci + n_out + co]
        scratch = refs[n_in + ci + n_out + co:]
        start, middle, finish = _phases(carried.copies(c_in, c_out, scratch[n_scratch], scratch[n_scratch + 1]))
        step, total = 0, 1
        for axis, steps in enumerate(grid):
            step = step * steps + pl.program_id(axis)
            total *= steps

        @pl.when(step == 0)
        def _():
            start()

        compute(*ins, *outs, *scratch[:n_scratch])

        @pl.when(step == (3 * total) // 4)
        def _():
            middle()

        @pl.when(step == total - 1)
        def _():
            finish()

    return body


def _dn_fwd(qkv, ba, alog, dtb, d, carried=None):
    t = qkv.shape[0]
    heads = d // DN_DK
    n_chunks = t // CHUNK
    grp = DN_GROUP_FWD if n_chunks % DN_GROUP_FWD == 0 else 1
    span = grp * CHUNK
    extra = carried or _Carried([], [], 0, None)

    def compute(qkv_ref, ba_ref, al_ref, dt_ref, o_ref, s_ref, y_ref, state):
        @pl.when(pl.program_id(0) == 0)
        def _():
            state[...] = jnp.zeros_like(state)

        tril, strict, _ = _chunk_masks()
        beta, g, _, _ = _beta_g(ba_ref[...], al_ref[...], dt_ref[...])
        lane = lax.broadcasted_iota(jnp.int32, (CHUNK, LANES), 1)
        sub = lax.broadcasted_iota(jnp.int32, (LANES, CHUNK), 0)
        rowc = lax.broadcasted_iota(jnp.int32, (CHUNK, 1), 0)
        hs = range(heads)
        units = [(c, h) for c in range(grp) for h in hs]
        un = range(len(units))
        rows = lambda c: slice(c * CHUNK, (c + 1) * CHUNK)
        gc = [_dot(jnp.where(tril, 1.0, 0.0), g[rows(c)], NN, HIGHEST) for c in range(grp)]
        gct = [m.T for m in gc]
        q = [qkv_ref[rows(c), h * DN_DK:(h + 1) * DN_DK] for c, h in units]
        k = [qkv_ref[rows(c), d + h * DN_DK:d + (h + 1) * DN_DK] for c, h in units]
        v = [qkv_ref[rows(c), 2 * d + h * DN_DK:2 * d + (h + 1) * DN_DK] for c, h in units]
        gch = [_col_of(gc[c], lane, h) for c, h in units]
        bh = [_col_of(beta[rows(c)], lane, h) for c, h in units]
        dec = [jnp.where(tril, jnp.exp(gch[n] - _row_of(gct[c], sub, h)), 0.0) for n, (c, h) in enumerate(units)]
        egc = [jnp.exp(gch[n]) for n in un]
        gl = [jnp.sum(jnp.where(rowc == CHUNK - 1, gch[n], 0.0), axis=0, keepdims=True) for n in un]
        kb = [k[n] * bh[n] for n in un]
        a = [jnp.where(strict, _dot(kb[n], k[n], NT) * dec[n], 0.0) for n in un]
        p = [_dot(q[n], k[n], NT) * dec[n] for n in un]
        ycor = _inv_corrections(a)
        rhs = [jnp.concatenate([v[n] * bh[n], kb[n] * egc[n]], axis=1) for n in un]
        sol = [rhs[n] + _dot(ycor[n], rhs[n]) for n in un]
        qg = [q[n] * egc[n] for n in un]
        kd = [k[n] * jnp.exp(gl[n] - gch[n]) for n in un]
        egl = [jnp.exp(gl[n]) for n in un]
        s_cur, s_in, o = [state[h] for h in hs], [], []
        for c in range(grp):
            ns = [c * heads + h for h in hs]
            vn = [sol[n][:, :DN_DK] - _dot(sol[n][:, DN_DK:], s_cur[h]) for h, n in enumerate(ns)]
            o += [_dot(qg[n], s_cur[h]) + _dot(p[n], vn[h]) for h, n in enumerate(ns)]
            s_in += s_cur
            s_cur = [s_cur[h] * egl[n] + _dot(kd[n], vn[h], TN) for h, n in enumerate(ns)]
        for n, (c, h) in enumerate(units):
            o_ref[rows(c), h * DN_DK:(h + 1) * DN_DK] = o[n]
            s_ref[c, h] = s_in[n]
            y_ref[h, rows(c), :] = ycor[n]
        for h in hs:
            state[h] = s_cur[h]

    res = pl.pallas_call(
        _carrying(compute, 4, 3, 1, carried, (n_chunks // grp,)),
        name="dn_fwd_carrying" if carried else "dn_fwd", grid=(n_chunks // grp,),
        in_specs=[pl.BlockSpec((span, 3 * d), lambda i: (i, 0)),
                  pl.BlockSpec((span, 2 * LANES), lambda i: (i, 0)),
                  pl.BlockSpec((1, LANES), lambda i: (0, 0)),
                  pl.BlockSpec((1, LANES), lambda i: (0, 0))] + [ANY] * len(extra.inputs),
        out_specs=[pl.BlockSpec((span, d), lambda i: (i, 0)),
                   pl.BlockSpec((grp, heads, DN_DK, DN_DK), lambda i: (i, 0, 0, 0)),
                   pl.BlockSpec((heads, span, CHUNK), lambda i: (0, i, 0))] + [ANY] * len(extra.out_shapes),
        out_shape=[jax.ShapeDtypeStruct((t, d), F32),
                   jax.ShapeDtypeStruct((n_chunks, heads, DN_DK, DN_DK), F32),
                   jax.ShapeDtypeStruct((heads, t, CHUNK), F32)] + extra.out_shapes,
        scratch_shapes=[pltpu.VMEM((heads, DN_DK, DN_DK), F32)] + (extra.scratch() if carried else []),
        compiler_params=_cparams(("arbitrary",)),
    )(qkv, ba, alog, dtb, *extra.inputs)
    return res[:3], res[3:]


def _dn_bwd(qkv, ba, alog, dtb, dout, states, ycors, d, carried=None):
    t = qkv.shape[0]
    heads = d // DN_DK
    n_chunks = t // CHUNK
    grp = DN_GROUP if n_chunks % DN_GROUP == 0 else 1
    span = grp * CHUNK
    rev = lambda i: n_chunks // grp - 1 - i
    extra = carried or _Carried([], [], 0, None)

    def compute(qkv_ref, ba_ref, al_ref, dt_ref, do_ref, s_ref, y_ref,
                dqkv_ref, dba_ref, dal_ref, ddt_ref, dstate):
        @pl.when(pl.program_id(0) == 0)
        def _():
            dstate[...] = jnp.zeros_like(dstate)
            dal_ref[...] = jnp.zeros_like(dal_ref)
            ddt_ref[...] = jnp.zeros_like(ddt_ref)

        tril, strict, triu = _chunk_masks()
        beta, g, ea, sig_a = _beta_g(ba_ref[...], al_ref[...], dt_ref[...])
        lane = lax.broadcasted_iota(jnp.int32, (CHUNK, LANES), 1)
        sub = lax.broadcasted_iota(jnp.int32, (LANES, CHUNK), 0)
        rowc = lax.broadcasted_iota(jnp.int32, (CHUNK, 1), 0)
        hs = range(heads)
        units = [(c, h) for c in range(grp) for h in hs]
        un = range(len(units))
        rows = lambda c: slice(c * CHUNK, (c + 1) * CHUNK)
        rsum = lambda x_: jnp.sum(x_, axis=1, keepdims=True)
        gc = [_dot(jnp.where(tril, 1.0, 0.0), g[rows(c)], NN, HIGHEST) for c in range(grp)]
        gct = [m.T for m in gc]
        q = [qkv_ref[rows(c), h * DN_DK:(h + 1) * DN_DK] for c, h in units]
        k = [qkv_ref[rows(c), d + h * DN_DK:d + (h + 1) * DN_DK] for c, h in units]
        v = [qkv_ref[rows(c), 2 * d + h * DN_DK:2 * d + (h + 1) * DN_DK] for c, h in units]
        dout_h = [do_ref[rows(c), h * DN_DK:(h + 1) * DN_DK] for c, h in units]
        s0 = [s_ref[c, h] for c, h in units]
        ycor = [y_ref[h, rows(c), :] for c, h in units]
        gch = [_col_of(gc[c], lane, h) for c, h in units]
        bh = [_col_of(beta[rows(c)], lane, h) for c, h in units]
        dec = [jnp.where(tril, jnp.exp(gch[n] - _row_of(gct[c], sub, h)), 0.0) for n, (c, h) in enumerate(units)]
        egc = [jnp.exp(gch[n]) for n in un]
        gl = [jnp.sum(jnp.where(rowc == CHUNK - 1, gch[n], 0.0), axis=0, keepdims=True) for n in un]
        egl = [jnp.exp(gl[n]) for n in un]
        ekd = [jnp.exp(gl[n] - gch[n]) for n in un]
        kb = [k[n] * bh[n] for n in un]
        kd = [k[n] * ekd[n] for n in un]
        qg = [q[n] * egc[n] for n in un]
        kbg = [kb[n] * egc[n] for n in un]
        a = [jnp.where(strict, _dot(kb[n], k[n], NT) * dec[n], 0.0) for n in un]
        p = [_dot(q[n], k[n], NT) * dec[n] for n in un]
        rhs = [jnp.concatenate([v[n] * bh[n], kbg[n]], axis=1) for n in un]
        sol = [rhs[n] + _dot(ycor[n], rhs[n]) for n in un]
        w = [sol[n][:, DN_DK:] for n in un]
        vn = [sol[n][:, :DN_DK] - _dot(w[n], s0[n]) for n in un]
        dqg = [_dot(dout_h[n], s0[n], NT) for n in un]
        dp = [jnp.where(tril, _dot(dout_h[n], vn[n], NT), 0.0) for n in un]
        pdo = [_dot(p[n], dout_h[n], TN) for n in un]
        qdo = [_dot(qg[n], dout_h[n], TN) for n in un]
        ds_cur = [dstate[h] for h in hs]
        dsn, dvn = [None] * len(units), [None] * len(units)
        for c in reversed(range(grp)):
            for h in hs:
                dsn[c * heads + h] = ds_cur[h]
            for h in hs:
                n = c * heads + h
                dvn[n] = pdo[n] + _dot(kd[n], ds_cur[h])
            ds_cur = [qdo[c * heads + h] + egl[c * heads + h] * ds_cur[h]
                      - _dot(w[c * heads + h], dvn[c * heads + h], TN) for h in hs]
        dkd = [_dot(vn[n], dsn[n], NT) for n in un]
        dw = [-_dot(dvn[n], s0[n], NT) for n in un]
        dgl = [jnp.sum(rsum(dsn[n] * s0[n]), axis=0, keepdims=True) * egl[n] for n in un]
        dsol = [jnp.concatenate([dvn[n], dw[n]], axis=1) for n in un]
        drhs = [dsol[n] + _dot(ycor[n], dsol[n], TN) for n in un]
        dvb = [drhs[n][:, :DN_DK] for n in un]
        dkbg = [drhs[n][:, DN_DK:] for n in un]
        da = [jnp.where(strict, -_dot(drhs[n], sol[n], NT), 0.0) for n in un]
        dma = [da[n] * dec[n] for n in un]
        dmp = [dp[n] * dec[n] for n in un]
        dkb = [_dot(dma[n], k[n]) + dkbg[n] * egc[n] for n in un]
        dq = [_dot(dmp[n], k[n]) + dqg[n] * egc[n] for n in un]
        dk = [_dot(dma[n], kb[n], TN) + _dot(dmp[n], q[n], TN) + dkd[n] * ekd[n] + dkb[n] * bh[n] for n in un]
        e = [da[n] * a[n] + dp[n] * p[n] for n in un]
        colsum = [jnp.sum(e[n], axis=0, keepdims=True) for n in un]
        tkd = [rsum(dkd[n] * kd[n]) for n in un]
        for n, (c, h) in enumerate(units):
            dqkv_ref[rows(c), h * DN_DK:(h + 1) * DN_DK] = dq[n]
            dqkv_ref[rows(c), d + h * DN_DK:d + (h + 1) * DN_DK] = dk[n]
            dqkv_ref[rows(c), 2 * d + h * DN_DK:2 * d + (h + 1) * DN_DK] = dvb[n] * bh[n]
        for h in hs:
            dstate[h] = ds_cur[h]
        valid = lane < heads
        dal_acc = jnp.zeros((SUBLANES, LANES), F32)
        ddt_acc = jnp.zeros((SUBLANES, LANES), F32)
        for c in range(grp):
            dgc_all = jnp.zeros((CHUNK, LANES), F32)
            dbeta_all = jnp.zeros((CHUNK, LANES), F32)
            colsums = jnp.zeros((LANES, CHUNK), F32)
            for h in hs:
                n = c * heads + h
                dgc = rsum(e[n]) + rsum(dqg[n] * qg[n]) - tkd[n] + rsum(dkbg[n] * kbg[n])
                dgc = dgc + jnp.where(rowc == CHUNK - 1, dgl[n] + jnp.sum(tkd[n], axis=0, keepdims=True), 0.0)
                dgc_all = dgc_all + jnp.where(lane == h, dgc, 0.0)
                colsums = colsums + jnp.where(sub == h, colsum[n], 0.0)
                dbeta_all = dbeta_all + jnp.where(lane == h, rsum(dkb[n] * k[n]) + rsum(dvb[n] * v[n]), 0.0)
            dg = _dot(jnp.where(triu, 1.0, 0.0), dgc_all - colsums.T, NN, HIGHEST)
            beta_c = beta[rows(c)]
            dbl = jnp.where(valid, dbeta_all * beta_c * (1.0 - beta_c), 0.0)
            dal = jnp.where(valid, -dg * ea * sig_a[rows(c)], 0.0)
            dba_ref[rows(c), :LANES] = dbl.astype(dba_ref.dtype)
            dba_ref[rows(c), LANES:] = dal.astype(dba_ref.dtype)
            dal_acc = dal_acc + _fold8(jnp.where(valid, dg * g[rows(c)], 0.0))
            ddt_acc = ddt_acc + _fold8(dal)
        dal_ref[...] += dal_acc
        ddt_ref[...] += ddt_acc

    res = pl.pallas_call(
        _carrying(compute, 7, 4, 1, carried, (n_chunks // grp,)),
        name="dn_bwd_carrying" if carried else "dn_bwd", grid=(n_chunks // grp,),
        in_specs=[pl.BlockSpec((span, 3 * d), lambda i: (rev(i), 0)),
                  pl.BlockSpec((span, 2 * LANES), lambda i: (rev(i), 0)),
                  pl.BlockSpec((1, LANES), lambda i: (0, 0)),
                  pl.BlockSpec((1, LANES), lambda i: (0, 0)),
                  pl.BlockSpec((span, d), lambda i: (rev(i), 0)),
                  pl.BlockSpec((grp, heads, DN_DK, DN_DK), lambda i: (rev(i), 0, 0, 0)),
                  pl.BlockSpec((heads, span, CHUNK), lambda i: (0, rev(i), 0))] + [ANY] * len(extra.inputs),
        out_specs=[pl.BlockSpec((span, 3 * d), lambda i: (rev(i), 0)),
                   pl.BlockSpec((span, 2 * LANES), lambda i: (rev(i), 0)),
                   pl.BlockSpec((SUBLANES, LANES), lambda i: (0, 0)),
                   pl.BlockSpec((SUBLANES, LANES), lambda i: (0, 0))] + [ANY] * len(extra.out_shapes),
        out_shape=[jax.ShapeDtypeStruct((t, 3 * d), F32),
                   jax.ShapeDtypeStruct((t, 2 * LANES), ACT),
                   jax.ShapeDtypeStruct((SUBLANES, LANES), F32),
                   jax.ShapeDtypeStruct((SUBLANES, LANES), F32)] + extra.out_shapes,
        scratch_shapes=[pltpu.VMEM((heads, DN_DK, DN_DK), F32)] + (extra.scratch() if carried else []),
        compiler_params=_cparams(("arbitrary",)),
    )(qkv, ba, alog, dtb, dout, states, ycors, *extra.inputs)
    return res[:4], res[4:]


def _sgu_mask():
    row = lax.broadcasted_iota(jnp.int32, (SGU_BLOCK, SGU_BLOCK), 0)
    col = lax.broadcasted_iota(jnp.int32, (SGU_BLOCK, SGU_BLOCK), 1)
    sh = int(math.log2(CHUNK))
    return lax.shift_right_logical(row, sh) >= lax.shift_right_logical(col, sh)


def _gate_sgu_fwd(o, projm, onw, lng, lnb, ws, bst, d):
    t = o.shape[0]
    heads, groups = d // DN_DK, d // SGU_GROUP_DIM
    tb = _tile(t, SGU_WINDOWS * SGU_BLOCK, SGU_BLOCK)
    row_spec = pl.BlockSpec((1, d), lambda i: (0, 0))

    def body(o_ref, z_ref, u_ref, v_ref, onw_ref, lng_ref, lnb_ref, ws_ref, bst_ref, ya_ref, yb_ref):
        for h in range(heads):
            cols = slice(h * DN_DK, (h + 1) * DN_DK)
            oh, zh = o_ref[:, cols], z_ref[:, cols]
            r = lax.rsqrt(jnp.mean(oh * oh, axis=1, keepdims=True) + RMS_EPS)
            ya_ref[:, cols] = (oh * r * onw_ref[:, cols] * (zh * _sigmoid(zh))).astype(ya_ref.dtype)
        xhat, _ = _ln_hat(_gelu(v_ref[...]))
        vgn = xhat * lng_ref[...] + lnb_ref[...]
        mask = _sgu_mask()
        lane = lax.broadcasted_iota(jnp.int32, (SGU_BLOCK, LANES), 1)
        bst_v = bst_ref[...]
        for gi in range(groups):
            cols = slice(gi * SGU_GROUP_DIM, (gi + 1) * SGU_GROUP_DIM)
            wsg = jnp.where(mask, ws_ref[gi], 0.0)
            bias = _col_of(bst_v, lane, gi)
            for win in range(tb // SGU_BLOCK):
                rows = slice(win * SGU_BLOCK, (win + 1) * SGU_BLOCK)
                sp = _dot(wsg, vgn[rows, cols]) + bias
                yb_ref[rows, cols] = (_gelu(u_ref[rows, cols]) * sp).astype(yb_ref.dtype)

    return pl.pallas_call(
        body, name="gate_sgu_fwd", grid=(t // tb,),
        in_specs=[pl.BlockSpec((tb, d), lambda i: (i, 0)),
                  pl.BlockSpec((tb, d), lambda i: (i, 3)),
                  pl.BlockSpec((tb, d), lambda i: (i, 4)),
                  pl.BlockSpec((tb, d), lambda i: (i, 5)),
                  row_spec, row_spec, row_spec,
                  pl.BlockSpec((groups, SGU_BLOCK, SGU_BLOCK), lambda i: (0, 0, 0)),
                  pl.BlockSpec((SGU_BLOCK, LANES), lambda i: (0, 0))],
        out_specs=[pl.BlockSpec((tb, d), lambda i: (i, 0)), pl.BlockSpec((tb, d), lambda i: (i, 0))],
        out_shape=[jax.ShapeDtypeStruct((t, d), ACT), jax.ShapeDtypeStruct((t, d), ACT)],
        compiler_params=_cparams(("parallel",)),
    )(o, projm, projm, projm, onw, lng, lnb, ws, bst)


def _gate_sgu_bwd(dya, dyb, o, projm, onw, lng, lnb, ws, bst, dprojm, d, carried=None):
    t = o.shape[0]
    heads, groups = d // DN_DK, d // SGU_GROUP_DIM
    tb = _tile(t, SGU_WINDOWS * SGU_BLOCK, SGU_BLOCK)
    extra = carried or _Carried([], [], 0, None)
    row_spec = pl.BlockSpec((1, d), lambda i: (0, 0))
    acc_row = pl.BlockSpec((SUBLANES, d), lambda i: (0, 0))

    def body(dya_ref, dyb_ref, o_ref, z_ref, u_ref, v_ref, onw_ref, lng_ref, lnb_ref, ws_ref, bst_ref, alias_ref,
             do_ref, dp_ref, donw_ref, dlng_ref, dlnb_ref, dws_ref, dbst_ref):
        @pl.when(pl.program_id(0) == 0)
        def _():
            for r_ in (donw_ref, dlng_ref, dlnb_ref, dws_ref, dbst_ref):
                r_[...] = jnp.zeros_like(r_)

        donw = jnp.zeros((SUBLANES, DN_DK), F32)
        for h in range(heads):
            cols = slice(h * DN_DK, (h + 1) * DN_DK)
            oh, zh, dyah, wh = o_ref[:, cols], z_ref[:, cols], dya_ref[:, cols], onw_ref[:, cols]
            r = lax.rsqrt(jnp.mean(oh * oh, axis=1, keepdims=True) + RMS_EPS)
            on = oh * r
            sz = _sigmoid(zh)
            silu_z = zh * sz
            don = dyah * wh * silu_z
            dp_ref[:, cols] = (dyah * on * wh * (sz * (1.0 + zh * (1.0 - sz)))).astype(dp_ref.dtype)
            donw = donw + _fold8(dyah * on * silu_z)
            do_ref[:, cols] = r * (don - on * jnp.mean(don * on, axis=1, keepdims=True))
        donw_ref[...] += donw

        vgp, up = v_ref[...], u_ref[...]
        xhat, rstd = _ln_hat(_gelu(vgp))
        lng_v = lng_ref[...]
        vgn = xhat * lng_v + lnb_ref[...]
        ua = _gelu(up)
        mask = _sgu_mask()
        lane = lax.broadcasted_iota(jnp.int32, (SGU_BLOCK, LANES), 1)
        bst_v = bst_ref[...]
        dbst = jnp.zeros((SGU_BLOCK, LANES), F32)
        dvgn_parts, dua_parts = [], []
        for gi in range(groups):
            cols = slice(gi * SGU_GROUP_DIM, (gi + 1) * SGU_GROUP_DIM)
            wsg = jnp.where(mask, ws_ref[gi], 0.0)
            bias = _col_of(bst_v, lane, gi)
            dws = jnp.zeros((SGU_BLOCK, SGU_BLOCK), F32)
            dvgn_g, dua_g = [], []
            for win in range(tb // SGU_BLOCK):
                rows = slice(win * SGU_BLOCK, (win + 1) * SGU_BLOCK)
                vg_g, dyb_g = vgn[rows, cols], dyb_ref[rows, cols]
                sp = _dot(wsg, vg_g) + bias
                dsp = dyb_g * ua[rows, cols]
                dua_g.append(dyb_g * sp)
                dws = dws + _dot(dsp, vg_g, NT)
                dbst = dbst + jnp.where(lane == gi, jnp.sum(dsp, axis=1, keepdims=True), 0.0)
                dvgn_g.append(_dot(wsg, dsp, TN))
            dws_ref[gi] += jnp.where(mask, dws, 0.0)
            dvgn_parts.append(jnp.concatenate(dvgn_g, axis=0))
            dua_parts.append(jnp.concatenate(dua_g, axis=0))
        dbst_ref[...] += dbst
        dvgn = jnp.concatenate(dvgn_parts, axis=1)
        dua = jnp.concatenate(dua_parts, axis=1)
        dlng_ref[...] += _fold8(dvgn * xhat)
        dlnb_ref[...] += _fold8(dvgn)
        dvga = _ln_bwd(dvgn * lng_v, xhat, rstd)
        dp_ref[:, d:2 * d] = (dua * _gelu_grad(up)).astype(dp_ref.dtype)
        dp_ref[:, 2 * d:] = (dvga * _gelu_grad(vgp)).astype(dp_ref.dtype)

    res = pl.pallas_call(
        _carrying(body, 12, 7, 0, carried, (t // tb,)),
        name="gate_sgu_bwd_carrying" if carried else "gate_sgu_bwd", grid=(t // tb,),
        in_specs=[pl.BlockSpec((tb, d), lambda i: (i, 0)),
                  pl.BlockSpec((tb, d), lambda i: (i, 0)),
                  pl.BlockSpec((tb, d), lambda i: (i, 0)),
                  pl.BlockSpec((tb, d), lambda i: (i, 3)),
                  pl.BlockSpec((tb, d), lambda i: (i, 4)),
                  pl.BlockSpec((tb, d), lambda i: (i, 5)),
                  row_spec, row_spec, row_spec,
                  pl.BlockSpec((groups, SGU_BLOCK, SGU_BLOCK), lambda i: (0, 0, 0)),
                  pl.BlockSpec((SGU_BLOCK, LANES), lambda i: (0, 0)),
                  ANY] + [ANY] * len(extra.inputs),
        out_specs=[pl.BlockSpec((tb, d), lambda i: (i, 0)),
                   pl.BlockSpec((tb, 3 * d), lambda i: (i, 1)),
                   pl.BlockSpec((SUBLANES, DN_DK), lambda i: (0, 0)),
                   acc_row, acc_row,
                   pl.BlockSpec((groups, SGU_BLOCK, SGU_BLOCK), lambda i: (0, 0, 0)),
                   pl.BlockSpec((SGU_BLOCK, LANES), lambda i: (0, 0))] + [ANY] * len(extra.out_shapes),
        out_shape=[jax.ShapeDtypeStruct((t, d), F32),
                   jax.ShapeDtypeStruct(dprojm.shape, dprojm.dtype),
                   jax.ShapeDtypeStruct((SUBLANES, DN_DK), F32),
                   jax.ShapeDtypeStruct((SUBLANES, d), F32),
                   jax.ShapeDtypeStruct((SUBLANES, d), F32),
                   jax.ShapeDtypeStruct((groups, SGU_BLOCK, SGU_BLOCK), F32),
                   jax.ShapeDtypeStruct((SGU_BLOCK, LANES), F32)] + extra.out_shapes,
        input_output_aliases={11: 1},
        scratch_shapes=extra.scratch() if carried else [],
        compiler_params=_cparams(("arbitrary",)),
    )(dya, dyb, o, projm, projm, projm, onw, lng, lnb, ws, bst, dprojm, *extra.inputs)
    return res[:7], res[7:]


def _mix_fwd(ya, yb, projm, x, wpa, wpb, wo, g1, b1, d, tb):
    t = x.shape[0]
    blk = pl.BlockSpec((tb, d), lambda i: (i, 0))
    wspec = pl.BlockSpec((d, d), lambda i: (0, 0))
    row_spec = pl.BlockSpec((1, d), lambda i: (0, 0))

    def body(ya_ref, yb_ref, ga_ref, gb_ref, x_ref, wpa_ref, wpb_ref, wo_ref, g_ref, b_ref,
             pa_ref, pb_ref, m_ref, h_ref, x1_ref, x1b_ref):
        pa = _dot(ya_ref[...], wpa_ref[...])
        pb = _dot(yb_ref[...], wpb_ref[...])
        m = _sigmoid(ga_ref[...]) * pa + _sigmoid(gb_ref[...]) * pb
        hres = ALPHA * x_ref[...] + _dot(m, wo_ref[...])
        xhat, _ = _ln_hat(hres)
        x1 = xhat * g_ref[...] + b_ref[...]
        pa_ref[...] = pa
        pb_ref[...] = pb
        m_ref[...] = m.astype(m_ref.dtype)
        h_ref[...] = hres
        x1_ref[...] = x1
        x1b_ref[...] = x1.astype(x1b_ref.dtype)

    f32_out = jax.ShapeDtypeStruct((t, d), F32)
    bf_out = jax.ShapeDtypeStruct((t, d), ACT)
    return pl.pallas_call(
        body, name="mix_fwd", grid=(t // tb,),
        in_specs=[blk, blk, pl.BlockSpec((tb, d), lambda i: (i, 6)), pl.BlockSpec((tb, d), lambda i: (i, 7)),
                  blk, wspec, wspec, wspec, row_spec, row_spec],
        out_specs=[blk] * 6,
        out_shape=[f32_out, f32_out, bf_out, f32_out, f32_out, bf_out],
        compiler_params=_cparams(("parallel",)),
    )(ya, yb, projm, projm, x, wpa, wpb, wo, g1, b1)


def _mix_bwd(dmix, pa, pb, projm, wpa, wpb, wo, d, tb):
    t = dmix.shape[0]
    blk = pl.BlockSpec((tb, d), lambda i: (i, 0))
    wspec = pl.BlockSpec((d, d), lambda i: (0, 0))

    def body(dmix_ref, pa_ref, pb_ref, ga_ref, gb_ref, wpa_ref, wpb_ref, wo_ref,
             dpa_ref, dpb_ref, dya_ref, dyb_ref, dg_ref):
        dm = _dot(dmix_ref[...], wo_ref[...], NT)
        sa, sb = _sigmoid(ga_ref[...]), _sigmoid(gb_ref[...])
        dpa, dpb = dm * sa, dm * sb
        dpa_ref[...] = dpa.astype(dpa_ref.dtype)
        dpb_ref[...] = dpb.astype(dpb_ref.dtype)
        dg_ref[:, :d] = (dm * pa_ref[...] * sa * (1.0 - sa)).astype(dg_ref.dtype)
        dg_ref[:, d:] = (dm * pb_ref[...] * sb * (1.0 - sb)).astype(dg_ref.dtype)
        dya_ref[...] = _dot(dpa, wpa_ref[...], NT)
        dyb_ref[...] = _dot(dpb, wpb_ref[...], NT)

    return pl.pallas_call(
        body, name="mix_bwd", grid=(t // tb,),
        in_specs=[blk, blk, blk, pl.BlockSpec((tb, d), lambda i: (i, 6)), pl.BlockSpec((tb, d), lambda i: (i, 7)),
                  wspec, wspec, wspec],
        out_specs=[blk, blk, blk, blk, pl.BlockSpec((tb, 2 * d), lambda i: (i, 3))],
        out_shape=[jax.ShapeDtypeStruct((t, d), ACT), jax.ShapeDtypeStruct((t, d), ACT),
                   jax.ShapeDtypeStruct((t, d), F32), jax.ShapeDtypeStruct((t, d), F32),
                   jax.ShapeDtypeStruct((t, 8 * d), ACT)],
        compiler_params=_cparams(("parallel",)),
    )(dmix, pa, pb, projm, projm, wpa, wpb, wo)


def _ffn_tail_fwd(gu, wd, x1, g, b, tb):
    t, d = x1.shape
    f = wd.shape[0]
    fc = _tile(f, MM_TILE)
    blk = pl.BlockSpec((tb, d), lambda i: (i, 0))
    row_spec = pl.BlockSpec((1, d), lambda i: (0, 0))

    def body(gu_ref, wd_ref, x_ref, g_ref, b_ref, a_ref, h_ref, y_ref, yb_ref):
        ffn = jnp.zeros((tb, d), F32)
        for c in range(f // fc):
            gp = gu_ref[:, c * fc:(c + 1) * fc].astype(F32)
            act = (gp * _sigmoid(gp) * gu_ref[:, f + c * fc:f + (c + 1) * fc].astype(F32)).astype(a_ref.dtype)
            a_ref[:, c * fc:(c + 1) * fc] = act
            ffn = ffn + _dot(act, wd_ref[c * fc:(c + 1) * fc, :])
        hres = ALPHA * x_ref[...] + ffn
        xhat, _ = _ln_hat(hres)
        y = xhat * g_ref[...] + b_ref[...]
        h_ref[...] = hres
        y_ref[...] = y
        yb_ref[...] = y.astype(yb_ref.dtype)

    return pl.pallas_call(
        body, name="ffn_tail_fwd", grid=(t // tb,),
        in_specs=[pl.BlockSpec((tb, 2 * f), lambda i: (i, 0)), pl.BlockSpec((f, d), lambda i: (0, 0)),
                  blk, row_spec, row_spec],
        out_specs=[pl.BlockSpec((tb, f), lambda i: (i, 0)), blk, blk, blk],
        out_shape=[jax.ShapeDtypeStruct((t, f), ACT), jax.ShapeDtypeStruct((t, d), F32),
                   jax.ShapeDtypeStruct((t, d), F32), jax.ShapeDtypeStruct((t, d), ACT)],
        compiler_params=_cparams(("parallel",)),
    )(gu, wd, x1, g, b)


def _ffn_tail_bwd(dh, wd, gu, tb):
    t, d = dh.shape
    f = wd.shape[0]
    fc = _tile(f, MM_TILE)

    def body(dh_ref, wd_ref, gu_ref, dgu_ref):
        dh_v = dh_ref[...]
        for c in range(f // fc):
            da = _dot(dh_v, wd_ref[c * fc:(c + 1) * fc, :], NT)
            gp = gu_ref[:, c * fc:(c + 1) * fc].astype(F32)
            sg = _sigmoid(gp)
            dgu_ref[:, c * fc:(c + 1) * fc] = (
                da * gu_ref[:, f + c * fc:f + (c + 1) * fc].astype(F32) * sg * (1.0 + gp * (1.0 - sg))
            ).astype(dgu_ref.dtype)
            dgu_ref[:, f + c * fc:f + (c + 1) * fc] = (da * gp * sg).astype(dgu_ref.dtype)

    return pl.pallas_call(
        body, name="ffn_tail_bwd", grid=(t // tb,),
        in_specs=[pl.BlockSpec((tb, d), lambda i: (i, 0)), pl.BlockSpec((f, d), lambda i: (0, 0)),
                  pl.BlockSpec((tb, 2 * f), lambda i: (i, 0))],
        out_specs=pl.BlockSpec((tb, 2 * f), lambda i: (i, 0)),
        out_shape=jax.ShapeDtypeStruct((t, 2 * f), ACT),
        compiler_params=_cparams(("parallel",)),
    )(dh, wd, gu)


def _ffn_head_bwd(dgu, wgu, dh2, hres, g, tb):
    t, d = dh2.shape
    f2 = wgu.shape[1]
    blk = pl.BlockSpec((tb, d), lambda i: (i, 0))
    acc = pl.BlockSpec((SUBLANES, d), lambda i: (0, 0))

    def body(dgu_ref, w_ref, dh2_ref, h_ref, g_ref, dh_ref, dhb_ref, dg_ref, db_ref):
        @pl.when(pl.program_id(0) == 0)
        def _():
            dg_ref[...] = jnp.zeros_like(dg_ref)
            db_ref[...] = jnp.zeros_like(db_ref)

        dy_v = _dot(dgu_ref[...], w_ref[...], NT) + ALPHA * dh2_ref[...]
        xhat, r = _ln_hat(h_ref[...])
        dh = _ln_bwd(dy_v * g_ref[...], xhat, r)
        dh_ref[...] = dh
        dhb_ref[...] = dh.astype(dhb_ref.dtype)
        dg_ref[...] += _fold8(dy_v * xhat)
        db_ref[...] += _fold8(dy_v)

    return pl.pallas_call(
        body, name="ffn_head_bwd", grid=(t // tb,),
        in_specs=[pl.BlockSpec((tb, f2), lambda i: (i, 0)), pl.BlockSpec((d, f2), lambda i: (0, 0)),
                  blk, blk, pl.BlockSpec((1, d), lambda i: (0, 0))],
        out_specs=[blk, blk, acc, acc],
        out_shape=[jax.ShapeDtypeStruct((t, d), F32), jax.ShapeDtypeStruct((t, d), ACT),
                   jax.ShapeDtypeStruct((SUBLANES, d), F32), jax.ShapeDtypeStruct((SUBLANES, d), F32)],
        compiler_params=_cparams(("arbitrary",)),
    )(dgu, wgu, dh2, hres, g)


def _loss_ln_bwd(y, target, hres, g, tb):
    t, d = y.shape
    blk = pl.BlockSpec((tb, d), lambda i: (i, 0))
    acc = pl.BlockSpec((SUBLANES, d), lambda i: (0, 0))

    def body(y_ref, t_ref, h_ref, g_ref, dh_ref, dhb_ref, dg_ref, db_ref, l_ref):
        @pl.when(pl.program_id(0) == 0)
        def _():
            for r_ in (dg_ref, db_ref, l_ref):
                r_[...] = jnp.zeros_like(r_)

        err = y_ref[...] - t_ref[...]
        dy_v = err * (1.0 / d)
        sq = _fold8(err * err)
        part = sq[:, :LANES]
        for c in range(1, d // LANES):
            part = part + sq[:, c * LANES:(c + 1) * LANES]
        l_ref[...] += part
        xhat, r = _ln_hat(h_ref[...])
        dh = _ln_bwd(dy_v * g_ref[...], xhat, r)
        dh_ref[...] = dh
        dhb_ref[...] = dh.astype(dhb_ref.dtype)
        dg_ref[...] += _fold8(dy_v * xhat)
        db_ref[...] += _fold8(dy_v)

    res = pl.pallas_call(
        body, name="loss_ln_bwd", grid=(t // tb,),
        in_specs=[blk, blk, blk, pl.BlockSpec((1, d), lambda i: (0, 0))],
        out_specs=[blk, blk, acc, acc, pl.BlockSpec((SUBLANES, LANES), lambda i: (0, 0))],
        out_shape=[jax.ShapeDtypeStruct((t, d), F32), jax.ShapeDtypeStruct((t, d), ACT),
                   jax.ShapeDtypeStruct((SUBLANES, d), F32), jax.ShapeDtypeStruct((SUBLANES, d), F32),
                   jax.ShapeDtypeStruct((SUBLANES, LANES), F32)],
        compiler_params=_cparams(("arbitrary",)),
    )(y, target, hres, g)
    return res[:4], res[4]


def _adamw(w, g, m, v):
    shape = w.shape
    cols = shape[-1]
    w2, g2, m2, v2 = (a.reshape(-1, cols) for a in (w, g, m, v))
    rows = w2.shape[0]
    tr = _tile(rows, 256, SUBLANES)
    blk = pl.BlockSpec((tr, cols), lambda i: (i, 0))

    def body(w_ref, g_ref, m_ref, v_ref, d_ref, nm_ref, nv_ref):
        g_v = g_ref[...]
        nm = ADAM_B1 * m_ref[...] + (1.0 - ADAM_B1) * g_v
        nv = ADAM_B2 * v_ref[...] + (1.0 - ADAM_B2) * (g_v * g_v)
        m_hat = nm / (1.0 - ADAM_B1 ** ADAM_STEP)
        v_hat = nv / (1.0 - ADAM_B2 ** ADAM_STEP)
        d_ref[...] = -ADAM_LR * (m_hat / (jnp.sqrt(v_hat) + ADAM_EPS) + ADAM_WD * w_ref[...])
        nm_ref[...] = nm
        nv_ref[...] = nv

    out = jax.ShapeDtypeStruct((rows, cols), F32)
    res = pl.pallas_call(
        body, name="adamw", grid=(rows // tr,),
        in_specs=[blk] * 4, out_specs=[blk] * 3, out_shape=[out] * 3,
        compiler_params=_cparams(("parallel",)),
    )(w2, g2, m2, v2)
    return tuple(r.reshape(shape) for r in res)


def _place():
    x, y, c = lax.axis_index("x"), lax.axis_index("y"), lax.axis_index("c")
    return x, y, c, [(1 - x, y), (x, 1 - y), (1 - x, 1 - y)]


def _remote(src, dst, send_sems, recv_sems, k, to):
    return pltpu.make_async_remote_copy(src_ref=src, dst_ref=dst, send_sem=send_sems.at[k],
                                        recv_sem=recv_sems.at[k], device_id=to, device_id_type=MESH)


class _Carried:
    def __init__(self, inputs, out_shapes, n_sems, copies):
        self.inputs, self.out_shapes, self.n_sems, self.copies = list(inputs), list(out_shapes), n_sems, copies

    def scratch(self):
        return [pltpu.SemaphoreType.DMA((self.n_sems,)), pltpu.SemaphoreType.DMA((self.n_sems,))]


def _join_plans(first, second):
    ni, no, ns = len(first.inputs), len(first.out_shapes), first.n_sems

    def copies(in_refs, out_refs, send_sems, recv_sems):
        one = _phases(first.copies(in_refs[:ni], out_refs[:no], send_sems, recv_sems))
        two = _phases(second.copies(in_refs[ni:], out_refs[no:], send_sems.at[pl.ds(ns, second.n_sems)],
                                    recv_sems.at[pl.ds(ns, second.n_sems)]))

        def both(k):
            def run():
                one[k]()
                two[k]()
            return run

        return both(0), both(1), both(2)

    return _Carried(first.inputs + second.inputs, first.out_shapes + second.out_shapes, ns + second.n_sems, copies)


def _run_comm(name, plan):
    n_in, n_out = len(plan.inputs), len(plan.out_shapes)

    def body(*refs):
        for phase in _phases(plan.copies(refs[:n_in], refs[n_in:n_in + n_out], refs[-2], refs[-1])):
            phase()

    return pl.pallas_call(
        body, name=name, in_specs=[ANY] * n_in, out_specs=[ANY] * n_out, out_shape=plan.out_shapes,
        scratch_shapes=plan.scratch(),
    )(*plan.inputs)


def _half_rows(rows, core):
    if rows % (4 * SUBLANES):
        return None
    return pl.ds(pl.multiple_of(core * (rows // 2), 2 * SUBLANES), rows // 2)


def _all_gather_plan(shards):
    n = len(shards)

    def copies(x_refs, out_refs, send_sems, recv_sems):
        x, y, c, chips = _place()
        sibling = (x, y, 1 - c)
        mine = 2 * x + y
        split = [_half_rows(x_refs[t].shape[0], c) is not None for t in range(n)]

        def src(t):
            return x_refs[t].at[_half_rows(x_refs[t].shape[0], c)] if split[t] else x_refs[t]

        def slot(t, chip_idx, core):
            rows = _half_rows(x_refs[t].shape[0], core)
            return out_refs[t].at[chip_idx, rows] if split[t] else out_refs[t].at[chip_idx]

        def first():
            return [_remote(src(t), slot(t, mine, c), send_sems, recv_sems, 6 * t + j, (cx, cy, c))
                    for j, (cx, cy) in enumerate(chips) for t in range(n)]

        def start():
            for cp in first():
                cp.start()

        def passed():
            return [_remote(slot(t, 2 * cx + cy, c), slot(t, 2 * cx + cy, c), send_sems, recv_sems, 6 * t + 3 + j,
                            sibling) for j, (cx, cy) in enumerate(chips) for t in range(n) if split[t]]

        def middle():
            for j, (cx, cy) in enumerate(chips):
                for t in range(n):
                    theirs = slot(t, 2 * cx + cy, c)
                    _remote(theirs, theirs, send_sems, recv_sems, 6 * t + j, (cx, cy, c)).wait_recv()
            for cp in passed():
                cp.start()

        def finish():
            for j, (cx, cy) in enumerate(chips):
                for t in range(n):
                    if split[t]:
                        other = slot(t, 2 * cx + cy, 1 - c)
                        _remote(other, other, send_sems, recv_sems, 6 * t + 3 + j, sibling).wait_recv()
            for cp in first() + passed():
                cp.wait_send()

        return start, middle, finish

    return _Carried(shards, [jax.ShapeDtypeStruct((N_CHIPS,) + s.shape, s.dtype) for s in shards], 6 * n, copies)


def _sibling_exchange_plan(grads, small=None):
    n = len(grads)
    extra = [] if small is None else [small]

    def copies(in_refs, out_refs, send_sems, recv_sems):
        x, y, c, _ = _place()
        sibling = (x, y, 1 - c)

        def all_copies():
            cps = [_remote(in_refs[t].at[:, _half_rows(in_refs[t].shape[1], 1 - c), :], out_refs[t],
                           send_sems, recv_sems, t, sibling) for t in range(n)]
            if extra:
                cps.append(_remote(in_refs[n], out_refs[n], send_sems, recv_sems, n, sibling))
            return cps

        def start():
            for cp in all_copies():
                cp.start()

        def finish():
            for cp in all_copies():
                cp.wait()

        return start, finish

    shapes = [jax.ShapeDtypeStruct((g.shape[0], g.shape[1] // 2, g.shape[2]), g.dtype) for g in grads]
    shapes += [jax.ShapeDtypeStruct(s.shape, s.dtype) for s in extra]
    return _Carried(list(grads) + extra, shapes, n + 1, copies)


def _chip_exchange_plan(travel, small=None):
    n = len(travel)
    extra = [] if small is None else [small]

    def copies(in_refs, out_refs, send_sems, recv_sems):
        x, y, c, chips = _place()
        mine = 2 * x + y

        def all_copies():
            cps = []
            for j, (cx, cy) in enumerate(chips):
                to = (cx, cy, c)
                for t in range(n):
                    cps.append(_remote(in_refs[t].at[2 * cx + cy], out_refs[t].at[mine], send_sems, recv_sems,
                                       3 * t + j, to))
                if extra:
                    cps.append(_remote(in_refs[n], out_refs[n].at[mine], send_sems, recv_sems, 3 * n + j, to))
            return cps

        def start():
            for cp in all_copies():
                cp.start()

        def finish():
            for cp in all_copies():
                cp.wait()

        return start, finish

    shapes = [jax.ShapeDtypeStruct(g.shape, g.dtype) for g in travel]
    shapes += [jax.ShapeDtypeStruct((N_CHIPS,) + s.shape, s.dtype) for s in extra]
    return _Carried(list(travel) + extra, shapes, 3 * n + 3, copies)


def _sibling_merge_plan(reduced):
    n = len(reduced)

    def copies(in_refs, out_refs, send_sems, recv_sems):
        x, y, c, _ = _place()

        def all_copies():
            return [_remote(in_refs[t], out_refs[t], send_sems, recv_sems, t, (x, y, 1 - c)) for t in range(n)]

        def start():
            for cp in all_copies():
                cp.start()

        def finish():
            for cp in all_copies():
                cp.wait()

        return start, finish

    return _Carried(reduced, [jax.ShapeDtypeStruct(r.shape, r.dtype) for r in reduced], n, copies)


def _pair_sum(place, grad, land):
    n, r, c = grad.shape
    half = r // 2
    tr = _tile(half, 256, SUBLANES)
    nb = half // tr

    def body(place_ref, a_ref, b_ref, travel_ref, own_ref):
        total = a_ref[0] + b_ref[0]
        travel_ref[0] = total.astype(travel_ref.dtype)

        @pl.when(pl.program_id(1) == place_ref[1])
        def _():
            own_ref[...] = total

    return pl.pallas_call(
        body, name="grad_pair_sum",
        grid_spec=pltpu.PrefetchScalarGridSpec(
            num_scalar_prefetch=1, grid=(nb, n),
            in_specs=[pl.BlockSpec((1, tr, c), lambda i, s, p: (s, p[0] * nb + i, 0)),
                      pl.BlockSpec((1, tr, c), lambda i, s, p: (s, i, 0))],
            out_specs=[pl.BlockSpec((1, tr, c), lambda i, s, p: (s, i, 0)),
                       pl.BlockSpec((tr, c), lambda i, s, p: (i, 0))]),
        out_shape=[jax.ShapeDtypeStruct((n, half, c), BF16), jax.ShapeDtypeStruct((half, c), F32)],
        compiler_params=_cparams(("parallel", "arbitrary")),
    )(place, grad, land)


def _chip_sum(place, own, land, name):
    n, r, c = land.shape
    tr = _tile(r, 256, SUBLANES)

    def body(place_ref, own_ref, land_ref, o_ref):
        mine = place_ref[1]
        acc = jnp.zeros(o_ref.shape, F32)
        for s in range(n):
            acc = acc + jnp.where(mine == s, own_ref[...], land_ref[s].astype(F32))
        o_ref[...] = acc

    return pl.pallas_call(
        body, name=name,
        grid_spec=pltpu.PrefetchScalarGridSpec(
            num_scalar_prefetch=1, grid=(r // tr,),
            in_specs=[pl.BlockSpec((tr, c), lambda i, p: (i, 0)),
                      pl.BlockSpec((n, tr, c), lambda i, p: (0, i, 0))],
            out_specs=pl.BlockSpec((tr, c), lambda i, p: (i, 0))),
        out_shape=jax.ShapeDtypeStruct((r, c), F32),
        compiler_params=_cparams(("parallel",)),
    )(place, own, land)


def _add2(a, b):
    rows = a.shape[0]
    tr = _tile(rows, 256, SUBLANES)
    blk = pl.BlockSpec((tr, a.shape[1]), lambda i: (i, 0))

    def body(a_ref, b_ref, o_ref):
        o_ref[...] = a_ref[...] + b_ref[...]

    return pl.pallas_call(
        body, name="grad_small_pair_sum", grid=(rows // tr,), in_specs=[blk, blk], out_specs=blk,
        out_shape=jax.ShapeDtypeStruct(a.shape, F32), compiler_params=_cparams(("parallel",)),
    )(a, b)


def _merge_halves(place, mine, other):
    first_core = place[0] == 0
    return jnp.concatenate([jnp.where(first_core, mine, other), jnp.where(first_core, other, mine)], axis=0)


_BIG = (("w_in", 2), ("w_pa", 1), ("w_pb", 1), ("w_o", 1), ("w_ffn_gate", 2), ("w_ffn_up", 2),
        ("w_ffn_down", 1))
_SMALL = ("conv_w", "a_log", "dt_bias", "o_norm_w", "sgu_ln_g", "sgu_ln_b", "w_s", "b_s",
          "ln1_g", "ln1_b", "ln2_g", "ln2_b")


def _pack_small(arrays):
    pieces = []
    for a in arrays:
        if a.shape[-1] % LANES == 0:
            a2 = a.reshape(-1, LANES)
        else:
            a2 = jnp.pad(a.reshape(-1, a.shape[-1]), ((0, 0), (0, LANES - a.shape[-1])))
        pieces.append(jnp.pad(a2, ((0, -a2.shape[0] % SUBLANES), (0, 0))))
    return jnp.concatenate(pieces, axis=0)


def _unpack_small(buf, like):
    out, off = [], 0
    for a in like:
        if a.shape[-1] % LANES == 0:
            rows = a.size // LANES
            out.append(buf[off:off + rows].reshape(a.shape))
        else:
            rows = a.size // a.shape[-1]
            out.append(buf[off:off + rows, :a.shape[-1]].reshape(a.shape))
        off += -(-rows // SUBLANES) * SUBLANES
    return out


def _unshard(gathered, local, chip, axis):
    parts = [jnp.where(chip == s, local, gathered[s]) for s in range(N_CHIPS)]
    return jnp.concatenate(parts, axis=axis - 1)


def _to_shards(full, axis):
    l, r, c = full.shape
    if axis == 1:
        return full.reshape(l, N_CHIPS, r // N_CHIPS, c)
    return jnp.transpose(full.reshape(l, r, N_CHIPS, c // N_CHIPS), (0, 2, 1, 3))


def _row(v, width=None):
    v = v.reshape(1, -1).astype(F32)
    if width is not None and v.shape[1] < width:
        v = jnp.pad(v, ((0, 0), (0, width - v.shape[1])))
    return v


def _layer_consts(p, l, d):
    heads = d // DN_DK
    return dict(
        alog=_row(p["a_log"][l], LANES), dtb=_row(p["dt_bias"][l], LANES),
        onw=_row(jnp.tile(p["o_norm_w"][l], heads)),
        lng=_row(p["sgu_ln_g"][l]), lnb=_row(p["sgu_ln_b"][l]),
        ws=p["w_s"][l].astype(F32),
        bst=jnp.pad(p["b_s"][l].T, ((0, 0), (0, LANES - p["b_s"].shape[1]))),
        g1=_row(p["ln1_g"][l]), b1=_row(p["ln1_b"][l]), g2=_row(p["ln2_g"][l]), b2=_row(p["ln2_b"][l]))


class _NoComm:
    def with_proj_main(self):
        return None

    def after_proj_main(self, got):
        pass

    def weights(self, full):
        return full

    def with_dn_fwd(self):
        return None

    def after_dn_fwd(self, got):
        pass

    def with_ffn_in_dw(self):
        return None

    def after_ffn_in_dw(self, got):
        pass

    def after_branch_grads(self, g):
        pass

    def with_dn_bwd(self):
        return None

    def after_dn_bwd(self, got):
        pass

    def with_proj_main_dw(self):
        return None

    def after_proj_main_dw(self, got):
        pass

    def with_ffn_in(self):
        return None

    def after_ffn_in(self, got):
        pass

    def after_all_grads(self, g):
        pass

    def with_gate_sgu_bwd(self):
        return None

    def after_gate_sgu_bwd(self, got):
        pass

    def with_proj_gates_dx(self):
        return None

    def after_proj_gates_dx(self, got):
        pass

    def with_proj_main_dx(self):
        return None

    def after_proj_main_dx(self, got):
        pass


def _carry(carried, after, call, *args, **kw):
    if carried is None:
        return call(*args, **kw)
    out, got = call(*args, carried=carried, **kw)
    after(got)
    return out


def _in_proj_weights(w_in, d):
    heads, q4 = d // DN_DK, 4 * d
    wba = jnp.zeros((d, 2 * LANES), w_in.dtype)
    wba = wba.at[:, :heads].set(w_in[:, q4:q4 + heads])
    wba = wba.at[:, LANES:LANES + heads].set(w_in[:, q4 + heads:q4 + 2 * heads])
    return jnp.concatenate([w_in[:, :q4], w_in[:, q4 + 2 * heads:]], axis=1), wba


def _layer_fwd(x, xb, full, cl, d, tb, comm):
    wm, wba = _in_proj_weights(full["w_in"], d)
    projm = _carry(comm.with_proj_main(), comm.after_proj_main, _matmul, xb, wm, NN, "proj_main", tn=MM_WIDE)
    full = comm.weights(full)
    wl = dict(wm=wm, wba=wba, conv=full["conv_w"], wpa=full["w_pa"], wpb=full["w_pb"], wo=full["w_o"],
              wgu=jnp.concatenate([full["w_ffn_gate"], full["w_ffn_up"]], axis=1), wd=full["w_ffn_down"])
    ba = _matmul(xb, wba, NN, "proj_gates")
    qkv = _conv_fwd(projm, wl["conv"], d, _tile(x.shape[0], 2 * tb, SUBLANES))
    (o, states, ycors), got = _dn_fwd(qkv, ba, cl["alog"], cl["dtb"], d, comm.with_dn_fwd())
    comm.after_dn_fwd(got)
    ya, yb = _gate_sgu_fwd(o, projm, cl["onw"], cl["lng"], cl["lnb"], cl["ws"], cl["bst"], d)
    pa, pb, m, h1, x1, x1b = _mix_fwd(ya, yb, projm, x, wl["wpa"], wl["wpb"], wl["wo"], cl["g1"], cl["b1"], d, tb)
    gu = _carry(comm.with_ffn_in(), comm.after_ffn_in, _matmul, x1b, wl["wgu"], NN, "ffn_in", out_dtype=ACT)
    act, h2, x2, x2b = _ffn_tail_fwd(gu, wl["wd"], x1, cl["g2"], cl["b2"], tb)
    saved = dict(xb=xb, projm=projm, ba=ba, qkv=qkv, o=o, states=states, ycors=ycors, ya=ya, yb=yb,
                 pa=pa, pb=pb, m=m, h1=h1, x1b=x1b, gu=gu, act=act, h2=h2)
    return x2, x2b, saved, wl


def _layer_bwd(sv, wl, cl, d, tb, comm, ln2_bwd, next_ln=None):
    g = {}
    dh2, dh2b, dg2, db2 = ln2_bwd
    g["ln2_g"], g["ln2_b"] = dg2.sum(0), db2.sum(0)
    g["wd"] = _matmul(sv["act"], dh2b, TN, "ffn_out_dw")
    dgu = _ffn_tail_bwd(dh2b, wl["wd"], sv["gu"], tb)
    g["wgu"] = _carry(comm.with_ffn_in_dw(), comm.after_ffn_in_dw, _matmul, sv["x1b"], dgu, TN, "ffn_in_dw")
    dh1, dh1b, dg1, db1 = _ffn_head_bwd(dgu, wl["wgu"], dh2, sv["h1"], cl["g1"], tb)
    g["ln1_g"], g["ln1_b"] = dg1.sum(0), db1.sum(0)
    g["wo"] = _matmul(sv["m"], dh1b, TN, "wo_dw")
    dpa, dpb, dya, dyb, dprojm = _mix_bwd(dh1b, sv["pa"], sv["pb"], sv["projm"], wl["wpa"], wl["wpb"], wl["wo"], d, tb)
    g["wpa"] = _matmul(sv["ya"], dpa, TN, "wpa_dw")
    g["wpb"] = _matmul(sv["yb"], dpb, TN, "wpb_dw")
    comm.after_branch_grads(g)
    (do, dprojm, donw, dlng, dlnb, dws, dbst), got = _gate_sgu_bwd(
        dya, dyb, sv["o"], sv["projm"], cl["onw"], cl["lng"], cl["lnb"], cl["ws"], cl["bst"], dprojm, d,
        comm.with_gate_sgu_bwd())
    comm.after_gate_sgu_bwd(got)
    heads, groups = d // DN_DK, d // SGU_GROUP_DIM
    g["o_norm_w"], g["sgu_ln_g"], g["sgu_ln_b"] = donw.sum(0), dlng.sum(0), dlnb.sum(0)
    g["w_s"], g["b_s"] = dws, dbst[:, :groups].T
    (dqkv, dba, dal, ddt), got = _dn_bwd(sv["qkv"], sv["ba"], cl["alog"], cl["dtb"], do, sv["states"],
                                         sv["ycors"], d, comm.with_dn_bwd())
    comm.after_dn_bwd(got)
    g["a_log"], g["dt_bias"] = dal.sum(0)[:heads], ddt.sum(0)[:heads]
    tbc = _tile(sv["xb"].shape[0], 2 * tb, SUBLANES)
    dy, dcw = _conv_bwd_dy(sv["projm"], wl["conv"], dqkv, d, tbc)
    g["conv_w"] = dcw.sum(1)
    dprojm = _conv_bwd_dx(dy, wl["conv"], dprojm, d, tbc)
    g["wba"] = _matmul(sv["xb"], dba, TN, "proj_gates_dw")
    g["wm"] = _carry(comm.with_proj_main_dw(), comm.after_proj_main_dw, _matmul, sv["xb"], dprojm, TN,
                     "proj_main_dw", tn=MM_WIDE)
    comm.after_all_grads(g)
    dx = _carry(comm.with_proj_gates_dx(), comm.after_proj_gates_dx, _matmul, dba, wl["wba"], NT, "proj_gates_dx",
                add=dh1, coef=ALPHA)
    if next_ln is not None:
        return _matmul(dprojm, wl["wm"], NT, "proj_main_dx", add=dx, tm=MM_TILE // 3, tk=MM_WIDE, ln=next_ln), g
    dx = _carry(comm.with_proj_main_dx(), comm.after_proj_main_dx, _matmul, dprojm, wl["wm"], NT, "proj_main_dx",
                add=dx, tk=MM_WIDE)
    return dx, g


_BRANCH = ("w_pa", "w_pb", "w_o", "w_ffn_gate", "w_ffn_up", "w_ffn_down")


def _grad_shards(g, d, keys):
    heads, q4 = d // DN_DK, 4 * d
    rows = lambda a: a.reshape(N_CHIPS, -1, a.shape[1])
    out = {}
    if "w_in" in keys:
        gm, gba, wsh = g["wm"], g["wba"], 2 * d + heads // 2
        out["w_in"] = jnp.stack([gm[:, :wsh],
                                 jnp.concatenate([gm[:, wsh:q4], gba[:, :heads]], axis=1),
                                 jnp.concatenate([gba[:, LANES:LANES + heads], gm[:, q4:q4 + wsh - heads]], axis=1),
                                 gm[:, q4 + wsh - heads:]])
    if "w_pa" in keys:
        ggu = g["wgu"]
        f = ggu.shape[1] // 2
        fs = f // N_CHIPS
        out.update({
            "w_pa": rows(g["wpa"]), "w_pb": rows(g["wpb"]), "w_o": rows(g["wo"]), "w_ffn_down": rows(g["wd"]),
            "w_ffn_gate": jnp.stack([ggu[:, s * fs:(s + 1) * fs] for s in range(N_CHIPS)]),
            "w_ffn_up": jnp.stack([ggu[:, f + s * fs:f + (s + 1) * fs] for s in range(N_CHIPS)])})
    return out


def _local_step(x, target, full0, full1_of, small_w, comm0=None):
    t, d = x.shape
    tb = _tile(t, 256, SUBLANES)
    comm0 = comm0 or _NoComm()
    consts = [_layer_consts(small_w, l, d) for l in range(DEPTH)]
    x1, x1b, sv0, w0 = _layer_fwd(x, x.astype(ACT), full0, consts[0], d, tb, comm0)
    x2, _, sv1, w1 = _layer_fwd(x1, x1b, full1_of(), consts[1], d, tb, _NoComm())
    ln2_bwd, loss_parts = _loss_ln_bwd(x2, target, sv1["h2"], consts[1]["g2"], tb)
    ln2_bwd, g1 = _layer_bwd(sv1, w1, consts[1], d, tb, _NoComm(), ln2_bwd, next_ln=(sv0["h2"], consts[0]["g2"]))
    comm0.layer1_grads = g1
    grad_x, g0 = _layer_bwd(sv0, w0, consts[0], d, tb, comm0, ln2_bwd)
    return loss_parts, grad_x, [g0, g1]


def kernel(x, w_in, conv_w, a_log, dt_bias, o_norm_w, sgu_ln_g, sgu_ln_b, w_s, b_s, w_pa, w_pb, w_o, ln1_g, ln1_b, w_ffn_gate, w_ffn_up, w_ffn_down, ln2_g, ln2_b, loss_target, m_w_in, m_conv_w, m_a_log, m_dt_bias, m_o_norm_w, m_sgu_ln_g, m_sgu_ln_b, m_w_s, m_b_s, m_w_pa, m_w_pb, m_w_o, m_ln1_g, m_ln1_b, m_w_ffn_gate, m_w_ffn_up, m_w_ffn_down, m_ln2_g, m_ln2_b, v_w_in, v_conv_w, v_a_log, v_dt_bias, v_o_norm_w, v_sgu_ln_g, v_sgu_ln_b, v_w_s, v_b_s, v_w_pa, v_w_pb, v_w_o, v_ln1_g, v_ln1_b, v_w_ffn_gate, v_w_ffn_up, v_w_ffn_down, v_ln2_g, v_ln2_b):
    names = ("w_in", "conv_w", "a_log", "dt_bias", "o_norm_w", "sgu_ln_g", "sgu_ln_b", "w_s", "b_s", "w_pa",
             "w_pb", "w_o", "ln1_g", "ln1_b", "w_ffn_gate", "w_ffn_up", "w_ffn_down", "ln2_g", "ln2_b")
    w = dict(zip(names, (w_in, conv_w, a_log, dt_bias, o_norm_w, sgu_ln_g, sgu_ln_b, w_s, b_s, w_pa, w_pb, w_o,
                         ln1_g, ln1_b, w_ffn_gate, w_ffn_up, w_ffn_down, ln2_g, ln2_b)))
    mom = dict(zip(names, (m_w_in, m_conv_w, m_a_log, m_dt_bias, m_o_norm_w, m_sgu_ln_g, m_sgu_ln_b, m_w_s, m_b_s,
                           m_w_pa, m_w_pb, m_w_o, m_ln1_g, m_ln1_b, m_w_ffn_gate, m_w_ffn_up, m_w_ffn_down,
                           m_ln2_g, m_ln2_b)))
    var = dict(zip(names, (v_w_in, v_conv_w, v_a_log, v_dt_bias, v_o_norm_w, v_sgu_ln_g, v_sgu_ln_b, v_w_s, v_b_s,
                           v_w_pa, v_w_pb, v_w_o, v_ln1_g, v_ln1_b, v_w_ffn_gate, v_w_ffn_up, v_w_ffn_down,
                           v_ln2_g, v_ln2_b)))
    chip = 2 * lax.axis_index("x") + lax.axis_index("y")
    place = jnp.stack([lax.axis_index("c"), chip]).astype(jnp.int32)

    big = [k for k, _ in _BIG]
    axis_of = dict(_BIG)
    local = {k: w[k].astype(BF16) for k in big}
    local["conv_w"] = conv_w

    def gather_plan(l, keys):
        return _all_gather_plan([local[k][l] for k in keys])

    def full_of(l, keys, gathered):
        return {k: _unshard(gt, local[k][l], chip, axis_of.get(k, 2)) for k, gt in zip(keys, gathered)}

    def pair_sums(grads_l, keys, lands):
        return [_pair_sum(place, grads_l[k], land) for k, land in zip(keys, lands)]

    def chip_sums(pairs, lands):
        return [_chip_sum(place, p[1], land, "grad_chip_sum") for p, land in zip(pairs, lands)]

    class Layer0Comm(_NoComm):
        def with_proj_main(self):
            return gather_plan(0, _BRANCH)

        def after_proj_main(self, got):
            self.rest = full_of(0, _BRANCH, got)

        def weights(self, full):
            return {**full, **self.rest}

        def with_dn_fwd(self):
            return gather_plan(1, mixer)

        def after_dn_fwd(self, got):
            self.full1 = full_of(1, mixer, got)

        def with_ffn_in(self):
            return gather_plan(1, ffn)

        def after_ffn_in(self, got):
            self.full1.update(full_of(1, ffn, got))

        def with_ffn_in_dw(self):
            self.g1 = _grad_shards(self.layer1_grads, x.shape[-1], big)
            return _sibling_exchange_plan([self.g1[k] for k in big])

        def after_ffn_in_dw(self, got):
            self.pairs1 = pair_sums(self.g1, big, got)

        def with_dn_bwd(self):
            return _chip_exchange_plan([p[0] for p in self.pairs1])

        def after_dn_bwd(self, got):
            self.red1 = chip_sums(self.pairs1, got)

        def after_branch_grads(self, g0):
            self.shards0 = _grad_shards(g0, x.shape[-1], _BRANCH)

        def with_gate_sgu_bwd(self):
            return _sibling_exchange_plan([self.shards0[k] for k in _BRANCH])

        def after_gate_sgu_bwd(self, got):
            self.pairs0 = pair_sums(self.shards0, _BRANCH, got)

        def with_proj_main_dw(self):
            return _chip_exchange_plan([p[0] for p in self.pairs0])

        def after_proj_main_dw(self, got):
            self.red0 = chip_sums(self.pairs0, got)

        def after_all_grads(self, g0):
            self.g_in = _grad_shards(g0, x.shape[-1], ["w_in"])["w_in"]
            self.small_g = {k: jnp.stack([g0[k], self.layer1_grads[k]]) for k in _SMALL}
            self.small = _pack_small([self.small_g[k] for k in _SMALL])

        def with_proj_gates_dx(self):
            return _sibling_exchange_plan([self.g_in], self.small)

        def after_proj_gates_dx(self, got):
            self.pair_in = _pair_sum(place, self.g_in, got[0])
            self.small_chip = _add2(self.small, got[1])

        def with_proj_main_dx(self):
            return _join_plans(_chip_exchange_plan([self.pair_in[0]], self.small_chip),
                               _sibling_merge_plan(self.red0 + self.red1))

        def after_proj_main_dx(self, got):
            self.red_in = _chip_sum(place, self.pair_in[1], got[0], "grad_chip_sum")
            self.small_total = _chip_sum(place, self.small_chip, got[1], "grad_small_chip_sum")
            self.others = got[2:]

    comm = Layer0Comm()
    first, mixer, ffn = ["w_in", "conv_w"], ["w_in", "conv_w", "w_pa", "w_pb", "w_o"], list(_BRANCH[3:])
    full0 = full_of(0, first, _run_comm("all_gather_weights", gather_plan(0, first)))
    small_w = {k: w[k] for k in _SMALL if k != "conv_w"}
    loss_parts, grad_x, g = _local_step(x[0], loss_target[0], full0, lambda: comm.full1, small_w, comm)

    reduced = [comm.red_in] + comm.red0 + comm.red1
    others = list(_run_comm("grad_sibling_merge", _sibling_merge_plan([comm.red_in]))) + list(comm.others)
    halves = [_merge_halves(place, mine, other) for mine, other in zip(reduced, others)]
    grads = {k: jnp.stack([halves[i], halves[len(big) + i]]) for i, k in enumerate(big)}
    grads.update(zip(_SMALL, _unpack_small(comm.small_total, [comm.small_g[k] for k in _SMALL])))
    grads["conv_w"] = lax.dynamic_index_in_dim(_to_shards(grads["conv_w"], 2), chip, 1, keepdims=False)

    delta, new_m, new_v = {}, {}, {}
    for k in [k for k, _ in _BIG] + ["conv_w"]:
        delta[k], new_m[k], new_v[k] = _adamw(w[k], grads[k], mom[k], var[k])
    rep = [k for k in _SMALL if k != "conv_w"]
    pack = lambda dct: _pack_small([dct[k] for k in rep])
    packed = _adamw(pack(w), pack(grads), pack(mom), pack(var))
    for dst, src in zip((delta, new_m, new_v), packed):
        dst.update(zip(rep, _unpack_small(src, [w[k] for k in rep])))

    loss = 0.5 * lax.psum(jnp.sum(loss_parts), ("x", "y", "c")) / x.shape[-1]
    return (loss, grad_x[None], *[grads[k] for k in names], *[delta[k] for k in names],
            *[new_m[k] for k in names], *[new_v[k] for k in names])
```

```python
import math

import jax
import jax.numpy as jnp
from jax import lax
from jax.experimental import pallas as pl
from jax.experimental.pallas import tpu as pltpu

F32 = jnp.float32
BF16 = jnp.bfloat16
MXU_DTYPE = jnp.bfloat16
ACT = jnp.bfloat16
HIGHEST = lax.Precision.HIGHEST

DEPTH = 2
CHUNK = 64
DN_GROUP = 2
DN_GROUP_FWD = 4
SGU_BLOCK = 128
SGU_WINDOWS = 4
CONV_K = 4
DN_DK = 128
SGU_GROUP_DIM = 128
LN_EPS = 1e-5
RMS_EPS = 1e-6
ALPHA = (2 * DEPTH) ** 0.25
ADAM_LR, ADAM_B1, ADAM_B2, ADAM_EPS, ADAM_WD, ADAM_STEP = 0.001, 0.9, 0.999, 1e-08, 0.01, 10

LANES = 128
SUBLANES = 8
VMEM_LIMIT = 52 * 2 ** 20
N_CHIPS = 4

NN = ((1,), (0,))
NT = ((1,), (1,))
TN = ((0,), (0,))
MESH = pl.DeviceIdType.MESH
ANY = pl.BlockSpec(memory_space=pl.ANY)


def _dot(a, b, dims=NN, prec=None):
    if prec is None:
        a = a.astype(MXU_DTYPE)
        b = b.astype(MXU_DTYPE)
    return lax.dot_general(a, b, (dims, ((), ())), preferred_element_type=F32, precision=prec)


def _cparams(sem=None):
    return pltpu.CompilerParams(dimension_semantics=sem, vmem_limit_bytes=VMEM_LIMIT)


def _tile(dim, pref, unit=LANES):
    t = (min(pref, dim) // unit) * unit
    while t >= unit:
        if dim % t == 0:
            return t
        t -= unit
    return dim


def _fold8(x):
    r, n = x.shape
    return x.reshape(r // SUBLANES, SUBLANES, n).sum(axis=0)


def _sigmoid(x):
    return 1.0 / (1.0 + jnp.exp(-x))


def _gelu(x):
    return 0.5 * x * (1.0 + lax.erf(x * (2.0 ** -0.5)))


def _gelu_grad(x):
    return 0.5 * (1.0 + lax.erf(x * (2.0 ** -0.5))) + x * jnp.exp(-0.5 * x * x) * (2.0 * math.pi) ** -0.5


def _ln_hat(h):
    mu = jnp.mean(h, axis=-1, keepdims=True)
    xc = h - mu
    var = jnp.mean(xc * xc, axis=-1, keepdims=True)
    r = lax.rsqrt(var + LN_EPS)
    return xc * r, r


def _ln_bwd(dxhat, xhat, r):
    return r * (dxhat - jnp.mean(dxhat, axis=-1, keepdims=True)
                - xhat * jnp.mean(dxhat * xhat, axis=-1, keepdims=True))


MM_TILE = 1536
MM_WIDE = 2048


def _matmul(a, b, dims, name, out_dtype=F32, add=None, coef=1.0, tm=MM_TILE, tn=MM_TILE, tk=MM_TILE, carried=None,
            ln=None):
    if dims == NN:
        (m, k), n = a.shape, b.shape[1]
    elif dims == NT:
        (m, k), n = a.shape, b.shape[0]
    else:
        (k, m), n = a.shape, b.shape[1]
    tm, tn, tk = _tile(m, tm), _tile(n, tn), _tile(k, tk)
    nk = k // tk
    a_spec = pl.BlockSpec((tk, tm), lambda j, i, q: (q, i)) if dims == TN else pl.BlockSpec((tm, tk), lambda j, i, q: (i, q))
    b_spec = pl.BlockSpec((tn, tk), lambda j, i, q: (j, q)) if dims == NT else pl.BlockSpec((tk, tn), lambda j, i, q: (q, j))
    o_spec = pl.BlockSpec((tm, tn), lambda j, i, q: (i, j))
    has_add = add is not None
    if ln is not None:
        assert n == tn and has_add and carried is None
        return _matmul_ln_bwd(a, b, dims, name, add, coef, ln, a_spec, b_spec, o_spec, (m, n, tm, tn, nk))

    def body(*refs):
        a_ref, b_ref = refs[0], refs[1]
        add_ref = refs[2] if has_add else None
        o_ref, acc_ref = refs[2 + has_add], refs[3 + has_add]
        q = pl.program_id(2)
        part = _dot(a_ref[...], b_ref[...], dims)

        def finish(r):
            if has_add:
                r = r + coef * add_ref[...]
            o_ref[...] = r.astype(out_dtype)

        if nk == 1:
            finish(part)
        else:
            @pl.when(q == 0)
            def _():
                acc_ref[...] = part

            @pl.when(q > 0)
            def _():
                acc_ref[...] += part

            @pl.when(q == nk - 1)
            def _():
                finish(acc_ref[...])

    ins = [a, b] + ([add] if has_add else [])
    in_specs = [a_spec, b_spec] + ([o_spec] if has_add else [])
    grid = (n // tn, m // tm, nk)
    acc = pltpu.VMEM((tm, tn) if nk > 1 else (SUBLANES, LANES), F32)
    out = jax.ShapeDtypeStruct((m, n), out_dtype)
    if carried is None:
        return pl.pallas_call(
            body, name=name, grid=grid, in_specs=in_specs, out_specs=o_spec, out_shape=out, scratch_shapes=[acc],
            compiler_params=_cparams(("parallel", "parallel", "arbitrary")),
        )(*ins)
    res = pl.pallas_call(
        _carrying(body, len(ins), 1, 1, carried, grid), name=name + "_carrying", grid=grid,
        in_specs=in_specs + [ANY] * len(carried.inputs), out_specs=[o_spec] + [ANY] * len(carried.out_shapes),
        out_shape=[out] + carried.out_shapes, scratch_shapes=[acc] + carried.scratch(),
        compiler_params=_cparams(("arbitrary", "arbitrary", "arbitrary")),
    )(*ins, *carried.inputs)
    return res[0], res[1:]


def _matmul_ln_bwd(a, b, dims, name, add, coef, ln, a_spec, b_spec, o_spec, sizes):
    m, n, tm, tn, nk = sizes
    hres, g = ln
    row = pl.BlockSpec((1, n), lambda j, i, q: (0, 0))
    sums = pl.BlockSpec((SUBLANES, n), lambda j, i, q: (0, 0))

    def body(a_ref, b_ref, add_ref, h_ref, g_ref, dh_ref, dhb_ref, dg_ref, db_ref, acc_ref):
        i, q = pl.program_id(1), pl.program_id(2)
        part = _dot(a_ref[...], b_ref[...], dims)

        @pl.when(jnp.logical_and(i == 0, q == 0))
        def _():
            dg_ref[...] = jnp.zeros_like(dg_ref)
            db_ref[...] = jnp.zeros_like(db_ref)

        @pl.when(q == 0)
        def _():
            acc_ref[...] = part

        @pl.when(q > 0)
        def _():
            acc_ref[...] += part

        @pl.when(q == nk - 1)
        def _():
            dy_v = acc_ref[...] + coef * add_ref[...]
            xhat, r = _ln_hat(h_ref[...])
            dh = _ln_bwd(dy_v * g_ref[...], xhat, r)
            dh_ref[...] = dh
            dhb_ref[...] = dh.astype(dhb_ref.dtype)
            dg_ref[...] += _fold8(dy_v * xhat)
            db_ref[...] += _fold8(dy_v)

    return pl.pallas_call(
        body, name=name + "_ln_bwd", grid=(1, m // tm, nk),
        in_specs=[a_spec, b_spec, o_spec, o_spec, row], out_specs=[o_spec, o_spec, sums, sums],
        out_shape=[jax.ShapeDtypeStruct((m, n), F32), jax.ShapeDtypeStruct((m, n), ACT),
                   jax.ShapeDtypeStruct((SUBLANES, n), F32), jax.ShapeDtypeStruct((SUBLANES, n), F32)],
        scratch_shapes=[pltpu.VMEM((tm, tn), F32)],
        compiler_params=_cparams(("arbitrary", "arbitrary", "arbitrary")),
    )(a, b, add, hres, g)


def _conv_taps(cur_ref, halo_ref, first):
    x = cur_ref[...]
    tb = x.shape[0]
    halo = jnp.where(first, 0.0, halo_ref[...])
    xc = jnp.concatenate([halo, x], axis=0)
    return [x] + [pltpu.roll(xc, s, 0)[SUBLANES:SUBLANES + tb] for s in range(1, CONV_K)]


def _conv_fwd(projm, conv_w, d, tb):
    t = projm.shape[0]
    heads = d // DN_DK
    hb = tb // SUBLANES

    def body(cur_ref, halo_ref, w_ref, o_ref):
        i, j = pl.program_id(0), pl.program_id(1)
        taps = _conv_taps(cur_ref, halo_ref, i == 0)
        y = taps[0] * w_ref[CONV_K - 1:CONV_K, :]
        for s in range(1, CONV_K):
            y = y + taps[s] * w_ref[CONV_K - 1 - s:CONV_K - s, :]
        act = y * _sigmoid(y)
        scale = jnp.where(j == 0, DN_DK ** -0.5, 1.0)
        for h in range(heads):
            seg = act[:, h * DN_DK:(h + 1) * DN_DK]
            r = lax.rsqrt(jnp.sum(seg * seg, axis=1, keepdims=True) + RMS_EPS) * scale
            o_ref[:, h * DN_DK:(h + 1) * DN_DK] = seg * jnp.where(j < 2, r, 1.0)

    blk = pl.BlockSpec((tb, d), lambda i, j: (i, j))
    return pl.pallas_call(
        body, name="conv_fwd", grid=(t // tb, 3),
        in_specs=[blk,
                  pl.BlockSpec((SUBLANES, d), lambda i, j: (jnp.maximum(i * hb - 1, 0), j)),
                  pl.BlockSpec((CONV_K, d), lambda i, j: (0, j))],
        out_specs=blk,
        out_shape=jax.ShapeDtypeStruct((t, 3 * d), F32),
        compiler_params=_cparams(("parallel", "parallel")),
    )(projm, projm, conv_w)


def _conv_bwd_dy(projm, conv_w, dqkv, d, tb):
    t = projm.shape[0]
    heads = d // DN_DK
    hb = tb // SUBLANES

    def body(cur_ref, halo_ref, w_ref, dout_ref, dy_ref, dw_ref):
        j, i = pl.program_id(0), pl.program_id(1)
        taps = _conv_taps(cur_ref, halo_ref, i == 0)
        y = taps[0] * w_ref[CONV_K - 1:CONV_K, :]
        for s in range(1, CONV_K):
            y = y + taps[s] * w_ref[CONV_K - 1 - s:CONV_K - s, :]
        sg = _sigmoid(y)
        act = y * sg
        dact = sg * (1.0 + y * (1.0 - sg))
        scale = jnp.where(j == 0, DN_DK ** -0.5, 1.0)
        for h in range(heads):
            cols = slice(h * DN_DK, (h + 1) * DN_DK)
            seg = act[:, cols]
            r = lax.rsqrt(jnp.sum(seg * seg, axis=1, keepdims=True) + RMS_EPS)
            nrm = seg * r
            dout = dout_ref[:, cols]
            ds = jnp.where(j < 2, (r * scale) * (dout - nrm * jnp.sum(dout * nrm, axis=1, keepdims=True)), dout)
            dy_ref[:, cols] = ds * dact[:, cols]
        dy = dy_ref[...]

        @pl.when(i == 0)
        def _():
            dw_ref[...] = jnp.zeros_like(dw_ref)

        for s in range(CONV_K):
            dw_ref[CONV_K - 1 - s] += _fold8(dy * taps[s])

    return pl.pallas_call(
        body, name="conv_bwd_dy", grid=(3, t // tb),
        in_specs=[pl.BlockSpec((tb, d), lambda j, i: (i, j)),
                  pl.BlockSpec((SUBLANES, d), lambda j, i: (jnp.maximum(i * hb - 1, 0), j)),
                  pl.BlockSpec((CONV_K, d), lambda j, i: (0, j)),
                  pl.BlockSpec((tb, d), lambda j, i: (i, j))],
        out_specs=[pl.BlockSpec((tb, d), lambda j, i: (i, j)),
                   pl.BlockSpec((CONV_K, SUBLANES, d), lambda j, i: (0, 0, j))],
        out_shape=[jax.ShapeDtypeStruct((t, 3 * d), F32),
                   jax.ShapeDtypeStruct((CONV_K, SUBLANES, 3 * d), F32)],
        compiler_params=_cparams(("parallel", "arbitrary")),
    )(projm, projm, conv_w, dqkv)


def _conv_bwd_dx(dy, conv_w, dprojm, d, tb):
    t = dy.shape[0]
    hb = tb // SUBLANES
    last = t // tb - 1

    def body(cur_ref, halo_ref, w_ref, alias_ref, o_ref):
        i = pl.program_id(0)
        cur = cur_ref[...]
        halo = jnp.where(i == last, 0.0, halo_ref[...])
        dc = jnp.concatenate([cur, halo], axis=0)
        acc = cur * w_ref[CONV_K - 1:CONV_K, :]
        for s in range(1, CONV_K):
            acc = acc + pltpu.roll(dc, tb + SUBLANES - s, 0)[:tb] * w_ref[CONV_K - 1 - s:CONV_K - s, :]
        o_ref[...] = acc.astype(o_ref.dtype)

    return pl.pallas_call(
        body, name="conv_bwd_dx", grid=(t // tb, 3),
        in_specs=[pl.BlockSpec((tb, d), lambda i, j: (i, j)),
                  pl.BlockSpec((SUBLANES, d), lambda i, j: (jnp.minimum((i + 1) * hb, t // SUBLANES - 1), j)),
                  pl.BlockSpec((CONV_K, d), lambda i, j: (0, j)),
                  ANY],
        out_specs=pl.BlockSpec((tb, d), lambda i, j: (i, j)),
        out_shape=jax.ShapeDtypeStruct(dprojm.shape, dprojm.dtype),
        input_output_aliases={3: 0},
        compiler_params=_cparams(("parallel", "parallel")),
    )(dy, dy, conv_w, dprojm)


def _beta_g(ba, alog, dtb):
    beta = _sigmoid(ba[:, :LANES])
    xa = ba[:, LANES:] + dtb
    softplus = jnp.maximum(xa, 0.0) + jnp.log(1.0 + jnp.exp(-jnp.abs(xa)))
    ea = jnp.exp(alog)
    return beta, -ea * softplus, ea, _sigmoid(xa)


def _inv_corrections(mats):
    ys = [-a for a in mats]
    ps = [_dot(a, a) for a in mats]
    steps = int(math.log2(CHUNK)) - 1
    for it in range(steps):
        ys = [y + p + _dot(y, p) for y, p in zip(ys, ps)]
        if it < steps - 1:
            ps = [_dot(p, p) for p in ps]
    return ys


def _chunk_masks():
    row = lax.broadcasted_iota(jnp.int32, (CHUNK, CHUNK), 0)
    col = lax.broadcasted_iota(jnp.int32, (CHUNK, CHUNK), 1)
    return row >= col, row > col, row <= col


def _col_of(mat, lane_idx, h):
    return jnp.sum(jnp.where(lane_idx == h, mat, 0.0), axis=1, keepdims=True)


def _row_of(mat, sub_idx, h):
    return jnp.sum(jnp.where(sub_idx == h, mat, 0.0), axis=0, keepdims=True)


def _phases(fns):
    return fns if len(fns) == 3 else (fns[0], lambda: None, fns[1])


def _carrying(compute, n_in, n_out, n_scratch, carried, grid):
    if carried is None:
        return compute
    ci, co = len(carried.inputs), len(carried.out_shapes)

    def body(*refs):
        ins, c_in = refs[:n_in], refs[n_in:n_in + ci]
        outs, c_out = refs[n_in + ci:n_in + ci + n_out], refs[n_in + ci + n_out:n_in + ci + n_out + co]
        scratch = refs[n_in + ci + n_out + co:]
        start, middle, finish = _phases(carried.copies(c_in, c_out, scratch[n_scratch], scratch[n_scratch + 1]))
        step, total = 0, 1
        for axis, steps in enumerate(grid):
            step = step * steps + pl.program_id(axis)
            total *= steps

        @pl.when(step == 0)
        def _():
            start()

        compute(*ins, *outs, *scratch[:n_scratch])

        @pl.when(step == (3 * total) // 4)
        def _():
            middle()

        @pl.when(step == total - 1)
        def _():
            finish()

    return body


def _dn_fwd(qkv, ba, alog, dtb, d, carried=None):
    t = qkv.shape[0]
    heads = d // DN_DK
    n_chunks = t // CHUNK
    grp = DN_GROUP_FWD if n_chunks % DN_GROUP_FWD == 0 else 1
    span = grp * CHUNK
    extra = carried or _Carried([], [], 0, None)

    def compute(qkv_ref, ba_ref, al_ref, dt_ref, o_ref, s_ref, y_ref, state):
        @pl.when(pl.program_id(0) == 0)
        def _():
            state[...] = jnp.zeros_like(state)

        tril, strict, _ = _chunk_masks()
        beta, g, _, _ = _beta_g(ba_ref[...], al_ref[...], dt_ref[...])
        lane = lax.broadcasted_iota(jnp.int32, (CHUNK, LANES), 1)
        sub = lax.broadcasted_iota(jnp.int32, (LANES, CHUNK), 0)
        rowc = lax.broadcasted_iota(jnp.int32, (CHUNK, 1), 0)
        hs = range(heads)
        units = [(c, h) for c in range(grp) for h in hs]
        un = range(len(units))
        rows = lambda c: slice(c * CHUNK, (c + 1) * CHUNK)
        gc = [_dot(jnp.where(tril, 1.0, 0.0), g[rows(c)], NN, HIGHEST) for c in range(grp)]
        gct = [m.T for m in gc]
        q = [qkv_ref[rows(c), h * DN_DK:(h + 1) * DN_DK] for c, h in units]
        k = [qkv_ref[rows(c), d + h * DN_DK:d + (h + 1) * DN_DK] for c, h in units]
        v = [qkv_ref[rows(c), 2 * d + h * DN_DK:2 * d + (h + 1) * DN_DK] for c, h in units]
        gch = [_col_of(gc[c], lane, h) for c, h in units]
        bh = [_col_of(beta[rows(c)], lane, h) for c, h in units]
        dec = [jnp.where(tril, jnp.exp(gch[n] - _row_of(gct[c], sub, h)), 0.0) for n, (c, h) in enumerate(units)]
        egc = [jnp.exp(gch[n]) for n in un]
        gl = [jnp.sum(jnp.where(rowc == CHUNK - 1, gch[n], 0.0), axis=0, keepdims=True) for n in un]
        kb = [k[n] * bh[n] for n in un]
        a = [jnp.where(strict, _dot(kb[n], k[n], NT) * dec[n], 0.0) for n in un]
        p = [_dot(q[n], k[n], NT) * dec[n] for n in un]
        ycor = _inv_corrections(a)
        rhs = [jnp.concatenate([v[n] * bh[n], kb[n] * egc[n]], axis=1) for n in un]
        sol = [rhs[n] + _dot(ycor[n], rhs[n]) for n in un]
        qg = [q[n] * egc[n] for n in un]
        kd = [k[n] * jnp.exp(gl[n] - gch[n]) for n in un]
        egl = [jnp.exp(gl[n]) for n in un]
        s_cur, s_in, o = [state[h] for h in hs], [], []
        for c in range(grp):
            ns = [c * heads + h for h in hs]
            vn = [sol[n][:, :DN_DK] - _dot(sol[n][:, DN_DK:], s_cur[h]) for h, n in enumerate(ns)]
            o += [_dot(qg[n], s_cur[h]) + _dot(p[n], vn[h]) for h, n in enumerate(ns)]
            s_in += s_cur
            s_cur = [s_cur[h] * egl[n] + _dot(kd[n], vn[h], TN) for h, n in enumerate(ns)]
        for n, (c, h) in enumerate(units):
            o_ref[rows(c), h * DN_DK:(h + 1) * DN_DK] = o[n]
            s_ref[c, h] = s_in[n]
            y_ref[h, rows(c), :] = ycor[n]
        for h in hs:
            state[h] = s_cur[h]

    res = pl.pallas_call(
        _carrying(compute, 4, 3, 1, carried, (n_chunks // grp,)),
        name="dn_fwd_carrying" if carried else "dn_fwd", grid=(n_chunks // grp,),
        in_specs=[pl.BlockSpec((span, 3 * d), lambda i: (i, 0)),
                  pl.BlockSpec((span, 2 * LANES), lambda i: (i, 0)),
                  pl.BlockSpec((1, LANES), lambda i: (0, 0)),
                  pl.BlockSpec((1, LANES), lambda i: (0, 0))] + [ANY] * len(extra.inputs),
        out_specs=[pl.BlockSpec((span, d), lambda i: (i, 0)),
                   pl.BlockSpec((grp, heads, DN_DK, DN_DK), lambda i: (i, 0, 0, 0)),
                   pl.BlockSpec((heads, span, CHUNK), lambda i: (0, i, 0))] + [ANY] * len(extra.out_shapes),
        out_shape=[jax.ShapeDtypeStruct((t, d), F32),
                   jax.ShapeDtypeStruct((n_chunks, heads, DN_DK, DN_DK), F32),
                   jax.ShapeDtypeStruct((heads, t, CHUNK), F32)] + extra.out_shapes,
        scratch_shapes=[pltpu.VMEM((heads, DN_DK, DN_DK), F32)] + (extra.scratch() if carried else []),
        compiler_params=_cparams(("arbitrary",)),
    )(qkv, ba, alog, dtb, *extra.inputs)
    return res[:3], res[3:]


def _dn_bwd(qkv, ba, alog, dtb, dout, states, ycors, d, carried=None):
    t = qkv.shape[0]
    heads = d // DN_DK
    n_chunks = t // CHUNK
    grp = DN_GROUP if n_chunks % DN_GROUP == 0 else 1
    span = grp * CHUNK
    rev = lambda i: n_chunks // grp - 1 - i
    extra = carried or _Carried([], [], 0, None)

    def compute(qkv_ref, ba_ref, al_ref, dt_ref, do_ref, s_ref, y_ref,
                dqkv_ref, dba_ref, dal_ref, ddt_ref, dstate):
        @pl.when(pl.program_id(0) == 0)
        def _():
            dstate[...] = jnp.zeros_like(dstate)
            dal_ref[...] = jnp.zeros_like(dal_ref)
            ddt_ref[...] = jnp.zeros_like(ddt_ref)

        tril, strict, triu = _chunk_masks()
        beta, g, ea, sig_a = _beta_g(ba_ref[...], al_ref[...], dt_ref[...])
        lane = lax.broadcasted_iota(jnp.int32, (CHUNK, LANES), 1)
        sub = lax.broadcasted_iota(jnp.int32, (LANES, CHUNK), 0)
        rowc = lax.broadcasted_iota(jnp.int32, (CHUNK, 1), 0)
        hs = range(heads)
        units = [(c, h) for c in range(grp) for h in hs]
        un = range(len(units))
        rows = lambda c: slice(c * CHUNK, (c + 1) * CHUNK)
        rsum = lambda x_: jnp.sum(x_, axis=1, keepdims=True)
        gc = [_dot(jnp.where(tril, 1.0, 0.0), g[rows(c)], NN, HIGHEST) for c in range(grp)]
        gct = [m.T for m in gc]
        q = [qkv_ref[rows(c), h * DN_DK:(h + 1) * DN_DK] for c, h in units]
        k = [qkv_ref[rows(c), d + h * DN_DK:d + (h + 1) * DN_DK] for c, h in units]
        v = [qkv_ref[rows(c), 2 * d + h * DN_DK:2 * d + (h + 1) * DN_DK] for c, h in units]
        dout_h = [do_ref[rows(c), h * DN_DK:(h + 1) * DN_DK] for c, h in units]
        s0 = [s_ref[c, h] for c, h in units]
        ycor = [y_ref[h, rows(c), :] for c, h in units]
        gch = [_col_of(gc[c], lane, h) for c, h in units]
        bh = [_col_of(beta[rows(c)], lane, h) for c, h in units]
        dec = [jnp.where(tril, jnp.exp(gch[n] - _row_of(gct[c], sub, h)), 0.0) for n, (c, h) in enumerate(units)]
        egc = [jnp.exp(gch[n]) for n in un]
        gl = [jnp.sum(jnp.where(rowc == CHUNK - 1, gch[n], 0.0), axis=0, keepdims=True) for n in un]
        egl = [jnp.exp(gl[n]) for n in un]
        ekd = [jnp.exp(gl[n] - gch[n]) for n in un]
        kb = [k[n] * bh[n] for n in un]
        kd = [k[n] * ekd[n] for n in un]
        qg = [q[n] * egc[n] for n in un]
        kbg = [kb[n] * egc[n] for n in un]
        a = [jnp.where(strict, _dot(kb[n], k[n], NT) * dec[n], 0.0) for n in un]
        p = [_dot(q[n], k[n], NT) * dec[n] for n in un]
        rhs = [jnp.concatenate([v[n] * bh[n], kbg[n]], axis=1) for n in un]
        sol = [rhs[n] + _dot(ycor[n], rhs[n]) for n in un]
        w = [sol[n][:, DN_DK:] for n in un]
        vn = [sol[n][:, :DN_DK] - _dot(w[n], s0[n]) for n in un]
        dqg = [_dot(dout_h[n], s0[n], NT) for n in un]
        dp = [jnp.where(tril, _dot(dout_h[n], vn[n], NT), 0.0) for n in un]
        pdo = [_dot(p[n], dout_h[n], TN) for n in un]
        qdo = [_dot(qg[n], dout_h[n], TN) for n in un]
        ds_cur = [dstate[h] for h in hs]
        dsn, dvn = [None] * len(units), [None] * len(units)
        for c in reversed(range(grp)):
            for h in hs:
                dsn[c * heads + h] = ds_cur[h]
            for h in hs:
                n = c * heads + h
                dvn[n] = pdo[n] + _dot(kd[n], ds_cur[h])
            ds_cur = [qdo[c * heads + h] + egl[c * heads + h] * ds_cur[h]
                      - _dot(w[c * heads + h], dvn[c * heads + h], TN) for h in hs]
        dkd = [_dot(vn[n], dsn[n], NT) for n in un]
        dw = [-_dot(dvn[n], s0[n], NT) for n in un]
        dgl = [jnp.sum(rsum(dsn[n] * s0[n]), axis=0, keepdims=True) * egl[n] for n in un]
        dsol = [jnp.concatenate([dvn[n], dw[n]], axis=1) for n in un]
        drhs = [dsol[n] + _dot(ycor[n], dsol[n], TN) for n in un]
        dvb = [drhs[n][:, :DN_DK] for n in un]
        dkbg = [drhs[n][:, DN_DK:] for n in un]
        da = [jnp.where(strict, -_dot(drhs[n], sol[n], NT), 0.0) for n in un]
        dma = [da[n] * dec[n] for n in un]
        dmp = [dp[n] * dec[n] for n in un]
        dkb = [_dot(dma[n], k[n]) + dkbg[n] * egc[n] for n in un]
        dq = [_dot(dmp[n], k[n]) + dqg[n] * egc[n] for n in un]
        dk = [_dot(dma[n], kb[n], TN) + _dot(dmp[n], q[n], TN) + dkd[n] * ekd[n] + dkb[n] * bh[n] for n in un]
        e = [da[n] * a[n] + dp[n] * p[n] for n in un]
        colsum = [jnp.sum(e[n], axis=0, keepdims=True) for n in un]
        tkd = [rsum(dkd[n] * kd[n]) for n in un]
        for n, (c, h) in enumerate(units):
            dqkv_ref[rows(c), h * DN_DK:(h + 1) * DN_DK] = dq[n]
            dqkv_ref[rows(c), d + h * DN_DK:d + (h + 1) * DN_DK] = dk[n]
            dqkv_ref[rows(c), 2 * d + h * DN_DK:2 * d + (h + 1) * DN_DK] = dvb[n] * bh[n]
        for h in hs:
            dstate[h] = ds_cur[h]
        valid = lane < heads
        dal_acc = jnp.zeros((SUBLANES, LANES), F32)
        ddt_acc = jnp.zeros((SUBLANES, LANES), F32)
        for c in range(grp):
            dgc_all = jnp.zeros((CHUNK, LANES), F32)
            dbeta_all = jnp.zeros((CHUNK, LANES), F32)
            colsums = jnp.zeros((LANES, CHUNK), F32)
            for h in hs:
                n = c * heads + h
                dgc = rsum(e[n]) + rsum(dqg[n] * qg[n]) - tkd[n] + rsum(dkbg[n] * kbg[n])
                dgc = dgc + jnp.where(rowc == CHUNK - 1, dgl[n] + jnp.sum(tkd[n], axis=0, keepdims=True), 0.0)
                dgc_all = dgc_all + jnp.where(lane == h, dgc, 0.0)
                colsums = colsums + jnp.where(sub == h, colsum[n], 0.0)
                dbeta_all = dbeta_all + jnp.where(lane == h, rsum(dkb[n] * k[n]) + rsum(dvb[n] * v[n]), 0.0)
            dg = _dot(jnp.where(triu, 1.0, 0.0), dgc_all - colsums.T, NN, HIGHEST)
            beta_c = beta[rows(c)]
            dbl = jnp.where(valid, dbeta_all * beta_c * (1.0 - beta_c), 0.0)
            dal = jnp.where(valid, -dg * ea * sig_a[rows(c)], 0.0)
            dba_ref[rows(c), :LANES] = dbl.astype(dba_ref.dtype)
            dba_ref[rows(c), LANES:] = dal.astype(dba_ref.dtype)
            dal_acc = dal_acc + _fold8(jnp.where(valid, dg * g[rows(c)], 0.0))
            ddt_acc = ddt_acc + _fold8(dal)
        dal_ref[...] += dal_acc
        ddt_ref[...] += ddt_acc

    res = pl.pallas_call(
        _carrying(compute, 7, 4, 1, carried, (n_chunks // grp,)),
        name="dn_bwd_carrying" if carried else "dn_bwd", grid=(n_chunks // grp,),
        in_specs=[pl.BlockSpec((span, 3 * d), lambda i: (rev(i), 0)),
                  pl.BlockSpec((span, 2 * LANES), lambda i: (rev(i), 0)),
                  pl.BlockSpec((1, LANES), lambda i: (0, 0)),
                  pl.BlockSpec((1, LANES), lambda i: (0, 0)),
                  pl.BlockSpec((span, d), lambda i: (rev(i), 0)),
                  pl.BlockSpec((grp, heads, DN_DK, DN_DK), lambda i: (rev(i), 0, 0, 0)),
                  pl.BlockSpec((heads, span, CHUNK), lambda i: (0, rev(i), 0))] + [ANY] * len(extra.inputs),
        out_specs=[pl.BlockSpec((span, 3 * d), lambda i: (rev(i), 0)),
                   pl.BlockSpec((span, 2 * LANES), lambda i: (rev(i), 0)),
                   pl.BlockSpec((SUBLANES, LANES), lambda i: (0, 0)),
                   pl.BlockSpec((SUBLANES, LANES), lambda i: (0, 0))] + [ANY] * len(extra.out_shapes),
        out_shape=[jax.ShapeDtypeStruct((t, 3 * d), F32),
                   jax.ShapeDtypeStruct((t, 2 * LANES), ACT),
                   jax.ShapeDtypeStruct((SUBLANES, LANES), F32),
                   jax.ShapeDtypeStruct((SUBLANES, LANES), F32)] + extra.out_shapes,
        scratch_shapes=[pltpu.VMEM((heads, DN_DK, DN_DK), F32)] + (extra.scratch() if carried else []),
        compiler_params=_cparams(("arbitrary",)),
    )(qkv, ba, alog, dtb, dout, states, ycors, *extra.inputs)
    return res[:4], res[4:]


def _sgu_mask():
    row = lax.broadcasted_iota(jnp.int32, (SGU_BLOCK, SGU_BLOCK), 0)
    col = lax.broadcasted_iota(jnp.int32, (SGU_BLOCK, SGU_BLOCK), 1)
    sh = int(math.log2(CHUNK))
    return lax.shift_right_logical(row, sh) >= lax.shift_right_logical(col, sh)


def _gate_sgu_fwd(o, projm, onw, lng, lnb, ws, bst, d):
    t = o.shape[0]
    heads, groups = d // DN_DK, d // SGU_GROUP_DIM
    tb = _tile(t, SGU_WINDOWS * SGU_BLOCK, SGU_BLOCK)
    row_spec = pl.BlockSpec((1, d), lambda i: (0, 0))

    def body(o_ref, z_ref, u_ref, v_ref, onw_ref, lng_ref, lnb_ref, ws_ref, bst_ref, ya_ref, yb_ref):
        for h in range(heads):
            cols = slice(h * DN_DK, (h + 1) * DN_DK)
            oh, zh = o_ref[:, cols], z_ref[:, cols]
            r = lax.rsqrt(jnp.mean(oh * oh, axis=1, keepdims=True) + RMS_EPS)
            ya_ref[:, cols] = (oh * r * onw_ref[:, cols] * (zh * _sigmoid(zh))).astype(ya_ref.dtype)
        xhat, _ = _ln_hat(_gelu(v_ref[...]))
        vgn = xhat * lng_ref[...] + lnb_ref[...]
        mask = _sgu_mask()
        lane = lax.broadcasted_iota(jnp.int32, (SGU_BLOCK, LANES), 1)
        bst_v = bst_ref[...]
        for gi in range(groups):
            cols = slice(gi * SGU_GROUP_DIM, (gi + 1) * SGU_GROUP_DIM)
            wsg = jnp.where(mask, ws_ref[gi], 0.0)
            bias = _col_of(bst_v, lane, gi)
            for win in range(tb // SGU_BLOCK):
                rows = slice(win * SGU_BLOCK, (win + 1) * SGU_BLOCK)
                sp = _dot(wsg, vgn[rows, cols]) + bias
                yb_ref[rows, cols] = (_gelu(u_ref[rows, cols]) * sp).astype(yb_ref.dtype)

    return pl.pallas_call(
        body, name="gate_sgu_fwd", grid=(t // tb,),
        in_specs=[pl.BlockSpec((tb, d), lambda i: (i, 0)),
                  pl.BlockSpec((tb, d), lambda i: (i, 3)),
                  pl.BlockSpec((tb, d), lambda i: (i, 4)),
                  pl.BlockSpec((tb, d), lambda i: (i, 5)),
                  row_spec, row_spec, row_spec,
                  pl.BlockSpec((groups, SGU_BLOCK, SGU_BLOCK), lambda i: (0, 0, 0)),
                  pl.BlockSpec((SGU_BLOCK, LANES), lambda i: (0, 0))],
        out_specs=[pl.BlockSpec((tb, d), lambda i: (i, 0)), pl.BlockSpec((tb, d), lambda i: (i, 0))],
        out_shape=[jax.ShapeDtypeStruct((t, d), ACT), jax.ShapeDtypeStruct((t, d), ACT)],
        compiler_params=_cparams(("parallel",)),
    )(o, projm, projm, projm, onw, lng, lnb, ws, bst)


def _gate_sgu_bwd(dya, dyb, o, projm, onw, lng, lnb, ws, bst, dprojm, d, carried=None):
    t = o.shape[0]
    heads, groups = d // DN_DK, d // SGU_GROUP_DIM
    tb = _tile(t, SGU_WINDOWS * SGU_BLOCK, SGU_BLOCK)
    extra = carried or _Carried([], [], 0, None)
    row_spec = pl.BlockSpec((1, d), lambda i: (0, 0))
    acc_row = pl.BlockSpec((SUBLANES, d), lambda i: (0, 0))

    def body(dya_ref, dyb_ref, o_ref, z_ref, u_ref, v_ref, onw_ref, lng_ref, lnb_ref, ws_ref, bst_ref, alias_ref,
             do_ref, dp_ref, donw_ref, dlng_ref, dlnb_ref, dws_ref, dbst_ref):
        @pl.when(pl.program_id(0) == 0)
        def _():
            for r_ in (donw_ref, dlng_ref, dlnb_ref, dws_ref, dbst_ref):
                r_[...] = jnp.zeros_like(r_)

        donw = jnp.zeros((SUBLANES, DN_DK), F32)
        for h in range(heads):
            cols = slice(h * DN_DK, (h + 1) * DN_DK)
            oh, zh, dyah, wh = o_ref[:, cols], z_ref[:, cols], dya_ref[:, cols], onw_ref[:, cols]
            r = lax.rsqrt(jnp.mean(oh * oh, axis=1, keepdims=True) + RMS_EPS)
            on = oh * r
            sz = _sigmoid(zh)
            silu_z = zh * sz
            don = dyah * wh * silu_z
            dp_ref[:, cols] = (dyah * on * wh * (sz * (1.0 + zh * (1.0 - sz)))).astype(dp_ref.dtype)
            donw = donw + _fold8(dyah * on * silu_z)
            do_ref[:, cols] = r * (don - on * jnp.mean(don * on, axis=1, keepdims=True))
        donw_ref[...] += donw

        vgp, up = v_ref[...], u_ref[...]
        xhat, rstd = _ln_hat(_gelu(vgp))
        lng_v = lng_ref[...]
        vgn = xhat * lng_v + lnb_ref[...]
        ua = _gelu(up)
        mask = _sgu_mask()
        lane = lax.broadcasted_iota(jnp.int32, (SGU_BLOCK, LANES), 1)
        bst_v = bst_ref[...]
        dbst = jnp.zeros((SGU_BLOCK, LANES), F32)
        dvgn_parts, dua_parts = [], []
        for gi in range(groups):
            cols = slice(gi * SGU_GROUP_DIM, (gi + 1) * SGU_GROUP_DIM)
            wsg = jnp.where(mask, ws_ref[gi], 0.0)
            bias = _col_of(bst_v, lane, gi)
            dws = jnp.zeros((SGU_BLOCK, SGU_BLOCK), F32)
            dvgn_g, dua_g = [], []
            for win in range(tb // SGU_BLOCK):
                rows = slice(win * SGU_BLOCK, (win + 1) * SGU_BLOCK)
                vg_g, dyb_g = vgn[rows, cols], dyb_ref[rows, cols]
                sp = _dot(wsg, vg_g) + bias
                dsp = dyb_g * ua[rows, cols]
                dua_g.append(dyb_g * sp)
                dws = dws + _dot(dsp, vg_g, NT)
                dbst = dbst + jnp.where(lane == gi, jnp.sum(dsp, axis=1, keepdims=True), 0.0)
                dvgn_g.append(_dot(wsg, dsp, TN))
            dws_ref[gi] += jnp.where(mask, dws, 0.0)
            dvgn_parts.append(jnp.concatenate(dvgn_g, axis=0))
            dua_parts.append(jnp.concatenate(dua_g, axis=0))
        dbst_ref[...] += dbst
        dvgn = jnp.concatenate(dvgn_parts, axis=1)
        dua = jnp.concatenate(dua_parts, axis=1)
        dlng_ref[...] += _fold8(dvgn * xhat)
        dlnb_ref[...] += _fold8(dvgn)
        dvga = _ln_bwd(dvgn * lng_v, xhat, rstd)
        dp_ref[:, d:2 * d] = (dua * _gelu_grad(up)).astype(dp_ref.dtype)
        dp_ref[:, 2 * d:] = (dvga * _gelu_grad(vgp)).astype(dp_ref.dtype)

    res = pl.pallas_call(
        _carrying(body, 12, 7, 0, carried, (t // tb,)),
        name="gate_sgu_bwd_carrying" if carried else "gate_sgu_bwd", grid=(t // tb,),
        in_specs=[pl.BlockSpec((tb, d), lambda i: (i, 0)),
                  pl.BlockSpec((tb, d), lambda i: (i, 0)),
                  pl.BlockSpec((tb, d), lambda i: (i, 0)),
                  pl.BlockSpec((tb, d), lambda i: (i, 3)),
                  pl.BlockSpec((tb, d), lambda i: (i, 4)),
                  pl.BlockSpec((tb, d), lambda i: (i, 5)),
                  row_spec, row_spec, row_spec,
                  pl.BlockSpec((groups, SGU_BLOCK, SGU_BLOCK), lambda i: (0, 0, 0)),
                  pl.BlockSpec((SGU_BLOCK, LANES), lambda i: (0, 0)),
                  ANY] + [ANY] * len(extra.inputs),
        out_specs=[pl.BlockSpec((tb, d), lambda i: (i, 0)),
                   pl.BlockSpec((tb, 3 * d), lambda i: (i, 1)),
                   pl.BlockSpec((SUBLANES, DN_DK), lambda i: (0, 0)),
                   acc_row, acc_row,
                   pl.BlockSpec((groups, SGU_BLOCK, SGU_BLOCK), lambda i: (0, 0, 0)),
                   pl.BlockSpec((SGU_BLOCK, LANES), lambda i: (0, 0))] + [ANY] * len(extra.out_shapes),
        out_shape=[jax.ShapeDtypeStruct((t, d), F32),
                   jax.ShapeDtypeStruct(dprojm.shape, dprojm.dtype),
                   jax.ShapeDtypeStruct((SUBLANES, DN_DK), F32),
                   jax.ShapeDtypeStruct((SUBLANES, d), F32),
                   jax.ShapeDtypeStruct((SUBLANES, d), F32),
                   jax.ShapeDtypeStruct((groups, SGU_BLOCK, SGU_BLOCK), F32),
                   jax.ShapeDtypeStruct((SGU_BLOCK, LANES), F32)] + extra.out_shapes,
        input_output_aliases={11: 1},
        scratch_shapes=extra.scratch() if carried else [],
        compiler_params=_cparams(("arbitrary",)),
    )(dya, dyb, o, projm, projm, projm, onw, lng, lnb, ws, bst, dprojm, *extra.inputs)
    return res[:7], res[7:]


def _mix_fwd(ya, yb, projm, x, wpa, wpb, wo, g1, b1, d, tb):
    t = x.shape[0]
    blk = pl.BlockSpec((tb, d), lambda i: (i, 0))
    wspec = pl.BlockSpec((d, d), lambda i: (0, 0))
    row_spec = pl.BlockSpec((1, d), lambda i: (0, 0))

    def body(ya_ref, yb_ref, ga_ref, gb_ref, x_ref, wpa_ref, wpb_ref, wo_ref, g_ref, b_ref,
             pa_ref, pb_ref, m_ref, h_ref, x1_ref, x1b_ref):
        pa = _dot(ya_ref[...], wpa_ref[...])
        pb = _dot(yb_ref[...], wpb_ref[...])
        m = _sigmoid(ga_ref[...]) * pa + _sigmoid(gb_ref[...]) * pb
        hres = ALPHA * x_ref[...] + _dot(m, wo_ref[...])
        xhat, _ = _ln_hat(hres)
        x1 = xhat * g_ref[...] + b_ref[...]
        pa_ref[...] = pa
        pb_ref[...] = pb
        m_ref[...] = m.astype(m_ref.dtype)
        h_ref[...] = hres
        x1_ref[...] = x1
        x1b_ref[...] = x1.astype(x1b_ref.dtype)

    f32_out = jax.ShapeDtypeStruct((t, d), F32)
    bf_out = jax.ShapeDtypeStruct((t, d), ACT)
    return pl.pallas_call(
        body, name="mix_fwd", grid=(t // tb,),
        in_specs=[blk, blk, pl.BlockSpec((tb, d), lambda i: (i, 6)), pl.BlockSpec((tb, d), lambda i: (i, 7)),
                  blk, wspec, wspec, wspec, row_spec, row_spec],
        out_specs=[blk] * 6,
        out_shape=[f32_out, f32_out, bf_out, f32_out, f32_out, bf_out],
        compiler_params=_cparams(("parallel",)),
    )(ya, yb, projm, projm, x, wpa, wpb, wo, g1, b1)


def _mix_bwd(dmix, pa, pb, projm, wpa, wpb, wo, d, tb):
    t = dmix.shape[0]
    blk = pl.BlockSpec((tb, d), lambda i: (i, 0))
    wspec = pl.BlockSpec((d, d), lambda i: (0, 0))

    def body(dmix_ref, pa_ref, pb_ref, ga_ref, gb_ref, wpa_ref, wpb_ref, wo_ref,
             dpa_ref, dpb_ref, dya_ref, dyb_ref, dg_ref):
        dm = _dot(dmix_ref[...], wo_ref[...], NT)
        sa, sb = _sigmoid(ga_ref[...]), _sigmoid(gb_ref[...])
        dpa, dpb = dm * sa, dm * sb
        dpa_ref[...] = dpa.astype(dpa_ref.dtype)
        dpb_ref[...] = dpb.astype(dpb_ref.dtype)
        dg_ref[:, :d] = (dm * pa_ref[...] * sa * (1.0 - sa)).astype(dg_ref.dtype)
        dg_ref[:, d:] = (dm * pb_ref[...] * sb * (1.0 - sb)).astype(dg_ref.dtype)
        dya_ref[...] = _dot(dpa, wpa_ref[...], NT)
        dyb_ref[...] = _dot(dpb, wpb_ref[...], NT)

    return pl.pallas_call(
        body, name="mix_bwd", grid=(t // tb,),
        in_specs=[blk, blk, blk, pl.BlockSpec((tb, d), lambda i: (i, 6)), pl.BlockSpec((tb, d), lambda i: (i, 7)),
                  wspec, wspec, wspec],
        out_specs=[blk, blk, blk, blk, pl.BlockSpec((tb, 2 * d), lambda i: (i, 3))],
        out_shape=[jax.ShapeDtypeStruct((t, d), ACT), jax.ShapeDtypeStruct((t, d), ACT),
                   jax.ShapeDtypeStruct((t, d), F32), jax.ShapeDtypeStruct((t, d), F32),
                   jax.ShapeDtypeStruct((t, 8 * d), ACT)],
        compiler_params=_cparams(("parallel",)),
    )(dmix, pa, pb, projm, projm, wpa, wpb, wo)


def _ffn_tail_fwd(gu, wd, x1, g, b, tb):
    t, d = x1.shape
    f = wd.shape[0]
    fc = _tile(f, MM_TILE)
    blk = pl.BlockSpec((tb, d), lambda i: (i, 0))
    row_spec = pl.BlockSpec((1, d), lambda i: (0, 0))

    def body(gu_ref, wd_ref, x_ref, g_ref, b_ref, a_ref, h_ref, y_ref, yb_ref):
        ffn = jnp.zeros((tb, d), F32)
        for c in range(f // fc):
            gp = gu_ref[:, c * fc:(c + 1) * fc].astype(F32)
            act = (gp * _sigmoid(gp) * gu_ref[:, f + c * fc:f + (c + 1) * fc].astype(F32)).astype(a_ref.dtype)
            a_ref[:, c * fc:(c + 1) * fc] = act
            ffn = ffn + _dot(act, wd_ref[c * fc:(c + 1) * fc, :])
        hres = ALPHA * x_ref[...] + ffn
        xhat, _ = _ln_hat(hres)
        y = xhat * g_ref[...] + b_ref[...]
        h_ref[...] = hres
        y_ref[...] = y
        yb_ref[...] = y.astype(yb_ref.dtype)

    return pl.pallas_call(
        body, name="ffn_tail_fwd", grid=(t // tb,),
        in_specs=[pl.BlockSpec((tb, 2 * f), lambda i: (i, 0)), pl.BlockSpec((f, d), lambda i: (0, 0)),
                  blk, row_spec, row_spec],
        out_specs=[pl.BlockSpec((tb, f), lambda i: (i, 0)), blk, blk, blk],
        out_shape=[jax.ShapeDtypeStruct((t, f), ACT), jax.ShapeDtypeStruct((t, d), F32),
                   jax.ShapeDtypeStruct((t, d), F32), jax.ShapeDtypeStruct((t, d), ACT)],
        compiler_params=_cparams(("parallel",)),
    )(gu, wd, x1, g, b)


def _ffn_tail_bwd(dh, wd, gu, tb):
    t, d = dh.shape
    f = wd.shape[0]
    fc = _tile(f, MM_TILE)

    def body(dh_ref, wd_ref, gu_ref, dgu_ref):
        dh_v = dh_ref[...]
        for c in range(f // fc):
            da = _dot(dh_v, wd_ref[c * fc:(c + 1) * fc, :], NT)
            gp = gu_ref[:, c * fc:(c + 1) * fc].astype(F32)
            sg = _sigmoid(gp)
            dgu_ref[:, c * fc:(c + 1) * fc] = (
                da * gu_ref[:, f + c * fc:f + (c + 1) * fc].astype(F32) * sg * (1.0 + gp * (1.0 - sg))
            ).astype(dgu_ref.dtype)
            dgu_ref[:, f + c * fc:f + (c + 1) * fc] = (da * gp * sg).astype(dgu_ref.dtype)

    return pl.pallas_call(
        body, name="ffn_tail_bwd", grid=(t // tb,),
        in_specs=[pl.BlockSpec((tb, d), lambda i: (i, 0)), pl.BlockSpec((f, d), lambda i: (0, 0)),
                  pl.BlockSpec((tb, 2 * f), lambda i: (i, 0))],
        out_specs=pl.BlockSpec((tb, 2 * f), lambda i: (i, 0)),
        out_shape=jax.ShapeDtypeStruct((t, 2 * f), ACT),
        compiler_params=_cparams(("parallel",)),
    )(dh, wd, gu)


def _ffn_head_bwd(dgu, wgu, dh2, hres, g, tb):
    t, d = dh2.shape
    f2 = wgu.shape[1]
    blk = pl.BlockSpec((tb, d), lambda i: (i, 0))
    acc = pl.BlockSpec((SUBLANES, d), lambda i: (0, 0))

    def body(dgu_ref, w_ref, dh2_ref, h_ref, g_ref, dh_ref, dhb_ref, dg_ref, db_ref):
        @pl.when(pl.program_id(0) == 0)
        def _():
            dg_ref[...] = jnp.zeros_like(dg_ref)
            db_ref[...] = jnp.zeros_like(db_ref)

        dy_v = _dot(dgu_ref[...], w_ref[...], NT) + ALPHA * dh2_ref[...]
        xhat, r = _ln_hat(h_ref[...])
        dh = _ln_bwd(dy_v * g_ref[...], xhat, r)
        dh_ref[...] = dh
        dhb_ref[...] = dh.astype(dhb_ref.dtype)
        dg_ref[...] += _fold8(dy_v * xhat)
        db_ref[...] += _fold8(dy_v)

    return pl.pallas_call(
        body, name="ffn_head_bwd", grid=(t // tb,),
        in_specs=[pl.BlockSpec((tb, f2), lambda i: (i, 0)), pl.BlockSpec((d, f2), lambda i: (0, 0)),
                  blk, blk, pl.BlockSpec((1, d), lambda i: (0, 0))],
        out_specs=[blk, blk, acc, acc],
        out_shape=[jax.ShapeDtypeStruct((t, d), F32), jax.ShapeDtypeStruct((t, d), ACT),
                   jax.ShapeDtypeStruct((SUBLANES, d), F32), jax.ShapeDtypeStruct((SUBLANES, d), F32)],
        compiler_params=_cparams(("arbitrary",)),
    )(dgu, wgu, dh2, hres, g)


def _loss_ln_bwd(y, target, hres, g, tb):
    t, d = y.shape
    blk = pl.BlockSpec((tb, d), lambda i: (i, 0))
    acc = pl.BlockSpec((SUBLANES, d), lambda i: (0, 0))

    def body(y_ref, t_ref, h_ref, g_ref, dh_ref, dhb_ref, dg_ref, db_ref, l_ref):
        @pl.when(pl.program_id(0) == 0)
        def _():
            for r_ in (dg_ref, db_ref, l_ref):
                r_[...] = jnp.zeros_like(r_)

        err = y_ref[...] - t_ref[...]
        dy_v = err * (1.0 / d)
        sq = _fold8(err * err)
        part = sq[:, :LANES]
        for c in range(1, d // LANES):
            part = part + sq[:, c * LANES:(c + 1) * LANES]
        l_ref[...] += part
        xhat, r = _ln_hat(h_ref[...])
        dh = _ln_bwd(dy_v * g_ref[...], xhat, r)
        dh_ref[...] = dh
        dhb_ref[...] = dh.astype(dhb_ref.dtype)
        dg_ref[...] += _fold8(dy_v * xhat)
        db_ref[...] += _fold8(dy_v)

    res = pl.pallas_call(
        body, name="loss_ln_bwd", grid=(t // tb,),
        in_specs=[blk, blk, blk, pl.BlockSpec((1, d), lambda i: (0, 0))],
        out_specs=[blk, blk, acc, acc, pl.BlockSpec((SUBLANES, LANES), lambda i: (0, 0))],
        out_shape=[jax.ShapeDtypeStruct((t, d), F32), jax.ShapeDtypeStruct((t, d), ACT),
                   jax.ShapeDtypeStruct((SUBLANES, d), F32), jax.ShapeDtypeStruct((SUBLANES, d), F32),
                   jax.ShapeDtypeStruct((SUBLANES, LANES), F32)],
        compiler_params=_cparams(("arbitrary",)),
    )(y, target, hres, g)
    return res[:4], res[4]


def _adamw(w, g, m, v):
    shape = w.shape
    cols = shape[-1]
    w2, g2, m2, v2 = (a.reshape(-1, cols) for a in (w, g, m, v))
    rows = w2.shape[0]
    tr = _tile(rows, 256, SUBLANES)
    blk = pl.BlockSpec((tr, cols), lambda i: (i, 0))

    def body(w_ref, g_ref, m_ref, v_ref, d_ref, nm_ref, nv_ref):
        g_v = g_ref[...]
        nm = ADAM_B1 * m_ref[...] + (1.0 - ADAM_B1) * g_v
        nv = ADAM_B2 * v_ref[...] + (1.0 - ADAM_B2) * (g_v * g_v)
        m_hat = nm / (1.0 - ADAM_B1 ** ADAM_STEP)
        v_hat = nv / (1.0 - ADAM_B2 ** ADAM_STEP)
        d_ref[...] = -ADAM_LR * (m_hat / (jnp.sqrt(v_hat) + ADAM_EPS) + ADAM_WD * w_ref[...])
        nm_ref[...] = nm
        nv_ref[...] = nv

    out = jax.ShapeDtypeStruct((rows, cols), F32)
    res = pl.pallas_call(
        body, name="adamw", grid=(rows // tr,),
        in_specs=[blk] * 4, out_specs=[blk] * 3, out_shape=[out] * 3,
        compiler_params=_cparams(("parallel",)),
    )(w2, g2, m2, v2)
    return tuple(r.reshape(shape) for r in res)


def _place():
    x, y, c = lax.axis_index("x"), lax.axis_index("y"), lax.axis_index("c")
    return x, y, c, [(1 - x, y), (x, 1 - y), (1 - x, 1 - y)]


def _remote(src, dst, send_sems, recv_sems, k, to):
    return pltpu.make_async_remote_copy(src_ref=src, dst_ref=dst, send_sem=send_sems.at[k],
                                        recv_sem=recv_sems.at[k], device_id=to, device_id_type=MESH)


class _Carried:
    def __init__(self, inputs, out_shapes, n_sems, copies):
        self.inputs, self.out_shapes, self.n_sems, self.copies = list(inputs), list(out_shapes), n_sems, copies

    def scratch(self):
        return [pltpu.SemaphoreType.DMA((self.n_sems,)), pltpu.SemaphoreType.DMA((self.n_sems,))]


def _join_plans(first, second):
    ni, no, ns = len(first.inputs), len(first.out_shapes), first.n_sems

    def copies(in_refs, out_refs, send_sems, recv_sems):
        one = _phases(first.copies(in_refs[:ni], out_refs[:no], send_sems, recv_sems))
        two = _phases(second.copies(in_refs[ni:], out_refs[no:], send_sems.at[pl.ds(ns, second.n_sems)],
                                    recv_sems.at[pl.ds(ns, second.n_sems)]))

        def both(k):
            def run():
                one[k]()
                two[k]()
            return run

        return both(0), both(1), both(2)

    return _Carried(first.inputs + second.inputs, first.out_shapes + second.out_shapes, ns + second.n_sems, copies)


def _run_comm(name, plan):
    n_in, n_out = len(plan.inputs), len(plan.out_shapes)

    def body(*refs):
        for phase in _phases(plan.copies(refs[:n_in], refs[n_in:n_in + n_out], refs[-2], refs[-1])):
            phase()

    return pl.pallas_call(
        body, name=name, in_specs=[ANY] * n_in, out_specs=[ANY] * n_out, out_shape=plan.out_shapes,
        scratch_shapes=plan.scratch(),
    )(*plan.inputs)


def _half_rows(rows, core):
    if rows % (4 * SUBLANES):
        return None
    return pl.ds(pl.multiple_of(core * (rows // 2), 2 * SUBLANES), rows // 2)


def _all_gather_plan(shards):
    n = len(shards)

    def copies(x_refs, out_refs, send_sems, recv_sems):
        x, y, c, chips = _place()
        sibling = (x, y, 1 - c)
        mine = 2 * x + y
        split = [_half_rows(x_refs[t].shape[0], c) is not None for t in range(n)]

        def src(t):
            return x_refs[t].at[_half_rows(x_refs[t].shape[0], c)] if split[t] else x_refs[t]

        def slot(t, chip_idx, core):
            rows = _half_rows(x_refs[t].shape[0], core)
            return out_refs[t].at[chip_idx, rows] if split[t] else out_refs[t].at[chip_idx]

        def first():
            return [_remote(src(t), slot(t, mine, c), send_sems, recv_sems, 6 * t + j, (cx, cy, c))
                    for j, (cx, cy) in enumerate(chips) for t in range(n)]

        def start():
            for cp in first():
                cp.start()

        def passed():
            return [_remote(slot(t, 2 * cx + cy, c), slot(t, 2 * cx + cy, c), send_sems, recv_sems, 6 * t + 3 + j,
                            sibling) for j, (cx, cy) in enumerate(chips) for t in range(n) if split[t]]

        def middle():
            for j, (cx, cy) in enumerate(chips):
                for t in range(n):
                    theirs = slot(t, 2 * cx + cy, c)
                    _remote(theirs, theirs, send_sems, recv_sems, 6 * t + j, (cx, cy, c)).wait_recv()
            for cp in passed():
                cp.start()

        def finish():
            for j, (cx, cy) in enumerate(chips):
                for t in range(n):
                    if split[t]:
                        other = slot(t, 2 * cx + cy, 1 - c)
                        _remote(other, other, send_sems, recv_sems, 6 * t + 3 + j, sibling).wait_recv()
            for cp in first() + passed():
                cp.wait_send()

        return start, middle, finish

    return _Carried(shards, [jax.ShapeDtypeStruct((N_CHIPS,) + s.shape, s.dtype) for s in shards], 6 * n, copies)


def _sibling_exchange_plan(grads, small=None):
    n = len(grads)
    extra = [] if small is None else [small]

    def copies(in_refs, out_refs, send_sems, recv_sems):
        x, y, c, _ = _place()
        sibling = (x, y, 1 - c)

        def all_copies():
            cps = [_remote(in_refs[t].at[:, _half_rows(in_refs[t].shape[1], 1 - c), :], out_refs[t],
                           send_sems, recv_sems, t, sibling) for t in range(n)]
            if extra:
                cps.append(_remote(in_refs[n], out_refs[n], send_sems, recv_sems, n, sibling))
            return cps

        def start():
            for cp in all_copies():
                cp.start()

        def finish():
            for cp in all_copies():
                cp.wait()

        return start, finish

    shapes = [jax.ShapeDtypeStruct((g.shape[0], g.shape[1] // 2, g.shape[2]), g.dtype) for g in grads]
    shapes += [jax.ShapeDtypeStruct(s.shape, s.dtype) for s in extra]
    return _Carried(list(grads) + extra, shapes, n + 1, copies)


def _chip_exchange_plan(travel, small=None):
    n = len(travel)
    extra = [] if small is None else [small]

    def copies(in_refs, out_refs, send_sems, recv_sems):
        x, y, c, chips = _place()
        mine = 2 * x + y

        def all_copies():
            cps = []
            for j, (cx, cy) in enumerate(chips):
                to = (cx, cy, c)
                for t in range(n):
                    cps.append(_remote(in_refs[t].at[2 * cx + cy], out_refs[t].at[mine], send_sems, recv_sems,
                                       3 * t + j, to))
                if extra:
                    cps.append(_remote(in_refs[n], out_refs[n].at[mine], send_sems, recv_sems, 3 * n + j, to))
            return cps

        def start():
            for cp in all_copies():
                cp.start()

        def finish():
            for cp in all_copies():
                cp.wait()

        return start, finish

    shapes = [jax.ShapeDtypeStruct(g.shape, g.dtype) for g in travel]
    shapes += [jax.ShapeDtypeStruct((N_CHIPS,) + s.shape, s.dtype) for s in extra]
    return _Carried(list(travel) + extra, shapes, 3 * n + 3, copies)


def _sibling_merge_plan(reduced):
    n = len(reduced)

    def copies(in_refs, out_refs, send_sems, recv_sems):
        x, y, c, _ = _place()

        def all_copies():
            return [_remote(in_refs[t], out_refs[t], send_sems, recv_sems, t, (x, y, 1 - c)) for t in range(n)]

        def start():
            for cp in all_copies():
                cp.start()

        def finish():
            for cp in all_copies():
                cp.wait()

        return start, finish

    return _Carried(reduced, [jax.ShapeDtypeStruct(r.shape, r.dtype) for r in reduced], n, copies)


def _pair_sum(place, grad, land):
    n, r, c = grad.shape
    half = r // 2
    tr = _tile(half, 256, SUBLANES)
    nb = half // tr

    def body(place_ref, a_ref, b_ref, travel_ref, own_ref):
        total = a_ref[0] + b_ref[0]
        travel_ref[0] = total.astype(travel_ref.dtype)

        @pl.when(pl.program_id(1) == place_ref[1])
        def _():
            own_ref[...] = total

    return pl.pallas_call(
        body, name="grad_pair_sum",
        grid_spec=pltpu.PrefetchScalarGridSpec(
            num_scalar_prefetch=1, grid=(nb, n),
            in_specs=[pl.BlockSpec((1, tr, c), lambda i, s, p: (s, p[0] * nb + i, 0)),
                      pl.BlockSpec((1, tr, c), lambda i, s, p: (s, i, 0))],
            out_specs=[pl.BlockSpec((1, tr, c), lambda i, s, p: (s, i, 0)),
                       pl.BlockSpec((tr, c), lambda i, s, p: (i, 0))]),
        out_shape=[jax.ShapeDtypeStruct((n, half, c), BF16), jax.ShapeDtypeStruct((half, c), F32)],
        compiler_params=_cparams(("parallel", "arbitrary")),
    )(place, grad, land)


def _chip_sum(place, own, land, name):
    n, r, c = land.shape
    tr = _tile(r, 256, SUBLANES)

    def body(place_ref, own_ref, land_ref, o_ref):
        mine = place_ref[1]
        acc = jnp.zeros(o_ref.shape, F32)
        for s in range(n):
            acc = acc + jnp.where(mine == s, own_ref[...], land_ref[s].astype(F32))
        o_ref[...] = acc

    return pl.pallas_call(
        body, name=name,
        grid_spec=pltpu.PrefetchScalarGridSpec(
            num_scalar_prefetch=1, grid=(r // tr,),
            in_specs=[pl.BlockSpec((tr, c), lambda i, p: (i, 0)),
                      pl.BlockSpec((n, tr, c), lambda i, p: (0, i, 0))],
            out_specs=pl.BlockSpec((tr, c), lambda i, p: (i, 0))),
        out_shape=jax.ShapeDtypeStruct((r, c), F32),
        compiler_params=_cparams(("parallel",)),
    )(place, own, land)


def _add2(a, b):
    rows = a.shape[0]
    tr = _tile(rows, 256, SUBLANES)
    blk = pl.BlockSpec((tr, a.shape[1]), lambda i: (i, 0))

    def body(a_ref, b_ref, o_ref):
        o_ref[...] = a_ref[...] + b_ref[...]

    return pl.pallas_call(
        body, name="grad_small_pair_sum", grid=(rows // tr,), in_specs=[blk, blk], out_specs=blk,
        out_shape=jax.ShapeDtypeStruct(a.shape, F32), compiler_params=_cparams(("parallel",)),
    )(a, b)


def _merge_halves(place, mine, other):
    first_core = place[0] == 0
    return jnp.concatenate([jnp.where(first_core, mine, other), jnp.where(first_core, other, mine)], axis=0)


_BIG = (("w_in", 2), ("w_pa", 1), ("w_pb", 1), ("w_o", 1), ("w_ffn_gate", 2), ("w_ffn_up", 2),
        ("w_ffn_down", 1))
_SMALL = ("conv_w", "a_log", "dt_bias", "o_norm_w", "sgu_ln_g", "sgu_ln_b", "w_s", "b_s",
          "ln1_g", "ln1_b", "ln2_g", "ln2_b")


def _pack_small(arrays):
    pieces = []
    for a in arrays:
        if a.shape[-1] % LANES == 0:
            a2 = a.reshape(-1, LANES)
        else:
            a2 = jnp.pad(a.reshape(-1, a.shape[-1]), ((0, 0), (0, LANES - a.shape[-1])))
        pieces.append(jnp.pad(a2, ((0, -a2.shape[0] % SUBLANES), (0, 0))))
    return jnp.concatenate(pieces, axis=0)


def _unpack_small(buf, like):
    out, off = [], 0
    for a in like:
        if a.shape[-1] % LANES == 0:
            rows = a.size // LANES
            out.append(buf[off:off + rows].reshape(a.shape))
        else:
            rows = a.size // a.shape[-1]
            out.append(buf[off:off + rows, :a.shape[-1]].reshape(a.shape))
        off += -(-rows // SUBLANES) * SUBLANES
    return out


def _unshard(gathered, local, chip, axis):
    parts = [jnp.where(chip == s, local, gathered[s]) for s in range(N_CHIPS)]
    return jnp.concatenate(parts, axis=axis - 1)


def _to_shards(full, axis):
    l, r, c = full.shape
    if axis == 1:
        return full.reshape(l, N_CHIPS, r // N_CHIPS, c)
    return jnp.transpose(full.reshape(l, r, N_CHIPS, c // N_CHIPS), (0, 2, 1, 3))


def _row(v, width=None):
    v = v.reshape(1, -1).astype(F32)
    if width is not None and v.shape[1] < width:
        v = jnp.pad(v, ((0, 0), (0, width - v.shape[1])))
    return v


def _layer_consts(p, l, d):
    heads = d // DN_DK
    return dict(
        alog=_row(p["a_log"][l], LANES), dtb=_row(p["dt_bias"][l], LANES),
        onw=_row(jnp.tile(p["o_norm_w"][l], heads)),
        lng=_row(p["sgu_ln_g"][l]), lnb=_row(p["sgu_ln_b"][l]),
        ws=p["w_s"][l].astype(F32),
        bst=jnp.pad(p["b_s"][l].T, ((0, 0), (0, LANES - p["b_s"].shape[1]))),
        g1=_row(p["ln1_g"][l]), b1=_row(p["ln1_b"][l]), g2=_row(p["ln2_g"][l]), b2=_row(p["ln2_b"][l]))


class _NoComm:
    def with_proj_main(self):
        return None

    def after_proj_main(self, got):
        pass

    def weights(self, full):
        return full

    def with_dn_fwd(self):
        return None

    def after_dn_fwd(self, got):
        pass

    def with_ffn_in_dw(self):
        return None

    def after_ffn_in_dw(self, got):
        pass

    def after_branch_grads(self, g):
        pass

    def with_dn_bwd(self):
        return None

    def after_dn_bwd(self, got):
        pass

    def with_proj_main_dw(self):
        return None

    def after_proj_main_dw(self, got):
        pass

    def with_ffn_in(self):
        return None

    def after_ffn_in(self, got):
        pass

    def after_all_grads(self, g):
        pass

    def with_gate_sgu_bwd(self):
        return None

    def after_gate_sgu_bwd(self, got):
        pass

    def with_proj_gates_dx(self):
        return None

    def after_proj_gates_dx(self, got):
        pass

    def with_proj_main_dx(self):
        return None

    def after_proj_main_dx(self, got):
        pass


def _carry(carried, after, call, *args, **kw):
    if carried is None:
        return call(*args, **kw)
    out, got = call(*args, carried=carried, **kw)
    after(got)
    return out


def _in_proj_weights(w_in, d):
    heads, q4 = d // DN_DK, 4 * d
    wba = jnp.zeros((d, 2 * LANES), w_in.dtype)
    wba = wba.at[:, :heads].set(w_in[:, q4:q4 + heads])
    wba = wba.at[:, LANES:LANES + heads].set(w_in[:, q4 + heads:q4 + 2 * heads])
    return jnp.concatenate([w_in[:, :q4], w_in[:, q4 + 2 * heads:]], axis=1), wba


def _layer_fwd(x, xb, full, cl, d, tb, comm):
    wm, wba = _in_proj_weights(full["w_in"], d)
    projm = _carry(comm.with_proj_main(), comm.after_proj_main, _matmul, xb, wm, NN, "proj_main", tn=MM_WIDE)
    full = comm.weights(full)
    wl = dict(wm=wm, wba=wba, conv=full["conv_w"], wpa=full["w_pa"], wpb=full["w_pb"], wo=full["w_o"],
              wgu=jnp.concatenate([full["w_ffn_gate"], full["w_ffn_up"]], axis=1), wd=full["w_ffn_down"])
    ba = _matmul(xb, wba, NN, "proj_gates")
    qkv = _conv_fwd(projm, wl["conv"], d, _tile(x.shape[0], 2 * tb, SUBLANES))
    (o, states, ycors), got = _dn_fwd(qkv, ba, cl["alog"], cl["dtb"], d, comm.with_dn_fwd())
    comm.after_dn_fwd(got)
    ya, yb = _gate_sgu_fwd(o, projm, cl["onw"], cl["lng"], cl["lnb"], cl["ws"], cl["bst"], d)
    pa, pb, m, h1, x1, x1b = _mix_fwd(ya, yb, projm, x, wl["wpa"], wl["wpb"], wl["wo"], cl["g1"], cl["b1"], d, tb)
    gu = _carry(comm.with_ffn_in(), comm.after_ffn_in, _matmul, x1b, wl["wgu"], NN, "ffn_in", out_dtype=ACT)
    act, h2, x2, x2b = _ffn_tail_fwd(gu, wl["wd"], x1, cl["g2"], cl["b2"], tb)
    saved = dict(xb=xb, projm=projm, ba=ba, qkv=qkv, o=o, states=states, ycors=ycors, ya=ya, yb=yb,
                 pa=pa, pb=pb, m=m, h1=h1, x1b=x1b, gu=gu, act=act, h2=h2)
    return x2, x2b, saved, wl


def _layer_bwd(sv, wl, cl, d, tb, comm, ln2_bwd, next_ln=None):
    g = {}
    dh2, dh2b, dg2, db2 = ln2_bwd
    g["ln2_g"], g["ln2_b"] = dg2.sum(0), db2.sum(0)
    g["wd"] = _matmul(sv["act"], dh2b, TN, "ffn_out_dw")
    dgu = _ffn_tail_bwd(dh2b, wl["wd"], sv["gu"], tb)
    g["wgu"] = _carry(comm.with_ffn_in_dw(), comm.after_ffn_in_dw, _matmul, sv["x1b"], dgu, TN, "ffn_in_dw")
    dh1, dh1b, dg1, db1 = _ffn_head_bwd(dgu, wl["wgu"], dh2, sv["h1"], cl["g1"], tb)
    g["ln1_g"], g["ln1_b"] = dg1.sum(0), db1.sum(0)
    g["wo"] = _matmul(sv["m"], dh1b, TN, "wo_dw")
    dpa, dpb, dya, dyb, dprojm = _mix_bwd(dh1b, sv["pa"], sv["pb"], sv["projm"], wl["wpa"], wl["wpb"], wl["wo"], d, tb)
    g["wpa"] = _matmul(sv["ya"], dpa, TN, "wpa_dw")
    g["wpb"] = _matmul(sv["yb"], dpb, TN, "wpb_dw")
    comm.after_branch_grads(g)
    (do, dprojm, donw, dlng, dlnb, dws, dbst), got = _gate_sgu_bwd(
        dya, dyb, sv["o"], sv["projm"], cl["onw"], cl["lng"], cl["lnb"], cl["ws"], cl["bst"], dprojm, d,
        comm.with_gate_sgu_bwd())
    comm.after_gate_sgu_bwd(got)
    heads, groups = d // DN_DK, d // SGU_GROUP_DIM
    g["o_norm_w"], g["sgu_ln_g"], g["sgu_ln_b"] = donw.sum(0), dlng.sum(0), dlnb.sum(0)
    g["w_s"], g["b_s"] = dws, dbst[:, :groups].T
    (dqkv, dba, dal, ddt), got = _dn_bwd(sv["qkv"], sv["ba"], cl["alog"], cl["dtb"], do, sv["states"],
                                         sv["ycors"], d, comm.with_dn_bwd())
    comm.after_dn_bwd(got)
    g["a_log"], g["dt_bias"] = dal.sum(0)[:heads], ddt.sum(0)[:heads]
    tbc = _tile(sv["xb"].shape[0], 2 * tb, SUBLANES)
    dy, dcw = _conv_bwd_dy(sv["projm"], wl["conv"], dqkv, d, tbc)
    g["conv_w"] = dcw.sum(1)
    dprojm = _conv_bwd_dx(dy, wl["conv"], dprojm, d, tbc)
    g["wba"] = _matmul(sv["xb"], dba, TN, "proj_gates_dw")
    g["wm"] = _carry(comm.with_proj_main_dw(), comm.after_proj_main_dw, _matmul, sv["xb"], dprojm, TN,
                     "proj_main_dw", tn=MM_WIDE)
    comm.after_all_grads(g)
    dx = _carry(comm.with_proj_gates_dx(), comm.after_proj_gates_dx, _matmul, dba, wl["wba"], NT, "proj_gates_dx",
                add=dh1, coef=ALPHA)
    if next_ln is not None:
        return _matmul(dprojm, wl["wm"], NT, "proj_main_dx", add=dx, tm=MM_TILE // 3, tk=MM_WIDE, ln=next_ln), g
    dx = _carry(comm.with_proj_main_dx(), comm.after_proj_main_dx, _matmul, dprojm, wl["wm"], NT, "proj_main_dx",
                add=dx, tk=MM_WIDE)
    return dx, g


_BRANCH = ("w_pa", "w_pb", "w_o", "w_ffn_gate", "w_ffn_up", "w_ffn_down")


def _grad_shards(g, d, keys):
    heads, q4 = d // DN_DK, 4 * d
    rows = lambda a: a.reshape(N_CHIPS, -1, a.shape[1])
    out = {}
    if "w_in" in keys:
        gm, gba, wsh = g["wm"], g["wba"], 2 * d + heads // 2
        out["w_in"] = jnp.stack([gm[:, :wsh],
                                 jnp.concatenate([gm[:, wsh:q4], gba[:, :heads]], axis=1),
                                 jnp.concatenate([gba[:, LANES:LANES + heads], gm[:, q4:q4 + wsh - heads]], axis=1),
                                 gm[:, q4 + wsh - heads:]])
    if "w_pa" in keys:
        ggu = g["wgu"]
        f = ggu.shape[1] // 2
        fs = f // N_CHIPS
        out.update({
            "w_pa": rows(g["wpa"]), "w_pb": rows(g["wpb"]), "w_o": rows(g["wo"]), "w_ffn_down": rows(g["wd"]),
            "w_ffn_gate": jnp.stack([ggu[:, s * fs:(s + 1) * fs] for s in range(N_CHIPS)]),
            "w_ffn_up": jnp.stack([ggu[:, f + s * fs:f + (s + 1) * fs] for s in range(N_CHIPS)])})
    return out


def _local_step(x, target, full0, full1_of, small_w, comm0=None):
    t, d = x.shape
    tb = _tile(t, 256, SUBLANES)
    comm0 = comm0 or _NoComm()
    consts = [_layer_consts(small_w, l, d) for l in range(DEPTH)]
    x1, x1b, sv0, w0 = _layer_fwd(x, x.astype(ACT), full0, consts[0], d, tb, comm0)
    x2, _, sv1, w1 = _layer_fwd(x1, x1b, full1_of(), consts[1], d, tb, _NoComm())
    ln2_bwd, loss_parts = _loss_ln_bwd(x2, target, sv1["h2"], consts[1]["g2"], tb)
    ln2_bwd, g1 = _layer_bwd(sv1, w1, consts[1], d, tb, _NoComm(), ln2_bwd, next_ln=(sv0["h2"], consts[0]["g2"]))
    comm0.layer1_grads = g1
    grad_x, g0 = _layer_bwd(sv0, w0, consts[0], d, tb, comm0, ln2_bwd)
    return loss_parts, grad_x, [g0, g1]


def kernel(x, w_in, conv_w, a_log, dt_bias, o_norm_w, sgu_ln_g, sgu_ln_b, w_s, b_s, w_pa, w_pb, w_o, ln1_g, ln1_b, w_ffn_gate, w_ffn_up, w_ffn_down, ln2_g, ln2_b, loss_target, m_w_in, m_conv_w, m_a_log, m_dt_bias, m_o_norm_w, m_sgu_ln_g, m_sgu_ln_b, m_w_s, m_b_s, m_w_pa, m_w_pb, m_w_o, m_ln1_g, m_ln1_b, m_w_ffn_gate, m_w_ffn_up, m_w_ffn_down, m_ln2_g, m_ln2_b, v_w_in, v_conv_w, v_a_log, v_dt_bias, v_o_norm_w, v_sgu_ln_g, v_sgu_ln_b, v_w_s, v_b_s, v_w_pa, v_w_pb, v_w_o, v_ln1_g, v_ln1_b, v_w_ffn_gate, v_w_ffn_up, v_w_ffn_down, v_ln2_g, v_ln2_b):
    names = ("w_in", "conv_w", "a_log", "dt_bias", "o_norm_w", "sgu_ln_g", "sgu_ln_b", "w_s", "b_s", "w_pa",
             "w_pb", "w_o", "ln1_g", "ln1_b", "w_ffn_gate", "w_ffn_up", "w_ffn_down", "ln2_g", "ln2_b")
    w = dict(zip(names, (w_in, conv_w, a_log, dt_bias, o_norm_w, sgu_ln_g, sgu_ln_b, w_s, b_s, w_pa, w_pb, w_o,
                         ln1_g, ln1_b, w_ffn_gate, w_ffn_up, w_ffn_down, ln2_g, ln2_b)))
    mom = dict(zip(names, (m_w_in, m_conv_w, m_a_log, m_dt_bias, m_o_norm_w, m_sgu_ln_g, m_sgu_ln_b, m_w_s, m_b_s,
                           m_w_pa, m_w_pb, m_w_o, m_ln1_g, m_ln1_b, m_w_ffn_gate, m_w_ffn_up, m_w_ffn_down,
                           m_ln2_g, m_ln2_b)))
    var = dict(zip(names, (v_w_in, v_conv_w, v_a_log, v_dt_bias, v_o_norm_w, v_sgu_ln_g, v_sgu_ln_b, v_w_s, v_b_s,
                           v_w_pa, v_w_pb, v_w_o, v_ln1_g, v_ln1_b, v_w_ffn_gate, v_w_ffn_up, v_w_ffn_down,
                           v_ln2_g, v_ln2_b)))
    chip = 2 * lax.axis_index("x") + lax.axis_index("y")
    place = jnp.stack([lax.axis_index("c"), chip]).astype(jnp.int32)

    big = [k for k, _ in _BIG]
    axis_of = dict(_BIG)
    local = {k: w[k].astype(BF16) for k in big}
    local["conv_w"] = conv_w

    def gather_plan(l, keys):
        return _all_gather_plan([local[k][l] for k in keys])

    def full_of(l, keys, gathered):
        return {k: _unshard(gt, local[k][l], chip, axis_of.get(k, 2)) for k, gt in zip(keys, gathered)}

    def pair_sums(grads_l, keys, lands):
        return [_pair_sum(place, grads_l[k], land) for k, land in zip(keys, lands)]

    def chip_sums(pairs, lands):
        return [_chip_sum(place, p[1], land, "grad_chip_sum") for p, land in zip(pairs, lands)]

    class Layer0Comm(_NoComm):
        def with_proj_main(self):
            return gather_plan(0, _BRANCH)

        def after_proj_main(self, got):
            self.rest = full_of(0, _BRANCH, got)

        def weights(self, full):
            return {**full, **self.rest}

        def with_dn_fwd(self):
            return gather_plan(1, mixer)

        def after_dn_fwd(self, got):
            self.full1 = full_of(1, mixer, got)

        def with_ffn_in(self):
            return gather_plan(1, ffn)

        def after_ffn_in(self, got):
            self.full1.update(full_of(1, ffn, got))

        def with_ffn_in_dw(self):
            self.g1 = _grad_shards(self.layer1_grads, x.shape[-1], big)
            return _sibling_exchange_plan([self.g1[k] for k in big])

        def after_ffn_in_dw(self, got):
            self.pairs1 = pair_sums(self.g1, big, got)

        def with_dn_bwd(self):
            return _chip_exchange_plan([p[0] for p in self.pairs1])

        def after_dn_bwd(self, got):
            self.red1 = chip_sums(self.pairs1, got)

        def after_branch_grads(self, g0):
            self.shards0 = _grad_shards(g0, x.shape[-1], _BRANCH)

        def with_gate_sgu_bwd(self):
            return _sibling_exchange_plan([self.shards0[k] for k in _BRANCH])

        def after_gate_sgu_bwd(self, got):
            self.pairs0 = pair_sums(self.shards0, _BRANCH, got)

        def with_proj_main_dw(self):
            return _chip_exchange_plan([p[0] for p in self.pairs0])

        def after_proj_main_dw(self, got):
            self.red0 = chip_sums(self.pairs0, got)

        def after_all_grads(self, g0):
            self.g_in = _grad_shards(g0, x.shape[-1], ["w_in"])["w_in"]
            self.small_g = {k: jnp.stack([g0[k], self.layer1_grads[k]]) for k in _SMALL}
            self.small = _pack_small([self.small_g[k] for k in _SMALL])

        def with_proj_gates_dx(self):
            return _sibling_exchange_plan([self.g_in], self.small)

        def after_proj_gates_dx(self, got):
            self.pair_in = _pair_sum(place, self.g_in, got[0])
            self.small_chip = _add2(self.small, got[1])

        def with_proj_main_dx(self):
            return _join_plans(_chip_exchange_plan([self.pair_in[0]], self.small_chip),
                               _sibling_merge_plan(self.red0 + self.red1))

        def after_proj_main_dx(self, got):
            self.red_in = _chip_sum(place, self.pair_in[1], got[0], "grad_chip_sum")
            self.small_total = _chip_sum(place, self.small_chip, got[1], "grad_small_chip_sum")
            self.others = got[2:]

    comm = Layer0Comm()
    first, mixer, ffn = ["w_in", "conv_w"], ["w_in", "conv_w", "w_pa", "w_pb", "w_o"], list(_BRANCH[3:])
    full0 = full_of(0, first, _run_comm("all_gather_weights", gather_plan(0, first)))
    small_w = {k: w[k] for k in _SMALL if k != "conv_w"}
    loss_parts, grad_x, g = _local_step(x[0], loss_target[0], full0, lambda: comm.full1, small_w, comm)

    reduced = [comm.red_in] + comm.red0 + comm.red1
    others = list(_run_comm("grad_sibling_merge", _sibling_merge_plan([comm.red_in]))) + list(comm.others)
    halves = [_merge_halves(place, mine, other) for mine, other in zip(reduced, others)]
    grads = {k: jnp.stack([halves[i], halves[len(big) + i]]) for i, k in enumerate(big)}
    grads.update(zip(_SMALL, _unpack_small(comm.small_total, [comm.small_g[k] for k in _SMALL])))
    grads["conv_w"] = lax.dynamic_index_in_dim(_to_shards(grads["conv_w"], 2), chip, 1, keepdims=False)

    delta, new_m, new_v = {}, {}, {}
    for k in [k for k, _ in _BIG] + ["conv_w"]:
        delta[k], new_m[k], new_v[k] = _adamw(w[k], grads[k], mom[k], var[k])
    rep = [k for k in _SMALL if k != "conv_w"]
    pack = lambda dct: _pack_small([dct[k] for k in rep])
    packed = _adamw(pack(w), pack(grads), pack(mom), pack(var))
    for dst, src in zip((delta, new_m, new_v), packed):
        dst.update(zip(rep, _unpack_small(src, [w[k] for k in rep])))

    loss = 0.5 * lax.psum(jnp.sum(loss_parts), ("x", "y", "c")) / x.shape[-1]
    return (loss, grad_x[None], *[grads[k] for k in names], *[delta[k] for k in names],
            *[new_m[k] for k in names], *[new_v[k] for k in names])
```

```python
import math

import jax
import jax.numpy as jnp
from jax import lax
from jax.experimental import pallas as pl
from jax.experimental.pallas import tpu as pltpu

F32 = jnp.float32
BF16 = jnp.bfloat16
MXU_DTYPE = jnp.bfloat16
ACT = jnp.bfloat16
HIGHEST = lax.Precision.HIGHEST

DEPTH = 2
CHUNK = 64
DN_GROUP = 2
DN_GROUP_FWD = 4
SGU_BLOCK = 128
SGU_WINDOWS = 4
CONV_K = 4
DN_DK = 128
SGU_GROUP_DIM = 128
LN_EPS = 1e-5
RMS_EPS = 1e-6
ALPHA = (2 * DEPTH) ** 0.25
ADAM_LR, ADAM_B1, ADAM_B2, ADAM_EPS, ADAM_WD, ADAM_STEP = 0.001, 0.9, 0.999, 1e-08, 0.01, 10

LANES = 128
SUBLANES = 8
VMEM_LIMIT = 52 * 2 ** 20
N_CHIPS = 4

NN = ((1,), (0,))
NT = ((1,), (1,))
TN = ((0,), (0,))
MESH = pl.DeviceIdType.MESH
ANY = pl.BlockSpec(memory_space=pl.ANY)


def _dot(a, b, dims=NN, prec=None):
    if prec is None:
        a = a.astype(MXU_DTYPE)
        b = b.astype(MXU_DTYPE)
    return lax.dot_general(a, b, (dims, ((), ())), preferred_element_type=F32, precision=prec)


def _cparams(sem=None):
    return pltpu.CompilerParams(dimension_semantics=sem, vmem_limit_bytes=VMEM_LIMIT)


def _tile(dim, pref, unit=LANES):
    t = (min(pref, dim) // unit) * unit
    while t >= unit:
        if dim % t == 0:
            return t
        t -= unit
    return dim


def _fold8(x):
    r, n = x.shape
    return x.reshape(r // SUBLANES, SUBLANES, n).sum(axis=0)


def _sigmoid(x):
    return 1.0 / (1.0 + jnp.exp(-x))


def _gelu(x):
    return 0.5 * x * (1.0 + lax.erf(x * (2.0 ** -0.5)))


def _gelu_grad(x):
    return 0.5 * (1.0 + lax.erf(x * (2.0 ** -0.5))) + x * jnp.exp(-0.5 * x * x) * (2.0 * math.pi) ** -0.5


def _ln_hat(h):
    mu = jnp.mean(h, axis=-1, keepdims=True)
    xc = h - mu
    var = jnp.mean(xc * xc, axis=-1, keepdims=True)
    r = lax.rsqrt(var + LN_EPS)
    return xc * r, r


def _ln_bwd(dxhat, xhat, r):
    return r * (dxhat - jnp.mean(dxhat, axis=-1, keepdims=True)
                - xhat * jnp.mean(dxhat * xhat, axis=-1, keepdims=True))


MM_TILE = 1536
MM_WIDE = 2048


def _matmul(a, b, dims, name, out_dtype=F32, add=None, coef=1.0, tm=MM_TILE, tn=MM_TILE, tk=MM_TILE, carried=None,
            ln=None):
    if dims == NN:
        (m, k), n = a.shape, b.shape[1]
    elif dims == NT:
        (m, k), n = a.shape, b.shape[0]
    else:
        (k, m), n = a.shape, b.shape[1]
    tm, tn, tk = _tile(m, tm), _tile(n, tn), _tile(k, tk)
    nk = k // tk
    a_spec = pl.BlockSpec((tk, tm), lambda j, i, q: (q, i)) if dims == TN else pl.BlockSpec((tm, tk), lambda j, i, q: (i, q))
    b_spec = pl.BlockSpec((tn, tk), lambda j, i, q: (j, q)) if dims == NT else pl.BlockSpec((tk, tn), lambda j, i, q: (q, j))
    o_spec = pl.BlockSpec((tm, tn), lambda j, i, q: (i, j))
    has_add = add is not None
    if ln is not None:
        assert n == tn and has_add and carried is None
        return _matmul_ln_bwd(a, b, dims, name, add, coef, ln, a_spec, b_spec, o_spec, (m, n, tm, tn, nk))

    def body(*refs):
        a_ref, b_ref = refs[0], refs[1]
        add_ref = refs[2] if has_add else None
        o_ref, acc_ref = refs[2 + has_add], refs[3 + has_add]
        q = pl.program_id(2)
        part = _dot(a_ref[...], b_ref[...], dims)

        def finish(r):
            if has_add:
                r = r + coef * add_ref[...]
            o_ref[...] = r.astype(out_dtype)

        if nk == 1:
            finish(part)
        else:
            @pl.when(q == 0)
            def _():
                acc_ref[...] = part

            @pl.when(q > 0)
            def _():
                acc_ref[...] += part

            @pl.when(q == nk - 1)
            def _():
                finish(acc_ref[...])

    ins = [a, b] + ([add] if has_add else [])
    in_specs = [a_spec, b_spec] + ([o_spec] if has_add else [])
    grid = (n // tn, m // tm, nk)
    acc = pltpu.VMEM((tm, tn) if nk > 1 else (SUBLANES, LANES), F32)
    out = jax.ShapeDtypeStruct((m, n), out_dtype)
    if carried is None:
        return pl.pallas_call(
            body, name=name, grid=grid, in_specs=in_specs, out_specs=o_spec, out_shape=out, scratch_shapes=[acc],
            compiler_params=_cparams(("parallel", "parallel", "arbitrary")),
        )(*ins)
    res = pl.pallas_call(
        _carrying(body, len(ins), 1, 1, carried, grid), name=name + "_carrying", grid=grid,
        in_specs=in_specs + [ANY] * len(carried.inputs), out_specs=[o_spec] + [ANY] * len(carried.out_shapes),
        out_shape=[out] + carried.out_shapes, scratch_shapes=[acc] + carried.scratch(),
        compiler_params=_cparams(("arbitrary", "arbitrary", "arbitrary")),
    )(*ins, *carried.inputs)
    return res[0], res[1:]


def _matmul_ln_bwd(a, b, dims, name, add, coef, ln, a_spec, b_spec, o_spec, sizes):
    m, n, tm, tn, nk = sizes
    hres, g = ln
    row = pl.BlockSpec((1, n), lambda j, i, q: (0, 0))
    sums = pl.BlockSpec((SUBLANES, n), lambda j, i, q: (0, 0))

    def body(a_ref, b_ref, add_ref, h_ref, g_ref, dh_ref, dhb_ref, dg_ref, db_ref, acc_ref):
        i, q = pl.program_id(1), pl.program_id(2)
        part = _dot(a_ref[...], b_ref[...], dims)

        @pl.when(jnp.logical_and(i == 0, q == 0))
        def _():
            dg_ref[...] = jnp.zeros_like(dg_ref)
            db_ref[...] = jnp.zeros_like(db_ref)

        @pl.when(q == 0)
        def _():
            acc_ref[...] = part

        @pl.when(q > 0)
        def _():
            acc_ref[...] += part

        @pl.when(q == nk - 1)
        def _():
            dy_v = acc_ref[...] + coef * add_ref[...]
            xhat, r = _ln_hat(h_ref[...])
            dh = _ln_bwd(dy_v * g_ref[...], xhat, r)
            dh_ref[...] = dh
            dhb_ref[...] = dh.astype(dhb_ref.dtype)
            dg_ref[...] += _fold8(dy_v * xhat)
            db_ref[...] += _fold8(dy_v)

    return pl.pallas_call(
        body, name=name + "_ln_bwd", grid=(1, m // tm, nk),
        in_specs=[a_spec, b_spec, o_spec, o_spec, row], out_specs=[o_spec, o_spec, sums, sums],
        out_shape=[jax.ShapeDtypeStruct((m, n), F32), jax.ShapeDtypeStruct((m, n), ACT),
                   jax.ShapeDtypeStruct((SUBLANES, n), F32), jax.ShapeDtypeStruct((SUBLANES, n), F32)],
        scratch_shapes=[pltpu.VMEM((tm, tn), F32)],
        compiler_params=_cparams(("arbitrary", "arbitrary", "arbitrary")),
    )(a, b, add, hres, g)


def _conv_taps(cur_ref, halo_ref, first):
    x = cur_ref[...]
    tb = x.shape[0]
    halo = jnp.where(first, 0.0, halo_ref[...])
    xc = jnp.concatenate([halo, x], axis=0)
    return [x] + [pltpu.roll(xc, s, 0)[SUBLANES:SUBLANES + tb] for s in range(1, CONV_K)]


def _conv_fwd(projm, conv_w, d, tb):
    t = projm.shape[0]
    heads = d // DN_DK
    hb = tb // SUBLANES

    def body(cur_ref, halo_ref, w_ref, o_ref):
        i, j = pl.program_id(0), pl.program_id(1)
        taps = _conv_taps(cur_ref, halo_ref, i == 0)
        y = taps[0] * w_ref[CONV_K - 1:CONV_K, :]
        for s in range(1, CONV_K):
            y = y + taps[s] * w_ref[CONV_K - 1 - s:CONV_K - s, :]
        act = y * _sigmoid(y)
        scale = jnp.where(j == 0, DN_DK ** -0.5, 1.0)
        for h in range(heads):
            seg = act[:, h * DN_DK:(h + 1) * DN_DK]
            r = lax.rsqrt(jnp.sum(seg * seg, axis=1, keepdims=True) + RMS_EPS) * scale
            o_ref[:, h * DN_DK:(h + 1) * DN_DK] = seg * jnp.where(j < 2, r, 1.0)

    blk = pl.BlockSpec((tb, d), lambda i, j: (i, j))
    return pl.pallas_call(
        body, name="conv_fwd", grid=(t // tb, 3),
        in_specs=[blk,
                  pl.BlockSpec((SUBLANES, d), lambda i, j: (jnp.maximum(i * hb - 1, 0), j)),
                  pl.BlockSpec((CONV_K, d), lambda i, j: (0, j))],
        out_specs=blk,
        out_shape=jax.ShapeDtypeStruct((t, 3 * d), F32),
        compiler_params=_cparams(("parallel", "parallel")),
    )(projm, projm, conv_w)


def _conv_bwd_dy(projm, conv_w, dqkv, d, tb):
    t = projm.shape[0]
    heads = d // DN_DK
    hb = tb // SUBLANES

    def body(cur_ref, halo_ref, w_ref, dout_ref, dy_ref, dw_ref):
        j, i = pl.program_id(0), pl.program_id(1)
        taps = _conv_taps(cur_ref, halo_ref, i == 0)
        y = taps[0] * w_ref[CONV_K - 1:CONV_K, :]
        for s in range(1, CONV_K):
            y = y + taps[s] * w_ref[CONV_K - 1 - s:CONV_K - s, :]
        sg = _sigmoid(y)
        act = y * sg
        dact = sg * (1.0 + y * (1.0 - sg))
        scale = jnp.where(j == 0, DN_DK ** -0.5, 1.0)
        for h in range(heads):
            cols = slice(h * DN_DK, (h + 1) * DN_DK)
            seg = act[:, cols]
            r = lax.rsqrt(jnp.sum(seg * seg, axis=1, keepdims=True) + RMS_EPS)
            nrm = seg * r
            dout = dout_ref[:, cols]
            ds = jnp.where(j < 2, (r * scale) * (dout - nrm * jnp.sum(dout * nrm, axis=1, keepdims=True)), dout)
            dy_ref[:, cols] = ds * dact[:, cols]
        dy = dy_ref[...]

        @pl.when(i == 0)
        def _():
            dw_ref[...] = jnp.zeros_like(dw_ref)

        for s in range(CONV_K):
            dw_ref[CONV_K - 1 - s] += _fold8(dy * taps[s])

    return pl.pallas_call(
        body, name="conv_bwd_dy", grid=(3, t // tb),
        in_specs=[pl.BlockSpec((tb, d), lambda j, i: (i, j)),
                  pl.BlockSpec((SUBLANES, d), lambda j, i: (jnp.maximum(i * hb - 1, 0), j)),
                  pl.BlockSpec((CONV_K, d), lambda j, i: (0, j)),
                  pl.BlockSpec((tb, d), lambda j, i: (i, j))],
        out_specs=[pl.BlockSpec((tb, d), lambda j, i: (i, j)),
                   pl.BlockSpec((CONV_K, SUBLANES, d), lambda j, i: (0, 0, j))],
        out_shape=[jax.ShapeDtypeStruct((t, 3 * d), F32),
                   jax.ShapeDtypeStruct((CONV_K, SUBLANES, 3 * d), F32)],
        compiler_params=_cparams(("parallel", "arbitrary")),
    )(projm, projm, conv_w, dqkv)


def _conv_bwd_dx(dy, conv_w, dprojm, d, tb):
    t = dy.shape[0]
    hb = tb // SUBLANES
    last = t // tb - 1

    def body(cur_ref, halo_ref, w_ref, alias_ref, o_ref):
        i = pl.program_id(0)
        cur = cur_ref[...]
        halo = jnp.where(i == last, 0.0, halo_ref[...])
        dc = jnp.concatenate([cur, halo], axis=0)
        acc = cur * w_ref[CONV_K - 1:CONV_K, :]
        for s in range(1, CONV_K):
            acc = acc + pltpu.roll(dc, tb + SUBLANES - s, 0)[:tb] * w_ref[CONV_K - 1 - s:CONV_K - s, :]
        o_ref[...] = acc.astype(o_ref.dtype)

    return pl.pallas_call(
        body, name="conv_bwd_dx", grid=(t // tb, 3),
        in_specs=[pl.BlockSpec((tb, d), lambda i, j: (i, j)),
                  pl.BlockSpec((SUBLANES, d), lambda i, j: (jnp.minimum((i + 1) * hb, t // SUBLANES - 1), j)),
                  pl.BlockSpec((CONV_K, d), lambda i, j: (0, j)),
                  ANY],
        out_specs=pl.BlockSpec((tb, d), lambda i, j: (i, j)),
        out_shape=jax.ShapeDtypeStruct(dprojm.shape, dprojm.dtype),
        input_output_aliases={3: 0},
        compiler_params=_cparams(("parallel", "parallel")),
    )(dy, dy, conv_w, dprojm)


def _beta_g(ba, alog, dtb):
    beta = _sigmoid(ba[:, :LANES])
    xa = ba[:, LANES:] + dtb
    softplus = jnp.maximum(xa, 0.0) + jnp.log(1.0 + jnp.exp(-jnp.abs(xa)))
    ea = jnp.exp(alog)
    return beta, -ea * softplus, ea, _sigmoid(xa)


def _inv_corrections(mats):
    ys = [-a for a in mats]
    ps = [_dot(a, a) for a in mats]
    steps = int(math.log2(CHUNK)) - 1
    for it in range(steps):
        ys = [y + p + _dot(y, p) for y, p in zip(ys, ps)]
        if it < steps - 1:
            ps = [_dot(p, p) for p in ps]
    return ys


def _chunk_masks():
    row = lax.broadcasted_iota(jnp.int32, (CHUNK, CHUNK), 0)
    col = lax.broadcasted_iota(jnp.int32, (CHUNK, CHUNK), 1)
    return row >= col, row > col, row <= col


def _col_of(mat, lane_idx, h):
    return jnp.sum(jnp.where(lane_idx == h, mat, 0.0), axis=1, keepdims=True)


def _row_of(mat, sub_idx, h):
    return jnp.sum(jnp.where(sub_idx == h, mat, 0.0), axis=0, keepdims=True)


def _phases(fns):
    return fns if len(fns) == 3 else (fns[0], lambda: None, fns[1])


def _carrying(compute, n_in, n_out, n_scratch, carried, grid):
    if carried is None:
        return compute
    ci, co = len(carried.inputs), len(carried.out_shapes)

    def body(*refs):
        ins, c_in = refs[:n_in], refs[n_in:n_in + ci]
        outs, c_out = refs[n_in + ci:n_in + ci + n_out], refs[n_in + ci + n_out:n_in + ci + n_out + co]
        scratch = refs[n_in + ci + n_out + co:]
        start, middle, finish = _phases(carried.copies(c_in, c_out, scratch[n_scratch], scratch[n_scratch + 1]))
        step, total = 0, 1
        for axis, steps in enumerate(grid):
            step = step * steps + pl.program_id(axis)
            total *= steps

        @pl.when(step == 0)
        def _():
            start()

        compute(*ins, *outs, *scratch[:n_scratch])

        @pl.when(step == (3 * total) // 4)
        def _():
            middle()

        @pl.when(step == total - 1)
        def _():
            finish()

    return body


def _dn_fwd(qkv, ba, alog, dtb, d, carried=None):
    t = qkv.shape[0]
    heads = d // DN_DK
    n_chunks = t // CHUNK
    grp = DN_GROUP_FWD if n_chunks % DN_GROUP_FWD == 0 else 1
    span = grp * CHUNK
    extra = carried or _Carried([], [], 0, None)

    def compute(qkv_ref, ba_ref, al_ref, dt_ref, o_ref, s_ref, y_ref, state):
        @pl.when(pl.program_id(0) == 0)
        def _():
            state[...] = jnp.zeros_like(state)

        tril, strict, _ = _chunk_masks()
        beta, g, _, _ = _beta_g(ba_ref[...], al_ref[...], dt_ref[...])
        lane = lax.broadcasted_iota(jnp.int32, (CHUNK, LANES), 1)
        sub = lax.broadcasted_iota(jnp.int32, (LANES, CHUNK), 0)
        rowc = lax.broadcasted_iota(jnp.int32, (CHUNK, 1), 0)
        hs = range(heads)
        units = [(c, h) for c in range(grp) for h in hs]
        un = range(len(units))
        rows = lambda c: slice(c * CHUNK, (c + 1) * CHUNK)
        gc = [_dot(jnp.where(tril, 1.0, 0.0), g[rows(c)], NN, HIGHEST) for c in range(grp)]
        gct = [m.T for m in gc]
        q = [qkv_ref[rows(c), h * DN_DK:(h + 1) * DN_DK] for c, h in units]
        k = [qkv_ref[rows(c), d + h * DN_DK:d + (h + 1) * DN_DK] for c, h in units]
        v = [qkv_ref[rows(c), 2 * d + h * DN_DK:2 * d + (h + 1) * DN_DK] for c, h in units]
        gch = [_col_of(gc[c], lane, h) for c, h in units]
        bh = [_col_of(beta[rows(c)], lane, h) for c, h in units]
        dec = [jnp.where(tril, jnp.exp(gch[n] - _row_of(gct[c], sub, h)), 0.0) for n, (c, h) in enumerate(units)]
        egc = [jnp.exp(gch[n]) for n in un]
        gl = [jnp.sum(jnp.where(rowc == CHUNK - 1, gch[n], 0.0), axis=0, keepdims=True) for n in un]
        kb = [k[n] * bh[n] for n in un]
        a = [jnp.where(strict, _dot(kb[n], k[n], NT) * dec[n], 0.0) for n in un]
        p = [_dot(q[n], k[n], NT) * dec[n] for n in un]
        ycor = _inv_corrections(a)
        rhs = [jnp.concatenate([v[n] * bh[n], kb[n] * egc[n]], axis=1) for n in un]
        sol = [rhs[n] + _dot(ycor[n], rhs[n]) for n in un]
        qg = [q[n] * egc[n] for n in un]
        kd = [k[n] * jnp.exp(gl[n] - gch[n]) for n in un]
        egl = [jnp.exp(gl[n]) for n in un]
        s_cur, s_in, o = [state[h] for h in hs], [], []
        for c in range(grp):
            ns = [c * heads + h for h in hs]
            vn = [sol[n][:, :DN_DK] - _dot(sol[n][:, DN_DK:], s_cur[h]) for h, n in enumerate(ns)]
            o += [_dot(qg[n], s_cur[h]) + _dot(p[n], vn[h]) for h, n in enumerate(ns)]
            s_in += s_cur
            s_cur = [s_cur[h] * egl[n] + _dot(kd[n], vn[h], TN) for h, n in enumerate(ns)]
        for n, (c, h) in enumerate(units):
            o_ref[rows(c), h * DN_DK:(h + 1) * DN_DK] = o[n]
            s_ref[c, h] = s_in[n]
            y_ref[h, rows(c), :] = ycor[n]
        for h in hs:
            state[h] = s_cur[h]

    res = pl.pallas_call(
        _carrying(compute, 4, 3, 1, carried, (n_chunks // grp,)),
        name="dn_fwd_carrying" if carried else "dn_fwd", grid=(n_chunks // grp,),
        in_specs=[pl.BlockSpec((span, 3 * d), lambda i: (i, 0)),
                  pl.BlockSpec((span, 2 * LANES), lambda i: (i, 0)),
                  pl.BlockSpec((1, LANES), lambda i: (0, 0)),
                  pl.BlockSpec((1, LANES), lambda i: (0, 0))] + [ANY] * len(extra.inputs),
        out_specs=[pl.BlockSpec((span, d), lambda i: (i, 0)),
                   pl.BlockSpec((grp, heads, DN_DK, DN_DK), lambda i: (i, 0, 0, 0)),
                   pl.BlockSpec((heads, span, CHUNK), lambda i: (0, i, 0))] + [ANY] * len(extra.out_shapes),
        out_shape=[jax.ShapeDtypeStruct((t, d), F32),
                   jax.ShapeDtypeStruct((n_chunks, heads, DN_DK, DN_DK), F32),
                   jax.ShapeDtypeStruct((heads, t, CHUNK), F32)] + extra.out_shapes,
        scratch_shapes=[pltpu.VMEM((heads, DN_DK, DN_DK), F32)] + (extra.scratch() if carried else []),
        compiler_params=_cparams(("arbitrary",)),
    )(qkv, ba, alog, dtb, *extra.inputs)
    return res[:3], res[3:]


def _dn_bwd(qkv, ba, alog, dtb, dout, states, ycors, d, carried=None):
    t = qkv.shape[0]
    heads = d // DN_DK
    n_chunks = t // CHUNK
    grp = DN_GROUP if n_chunks % DN_GROUP == 0 else 1
    span = grp * CHUNK
    rev = lambda i: n_chunks // grp - 1 - i
    extra = carried or _Carried([], [], 0, None)

    def compute(qkv_ref, ba_ref, al_ref, dt_ref, do_ref, s_ref, y_ref,
                dqkv_ref, dba_ref, dal_ref, ddt_ref, dstate):
        @pl.when(pl.program_id(0) == 0)
        def _():
            dstate[...] = jnp.zeros_like(dstate)
            dal_ref[...] = jnp.zeros_like(dal_ref)
            ddt_ref[...] = jnp.zeros_like(ddt_ref)

        tril, strict, triu = _chunk_masks()
        beta, g, ea, sig_a = _beta_g(ba_ref[...], al_ref[...], dt_ref[...])
        lane = lax.broadcasted_iota(jnp.int32, (CHUNK, LANES), 1)
        sub = lax.broadcasted_iota(jnp.int32, (LANES, CHUNK), 0)
        rowc = lax.broadcasted_iota(jnp.int32, (CHUNK, 1), 0)
        hs = range(heads)
        units = [(c, h) for c in range(grp) for h in hs]
        un = range(len(units))
        rows = lambda c: slice(c * CHUNK, (c + 1) * CHUNK)
        rsum = lambda x_: jnp.sum(x_, axis=1, keepdims=True)
        gc = [_dot(jnp.where(tril, 1.0, 0.0), g[rows(c)], NN, HIGHEST) for c in range(grp)]
        gct = [m.T for m in gc]
        q = [qkv_ref[rows(c), h * DN_DK:(h + 1) * DN_DK] for c, h in units]
        k = [qkv_ref[rows(c), d + h * DN_DK:d + (h + 1) * DN_DK] for c, h in units]
        v = [qkv_ref[rows(c), 2 * d + h * DN_DK:2 * d + (h + 1) * DN_DK] for c, h in units]
        dout_h = [do_ref[rows(c), h * DN_DK:(h + 1) * DN_DK] for c, h in units]
        s0 = [s_ref[c, h] for c, h in units]
        ycor = [y_ref[h, rows(c), :] for c, h in units]
        gch = [_col_of(gc[c], lane, h) for c, h in units]
        bh = [_col_of(beta[rows(c)], lane, h) for c, h in units]
        dec = [jnp.where(tril, jnp.exp(gch[n] - _row_of(gct[c], sub, h)), 0.0) for n, (c, h) in enumerate(units)]
        egc = [jnp.exp(gch[n]) for n in un]
        gl = [jnp.sum(jnp.where(rowc == CHUNK - 1, gch[n], 0.0), axis=0, keepdims=True) for n in un]
        egl = [jnp.exp(gl[n]) for n in un]
        ekd = [jnp.exp(gl[n] - gch[n]) for n in un]
        kb = [k[n] * bh[n] for n in un]
        kd = [k[n] * ekd[n] for n in un]
        qg = [q[n] * egc[n] for n in un]
        kbg = [kb[n] * egc[n] for n in un]
        a = [jnp.where(strict, _dot(kb[n], k[n], NT) * dec[n], 0.0) for n in un]
        p = [_dot(q[n], k[n], NT) * dec[n] for n in un]
        rhs = [jnp.concatenate([v[n] * bh[n], kbg[n]], axis=1) for n in un]
        sol = [rhs[n] + _dot(ycor[n], rhs[n]) for n in un]
        w = [sol[n][:, DN_DK:] for n in un]
        vn = [sol[n][:, :DN_DK] - _dot(w[n], s0[n]) for n in un]
        dqg = [_dot(dout_h[n], s0[n], NT) for n in un]
        dp = [jnp.where(tril, _dot(dout_h[n], vn[n], NT), 0.0) for n in un]
        pdo = [_dot(p[n], dout_h[n], TN) for n in un]
        qdo = [_dot(qg[n], dout_h[n], TN) for n in un]
        ds_cur = [dstate[h] for h in hs]
        dsn, dvn = [None] * len(units), [None] * len(units)
        for c in reversed(range(grp)):
            for h in hs:
                dsn[c * heads + h] = ds_cur[h]
            for h in hs:
                n = c * heads + h
                dvn[n] = pdo[n] + _dot(kd[n], ds_cur[h])
            ds_cur = [qdo[c * heads + h] + egl[c * heads + h] * ds_cur[h]
                      - _dot(w[c * heads + h], dvn[c * heads + h], TN) for h in hs]
        dkd = [_dot(vn[n], dsn[n], NT) for n in un]
        dw = [-_dot(dvn[n], s0[n], NT) for n in un]
        dgl = [jnp.sum(rsum(dsn[n] * s0[n]), axis=0, keepdims=True) * egl[n] for n in un]
        dsol = [jnp.concatenate([dvn[n], dw[n]], axis=1) for n in un]
        drhs = [dsol[n] + _dot(ycor[n], dsol[n], TN) for n in un]
        dvb = [drhs[n][:, :DN_DK] for n in un]
        dkbg = [drhs[n][:, DN_DK:] for n in un]
        da = [jnp.where(strict, -_dot(drhs[n], sol[n], NT), 0.0) for n in un]
        dma = [da[n] * dec[n] for n in un]
        dmp = [dp[n] * dec[n] for n in un]
        dkb = [_dot(dma[n], k[n]) + dkbg[n] * egc[n] for n in un]
        dq = [_dot(dmp[n], k[n]) + dqg[n] * egc[n] for n in un]
        dk = [_dot(dma[n], kb[n], TN) + _dot(dmp[n], q[n], TN) + dkd[n] * ekd[n] + dkb[n] * bh[n] for n in un]
        e = [da[n] * a[n] + dp[n] * p[n] for n in un]
        colsum = [jnp.sum(e[n], axis=0, keepdims=True) for n in un]
        tkd = [rsum(dkd[n] * kd[n]) for n in un]
        for n, (c, h) in enumerate(units):
            dqkv_ref[rows(c), h * DN_DK:(h + 1) * DN_DK] = dq[n]
            dqkv_ref[rows(c), d + h * DN_DK:d + (h + 1) * DN_DK] = dk[n]
            dqkv_ref[rows(c), 2 * d + h * DN_DK:2 * d + (h + 1) * DN_DK] = dvb[n] * bh[n]
        for h in hs:
            dstate[h] = ds_cur[h]
        valid = lane < heads
        dal_acc = jnp.zeros((SUBLANES, LANES), F32)
        ddt_acc = jnp.zeros((SUBLANES, LANES), F32)
        for c in range(grp):
            dgc_all = jnp.zeros((CHUNK, LANES), F32)
            dbeta_all = jnp.zeros((CHUNK, LANES), F32)
            colsums = jnp.zeros((LANES, CHUNK), F32)
            for h in hs:
                n = c * heads + h
                dgc = rsum(e[n]) + rsum(dqg[n] * qg[n]) - tkd[n] + rsum(dkbg[n] * kbg[n])
                dgc = dgc + jnp.where(rowc == CHUNK - 1, dgl[n] + jnp.sum(tkd[n], axis=0, keepdims=True), 0.0)
                dgc_all = dgc_all + jnp.where(lane == h, dgc, 0.0)
                colsums = colsums + jnp.where(sub == h, colsum[n], 0.0)
                dbeta_all = dbeta_all + jnp.where(lane == h, rsum(dkb[n] * k[n]) + rsum(dvb[n] * v[n]), 0.0)
            dg = _dot(jnp.where(triu, 1.0, 0.0), dgc_all - colsums.T, NN, HIGHEST)
            beta_c = beta[rows(c)]
            dbl = jnp.where(valid, dbeta_all * beta_c * (1.0 - beta_c), 0.0)
            dal = jnp.where(valid, -dg * ea * sig_a[rows(c)], 0.0)
            dba_ref[rows(c), :LANES] = dbl.astype(dba_ref.dtype)
            dba_ref[rows(c), LANES:] = dal.astype(dba_ref.dtype)
            dal_acc = dal_acc + _fold8(jnp.where(valid, dg * g[rows(c)], 0.0))
            ddt_acc = ddt_acc + _fold8(dal)
        dal_ref[...] += dal_acc
        ddt_ref[...] += ddt_acc

    res = pl.pallas_call(
        _carrying(compute, 7, 4, 1, carried, (n_chunks // grp,)),
        name="dn_bwd_carrying" if carried else "dn_bwd", grid=(n_chunks // grp,),
        in_specs=[pl.BlockSpec((span, 3 * d), lambda i: (rev(i), 0)),
                  pl.BlockSpec((span, 2 * LANES), lambda i: (rev(i), 0)),
                  pl.BlockSpec((1, LANES), lambda i: (0, 0)),
                  pl.BlockSpec((1, LANES), lambda i: (0, 0)),
                  pl.BlockSpec((span, d), lambda i: (rev(i), 0)),
                  pl.BlockSpec((grp, heads, DN_DK, DN_DK), lambda i: (rev(i), 0, 0, 0)),
                  pl.BlockSpec((heads, span, CHUNK), lambda i: (0, rev(i), 0))] + [ANY] * len(extra.inputs),
        out_specs=[pl.BlockSpec((span, 3 * d), lambda i: (rev(i), 0)),
                   pl.BlockSpec((span, 2 * LANES), lambda i: (rev(i), 0)),
                   pl.BlockSpec((SUBLANES, LANES), lambda i: (0, 0)),
                   pl.BlockSpec((SUBLANES, LANES), lambda i: (0, 0))] + [ANY] * len(extra.out_shapes),
        out_shape=[jax.ShapeDtypeStruct((t, 3 * d), F32),
                   jax.ShapeDtypeStruct((t, 2 * LANES), ACT),
                   jax.ShapeDtypeStruct((SUBLANES, LANES), F32),
                   jax.ShapeDtypeStruct((SUBLANES, LANES), F32)] + extra.out_shapes,
        scratch_shapes=[pltpu.VMEM((heads, DN_DK, DN_DK), F32)] + (extra.scratch() if carried else []),
        compiler_params=_cparams(("arbitrary",)),
    )(qkv, ba, alog, dtb, dout, states, ycors, *extra.inputs)
    return res[:4], res[4:]


def _sgu_mask():
    row = lax.broadcasted_iota(jnp.int32, (SGU_BLOCK, SGU_BLOCK), 0)
    col = lax.broadcasted_iota(jnp.int32, (SGU_BLOCK, SGU_BLOCK), 1)
    sh = int(math.log2(CHUNK))
    return lax.shift_right_logical(row, sh) >= lax.shift_right_logical(col, sh)


def _gate_sgu_fwd(o, projm, onw, lng, lnb, ws, bst, d):
    t = o.shape[0]
    heads, groups = d // DN_DK, d // SGU_GROUP_DIM
    tb = _tile(t, SGU_WINDOWS * SGU_BLOCK, SGU_BLOCK)
    row_spec = pl.BlockSpec((1, d), lambda i: (0, 0))

    def body(o_ref, z_ref, u_ref, v_ref, onw_ref, lng_ref, lnb_ref, ws_ref, bst_ref, ya_ref, yb_ref):
        for h in range(heads):
            cols = slice(h * DN_DK, (h + 1) * DN_DK)
            oh, zh = o_ref[:, cols], z_ref[:, cols]
            r = lax.rsqrt(jnp.mean(oh * oh, axis=1, keepdims=True) + RMS_EPS)
            ya_ref[:, cols] = (oh * r * onw_ref[:, cols] * (zh * _sigmoid(zh))).astype(ya_ref.dtype)
        xhat, _ = _ln_hat(_gelu(v_ref[...]))
        vgn = xhat * lng_ref[...] + lnb_ref[...]
        mask = _sgu_mask()
        lane = lax.broadcasted_iota(jnp.int32, (SGU_BLOCK, LANES), 1)
        bst_v = bst_ref[...]
        for gi in range(groups):
            cols = slice(gi * SGU_GROUP_DIM, (gi + 1) * SGU_GROUP_DIM)
            wsg = jnp.where(mask, ws_ref[gi], 0.0)
            bias = _col_of(bst_v, lane, gi)
            for win in range(tb // SGU_BLOCK):
                rows = slice(win * SGU_BLOCK, (win + 1) * SGU_BLOCK)
                sp = _dot(wsg, vgn[rows, cols]) + bias
                yb_ref[rows, cols] = (_gelu(u_ref[rows, cols]) * sp).astype(yb_ref.dtype)

    return pl.pallas_call(
        body, name="gate_sgu_fwd", grid=(t // tb,),
        in_specs=[pl.BlockSpec((tb, d), lambda i: (i, 0)),
                  pl.BlockSpec((tb, d), lambda i: (i, 3)),
                  pl.BlockSpec((tb, d), lambda i: (i, 4)),
                  pl.BlockSpec((tb, d), lambda i: (i, 5)),
                  row_spec, row_spec, row_spec,
                  pl.BlockSpec((groups, SGU_BLOCK, SGU_BLOCK), lambda i: (0, 0, 0)),
                  pl.BlockSpec((SGU_BLOCK, LANES), lambda i: (0, 0))],
        out_specs=[pl.BlockSpec((tb, d), lambda i: (i, 0)), pl.BlockSpec((tb, d), lambda i: (i, 0))],
        out_shape=[jax.ShapeDtypeStruct((t, d), ACT), jax.ShapeDtypeStruct((t, d), ACT)],
        compiler_params=_cparams(("parallel",)),
    )(o, projm, projm, projm, onw, lng, lnb, ws, bst)


def _gate_sgu_bwd(dya, dyb, o, projm, onw, lng, lnb, ws, bst, dprojm, d, carried=None):
    t = o.shape[0]
    heads, groups = d // DN_DK, d // SGU_GROUP_DIM
    tb = _tile(t, SGU_WINDOWS * SGU_BLOCK, SGU_BLOCK)
    extra = carried or _Carried([], [], 0, None)
    row_spec = pl.BlockSpec((1, d), lambda i: (0, 0))
    acc_row = pl.BlockSpec((SUBLANES, d), lambda i: (0, 0))

    def body(dya_ref, dyb_ref, o_ref, z_ref, u_ref, v_ref, onw_ref, lng_ref, lnb_ref, ws_ref, bst_ref, alias_ref,
             do_ref, dp_ref, donw_ref, dlng_ref, dlnb_ref, dws_ref, dbst_ref):
        @pl.when(pl.program_id(0) == 0)
        def _():
            for r_ in (donw_ref, dlng_ref, dlnb_ref, dws_ref, dbst_ref):
                r_[...] = jnp.zeros_like(r_)

        donw = jnp.zeros((SUBLANES, DN_DK), F32)
        for h in range(heads):
            cols = slice(h * DN_DK, (h + 1) * DN_DK)
            oh, zh, dyah, wh = o_ref[:, cols], z_ref[:, cols], dya_ref[:, cols], onw_ref[:, cols]
            r = lax.rsqrt(jnp.mean(oh * oh, axis=1, keepdims=True) + RMS_EPS)
            on = oh * r
            sz = _sigmoid(zh)
            silu_z = zh * sz
            don = dyah * wh * silu_z
            dp_ref[:, cols] = (dyah * on * wh * (sz * (1.0 + zh * (1.0 - sz)))).astype(dp_ref.dtype)
            donw = donw + _fold8(dyah * on * silu_z)
            do_ref[:, cols] = r * (don - on * jnp.mean(don * on, axis=1, keepdims=True))
        donw_ref[...] += donw

        vgp, up = v_ref[...], u_ref[...]
        xhat, rstd = _ln_hat(_gelu(vgp))
        lng_v = lng_ref[...]
        vgn = xhat * lng_v + lnb_ref[...]
        ua = _gelu(up)
        mask = _sgu_mask()
        lane = lax.broadcasted_iota(jnp.int32, (SGU_BLOCK, LANES), 1)
        bst_v = bst_ref[...]
        dbst = jnp.zeros((SGU_BLOCK, LANES), F32)
        dvgn_parts, dua_parts = [], []
        for gi in range(groups):
            cols = slice(gi * SGU_GROUP_DIM, (gi + 1) * SGU_GROUP_DIM)
            wsg = jnp.where(mask, ws_ref[gi], 0.0)
            bias = _col_of(bst_v, lane, gi)
            dws = jnp.zeros((SGU_BLOCK, SGU_BLOCK), F32)
            dvgn_g, dua_g = [], []
            for win in range(tb // SGU_BLOCK):
                rows = slice(win * SGU_BLOCK, (win + 1) * SGU_BLOCK)
                vg_g, dyb_g = vgn[rows, cols], dyb_ref[rows, cols]
                sp = _dot(wsg, vg_g) + bias
                dsp = dyb_g * ua[rows, cols]
                dua_g.append(dyb_g * sp)
                dws = dws + _dot(dsp, vg_g, NT)
                dbst = dbst + jnp.where(lane == gi, jnp.sum(dsp, axis=1, keepdims=True), 0.0)
                dvgn_g.append(_dot(wsg, dsp, TN))
            dws_ref[gi] += jnp.where(mask, dws, 0.0)
            dvgn_parts.append(jnp.concatenate(dvgn_g, axis=0))
            dua_parts.append(jnp.concatenate(dua_g, axis=0))
        dbst_ref[...] += dbst
        dvgn = jnp.concatenate(dvgn_parts, axis=1)
        dua = jnp.concatenate(dua_parts, axis=1)
        dlng_ref[...] += _fold8(dvgn * xhat)
        dlnb_ref[...] += _fold8(dvgn)
        dvga = _ln_bwd(dvgn * lng_v, xhat, rstd)
        dp_ref[:, d:2 * d] = (dua * _gelu_grad(up)).astype(dp_ref.dtype)
        dp_ref[:, 2 * d:] = (dvga * _gelu_grad(vgp)).astype(dp_ref.dtype)

    res = pl.pallas_call(
        _carrying(body, 12, 7, 0, carried, (t // tb,)),
        name="gate_sgu_bwd_carrying" if carried else "gate_sgu_bwd", grid=(t // tb,),
        in_specs=[pl.BlockSpec((tb, d), lambda i: (i, 0)),
                  pl.BlockSpec((tb, d), lambda i: (i, 0)),
                  pl.BlockSpec((tb, d), lambda i: (i, 0)),
                  pl.BlockSpec((tb, d), lambda i: (i, 3)),
                  pl.BlockSpec((tb, d), lambda i: (i, 4)),
                  pl.BlockSpec((tb, d), lambda i: (i, 5)),
                  row_spec, row_spec, row_spec,
                  pl.BlockSpec((groups, SGU_BLOCK, SGU_BLOCK), lambda i: (0, 0, 0)),
                  pl.BlockSpec((SGU_BLOCK, LANES), lambda i: (0, 0)),
                  ANY] + [ANY] * len(extra.inputs),
        out_specs=[pl.BlockSpec((tb, d), lambda i: (i, 0)),
                   pl.BlockSpec((tb, 3 * d), lambda i: (i, 1)),
                   pl.BlockSpec((SUBLANES, DN_DK), lambda i: (0, 0)),
                   acc_row, acc_row,
                   pl.BlockSpec((groups, SGU_BLOCK, SGU_BLOCK), lambda i: (0, 0, 0)),
                   pl.BlockSpec((SGU_BLOCK, LANES), lambda i: (0, 0))] + [ANY] * len(extra.out_shapes),
        out_shape=[jax.ShapeDtypeStruct((t, d), F32),
                   jax.ShapeDtypeStruct(dprojm.shape, dprojm.dtype),
                   jax.ShapeDtypeStruct((SUBLANES, DN_DK), F32),
                   jax.ShapeDtypeStruct((SUBLANES, d), F32),
                   jax.ShapeDtypeStruct((SUBLANES, d), F32),
                   jax.ShapeDtypeStruct((groups, SGU_BLOCK, SGU_BLOCK), F32),
                   jax.ShapeDtypeStruct((SGU_BLOCK, LANES), F32)] + extra.out_shapes,
        input_output_aliases={11: 1},
        scratch_shapes=extra.scratch() if carried else [],
        compiler_params=_cparams(("arbitrary",)),
    )(dya, dyb, o, projm, projm, projm, onw, lng, lnb, ws, bst, dprojm, *extra.inputs)
    return res[:7], res[7:]


def _mix_fwd(ya, yb, projm, x, wpa, wpb, wo, g1, b1, d, tb):
    t = x.shape[0]
    blk = pl.BlockSpec((tb, d), lambda i: (i, 0))
    wspec = pl.BlockSpec((d, d), lambda i: (0, 0))
    row_spec = pl.BlockSpec((1, d), lambda i: (0, 0))

    def body(ya_ref, yb_ref, ga_ref, gb_ref, x_ref, wpa_ref, wpb_ref, wo_ref, g_ref, b_ref,
             pa_ref, pb_ref, m_ref, h_ref, x1_ref, x1b_ref):
        pa = _dot(ya_ref[...], wpa_ref[...])
        pb = _dot(yb_ref[...], wpb_ref[...])
        m = _sigmoid(ga_ref[...]) * pa + _sigmoid(gb_ref[...]) * pb
        hres = ALPHA * x_ref[...] + _dot(m, wo_ref[...])
        xhat, _ = _ln_hat(hres)
        x1 = xhat * g_ref[...] + b_ref[...]
        pa_ref[...] = pa.astype(pa_ref.dtype)
        pb_ref[...] = pb.astype(pb_ref.dtype)
        m_ref[...] = m.astype(m_ref.dtype)
        h_ref[...] = hres
        x1_ref[...] = x1
        x1b_ref[...] = x1.astype(x1b_ref.dtype)

    f32_out = jax.ShapeDtypeStruct((t, d), F32)
    bf_out = jax.ShapeDtypeStruct((t, d), ACT)
    return pl.pallas_call(
        body, name="mix_fwd", grid=(t // tb,),
        in_specs=[blk, blk, pl.BlockSpec((tb, d), lambda i: (i, 6)), pl.BlockSpec((tb, d), lambda i: (i, 7)),
                  blk, wspec, wspec, wspec, row_spec, row_spec],
        out_specs=[blk] * 6,
        out_shape=[bf_out, bf_out, bf_out, f32_out, f32_out, bf_out],
        compiler_params=_cparams(("parallel",)),
    )(ya, yb, projm, projm, x, wpa, wpb, wo, g1, b1)


def _mix_bwd(dmix, pa, pb, projm, wpa, wpb, wo, d, tb):
    t = dmix.shape[0]
    blk = pl.BlockSpec((tb, d), lambda i: (i, 0))
    wspec = pl.BlockSpec((d, d), lambda i: (0, 0))

    def body(dmix_ref, pa_ref, pb_ref, ga_ref, gb_ref, wpa_ref, wpb_ref, wo_ref,
             dpa_ref, dpb_ref, dya_ref, dyb_ref, dg_ref):
        dm = _dot(dmix_ref[...], wo_ref[...], NT)
        sa, sb = _sigmoid(ga_ref[...]), _sigmoid(gb_ref[...])
        dpa, dpb = dm * sa, dm * sb
        dpa_ref[...] = dpa.astype(dpa_ref.dtype)
        dpb_ref[...] = dpb.astype(dpb_ref.dtype)
        dg_ref[:, :d] = (dm * pa_ref[...].astype(F32) * sa * (1.0 - sa)).astype(dg_ref.dtype)
        dg_ref[:, d:] = (dm * pb_ref[...].astype(F32) * sb * (1.0 - sb)).astype(dg_ref.dtype)
        dya_ref[...] = _dot(dpa, wpa_ref[...], NT)
        dyb_ref[...] = _dot(dpb, wpb_ref[...], NT)

    return pl.pallas_call(
        body, name="mix_bwd", grid=(t // tb,),
        in_specs=[blk, blk, blk, pl.BlockSpec((tb, d), lambda i: (i, 6)), pl.BlockSpec((tb, d), lambda i: (i, 7)),
                  wspec, wspec, wspec],
        out_specs=[blk, blk, blk, blk, pl.BlockSpec((tb, 2 * d), lambda i: (i, 3))],
        out_shape=[jax.ShapeDtypeStruct((t, d), ACT), jax.ShapeDtypeStruct((t, d), ACT),
                   jax.ShapeDtypeStruct((t, d), F32), jax.ShapeDtypeStruct((t, d), F32),
                   jax.ShapeDtypeStruct((t, 8 * d), ACT)],
        compiler_params=_cparams(("parallel",)),
    )(dmix, pa, pb, projm, projm, wpa, wpb, wo)


def _ffn_tail_fwd(gu, wd, x1, g, b, tb):
    t, d = x1.shape
    f = wd.shape[0]
    fc = _tile(f, MM_TILE)
    blk = pl.BlockSpec((tb, d), lambda i: (i, 0))
    row_spec = pl.BlockSpec((1, d), lambda i: (0, 0))

    def body(gu_ref, wd_ref, x_ref, g_ref, b_ref, a_ref, h_ref, y_ref, yb_ref):
        ffn = jnp.zeros((tb, d), F32)
        for c in range(f // fc):
            gp = gu_ref[:, c * fc:(c + 1) * fc].astype(F32)
            act = (gp * _sigmoid(gp) * gu_ref[:, f + c * fc:f + (c + 1) * fc].astype(F32)).astype(a_ref.dtype)
            a_ref[:, c * fc:(c + 1) * fc] = act
            ffn = ffn + _dot(act, wd_ref[c * fc:(c + 1) * fc, :])
        hres = ALPHA * x_ref[...] + ffn
        xhat, _ = _ln_hat(hres)
        y = xhat * g_ref[...] + b_ref[...]
        h_ref[...] = hres
        y_ref[...] = y
        yb_ref[...] = y.astype(yb_ref.dtype)

    return pl.pallas_call(
        body, name="ffn_tail_fwd", grid=(t // tb,),
        in_specs=[pl.BlockSpec((tb, 2 * f), lambda i: (i, 0)), pl.BlockSpec((f, d), lambda i: (0, 0)),
                  blk, row_spec, row_spec],
        out_specs=[pl.BlockSpec((tb, f), lambda i: (i, 0)), blk, blk, blk],
        out_shape=[jax.ShapeDtypeStruct((t, f), ACT), jax.ShapeDtypeStruct((t, d), F32),
                   jax.ShapeDtypeStruct((t, d), F32), jax.ShapeDtypeStruct((t, d), ACT)],
        compiler_params=_cparams(("parallel",)),
    )(gu, wd, x1, g, b)


def _ffn_tail_bwd(dh, wd, gu, tb):
    t, d = dh.shape
    f = wd.shape[0]
    fc = _tile(f, MM_TILE)

    def body(dh_ref, wd_ref, gu_ref, dgu_ref):
        dh_v = dh_ref[...]
        for c in range(f // fc):
            da = _dot(dh_v, wd_ref[c * fc:(c + 1) * fc, :], NT)
            gp = gu_ref[:, c * fc:(c + 1) * fc].astype(F32)
            sg = _sigmoid(gp)
            dgu_ref[:, c * fc:(c + 1) * fc] = (
                da * gu_ref[:, f + c * fc:f + (c + 1) * fc].astype(F32) * sg * (1.0 + gp * (1.0 - sg))
            ).astype(dgu_ref.dtype)
            dgu_ref[:, f + c * fc:f + (c + 1) * fc] = (da * gp * sg).astype(dgu_ref.dtype)

    return pl.pallas_call(
        body, name="ffn_tail_bwd", grid=(t // tb,),
        in_specs=[pl.BlockSpec((tb, d), lambda i: (i, 0)), pl.BlockSpec((f, d), lambda i: (0, 0)),
                  pl.BlockSpec((tb, 2 * f), lambda i: (i, 0))],
        out_specs=pl.BlockSpec((tb, 2 * f), lambda i: (i, 0)),
        out_shape=jax.ShapeDtypeStruct((t, 2 * f), ACT),
        compiler_params=_cparams(("parallel",)),
    )(dh, wd, gu)


def _ffn_head_bwd(dgu, wgu, dh2, hres, g, tb):
    t, d = dh2.shape
    f2 = wgu.shape[1]
    blk = pl.BlockSpec((tb, d), lambda i: (i, 0))
    acc = pl.BlockSpec((SUBLANES, d), lambda i: (0, 0))

    def body(dgu_ref, w_ref, dh2_ref, h_ref, g_ref, dh_ref, dhb_ref, dg_ref, db_ref):
        @pl.when(pl.program_id(0) == 0)
        def _():
            dg_ref[...] = jnp.zeros_like(dg_ref)
            db_ref[...] = jnp.zeros_like(db_ref)

        dy_v = _dot(dgu_ref[...], w_ref[...], NT) + ALPHA * dh2_ref[...]
        xhat, r = _ln_hat(h_ref[...])
        dh = _ln_bwd(dy_v * g_ref[...], xhat, r)
        dh_ref[...] = dh
        dhb_ref[...] = dh.astype(dhb_ref.dtype)
        dg_ref[...] += _fold8(dy_v * xhat)
        db_ref[...] += _fold8(dy_v)

    return pl.pallas_call(
        body, name="ffn_head_bwd", grid=(t // tb,),
        in_specs=[pl.BlockSpec((tb, f2), lambda i: (i, 0)), pl.BlockSpec((d, f2), lambda i: (0, 0)),
                  blk, blk, pl.BlockSpec((1, d), lambda i: (0, 0))],
        out_specs=[blk, blk, acc, acc],
        out_shape=[jax.ShapeDtypeStruct((t, d), F32), jax.ShapeDtypeStruct((t, d), ACT),
                   jax.ShapeDtypeStruct((SUBLANES, d), F32), jax.ShapeDtypeStruct((SUBLANES, d), F32)],
        compiler_params=_cparams(("arbitrary",)),
    )(dgu, wgu, dh2, hres, g)


def _loss_ln_bwd(y, target, hres, g, tb):
    t, d = y.shape
    blk = pl.BlockSpec((tb, d), lambda i: (i, 0))
    acc = pl.BlockSpec((SUBLANES, d), lambda i: (0, 0))

    def body(y_ref, t_ref, h_ref, g_ref, dh_ref, dhb_ref, dg_ref, db_ref, l_ref):
        @pl.when(pl.program_id(0) == 0)
        def _():
            for r_ in (dg_ref, db_ref, l_ref):
                r_[...] = jnp.zeros_like(r_)

        err = y_ref[...] - t_ref[...]
        dy_v = err * (1.0 / d)
        sq = _fold8(err * err)
        part = sq[:, :LANES]
        for c in range(1, d // LANES):
            part = part + sq[:, c * LANES:(c + 1) * LANES]
        l_ref[...] += part
        xhat, r = _ln_hat(h_ref[...])
        dh = _ln_bwd(dy_v * g_ref[...], xhat, r)
        dh_ref[...] = dh
        dhb_ref[...] = dh.astype(dhb_ref.dtype)
        dg_ref[...] += _fold8(dy_v * xhat)
        db_ref[...] += _fold8(dy_v)

    res = pl.pallas_call(
        body, name="loss_ln_bwd", grid=(t // tb,),
        in_specs=[blk, blk, blk, pl.BlockSpec((1, d), lambda i: (0, 0))],
        out_specs=[blk, blk, acc, acc, pl.BlockSpec((SUBLANES, LANES), lambda i: (0, 0))],
        out_shape=[jax.ShapeDtypeStruct((t, d), F32), jax.ShapeDtypeStruct((t, d), ACT),
                   jax.ShapeDtypeStruct((SUBLANES, d), F32), jax.ShapeDtypeStruct((SUBLANES, d), F32),
                   jax.ShapeDtypeStruct((SUBLANES, LANES), F32)],
        compiler_params=_cparams(("arbitrary",)),
    )(y, target, hres, g)
    return res[:4], res[4]


def _adamw(w, g, m, v):
    shape = w.shape
    cols = shape[-1]
    w2, g2, m2, v2 = (a.reshape(-1, cols) for a in (w, g, m, v))
    rows = w2.shape[0]
    tr = _tile(rows, 256, SUBLANES)
    blk = pl.BlockSpec((tr, cols), lambda i: (i, 0))

    def body(w_ref, g_ref, m_ref, v_ref, d_ref, nm_ref, nv_ref):
        g_v = g_ref[...]
        nm = ADAM_B1 * m_ref[...] + (1.0 - ADAM_B1) * g_v
        nv = ADAM_B2 * v_ref[...] + (1.0 - ADAM_B2) * (g_v * g_v)
        m_hat = nm / (1.0 - ADAM_B1 ** ADAM_STEP)
        v_hat = nv / (1.0 - ADAM_B2 ** ADAM_STEP)
        d_ref[...] = -ADAM_LR * (m_hat / (jnp.sqrt(v_hat) + ADAM_EPS) + ADAM_WD * w_ref[...])
        nm_ref[...] = nm
        nv_ref[...] = nv

    out = jax.ShapeDtypeStruct((rows, cols), F32)
    res = pl.pallas_call(
        body, name="adamw", grid=(rows // tr,),
        in_specs=[blk] * 4, out_specs=[blk] * 3, out_shape=[out] * 3,
        compiler_params=_cparams(("parallel",)),
    )(w2, g2, m2, v2)
    return tuple(r.reshape(shape) for r in res)


def _place():
    x, y, c = lax.axis_index("x"), lax.axis_index("y"), lax.axis_index("c")
    return x, y, c, [(1 - x, y), (x, 1 - y), (1 - x, 1 - y)]


def _remote(src, dst, send_sems, recv_sems, k, to):
    return pltpu.make_async_remote_copy(src_ref=src, dst_ref=dst, send_sem=send_sems.at[k],
                                        recv_sem=recv_sems.at[k], device_id=to, device_id_type=MESH)


class _Carried:
    def __init__(self, inputs, out_shapes, n_sems, copies):
        self.inputs, self.out_shapes, self.n_sems, self.copies = list(inputs), list(out_shapes), n_sems, copies

    def scratch(self):
        return [pltpu.SemaphoreType.DMA((self.n_sems,)), pltpu.SemaphoreType.DMA((self.n_sems,))]


def _join_plans(first, second):
    ni, no, ns = len(first.inputs), len(first.out_shapes), first.n_sems

    def copies(in_refs, out_refs, send_sems, recv_sems):
        one = _phases(first.copies(in_refs[:ni], out_refs[:no], send_sems, recv_sems))
        two = _phases(second.copies(in_refs[ni:], out_refs[no:], send_sems.at[pl.ds(ns, second.n_sems)],
                                    recv_sems.at[pl.ds(ns, second.n_sems)]))

        def both(k):
            def run():
                one[k]()
                two[k]()
            return run

        return both(0), both(1), both(2)

    return _Carried(first.inputs + second.inputs, first.out_shapes + second.out_shapes, ns + second.n_sems, copies)


def _run_comm(name, plan):
    n_in, n_out = len(plan.inputs), len(plan.out_shapes)

    def body(*refs):
        for phase in _phases(plan.copies(refs[:n_in], refs[n_in:n_in + n_out], refs[-2], refs[-1])):
            phase()

    return pl.pallas_call(
        body, name=name, in_specs=[ANY] * n_in, out_specs=[ANY] * n_out, out_shape=plan.out_shapes,
        scratch_shapes=plan.scratch(),
    )(*plan.inputs)


def _half_rows(rows, core):
    if rows % (4 * SUBLANES):
        return None
    return pl.ds(pl.multiple_of(core * (rows // 2), 2 * SUBLANES), rows // 2)


def _all_gather_plan(shards):
    n = len(shards)

    def copies(x_refs, out_refs, send_sems, recv_sems):
        x, y, c, chips = _place()
        sibling = (x, y, 1 - c)
        mine = 2 * x + y
        split = [_half_rows(x_refs[t].shape[0], c) is not None for t in range(n)]

        def src(t):
            return x_refs[t].at[_half_rows(x_refs[t].shape[0], c)] if split[t] else x_refs[t]

        def slot(t, chip_idx, core):
            rows = _half_rows(x_refs[t].shape[0], core)
            return out_refs[t].at[chip_idx, rows] if split[t] else out_refs[t].at[chip_idx]

        def first():
            return [_remote(src(t), slot(t, mine, c), send_sems, recv_sems, 6 * t + j, (cx, cy, c))
                    for j, (cx, cy) in enumerate(chips) for t in range(n)]

        def start():
            for cp in first():
                cp.start()

        def passed():
            return [_remote(slot(t, 2 * cx + cy, c), slot(t, 2 * cx + cy, c), send_sems, recv_sems, 6 * t + 3 + j,
                            sibling) for j, (cx, cy) in enumerate(chips) for t in range(n) if split[t]]

        def middle():
            for j, (cx, cy) in enumerate(chips):
                for t in range(n):
                    theirs = slot(t, 2 * cx + cy, c)
                    _remote(theirs, theirs, send_sems, recv_sems, 6 * t + j, (cx, cy, c)).wait_recv()
            for cp in passed():
                cp.start()

        def finish():
            for j, (cx, cy) in enumerate(chips):
                for t in range(n):
                    if split[t]:
                        other = slot(t, 2 * cx + cy, 1 - c)
                        _remote(other, other, send_sems, recv_sems, 6 * t + 3 + j, sibling).wait_recv()
            for cp in first() + passed():
                cp.wait_send()

        return start, middle, finish

    return _Carried(shards, [jax.ShapeDtypeStruct((N_CHIPS,) + s.shape, s.dtype) for s in shards], 6 * n, copies)


def _sibling_exchange_plan(grads, small=None):
    n = len(grads)
    extra = [] if small is None else [small]

    def copies(in_refs, out_refs, send_sems, recv_sems):
        x, y, c, _ = _place()
        sibling = (x, y, 1 - c)

        def all_copies():
            cps = [_remote(in_refs[t].at[:, _half_rows(in_refs[t].shape[1], 1 - c), :], out_refs[t],
                           send_sems, recv_sems, t, sibling) for t in range(n)]
            if extra:
                cps.append(_remote(in_refs[n], out_refs[n], send_sems, recv_sems, n, sibling))
            return cps

        def start():
            for cp in all_copies():
                cp.start()

        def finish():
            for cp in all_copies():
                cp.wait()

        return start, finish

    shapes = [jax.ShapeDtypeStruct((g.shape[0], g.shape[1] // 2, g.shape[2]), g.dtype) for g in grads]
    shapes += [jax.ShapeDtypeStruct(s.shape, s.dtype) for s in extra]
    return _Carried(list(grads) + extra, shapes, n + 1, copies)


def _chip_exchange_plan(travel, small=None):
    n = len(travel)
    extra = [] if small is None else [small]

    def copies(in_refs, out_refs, send_sems, recv_sems):
        x, y, c, chips = _place()
        mine = 2 * x + y

        def all_copies():
            cps = []
            for j, (cx, cy) in enumerate(chips):
                to = (cx, cy, c)
                for t in range(n):
                    cps.append(_remote(in_refs[t].at[2 * cx + cy], out_refs[t].at[mine], send_sems, recv_sems,
                                       3 * t + j, to))
                if extra:
                    cps.append(_remote(in_refs[n], out_refs[n].at[mine], send_sems, recv_sems, 3 * n + j, to))
            return cps

        def start():
            for cp in all_copies():
                cp.start()

        def finish():
            for cp in all_copies():
                cp.wait()

        return start, finish

    shapes = [jax.ShapeDtypeStruct(g.shape, g.dtype) for g in travel]
    shapes += [jax.ShapeDtypeStruct((N_CHIPS,) + s.shape, s.dtype) for s in extra]
    return _Carried(list(travel) + extra, shapes, 3 * n + 3, copies)


def _sibling_merge_plan(reduced):
    n = len(reduced)

    def copies(in_refs, out_refs, send_sems, recv_sems):
        x, y, c, _ = _place()

        def all_copies():
            return [_remote(in_refs[t], out_refs[t], send_sems, recv_sems, t, (x, y, 1 - c)) for t in range(n)]

        def start():
            for cp in all_copies():
                cp.start()

        def finish():
            for cp in all_copies():
                cp.wait()

        return start, finish

    return _Carried(reduced, [jax.ShapeDtypeStruct(r.shape, r.dtype) for r in reduced], n, copies)


def _pair_sum(place, grad, land):
    n, r, c = grad.shape
    half = r // 2
    tr = _tile(half, 256, SUBLANES)
    nb = half // tr

    def body(place_ref, a_ref, b_ref, travel_ref, own_ref):
        total = a_ref[0] + b_ref[0]
        travel_ref[0] = total.astype(travel_ref.dtype)

        @pl.when(pl.program_id(1) == place_ref[1])
        def _():
            own_ref[...] = total

    return pl.pallas_call(
        body, name="grad_pair_sum",
        grid_spec=pltpu.PrefetchScalarGridSpec(
            num_scalar_prefetch=1, grid=(nb, n),
            in_specs=[pl.BlockSpec((1, tr, c), lambda i, s, p: (s, p[0] * nb + i, 0)),
                      pl.BlockSpec((1, tr, c), lambda i, s, p: (s, i, 0))],
            out_specs=[pl.BlockSpec((1, tr, c), lambda i, s, p: (s, i, 0)),
                       pl.BlockSpec((tr, c), lambda i, s, p: (i, 0))]),
        out_shape=[jax.ShapeDtypeStruct((n, half, c), BF16), jax.ShapeDtypeStruct((half, c), F32)],
        compiler_params=_cparams(("parallel", "arbitrary")),
    )(place, grad, land)


def _chip_sum(place, own, land, name):
    n, r, c = land.shape
    tr = _tile(r, 256, SUBLANES)

    def body(place_ref, own_ref, land_ref, o_ref):
        mine = place_ref[1]
        acc = jnp.zeros(o_ref.shape, F32)
        for s in range(n):
            acc = acc + jnp.where(mine == s, own_ref[...], land_ref[s].astype(F32))
        o_ref[...] = acc

    return pl.pallas_call(
        body, name=name,
        grid_spec=pltpu.PrefetchScalarGridSpec(
            num_scalar_prefetch=1, grid=(r // tr,),
            in_specs=[pl.BlockSpec((tr, c), lambda i, p: (i, 0)),
                      pl.BlockSpec((n, tr, c), lambda i, p: (0, i, 0))],
            out_specs=pl.BlockSpec((tr, c), lambda i, p: (i, 0))),
        out_shape=jax.ShapeDtypeStruct((r, c), F32),
        compiler_params=_cparams(("parallel",)),
    )(place, own, land)


def _add2(a, b):
    rows = a.shape[0]
    tr = _tile(rows, 256, SUBLANES)
    blk = pl.BlockSpec((tr, a.shape[1]), lambda i: (i, 0))

    def body(a_ref, b_ref, o_ref):
        o_ref[...] = a_ref[...] + b_ref[...]

    return pl.pallas_call(
        body, name="grad_small_pair_sum", grid=(rows // tr,), in_specs=[blk, blk], out_specs=blk,
        out_shape=jax.ShapeDtypeStruct(a.shape, F32), compiler_params=_cparams(("parallel",)),
    )(a, b)


def _merge_halves(place, mine, other):
    first_core = place[0] == 0
    return jnp.concatenate([jnp.where(first_core, mine, other), jnp.where(first_core, other, mine)], axis=0)


_BIG = (("w_in", 2), ("w_pa", 1), ("w_pb", 1), ("w_o", 1), ("w_ffn_gate", 2), ("w_ffn_up", 2),
        ("w_ffn_down", 1))
_SMALL = ("conv_w", "a_log", "dt_bias", "o_norm_w", "sgu_ln_g", "sgu_ln_b", "w_s", "b_s",
          "ln1_g", "ln1_b", "ln2_g", "ln2_b")


def _pack_small(arrays):
    pieces = []
    for a in arrays:
        if a.shape[-1] % LANES == 0:
            a2 = a.reshape(-1, LANES)
        else:
            a2 = jnp.pad(a.reshape(-1, a.shape[-1]), ((0, 0), (0, LANES - a.shape[-1])))
        pieces.append(jnp.pad(a2, ((0, -a2.shape[0] % SUBLANES), (0, 0))))
    return jnp.concatenate(pieces, axis=0)


def _unpack_small(buf, like):
    out, off = [], 0
    for a in like:
        if a.shape[-1] % LANES == 0:
            rows = a.size // LANES
            out.append(buf[off:off + rows].reshape(a.shape))
        else:
            rows = a.size // a.shape[-1]
            out.append(buf[off:off + rows, :a.shape[-1]].reshape(a.shape))
        off += -(-rows // SUBLANES) * SUBLANES
    return out


def _unshard(gathered, local, chip, axis):
    parts = [jnp.where(chip == s, local, gathered[s]) for s in range(N_CHIPS)]
    return jnp.concatenate(parts, axis=axis - 1)


def _to_shards(full, axis):
    l, r, c = full.shape
    if axis == 1:
        return full.reshape(l, N_CHIPS, r // N_CHIPS, c)
    return jnp.transpose(full.reshape(l, r, N_CHIPS, c // N_CHIPS), (0, 2, 1, 3))


def _row(v, width=None):
    v = v.reshape(1, -1).astype(F32)
    if width is not None and v.shape[1] < width:
        v = jnp.pad(v, ((0, 0), (0, width - v.shape[1])))
    return v


def _layer_consts(p, l, d):
    heads = d // DN_DK
    return dict(
        alog=_row(p["a_log"][l], LANES), dtb=_row(p["dt_bias"][l], LANES),
        onw=_row(jnp.tile(p["o_norm_w"][l], heads)),
        lng=_row(p["sgu_ln_g"][l]), lnb=_row(p["sgu_ln_b"][l]),
        ws=p["w_s"][l].astype(F32),
        bst=jnp.pad(p["b_s"][l].T, ((0, 0), (0, LANES - p["b_s"].shape[1]))),
        g1=_row(p["ln1_g"][l]), b1=_row(p["ln1_b"][l]), g2=_row(p["ln2_g"][l]), b2=_row(p["ln2_b"][l]))


class _NoComm:
    def with_proj_main(self):
        return None

    def after_proj_main(self, got):
        pass

    def weights(self, full):
        return full

    def with_dn_fwd(self):
        return None

    def after_dn_fwd(self, got):
        pass

    def with_ffn_in_dw(self):
        return None

    def after_ffn_in_dw(self, got):
        pass

    def after_branch_grads(self, g):
        pass

    def with_dn_bwd(self):
        return None

    def after_dn_bwd(self, got):
        pass

    def with_proj_main_dw(self):
        return None

    def after_proj_main_dw(self, got):
        pass

    def with_ffn_in(self):
        return None

    def after_ffn_in(self, got):
        pass

    def after_all_grads(self, g):
        pass

    def with_gate_sgu_bwd(self):
        return None

    def after_gate_sgu_bwd(self, got):
        pass

    def with_proj_gates_dx(self):
        return None

    def after_proj_gates_dx(self, got):
        pass

    def with_proj_main_dx(self):
        return None

    def after_proj_main_dx(self, got):
        pass


def _carry(carried, after, call, *args, **kw):
    if carried is None:
        return call(*args, **kw)
    out, got = call(*args, carried=carried, **kw)
    after(got)
    return out


def _in_proj_weights(w_in, d):
    heads, q4 = d // DN_DK, 4 * d
    wba = jnp.zeros((d, 2 * LANES), w_in.dtype)
    wba = wba.at[:, :heads].set(w_in[:, q4:q4 + heads])
    wba = wba.at[:, LANES:LANES + heads].set(w_in[:, q4 + heads:q4 + 2 * heads])
    return jnp.concatenate([w_in[:, :q4], w_in[:, q4 + 2 * heads:]], axis=1), wba


def _layer_fwd(x, xb, full, cl, d, tb, comm):
    wm, wba = _in_proj_weights(full["w_in"], d)
    projm = _carry(comm.with_proj_main(), comm.after_proj_main, _matmul, xb, wm, NN, "proj_main", tn=MM_WIDE)
    full = comm.weights(full)
    wl = dict(wm=wm, wba=wba, conv=full["conv_w"], wpa=full["w_pa"], wpb=full["w_pb"], wo=full["w_o"],
              wgu=jnp.concatenate([full["w_ffn_gate"], full["w_ffn_up"]], axis=1), wd=full["w_ffn_down"])
    ba = _matmul(xb, wba, NN, "proj_gates")
    qkv = _conv_fwd(projm, wl["conv"], d, _tile(x.shape[0], 2 * tb, SUBLANES))
    (o, states, ycors), got = _dn_fwd(qkv, ba, cl["alog"], cl["dtb"], d, comm.with_dn_fwd())
    comm.after_dn_fwd(got)
    ya, yb = _gate_sgu_fwd(o, projm, cl["onw"], cl["lng"], cl["lnb"], cl["ws"], cl["bst"], d)
    pa, pb, m, h1, x1, x1b = _mix_fwd(ya, yb, projm, x, wl["wpa"], wl["wpb"], wl["wo"], cl["g1"], cl["b1"], d, tb)
    gu = _carry(comm.with_ffn_in(), comm.after_ffn_in, _matmul, x1b, wl["wgu"], NN, "ffn_in", out_dtype=ACT)
    act, h2, x2, x2b = _ffn_tail_fwd(gu, wl["wd"], x1, cl["g2"], cl["b2"], tb)
    saved = dict(xb=xb, projm=projm, ba=ba, qkv=qkv, o=o, states=states, ycors=ycors, ya=ya, yb=yb,
                 pa=pa, pb=pb, m=m, h1=h1, x1b=x1b, gu=gu, act=act, h2=h2)
    return x2, x2b, saved, wl


def _layer_bwd(sv, wl, cl, d, tb, comm, ln2_bwd, next_ln=None):
    g = {}
    dh2, dh2b, dg2, db2 = ln2_bwd
    g["ln2_g"], g["ln2_b"] = dg2.sum(0), db2.sum(0)
    g["wd"] = _matmul(sv["act"], dh2b, TN, "ffn_out_dw")
    dgu = _ffn_tail_bwd(dh2b, wl["wd"], sv["gu"], tb)
    g["wgu"] = _carry(comm.with_ffn_in_dw(), comm.after_ffn_in_dw, _matmul, sv["x1b"], dgu, TN, "ffn_in_dw")
    dh1, dh1b, dg1, db1 = _ffn_head_bwd(dgu, wl["wgu"], dh2, sv["h1"], cl["g1"], tb)
    g["ln1_g"], g["ln1_b"] = dg1.sum(0), db1.sum(0)
    g["wo"] = _matmul(sv["m"], dh1b, TN, "wo_dw")
    dpa, dpb, dya, dyb, dprojm = _mix_bwd(dh1b, sv["pa"], sv["pb"], sv["projm"], wl["wpa"], wl["wpb"], wl["wo"], d, tb)
    g["wpa"] = _matmul(sv["ya"], dpa, TN, "wpa_dw")
    g["wpb"] = _matmul(sv["yb"], dpb, TN, "wpb_dw")
    comm.after_branch_grads(g)
    (do, dprojm, donw, dlng, dlnb, dws, dbst), got = _gate_sgu_bwd(
        dya, dyb, sv["o"], sv["projm"], cl["onw"], cl["lng"], cl["lnb"], cl["ws"], cl["bst"], dprojm, d,
        comm.with_gate_sgu_bwd())
    comm.after_gate_sgu_bwd(got)
    heads, groups = d // DN_DK, d // SGU_GROUP_DIM
    g["o_norm_w"], g["sgu_ln_g"], g["sgu_ln_b"] = donw.sum(0), dlng.sum(0), dlnb.sum(0)
    g["w_s"], g["b_s"] = dws, dbst[:, :groups].T
    (dqkv, dba, dal, ddt), got = _dn_bwd(sv["qkv"], sv["ba"], cl["alog"], cl["dtb"], do, sv["states"],
                                         sv["ycors"], d, comm.with_dn_bwd())
    comm.after_dn_bwd(got)
    g["a_log"], g["dt_bias"] = dal.sum(0)[:heads], ddt.sum(0)[:heads]
    tbc = _tile(sv["xb"].shape[0], 2 * tb, SUBLANES)
    dy, dcw = _conv_bwd_dy(sv["projm"], wl["conv"], dqkv, d, tbc)
    g["conv_w"] = dcw.sum(1)
    dprojm = _conv_bwd_dx(dy, wl["conv"], dprojm, d, tbc)
    g["wba"] = _matmul(sv["xb"], dba, TN, "proj_gates_dw")
    g["wm"] = _carry(comm.with_proj_main_dw(), comm.after_proj_main_dw, _matmul, sv["xb"], dprojm, TN,
                     "proj_main_dw", tn=MM_WIDE)
    comm.after_all_grads(g)
    dx = _carry(comm.with_proj_gates_dx(), comm.after_proj_gates_dx, _matmul, dba, wl["wba"], NT, "proj_gates_dx",
                add=dh1, coef=ALPHA)
    if next_ln is not None:
        return _matmul(dprojm, wl["wm"], NT, "proj_main_dx", add=dx, tm=MM_TILE // 3, tk=MM_WIDE, ln=next_ln), g
    dx = _carry(comm.with_proj_main_dx(), comm.after_proj_main_dx, _matmul, dprojm, wl["wm"], NT, "proj_main_dx",
                add=dx, tk=MM_WIDE)
    return dx, g


_BRANCH = ("w_pa", "w_pb", "w_o", "w_ffn_gate", "w_ffn_up", "w_ffn_down")


def _grad_shards(g, d, keys):
    heads, q4 = d // DN_DK, 4 * d
    rows = lambda a: a.reshape(N_CHIPS, -1, a.shape[1])
    out = {}
    if "w_in" in keys:
        gm, gba, wsh = g["wm"], g["wba"], 2 * d + heads // 2
        out["w_in"] = jnp.stack([gm[:, :wsh],
                                 jnp.concatenate([gm[:, wsh:q4], gba[:, :heads]], axis=1),
                                 jnp.concatenate([gba[:, LANES:LANES + heads], gm[:, q4:q4 + wsh - heads]], axis=1),
                                 gm[:, q4 + wsh - heads:]])
    if "w_pa" in keys:
        ggu = g["wgu"]
        f = ggu.shape[1] // 2
        fs = f // N_CHIPS
        out.update({
            "w_pa": rows(g["wpa"]), "w_pb": rows(g["wpb"]), "w_o": rows(g["wo"]), "w_ffn_down": rows(g["wd"]),
            "w_ffn_gate": jnp.stack([ggu[:, s * fs:(s + 1) * fs] for s in range(N_CHIPS)]),
            "w_ffn_up": jnp.stack([ggu[:, f + s * fs:f + (s + 1) * fs] for s in range(N_CHIPS)])})
    return out


def _local_step(x, target, full0, full1_of, small_w, comm0=None):
    t, d = x.shape
    tb = _tile(t, 256, SUBLANES)
    comm0 = comm0 or _NoComm()
    consts = [_layer_consts(small_w, l, d) for l in range(DEPTH)]
    x1, x1b, sv0, w0 = _layer_fwd(x, x.astype(ACT), full0, consts[0], d, tb, comm0)
    x2, _, sv1, w1 = _layer_fwd(x1, x1b, full1_of(), consts[1], d, tb, _NoComm())
    ln2_bwd, loss_parts = _loss_ln_bwd(x2, target, sv1["h2"], consts[1]["g2"], tb)
    ln2_bwd, g1 = _layer_bwd(sv1, w1, consts[1], d, tb, _NoComm(), ln2_bwd, next_ln=(sv0["h2"], consts[0]["g2"]))
    comm0.layer1_grads = g1
    grad_x, g0 = _layer_bwd(sv0, w0, consts[0], d, tb, comm0, ln2_bwd)
    return loss_parts, grad_x, [g0, g1]


def kernel(x, w_in, conv_w, a_log, dt_bias, o_norm_w, sgu_ln_g, sgu_ln_b, w_s, b_s, w_pa, w_pb, w_o, ln1_g, ln1_b, w_ffn_gate, w_ffn_up, w_ffn_down, ln2_g, ln2_b, loss_target, m_w_in, m_conv_w, m_a_log, m_dt_bias, m_o_norm_w, m_sgu_ln_g, m_sgu_ln_b, m_w_s, m_b_s, m_w_pa, m_w_pb, m_w_o, m_ln1_g, m_ln1_b, m_w_ffn_gate, m_w_ffn_up, m_w_ffn_down, m_ln2_g, m_ln2_b, v_w_in, v_conv_w, v_a_log, v_dt_bias, v_o_norm_w, v_sgu_ln_g, v_sgu_ln_b, v_w_s, v_b_s, v_w_pa, v_w_pb, v_w_o, v_ln1_g, v_ln1_b, v_w_ffn_gate, v_w_ffn_up, v_w_ffn_down, v_ln2_g, v_ln2_b):
    names = ("w_in", "conv_w", "a_log", "dt_bias", "o_norm_w", "sgu_ln_g", "sgu_ln_b", "w_s", "b_s", "w_pa",
             "w_pb", "w_o", "ln1_g", "ln1_b", "w_ffn_gate", "w_ffn_up", "w_ffn_down", "ln2_g", "ln2_b")
    w = dict(zip(names, (w_in, conv_w, a_log, dt_bias, o_norm_w, sgu_ln_g, sgu_ln_b, w_s, b_s, w_pa, w_pb, w_o,
                         ln1_g, ln1_b, w_ffn_gate, w_ffn_up, w_ffn_down, ln2_g, ln2_b)))
    mom = dict(zip(names, (m_w_in, m_conv_w, m_a_log, m_dt_bias, m_o_norm_w, m_sgu_ln_g, m_sgu_ln_b, m_w_s, m_b_s,
                           m_w_pa, m_w_pb, m_w_o, m_ln1_g, m_ln1_b, m_w_ffn_gate, m_w_ffn_up, m_w_ffn_down,
                           m_ln2_g, m_ln2_b)))
    var = dict(zip(names, (v_w_in, v_conv_w, v_a_log, v_dt_bias, v_o_norm_w, v_sgu_ln_g, v_sgu_ln_b, v_w_s, v_b_s,
                           v_w_pa, v_w_pb, v_w_o, v_ln1_g, v_ln1_b, v_w_ffn_gate, v_w_ffn_up, v_w_ffn_down,
                           v_ln2_g, v_ln2_b)))
    chip = 2 * lax.axis_index("x") + lax.axis_index("y")
    place = jnp.stack([lax.axis_index("c"), chip]).astype(jnp.int32)

    big = [k for k, _ in _BIG]
    axis_of = dict(_BIG)
    local = {k: w[k].astype(BF16) for k in big}
    local["conv_w"] = conv_w

    def gather_plan(l, keys):
        return _all_gather_plan([local[k][l] for k in keys])

    def full_of(l, keys, gathered):
        return {k: _unshard(gt, local[k][l], chip, axis_of.get(k, 2)) for k, gt in zip(keys, gathered)}

    def pair_sums(grads_l, keys, lands):
        return [_pair_sum(place, grads_l[k], land) for k, land in zip(keys, lands)]

    def chip_sums(pairs, lands):
        return [_chip_sum(place, p[1], land, "grad_chip_sum") for p, land in zip(pairs, lands)]

    class Layer0Comm(_NoComm):
        def with_proj_main(self):
            return gather_plan(0, _BRANCH)

        def after_proj_main(self, got):
            self.rest = full_of(0, _BRANCH, got)

        def weights(self, full):
            return {**full, **self.rest}

        def with_dn_fwd(self):
            return gather_plan(1, mixer)

        def after_dn_fwd(self, got):
            self.full1 = full_of(1, mixer, got)

        def with_ffn_in(self):
            return gather_plan(1, ffn)

        def after_ffn_in(self, got):
            self.full1.update(full_of(1, ffn, got))

        def with_ffn_in_dw(self):
            self.g1 = _grad_shards(self.layer1_grads, x.shape[-1], big)
            return _sibling_exchange_plan([self.g1[k] for k in big])

        def after_ffn_in_dw(self, got):
            self.pairs1 = pair_sums(self.g1, big, got)

        def with_dn_bwd(self):
            return _chip_exchange_plan([p[0] for p in self.pairs1])

        def after_dn_bwd(self, got):
            self.red1 = chip_sums(self.pairs1, got)

        def after_branch_grads(self, g0):
            self.shards0 = _grad_shards(g0, x.shape[-1], _BRANCH)

        def with_gate_sgu_bwd(self):
            return _sibling_exchange_plan([self.shards0[k] for k in _BRANCH])

        def after_gate_sgu_bwd(self, got):
            self.pairs0 = pair_sums(self.shards0, _BRANCH, got)

        def with_proj_main_dw(self):
            return _chip_exchange_plan([p[0] for p in self.pairs0])

        def after_proj_main_dw(self, got):
            self.red0 = chip_sums(self.pairs0, got)

        def after_all_grads(self, g0):
            self.g_in = _grad_shards(g0, x.shape[-1], ["w_in"])["w_in"]
            self.small_g = {k: jnp.stack([g0[k], self.layer1_grads[k]]) for k in _SMALL}
            self.small = _pack_small([self.small_g[k] for k in _SMALL])

        def with_proj_gates_dx(self):
            return _sibling_exchange_plan([self.g_in], self.small)

        def after_proj_gates_dx(self, got):
            self.pair_in = _pair_sum(place, self.g_in, got[0])
            self.small_chip = _add2(self.small, got[1])

        def with_proj_main_dx(self):
            return _join_plans(_chip_exchange_plan([self.pair_in[0]], self.small_chip),
                               _sibling_merge_plan(self.red0 + self.red1))

        def after_proj_main_dx(self, got):
            self.red_in = _chip_sum(place, self.pair_in[1], got[0], "grad_chip_sum")
            self.small_total = _chip_sum(place, self.small_chip, got[1], "grad_small_chip_sum")
            self.others = got[2:]

    comm = Layer0Comm()
    first, mixer, ffn = ["w_in", "conv_w"], ["w_in", "conv_w", "w_pa", "w_pb", "w_o"], list(_BRANCH[3:])
    full0 = full_of(0, first, _run_comm("all_gather_weights", gather_plan(0, first)))
    small_w = {k: w[k] for k in _SMALL if k != "conv_w"}
    loss_parts, grad_x, g = _local_step(x[0], loss_target[0], full0, lambda: comm.full1, small_w, comm)

    reduced = [comm.red_in] + comm.red0 + comm.red1
    others = list(_run_comm("grad_sibling_merge", _sibling_merge_plan([comm.red_in]))) + list(comm.others)
    halves = [_merge_halves(place, mine, other) for mine, other in zip(reduced, others)]
    grads = {k: jnp.stack([halves[i], halves[len(big) + i]]) for i, k in enumerate(big)}
    grads.update(zip(_SMALL, _unpack_small(comm.small_total, [comm.small_g[k] for k in _SMALL])))
    grads["conv_w"] = lax.dynamic_index_in_dim(_to_shards(grads["conv_w"], 2), chip, 1, keepdims=False)

    delta, new_m, new_v = {}, {}, {}
    for k in [k for k, _ in _BIG] + ["conv_w"]:
        delta[k], new_m[k], new_v[k] = _adamw(w[k], grads[k], mom[k], var[k])
    rep = [k for k in _SMALL if k != "conv_w"]
    pack = lambda dct: _pack_small([dct[k] for k in rep])
    packed = _adamw(pack(w), pack(grads), pack(mom), pack(var))
    for dst, src in zip((delta, new_m, new_v), packed):
        dst.update(zip(rep, _unpack_small(src, [w[k] for k in rep])))

    loss = 0.5 * lax.psum(jnp.sum(loss_parts), ("x", "y", "c")) / x.shape[-1]
    return (loss, grad_x[None], *[grads[k] for k in names], *[delta[k] for k in names],
            *[new_m[k] for k in names], *[new_v[k] for k in names])
```

```python
import math

import jax
import jax.numpy as jnp
from jax import lax
from jax.experimental import pallas as pl
from jax.experimental.pallas import tpu as pltpu

F32 = jnp.float32
BF16 = jnp.bfloat16
MXU_DTYPE = jnp.bfloat16
ACT = jnp.bfloat16
HIGHEST = lax.Precision.HIGHEST

DEPTH = 2
CHUNK = 64
DN_GROUP = 2
DN_GROUP_FWD = 4
SGU_BLOCK = 128
SGU_WINDOWS = 4
CONV_K = 4
DN_DK = 128
SGU_GROUP_DIM = 128
LN_EPS = 1e-5
RMS_EPS = 1e-6
ALPHA = (2 * DEPTH) ** 0.25
ADAM_LR, ADAM_B1, ADAM_B2, ADAM_EPS, ADAM_WD, ADAM_STEP = 0.001, 0.9, 0.999, 1e-08, 0.01, 10

LANES = 128
SUBLANES = 8
VMEM_LIMIT = 52 * 2 ** 20
N_CHIPS = 4

NN = ((1,), (0,))
NT = ((1,), (1,))
TN = ((0,), (0,))
MESH = pl.DeviceIdType.MESH
ANY = pl.BlockSpec(memory_space=pl.ANY)


def _dot(a, b, dims=NN, prec=None):
    if prec is None:
        a = a.astype(MXU_DTYPE)
        b = b.astype(MXU_DTYPE)
    return lax.dot_general(a, b, (dims, ((), ())), preferred_element_type=F32, precision=prec)


def _cparams(sem=None):
    return pltpu.CompilerParams(dimension_semantics=sem, vmem_limit_bytes=VMEM_LIMIT)


def _tile(dim, pref, unit=LANES):
    t = (min(pref, dim) // unit) * unit
    while t >= unit:
        if dim % t == 0:
            return t
        t -= unit
    return dim


def _fold8(x):
    r, n = x.shape
    return x.reshape(r // SUBLANES, SUBLANES, n).sum(axis=0)


def _sigmoid(x):
    return 1.0 / (1.0 + jnp.exp(-x))


def _gelu(x):
    return 0.5 * x * (1.0 + lax.erf(x * (2.0 ** -0.5)))


def _gelu_grad(x):
    return 0.5 * (1.0 + lax.erf(x * (2.0 ** -0.5))) + x * jnp.exp(-0.5 * x * x) * (2.0 * math.pi) ** -0.5


def _ln_hat(h):
    mu = jnp.mean(h, axis=-1, keepdims=True)
    xc = h - mu
    var = jnp.mean(xc * xc, axis=-1, keepdims=True)
    r = lax.rsqrt(var + LN_EPS)
    return xc * r, r


def _ln_bwd(dxhat, xhat, r):
    return r * (dxhat - jnp.mean(dxhat, axis=-1, keepdims=True)
                - xhat * jnp.mean(dxhat * xhat, axis=-1, keepdims=True))


MM_TILE = 1536
MM_WIDE = 2048


def _matmul(a, b, dims, name, out_dtype=F32, add=None, coef=1.0, tm=MM_TILE, tn=MM_TILE, tk=MM_TILE, carried=None,
            ln=None):
    if dims == NN:
        (m, k), n = a.shape, b.shape[1]
    elif dims == NT:
        (m, k), n = a.shape, b.shape[0]
    else:
        (k, m), n = a.shape, b.shape[1]
    tm, tn, tk = _tile(m, tm), _tile(n, tn), _tile(k, tk)
    nk = k // tk
    a_spec = pl.BlockSpec((tk, tm), lambda j, i, q: (q, i)) if dims == TN else pl.BlockSpec((tm, tk), lambda j, i, q: (i, q))
    b_spec = pl.BlockSpec((tn, tk), lambda j, i, q: (j, q)) if dims == NT else pl.BlockSpec((tk, tn), lambda j, i, q: (q, j))
    o_spec = pl.BlockSpec((tm, tn), lambda j, i, q: (i, j))
    has_add = add is not None
    if ln is not None:
        assert n == tn and has_add and carried is None
        return _matmul_ln_bwd(a, b, dims, name, add, coef, ln, a_spec, b_spec, o_spec, (m, n, tm, tn, nk))

    def body(*refs):
        a_ref, b_ref = refs[0], refs[1]
        add_ref = refs[2] if has_add else None
        o_ref, acc_ref = refs[2 + has_add], refs[3 + has_add]
        q = pl.program_id(2)
        part = _dot(a_ref[...], b_ref[...], dims)

        def finish(r):
            if has_add:
                r = r + coef * add_ref[...]
            o_ref[...] = r.astype(out_dtype)

        if nk == 1:
            finish(part)
        else:
            @pl.when(q == 0)
            def _():
                acc_ref[...] = part

            @pl.when(q > 0)
            def _():
                acc_ref[...] += part

            @pl.when(q == nk - 1)
            def _():
                finish(acc_ref[...])

    ins = [a, b] + ([add] if has_add else [])
    in_specs = [a_spec, b_spec] + ([o_spec] if has_add else [])
    grid = (n // tn, m // tm, nk)
    acc = pltpu.VMEM((tm, tn) if nk > 1 else (SUBLANES, LANES), F32)
    out = jax.ShapeDtypeStruct((m, n), out_dtype)
    if carried is None:
        return pl.pallas_call(
            body, name=name, grid=grid, in_specs=in_specs, out_specs=o_spec, out_shape=out, scratch_shapes=[acc],
            compiler_params=_cparams(("parallel", "parallel", "arbitrary")),
        )(*ins)
    res = pl.pallas_call(
        _carrying(body, len(ins), 1, 1, carried, grid), name=name + "_carrying", grid=grid,
        in_specs=in_specs + [ANY] * len(carried.inputs), out_specs=[o_spec] + [ANY] * len(carried.out_shapes),
        out_shape=[out] + carried.out_shapes, scratch_shapes=[acc] + carried.scratch(),
        compiler_params=_cparams(("arbitrary", "arbitrary", "arbitrary")),
    )(*ins, *carried.inputs)
    return res[0], res[1:]


def _matmul_ln_bwd(a, b, dims, name, add, coef, ln, a_spec, b_spec, o_spec, sizes):
    m, n, tm, tn, nk = sizes
    hres, g = ln
    row = pl.BlockSpec((1, n), lambda j, i, q: (0, 0))
    sums = pl.BlockSpec((SUBLANES, n), lambda j, i, q: (0, 0))

    def body(a_ref, b_ref, add_ref, h_ref, g_ref, dh_ref, dhb_ref, dg_ref, db_ref, acc_ref):
        i, q = pl.program_id(1), pl.program_id(2)
        part = _dot(a_ref[...], b_ref[...], dims)

        @pl.when(jnp.logical_and(i == 0, q == 0))
        def _():
            dg_ref[...] = jnp.zeros_like(dg_ref)
            db_ref[...] = jnp.zeros_like(db_ref)

        @pl.when(q == 0)
        def _():
            acc_ref[...] = part

        @pl.when(q > 0)
        def _():
            acc_ref[...] += part

        @pl.when(q == nk - 1)
        def _():
            dy_v = acc_ref[...] + coef * add_ref[...]
            xhat, r = _ln_hat(h_ref[...])
            dh = _ln_bwd(dy_v * g_ref[...], xhat, r)
            dh_ref[...] = dh
            dhb_ref[...] = dh.astype(dhb_ref.dtype)
            dg_ref[...] += _fold8(dy_v * xhat)
            db_ref[...] += _fold8(dy_v)

    return pl.pallas_call(
        body, name=name + "_ln_bwd", grid=(1, m // tm, nk),
        in_specs=[a_spec, b_spec, o_spec, o_spec, row], out_specs=[o_spec, o_spec, sums, sums],
        out_shape=[jax.ShapeDtypeStruct((m, n), F32), jax.ShapeDtypeStruct((m, n), ACT),
                   jax.ShapeDtypeStruct((SUBLANES, n), F32), jax.ShapeDtypeStruct((SUBLANES, n), F32)],
        scratch_shapes=[pltpu.VMEM((tm, tn), F32)],
        compiler_params=_cparams(("arbitrary", "arbitrary", "arbitrary")),
    )(a, b, add, hres, g)


def _conv_taps(cur_ref, halo_ref, first):
    x = cur_ref[...]
    tb = x.shape[0]
    halo = jnp.where(first, 0.0, halo_ref[...])
    xc = jnp.concatenate([halo, x], axis=0)
    return [x] + [pltpu.roll(xc, s, 0)[SUBLANES:SUBLANES + tb] for s in range(1, CONV_K)]


def _conv_fwd(projm, conv_w, d, tb):
    t = projm.shape[0]
    heads = d // DN_DK
    hb = tb // SUBLANES

    def body(cur_ref, halo_ref, w_ref, o_ref):
        i, j = pl.program_id(0), pl.program_id(1)
        taps = _conv_taps(cur_ref, halo_ref, i == 0)
        y = taps[0] * w_ref[CONV_K - 1:CONV_K, :]
        for s in range(1, CONV_K):
            y = y + taps[s] * w_ref[CONV_K - 1 - s:CONV_K - s, :]
        act = y * _sigmoid(y)
        scale = jnp.where(j == 0, DN_DK ** -0.5, 1.0)
        for h in range(heads):
            seg = act[:, h * DN_DK:(h + 1) * DN_DK]
            r = lax.rsqrt(jnp.sum(seg * seg, axis=1, keepdims=True) + RMS_EPS) * scale
            o_ref[:, h * DN_DK:(h + 1) * DN_DK] = seg * jnp.where(j < 2, r, 1.0)

    blk = pl.BlockSpec((tb, d), lambda i, j: (i, j))
    return pl.pallas_call(
        body, name="conv_fwd", grid=(t // tb, 3),
        in_specs=[blk,
                  pl.BlockSpec((SUBLANES, d), lambda i, j: (jnp.maximum(i * hb - 1, 0), j)),
                  pl.BlockSpec((CONV_K, d), lambda i, j: (0, j))],
        out_specs=blk,
        out_shape=jax.ShapeDtypeStruct((t, 3 * d), F32),
        compiler_params=_cparams(("parallel", "parallel")),
    )(projm, projm, conv_w)


def _conv_bwd_dy(projm, conv_w, dqkv, d, tb):
    t = projm.shape[0]
    heads = d // DN_DK
    hb = tb // SUBLANES

    def body(cur_ref, halo_ref, w_ref, dout_ref, dy_ref, dw_ref):
        j, i = pl.program_id(0), pl.program_id(1)
        taps = _conv_taps(cur_ref, halo_ref, i == 0)
        y = taps[0] * w_ref[CONV_K - 1:CONV_K, :]
        for s in range(1, CONV_K):
            y = y + taps[s] * w_ref[CONV_K - 1 - s:CONV_K - s, :]
        sg = _sigmoid(y)
        act = y * sg
        dact = sg * (1.0 + y * (1.0 - sg))
        scale = jnp.where(j == 0, DN_DK ** -0.5, 1.0)
        for h in range(heads):
            cols = slice(h * DN_DK, (h + 1) * DN_DK)
            seg = act[:, cols]
            r = lax.rsqrt(jnp.sum(seg * seg, axis=1, keepdims=True) + RMS_EPS)
            nrm = seg * r
            dout = dout_ref[:, cols]
            ds = jnp.where(j < 2, (r * scale) * (dout - nrm * jnp.sum(dout * nrm, axis=1, keepdims=True)), dout)
            dy_ref[:, cols] = ds * dact[:, cols]
        dy = dy_ref[...]

        @pl.when(i == 0)
        def _():
            dw_ref[...] = jnp.zeros_like(dw_ref)

        for s in range(CONV_K):
            dw_ref[CONV_K - 1 - s] += _fold8(dy * taps[s])

    return pl.pallas_call(
        body, name="conv_bwd_dy", grid=(3, t // tb),
        in_specs=[pl.BlockSpec((tb, d), lambda j, i: (i, j)),
                  pl.BlockSpec((SUBLANES, d), lambda j, i: (jnp.maximum(i * hb - 1, 0), j)),
                  pl.BlockSpec((CONV_K, d), lambda j, i: (0, j)),
                  pl.BlockSpec((tb, d), lambda j, i: (i, j))],
        out_specs=[pl.BlockSpec((tb, d), lambda j, i: (i, j)),
                   pl.BlockSpec((CONV_K, SUBLANES, d), lambda j, i: (0, 0, j))],
        out_shape=[jax.ShapeDtypeStruct((t, 3 * d), F32),
                   jax.ShapeDtypeStruct((CONV_K, SUBLANES, 3 * d), F32)],
        compiler_params=_cparams(("parallel", "arbitrary")),
    )(projm, projm, conv_w, dqkv)


def _conv_bwd_dx(dy, conv_w, dprojm, d, tb):
    t = dy.shape[0]
    hb = tb // SUBLANES
    last = t // tb - 1

    def body(cur_ref, halo_ref, w_ref, alias_ref, o_ref):
        i = pl.program_id(0)
        cur = cur_ref[...]
        halo = jnp.where(i == last, 0.0, halo_ref[...])
        dc = jnp.concatenate([cur, halo], axis=0)
        acc = cur * w_ref[CONV_K - 1:CONV_K, :]
        for s in range(1, CONV_K):
            acc = acc + pltpu.roll(dc, tb + SUBLANES - s, 0)[:tb] * w_ref[CONV_K - 1 - s:CONV_K - s, :]
        o_ref[...] = acc.astype(o_ref.dtype)

    return pl.pallas_call(
        body, name="conv_bwd_dx", grid=(t // tb, 3),
        in_specs=[pl.BlockSpec((tb, d), lambda i, j: (i, j)),
                  pl.BlockSpec((SUBLANES, d), lambda i, j: (jnp.minimum((i + 1) * hb, t // SUBLANES - 1), j)),
                  pl.BlockSpec((CONV_K, d), lambda i, j: (0, j)),
                  ANY],
        out_specs=pl.BlockSpec((tb, d), lambda i, j: (i, j)),
        out_shape=jax.ShapeDtypeStruct(dprojm.shape, dprojm.dtype),
        input_output_aliases={3: 0},
        compiler_params=_cparams(("parallel", "parallel")),
    )(dy, dy, conv_w, dprojm)


def _beta_g(ba, alog, dtb):
    beta = _sigmoid(ba[:, :LANES])
    xa = ba[:, LANES:] + dtb
    softplus = jnp.maximum(xa, 0.0) + jnp.log(1.0 + jnp.exp(-jnp.abs(xa)))
    ea = jnp.exp(alog)
    return beta, -ea * softplus, ea, _sigmoid(xa)


def _inv_corrections(mats):
    ys = [-a for a in mats]
    ps = [_dot(a, a) for a in mats]
    steps = int(math.log2(CHUNK)) - 1
    for it in range(steps):
        ys = [y + p + _dot(y, p) for y, p in zip(ys, ps)]
        if it < steps - 1:
            ps = [_dot(p, p) for p in ps]
    return ys


def _chunk_masks():
    row = lax.broadcasted_iota(jnp.int32, (CHUNK, CHUNK), 0)
    col = lax.broadcasted_iota(jnp.int32, (CHUNK, CHUNK), 1)
    return row >= col, row > col, row <= col


def _col_of(mat, lane_idx, h):
    return jnp.sum(jnp.where(lane_idx == h, mat, 0.0), axis=1, keepdims=True)


def _row_of(mat, sub_idx, h):
    return jnp.sum(jnp.where(sub_idx == h, mat, 0.0), axis=0, keepdims=True)


def _phases(fns):
    return fns if len(fns) == 3 else (fns[0], lambda: None, fns[1])


def _carrying(compute, n_in, n_out, n_scratch, carried, grid):
    if carried is None:
        return compute
    ci, co = len(carried.inputs), len(carried.out_shapes)

    def body(*refs):
        ins, c_in = refs[:n_in], refs[n_in:n_in + ci]
        outs, c_out = refs[n_in + ci:n_in + ci + n_out], refs[n_in + ci + n_out:n_in + ci + n_out + co]
        scratch = refs[n_in + ci + n_out + co:]
        start, middle, finish = _phases(carried.copies(c_in, c_out, scratch[n_scratch], scratch[n_scratch + 1]))
        step, total = 0, 1
        for axis, steps in enumerate(grid):
            step = step * steps + pl.program_id(axis)
            total *= steps

        @pl.when(step == 0)
        def _():
            start()

        compute(*ins, *outs, *scratch[:n_scratch])

        @pl.when(step == (3 * total) // 4)
        def _():
            middle()

        @pl.when(step == total - 1)
        def _():
            finish()

    return body


def _dn_fwd(qkv, ba, alog, dtb, d, carried=None):
    t = qkv.shape[0]
    heads = d // DN_DK
    n_chunks = t // CHUNK
    grp = DN_GROUP_FWD if n_chunks % DN_GROUP_FWD == 0 else 1
    span = grp * CHUNK
    extra = carried or _Carried([], [], 0, None)

    def compute(qkv_ref, ba_ref, al_ref, dt_ref, o_ref, s_ref, y_ref, state):
        @pl.when(pl.program_id(0) == 0)
        def _():
            state[...] = jnp.zeros_like(state)

        tril, strict, _ = _chunk_masks()
        beta, g, _, _ = _beta_g(ba_ref[...], al_ref[...], dt_ref[...])
        lane = lax.broadcasted_iota(jnp.int32, (CHUNK, LANES), 1)
        sub = lax.broadcasted_iota(jnp.int32, (LANES, CHUNK), 0)
        rowc = lax.broadcasted_iota(jnp.int32, (CHUNK, 1), 0)
        hs = range(heads)
        units = [(c, h) for c in range(grp) for h in hs]
        un = range(len(units))
        rows = lambda c: slice(c * CHUNK, (c + 1) * CHUNK)
        gc = [_dot(jnp.where(tril, 1.0, 0.0), g[rows(c)], NN, HIGHEST) for c in range(grp)]
        gct = [m.T for m in gc]
        q = [qkv_ref[rows(c), h * DN_DK:(h + 1) * DN_DK] for c, h in units]
        k = [qkv_ref[rows(c), d + h * DN_DK:d + (h + 1) * DN_DK] for c, h in units]
        v = [qkv_ref[rows(c), 2 * d + h * DN_DK:2 * d + (h + 1) * DN_DK] for c, h in units]
        gch = [_col_of(gc[c], lane, h) for c, h in units]
        bh = [_col_of(beta[rows(c)], lane, h) for c, h in units]
        dec = [jnp.where(tril, jnp.exp(gch[n] - _row_of(gct[c], sub, h)), 0.0) for n, (c, h) in enumerate(units)]
        egc = [jnp.exp(gch[n]) for n in un]
        gl = [jnp.sum(jnp.where(rowc == CHUNK - 1, gch[n], 0.0), axis=0, keepdims=True) for n in un]
        kb = [k[n] * bh[n] for n in un]
        a = [jnp.where(strict, _dot(kb[n], k[n], NT) * dec[n], 0.0) for n in un]
        p = [_dot(q[n], k[n], NT) * dec[n] for n in un]
        ycor = _inv_corrections(a)
        rhs = [jnp.concatenate([v[n] * bh[n], kb[n] * egc[n]], axis=1) for n in un]
        sol = [rhs[n] + _dot(ycor[n], rhs[n]) for n in un]
        qg = [q[n] * egc[n] for n in un]
        kd = [k[n] * jnp.exp(gl[n] - gch[n]) for n in un]
        egl = [jnp.exp(gl[n]) for n in un]
        s_cur, s_in, o = [state[h] for h in hs], [], []
        for c in range(grp):
            ns = [c * heads + h for h in hs]
            vn = [sol[n][:, :DN_DK] - _dot(sol[n][:, DN_DK:], s_cur[h]) for h, n in enumerate(ns)]
            o += [_dot(qg[n], s_cur[h]) + _dot(p[n], vn[h]) for h, n in enumerate(ns)]
            s_in += s_cur
            s_cur = [s_cur[h] * egl[n] + _dot(kd[n], vn[h], TN) for h, n in enumerate(ns)]
        for n, (c, h) in enumerate(units):
            o_ref[rows(c), h * DN_DK:(h + 1) * DN_DK] = o[n]
            s_ref[c, h] = s_in[n]
            y_ref[h, rows(c), :] = ycor[n]
        for h in hs:
            state[h] = s_cur[h]

    res = pl.pallas_call(
        _carrying(compute, 4, 3, 1, carried, (n_chunks // grp,)),
        name="dn_fwd_carrying" if carried else "dn_fwd", grid=(n_chunks // grp,),
        in_specs=[pl.BlockSpec((span, 3 * d), lambda i: (i, 0)),
                  pl.BlockSpec((span, 2 * LANES), lambda i: (i, 0)),
                  pl.BlockSpec((1, LANES), lambda i: (0, 0)),
                  pl.BlockSpec((1, LANES), lambda i: (0, 0))] + [ANY] * len(extra.inputs),
        out_specs=[pl.BlockSpec((span, d), lambda i: (i, 0)),
                   pl.BlockSpec((grp, heads, DN_DK, DN_DK), lambda i: (i, 0, 0, 0)),
                   pl.BlockSpec((heads, span, CHUNK), lambda i: (0, i, 0))] + [ANY] * len(extra.out_shapes),
        out_shape=[jax.ShapeDtypeStruct((t, d), F32),
                   jax.ShapeDtypeStruct((n_chunks, heads, DN_DK, DN_DK), F32),
                   jax.ShapeDtypeStruct((heads, t, CHUNK), F32)] + extra.out_shapes,
        scratch_shapes=[pltpu.VMEM((heads, DN_DK, DN_DK), F32)] + (extra.scratch() if carried else []),
        compiler_params=_cparams(("arbitrary",)),
    )(qkv, ba, alog, dtb, *extra.inputs)
    return res[:3], res[3:]


def _dn_bwd(qkv, ba, alog, dtb, dout, states, ycors, d, carried=None):
    t = qkv.shape[0]
    heads = d // DN_DK
    n_chunks = t // CHUNK
    grp = DN_GROUP if n_chunks % DN_GROUP == 0 else 1
    span = grp * CHUNK
    rev = lambda i: n_chunks // grp - 1 - i
    extra = carried or _Carried([], [], 0, None)

    def compute(qkv_ref, ba_ref, al_ref, dt_ref, do_ref, s_ref, y_ref,
                dqkv_ref, dba_ref, dal_ref, ddt_ref, dstate):
        @pl.when(pl.program_id(0) == 0)
        def _():
            dstate[...] = jnp.zeros_like(dstate)
            dal_ref[...] = jnp.zeros_like(dal_ref)
            ddt_ref[...] = jnp.zeros_like(ddt_ref)

        tril, strict, triu = _chunk_masks()
        beta, g, ea, sig_a = _beta_g(ba_ref[...], al_ref[...], dt_ref[...])
        lane = lax.broadcasted_iota(jnp.int32, (CHUNK, LANES), 1)
        sub = lax.broadcasted_iota(jnp.int32, (LANES, CHUNK), 0)
        rowc = lax.broadcasted_iota(jnp.int32, (CHUNK, 1), 0)
        hs = range(heads)
        units = [(c, h) for c in range(grp) for h in hs]
        un = range(len(units))
        rows = lambda c: slice(c * CHUNK, (c + 1) * CHUNK)
        rsum = lambda x_: jnp.sum(x_, axis=1, keepdims=True)
        gc = [_dot(jnp.where(tril, 1.0, 0.0), g[rows(c)], NN, HIGHEST) for c in range(grp)]
        gct = [m.T for m in gc]
        q = [qkv_ref[rows(c), h * DN_DK:(h + 1) * DN_DK] for c, h in units]
        k = [qkv_ref[rows(c), d + h * DN_DK:d + (h + 1) * DN_DK] for c, h in units]
        v = [qkv_ref[rows(c), 2 * d + h * DN_DK:2 * d + (h + 1) * DN_DK] for c, h in units]
        dout_h = [do_ref[rows(c), h * DN_DK:(h + 1) * DN_DK] for c, h in units]
        s0 = [s_ref[c, h] for c, h in units]
        ycor = [y_ref[h, rows(c), :] for c, h in units]
        gch = [_col_of(gc[c], lane, h) for c, h in units]
        bh = [_col_of(beta[rows(c)], lane, h) for c, h in units]
        dec = [jnp.where(tril, jnp.exp(gch[n] - _row_of(gct[c], sub, h)), 0.0) for n, (c, h) in enumerate(units)]
        egc = [jnp.exp(gch[n]) for n in un]
        gl = [jnp.sum(jnp.where(rowc == CHUNK - 1, gch[n], 0.0), axis=0, keepdims=True) for n in un]
        egl = [jnp.exp(gl[n]) for n in un]
        ekd = [jnp.exp(gl[n] - gch[n]) for n in un]
        kb = [k[n] * bh[n] for n in un]
        kd = [k[n] * ekd[n] for n in un]
        qg = [q[n] * egc[n] for n in un]
        kbg = [kb[n] * egc[n] for n in un]
        a = [jnp.where(strict, _dot(kb[n], k[n], NT) * dec[n], 0.0) for n in un]
        p = [_dot(q[n], k[n], NT) * dec[n] for n in un]
        rhs = [jnp.concatenate([v[n] * bh[n], kbg[n]], axis=1) for n in un]
        sol = [rhs[n] + _dot(ycor[n], rhs[n]) for n in un]
        w = [sol[n][:, DN_DK:] for n in un]
        vn = [sol[n][:, :DN_DK] - _dot(w[n], s0[n]) for n in un]
        dqg = [_dot(dout_h[n], s0[n], NT) for n in un]
        dp = [jnp.where(tril, _dot(dout_h[n], vn[n], NT), 0.0) for n in un]
        pdo = [_dot(p[n], dout_h[n], TN) for n in un]
        qdo = [_dot(qg[n], dout_h[n], TN) for n in un]
        ds_cur = [dstate[h] for h in hs]
        dsn, dvn = [None] * len(units), [None] * len(units)
        for c in reversed(range(grp)):
            for h in hs:
                dsn[c * heads + h] = ds_cur[h]
            for h in hs:
                n = c * heads + h
                dvn[n] = pdo[n] + _dot(kd[n], ds_cur[h])
            ds_cur = [qdo[c * heads + h] + egl[c * heads + h] * ds_cur[h]
                      - _dot(w[c * heads + h], dvn[c * heads + h], TN) for h in hs]
        dkd = [_dot(vn[n], dsn[n], NT) for n in un]
        dw = [-_dot(dvn[n], s0[n], NT) for n in un]
        dgl = [jnp.sum(rsum(dsn[n] * s0[n]), axis=0, keepdims=True) * egl[n] for n in un]
        dsol = [jnp.concatenate([dvn[n], dw[n]], axis=1) for n in un]
        drhs = [dsol[n] + _dot(ycor[n], dsol[n], TN) for n in un]
        dvb = [drhs[n][:, :DN_DK] for n in un]
        dkbg = [drhs[n][:, DN_DK:] for n in un]
        da = [jnp.where(strict, -_dot(drhs[n], sol[n], NT), 0.0) for n in un]
        dma = [da[n] * dec[n] for n in un]
        dmp = [dp[n] * dec[n] for n in un]
        dkb = [_dot(dma[n], k[n]) + dkbg[n] * egc[n] for n in un]
        dq = [_dot(dmp[n], k[n]) + dqg[n] * egc[n] for n in un]
        dk = [_dot(dma[n], kb[n], TN) + _dot(dmp[n], q[n], TN) + dkd[n] * ekd[n] + dkb[n] * bh[n] for n in un]
        e = [da[n] * a[n] + dp[n] * p[n] for n in un]
        colsum = [jnp.sum(e[n], axis=0, keepdims=True) for n in un]
        tkd = [rsum(dkd[n] * kd[n]) for n in un]
        for n, (c, h) in enumerate(units):
            dqkv_ref[rows(c), h * DN_DK:(h + 1) * DN_DK] = dq[n]
            dqkv_ref[rows(c), d + h * DN_DK:d + (h + 1) * DN_DK] = dk[n]
            dqkv_ref[rows(c), 2 * d + h * DN_DK:2 * d + (h + 1) * DN_DK] = dvb[n] * bh[n]
        for h in hs:
            dstate[h] = ds_cur[h]
        valid = lane < heads
        dal_acc = jnp.zeros((SUBLANES, LANES), F32)
        ddt_acc = jnp.zeros((SUBLANES, LANES), F32)
        for c in range(grp):
            dgc_all = jnp.zeros((CHUNK, LANES), F32)
            dbeta_all = jnp.zeros((CHUNK, LANES), F32)
            colsums = jnp.zeros((LANES, CHUNK), F32)
            for h in hs:
                n = c * heads + h
                dgc = rsum(e[n]) + rsum(dqg[n] * qg[n]) - tkd[n] + rsum(dkbg[n] * kbg[n])
                dgc = dgc + jnp.where(rowc == CHUNK - 1, dgl[n] + jnp.sum(tkd[n], axis=0, keepdims=True), 0.0)
                dgc_all = dgc_all + jnp.where(lane == h, dgc, 0.0)
                colsums = colsums + jnp.where(sub == h, colsum[n], 0.0)
                dbeta_all = dbeta_all + jnp.where(lane == h, rsum(dkb[n] * k[n]) + rsum(dvb[n] * v[n]), 0.0)
            dg = _dot(jnp.where(triu, 1.0, 0.0), dgc_all - colsums.T, NN, HIGHEST)
            beta_c = beta[rows(c)]
            dbl = jnp.where(valid, dbeta_all * beta_c * (1.0 - beta_c), 0.0)
            dal = jnp.where(valid, -dg * ea * sig_a[rows(c)], 0.0)
            dba_ref[rows(c), :LANES] = dbl.astype(dba_ref.dtype)
            dba_ref[rows(c), LANES:] = dal.astype(dba_ref.dtype)
            dal_acc = dal_acc + _fold8(jnp.where(valid, dg * g[rows(c)], 0.0))
            ddt_acc = ddt_acc + _fold8(dal)
        dal_ref[...] += dal_acc
        ddt_ref[...] += ddt_acc

    res = pl.pallas_call(
        _carrying(compute, 7, 4, 1, carried, (n_chunks // grp,)),
        name="dn_bwd_carrying" if carried else "dn_bwd", grid=(n_chunks // grp,),
        in_specs=[pl.BlockSpec((span, 3 * d), lambda i: (rev(i), 0)),
                  pl.BlockSpec((span, 2 * LANES), lambda i: (rev(i), 0)),
                  pl.BlockSpec((1, LANES), lambda i: (0, 0)),
                  pl.BlockSpec((1, LANES), lambda i: (0, 0)),
                  pl.BlockSpec((span, d), lambda i: (rev(i), 0)),
                  pl.BlockSpec((grp, heads, DN_DK, DN_DK), lambda i: (rev(i), 0, 0, 0)),
                  pl.BlockSpec((heads, span, CHUNK), lambda i: (0, rev(i), 0))] + [ANY] * len(extra.inputs),
        out_specs=[pl.BlockSpec((span, 3 * d), lambda i: (rev(i), 0)),
                   pl.BlockSpec((span, 2 * LANES), lambda i: (rev(i), 0)),
                   pl.BlockSpec((SUBLANES, LANES), lambda i: (0, 0)),
                   pl.BlockSpec((SUBLANES, LANES), lambda i: (0, 0))] + [ANY] * len(extra.out_shapes),
        out_shape=[jax.ShapeDtypeStruct((t, 3 * d), F32),
                   jax.ShapeDtypeStruct((t, 2 * LANES), ACT),
                   jax.ShapeDtypeStruct((SUBLANES, LANES), F32),
                   jax.ShapeDtypeStruct((SUBLANES, LANES), F32)] + extra.out_shapes,
        scratch_shapes=[pltpu.VMEM((heads, DN_DK, DN_DK), F32)] + (extra.scratch() if carried else []),
        compiler_params=_cparams(("arbitrary",)),
    )(qkv, ba, alog, dtb, dout, states, ycors, *extra.inputs)
    return res[:4], res[4:]


def _sgu_mask():
    row = lax.broadcasted_iota(jnp.int32, (SGU_BLOCK, SGU_BLOCK), 0)
    col = lax.broadcasted_iota(jnp.int32, (SGU_BLOCK, SGU_BLOCK), 1)
    sh = int(math.log2(CHUNK))
    return lax.shift_right_logical(row, sh) >= lax.shift_right_logical(col, sh)


def _gate_sgu_fwd(o, projm, onw, lng, lnb, ws, bst, d):
    t = o.shape[0]
    heads, groups = d // DN_DK, d // SGU_GROUP_DIM
    tb = _tile(t, SGU_WINDOWS * SGU_BLOCK, SGU_BLOCK)
    row_spec = pl.BlockSpec((1, d), lambda i: (0, 0))

    def body(o_ref, z_ref, u_ref, v_ref, onw_ref, lng_ref, lnb_ref, ws_ref, bst_ref, ya_ref, yb_ref):
        for h in range(heads):
            cols = slice(h * DN_DK, (h + 1) * DN_DK)
            oh, zh = o_ref[:, cols], z_ref[:, cols]
            r = lax.rsqrt(jnp.mean(oh * oh, axis=1, keepdims=True) + RMS_EPS)
            ya_ref[:, cols] = (oh * r * onw_ref[:, cols] * (zh * _sigmoid(zh))).astype(ya_ref.dtype)
        xhat, _ = _ln_hat(_gelu(v_ref[...]))
        vgn = xhat * lng_ref[...] + lnb_ref[...]
        mask = _sgu_mask()
        lane = lax.broadcasted_iota(jnp.int32, (SGU_BLOCK, LANES), 1)
        bst_v = bst_ref[...]
        for gi in range(groups):
            cols = slice(gi * SGU_GROUP_DIM, (gi + 1) * SGU_GROUP_DIM)
            wsg = jnp.where(mask, ws_ref[gi], 0.0)
            bias = _col_of(bst_v, lane, gi)
            for win in range(tb // SGU_BLOCK):
                rows = slice(win * SGU_BLOCK, (win + 1) * SGU_BLOCK)
                sp = _dot(wsg, vgn[rows, cols]) + bias
                yb_ref[rows, cols] = (_gelu(u_ref[rows, cols]) * sp).astype(yb_ref.dtype)

    return pl.pallas_call(
        body, name="gate_sgu_fwd", grid=(t // tb,),
        in_specs=[pl.BlockSpec((tb, d), lambda i: (i, 0)),
                  pl.BlockSpec((tb, d), lambda i: (i, 3)),
                  pl.BlockSpec((tb, d), lambda i: (i, 4)),
                  pl.BlockSpec((tb, d), lambda i: (i, 5)),
                  row_spec, row_spec, row_spec,
                  pl.BlockSpec((groups, SGU_BLOCK, SGU_BLOCK), lambda i: (0, 0, 0)),
                  pl.BlockSpec((SGU_BLOCK, LANES), lambda i: (0, 0))],
        out_specs=[pl.BlockSpec((tb, d), lambda i: (i, 0)), pl.BlockSpec((tb, d), lambda i: (i, 0))],
        out_shape=[jax.ShapeDtypeStruct((t, d), ACT), jax.ShapeDtypeStruct((t, d), ACT)],
        compiler_params=_cparams(("parallel",)),
    )(o, projm, projm, projm, onw, lng, lnb, ws, bst)


def _gate_sgu_bwd(dya, dyb, o, projm, onw, lng, lnb, ws, bst, dprojm, d, carried=None):
    t = o.shape[0]
    heads, groups = d // DN_DK, d // SGU_GROUP_DIM
    tb = _tile(t, SGU_WINDOWS * SGU_BLOCK, SGU_BLOCK)
    extra = carried or _Carried([], [], 0, None)
    row_spec = pl.BlockSpec((1, d), lambda i: (0, 0))
    acc_row = pl.BlockSpec((SUBLANES, d), lambda i: (0, 0))

    def body(dya_ref, dyb_ref, o_ref, z_ref, u_ref, v_ref, onw_ref, lng_ref, lnb_ref, ws_ref, bst_ref, alias_ref,
             do_ref, dp_ref, donw_ref, dlng_ref, dlnb_ref, dws_ref, dbst_ref):
        @pl.when(pl.program_id(0) == 0)
        def _():
            for r_ in (donw_ref, dlng_ref, dlnb_ref, dws_ref, dbst_ref):
                r_[...] = jnp.zeros_like(r_)

        donw = jnp.zeros((SUBLANES, DN_DK), F32)
        for h in range(heads):
            cols = slice(h * DN_DK, (h + 1) * DN_DK)
            oh, zh, dyah, wh = o_ref[:, cols], z_ref[:, cols], dya_ref[:, cols], onw_ref[:, cols]
            r = lax.rsqrt(jnp.mean(oh * oh, axis=1, keepdims=True) + RMS_EPS)
            on = oh * r
            sz = _sigmoid(zh)
            silu_z = zh * sz
            don = dyah * wh * silu_z
            dp_ref[:, cols] = (dyah * on * wh * (sz * (1.0 + zh * (1.0 - sz)))).astype(dp_ref.dtype)
            donw = donw + _fold8(dyah * on * silu_z)
            do_ref[:, cols] = r * (don - on * jnp.mean(don * on, axis=1, keepdims=True))
        donw_ref[...] += donw

        vgp, up = v_ref[...], u_ref[...]
        xhat, rstd = _ln_hat(_gelu(vgp))
        lng_v = lng_ref[...]
        vgn = xhat * lng_v + lnb_ref[...]
        ua = _gelu(up)
        mask = _sgu_mask()
        lane = lax.broadcasted_iota(jnp.int32, (SGU_BLOCK, LANES), 1)
        bst_v = bst_ref[...]
        dbst = jnp.zeros((SGU_BLOCK, LANES), F32)
        dvgn_parts, dua_parts = [], []
        for gi in range(groups):
            cols = slice(gi * SGU_GROUP_DIM, (gi + 1) * SGU_GROUP_DIM)
            wsg = jnp.where(mask, ws_ref[gi], 0.0)
            bias = _col_of(bst_v, lane, gi)
            dws = jnp.zeros((SGU_BLOCK, SGU_BLOCK), F32)
            dvgn_g, dua_g = [], []
            for win in range(tb // SGU_BLOCK):
                rows = slice(win * SGU_BLOCK, (win + 1) * SGU_BLOCK)
                vg_g, dyb_g = vgn[rows, cols], dyb_ref[rows, cols]
                sp = _dot(wsg, vg_g) + bias
                dsp = dyb_g * ua[rows, cols]
                dua_g.append(dyb_g * sp)
                dws = dws + _dot(dsp, vg_g, NT)
                dbst = dbst + jnp.where(lane == gi, jnp.sum(dsp, axis=1, keepdims=True), 0.0)
                dvgn_g.append(_dot(wsg, dsp, TN))
            dws_ref[gi] += jnp.where(mask, dws, 0.0)
            dvgn_parts.append(jnp.concatenate(dvgn_g, axis=0))
            dua_parts.append(jnp.concatenate(dua_g, axis=0))
        dbst_ref[...] += dbst
        dvgn = jnp.concatenate(dvgn_parts, axis=1)
        dua = jnp.concatenate(dua_parts, axis=1)
        dlng_ref[...] += _fold8(dvgn * xhat)
        dlnb_ref[...] += _fold8(dvgn)
        dvga = _ln_bwd(dvgn * lng_v, xhat, rstd)
        dp_ref[:, d:2 * d] = (dua * _gelu_grad(up)).astype(dp_ref.dtype)
        dp_ref[:, 2 * d:] = (dvga * _gelu_grad(vgp)).astype(dp_ref.dtype)

    res = pl.pallas_call(
        _carrying(body, 12, 7, 0, carried, (t // tb,)),
        name="gate_sgu_bwd_carrying" if carried else "gate_sgu_bwd", grid=(t // tb,),
        in_specs=[pl.BlockSpec((tb, d), lambda i: (i, 0)),
                  pl.BlockSpec((tb, d), lambda i: (i, 0)),
                  pl.BlockSpec((tb, d), lambda i: (i, 0)),
                  pl.BlockSpec((tb, d), lambda i: (i, 3)),
                  pl.BlockSpec((tb, d), lambda i: (i, 4)),
                  pl.BlockSpec((tb, d), lambda i: (i, 5)),
                  row_spec, row_spec, row_spec,
                  pl.BlockSpec((groups, SGU_BLOCK, SGU_BLOCK), lambda i: (0, 0, 0)),
                  pl.BlockSpec((SGU_BLOCK, LANES), lambda i: (0, 0)),
                  ANY] + [ANY] * len(extra.inputs),
        out_specs=[pl.BlockSpec((tb, d), lambda i: (i, 0)),
                   pl.BlockSpec((tb, 3 * d), lambda i: (i, 1)),
                   pl.BlockSpec((SUBLANES, DN_DK), lambda i: (0, 0)),
                   acc_row, acc_row,
                   pl.BlockSpec((groups, SGU_BLOCK, SGU_BLOCK), lambda i: (0, 0, 0)),
                   pl.BlockSpec((SGU_BLOCK, LANES), lambda i: (0, 0))] + [ANY] * len(extra.out_shapes),
        out_shape=[jax.ShapeDtypeStruct((t, d), F32),
                   jax.ShapeDtypeStruct(dprojm.shape, dprojm.dtype),
                   jax.ShapeDtypeStruct((SUBLANES, DN_DK), F32),
                   jax.ShapeDtypeStruct((SUBLANES, d), F32),
                   jax.ShapeDtypeStruct((SUBLANES, d), F32),
                   jax.ShapeDtypeStruct((groups, SGU_BLOCK, SGU_BLOCK), F32),
                   jax.ShapeDtypeStruct((SGU_BLOCK, LANES), F32)] + extra.out_shapes,
        input_output_aliases={11: 1},
        scratch_shapes=extra.scratch() if carried else [],
        compiler_params=_cparams(("arbitrary",)),
    )(dya, dyb, o, projm, projm, projm, onw, lng, lnb, ws, bst, dprojm, *extra.inputs)
    return res[:7], res[7:]


def _mix_fwd(ya, yb, projm, x, wpa, wpb, wo, g1, b1, d, tb):
    t = x.shape[0]
    blk = pl.BlockSpec((tb, d), lambda i: (i, 0))
    wspec = pl.BlockSpec((d, d), lambda i: (0, 0))
    row_spec = pl.BlockSpec((1, d), lambda i: (0, 0))

    def body(ya_ref, yb_ref, ga_ref, gb_ref, x_ref, wpa_ref, wpb_ref, wo_ref, g_ref, b_ref,
             pa_ref, pb_ref, m_ref, h_ref, x1_ref, x1b_ref):
        pa = _dot(ya_ref[...], wpa_ref[...])
        pb = _dot(yb_ref[...], wpb_ref[...])
        m = _sigmoid(ga_ref[...]) * pa + _sigmoid(gb_ref[...]) * pb
        hres = ALPHA * x_ref[...] + _dot(m, wo_ref[...])
        xhat, _ = _ln_hat(hres)
        x1 = xhat * g_ref[...] + b_ref[...]
        pa_ref[...] = pa.astype(pa_ref.dtype)
        pb_ref[...] = pb.astype(pb_ref.dtype)
        m_ref[...] = m.astype(m_ref.dtype)
        h_ref[...] = hres
        x1_ref[...] = x1
        x1b_ref[...] = x1.astype(x1b_ref.dtype)

    f32_out = jax.ShapeDtypeStruct((t, d), F32)
    bf_out = jax.ShapeDtypeStruct((t, d), ACT)
    return pl.pallas_call(
        body, name="mix_fwd", grid=(t // tb,),
        in_specs=[blk, blk, pl.BlockSpec((tb, d), lambda i: (i, 6)), pl.BlockSpec((tb, d), lambda i: (i, 7)),
                  blk, wspec, wspec, wspec, row_spec, row_spec],
        out_specs=[blk] * 6,
        out_shape=[bf_out, bf_out, bf_out, f32_out, f32_out, bf_out],
        compiler_params=_cparams(("parallel",)),
    )(ya, yb, projm, projm, x, wpa, wpb, wo, g1, b1)


def _mix_bwd(dmix, pa, pb, projm, wpa, wpb, wo, d, tb):
    t = dmix.shape[0]
    blk = pl.BlockSpec((tb, d), lambda i: (i, 0))
    wspec = pl.BlockSpec((d, d), lambda i: (0, 0))

    def body(dmix_ref, pa_ref, pb_ref, ga_ref, gb_ref, wpa_ref, wpb_ref, wo_ref,
             dpa_ref, dpb_ref, dya_ref, dyb_ref, dg_ref):
        dm = _dot(dmix_ref[...], wo_ref[...], NT)
        sa, sb = _sigmoid(ga_ref[...]), _sigmoid(gb_ref[...])
        dpa, dpb = dm * sa, dm * sb
        dpa_ref[...] = dpa.astype(dpa_ref.dtype)
        dpb_ref[...] = dpb.astype(dpb_ref.dtype)
        dg_ref[:, :d] = (dm * pa_ref[...].astype(F32) * sa * (1.0 - sa)).astype(dg_ref.dtype)
        dg_ref[:, d:] = (dm * pb_ref[...].astype(F32) * sb * (1.0 - sb)).astype(dg_ref.dtype)
        dya_ref[...] = _dot(dpa, wpa_ref[...], NT)
        dyb_ref[...] = _dot(dpb, wpb_ref[...], NT)

    return pl.pallas_call(
        body, name="mix_bwd", grid=(t // tb,),
        in_specs=[blk, blk, blk, pl.BlockSpec((tb, d), lambda i: (i, 6)), pl.BlockSpec((tb, d), lambda i: (i, 7)),
                  wspec, wspec, wspec],
        out_specs=[blk, blk, blk, blk, pl.BlockSpec((tb, 2 * d), lambda i: (i, 3))],
        out_shape=[jax.ShapeDtypeStruct((t, d), ACT), jax.ShapeDtypeStruct((t, d), ACT),
                   jax.ShapeDtypeStruct((t, d), F32), jax.ShapeDtypeStruct((t, d), F32),
                   jax.ShapeDtypeStruct((t, 8 * d), ACT)],
        compiler_params=_cparams(("parallel",)),
    )(dmix, pa, pb, projm, projm, wpa, wpb, wo)


def _ffn_tail_fwd(gu, wd, x1, g, b, tb):
    t, d = x1.shape
    f = wd.shape[0]
    fc = _tile(f, MM_TILE)
    blk = pl.BlockSpec((tb, d), lambda i: (i, 0))
    row_spec = pl.BlockSpec((1, d), lambda i: (0, 0))

    def body(gu_ref, wd_ref, x_ref, g_ref, b_ref, a_ref, h_ref, y_ref, yb_ref):
        ffn = jnp.zeros((tb, d), F32)
        for c in range(f // fc):
            gp = gu_ref[:, c * fc:(c + 1) * fc].astype(F32)
            act = (gp * _sigmoid(gp) * gu_ref[:, f + c * fc:f + (c + 1) * fc].astype(F32)).astype(a_ref.dtype)
            a_ref[:, c * fc:(c + 1) * fc] = act
            ffn = ffn + _dot(act, wd_ref[c * fc:(c + 1) * fc, :])
        hres = ALPHA * x_ref[...] + ffn
        xhat, _ = _ln_hat(hres)
        y = xhat * g_ref[...] + b_ref[...]
        h_ref[...] = hres
        y_ref[...] = y
        yb_ref[...] = y.astype(yb_ref.dtype)

    return pl.pallas_call(
        body, name="ffn_tail_fwd", grid=(t // tb,),
        in_specs=[pl.BlockSpec((tb, 2 * f), lambda i: (i, 0)), pl.BlockSpec((f, d), lambda i: (0, 0)),
                  blk, row_spec, row_spec],
        out_specs=[pl.BlockSpec((tb, f), lambda i: (i, 0)), blk, blk, blk],
        out_shape=[jax.ShapeDtypeStruct((t, f), ACT), jax.ShapeDtypeStruct((t, d), F32),
                   jax.ShapeDtypeStruct((t, d), F32), jax.ShapeDtypeStruct((t, d), ACT)],
        compiler_params=_cparams(("parallel",)),
    )(gu, wd, x1, g, b)


def _ffn_tail_bwd(dh, wd, gu, tb):
    t, d = dh.shape
    f = wd.shape[0]
    fc = _tile(f, MM_TILE)

    def body(dh_ref, wd_ref, gu_ref, dgu_ref):
        dh_v = dh_ref[...]
        for c in range(f // fc):
            da = _dot(dh_v, wd_ref[c * fc:(c + 1) * fc, :], NT)
            gp = gu_ref[:, c * fc:(c + 1) * fc].astype(F32)
            sg = _sigmoid(gp)
            dgu_ref[:, c * fc:(c + 1) * fc] = (
                da * gu_ref[:, f + c * fc:f + (c + 1) * fc].astype(F32) * sg * (1.0 + gp * (1.0 - sg))
            ).astype(dgu_ref.dtype)
            dgu_ref[:, f + c * fc:f + (c + 1) * fc] = (da * gp * sg).astype(dgu_ref.dtype)

    return pl.pallas_call(
        body, name="ffn_tail_bwd", grid=(t // tb,),
        in_specs=[pl.BlockSpec((tb, d), lambda i: (i, 0)), pl.BlockSpec((f, d), lambda i: (0, 0)),
                  pl.BlockSpec((tb, 2 * f), lambda i: (i, 0))],
        out_specs=pl.BlockSpec((tb, 2 * f), lambda i: (i, 0)),
        out_shape=jax.ShapeDtypeStruct((t, 2 * f), ACT),
        compiler_params=_cparams(("parallel",)),
    )(dh, wd, gu)


def _ffn_head_bwd(dgu, wgu, dh2, hres, g, tb):
    t, d = dh2.shape
    f2 = wgu.shape[1]
    blk = pl.BlockSpec((tb, d), lambda i: (i, 0))
    acc = pl.BlockSpec((SUBLANES, d), lambda i: (0, 0))

    def body(dgu_ref, w_ref, dh2_ref, h_ref, g_ref, dh_ref, dhb_ref, dg_ref, db_ref):
        @pl.when(pl.program_id(0) == 0)
        def _():
            dg_ref[...] = jnp.zeros_like(dg_ref)
            db_ref[...] = jnp.zeros_like(db_ref)

        dy_v = _dot(dgu_ref[...], w_ref[...], NT) + ALPHA * dh2_ref[...]
        xhat, r = _ln_hat(h_ref[...])
        dh = _ln_bwd(dy_v * g_ref[...], xhat, r)
        dh_ref[...] = dh
        dhb_ref[...] = dh.astype(dhb_ref.dtype)
        dg_ref[...] += _fold8(dy_v * xhat)
        db_ref[...] += _fold8(dy_v)

    return pl.pallas_call(
        body, name="ffn_head_bwd", grid=(t // tb,),
        in_specs=[pl.BlockSpec((tb, f2), lambda i: (i, 0)), pl.BlockSpec((d, f2), lambda i: (0, 0)),
                  blk, blk, pl.BlockSpec((1, d), lambda i: (0, 0))],
        out_specs=[blk, blk, acc, acc],
        out_shape=[jax.ShapeDtypeStruct((t, d), F32), jax.ShapeDtypeStruct((t, d), ACT),
                   jax.ShapeDtypeStruct((SUBLANES, d), F32), jax.ShapeDtypeStruct((SUBLANES, d), F32)],
        compiler_params=_cparams(("arbitrary",)),
    )(dgu, wgu, dh2, hres, g)


def _loss_ln_bwd(y, target, hres, g, tb):
    t, d = y.shape
    blk = pl.BlockSpec((tb, d), lambda i: (i, 0))
    acc = pl.BlockSpec((SUBLANES, d), lambda i: (0, 0))

    def body(y_ref, t_ref, h_ref, g_ref, dh_ref, dhb_ref, dg_ref, db_ref, l_ref):
        @pl.when(pl.program_id(0) == 0)
        def _():
            for r_ in (dg_ref, db_ref, l_ref):
                r_[...] = jnp.zeros_like(r_)

        err = y_ref[...] - t_ref[...]
        dy_v = err * (1.0 / d)
        sq = _fold8(err * err)
        part = sq[:, :LANES]
        for c in range(1, d // LANES):
            part = part + sq[:, c * LANES:(c + 1) * LANES]
        l_ref[...] += part
        xhat, r = _ln_hat(h_ref[...])
        dh = _ln_bwd(dy_v * g_ref[...], xhat, r)
        dh_ref[...] = dh
        dhb_ref[...] = dh.astype(dhb_ref.dtype)
        dg_ref[...] += _fold8(dy_v * xhat)
        db_ref[...] += _fold8(dy_v)

    res = pl.pallas_call(
        body, name="loss_ln_bwd", grid=(t // tb,),
        in_specs=[blk, blk, blk, pl.BlockSpec((1, d), lambda i: (0, 0))],
        out_specs=[blk, blk, acc, acc, pl.BlockSpec((SUBLANES, LANES), lambda i: (0, 0))],
        out_shape=[jax.ShapeDtypeStruct((t, d), F32), jax.ShapeDtypeStruct((t, d), ACT),
                   jax.ShapeDtypeStruct((SUBLANES, d), F32), jax.ShapeDtypeStruct((SUBLANES, d), F32),
                   jax.ShapeDtypeStruct((SUBLANES, LANES), F32)],
        compiler_params=_cparams(("arbitrary",)),
    )(y, target, hres, g)
    return res[:4], res[4]


def _adamw(w, g, m, v):
    shape = w.shape
    cols = shape[-1]
    w2, g2, m2, v2 = (a.reshape(-1, cols) for a in (w, g, m, v))
    rows = w2.shape[0]
    tr = _tile(rows, 256, SUBLANES)
    blk = pl.BlockSpec((tr, cols), lambda i: (i, 0))

    def body(w_ref, g_ref, m_ref, v_ref, d_ref, nm_ref, nv_ref):
        g_v = g_ref[...]
        nm = ADAM_B1 * m_ref[...] + (1.0 - ADAM_B1) * g_v
        nv = ADAM_B2 * v_ref[...] + (1.0 - ADAM_B2) * (g_v * g_v)
        m_hat = nm / (1.0 - ADAM_B1 ** ADAM_STEP)
        v_hat = nv / (1.0 - ADAM_B2 ** ADAM_STEP)
        d_ref[...] = -ADAM_LR * (m_hat / (jnp.sqrt(v_hat) + ADAM_EPS) + ADAM_WD * w_ref[...])
        nm_ref[...] = nm
        nv_ref[...] = nv

    out = jax.ShapeDtypeStruct((rows, cols), F32)
    res = pl.pallas_call(
        body, name="adamw", grid=(rows // tr,),
        in_specs=[blk] * 4, out_specs=[blk] * 3, out_shape=[out] * 3,
        compiler_params=_cparams(("parallel",)),
    )(w2, g2, m2, v2)
    return tuple(r.reshape(shape) for r in res)


def _place():
    x, y, c = lax.axis_index("x"), lax.axis_index("y"), lax.axis_index("c")
    return x, y, c, [(1 - x, y), (x, 1 - y), (1 - x, 1 - y)]


def _remote(src, dst, send_sems, recv_sems, k, to):
    return pltpu.make_async_remote_copy(src_ref=src, dst_ref=dst, send_sem=send_sems.at[k],
                                        recv_sem=recv_sems.at[k], device_id=to, device_id_type=MESH)


class _Carried:
    def __init__(self, inputs, out_shapes, n_sems, copies):
        self.inputs, self.out_shapes, self.n_sems, self.copies = list(inputs), list(out_shapes), n_sems, copies

    def scratch(self):
        return [pltpu.SemaphoreType.DMA((self.n_sems,)), pltpu.SemaphoreType.DMA((self.n_sems,))]


def _join_plans(first, second):
    ni, no, ns = len(first.inputs), len(first.out_shapes), first.n_sems

    def copies(in_refs, out_refs, send_sems, recv_sems):
        one = _phases(first.copies(in_refs[:ni], out_refs[:no], send_sems, recv_sems))
        two = _phases(second.copies(in_refs[ni:], out_refs[no:], send_sems.at[pl.ds(ns, second.n_sems)],
                                    recv_sems.at[pl.ds(ns, second.n_sems)]))

        def both(k):
            def run():
                one[k]()
                two[k]()
            return run

        return both(0), both(1), both(2)

    return _Carried(first.inputs + second.inputs, first.out_shapes + second.out_shapes, ns + second.n_sems, copies)


def _run_comm(name, plan):
    n_in, n_out = len(plan.inputs), len(plan.out_shapes)

    def body(*refs):
        for phase in _phases(plan.copies(refs[:n_in], refs[n_in:n_in + n_out], refs[-2], refs[-1])):
            phase()

    return pl.pallas_call(
        body, name=name, in_specs=[ANY] * n_in, out_specs=[ANY] * n_out, out_shape=plan.out_shapes,
        scratch_shapes=plan.scratch(),
    )(*plan.inputs)


def _half_rows(rows, core):
    if rows % (4 * SUBLANES):
        return None
    return pl.ds(pl.multiple_of(core * (rows // 2), 2 * SUBLANES), rows // 2)


def _all_gather_plan(shards):
    n = len(shards)

    def copies(x_refs, out_refs, send_sems, recv_sems):
        x, y, c, chips = _place()
        sibling = (x, y, 1 - c)
        mine = 2 * x + y
        split = [_half_rows(x_refs[t].shape[0], c) is not None for t in range(n)]

        def src(t):
            return x_refs[t].at[_half_rows(x_refs[t].shape[0], c)] if split[t] else x_refs[t]

        def slot(t, chip_idx, core):
            rows = _half_rows(x_refs[t].shape[0], core)
            return out_refs[t].at[chip_idx, rows] if split[t] else out_refs[t].at[chip_idx]

        def first():
            return [_remote(src(t), slot(t, mine, c), send_sems, recv_sems, 6 * t + j, (cx, cy, c))
                    for j, (cx, cy) in enumerate(chips) for t in range(n)]

        def start():
            for cp in first():
                cp.start()

        def passed():
            return [_remote(slot(t, 2 * cx + cy, c), slot(t, 2 * cx + cy, c), send_sems, recv_sems, 6 * t + 3 + j,
                            sibling) for j, (cx, cy) in enumerate(chips) for t in range(n) if split[t]]

        def middle():
            for j, (cx, cy) in enumerate(chips):
                for t in range(n):
                    theirs = slot(t, 2 * cx + cy, c)
                    _remote(theirs, theirs, send_sems, recv_sems, 6 * t + j, (cx, cy, c)).wait_recv()
            for cp in passed():
                cp.start()

        def finish():
            for j, (cx, cy) in enumerate(chips):
                for t in range(n):
                    if split[t]:
                        other = slot(t, 2 * cx + cy, 1 - c)
                        _remote(other, other, send_sems, recv_sems, 6 * t + 3 + j, sibling).wait_recv()
            for cp in first() + passed():
                cp.wait_send()

        return start, middle, finish

    return _Carried(shards, [jax.ShapeDtypeStruct((N_CHIPS,) + s.shape, s.dtype) for s in shards], 6 * n, copies)


def _sibling_exchange_plan(grads, small=None):
    n = len(grads)
    extra = [] if small is None else [small]

    def copies(in_refs, out_refs, send_sems, recv_sems):
        x, y, c, _ = _place()
        sibling = (x, y, 1 - c)

        def all_copies():
            cps = [_remote(in_refs[t].at[:, _half_rows(in_refs[t].shape[1], 1 - c), :], out_refs[t],
                           send_sems, recv_sems, t, sibling) for t in range(n)]
            if extra:
                cps.append(_remote(in_refs[n], out_refs[n], send_sems, recv_sems, n, sibling))
            return cps

        def start():
            for cp in all_copies():
                cp.start()

        def finish():
            for cp in all_copies():
                cp.wait()

        return start, finish

    shapes = [jax.ShapeDtypeStruct((g.shape[0], g.shape[1] // 2, g.shape[2]), g.dtype) for g in grads]
    shapes += [jax.ShapeDtypeStruct(s.shape, s.dtype) for s in extra]
    return _Carried(list(grads) + extra, shapes, n + 1, copies)


def _chip_exchange_plan(travel, small=None):
    n = len(travel)
    extra = [] if small is None else [small]

    def copies(in_refs, out_refs, send_sems, recv_sems):
        x, y, c, chips = _place()
        mine = 2 * x + y

        def all_copies():
            cps = []
            for j, (cx, cy) in enumerate(chips):
                to = (cx, cy, c)
                for t in range(n):
                    cps.append(_remote(in_refs[t].at[2 * cx + cy], out_refs[t].at[mine], send_sems, recv_sems,
                                       3 * t + j, to))
                if extra:
                    cps.append(_remote(in_refs[n], out_refs[n].at[mine], send_sems, recv_sems, 3 * n + j, to))
            return cps

        def start():
            for cp in all_copies():
                cp.start()

        def finish():
            for cp in all_copies():
                cp.wait()

        return start, finish

    shapes = [jax.ShapeDtypeStruct(g.shape, g.dtype) for g in travel]
    shapes += [jax.ShapeDtypeStruct((N_CHIPS,) + s.shape, s.dtype) for s in extra]
    return _Carried(list(travel) + extra, shapes, 3 * n + 3, copies)


def _sibling_merge_plan(reduced):
    n = len(reduced)

    def copies(in_refs, out_refs, send_sems, recv_sems):
        x, y, c, _ = _place()

        def all_copies():
            return [_remote(in_refs[t], out_refs[t], send_sems, recv_sems, t, (x, y, 1 - c)) for t in range(n)]

        def start():
            for cp in all_copies():
                cp.start()

        def finish():
            for cp in all_copies():
                cp.wait()

        return start, finish

    return _Carried(reduced, [jax.ShapeDtypeStruct(r.shape, r.dtype) for r in reduced], n, copies)


def _pair_sum(place, grad, land):
    n, r, c = grad.shape
    half = r // 2
    tr = _tile(half, 256, SUBLANES)
    nb = half // tr

    def body(place_ref, a_ref, b_ref, travel_ref, own_ref):
        total = a_ref[0] + b_ref[0]
        travel_ref[0] = total.astype(travel_ref.dtype)

        @pl.when(pl.program_id(1) == place_ref[1])
        def _():
            own_ref[...] = total

    return pl.pallas_call(
        body, name="grad_pair_sum",
        grid_spec=pltpu.PrefetchScalarGridSpec(
            num_scalar_prefetch=1, grid=(nb, n),
            in_specs=[pl.BlockSpec((1, tr, c), lambda i, s, p: (s, p[0] * nb + i, 0)),
                      pl.BlockSpec((1, tr, c), lambda i, s, p: (s, i, 0))],
            out_specs=[pl.BlockSpec((1, tr, c), lambda i, s, p: (s, i, 0)),
                       pl.BlockSpec((tr, c), lambda i, s, p: (i, 0))]),
        out_shape=[jax.ShapeDtypeStruct((n, half, c), BF16), jax.ShapeDtypeStruct((half, c), F32)],
        compiler_params=_cparams(("parallel", "arbitrary")),
    )(place, grad, land)


def _chip_sum(place, own, land, name):
    n, r, c = land.shape
    tr = _tile(r, 256, SUBLANES)

    def body(place_ref, own_ref, land_ref, o_ref):
        mine = place_ref[1]
        acc = jnp.zeros(o_ref.shape, F32)
        for s in range(n):
            acc = acc + jnp.where(mine == s, own_ref[...], land_ref[s].astype(F32))
        o_ref[...] = acc

    return pl.pallas_call(
        body, name=name,
        grid_spec=pltpu.PrefetchScalarGridSpec(
            num_scalar_prefetch=1, grid=(r // tr,),
            in_specs=[pl.BlockSpec((tr, c), lambda i, p: (i, 0)),
                      pl.BlockSpec((n, tr, c), lambda i, p: (0, i, 0))],
            out_specs=pl.BlockSpec((tr, c), lambda i, p: (i, 0))),
        out_shape=jax.ShapeDtypeStruct((r, c), F32),
        compiler_params=_cparams(("parallel",)),
    )(place, own, land)


def _add2(a, b):
    rows = a.shape[0]
    tr = _tile(rows, 256, SUBLANES)
    blk = pl.BlockSpec((tr, a.shape[1]), lambda i: (i, 0))

    def body(a_ref, b_ref, o_ref):
        o_ref[...] = a_ref[...] + b_ref[...]

    return pl.pallas_call(
        body, name="grad_small_pair_sum", grid=(rows // tr,), in_specs=[blk, blk], out_specs=blk,
        out_shape=jax.ShapeDtypeStruct(a.shape, F32), compiler_params=_cparams(("parallel",)),
    )(a, b)


def _merge_halves(place, mine, other):
    first_core = place[0] == 0
    return jnp.concatenate([jnp.where(first_core, mine, other), jnp.where(first_core, other, mine)], axis=0)


_BIG = (("w_in", 2), ("w_pa", 1), ("w_pb", 1), ("w_o", 1), ("w_ffn_gate", 2), ("w_ffn_up", 2),
        ("w_ffn_down", 1))
_SMALL = ("conv_w", "a_log", "dt_bias", "o_norm_w", "sgu_ln_g", "sgu_ln_b", "w_s", "b_s",
          "ln1_g", "ln1_b", "ln2_g", "ln2_b")


def _pack_small(arrays):
    pieces = []
    for a in arrays:
        if a.shape[-1] % LANES == 0:
            a2 = a.reshape(-1, LANES)
        else:
            a2 = jnp.pad(a.reshape(-1, a.shape[-1]), ((0, 0), (0, LANES - a.shape[-1])))
        pieces.append(jnp.pad(a2, ((0, -a2.shape[0] % SUBLANES), (0, 0))))
    return jnp.concatenate(pieces, axis=0)


def _unpack_small(buf, like):
    out, off = [], 0
    for a in like:
        if a.shape[-1] % LANES == 0:
            rows = a.size // LANES
            out.append(buf[off:off + rows].reshape(a.shape))
        else:
            rows = a.size // a.shape[-1]
            out.append(buf[off:off + rows, :a.shape[-1]].reshape(a.shape))
        off += -(-rows // SUBLANES) * SUBLANES
    return out


def _unshard(gathered, local, chip, axis):
    parts = [jnp.where(chip == s, local, gathered[s]) for s in range(N_CHIPS)]
    return jnp.concatenate(parts, axis=axis - 1)


def _to_shards(full, axis):
    l, r, c = full.shape
    if axis == 1:
        return full.reshape(l, N_CHIPS, r // N_CHIPS, c)
    return jnp.transpose(full.reshape(l, r, N_CHIPS, c // N_CHIPS), (0, 2, 1, 3))


def _row(v, width=None):
    v = v.reshape(1, -1).astype(F32)
    if width is not None and v.shape[1] < width:
        v = jnp.pad(v, ((0, 0), (0, width - v.shape[1])))
    return v


def _layer_consts(p, l, d):
    heads = d // DN_DK
    return dict(
        alog=_row(p["a_log"][l], LANES), dtb=_row(p["dt_bias"][l], LANES),
        onw=_row(jnp.tile(p["o_norm_w"][l], heads)),
        lng=_row(p["sgu_ln_g"][l]), lnb=_row(p["sgu_ln_b"][l]),
        ws=p["w_s"][l].astype(F32),
        bst=jnp.pad(p["b_s"][l].T, ((0, 0), (0, LANES - p["b_s"].shape[1]))),
        g1=_row(p["ln1_g"][l]), b1=_row(p["ln1_b"][l]), g2=_row(p["ln2_g"][l]), b2=_row(p["ln2_b"][l]))


class _NoComm:
    def with_proj_main(self):
        return None

    def after_proj_main(self, got):
        pass

    def weights(self, full):
        return full

    def with_dn_fwd(self):
        return None

    def after_dn_fwd(self, got):
        pass

    def with_ffn_in_dw(self):
        return None

    def after_ffn_in_dw(self, got):
        pass

    def after_branch_grads(self, g):
        pass

    def with_dn_bwd(self):
        return None

    def after_dn_bwd(self, got):
        pass

    def with_proj_main_dw(self):
        return None

    def after_proj_main_dw(self, got):
        pass

    def with_ffn_in(self):
        return None

    def after_ffn_in(self, got):
        pass

    def after_all_grads(self, g):
        pass

    def with_gate_sgu_bwd(self):
        return None

    def after_gate_sgu_bwd(self, got):
        pass

    def with_proj_gates_dx(self):
        return None

    def after_proj_gates_dx(self, got):
        pass

    def with_proj_main_dx(self):
        return None

    def after_proj_main_dx(self, got):
        pass


def _carry(carried, after, call, *args, **kw):
    if carried is None:
        return call(*args, **kw)
    out, got = call(*args, carried=carried, **kw)
    after(got)
    return out


def _in_proj_weights(w_in, d):
    heads, q4 = d // DN_DK, 4 * d
    wba = jnp.zeros((d, 2 * LANES), w_in.dtype)
    wba = wba.at[:, :heads].set(w_in[:, q4:q4 + heads])
    wba = wba.at[:, LANES:LANES + heads].set(w_in[:, q4 + heads:q4 + 2 * heads])
    return jnp.concatenate([w_in[:, :q4], w_in[:, q4 + 2 * heads:]], axis=1), wba


def _layer_fwd(x, xb, full, cl, d, tb, comm):
    wm, wba = _in_proj_weights(full["w_in"], d)
    projm = _carry(comm.with_proj_main(), comm.after_proj_main, _matmul, xb, wm, NN, "proj_main", tn=MM_WIDE)
    full = comm.weights(full)
    wl = dict(wm=wm, wba=wba, conv=full["conv_w"], wpa=full["w_pa"], wpb=full["w_pb"], wo=full["w_o"],
              wgu=jnp.concatenate([full["w_ffn_gate"], full["w_ffn_up"]], axis=1), wd=full["w_ffn_down"])
    ba = _matmul(xb, wba, NN, "proj_gates")
    qkv = _conv_fwd(projm, wl["conv"], d, _tile(x.shape[0], 2 * tb, SUBLANES))
    (o, states, ycors), got = _dn_fwd(qkv, ba, cl["alog"], cl["dtb"], d, comm.with_dn_fwd())
    comm.after_dn_fwd(got)
    ya, yb = _gate_sgu_fwd(o, projm, cl["onw"], cl["lng"], cl["lnb"], cl["ws"], cl["bst"], d)
    pa, pb, m, h1, x1, x1b = _mix_fwd(ya, yb, projm, x, wl["wpa"], wl["wpb"], wl["wo"], cl["g1"], cl["b1"], d, tb)
    gu = _carry(comm.with_ffn_in(), comm.after_ffn_in, _matmul, x1b, wl["wgu"], NN, "ffn_in", out_dtype=ACT,
                tn=2 * MM_TILE)
    act, h2, x2, x2b = _ffn_tail_fwd(gu, wl["wd"], x1, cl["g2"], cl["b2"], tb)
    saved = dict(xb=xb, projm=projm, ba=ba, qkv=qkv, o=o, states=states, ycors=ycors, ya=ya, yb=yb,
                 pa=pa, pb=pb, m=m, h1=h1, x1b=x1b, gu=gu, act=act, h2=h2)
    return x2, x2b, saved, wl


def _layer_bwd(sv, wl, cl, d, tb, comm, ln2_bwd, next_ln=None):
    g = {}
    dh2, dh2b, dg2, db2 = ln2_bwd
    g["ln2_g"], g["ln2_b"] = dg2.sum(0), db2.sum(0)
    g["wd"] = _matmul(sv["act"], dh2b, TN, "ffn_out_dw")
    dgu = _ffn_tail_bwd(dh2b, wl["wd"], sv["gu"], tb)
    g["wgu"] = _carry(comm.with_ffn_in_dw(), comm.after_ffn_in_dw, _matmul, sv["x1b"], dgu, TN, "ffn_in_dw")
    dh1, dh1b, dg1, db1 = _ffn_head_bwd(dgu, wl["wgu"], dh2, sv["h1"], cl["g1"], tb)
    g["ln1_g"], g["ln1_b"] = dg1.sum(0), db1.sum(0)
    g["wo"] = _matmul(sv["m"], dh1b, TN, "wo_dw")
    dpa, dpb, dya, dyb, dprojm = _mix_bwd(dh1b, sv["pa"], sv["pb"], sv["projm"], wl["wpa"], wl["wpb"], wl["wo"], d, tb)
    g["wpa"] = _matmul(sv["ya"], dpa, TN, "wpa_dw")
    g["wpb"] = _matmul(sv["yb"], dpb, TN, "wpb_dw")
    comm.after_branch_grads(g)
    (do, dprojm, donw, dlng, dlnb, dws, dbst), got = _gate_sgu_bwd(
        dya, dyb, sv["o"], sv["projm"], cl["onw"], cl["lng"], cl["lnb"], cl["ws"], cl["bst"], dprojm, d,
        comm.with_gate_sgu_bwd())
    comm.after_gate_sgu_bwd(got)
    heads, groups = d // DN_DK, d // SGU_GROUP_DIM
    g["o_norm_w"], g["sgu_ln_g"], g["sgu_ln_b"] = donw.sum(0), dlng.sum(0), dlnb.sum(0)
    g["w_s"], g["b_s"] = dws, dbst[:, :groups].T
    (dqkv, dba, dal, ddt), got = _dn_bwd(sv["qkv"], sv["ba"], cl["alog"], cl["dtb"], do, sv["states"],
                                         sv["ycors"], d, comm.with_dn_bwd())
    comm.after_dn_bwd(got)
    g["a_log"], g["dt_bias"] = dal.sum(0)[:heads], ddt.sum(0)[:heads]
    tbc = _tile(sv["xb"].shape[0], 2 * tb, SUBLANES)
    dy, dcw = _conv_bwd_dy(sv["projm"], wl["conv"], dqkv, d, tbc)
    g["conv_w"] = dcw.sum(1)
    dprojm = _conv_bwd_dx(dy, wl["conv"], dprojm, d, tbc)
    g["wba"] = _matmul(sv["xb"], dba, TN, "proj_gates_dw")
    g["wm"] = _carry(comm.with_proj_main_dw(), comm.after_proj_main_dw, _matmul, sv["xb"], dprojm, TN,
                     "proj_main_dw", tn=MM_WIDE)
    comm.after_all_grads(g)
    dx = _carry(comm.with_proj_gates_dx(), comm.after_proj_gates_dx, _matmul, dba, wl["wba"], NT, "proj_gates_dx",
                add=dh1, coef=ALPHA)
    if next_ln is not None:
        return _matmul(dprojm, wl["wm"], NT, "proj_main_dx", add=dx, tm=MM_TILE // 3, tk=MM_WIDE, ln=next_ln), g
    dx = _carry(comm.with_proj_main_dx(), comm.after_proj_main_dx, _matmul, dprojm, wl["wm"], NT, "proj_main_dx",
                add=dx, tk=MM_WIDE)
    return dx, g


_BRANCH = ("w_pa", "w_pb", "w_o", "w_ffn_gate", "w_ffn_up", "w_ffn_down")


def _grad_shards(g, d, keys):
    heads, q4 = d // DN_DK, 4 * d
    rows = lambda a: a.reshape(N_CHIPS, -1, a.shape[1])
    out = {}
    if "w_in" in keys:
        gm, gba, wsh = g["wm"], g["wba"], 2 * d + heads // 2
        out["w_in"] = jnp.stack([gm[:, :wsh],
                                 jnp.concatenate([gm[:, wsh:q4], gba[:, :heads]], axis=1),
                                 jnp.concatenate([gba[:, LANES:LANES + heads], gm[:, q4:q4 + wsh - heads]], axis=1),
                                 gm[:, q4 + wsh - heads:]])
    if "w_pa" in keys:
        ggu = g["wgu"]
        f = ggu.shape[1] // 2
        fs = f // N_CHIPS
        out.update({
            "w_pa": rows(g["wpa"]), "w_pb": rows(g["wpb"]), "w_o": rows(g["wo"]), "w_ffn_down": rows(g["wd"]),
            "w_ffn_gate": jnp.stack([ggu[:, s * fs:(s + 1) * fs] for s in range(N_CHIPS)]),
            "w_ffn_up": jnp.stack([ggu[:, f + s * fs:f + (s + 1) * fs] for s in range(N_CHIPS)])})
    return out


def _local_step(x, target, full0, full1_of, small_w, comm0=None):
    t, d = x.shape
    tb = _tile(t, 256, SUBLANES)
    comm0 = comm0 or _NoComm()
    consts = [_layer_consts(small_w, l, d) for l in range(DEPTH)]
    x1, x1b, sv0, w0 = _layer_fwd(x, x.astype(ACT), full0, consts[0], d, tb, comm0)
    x2, _, sv1, w1 = _layer_fwd(x1, x1b, full1_of(), consts[1], d, tb, _NoComm())
    ln2_bwd, loss_parts = _loss_ln_bwd(x2, target, sv1["h2"], consts[1]["g2"], tb)
    ln2_bwd, g1 = _layer_bwd(sv1, w1, consts[1], d, tb, _NoComm(), ln2_bwd, next_ln=(sv0["h2"], consts[0]["g2"]))
    comm0.layer1_grads = g1
    grad_x, g0 = _layer_bwd(sv0, w0, consts[0], d, tb, comm0, ln2_bwd)
    return loss_parts, grad_x, [g0, g1]


def kernel(x, w_in, conv_w, a_log, dt_bias, o_norm_w, sgu_ln_g, sgu_ln_b, w_s, b_s, w_pa, w_pb, w_o, ln1_g, ln1_b, w_ffn_gate, w_ffn_up, w_ffn_down, ln2_g, ln2_b, loss_target, m_w_in, m_conv_w, m_a_log, m_dt_bias, m_o_norm_w, m_sgu_ln_g, m_sgu_ln_b, m_w_s, m_b_s, m_w_pa, m_w_pb, m_w_o, m_ln1_g, m_ln1_b, m_w_ffn_gate, m_w_ffn_up, m_w_ffn_down, m_ln2_g, m_ln2_b, v_w_in, v_conv_w, v_a_log, v_dt_bias, v_o_norm_w, v_sgu_ln_g, v_sgu_ln_b, v_w_s, v_b_s, v_w_pa, v_w_pb, v_w_o, v_ln1_g, v_ln1_b, v_w_ffn_gate, v_w_ffn_up, v_w_ffn_down, v_ln2_g, v_ln2_b):
    names = ("w_in", "conv_w", "a_log", "dt_bias", "o_norm_w", "sgu_ln_g", "sgu_ln_b", "w_s", "b_s", "w_pa",
             "w_pb", "w_o", "ln1_g", "ln1_b", "w_ffn_gate", "w_ffn_up", "w_ffn_down", "ln2_g", "ln2_b")
    w = dict(zip(names, (w_in, conv_w, a_log, dt_bias, o_norm_w, sgu_ln_g, sgu_ln_b, w_s, b_s, w_pa, w_pb, w_o,
                         ln1_g, ln1_b, w_ffn_gate, w_ffn_up, w_ffn_down, ln2_g, ln2_b)))
    mom = dict(zip(names, (m_w_in, m_conv_w, m_a_log, m_dt_bias, m_o_norm_w, m_sgu_ln_g, m_sgu_ln_b, m_w_s, m_b_s,
                           m_w_pa, m_w_pb, m_w_o, m_ln1_g, m_ln1_b, m_w_ffn_gate, m_w_ffn_up, m_w_ffn_down,
                           m_ln2_g, m_ln2_b)))
    var = dict(zip(names, (v_w_in, v_conv_w, v_a_log, v_dt_bias, v_o_norm_w, v_sgu_ln_g, v_sgu_ln_b, v_w_s, v_b_s,
                           v_w_pa, v_w_pb, v_w_o, v_ln1_g, v_ln1_b, v_w_ffn_gate, v_w_ffn_up, v_w_ffn_down,
                           v_ln2_g, v_ln2_b)))
    chip = 2 * lax.axis_index("x") + lax.axis_index("y")
    place = jnp.stack([lax.axis_index("c"), chip]).astype(jnp.int32)

    big = [k for k, _ in _BIG]
    axis_of = dict(_BIG)
    local = {k: w[k].astype(BF16) for k in big}
    local["conv_w"] = conv_w

    def gather_plan(l, keys):
        return _all_gather_plan([local[k][l] for k in keys])

    def full_of(l, keys, gathered):
        return {k: _unshard(gt, local[k][l], chip, axis_of.get(k, 2)) for k, gt in zip(keys, gathered)}

    def pair_sums(grads_l, keys, lands):
        return [_pair_sum(place, grads_l[k], land) for k, land in zip(keys, lands)]

    def chip_sums(pairs, lands):
        return [_chip_sum(place, p[1], land, "grad_chip_sum") for p, land in zip(pairs, lands)]

    class Layer0Comm(_NoComm):
        def with_proj_main(self):
            return gather_plan(0, _BRANCH)

        def after_proj_main(self, got):
            self.rest = full_of(0, _BRANCH, got)

        def weights(self, full):
            return {**full, **self.rest}

        def with_dn_fwd(self):
            return gather_plan(1, mixer)

        def after_dn_fwd(self, got):
            self.full1 = full_of(1, mixer, got)

        def with_ffn_in(self):
            return gather_plan(1, ffn)

        def after_ffn_in(self, got):
            self.full1.update(full_of(1, ffn, got))

        def with_ffn_in_dw(self):
            self.g1 = _grad_shards(self.layer1_grads, x.shape[-1], big)
            return _sibling_exchange_plan([self.g1[k] for k in big])

        def after_ffn_in_dw(self, got):
            self.pairs1 = pair_sums(self.g1, big, got)

        def with_dn_bwd(self):
            return _chip_exchange_plan([p[0] for p in self.pairs1])

        def after_dn_bwd(self, got):
            self.red1 = chip_sums(self.pairs1, got)

        def after_branch_grads(self, g0):
            self.shards0 = _grad_shards(g0, x.shape[-1], _BRANCH)

        def with_gate_sgu_bwd(self):
            return _sibling_exchange_plan([self.shards0[k] for k in _BRANCH])

        def after_gate_sgu_bwd(self, got):
            self.pairs0 = pair_sums(self.shards0, _BRANCH, got)

        def with_proj_main_dw(self):
            return _chip_exchange_plan([p[0] for p in self.pairs0])

        def after_proj_main_dw(self, got):
            self.red0 = chip_sums(self.pairs0, got)

        def after_all_grads(self, g0):
            self.g_in = _grad_shards(g0, x.shape[-1], ["w_in"])["w_in"]
            self.small_g = {k: jnp.stack([g0[k], self.layer1_grads[k]]) for k in _SMALL}
            self.small = _pack_small([self.small_g[k] for k in _SMALL])

        def with_proj_gates_dx(self):
            return _sibling_exchange_plan([self.g_in], self.small)

        def after_proj_gates_dx(self, got):
            self.pair_in = _pair_sum(place, self.g_in, got[0])
            self.small_chip = _add2(self.small, got[1])

        def with_proj_main_dx(self):
            return _join_plans(_chip_exchange_plan([self.pair_in[0]], self.small_chip),
                               _sibling_merge_plan(self.red0 + self.red1))

        def after_proj_main_dx(self, got):
            self.red_in = _chip_sum(place, self.pair_in[1], got[0], "grad_chip_sum")
            self.small_total = _chip_sum(place, self.small_chip, got[1], "grad_small_chip_sum")
            self.others = got[2:]

    comm = Layer0Comm()
    first, mixer, ffn = ["w_in", "conv_w"], ["w_in", "conv_w", "w_pa", "w_pb", "w_o"], list(_BRANCH[3:])
    full0 = full_of(0, first, _run_comm("all_gather_weights", gather_plan(0, first)))
    small_w = {k: w[k] for k in _SMALL if k != "conv_w"}
    loss_parts, grad_x, g = _local_step(x[0], loss_target[0], full0, lambda: comm.full1, small_w, comm)

    reduced = [comm.red_in] + comm.red0 + comm.red1
    others = list(_run_comm("grad_sibling_merge", _sibling_merge_plan([comm.red_in]))) + list(comm.others)
    halves = [_merge_halves(place, mine, other) for mine, other in zip(reduced, others)]
    grads = {k: jnp.stack([halves[i], halves[len(big) + i]]) for i, k in enumerate(big)}
    grads.update(zip(_SMALL, _unpack_small(comm.small_total, [comm.small_g[k] for k in _SMALL])))
    grads["conv_w"] = lax.dynamic_index_in_dim(_to_shards(grads["conv_w"], 2), chip, 1, keepdims=False)

    delta, new_m, new_v = {}, {}, {}
    for k in [k for k, _ in _BIG] + ["conv_w"]:
        delta[k], new_m[k], new_v[k] = _adamw(w[k], grads[k], mom[k], var[k])
    rep = [k for k in _SMALL if k != "conv_w"]
    pack = lambda dct: _pack_small([dct[k] for k in rep])
    packed = _adamw(pack(w), pack(grads), pack(mom), pack(var))
    for dst, src in zip((delta, new_m, new_v), packed):
        dst.update(zip(rep, _unpack_small(src, [w[k] for k in rep])))

    loss = 0.5 * lax.psum(jnp.sum(loss_parts), ("x", "y", "c")) / x.shape[-1]
    return (loss, grad_x[None], *[grads[k] for k in names], *[delta[k] for k in names],
            *[new_m[k] for k in names], *[new_v[k] for k in names])
```

```python
import math

import jax
import jax.numpy as jnp
from jax import lax
from jax.experimental import pallas as pl
from jax.experimental.pallas import tpu as pltpu

F32 = jnp.float32
BF16 = jnp.bfloat16
MXU_DTYPE = jnp.bfloat16
ACT = jnp.bfloat16
HIGHEST = lax.Precision.HIGHEST

DEPTH = 2
CHUNK = 64
DN_GROUP = 2
DN_GROUP_FWD = 4
SGU_BLOCK = 128
SGU_WINDOWS = 4
CONV_K = 4
DN_DK = 128
SGU_GROUP_DIM = 128
LN_EPS = 1e-5
RMS_EPS = 1e-6
ALPHA = (2 * DEPTH) ** 0.25
ADAM_LR, ADAM_B1, ADAM_B2, ADAM_EPS, ADAM_WD, ADAM_STEP = 0.001, 0.9, 0.999, 1e-08, 0.01, 10

LANES = 128
SUBLANES = 8
VMEM_LIMIT = 52 * 2 ** 20
N_CHIPS = 4

NN = ((1,), (0,))
NT = ((1,), (1,))
TN = ((0,), (0,))
MESH = pl.DeviceIdType.MESH
ANY = pl.BlockSpec(memory_space=pl.ANY)


def _dot(a, b, dims=NN, prec=None):
    if prec is None:
        a = a.astype(MXU_DTYPE)
        b = b.astype(MXU_DTYPE)
    return lax.dot_general(a, b, (dims, ((), ())), preferred_element_type=F32, precision=prec)


def _cparams(sem=None):
    return pltpu.CompilerParams(dimension_semantics=sem, vmem_limit_bytes=VMEM_LIMIT)


def _tile(dim, pref, unit=LANES):
    t = (min(pref, dim) // unit) * unit
    while t >= unit:
        if dim % t == 0:
            return t
        t -= unit
    return dim


def _fold8(x):
    r, n = x.shape
    return x.reshape(r // SUBLANES, SUBLANES, n).sum(axis=0)


def _sigmoid(x):
    return 1.0 / (1.0 + jnp.exp(-x))


def _gelu(x):
    return 0.5 * x * (1.0 + lax.erf(x * (2.0 ** -0.5)))


def _gelu_grad(x):
    return 0.5 * (1.0 + lax.erf(x * (2.0 ** -0.5))) + x * jnp.exp(-0.5 * x * x) * (2.0 * math.pi) ** -0.5


def _ln_hat(h):
    mu = jnp.mean(h, axis=-1, keepdims=True)
    xc = h - mu
    var = jnp.mean(xc * xc, axis=-1, keepdims=True)
    r = lax.rsqrt(var + LN_EPS)
    return xc * r, r


def _ln_bwd(dxhat, xhat, r):
    return r * (dxhat - jnp.mean(dxhat, axis=-1, keepdims=True)
                - xhat * jnp.mean(dxhat * xhat, axis=-1, keepdims=True))


MM_TILE = 1536
MM_WIDE = 2048


def _matmul(a, b, dims, name, out_dtype=F32, add=None, coef=1.0, tm=MM_TILE, tn=MM_TILE, tk=MM_TILE, carried=None,
            ln=None):
    if dims == NN:
        (m, k), n = a.shape, b.shape[1]
    elif dims == NT:
        (m, k), n = a.shape, b.shape[0]
    else:
        (k, m), n = a.shape, b.shape[1]
    tm, tn, tk = _tile(m, tm), _tile(n, tn), _tile(k, tk)
    nk = k // tk
    a_spec = pl.BlockSpec((tk, tm), lambda j, i, q: (q, i)) if dims == TN else pl.BlockSpec((tm, tk), lambda j, i, q: (i, q))
    b_spec = pl.BlockSpec((tn, tk), lambda j, i, q: (j, q)) if dims == NT else pl.BlockSpec((tk, tn), lambda j, i, q: (q, j))
    o_spec = pl.BlockSpec((tm, tn), lambda j, i, q: (i, j))
    has_add = add is not None
    if ln is not None:
        assert n == tn and has_add and carried is None
        return _matmul_ln_bwd(a, b, dims, name, add, coef, ln, a_spec, b_spec, o_spec, (m, n, tm, tn, nk))

    def body(*refs):
        a_ref, b_ref = refs[0], refs[1]
        add_ref = refs[2] if has_add else None
        o_ref, acc_ref = refs[2 + has_add], refs[3 + has_add]
        q = pl.program_id(2)
        part = _dot(a_ref[...], b_ref[...], dims)

        def finish(r):
            if has_add:
                r = r + coef * add_ref[...]
            o_ref[...] = r.astype(out_dtype)

        if nk == 1:
            finish(part)
        else:
            @pl.when(q == 0)
            def _():
                acc_ref[...] = part

            @pl.when(q > 0)
            def _():
                acc_ref[...] += part

            @pl.when(q == nk - 1)
            def _():
                finish(acc_ref[...])

    ins = [a, b] + ([add] if has_add else [])
    in_specs = [a_spec, b_spec] + ([o_spec] if has_add else [])
    grid = (n // tn, m // tm, nk)
    acc = pltpu.VMEM((tm, tn) if nk > 1 else (SUBLANES, LANES), F32)
    out = jax.ShapeDtypeStruct((m, n), out_dtype)
    if carried is None:
        return pl.pallas_call(
            body, name=name, grid=grid, in_specs=in_specs, out_specs=o_spec, out_shape=out, scratch_shapes=[acc],
            compiler_params=_cparams(("parallel", "parallel", "arbitrary")),
        )(*ins)
    res = pl.pallas_call(
        _carrying(body, len(ins), 1, 1, carried, grid), name=name + "_carrying", grid=grid,
        in_specs=in_specs + [ANY] * len(carried.inputs), out_specs=[o_spec] + [ANY] * len(carried.out_shapes),
        out_shape=[out] + carried.out_shapes, scratch_shapes=[acc] + carried.scratch(),
        compiler_params=_cparams(("arbitrary", "arbitrary", "arbitrary")),
    )(*ins, *carried.inputs)
    return res[0], res[1:]


def _matmul_ln_bwd(a, b, dims, name, add, coef, ln, a_spec, b_spec, o_spec, sizes):
    m, n, tm, tn, nk = sizes
    hres, g = ln
    row = pl.BlockSpec((1, n), lambda j, i, q: (0, 0))
    sums = pl.BlockSpec((SUBLANES, n), lambda j, i, q: (0, 0))

    def body(a_ref, b_ref, add_ref, h_ref, g_ref, dh_ref, dhb_ref, dg_ref, db_ref, acc_ref):
        i, q = pl.program_id(1), pl.program_id(2)
        part = _dot(a_ref[...], b_ref[...], dims)

        @pl.when(jnp.logical_and(i == 0, q == 0))
        def _():
            dg_ref[...] = jnp.zeros_like(dg_ref)
            db_ref[...] = jnp.zeros_like(db_ref)

        @pl.when(q == 0)
        def _():
            acc_ref[...] = part

        @pl.when(q > 0)
        def _():
            acc_ref[...] += part

        @pl.when(q == nk - 1)
        def _():
            dy_v = acc_ref[...] + coef * add_ref[...]
            xhat, r = _ln_hat(h_ref[...])
            dh = _ln_bwd(dy_v * g_ref[...], xhat, r)
            dh_ref[...] = dh
            dhb_ref[...] = dh.astype(dhb_ref.dtype)
            dg_ref[...] += _fold8(dy_v * xhat)
            db_ref[...] += _fold8(dy_v)

    return pl.pallas_call(
        body, name=name + "_ln_bwd", grid=(1, m // tm, nk),
        in_specs=[a_spec, b_spec, o_spec, o_spec, row], out_specs=[o_spec, o_spec, sums, sums],
        out_shape=[jax.ShapeDtypeStruct((m, n), F32), jax.ShapeDtypeStruct((m, n), ACT),
                   jax.ShapeDtypeStruct((SUBLANES, n), F32), jax.ShapeDtypeStruct((SUBLANES, n), F32)],
        scratch_shapes=[pltpu.VMEM((tm, tn), F32)],
        compiler_params=_cparams(("arbitrary", "arbitrary", "arbitrary")),
    )(a, b, add, hres, g)


def _conv_taps(cur_ref, halo_ref, first):
    x = cur_ref[...]
    tb = x.shape[0]
    halo = jnp.where(first, 0.0, halo_ref[...])
    xc = jnp.concatenate([halo, x], axis=0)
    return [x] + [pltpu.roll(xc, s, 0)[SUBLANES:SUBLANES + tb] for s in range(1, CONV_K)]


def _conv_fwd(projm, conv_w, d, tb):
    t = projm.shape[0]
    heads = d // DN_DK
    hb = tb // SUBLANES

    def body(cur_ref, halo_ref, w_ref, o_ref):
        i, j = pl.program_id(0), pl.program_id(1)
        taps = _conv_taps(cur_ref, halo_ref, i == 0)
        y = taps[0] * w_ref[CONV_K - 1:CONV_K, :]
        for s in range(1, CONV_K):
            y = y + taps[s] * w_ref[CONV_K - 1 - s:CONV_K - s, :]
        act = y * _sigmoid(y)
        scale = jnp.where(j == 0, DN_DK ** -0.5, 1.0)
        for h in range(heads):
            seg = act[:, h * DN_DK:(h + 1) * DN_DK]
            r = lax.rsqrt(jnp.sum(seg * seg, axis=1, keepdims=True) + RMS_EPS) * scale
            o_ref[:, h * DN_DK:(h + 1) * DN_DK] = seg * jnp.where(j < 2, r, 1.0)

    blk = pl.BlockSpec((tb, d), lambda i, j: (i, j))
    return pl.pallas_call(
        body, name="conv_fwd", grid=(t // tb, 3),
        in_specs=[blk,
                  pl.BlockSpec((SUBLANES, d), lambda i, j: (jnp.maximum(i * hb - 1, 0), j)),
                  pl.BlockSpec((CONV_K, d), lambda i, j: (0, j))],
        out_specs=blk,
        out_shape=jax.ShapeDtypeStruct((t, 3 * d), F32),
        compiler_params=_cparams(("parallel", "parallel")),
    )(projm, projm, conv_w)


def _conv_bwd_dy(projm, conv_w, dqkv, d, tb):
    t = projm.shape[0]
    heads = d // DN_DK
    hb = tb // SUBLANES

    def body(cur_ref, halo_ref, w_ref, dout_ref, dy_ref, dw_ref):
        j, i = pl.program_id(0), pl.program_id(1)
        taps = _conv_taps(cur_ref, halo_ref, i == 0)
        y = taps[0] * w_ref[CONV_K - 1:CONV_K, :]
        for s in range(1, CONV_K):
            y = y + taps[s] * w_ref[CONV_K - 1 - s:CONV_K - s, :]
        sg = _sigmoid(y)
        act = y * sg
        dact = sg * (1.0 + y * (1.0 - sg))
        scale = jnp.where(j == 0, DN_DK ** -0.5, 1.0)
        for h in range(heads):
            cols = slice(h * DN_DK, (h + 1) * DN_DK)
            seg = act[:, cols]
            r = lax.rsqrt(jnp.sum(seg * seg, axis=1, keepdims=True) + RMS_EPS)
            nrm = seg * r
            dout = dout_ref[:, cols]
            ds = jnp.where(j < 2, (r * scale) * (dout - nrm * jnp.sum(dout * nrm, axis=1, keepdims=True)), dout)
            dy_ref[:, cols] = ds * dact[:, cols]
        dy = dy_ref[...]

        @pl.when(i == 0)
        def _():
            dw_ref[...] = jnp.zeros_like(dw_ref)

        for s in range(CONV_K):
            dw_ref[CONV_K - 1 - s] += _fold8(dy * taps[s])

    return pl.pallas_call(
        body, name="conv_bwd_dy", grid=(3, t // tb),
        in_specs=[pl.BlockSpec((tb, d), lambda j, i: (i, j)),
                  pl.BlockSpec((SUBLANES, d), lambda j, i: (jnp.maximum(i * hb - 1, 0), j)),
                  pl.BlockSpec((CONV_K, d), lambda j, i: (0, j)),
                  pl.BlockSpec((tb, d), lambda j, i: (i, j))],
        out_specs=[pl.BlockSpec((tb, d), lambda j, i: (i, j)),
                   pl.BlockSpec((CONV_K, SUBLANES, d), lambda j, i: (0, 0, j))],
        out_shape=[jax.ShapeDtypeStruct((t, 3 * d), F32),
                   jax.ShapeDtypeStruct((CONV_K, SUBLANES, 3 * d), F32)],
        compiler_params=_cparams(("parallel", "arbitrary")),
    )(projm, projm, conv_w, dqkv)


def _conv_bwd_dx(dy, conv_w, dprojm, d, tb):
    t = dy.shape[0]
    hb = tb // SUBLANES
    last = t // tb - 1

    def body(cur_ref, halo_ref, w_ref, alias_ref, o_ref):
        i = pl.program_id(0)
        cur = cur_ref[...]
        halo = jnp.where(i == last, 0.0, halo_ref[...])
        dc = jnp.concatenate([cur, halo], axis=0)
        acc = cur * w_ref[CONV_K - 1:CONV_K, :]
        for s in range(1, CONV_K):
            acc = acc + pltpu.roll(dc, tb + SUBLANES - s, 0)[:tb] * w_ref[CONV_K - 1 - s:CONV_K - s, :]
        o_ref[...] = acc.astype(o_ref.dtype)

    return pl.pallas_call(
        body, name="conv_bwd_dx", grid=(t // tb, 3),
        in_specs=[pl.BlockSpec((tb, d), lambda i, j: (i, j)),
                  pl.BlockSpec((SUBLANES, d), lambda i, j: (jnp.minimum((i + 1) * hb, t // SUBLANES - 1), j)),
                  pl.BlockSpec((CONV_K, d), lambda i, j: (0, j)),
                  ANY],
        out_specs=pl.BlockSpec((tb, d), lambda i, j: (i, j)),
        out_shape=jax.ShapeDtypeStruct(dprojm.shape, dprojm.dtype),
        input_output_aliases={3: 0},
        compiler_params=_cparams(("parallel", "parallel")),
    )(dy, dy, conv_w, dprojm)


def _beta_g(ba, alog, dtb):
    beta = _sigmoid(ba[:, :LANES])
    xa = ba[:, LANES:] + dtb
    softplus = jnp.maximum(xa, 0.0) + jnp.log(1.0 + jnp.exp(-jnp.abs(xa)))
    ea = jnp.exp(alog)
    return beta, -ea * softplus, ea, _sigmoid(xa)


def _inv_corrections(mats):
    ys = [-a for a in mats]
    ps = [_dot(a, a) for a in mats]
    steps = int(math.log2(CHUNK)) - 1
    for it in range(steps):
        ys = [y + p + _dot(y, p) for y, p in zip(ys, ps)]
        if it < steps - 1:
            ps = [_dot(p, p) for p in ps]
    return ys


def _chunk_masks():
    row = lax.broadcasted_iota(jnp.int32, (CHUNK, CHUNK), 0)
    col = lax.broadcasted_iota(jnp.int32, (CHUNK, CHUNK), 1)
    return row >= col, row > col, row <= col


def _col_of(mat, lane_idx, h):
    return jnp.sum(jnp.where(lane_idx == h, mat, 0.0), axis=1, keepdims=True)


def _row_of(mat, sub_idx, h):
    return jnp.sum(jnp.where(sub_idx == h, mat, 0.0), axis=0, keepdims=True)


def _phases(fns):
    return fns if len(fns) == 3 else (fns[0], lambda: None, fns[1])


def _carrying(compute, n_in, n_out, n_scratch, carried, grid):
    if carried is None:
        return compute
    ci, co = len(carried.inputs), len(carried.out_shapes)

    def body(*refs):
        ins, c_in = refs[:n_in], refs[n_in:n_in + ci]
        outs, c_out = refs[n_in + ci:n_in + ci + n_out], refs[n_in + ci + n_out:n_in + ci + n_out + co]
        scratch = refs[n_in + ci + n_out + co:]
        start, middle, finish = _phases(carried.copies(c_in, c_out, scratch[n_scratch], scratch[n_scratch + 1]))
        step, total = 0, 1
        for axis, steps in enumerate(grid):
            step = step * steps + pl.program_id(axis)
            total *= steps

        @pl.when(step == 0)
        def _():
            start()

        compute(*ins, *outs, *scratch[:n_scratch])

        @pl.when(step == (3 * total) // 4)
        def _():
            middle()

        @pl.when(step == total - 1)
        def _():
            finish()

    return body


def _dn_fwd(qkv, ba, alog, dtb, d, carried=None):
    t = qkv.shape[0]
    heads = d // DN_DK
    n_chunks = t // CHUNK
    grp = DN_GROUP_FWD if n_chunks % DN_GROUP_FWD == 0 else 1
    span = grp * CHUNK
    extra = carried or _Carried([], [], 0, None)

    def compute(qkv_ref, ba_ref, al_ref, dt_ref, o_ref, s_ref, y_ref, state):
        @pl.when(pl.program_id(0) == 0)
        def _():
            state[...] = jnp.zeros_like(state)

        tril, strict, _ = _chunk_masks()
        beta, g, _, _ = _beta_g(ba_ref[...], al_ref[...], dt_ref[...])
        lane = lax.broadcasted_iota(jnp.int32, (CHUNK, LANES), 1)
        sub = lax.broadcasted_iota(jnp.int32, (LANES, CHUNK), 0)
        rowc = lax.broadcasted_iota(jnp.int32, (CHUNK, 1), 0)
        hs = range(heads)
        units = [(c, h) for c in range(grp) for h in hs]
        un = range(len(units))
        rows = lambda c: slice(c * CHUNK, (c + 1) * CHUNK)
        gc = [_dot(jnp.where(tril, 1.0, 0.0), g[rows(c)], NN, HIGHEST) for c in range(grp)]
        gct = [m.T for m in gc]
        q = [qkv_ref[rows(c), h * DN_DK:(h + 1) * DN_DK] for c, h in units]
        k = [qkv_ref[rows(c), d + h * DN_DK:d + (h + 1) * DN_DK] for c, h in units]
        v = [qkv_ref[rows(c), 2 * d + h * DN_DK:2 * d + (h + 1) * DN_DK] for c, h in units]
        gch = [_col_of(gc[c], lane, h) for c, h in units]
        bh = [_col_of(beta[rows(c)], lane, h) for c, h in units]
        dec = [jnp.where(tril, jnp.exp(gch[n] - _row_of(gct[c], sub, h)), 0.0) for n, (c, h) in enumerate(units)]
        egc = [jnp.exp(gch[n]) for n in un]
        gl = [jnp.sum(jnp.where(rowc == CHUNK - 1, gch[n], 0.0), axis=0, keepdims=True) for n in un]
        kb = [k[n] * bh[n] for n in un]
        a = [jnp.where(strict, _dot(kb[n], k[n], NT) * dec[n], 0.0) for n in un]
        p = [_dot(q[n], k[n], NT) * dec[n] for n in un]
        ycor = _inv_corrections(a)
        rhs = [jnp.concatenate([v[n] * bh[n], kb[n] * egc[n]], axis=1) for n in un]
        sol = [rhs[n] + _dot(ycor[n], rhs[n]) for n in un]
        qg = [q[n] * egc[n] for n in un]
        kd = [k[n] * jnp.exp(gl[n] - gch[n]) for n in un]
        egl = [jnp.exp(gl[n]) for n in un]
        s_cur, s_in, o = [state[h] for h in hs], [], []
        for c in range(grp):
            ns = [c * heads + h for h in hs]
            vn = [sol[n][:, :DN_DK] - _dot(sol[n][:, DN_DK:], s_cur[h]) for h, n in enumerate(ns)]
            o += [_dot(qg[n], s_cur[h]) + _dot(p[n], vn[h]) for h, n in enumerate(ns)]
            s_in += s_cur
            s_cur = [s_cur[h] * egl[n] + _dot(kd[n], vn[h], TN) for h, n in enumerate(ns)]
        for n, (c, h) in enumerate(units):
            o_ref[rows(c), h * DN_DK:(h + 1) * DN_DK] = o[n]
            s_ref[c, h] = s_in[n]
            y_ref[h, rows(c), :] = ycor[n]
        for h in hs:
            state[h] = s_cur[h]

    res = pl.pallas_call(
        _carrying(compute, 4, 3, 1, carried, (n_chunks // grp,)),
        name="dn_fwd_carrying" if carried else "dn_fwd", grid=(n_chunks // grp,),
        in_specs=[pl.BlockSpec((span, 3 * d), lambda i: (i, 0)),
                  pl.BlockSpec((span, 2 * LANES), lambda i: (i, 0)),
                  pl.BlockSpec((1, LANES), lambda i: (0, 0)),
                  pl.BlockSpec((1, LANES), lambda i: (0, 0))] + [ANY] * len(extra.inputs),
        out_specs=[pl.BlockSpec((span, d), lambda i: (i, 0)),
                   pl.BlockSpec((grp, heads, DN_DK, DN_DK), lambda i: (i, 0, 0, 0)),
                   pl.BlockSpec((heads, span, CHUNK), lambda i: (0, i, 0))] + [ANY] * len(extra.out_shapes),
        out_shape=[jax.ShapeDtypeStruct((t, d), F32),
                   jax.ShapeDtypeStruct((n_chunks, heads, DN_DK, DN_DK), F32),
                   jax.ShapeDtypeStruct((heads, t, CHUNK), F32)] + extra.out_shapes,
        scratch_shapes=[pltpu.VMEM((heads, DN_DK, DN_DK), F32)] + (extra.scratch() if carried else []),
        compiler_params=_cparams(("arbitrary",)),
    )(qkv, ba, alog, dtb, *extra.inputs)
    return res[:3], res[3:]


def _dn_bwd(qkv, ba, alog, dtb, dout, states, ycors, d, carried=None):
    t = qkv.shape[0]
    heads = d // DN_DK
    n_chunks = t // CHUNK
    grp = DN_GROUP if n_chunks % DN_GROUP == 0 else 1
    span = grp * CHUNK
    rev = lambda i: n_chunks // grp - 1 - i
    extra = carried or _Carried([], [], 0, None)

    def compute(qkv_ref, ba_ref, al_ref, dt_ref, do_ref, s_ref, y_ref,
                dqkv_ref, dba_ref, dal_ref, ddt_ref, dstate):
        @pl.when(pl.program_id(0) == 0)
        def _():
            dstate[...] = jnp.zeros_like(dstate)
            dal_ref[...] = jnp.zeros_like(dal_ref)
            ddt_ref[...] = jnp.zeros_like(ddt_ref)

        tril, strict, triu = _chunk_masks()
        beta, g, ea, sig_a = _beta_g(ba_ref[...], al_ref[...], dt_ref[...])
        lane = lax.broadcasted_iota(jnp.int32, (CHUNK, LANES), 1)
        sub = lax.broadcasted_iota(jnp.int32, (LANES, CHUNK), 0)
        rowc = lax.broadcasted_iota(jnp.int32, (CHUNK, 1), 0)
        hs = range(heads)
        units = [(c, h) for c in range(grp) for h in hs]
        un = range(len(units))
        rows = lambda c: slice(c * CHUNK, (c + 1) * CHUNK)
        rsum = lambda x_: jnp.sum(x_, axis=1, keepdims=True)
        gc = [_dot(jnp.where(tril, 1.0, 0.0), g[rows(c)], NN, HIGHEST) for c in range(grp)]
        gct = [m.T for m in gc]
        q = [qkv_ref[rows(c), h * DN_DK:(h + 1) * DN_DK] for c, h in units]
        k = [qkv_ref[rows(c), d + h * DN_DK:d + (h + 1) * DN_DK] for c, h in units]
        v = [qkv_ref[rows(c), 2 * d + h * DN_DK:2 * d + (h + 1) * DN_DK] for c, h in units]
        dout_h = [do_ref[rows(c), h * DN_DK:(h + 1) * DN_DK] for c, h in units]
        s0 = [s_ref[c, h] for c, h in units]
        ycor = [y_ref[h, rows(c), :] for c, h in units]
        gch = [_col_of(gc[c], lane, h) for c, h in units]
        bh = [_col_of(beta[rows(c)], lane, h) for c, h in units]
        dec = [jnp.where(tril, jnp.exp(gch[n] - _row_of(gct[c], sub, h)), 0.0) for n, (c, h) in enumerate(units)]
        egc = [jnp.exp(gch[n]) for n in un]
        gl = [jnp.sum(jnp.where(rowc == CHUNK - 1, gch[n], 0.0), axis=0, keepdims=True) for n in un]
        egl = [jnp.exp(gl[n]) for n in un]
        ekd = [jnp.exp(gl[n] - gch[n]) for n in un]
        kb = [k[n] * bh[n] for n in un]
        kd = [k[n] * ekd[n] for n in un]
        qg = [q[n] * egc[n] for n in un]
        kbg = [kb[n] * egc[n] for n in un]
        a = [jnp.where(strict, _dot(kb[n], k[n], NT) * dec[n], 0.0) for n in un]
        p = [_dot(q[n], k[n], NT) * dec[n] for n in un]
        rhs = [jnp.concatenate([v[n] * bh[n], kbg[n]], axis=1) for n in un]
        sol = [rhs[n] + _dot(ycor[n], rhs[n]) for n in un]
        w = [sol[n][:, DN_DK:] for n in un]
        vn = [sol[n][:, :DN_DK] - _dot(w[n], s0[n]) for n in un]
        dqg = [_dot(dout_h[n], s0[n], NT) for n in un]
        dp = [jnp.where(tril, _dot(dout_h[n], vn[n], NT), 0.0) for n in un]
        pdo = [_dot(p[n], dout_h[n], TN) for n in un]
        qdo = [_dot(qg[n], dout_h[n], TN) for n in un]
        ds_cur = [dstate[h] for h in hs]
        dsn, dvn = [None] * len(units), [None] * len(units)
        for c in reversed(range(grp)):
            for h in hs:
                dsn[c * heads + h] = ds_cur[h]
            for h in hs:
                n = c * heads + h
                dvn[n] = pdo[n] + _dot(kd[n], ds_cur[h])
            ds_cur = [qdo[c * heads + h] + egl[c * heads + h] * ds_cur[h]
                      - _dot(w[c * heads + h], dvn[c * heads + h], TN) for h in hs]
        dkd = [_dot(vn[n], dsn[n], NT) for n in un]
        dw = [-_dot(dvn[n], s0[n], NT) for n in un]
        dgl = [jnp.sum(rsum(dsn[n] * s0[n]), axis=0, keepdims=True) * egl[n] for n in un]
        dsol = [jnp.concatenate([dvn[n], dw[n]], axis=1) for n in un]
        drhs = [dsol[n] + _dot(ycor[n], dsol[n], TN) for n in un]
        dvb = [drhs[n][:, :DN_DK] for n in un]
        dkbg = [drhs[n][:, DN_DK:] for n in un]
        da = [jnp.where(strict, -_dot(drhs[n], sol[n], NT), 0.0) for n in un]
        dma = [da[n] * dec[n] for n in un]
        dmp = [dp[n] * dec[n] for n in un]
        dkb = [_dot(dma[n], k[n]) + dkbg[n] * egc[n] for n in un]
        dq = [_dot(dmp[n], k[n]) + dqg[n] * egc[n] for n in un]
        dk = [_dot(dma[n], kb[n], TN) + _dot(dmp[n], q[n], TN) + dkd[n] * ekd[n] + dkb[n] * bh[n] for n in un]
        e = [da[n] * a[n] + dp[n] * p[n] for n in un]
        colsum = [jnp.sum(e[n], axis=0, keepdims=True) for n in un]
        tkd = [rsum(dkd[n] * kd[n]) for n in un]
        for n, (c, h) in enumerate(units):
            dqkv_ref[rows(c), h * DN_DK:(h + 1) * DN_DK] = dq[n]
            dqkv_ref[rows(c), d + h * DN_DK:d + (h + 1) * DN_DK] = dk[n]
            dqkv_ref[rows(c), 2 * d + h * DN_DK:2 * d + (h + 1) * DN_DK] = dvb[n] * bh[n]
        for h in hs:
            dstate[h] = ds_cur[h]
        valid = lane < heads
        dal_acc = jnp.zeros((SUBLANES, LANES), F32)
        ddt_acc = jnp.zeros((SUBLANES, LANES), F32)
        for c in range(grp):
            dgc_all = jnp.zeros((CHUNK, LANES), F32)
            dbeta_all = jnp.zeros((CHUNK, LANES), F32)
            colsums = jnp.zeros((LANES, CHUNK), F32)
            for h in hs:
                n = c * heads + h
                dgc = rsum(e[n]) + rsum(dqg[n] * qg[n]) - tkd[n] + rsum(dkbg[n] * kbg[n])
                dgc = dgc + jnp.where(rowc == CHUNK - 1, dgl[n] + jnp.sum(tkd[n], axis=0, keepdims=True), 0.0)
                dgc_all = dgc_all + jnp.where(lane == h, dgc, 0.0)
                colsums = colsums + jnp.where(sub == h, colsum[n], 0.0)
                dbeta_all = dbeta_all + jnp.where(lane == h, rsum(dkb[n] * k[n]) + rsum(dvb[n] * v[n]), 0.0)
            dg = _dot(jnp.where(triu, 1.0, 0.0), dgc_all - colsums.T, NN, HIGHEST)
            beta_c = beta[rows(c)]
            dbl = jnp.where(valid, dbeta_all * beta_c * (1.0 - beta_c), 0.0)
            dal = jnp.where(valid, -dg * ea * sig_a[rows(c)], 0.0)
            dba_ref[rows(c), :LANES] = dbl.astype(dba_ref.dtype)
            dba_ref[rows(c), LANES:] = dal.astype(dba_ref.dtype)
            dal_acc = dal_acc + _fold8(jnp.where(valid, dg * g[rows(c)], 0.0))
            ddt_acc = ddt_acc + _fold8(dal)
        dal_ref[...] += dal_acc
        ddt_ref[...] += ddt_acc

    res = pl.pallas_call(
        _carrying(compute, 7, 4, 1, carried, (n_chunks // grp,)),
        name="dn_bwd_carrying" if carried else "dn_bwd", grid=(n_chunks // grp,),
        in_specs=[pl.BlockSpec((span, 3 * d), lambda i: (rev(i), 0)),
                  pl.BlockSpec((span, 2 * LANES), lambda i: (rev(i), 0)),
                  pl.BlockSpec((1, LANES), lambda i: (0, 0)),
                  pl.BlockSpec((1, LANES), lambda i: (0, 0)),
                  pl.BlockSpec((span, d), lambda i: (rev(i), 0)),
                  pl.BlockSpec((grp, heads, DN_DK, DN_DK), lambda i: (rev(i), 0, 0, 0)),
                  pl.BlockSpec((heads, span, CHUNK), lambda i: (0, rev(i), 0))] + [ANY] * len(extra.inputs),
        out_specs=[pl.BlockSpec((span, 3 * d), lambda i: (rev(i), 0)),
                   pl.BlockSpec((span, 2 * LANES), lambda i: (rev(i), 0)),
                   pl.BlockSpec((SUBLANES, LANES), lambda i: (0, 0)),
                   pl.BlockSpec((SUBLANES, LANES), lambda i: (0, 0))] + [ANY] * len(extra.out_shapes),
        out_shape=[jax.ShapeDtypeStruct((t, 3 * d), F32),
                   jax.ShapeDtypeStruct((t, 2 * LANES), ACT),
                   jax.ShapeDtypeStruct((SUBLANES, LANES), F32),
                   jax.ShapeDtypeStruct((SUBLANES, LANES), F32)] + extra.out_shapes,
        scratch_shapes=[pltpu.VMEM((heads, DN_DK, DN_DK), F32)] + (extra.scratch() if carried else []),
        compiler_params=_cparams(("arbitrary",)),
    )(qkv, ba, alog, dtb, dout, states, ycors, *extra.inputs)
    return res[:4], res[4:]


def _sgu_mask():
    row = lax.broadcasted_iota(jnp.int32, (SGU_BLOCK, SGU_BLOCK), 0)
    col = lax.broadcasted_iota(jnp.int32, (SGU_BLOCK, SGU_BLOCK), 1)
    sh = int(math.log2(CHUNK))
    return lax.shift_right_logical(row, sh) >= lax.shift_right_logical(col, sh)


def _gate_sgu_fwd(o, projm, onw, lng, lnb, ws, bst, d):
    t = o.shape[0]
    heads, groups = d // DN_DK, d // SGU_GROUP_DIM
    tb = _tile(t, SGU_WINDOWS * SGU_BLOCK, SGU_BLOCK)
    row_spec = pl.BlockSpec((1, d), lambda i: (0, 0))

    def body(o_ref, z_ref, u_ref, v_ref, onw_ref, lng_ref, lnb_ref, ws_ref, bst_ref, ya_ref, yb_ref):
        for h in range(heads):
            cols = slice(h * DN_DK, (h + 1) * DN_DK)
            oh, zh = o_ref[:, cols], z_ref[:, cols]
            r = lax.rsqrt(jnp.mean(oh * oh, axis=1, keepdims=True) + RMS_EPS)
            ya_ref[:, cols] = (oh * r * onw_ref[:, cols] * (zh * _sigmoid(zh))).astype(ya_ref.dtype)
        xhat, _ = _ln_hat(_gelu(v_ref[...]))
        vgn = xhat * lng_ref[...] + lnb_ref[...]
        mask = _sgu_mask()
        lane = lax.broadcasted_iota(jnp.int32, (SGU_BLOCK, LANES), 1)
        bst_v = bst_ref[...]
        for gi in range(groups):
            cols = slice(gi * SGU_GROUP_DIM, (gi + 1) * SGU_GROUP_DIM)
            wsg = jnp.where(mask, ws_ref[gi], 0.0)
            bias = _col_of(bst_v, lane, gi)
            for win in range(tb // SGU_BLOCK):
                rows = slice(win * SGU_BLOCK, (win + 1) * SGU_BLOCK)
                sp = _dot(wsg, vgn[rows, cols]) + bias
                yb_ref[rows, cols] = (_gelu(u_ref[rows, cols]) * sp).astype(yb_ref.dtype)

    return pl.pallas_call(
        body, name="gate_sgu_fwd", grid=(t // tb,),
        in_specs=[pl.BlockSpec((tb, d), lambda i: (i, 0)),
                  pl.BlockSpec((tb, d), lambda i: (i, 3)),
                  pl.BlockSpec((tb, d), lambda i: (i, 4)),
                  pl.BlockSpec((tb, d), lambda i: (i, 5)),
                  row_spec, row_spec, row_spec,
                  pl.BlockSpec((groups, SGU_BLOCK, SGU_BLOCK), lambda i: (0, 0, 0)),
                  pl.BlockSpec((SGU_BLOCK, LANES), lambda i: (0, 0))],
        out_specs=[pl.BlockSpec((tb, d), lambda i: (i, 0)), pl.BlockSpec((tb, d), lambda i: (i, 0))],
        out_shape=[jax.ShapeDtypeStruct((t, d), ACT), jax.ShapeDtypeStruct((t, d), ACT)],
        compiler_params=_cparams(("parallel",)),
    )(o, projm, projm, projm, onw, lng, lnb, ws, bst)


def _gate_sgu_bwd(dya, dyb, o, projm, onw, lng, lnb, ws, bst, dprojm, d, carried=None):
    t = o.shape[0]
    heads, groups = d // DN_DK, d // SGU_GROUP_DIM
    tb = _tile(t, SGU_WINDOWS * SGU_BLOCK, SGU_BLOCK)
    extra = carried or _Carried([], [], 0, None)
    row_spec = pl.BlockSpec((1, d), lambda i: (0, 0))
    acc_row = pl.BlockSpec((SUBLANES, d), lambda i: (0, 0))

    def body(dya_ref, dyb_ref, o_ref, z_ref, u_ref, v_ref, onw_ref, lng_ref, lnb_ref, ws_ref, bst_ref, alias_ref,
             do_ref, dp_ref, donw_ref, dlng_ref, dlnb_ref, dws_ref, dbst_ref):
        @pl.when(pl.program_id(0) == 0)
        def _():
            for r_ in (donw_ref, dlng_ref, dlnb_ref, dws_ref, dbst_ref):
                r_[...] = jnp.zeros_like(r_)

        donw = jnp.zeros((SUBLANES, DN_DK), F32)
        for h in range(heads):
            cols = slice(h * DN_DK, (h + 1) * DN_DK)
            oh, zh, dyah, wh = o_ref[:, cols], z_ref[:, cols], dya_ref[:, cols], onw_ref[:, cols]
            r = lax.rsqrt(jnp.mean(oh * oh, axis=1, keepdims=True) + RMS_EPS)
            on = oh * r
            sz = _sigmoid(zh)
            silu_z = zh * sz
            don = dyah * wh * silu_z
            dp_ref[:, cols] = (dyah * on * wh * (sz * (1.0 + zh * (1.0 - sz)))).astype(dp_ref.dtype)
            donw = donw + _fold8(dyah * on * silu_z)
            do_ref[:, cols] = r * (don - on * jnp.mean(don * on, axis=1, keepdims=True))
        donw_ref[...] += donw

        vgp, up = v_ref[...], u_ref[...]
        xhat, rstd = _ln_hat(_gelu(vgp))
        lng_v = lng_ref[...]
        vgn = xhat * lng_v + lnb_ref[...]
        ua = _gelu(up)
        mask = _sgu_mask()
        lane = lax.broadcasted_iota(jnp.int32, (SGU_BLOCK, LANES), 1)
        bst_v = bst_ref[...]
        dbst = jnp.zeros((SGU_BLOCK, LANES), F32)
        dvgn_parts, dua_parts = [], []
        for gi in range(groups):
            cols = slice(gi * SGU_GROUP_DIM, (gi + 1) * SGU_GROUP_DIM)
            wsg = jnp.where(mask, ws_ref[gi], 0.0)
            bias = _col_of(bst_v, lane, gi)
            dws = jnp.zeros((SGU_BLOCK, SGU_BLOCK), F32)
            dvgn_g, dua_g = [], []
            for win in range(tb // SGU_BLOCK):
                rows = slice(win * SGU_BLOCK, (win + 1) * SGU_BLOCK)
                vg_g, dyb_g = vgn[rows, cols], dyb_ref[rows, cols]
                sp = _dot(wsg, vg_g) + bias
                dsp = dyb_g * ua[rows, cols]
                dua_g.append(dyb_g * sp)
                dws = dws + _dot(dsp, vg_g, NT)
                dbst = dbst + jnp.where(lane == gi, jnp.sum(dsp, axis=1, keepdims=True), 0.0)
                dvgn_g.append(_dot(wsg, dsp, TN))
            dws_ref[gi] += jnp.where(mask, dws, 0.0)
            dvgn_parts.append(jnp.concatenate(dvgn_g, axis=0))
            dua_parts.append(jnp.concatenate(dua_g, axis=0))
        dbst_ref[...] += dbst
        dvgn = jnp.concatenate(dvgn_parts, axis=1)
        dua = jnp.concatenate(dua_parts, axis=1)
        dlng_ref[...] += _fold8(dvgn * xhat)
        dlnb_ref[...] += _fold8(dvgn)
        dvga = _ln_bwd(dvgn * lng_v, xhat, rstd)
        dp_ref[:, d:2 * d] = (dua * _gelu_grad(up)).astype(dp_ref.dtype)
        dp_ref[:, 2 * d:] = (dvga * _gelu_grad(vgp)).astype(dp_ref.dtype)

    res = pl.pallas_call(
        _carrying(body, 12, 7, 0, carried, (t // tb,)),
        name="gate_sgu_bwd_carrying" if carried else "gate_sgu_bwd", grid=(t // tb,),
        in_specs=[pl.BlockSpec((tb, d), lambda i: (i, 0)),
                  pl.BlockSpec((tb, d), lambda i: (i, 0)),
                  pl.BlockSpec((tb, d), lambda i: (i, 0)),
                  pl.BlockSpec((tb, d), lambda i: (i, 3)),
                  pl.BlockSpec((tb, d), lambda i: (i, 4)),
                  pl.BlockSpec((tb, d), lambda i: (i, 5)),
                  row_spec, row_spec, row_spec,
                  pl.BlockSpec((groups, SGU_BLOCK, SGU_BLOCK), lambda i: (0, 0, 0)),
                  pl.BlockSpec((SGU_BLOCK, LANES), lambda i: (0, 0)),
                  ANY] + [ANY] * len(extra.inputs),
        out_specs=[pl.BlockSpec((tb, d), lambda i: (i, 0)),
                   pl.BlockSpec((tb, 3 * d), lambda i: (i, 1)),
                   pl.BlockSpec((SUBLANES, DN_DK), lambda i: (0, 0)),
                   acc_row, acc_row,
                   pl.BlockSpec((groups, SGU_BLOCK, SGU_BLOCK), lambda i: (0, 0, 0)),
                   pl.BlockSpec((SGU_BLOCK, LANES), lambda i: (0, 0))] + [ANY] * len(extra.out_shapes),
        out_shape=[jax.ShapeDtypeStruct((t, d), F32),
                   jax.ShapeDtypeStruct(dprojm.shape, dprojm.dtype),
                   jax.ShapeDtypeStruct((SUBLANES, DN_DK), F32),
                   jax.ShapeDtypeStruct((SUBLANES, d), F32),
                   jax.ShapeDtypeStruct((SUBLANES, d), F32),
                   jax.ShapeDtypeStruct((groups, SGU_BLOCK, SGU_BLOCK), F32),
                   jax.ShapeDtypeStruct((SGU_BLOCK, LANES), F32)] + extra.out_shapes,
        input_output_aliases={11: 1},
        scratch_shapes=extra.scratch() if carried else [],
        compiler_params=_cparams(("arbitrary",)),
    )(dya, dyb, o, projm, projm, projm, onw, lng, lnb, ws, bst, dprojm, *extra.inputs)
    return res[:7], res[7:]


def _mix_fwd(ya, yb, projm, x, wpa, wpb, wo, g1, b1, d, tb):
    t = x.shape[0]
    blk = pl.BlockSpec((tb, d), lambda i: (i, 0))
    wspec = pl.BlockSpec((d, d), lambda i: (0, 0))
    row_spec = pl.BlockSpec((1, d), lambda i: (0, 0))

    def body(ya_ref, yb_ref, ga_ref, gb_ref, x_ref, wpa_ref, wpb_ref, wo_ref, g_ref, b_ref,
             pa_ref, pb_ref, m_ref, h_ref, x1_ref, x1b_ref):
        pa = _dot(ya_ref[...], wpa_ref[...])
        pb = _dot(yb_ref[...], wpb_ref[...])
        m = _sigmoid(ga_ref[...]) * pa + _sigmoid(gb_ref[...]) * pb
        hres = ALPHA * x_ref[...] + _dot(m, wo_ref[...])
        xhat, _ = _ln_hat(hres)
        x1 = xhat * g_ref[...] + b_ref[...]
        pa_ref[...] = pa.astype(pa_ref.dtype)
        pb_ref[...] = pb.astype(pb_ref.dtype)
        m_ref[...] = m.astype(m_ref.dtype)
        h_ref[...] = hres
        x1_ref[...] = x1
        x1b_ref[...] = x1.astype(x1b_ref.dtype)

    f32_out = jax.ShapeDtypeStruct((t, d), F32)
    bf_out = jax.ShapeDtypeStruct((t, d), ACT)
    return pl.pallas_call(
        body, name="mix_fwd", grid=(t // tb,),
        in_specs=[blk, blk, pl.BlockSpec((tb, d), lambda i: (i, 6)), pl.BlockSpec((tb, d), lambda i: (i, 7)),
                  blk, wspec, wspec, wspec, row_spec, row_spec],
        out_specs=[blk] * 6,
        out_shape=[bf_out, bf_out, bf_out, f32_out, f32_out, bf_out],
        compiler_params=_cparams(("parallel",)),
    )(ya, yb, projm, projm, x, wpa, wpb, wo, g1, b1)


def _mix_bwd(dmix, pa, pb, projm, wpa, wpb, wo, d, tb):
    t = dmix.shape[0]
    blk = pl.BlockSpec((tb, d), lambda i: (i, 0))
    wspec = pl.BlockSpec((d, d), lambda i: (0, 0))

    def body(dmix_ref, pa_ref, pb_ref, ga_ref, gb_ref, wpa_ref, wpb_ref, wo_ref,
             dpa_ref, dpb_ref, dya_ref, dyb_ref, dg_ref):
        dm = _dot(dmix_ref[...], wo_ref[...], NT)
        sa, sb = _sigmoid(ga_ref[...]), _sigmoid(gb_ref[...])
        dpa, dpb = dm * sa, dm * sb
        dpa_ref[...] = dpa.astype(dpa_ref.dtype)
        dpb_ref[...] = dpb.astype(dpb_ref.dtype)
        dg_ref[:, :d] = (dm * pa_ref[...].astype(F32) * sa * (1.0 - sa)).astype(dg_ref.dtype)
        dg_ref[:, d:] = (dm * pb_ref[...].astype(F32) * sb * (1.0 - sb)).astype(dg_ref.dtype)
        dya_ref[...] = _dot(dpa, wpa_ref[...], NT)
        dyb_ref[...] = _dot(dpb, wpb_ref[...], NT)

    return pl.pallas_call(
        body, name="mix_bwd", grid=(t // tb,),
        in_specs=[blk, blk, blk, pl.BlockSpec((tb, d), lambda i: (i, 6)), pl.BlockSpec((tb, d), lambda i: (i, 7)),
                  wspec, wspec, wspec],
        out_specs=[blk, blk, blk, blk, pl.BlockSpec((tb, 2 * d), lambda i: (i, 3))],
        out_shape=[jax.ShapeDtypeStruct((t, d), ACT), jax.ShapeDtypeStruct((t, d), ACT),
                   jax.ShapeDtypeStruct((t, d), F32), jax.ShapeDtypeStruct((t, d), F32),
                   jax.ShapeDtypeStruct((t, 8 * d), ACT)],
        compiler_params=_cparams(("parallel",)),
    )(dmix, pa, pb, projm, projm, wpa, wpb, wo)


def _ffn_tail_fwd(gu, wd, x1, g, b, tb):
    t, d = x1.shape
    f = wd.shape[0]
    fc = _tile(f, MM_TILE)
    blk = pl.BlockSpec((tb, d), lambda i: (i, 0))
    row_spec = pl.BlockSpec((1, d), lambda i: (0, 0))

    def body(gu_ref, wd_ref, x_ref, g_ref, b_ref, a_ref, h_ref, y_ref, yb_ref):
        ffn = jnp.zeros((tb, d), F32)
        for c in range(f // fc):
            gp = gu_ref[:, c * fc:(c + 1) * fc].astype(F32)
            act = (gp * _sigmoid(gp) * gu_ref[:, f + c * fc:f + (c + 1) * fc].astype(F32)).astype(a_ref.dtype)
            a_ref[:, c * fc:(c + 1) * fc] = act
            ffn = ffn + _dot(act, wd_ref[c * fc:(c + 1) * fc, :])
        hres = ALPHA * x_ref[...] + ffn
        xhat, _ = _ln_hat(hres)
        y = xhat * g_ref[...] + b_ref[...]
        h_ref[...] = hres
        y_ref[...] = y
        yb_ref[...] = y.astype(yb_ref.dtype)

    return pl.pallas_call(
        body, name="ffn_tail_fwd", grid=(t // tb,),
        in_specs=[pl.BlockSpec((tb, 2 * f), lambda i: (i, 0)), pl.BlockSpec((f, d), lambda i: (0, 0)),
                  blk, row_spec, row_spec],
        out_specs=[pl.BlockSpec((tb, f), lambda i: (i, 0)), blk, blk, blk],
        out_shape=[jax.ShapeDtypeStruct((t, f), ACT), jax.ShapeDtypeStruct((t, d), F32),
                   jax.ShapeDtypeStruct((t, d), F32), jax.ShapeDtypeStruct((t, d), ACT)],
        compiler_params=_cparams(("parallel",)),
    )(gu, wd, x1, g, b)


def _ffn_tail_bwd(dh, wd, gu, tb):
    t, d = dh.shape
    f = wd.shape[0]
    fc = _tile(f, MM_TILE)

    def body(dh_ref, wd_ref, gu_ref, dgu_ref):
        dh_v = dh_ref[...]
        for c in range(f // fc):
            da = _dot(dh_v, wd_ref[c * fc:(c + 1) * fc, :], NT)
            gp = gu_ref[:, c * fc:(c + 1) * fc].astype(F32)
            sg = _sigmoid(gp)
            dgu_ref[:, c * fc:(c + 1) * fc] = (
                da * gu_ref[:, f + c * fc:f + (c + 1) * fc].astype(F32) * sg * (1.0 + gp * (1.0 - sg))
            ).astype(dgu_ref.dtype)
            dgu_ref[:, f + c * fc:f + (c + 1) * fc] = (da * gp * sg).astype(dgu_ref.dtype)

    return pl.pallas_call(
        body, name="ffn_tail_bwd", grid=(t // tb,),
        in_specs=[pl.BlockSpec((tb, d), lambda i: (i, 0)), pl.BlockSpec((f, d), lambda i: (0, 0)),
                  pl.BlockSpec((tb, 2 * f), lambda i: (i, 0))],
        out_specs=pl.BlockSpec((tb, 2 * f), lambda i: (i, 0)),
        out_shape=jax.ShapeDtypeStruct((t, 2 * f), ACT),
        compiler_params=_cparams(("parallel",)),
    )(dh, wd, gu)


def _ffn_head_bwd(dgu, wgu, dh2, hres, g, tb):
    t, d = dh2.shape
    f2 = wgu.shape[1]
    blk = pl.BlockSpec((tb, d), lambda i: (i, 0))
    acc = pl.BlockSpec((SUBLANES, d), lambda i: (0, 0))

    def body(dgu_ref, w_ref, dh2_ref, h_ref, g_ref, dh_ref, dhb_ref, dg_ref, db_ref):
        @pl.when(pl.program_id(0) == 0)
        def _():
            dg_ref[...] = jnp.zeros_like(dg_ref)
            db_ref[...] = jnp.zeros_like(db_ref)

        dy_v = _dot(dgu_ref[...], w_ref[...], NT) + ALPHA * dh2_ref[...]
        xhat, r = _ln_hat(h_ref[...])
        dh = _ln_bwd(dy_v * g_ref[...], xhat, r)
        dh_ref[...] = dh
        dhb_ref[...] = dh.astype(dhb_ref.dtype)
        dg_ref[...] += _fold8(dy_v * xhat)
        db_ref[...] += _fold8(dy_v)

    return pl.pallas_call(
        body, name="ffn_head_bwd", grid=(t // tb,),
        in_specs=[pl.BlockSpec((tb, f2), lambda i: (i, 0)), pl.BlockSpec((d, f2), lambda i: (0, 0)),
                  blk, blk, pl.BlockSpec((1, d), lambda i: (0, 0))],
        out_specs=[blk, blk, acc, acc],
        out_shape=[jax.ShapeDtypeStruct((t, d), F32), jax.ShapeDtypeStruct((t, d), ACT),
                   jax.ShapeDtypeStruct((SUBLANES, d), F32), jax.ShapeDtypeStruct((SUBLANES, d), F32)],
        compiler_params=_cparams(("arbitrary",)),
    )(dgu, wgu, dh2, hres, g)


def _loss_ln_bwd(y, target, hres, g, tb):
    t, d = y.shape
    blk = pl.BlockSpec((tb, d), lambda i: (i, 0))
    acc = pl.BlockSpec((SUBLANES, d), lambda i: (0, 0))

    def body(y_ref, t_ref, h_ref, g_ref, dh_ref, dhb_ref, dg_ref, db_ref, l_ref):
        @pl.when(pl.program_id(0) == 0)
        def _():
            for r_ in (dg_ref, db_ref, l_ref):
                r_[...] = jnp.zeros_like(r_)

        err = y_ref[...] - t_ref[...]
        dy_v = err * (1.0 / d)
        sq = _fold8(err * err)
        part = sq[:, :LANES]
        for c in range(1, d // LANES):
            part = part + sq[:, c * LANES:(c + 1) * LANES]
        l_ref[...] += part
        xhat, r = _ln_hat(h_ref[...])
        dh = _ln_bwd(dy_v * g_ref[...], xhat, r)
        dh_ref[...] = dh
        dhb_ref[...] = dh.astype(dhb_ref.dtype)
        dg_ref[...] += _fold8(dy_v * xhat)
        db_ref[...] += _fold8(dy_v)

    res = pl.pallas_call(
        body, name="loss_ln_bwd", grid=(t // tb,),
        in_specs=[blk, blk, blk, pl.BlockSpec((1, d), lambda i: (0, 0))],
        out_specs=[blk, blk, acc, acc, pl.BlockSpec((SUBLANES, LANES), lambda i: (0, 0))],
        out_shape=[jax.ShapeDtypeStruct((t, d), F32), jax.ShapeDtypeStruct((t, d), ACT),
                   jax.ShapeDtypeStruct((SUBLANES, d), F32), jax.ShapeDtypeStruct((SUBLANES, d), F32),
                   jax.ShapeDtypeStruct((SUBLANES, LANES), F32)],
        compiler_params=_cparams(("arbitrary",)),
    )(y, target, hres, g)
    return res[:4], res[4]


def _adamw(w, g, m, v):
    shape = w.shape
    cols = shape[-1]
    w2, g2, m2, v2 = (a.reshape(-1, cols) for a in (w, g, m, v))
    rows = w2.shape[0]
    tr = _tile(rows, 256, SUBLANES)
    blk = pl.BlockSpec((tr, cols), lambda i: (i, 0))

    def body(w_ref, g_ref, m_ref, v_ref, d_ref, nm_ref, nv_ref):
        g_v = g_ref[...]
        nm = ADAM_B1 * m_ref[...] + (1.0 - ADAM_B1) * g_v
        nv = ADAM_B2 * v_ref[...] + (1.0 - ADAM_B2) * (g_v * g_v)
        m_hat = nm / (1.0 - ADAM_B1 ** ADAM_STEP)
        v_hat = nv / (1.0 - ADAM_B2 ** ADAM_STEP)
        d_ref[...] = -ADAM_LR * (m_hat / (jnp.sqrt(v_hat) + ADAM_EPS) + ADAM_WD * w_ref[...])
        nm_ref[...] = nm
        nv_ref[...] = nv

    out = jax.ShapeDtypeStruct((rows, cols), F32)
    res = pl.pallas_call(
        body, name="adamw", grid=(rows // tr,),
        in_specs=[blk] * 4, out_specs=[blk] * 3, out_shape=[out] * 3,
        compiler_params=_cparams(("parallel",)),
    )(w2, g2, m2, v2)
    return tuple(r.reshape(shape) for r in res)


def _adamw_halves(place, w, m, v, halves):
    depth, r, c = w.shape
    half = r // 2
    tr = _tile(half, LANES, SUBLANES)
    nb = half // tr
    wspec = pl.BlockSpec((1, tr, c), lambda l, h, i, p: (l, h * nb + i, 0))

    def gspec(layer, own):
        def index(l, h, i, p):
            mine = h == p[0]
            used = jnp.logical_and(l == layer, mine if own else jnp.logical_not(mine))
            return (jnp.where(used, i, 0), 0)
        return pl.BlockSpec((tr, c), index)

    def body(p_ref, w_ref, m_ref, v_ref, a0_ref, b0_ref, a1_ref, b1_ref, g_ref, d_ref, nm_ref, nv_ref):
        mine = pl.program_id(1) == p_ref[0]
        g_v = jnp.where(pl.program_id(0) == 0, jnp.where(mine, a0_ref[...], b0_ref[...]),
                        jnp.where(mine, a1_ref[...], b1_ref[...]))
        nm = ADAM_B1 * m_ref[0] + (1.0 - ADAM_B1) * g_v
        nv = ADAM_B2 * v_ref[0] + (1.0 - ADAM_B2) * (g_v * g_v)
        m_hat = nm / (1.0 - ADAM_B1 ** ADAM_STEP)
        v_hat = nv / (1.0 - ADAM_B2 ** ADAM_STEP)
        g_ref[0] = g_v
        d_ref[0] = -ADAM_LR * (m_hat / (jnp.sqrt(v_hat) + ADAM_EPS) + ADAM_WD * w_ref[0])
        nm_ref[0] = nm
        nv_ref[0] = nv

    (a0, b0), (a1, b1) = halves
    out = jax.ShapeDtypeStruct(w.shape, F32)
    return pl.pallas_call(
        body, name="adamw_halves",
        grid_spec=pltpu.PrefetchScalarGridSpec(
            num_scalar_prefetch=1, grid=(depth, 2, nb),
            in_specs=[wspec, wspec, wspec, gspec(0, True), gspec(0, False), gspec(1, True), gspec(1, False)],
            out_specs=[wspec] * 4),
        out_shape=[out] * 4,
        compiler_params=_cparams(("arbitrary", "arbitrary", "arbitrary")),
    )(place, w, m, v, a0, b0, a1, b1)


def _place():
    x, y, c = lax.axis_index("x"), lax.axis_index("y"), lax.axis_index("c")
    return x, y, c, [(1 - x, y), (x, 1 - y), (1 - x, 1 - y)]


def _remote(src, dst, send_sems, recv_sems, k, to):
    return pltpu.make_async_remote_copy(src_ref=src, dst_ref=dst, send_sem=send_sems.at[k],
                                        recv_sem=recv_sems.at[k], device_id=to, device_id_type=MESH)


class _Carried:
    def __init__(self, inputs, out_shapes, n_sems, copies):
        self.inputs, self.out_shapes, self.n_sems, self.copies = list(inputs), list(out_shapes), n_sems, copies

    def scratch(self):
        return [pltpu.SemaphoreType.DMA((self.n_sems,)), pltpu.SemaphoreType.DMA((self.n_sems,))]


def _join_plans(first, second):
    ni, no, ns = len(first.inputs), len(first.out_shapes), first.n_sems

    def copies(in_refs, out_refs, send_sems, recv_sems):
        one = _phases(first.copies(in_refs[:ni], out_refs[:no], send_sems, recv_sems))
        two = _phases(second.copies(in_refs[ni:], out_refs[no:], send_sems.at[pl.ds(ns, second.n_sems)],
                                    recv_sems.at[pl.ds(ns, second.n_sems)]))

        def both(k):
            def run():
                one[k]()
                two[k]()
            return run

        return both(0), both(1), both(2)

    return _Carried(first.inputs + second.inputs, first.out_shapes + second.out_shapes, ns + second.n_sems, copies)


def _run_comm(name, plan):
    n_in, n_out = len(plan.inputs), len(plan.out_shapes)

    def body(*refs):
        for phase in _phases(plan.copies(refs[:n_in], refs[n_in:n_in + n_out], refs[-2], refs[-1])):
            phase()

    return pl.pallas_call(
        body, name=name, in_specs=[ANY] * n_in, out_specs=[ANY] * n_out, out_shape=plan.out_shapes,
        scratch_shapes=plan.scratch(),
    )(*plan.inputs)


def _half_rows(rows, core):
    if rows % (4 * SUBLANES):
        return None
    return pl.ds(pl.multiple_of(core * (rows // 2), 2 * SUBLANES), rows // 2)


def _all_gather_plan(shards):
    n = len(shards)

    def copies(x_refs, out_refs, send_sems, recv_sems):
        x, y, c, chips = _place()
        sibling = (x, y, 1 - c)
        mine = 2 * x + y
        split = [_half_rows(x_refs[t].shape[0], c) is not None for t in range(n)]

        def src(t):
            return x_refs[t].at[_half_rows(x_refs[t].shape[0], c)] if split[t] else x_refs[t]

        def slot(t, chip_idx, core):
            rows = _half_rows(x_refs[t].shape[0], core)
            return out_refs[t].at[chip_idx, rows] if split[t] else out_refs[t].at[chip_idx]

        def first():
            return [_remote(src(t), slot(t, mine, c), send_sems, recv_sems, 6 * t + j, (cx, cy, c))
                    for j, (cx, cy) in enumerate(chips) for t in range(n)]

        def start():
            for cp in first():
                cp.start()

        def passed():
            return [_remote(slot(t, 2 * cx + cy, c), slot(t, 2 * cx + cy, c), send_sems, recv_sems, 6 * t + 3 + j,
                            sibling) for j, (cx, cy) in enumerate(chips) for t in range(n) if split[t]]

        def middle():
            for j, (cx, cy) in enumerate(chips):
                for t in range(n):
                    theirs = slot(t, 2 * cx + cy, c)
                    _remote(theirs, theirs, send_sems, recv_sems, 6 * t + j, (cx, cy, c)).wait_recv()
            for cp in passed():
                cp.start()

        def finish():
            for j, (cx, cy) in enumerate(chips):
                for t in range(n):
                    if split[t]:
                        other = slot(t, 2 * cx + cy, 1 - c)
                        _remote(other, other, send_sems, recv_sems, 6 * t + 3 + j, sibling).wait_recv()
            for cp in first() + passed():
                cp.wait_send()

        return start, middle, finish

    return _Carried(shards, [jax.ShapeDtypeStruct((N_CHIPS,) + s.shape, s.dtype) for s in shards], 6 * n, copies)


def _sibling_exchange_plan(grads, small=None):
    n = len(grads)
    extra = [] if small is None else [small]

    def copies(in_refs, out_refs, send_sems, recv_sems):
        x, y, c, _ = _place()
        sibling = (x, y, 1 - c)

        def all_copies():
            cps = [_remote(in_refs[t].at[:, _half_rows(in_refs[t].shape[1], 1 - c), :], out_refs[t],
                           send_sems, recv_sems, t, sibling) for t in range(n)]
            if extra:
                cps.append(_remote(in_refs[n], out_refs[n], send_sems, recv_sems, n, sibling))
            return cps

        def start():
            for cp in all_copies():
                cp.start()

        def finish():
            for cp in all_copies():
                cp.wait()

        return start, finish

    shapes = [jax.ShapeDtypeStruct((g.shape[0], g.shape[1] // 2, g.shape[2]), g.dtype) for g in grads]
    shapes += [jax.ShapeDtypeStruct(s.shape, s.dtype) for s in extra]
    return _Carried(list(grads) + extra, shapes, n + 1, copies)


def _chip_exchange_plan(travel, small=None):
    n = len(travel)
    extra = [] if small is None else [small]

    def copies(in_refs, out_refs, send_sems, recv_sems):
        x, y, c, chips = _place()
        mine = 2 * x + y

        def all_copies():
            cps = []
            for j, (cx, cy) in enumerate(chips):
                to = (cx, cy, c)
                for t in range(n):
                    cps.append(_remote(in_refs[t].at[2 * cx + cy], out_refs[t].at[mine], send_sems, recv_sems,
                                       3 * t + j, to))
                if extra:
                    cps.append(_remote(in_refs[n], out_refs[n].at[mine], send_sems, recv_sems, 3 * n + j, to))
            return cps

        def start():
            for cp in all_copies():
                cp.start()

        def finish():
            for cp in all_copies():
                cp.wait()

        return start, finish

    shapes = [jax.ShapeDtypeStruct(g.shape, g.dtype) for g in travel]
    shapes += [jax.ShapeDtypeStruct((N_CHIPS,) + s.shape, s.dtype) for s in extra]
    return _Carried(list(travel) + extra, shapes, 3 * n + 3, copies)


def _sibling_merge_plan(reduced):
    n = len(reduced)

    def copies(in_refs, out_refs, send_sems, recv_sems):
        x, y, c, _ = _place()

        def all_copies():
            return [_remote(in_refs[t], out_refs[t], send_sems, recv_sems, t, (x, y, 1 - c)) for t in range(n)]

        def start():
            for cp in all_copies():
                cp.start()

        def finish():
            for cp in all_copies():
                cp.wait()

        return start, finish

    return _Carried(reduced, [jax.ShapeDtypeStruct(r.shape, r.dtype) for r in reduced], n, copies)


def _pair_sum(place, grad, land):
    n, r, c = grad.shape
    half = r // 2
    tr = _tile(half, 256, SUBLANES)
    nb = half // tr

    def body(place_ref, a_ref, b_ref, travel_ref, own_ref):
        total = a_ref[0] + b_ref[0]
        travel_ref[0] = total.astype(travel_ref.dtype)

        @pl.when(pl.program_id(1) == place_ref[1])
        def _():
            own_ref[...] = total

    return pl.pallas_call(
        body, name="grad_pair_sum",
        grid_spec=pltpu.PrefetchScalarGridSpec(
            num_scalar_prefetch=1, grid=(nb, n),
            in_specs=[pl.BlockSpec((1, tr, c), lambda i, s, p: (s, p[0] * nb + i, 0)),
                      pl.BlockSpec((1, tr, c), lambda i, s, p: (s, i, 0))],
            out_specs=[pl.BlockSpec((1, tr, c), lambda i, s, p: (s, i, 0)),
                       pl.BlockSpec((tr, c), lambda i, s, p: (i, 0))]),
        out_shape=[jax.ShapeDtypeStruct((n, half, c), BF16), jax.ShapeDtypeStruct((half, c), F32)],
        compiler_params=_cparams(("parallel", "arbitrary")),
    )(place, grad, land)


def _chip_sum(place, own, land, name):
    n, r, c = land.shape
    tr = _tile(r, 256, SUBLANES)

    def body(place_ref, own_ref, land_ref, o_ref):
        mine = place_ref[1]
        acc = jnp.zeros(o_ref.shape, F32)
        for s in range(n):
            acc = acc + jnp.where(mine == s, own_ref[...], land_ref[s].astype(F32))
        o_ref[...] = acc

    return pl.pallas_call(
        body, name=name,
        grid_spec=pltpu.PrefetchScalarGridSpec(
            num_scalar_prefetch=1, grid=(r // tr,),
            in_specs=[pl.BlockSpec((tr, c), lambda i, p: (i, 0)),
                      pl.BlockSpec((n, tr, c), lambda i, p: (0, i, 0))],
            out_specs=pl.BlockSpec((tr, c), lambda i, p: (i, 0))),
        out_shape=jax.ShapeDtypeStruct((r, c), F32),
        compiler_params=_cparams(("parallel",)),
    )(place, own, land)


def _add2(a, b):
    rows = a.shape[0]
    tr = _tile(rows, 256, SUBLANES)
    blk = pl.BlockSpec((tr, a.shape[1]), lambda i: (i, 0))

    def body(a_ref, b_ref, o_ref):
        o_ref[...] = a_ref[...] + b_ref[...]

    return pl.pallas_call(
        body, name="grad_small_pair_sum", grid=(rows // tr,), in_specs=[blk, blk], out_specs=blk,
        out_shape=jax.ShapeDtypeStruct(a.shape, F32), compiler_params=_cparams(("parallel",)),
    )(a, b)


_BIG = (("w_in", 2), ("w_pa", 1), ("w_pb", 1), ("w_o", 1), ("w_ffn_gate", 2), ("w_ffn_up", 2),
        ("w_ffn_down", 1))
_SMALL = ("conv_w", "a_log", "dt_bias", "o_norm_w", "sgu_ln_g", "sgu_ln_b", "w_s", "b_s",
          "ln1_g", "ln1_b", "ln2_g", "ln2_b")


def _pack_small(arrays):
    pieces = []
    for a in arrays:
        if a.shape[-1] % LANES == 0:
            a2 = a.reshape(-1, LANES)
        else:
            a2 = jnp.pad(a.reshape(-1, a.shape[-1]), ((0, 0), (0, LANES - a.shape[-1])))
        pieces.append(jnp.pad(a2, ((0, -a2.shape[0] % SUBLANES), (0, 0))))
    return jnp.concatenate(pieces, axis=0)


def _unpack_small(buf, like):
    out, off = [], 0
    for a in like:
        if a.shape[-1] % LANES == 0:
            rows = a.size // LANES
            out.append(buf[off:off + rows].reshape(a.shape))
        else:
            rows = a.size // a.shape[-1]
            out.append(buf[off:off + rows, :a.shape[-1]].reshape(a.shape))
        off += -(-rows // SUBLANES) * SUBLANES
    return out


def _unshard(gathered, local, chip, axis):
    parts = [jnp.where(chip == s, local, gathered[s]) for s in range(N_CHIPS)]
    return jnp.concatenate(parts, axis=axis - 1)


def _to_shards(full, axis):
    l, r, c = full.shape
    if axis == 1:
        return full.reshape(l, N_CHIPS, r // N_CHIPS, c)
    return jnp.transpose(full.reshape(l, r, N_CHIPS, c // N_CHIPS), (0, 2, 1, 3))


def _row(v, width=None):
    v = v.reshape(1, -1).astype(F32)
    if width is not None and v.shape[1] < width:
        v = jnp.pad(v, ((0, 0), (0, width - v.shape[1])))
    return v


def _layer_consts(p, l, d):
    heads = d // DN_DK
    return dict(
        alog=_row(p["a_log"][l], LANES), dtb=_row(p["dt_bias"][l], LANES),
        onw=_row(jnp.tile(p["o_norm_w"][l], heads)),
        lng=_row(p["sgu_ln_g"][l]), lnb=_row(p["sgu_ln_b"][l]),
        ws=p["w_s"][l].astype(F32),
        bst=jnp.pad(p["b_s"][l].T, ((0, 0), (0, LANES - p["b_s"].shape[1]))),
        g1=_row(p["ln1_g"][l]), b1=_row(p["ln1_b"][l]), g2=_row(p["ln2_g"][l]), b2=_row(p["ln2_b"][l]))


class _NoComm:
    def with_proj_main(self):
        return None

    def after_proj_main(self, got):
        pass

    def weights(self, full):
        return full

    def with_dn_fwd(self):
        return None

    def after_dn_fwd(self, got):
        pass

    def with_ffn_in_dw(self):
        return None

    def after_ffn_in_dw(self, got):
        pass

    def after_branch_grads(self, g):
        pass

    def with_dn_bwd(self):
        return None

    def after_dn_bwd(self, got):
        pass

    def with_proj_main_dw(self):
        return None

    def after_proj_main_dw(self, got):
        pass

    def with_ffn_in(self):
        return None

    def after_ffn_in(self, got):
        pass

    def after_all_grads(self, g):
        pass

    def with_gate_sgu_bwd(self):
        return None

    def after_gate_sgu_bwd(self, got):
        pass

    def with_proj_gates_dx(self):
        return None

    def after_proj_gates_dx(self, got):
        pass

    def with_proj_main_dx(self):
        return None

    def after_proj_main_dx(self, got):
        pass


def _carry(carried, after, call, *args, **kw):
    if carried is None:
        return call(*args, **kw)
    out, got = call(*args, carried=carried, **kw)
    after(got)
    return out


def _in_proj_weights(w_in, d):
    heads, q4 = d // DN_DK, 4 * d
    wba = jnp.zeros((d, 2 * LANES), w_in.dtype)
    wba = wba.at[:, :heads].set(w_in[:, q4:q4 + heads])
    wba = wba.at[:, LANES:LANES + heads].set(w_in[:, q4 + heads:q4 + 2 * heads])
    return jnp.concatenate([w_in[:, :q4], w_in[:, q4 + 2 * heads:]], axis=1), wba


def _layer_fwd(x, xb, full, cl, d, tb, comm):
    wm, wba = _in_proj_weights(full["w_in"], d)
    projm = _carry(comm.with_proj_main(), comm.after_proj_main, _matmul, xb, wm, NN, "proj_main", tn=MM_WIDE)
    full = comm.weights(full)
    wl = dict(wm=wm, wba=wba, conv=full["conv_w"], wpa=full["w_pa"], wpb=full["w_pb"], wo=full["w_o"],
              wgu=jnp.concatenate([full["w_ffn_gate"], full["w_ffn_up"]], axis=1), wd=full["w_ffn_down"])
    ba = _matmul(xb, wba, NN, "proj_gates")
    qkv = _conv_fwd(projm, wl["conv"], d, _tile(x.shape[0], 2 * tb, SUBLANES))
    (o, states, ycors), got = _dn_fwd(qkv, ba, cl["alog"], cl["dtb"], d, comm.with_dn_fwd())
    comm.after_dn_fwd(got)
    ya, yb = _gate_sgu_fwd(o, projm, cl["onw"], cl["lng"], cl["lnb"], cl["ws"], cl["bst"], d)
    pa, pb, m, h1, x1, x1b = _mix_fwd(ya, yb, projm, x, wl["wpa"], wl["wpb"], wl["wo"], cl["g1"], cl["b1"], d, tb)
    gu = _carry(comm.with_ffn_in(), comm.after_ffn_in, _matmul, x1b, wl["wgu"], NN, "ffn_in", out_dtype=ACT,
                tn=2 * MM_TILE)
    act, h2, x2, x2b = _ffn_tail_fwd(gu, wl["wd"], x1, cl["g2"], cl["b2"], tb)
    saved = dict(xb=xb, projm=projm, ba=ba, qkv=qkv, o=o, states=states, ycors=ycors, ya=ya, yb=yb,
                 pa=pa, pb=pb, m=m, h1=h1, x1b=x1b, gu=gu, act=act, h2=h2)
    return x2, x2b, saved, wl


def _layer_bwd(sv, wl, cl, d, tb, comm, ln2_bwd, next_ln=None):
    g = {}
    dh2, dh2b, dg2, db2 = ln2_bwd
    g["ln2_g"], g["ln2_b"] = dg2.sum(0), db2.sum(0)
    g["wd"] = _matmul(sv["act"], dh2b, TN, "ffn_out_dw")
    dgu = _ffn_tail_bwd(dh2b, wl["wd"], sv["gu"], tb)
    g["wgu"] = _carry(comm.with_ffn_in_dw(), comm.after_ffn_in_dw, _matmul, sv["x1b"], dgu, TN, "ffn_in_dw")
    dh1, dh1b, dg1, db1 = _ffn_head_bwd(dgu, wl["wgu"], dh2, sv["h1"], cl["g1"], tb)
    g["ln1_g"], g["ln1_b"] = dg1.sum(0), db1.sum(0)
    g["wo"] = _matmul(sv["m"], dh1b, TN, "wo_dw")
    dpa, dpb, dya, dyb, dprojm = _mix_bwd(dh1b, sv["pa"], sv["pb"], sv["projm"], wl["wpa"], wl["wpb"], wl["wo"], d, tb)
    g["wpa"] = _matmul(sv["ya"], dpa, TN, "wpa_dw")
    g["wpb"] = _matmul(sv["yb"], dpb, TN, "wpb_dw")
    comm.after_branch_grads(g)
    (do, dprojm, donw, dlng, dlnb, dws, dbst), got = _gate_sgu_bwd(
        dya, dyb, sv["o"], sv["projm"], cl["onw"], cl["lng"], cl["lnb"], cl["ws"], cl["bst"], dprojm, d,
        comm.with_gate_sgu_bwd())
    comm.after_gate_sgu_bwd(got)
    heads, groups = d // DN_DK, d // SGU_GROUP_DIM
    g["o_norm_w"], g["sgu_ln_g"], g["sgu_ln_b"] = donw.sum(0), dlng.sum(0), dlnb.sum(0)
    g["w_s"], g["b_s"] = dws, dbst[:, :groups].T
    (dqkv, dba, dal, ddt), got = _dn_bwd(sv["qkv"], sv["ba"], cl["alog"], cl["dtb"], do, sv["states"],
                                         sv["ycors"], d, comm.with_dn_bwd())
    comm.after_dn_bwd(got)
    g["a_log"], g["dt_bias"] = dal.sum(0)[:heads], ddt.sum(0)[:heads]
    tbc = _tile(sv["xb"].shape[0], 2 * tb, SUBLANES)
    dy, dcw = _conv_bwd_dy(sv["projm"], wl["conv"], dqkv, d, tbc)
    g["conv_w"] = dcw.sum(1)
    dprojm = _conv_bwd_dx(dy, wl["conv"], dprojm, d, tbc)
    g["wba"] = _matmul(sv["xb"], dba, TN, "proj_gates_dw")
    g["wm"] = _carry(comm.with_proj_main_dw(), comm.after_proj_main_dw, _matmul, sv["xb"], dprojm, TN,
                     "proj_main_dw", tn=MM_WIDE)
    comm.after_all_grads(g)
    dx = _carry(comm.with_proj_gates_dx(), comm.after_proj_gates_dx, _matmul, dba, wl["wba"], NT, "proj_gates_dx",
                add=dh1, coef=ALPHA)
    if next_ln is not None:
        return _matmul(dprojm, wl["wm"], NT, "proj_main_dx", add=dx, tm=MM_TILE // 3, tk=MM_WIDE, ln=next_ln), g
    dx = _carry(comm.with_proj_main_dx(), comm.after_proj_main_dx, _matmul, dprojm, wl["wm"], NT, "proj_main_dx",
                add=dx, tk=MM_WIDE)
    return dx, g


_BRANCH = ("w_pa", "w_pb", "w_o", "w_ffn_gate", "w_ffn_up", "w_ffn_down")


def _grad_shards(g, d, keys):
    heads, q4 = d // DN_DK, 4 * d
    rows = lambda a: a.reshape(N_CHIPS, -1, a.shape[1])
    out = {}
    if "w_in" in keys:
        gm, gba, wsh = g["wm"], g["wba"], 2 * d + heads // 2
        out["w_in"] = jnp.stack([gm[:, :wsh],
                                 jnp.concatenate([gm[:, wsh:q4], gba[:, :heads]], axis=1),
                                 jnp.concatenate([gba[:, LANES:LANES + heads], gm[:, q4:q4 + wsh - heads]], axis=1),
                                 gm[:, q4 + wsh - heads:]])
    if "w_pa" in keys:
        ggu = g["wgu"]
        f = ggu.shape[1] // 2
        fs = f // N_CHIPS
        out.update({
            "w_pa": rows(g["wpa"]), "w_pb": rows(g["wpb"]), "w_o": rows(g["wo"]), "w_ffn_down": rows(g["wd"]),
            "w_ffn_gate": jnp.stack([ggu[:, s * fs:(s + 1) * fs] for s in range(N_CHIPS)]),
            "w_ffn_up": jnp.stack([ggu[:, f + s * fs:f + (s + 1) * fs] for s in range(N_CHIPS)])})
    return out


def _local_step(x, target, full0, full1_of, small_w, comm0=None):
    t, d = x.shape
    tb = _tile(t, 256, SUBLANES)
    comm0 = comm0 or _NoComm()
    consts = [_layer_consts(small_w, l, d) for l in range(DEPTH)]
    x1, x1b, sv0, w0 = _layer_fwd(x, x.astype(ACT), full0, consts[0], d, tb, comm0)
    x2, _, sv1, w1 = _layer_fwd(x1, x1b, full1_of(), consts[1], d, tb, _NoComm())
    ln2_bwd, loss_parts = _loss_ln_bwd(x2, target, sv1["h2"], consts[1]["g2"], tb)
    ln2_bwd, g1 = _layer_bwd(sv1, w1, consts[1], d, tb, _NoComm(), ln2_bwd, next_ln=(sv0["h2"], consts[0]["g2"]))
    comm0.layer1_grads = g1
    grad_x, g0 = _layer_bwd(sv0, w0, consts[0], d, tb, comm0, ln2_bwd)
    return loss_parts, grad_x, [g0, g1]


def kernel(x, w_in, conv_w, a_log, dt_bias, o_norm_w, sgu_ln_g, sgu_ln_b, w_s, b_s, w_pa, w_pb, w_o, ln1_g, ln1_b, w_ffn_gate, w_ffn_up, w_ffn_down, ln2_g, ln2_b, loss_target, m_w_in, m_conv_w, m_a_log, m_dt_bias, m_o_norm_w, m_sgu_ln_g, m_sgu_ln_b, m_w_s, m_b_s, m_w_pa, m_w_pb, m_w_o, m_ln1_g, m_ln1_b, m_w_ffn_gate, m_w_ffn_up, m_w_ffn_down, m_ln2_g, m_ln2_b, v_w_in, v_conv_w, v_a_log, v_dt_bias, v_o_norm_w, v_sgu_ln_g, v_sgu_ln_b, v_w_s, v_b_s, v_w_pa, v_w_pb, v_w_o, v_ln1_g, v_ln1_b, v_w_ffn_gate, v_w_ffn_up, v_w_ffn_down, v_ln2_g, v_ln2_b):
    names = ("w_in", "conv_w", "a_log", "dt_bias", "o_norm_w", "sgu_ln_g", "sgu_ln_b", "w_s", "b_s", "w_pa",
             "w_pb", "w_o", "ln1_g", "ln1_b", "w_ffn_gate", "w_ffn_up", "w_ffn_down", "ln2_g", "ln2_b")
    w = dict(zip(names, (w_in, conv_w, a_log, dt_bias, o_norm_w, sgu_ln_g, sgu_ln_b, w_s, b_s, w_pa, w_pb, w_o,
                         ln1_g, ln1_b, w_ffn_gate, w_ffn_up, w_ffn_down, ln2_g, ln2_b)))
    mom = dict(zip(names, (m_w_in, m_conv_w, m_a_log, m_dt_bias, m_o_norm_w, m_sgu_ln_g, m_sgu_ln_b, m_w_s, m_b_s,
                           m_w_pa, m_w_pb, m_w_o, m_ln1_g, m_ln1_b, m_w_ffn_gate, m_w_ffn_up, m_w_ffn_down,
                           m_ln2_g, m_ln2_b)))
    var = dict(zip(names, (v_w_in, v_conv_w, v_a_log, v_dt_bias, v_o_norm_w, v_sgu_ln_g, v_sgu_ln_b, v_w_s, v_b_s,
                           v_w_pa, v_w_pb, v_w_o, v_ln1_g, v_ln1_b, v_w_ffn_gate, v_w_ffn_up, v_w_ffn_down,
                           v_ln2_g, v_ln2_b)))
    chip = 2 * lax.axis_index("x") + lax.axis_index("y")
    place = jnp.stack([lax.axis_index("c"), chip]).astype(jnp.int32)

    big = [k for k, _ in _BIG]
    axis_of = dict(_BIG)
    local = {k: w[k].astype(BF16) for k in big}
    local["conv_w"] = conv_w

    def gather_plan(l, keys):
        return _all_gather_plan([local[k][l] for k in keys])

    def full_of(l, keys, gathered):
        return {k: _unshard(gt, local[k][l], chip, axis_of.get(k, 2)) for k, gt in zip(keys, gathered)}

    def pair_sums(grads_l, keys, lands):
        return [_pair_sum(place, grads_l[k], land) for k, land in zip(keys, lands)]

    def chip_sums(pairs, lands):
        return [_chip_sum(place, p[1], land, "grad_chip_sum") for p, land in zip(pairs, lands)]

    class Layer0Comm(_NoComm):
        def with_proj_main(self):
            return gather_plan(0, _BRANCH)

        def after_proj_main(self, got):
            self.rest = full_of(0, _BRANCH, got)

        def weights(self, full):
            return {**full, **self.rest}

        def with_dn_fwd(self):
            return gather_plan(1, mixer)

        def after_dn_fwd(self, got):
            self.full1 = full_of(1, mixer, got)

        def with_ffn_in(self):
            return gather_plan(1, ffn)

        def after_ffn_in(self, got):
            self.full1.update(full_of(1, ffn, got))

        def with_ffn_in_dw(self):
            self.g1 = _grad_shards(self.layer1_grads, x.shape[-1], big)
            return _sibling_exchange_plan([self.g1[k] for k in big])

        def after_ffn_in_dw(self, got):
            self.pairs1 = pair_sums(self.g1, big, got)

        def with_dn_bwd(self):
            return _chip_exchange_plan([p[0] for p in self.pairs1])

        def after_dn_bwd(self, got):
            self.red1 = chip_sums(self.pairs1, got)

        def after_branch_grads(self, g0):
            self.shards0 = _grad_shards(g0, x.shape[-1], _BRANCH)

        def with_gate_sgu_bwd(self):
            return _sibling_exchange_plan([self.shards0[k] for k in _BRANCH])

        def after_gate_sgu_bwd(self, got):
            self.pairs0 = pair_sums(self.shards0, _BRANCH, got)

        def with_proj_main_dw(self):
            return _chip_exchange_plan([p[0] for p in self.pairs0])

        def after_proj_main_dw(self, got):
            self.red0 = chip_sums(self.pairs0, got)

        def after_all_grads(self, g0):
            self.g_in = _grad_shards(g0, x.shape[-1], ["w_in"])["w_in"]
            self.small_g = {k: jnp.stack([g0[k], self.layer1_grads[k]]) for k in _SMALL}
            self.small = _pack_small([self.small_g[k] for k in _SMALL])

        def with_proj_gates_dx(self):
            return _sibling_exchange_plan([self.g_in], self.small)

        def after_proj_gates_dx(self, got):
            self.pair_in = _pair_sum(place, self.g_in, got[0])
            self.small_chip = _add2(self.small, got[1])

        def with_proj_main_dx(self):
            return _join_plans(_chip_exchange_plan([self.pair_in[0]], self.small_chip),
                               _sibling_merge_plan(self.red0 + self.red1))

        def after_proj_main_dx(self, got):
            self.red_in = _chip_sum(place, self.pair_in[1], got[0], "grad_chip_sum")
            self.small_total = _chip_sum(place, self.small_chip, got[1], "grad_small_chip_sum")
            self.others = got[2:]

    comm = Layer0Comm()
    first, mixer, ffn = ["w_in", "conv_w"], ["w_in", "conv_w", "w_pa", "w_pb", "w_o"], list(_BRANCH[3:])
    full0 = full_of(0, first, _run_comm("all_gather_weights", gather_plan(0, first)))
    small_w = {k: w[k] for k in _SMALL if k != "conv_w"}
    loss_parts, grad_x, g = _local_step(x[0], loss_target[0], full0, lambda: comm.full1, small_w, comm)

    reduced = [comm.red_in] + comm.red0 + comm.red1
    others = list(_run_comm("grad_sibling_merge", _sibling_merge_plan([comm.red_in]))) + list(comm.others)
    halves = list(zip(reduced, others))
    grads = dict(zip(_SMALL, _unpack_small(comm.small_total, [comm.small_g[k] for k in _SMALL])))
    grads["conv_w"] = lax.dynamic_index_in_dim(_to_shards(grads["conv_w"], 2), chip, 1, keepdims=False)

    delta, new_m, new_v = {}, {}, {}
    for i, k in enumerate(big):
        grads[k], delta[k], new_m[k], new_v[k] = _adamw_halves(place, w[k], mom[k], var[k],
                                                               [halves[i], halves[len(big) + i]])
    delta["conv_w"], new_m["conv_w"], new_v["conv_w"] = _adamw(conv_w, grads["conv_w"], mom["conv_w"], var["conv_w"])
    rep = [k for k in _SMALL if k != "conv_w"]
    pack = lambda dct: _pack_small([dct[k] for k in rep])
    packed = _adamw(pack(w), pack(grads), pack(mom), pack(var))
    for dst, src in zip((delta, new_m, new_v), packed):
        dst.update(zip(rep, _unpack_small(src, [w[k] for k in rep])))

    loss = 0.5 * lax.psum(jnp.sum(loss_parts), ("x", "y", "c")) / x.shape[-1]
    return (loss, grad_x[None], *[grads[k] for k in names], *[delta[k] for k in names],
            *[new_m[k] for k in names], *[new_v[k] for k in names])
```

```python
import math

import jax
import jax.numpy as jnp
from jax import lax
from jax.experimental import pallas as pl
from jax.experimental.pallas import tpu as pltpu

F32 = jnp.float32
BF16 = jnp.bfloat16
MXU_DTYPE = jnp.bfloat16
ACT = jnp.bfloat16
HIGHEST = lax.Precision.HIGHEST

DEPTH = 2
CHUNK = 64
DN_GROUP = 2
DN_GROUP_FWD = 4
SGU_BLOCK = 128
SGU_WINDOWS = 4
CONV_K = 4
DN_DK = 128
SGU_GROUP_DIM = 128
LN_EPS = 1e-5
RMS_EPS = 1e-6
ALPHA = (2 * DEPTH) ** 0.25
ADAM_LR, ADAM_B1, ADAM_B2, ADAM_EPS, ADAM_WD, ADAM_STEP = 0.001, 0.9, 0.999, 1e-08, 0.01, 10

LANES = 128
SUBLANES = 8
VMEM_LIMIT = 52 * 2 ** 20
N_CHIPS = 4

NN = ((1,), (0,))
NT = ((1,), (1,))
TN = ((0,), (0,))
MESH = pl.DeviceIdType.MESH
ANY = pl.BlockSpec(memory_space=pl.ANY)


def _dot(a, b, dims=NN, prec=None):
    if prec is None:
        a = a.astype(MXU_DTYPE)
        b = b.astype(MXU_DTYPE)
    return lax.dot_general(a, b, (dims, ((), ())), preferred_element_type=F32, precision=prec)


def _cparams(sem=None):
    return pltpu.CompilerParams(dimension_semantics=sem, vmem_limit_bytes=VMEM_LIMIT)


def _tile(dim, pref, unit=LANES):
    t = (min(pref, dim) // unit) * unit
    while t >= unit:
        if dim % t == 0:
            return t
        t -= unit
    return dim


def _fold8(x):
    r, n = x.shape
    return x.reshape(r // SUBLANES, SUBLANES, n).sum(axis=0)


def _sigmoid(x):
    return 1.0 / (1.0 + jnp.exp(-x))


def _gelu(x):
    return 0.5 * x * (1.0 + lax.erf(x * (2.0 ** -0.5)))


def _gelu_grad(x):
    return 0.5 * (1.0 + lax.erf(x * (2.0 ** -0.5))) + x * jnp.exp(-0.5 * x * x) * (2.0 * math.pi) ** -0.5


def _ln_hat(h):
    mu = jnp.mean(h, axis=-1, keepdims=True)
    xc = h - mu
    var = jnp.mean(xc * xc, axis=-1, keepdims=True)
    r = lax.rsqrt(var + LN_EPS)
    return xc * r, r


def _ln_bwd(dxhat, xhat, r):
    return r * (dxhat - jnp.mean(dxhat, axis=-1, keepdims=True)
                - xhat * jnp.mean(dxhat * xhat, axis=-1, keepdims=True))


MM_TILE = 1536
MM_WIDE = 2048


def _matmul(a, b, dims, name, out_dtype=F32, add=None, coef=1.0, tm=MM_TILE, tn=MM_TILE, tk=MM_TILE, carried=None,
            ln=None):
    if dims == NN:
        (m, k), n = a.shape, b.shape[1]
    elif dims == NT:
        (m, k), n = a.shape, b.shape[0]
    else:
        (k, m), n = a.shape, b.shape[1]
    tm, tn, tk = _tile(m, tm), _tile(n, tn), _tile(k, tk)
    nk = k // tk
    a_spec = pl.BlockSpec((tk, tm), lambda j, i, q: (q, i)) if dims == TN else pl.BlockSpec((tm, tk), lambda j, i, q: (i, q))
    b_spec = pl.BlockSpec((tn, tk), lambda j, i, q: (j, q)) if dims == NT else pl.BlockSpec((tk, tn), lambda j, i, q: (q, j))
    o_spec = pl.BlockSpec((tm, tn), lambda j, i, q: (i, j))
    has_add = add is not None
    if ln is not None:
        assert n == tn and has_add and carried is None
        return _matmul_ln_bwd(a, b, dims, name, add, coef, ln, a_spec, b_spec, o_spec, (m, n, tm, tn, nk))

    def body(*refs):
        a_ref, b_ref = refs[0], refs[1]
        add_ref = refs[2] if has_add else None
        o_ref, acc_ref = refs[2 + has_add], refs[3 + has_add]
        q = pl.program_id(2)
        part = _dot(a_ref[...], b_ref[...], dims)

        def finish(r):
            if has_add:
                r = r + coef * add_ref[...]
            o_ref[...] = r.astype(out_dtype)

        if nk == 1:
            finish(part)
        else:
            @pl.when(q == 0)
            def _():
                acc_ref[...] = part

            @pl.when(q > 0)
            def _():
                acc_ref[...] += part

            @pl.when(q == nk - 1)
            def _():
                finish(acc_ref[...])

    ins = [a, b] + ([add] if has_add else [])
    in_specs = [a_spec, b_spec] + ([o_spec] if has_add else [])
    grid = (n // tn, m // tm, nk)
    acc = pltpu.VMEM((tm, tn) if nk > 1 else (SUBLANES, LANES), F32)
    out = jax.ShapeDtypeStruct((m, n), out_dtype)
    if carried is None:
        return pl.pallas_call(
            body, name=name, grid=grid, in_specs=in_specs, out_specs=o_spec, out_shape=out, scratch_shapes=[acc],
            compiler_params=_cparams(("parallel", "parallel", "arbitrary")),
        )(*ins)
    res = pl.pallas_call(
        _carrying(body, len(ins), 1, 1, carried, grid), name=name + "_carrying", grid=grid,
        in_specs=in_specs + [ANY] * len(carried.inputs), out_specs=[o_spec] + [ANY] * len(carried.out_shapes),
        out_shape=[out] + carried.out_shapes, scratch_shapes=[acc] + carried.scratch(),
        compiler_params=_cparams(("arbitrary", "arbitrary", "arbitrary")),
    )(*ins, *carried.inputs)
    return res[0], res[1:]


def _matmul_ln_bwd(a, b, dims, name, add, coef, ln, a_spec, b_spec, o_spec, sizes):
    m, n, tm, tn, nk = sizes
    hres, g = ln
    row = pl.BlockSpec((1, n), lambda j, i, q: (0, 0))
    sums = pl.BlockSpec((SUBLANES, n), lambda j, i, q: (0, 0))

    def body(a_ref, b_ref, add_ref, h_ref, g_ref, dh_ref, dhb_ref, dg_ref, db_ref, acc_ref):
        i, q = pl.program_id(1), pl.program_id(2)
        part = _dot(a_ref[...], b_ref[...], dims)

        @pl.when(jnp.logical_and(i == 0, q == 0))
        def _():
            dg_ref[...] = jnp.zeros_like(dg_ref)
            db_ref[...] = jnp.zeros_like(db_ref)

        @pl.when(q == 0)
        def _():
            acc_ref[...] = part

        @pl.when(q > 0)
        def _():
            acc_ref[...] += part

        @pl.when(q == nk - 1)
        def _():
            dy_v = acc_ref[...] + coef * add_ref[...]
            xhat, r = _ln_hat(h_ref[...])
            dh = _ln_bwd(dy_v * g_ref[...], xhat, r)
            dh_ref[...] = dh
            dhb_ref[...] = dh.astype(dhb_ref.dtype)
            dg_ref[...] += _fold8(dy_v * xhat)
            db_ref[...] += _fold8(dy_v)

    return pl.pallas_call(
        body, name=name + "_ln_bwd", grid=(1, m // tm, nk),
        in_specs=[a_spec, b_spec, o_spec, o_spec, row], out_specs=[o_spec, o_spec, sums, sums],
        out_shape=[jax.ShapeDtypeStruct((m, n), F32), jax.ShapeDtypeStruct((m, n), ACT),
                   jax.ShapeDtypeStruct((SUBLANES, n), F32), jax.ShapeDtypeStruct((SUBLANES, n), F32)],
        scratch_shapes=[pltpu.VMEM((tm, tn), F32)],
        compiler_params=_cparams(("arbitrary", "arbitrary", "arbitrary")),
    )(a, b, add, hres, g)


def _conv_taps(cur_ref, halo_ref, first):
    x = cur_ref[...]
    tb = x.shape[0]
    halo = jnp.where(first, 0.0, halo_ref[...])
    xc = jnp.concatenate([halo, x], axis=0)
    return [x] + [pltpu.roll(xc, s, 0)[SUBLANES:SUBLANES + tb] for s in range(1, CONV_K)]


def _conv_fwd(projm, conv_w, d, tb):
    t = projm.shape[0]
    heads = d // DN_DK
    hb = tb // SUBLANES

    def body(cur_ref, halo_ref, w_ref, o_ref):
        i, j = pl.program_id(0), pl.program_id(1)
        taps = _conv_taps(cur_ref, halo_ref, i == 0)
        y = taps[0] * w_ref[CONV_K - 1:CONV_K, :]
        for s in range(1, CONV_K):
            y = y + taps[s] * w_ref[CONV_K - 1 - s:CONV_K - s, :]
        act = y * _sigmoid(y)
        scale = jnp.where(j == 0, DN_DK ** -0.5, 1.0)
        for h in range(heads):
            seg = act[:, h * DN_DK:(h + 1) * DN_DK]
            r = lax.rsqrt(jnp.sum(seg * seg, axis=1, keepdims=True) + RMS_EPS) * scale
            o_ref[:, h * DN_DK:(h + 1) * DN_DK] = seg * jnp.where(j < 2, r, 1.0)

    blk = pl.BlockSpec((tb, d), lambda i, j: (i, j))
    return pl.pallas_call(
        body, name="conv_fwd", grid=(t // tb, 3),
        in_specs=[blk,
                  pl.BlockSpec((SUBLANES, d), lambda i, j: (jnp.maximum(i * hb - 1, 0), j)),
                  pl.BlockSpec((CONV_K, d), lambda i, j: (0, j))],
        out_specs=blk,
        out_shape=jax.ShapeDtypeStruct((t, 3 * d), F32),
        compiler_params=_cparams(("parallel", "parallel")),
    )(projm, projm, conv_w)


def _conv_bwd_dy(projm, conv_w, dqkv, d, tb):
    t = projm.shape[0]
    heads = d // DN_DK
    hb = tb // SUBLANES

    def body(cur_ref, halo_ref, w_ref, dout_ref, dy_ref, dw_ref):
        j, i = pl.program_id(0), pl.program_id(1)
        taps = _conv_taps(cur_ref, halo_ref, i == 0)
        y = taps[0] * w_ref[CONV_K - 1:CONV_K, :]
        for s in range(1, CONV_K):
            y = y + taps[s] * w_ref[CONV_K - 1 - s:CONV_K - s, :]
        sg = _sigmoid(y)
        act = y * sg
        dact = sg * (1.0 + y * (1.0 - sg))
        scale = jnp.where(j == 0, DN_DK ** -0.5, 1.0)
        for h in range(heads):
            cols = slice(h * DN_DK, (h + 1) * DN_DK)
            seg = act[:, cols]
            r = lax.rsqrt(jnp.sum(seg * seg, axis=1, keepdims=True) + RMS_EPS)
            nrm = seg * r
            dout = dout_ref[:, cols]
            ds = jnp.where(j < 2, (r * scale) * (dout - nrm * jnp.sum(dout * nrm, axis=1, keepdims=True)), dout)
            dy_ref[:, cols] = ds * dact[:, cols]
        dy = dy_ref[...]

        @pl.when(i == 0)
        def _():
            dw_ref[...] = jnp.zeros_like(dw_ref)

        for s in range(CONV_K):
            dw_ref[CONV_K - 1 - s] += _fold8(dy * taps[s])

    return pl.pallas_call(
        body, name="conv_bwd_dy", grid=(3, t // tb),
        in_specs=[pl.BlockSpec((tb, d), lambda j, i: (i, j)),
                  pl.BlockSpec((SUBLANES, d), lambda j, i: (jnp.maximum(i * hb - 1, 0), j)),
                  pl.BlockSpec((CONV_K, d), lambda j, i: (0, j)),
                  pl.BlockSpec((tb, d), lambda j, i: (i, j))],
        out_specs=[pl.BlockSpec((tb, d), lambda j, i: (i, j)),
                   pl.BlockSpec((CONV_K, SUBLANES, d), lambda j, i: (0, 0, j))],
        out_shape=[jax.ShapeDtypeStruct((t, 3 * d), F32),
                   jax.ShapeDtypeStruct((CONV_K, SUBLANES, 3 * d), F32)],
        compiler_params=_cparams(("parallel", "arbitrary")),
    )(projm, projm, conv_w, dqkv)


def _conv_bwd_dx(dy, conv_w, dprojm, d, tb):
    t = dy.shape[0]
    hb = tb // SUBLANES
    last = t // tb - 1

    def body(cur_ref, halo_ref, w_ref, alias_ref, o_ref):
        i = pl.program_id(0)
        cur = cur_ref[...]
        halo = jnp.where(i == last, 0.0, halo_ref[...])
        dc = jnp.concatenate([cur, halo], axis=0)
        acc = cur * w_ref[CONV_K - 1:CONV_K, :]
        for s in range(1, CONV_K):
            acc = acc + pltpu.roll(dc, tb + SUBLANES - s, 0)[:tb] * w_ref[CONV_K - 1 - s:CONV_K - s, :]
        o_ref[...] = acc.astype(o_ref.dtype)

    return pl.pallas_call(
        body, name="conv_bwd_dx", grid=(t // tb, 3),
        in_specs=[pl.BlockSpec((tb, d), lambda i, j: (i, j)),
                  pl.BlockSpec((SUBLANES, d), lambda i, j: (jnp.minimum((i + 1) * hb, t // SUBLANES - 1), j)),
                  pl.BlockSpec((CONV_K, d), lambda i, j: (0, j)),
                  ANY],
        out_specs=pl.BlockSpec((tb, d), lambda i, j: (i, j)),
        out_shape=jax.ShapeDtypeStruct(dprojm.shape, dprojm.dtype),
        input_output_aliases={3: 0},
        compiler_params=_cparams(("parallel", "parallel")),
    )(dy, dy, conv_w, dprojm)


def _beta_g(ba, alog, dtb):
    beta = _sigmoid(ba[:, :LANES])
    xa = ba[:, LANES:] + dtb
    softplus = jnp.maximum(xa, 0.0) + jnp.log(1.0 + jnp.exp(-jnp.abs(xa)))
    ea = jnp.exp(alog)
    return beta, -ea * softplus, ea, _sigmoid(xa)


def _inv_corrections(mats):
    ys = [-a for a in mats]
    ps = [_dot(a, a) for a in mats]
    steps = int(math.log2(CHUNK)) - 1
    for it in range(steps):
        ys = [y + p + _dot(y, p) for y, p in zip(ys, ps)]
        if it < steps - 1:
            ps = [_dot(p, p) for p in ps]
    return ys


def _chunk_masks():
    row = lax.broadcasted_iota(jnp.int32, (CHUNK, CHUNK), 0)
    col = lax.broadcasted_iota(jnp.int32, (CHUNK, CHUNK), 1)
    return row >= col, row > col, row <= col


def _col_of(mat, lane_idx, h):
    return jnp.sum(jnp.where(lane_idx == h, mat, 0.0), axis=1, keepdims=True)


def _row_of(mat, sub_idx, h):
    return jnp.sum(jnp.where(sub_idx == h, mat, 0.0), axis=0, keepdims=True)


def _phases(fns):
    return fns if len(fns) == 3 else (fns[0], lambda: None, fns[1])


def _carrying(compute, n_in, n_out, n_scratch, carried, grid):
    if carried is None:
        return compute
    ci, co = len(carried.inputs), len(carried.out_shapes)

    def body(*refs):
        ins, c_in = refs[:n_in], refs[n_in:n_in + ci]
        outs, c_out = refs[n_in + ci:n_in + ci + n_out], refs[n_in + ci + n_out:n_in + ci + n_out + co]
        scratch = refs[n_in + ci + n_out + co:]
        start, middle, finish = _phases(carried.copies(c_in, c_out, scratch[n_scratch], scratch[n_scratch + 1]))
        step, total = 0, 1
        for axis, steps in enumerate(grid):
            step = step * steps + pl.program_id(axis)
            total *= steps

        @pl.when(step == 0)
        def _():
            start()

        compute(*ins, *outs, *scratch[:n_scratch])

        @pl.when(step == (3 * total) // 4)
        def _():
            middle()

        @pl.when(step == total - 1)
        def _():
            finish()

    return body


def _dn_fwd(qkv, ba, alog, dtb, d, carried=None):
    t = qkv.shape[0]
    heads = d // DN_DK
    n_chunks = t // CHUNK
    grp = DN_GROUP_FWD if n_chunks % DN_GROUP_FWD == 0 else 1
    span = grp * CHUNK
    extra = carried or _Carried([], [], 0, None)

    def compute(qkv_ref, ba_ref, al_ref, dt_ref, o_ref, s_ref, y_ref, state):
        @pl.when(pl.program_id(0) == 0)
        def _():
            state[...] = jnp.zeros_like(state)

        tril, strict, _ = _chunk_masks()
        beta, g, _, _ = _beta_g(ba_ref[...], al_ref[...], dt_ref[...])
        lane = lax.broadcasted_iota(jnp.int32, (CHUNK, LANES), 1)
        sub = lax.broadcasted_iota(jnp.int32, (LANES, CHUNK), 0)
        rowc = lax.broadcasted_iota(jnp.int32, (CHUNK, 1), 0)
        hs = range(heads)
        units = [(c, h) for c in range(grp) for h in hs]
        un = range(len(units))
        rows = lambda c: slice(c * CHUNK, (c + 1) * CHUNK)
        gc = [_dot(jnp.where(tril, 1.0, 0.0), g[rows(c)], NN, HIGHEST) for c in range(grp)]
        gct = [m.T for m in gc]
        q = [qkv_ref[rows(c), h * DN_DK:(h + 1) * DN_DK] for c, h in units]
        k = [qkv_ref[rows(c), d + h * DN_DK:d + (h + 1) * DN_DK] for c, h in units]
        v = [qkv_ref[rows(c), 2 * d + h * DN_DK:2 * d + (h + 1) * DN_DK] for c, h in units]
        gch = [_col_of(gc[c], lane, h) for c, h in units]
        bh = [_col_of(beta[rows(c)], lane, h) for c, h in units]
        dec = [jnp.where(tril, jnp.exp(gch[n] - _row_of(gct[c], sub, h)), 0.0) for n, (c, h) in enumerate(units)]
        egc = [jnp.exp(gch[n]) for n in un]
        gl = [jnp.sum(jnp.where(rowc == CHUNK - 1, gch[n], 0.0), axis=0, keepdims=True) for n in un]
        kb = [k[n] * bh[n] for n in un]
        a = [jnp.where(strict, _dot(kb[n], k[n], NT) * dec[n], 0.0) for n in un]
        p = [_dot(q[n], k[n], NT) * dec[n] for n in un]
        ycor = _inv_corrections(a)
        rhs = [jnp.concatenate([v[n] * bh[n], kb[n] * egc[n]], axis=1) for n in un]
        sol = [rhs[n] + _dot(ycor[n], rhs[n]) for n in un]
        qg = [q[n] * egc[n] for n in un]
        kd = [k[n] * jnp.exp(gl[n] - gch[n]) for n in un]
        egl = [jnp.exp(gl[n]) for n in un]
        s_cur, s_in, o = [state[h] for h in hs], [], []
        for c in range(grp):
            ns = [c * heads + h for h in hs]
            vn = [sol[n][:, :DN_DK] - _dot(sol[n][:, DN_DK:], s_cur[h]) for h, n in enumerate(ns)]
            o += [_dot(qg[n], s_cur[h]) + _dot(p[n], vn[h]) for h, n in enumerate(ns)]
            s_in += s_cur
            s_cur = [s_cur[h] * egl[n] + _dot(kd[n], vn[h], TN) for h, n in enumerate(ns)]
        for n, (c, h) in enumerate(units):
            o_ref[rows(c), h * DN_DK:(h + 1) * DN_DK] = o[n]
            s_ref[c, h] = s_in[n]
            y_ref[h, rows(c), :] = ycor[n]
        for h in hs:
            state[h] = s_cur[h]

    res = pl.pallas_call(
        _carrying(compute, 4, 3, 1, carried, (n_chunks // grp,)),
        name="dn_fwd_carrying" if carried else "dn_fwd", grid=(n_chunks // grp,),
        in_specs=[pl.BlockSpec((span, 3 * d), lambda i: (i, 0)),
                  pl.BlockSpec((span, 2 * LANES), lambda i: (i, 0)),
                  pl.BlockSpec((1, LANES), lambda i: (0, 0)),
                  pl.BlockSpec((1, LANES), lambda i: (0, 0))] + [ANY] * len(extra.inputs),
        out_specs=[pl.BlockSpec((span, d), lambda i: (i, 0)),
                   pl.BlockSpec((grp, heads, DN_DK, DN_DK), lambda i: (i, 0, 0, 0)),
                   pl.BlockSpec((heads, span, CHUNK), lambda i: (0, i, 0))] + [ANY] * len(extra.out_shapes),
        out_shape=[jax.ShapeDtypeStruct((t, d), F32),
                   jax.ShapeDtypeStruct((n_chunks, heads, DN_DK, DN_DK), F32),
                   jax.ShapeDtypeStruct((heads, t, CHUNK), F32)] + extra.out_shapes,
        scratch_shapes=[pltpu.VMEM((heads, DN_DK, DN_DK), F32)] + (extra.scratch() if carried else []),
        compiler_params=_cparams(("arbitrary",)),
    )(qkv, ba, alog, dtb, *extra.inputs)
    return res[:3], res[3:]


def _dn_bwd(qkv, ba, alog, dtb, dout, states, ycors, d, carried=None):
    t = qkv.shape[0]
    heads = d // DN_DK
    n_chunks = t // CHUNK
    grp = DN_GROUP if n_chunks % DN_GROUP == 0 else 1
    span = grp * CHUNK
    rev = lambda i: n_chunks // grp - 1 - i
    extra = carried or _Carried([], [], 0, None)

    def compute(qkv_ref, ba_ref, al_ref, dt_ref, do_ref, s_ref, y_ref,
                dqkv_ref, dba_ref, dal_ref, ddt_ref, dstate):
        @pl.when(pl.program_id(0) == 0)
        def _():
            dstate[...] = jnp.zeros_like(dstate)
            dal_ref[...] = jnp.zeros_like(dal_ref)
            ddt_ref[...] = jnp.zeros_like(ddt_ref)

        tril, strict, triu = _chunk_masks()
        beta, g, ea, sig_a = _beta_g(ba_ref[...], al_ref[...], dt_ref[...])
        lane = lax.broadcasted_iota(jnp.int32, (CHUNK, LANES), 1)
        sub = lax.broadcasted_iota(jnp.int32, (LANES, CHUNK), 0)
        rowc = lax.broadcasted_iota(jnp.int32, (CHUNK, 1), 0)
        hs = range(heads)
        units = [(c, h) for c in range(grp) for h in hs]
        un = range(len(units))
        rows = lambda c: slice(c * CHUNK, (c + 1) * CHUNK)
        rsum = lambda x_: jnp.sum(x_, axis=1, keepdims=True)
        gc = [_dot(jnp.where(tril, 1.0, 0.0), g[rows(c)], NN, HIGHEST) for c in range(grp)]
        gct = [m.T for m in gc]
        q = [qkv_ref[rows(c), h * DN_DK:(h + 1) * DN_DK] for c, h in units]
        k = [qkv_ref[rows(c), d + h * DN_DK:d + (h + 1) * DN_DK] for c, h in units]
        v = [qkv_ref[rows(c), 2 * d + h * DN_DK:2 * d + (h + 1) * DN_DK] for c, h in units]
        dout_h = [do_ref[rows(c), h * DN_DK:(h + 1) * DN_DK] for c, h in units]
        s0 = [s_ref[c, h] for c, h in units]
        ycor = [y_ref[h, rows(c), :] for c, h in units]
        gch = [_col_of(gc[c], lane, h) for c, h in units]
        bh = [_col_of(beta[rows(c)], lane, h) for c, h in units]
        dec = [jnp.where(tril, jnp.exp(gch[n] - _row_of(gct[c], sub, h)), 0.0) for n, (c, h) in enumerate(units)]
        egc = [jnp.exp(gch[n]) for n in un]
        gl = [jnp.sum(jnp.where(rowc == CHUNK - 1, gch[n], 0.0), axis=0, keepdims=True) for n in un]
        egl = [jnp.exp(gl[n]) for n in un]
        ekd = [jnp.exp(gl[n] - gch[n]) for n in un]
        kb = [k[n] * bh[n] for n in un]
        kd = [k[n] * ekd[n] for n in un]
        qg = [q[n] * egc[n] for n in un]
        kbg = [kb[n] * egc[n] for n in un]
        a = [jnp.where(strict, _dot(kb[n], k[n], NT) * dec[n], 0.0) for n in un]
        p = [_dot(q[n], k[n], NT) * dec[n] for n in un]
        rhs = [jnp.concatenate([v[n] * bh[n], kbg[n]], axis=1) for n in un]
        sol = [rhs[n] + _dot(ycor[n], rhs[n]) for n in un]
        w = [sol[n][:, DN_DK:] for n in un]
        vn = [sol[n][:, :DN_DK] - _dot(w[n], s0[n]) for n in un]
        dqg = [_dot(dout_h[n], s0[n], NT) for n in un]
        dp = [jnp.where(tril, _dot(dout_h[n], vn[n], NT), 0.0) for n in un]
        pdo = [_dot(p[n], dout_h[n], TN) for n in un]
        qdo = [_dot(qg[n], dout_h[n], TN) for n in un]
        ds_cur = [dstate[h] for h in hs]
        dsn, dvn = [None] * len(units), [None] * len(units)
        for c in reversed(range(grp)):
            for h in hs:
                dsn[c * heads + h] = ds_cur[h]
            for h in hs:
                n = c * heads + h
                dvn[n] = pdo[n] + _dot(kd[n], ds_cur[h])
            ds_cur = [qdo[c * heads + h] + egl[c * heads + h] * ds_cur[h]
                      - _dot(w[c * heads + h], dvn[c * heads + h], TN) for h in hs]
        dkd = [_dot(vn[n], dsn[n], NT) for n in un]
        dw = [-_dot(dvn[n], s0[n], NT) for n in un]
        dgl = [jnp.sum(rsum(dsn[n] * s0[n]), axis=0, keepdims=True) * egl[n] for n in un]
        dsol = [jnp.concatenate([dvn[n], dw[n]], axis=1) for n in un]
        drhs = [dsol[n] + _dot(ycor[n], dsol[n], TN) for n in un]
        dvb = [drhs[n][:, :DN_DK] for n in un]
        dkbg = [drhs[n][:, DN_DK:] for n in un]
        da = [jnp.where(strict, -_dot(drhs[n], sol[n], NT), 0.0) for n in un]
        dma = [da[n] * dec[n] for n in un]
        dmp = [dp[n] * dec[n] for n in un]
        dkb = [_dot(dma[n], k[n]) + dkbg[n] * egc[n] for n in un]
        dq = [_dot(dmp[n], k[n]) + dqg[n] * egc[n] for n in un]
        dk = [_dot(dma[n], kb[n], TN) + _dot(dmp[n], q[n], TN) + dkd[n] * ekd[n] + dkb[n] * bh[n] for n in un]
        e = [da[n] * a[n] + dp[n] * p[n] for n in un]
        colsum = [jnp.sum(e[n], axis=0, keepdims=True) for n in un]
        tkd = [rsum(dkd[n] * kd[n]) for n in un]
        for n, (c, h) in enumerate(units):
            dqkv_ref[rows(c), h * DN_DK:(h + 1) * DN_DK] = dq[n]
            dqkv_ref[rows(c), d + h * DN_DK:d + (h + 1) * DN_DK] = dk[n]
            dqkv_ref[rows(c), 2 * d + h * DN_DK:2 * d + (h + 1) * DN_DK] = dvb[n] * bh[n]
        for h in hs:
            dstate[h] = ds_cur[h]
        valid = lane < heads
        dal_acc = jnp.zeros((SUBLANES, LANES), F32)
        ddt_acc = jnp.zeros((SUBLANES, LANES), F32)
        for c in range(grp):
            dgc_all = jnp.zeros((CHUNK, LANES), F32)
            dbeta_all = jnp.zeros((CHUNK, LANES), F32)
            colsums = jnp.zeros((LANES, CHUNK), F32)
            for h in hs:
                n = c * heads + h
                dgc = rsum(e[n]) + rsum(dqg[n] * qg[n]) - tkd[n] + rsum(dkbg[n] * kbg[n])
                dgc = dgc + jnp.where(rowc == CHUNK - 1, dgl[n] + jnp.sum(tkd[n], axis=0, keepdims=True), 0.0)
                dgc_all = dgc_all + jnp.where(lane == h, dgc, 0.0)
                colsums = colsums + jnp.where(sub == h, colsum[n], 0.0)
                dbeta_all = dbeta_all + jnp.where(lane == h, rsum(dkb[n] * k[n]) + rsum(dvb[n] * v[n]), 0.0)
            dg = _dot(jnp.where(triu, 1.0, 0.0), dgc_all - colsums.T, NN, HIGHEST)
            beta_c = beta[rows(c)]
            dbl = jnp.where(valid, dbeta_all * beta_c * (1.0 - beta_c), 0.0)
            dal = jnp.where(valid, -dg * ea * sig_a[rows(c)], 0.0)
            dba_ref[rows(c), :LANES] = dbl.astype(dba_ref.dtype)
            dba_ref[rows(c), LANES:] = dal.astype(dba_ref.dtype)
            dal_acc = dal_acc + _fold8(jnp.where(valid, dg * g[rows(c)], 0.0))
            ddt_acc = ddt_acc + _fold8(dal)
        dal_ref[...] += dal_acc
        ddt_ref[...] += ddt_acc

    res = pl.pallas_call(
        _carrying(compute, 7, 4, 1, carried, (n_chunks // grp,)),
        name="dn_bwd_carrying" if carried else "dn_bwd", grid=(n_chunks // grp,),
        in_specs=[pl.BlockSpec((span, 3 * d), lambda i: (rev(i), 0)),
                  pl.BlockSpec((span, 2 * LANES), lambda i: (rev(i), 0)),
                  pl.BlockSpec((1, LANES), lambda i: (0, 0)),
                  pl.BlockSpec((1, LANES), lambda i: (0, 0)),
                  pl.BlockSpec((span, d), lambda i: (rev(i), 0)),
                  pl.BlockSpec((grp, heads, DN_DK, DN_DK), lambda i: (rev(i), 0, 0, 0)),
                  pl.BlockSpec((heads, span, CHUNK), lambda i: (0, rev(i), 0))] + [ANY] * len(extra.inputs),
        out_specs=[pl.BlockSpec((span, 3 * d), lambda i: (rev(i), 0)),
                   pl.BlockSpec((span, 2 * LANES), lambda i: (rev(i), 0)),
                   pl.BlockSpec((SUBLANES, LANES), lambda i: (0, 0)),
                   pl.BlockSpec((SUBLANES, LANES), lambda i: (0, 0))] + [ANY] * len(extra.out_shapes),
        out_shape=[jax.ShapeDtypeStruct((t, 3 * d), F32),
                   jax.ShapeDtypeStruct((t, 2 * LANES), ACT),
                   jax.ShapeDtypeStruct((SUBLANES, LANES), F32),
                   jax.ShapeDtypeStruct((SUBLANES, LANES), F32)] + extra.out_shapes,
        scratch_shapes=[pltpu.VMEM((heads, DN_DK, DN_DK), F32)] + (extra.scratch() if carried else []),
        compiler_params=_cparams(("arbitrary",)),
    )(qkv, ba, alog, dtb, dout, states, ycors, *extra.inputs)
    return res[:4], res[4:]


def _sgu_mask():
    row = lax.broadcasted_iota(jnp.int32, (SGU_BLOCK, SGU_BLOCK), 0)
    col = lax.broadcasted_iota(jnp.int32, (SGU_BLOCK, SGU_BLOCK), 1)
    sh = int(math.log2(CHUNK))
    return lax.shift_right_logical(row, sh) >= lax.shift_right_logical(col, sh)


def _gate_sgu_fwd(o, projm, onw, lng, lnb, ws, bst, d):
    t = o.shape[0]
    heads, groups = d // DN_DK, d // SGU_GROUP_DIM
    tb = _tile(t, SGU_WINDOWS * SGU_BLOCK, SGU_BLOCK)
    row_spec = pl.BlockSpec((1, d), lambda i: (0, 0))

    def body(o_ref, z_ref, u_ref, v_ref, onw_ref, lng_ref, lnb_ref, ws_ref, bst_ref, ya_ref, yb_ref):
        for h in range(heads):
            cols = slice(h * DN_DK, (h + 1) * DN_DK)
            oh, zh = o_ref[:, cols], z_ref[:, cols]
            r = lax.rsqrt(jnp.mean(oh * oh, axis=1, keepdims=True) + RMS_EPS)
            ya_ref[:, cols] = (oh * r * onw_ref[:, cols] * (zh * _sigmoid(zh))).astype(ya_ref.dtype)
        xhat, _ = _ln_hat(_gelu(v_ref[...]))
        vgn = xhat * lng_ref[...] + lnb_ref[...]
        mask = _sgu_mask()
        lane = lax.broadcasted_iota(jnp.int32, (SGU_BLOCK, LANES), 1)
        bst_v = bst_ref[...]
        for gi in range(groups):
            cols = slice(gi * SGU_GROUP_DIM, (gi + 1) * SGU_GROUP_DIM)
            wsg = jnp.where(mask, ws_ref[gi], 0.0)
            bias = _col_of(bst_v, lane, gi)
            for win in range(tb // SGU_BLOCK):
                rows = slice(win * SGU_BLOCK, (win + 1) * SGU_BLOCK)
                sp = _dot(wsg, vgn[rows, cols]) + bias
                yb_ref[rows, cols] = (_gelu(u_ref[rows, cols]) * sp).astype(yb_ref.dtype)

    return pl.pallas_call(
        body, name="gate_sgu_fwd", grid=(t // tb,),
        in_specs=[pl.BlockSpec((tb, d), lambda i: (i, 0)),
                  pl.BlockSpec((tb, d), lambda i: (i, 3)),
                  pl.BlockSpec((tb, d), lambda i: (i, 4)),
                  pl.BlockSpec((tb, d), lambda i: (i, 5)),
                  row_spec, row_spec, row_spec,
                  pl.BlockSpec((groups, SGU_BLOCK, SGU_BLOCK), lambda i: (0, 0, 0)),
                  pl.BlockSpec((SGU_BLOCK, LANES), lambda i: (0, 0))],
        out_specs=[pl.BlockSpec((tb, d), lambda i: (i, 0)), pl.BlockSpec((tb, d), lambda i: (i, 0))],
        out_shape=[jax.ShapeDtypeStruct((t, d), ACT), jax.ShapeDtypeStruct((t, d), ACT)],
        compiler_params=_cparams(("parallel",)),
    )(o, projm, projm, projm, onw, lng, lnb, ws, bst)


def _gate_sgu_bwd(dya, dyb, o, projm, onw, lng, lnb, ws, bst, dprojm, d, carried=None):
    t = o.shape[0]
    heads, groups = d // DN_DK, d // SGU_GROUP_DIM
    tb = _tile(t, SGU_WINDOWS * SGU_BLOCK, SGU_BLOCK)
    extra = carried or _Carried([], [], 0, None)
    row_spec = pl.BlockSpec((1, d), lambda i: (0, 0))
    acc_row = pl.BlockSpec((SUBLANES, d), lambda i: (0, 0))

    def body(dya_ref, dyb_ref, o_ref, z_ref, u_ref, v_ref, onw_ref, lng_ref, lnb_ref, ws_ref, bst_ref, alias_ref,
             do_ref, dp_ref, donw_ref, dlng_ref, dlnb_ref, dws_ref, dbst_ref):
        @pl.when(pl.program_id(0) == 0)
        def _():
            for r_ in (donw_ref, dlng_ref, dlnb_ref, dws_ref, dbst_ref):
                r_[...] = jnp.zeros_like(r_)

        donw = jnp.zeros((SUBLANES, DN_DK), F32)
        for h in range(heads):
            cols = slice(h * DN_DK, (h + 1) * DN_DK)
            oh, zh, dyah, wh = o_ref[:, cols], z_ref[:, cols], dya_ref[:, cols], onw_ref[:, cols]
            r = lax.rsqrt(jnp.mean(oh * oh, axis=1, keepdims=True) + RMS_EPS)
            on = oh * r
            sz = _sigmoid(zh)
            silu_z = zh * sz
            don = dyah * wh * silu_z
            dp_ref[:, cols] = (dyah * on * wh * (sz * (1.0 + zh * (1.0 - sz)))).astype(dp_ref.dtype)
            donw = donw + _fold8(dyah * on * silu_z)
            do_ref[:, cols] = r * (don - on * jnp.mean(don * on, axis=1, keepdims=True))
        donw_ref[...] += donw

        vgp, up = v_ref[...], u_ref[...]
        xhat, rstd = _ln_hat(_gelu(vgp))
        lng_v = lng_ref[...]
        vgn = xhat * lng_v + lnb_ref[...]
        ua = _gelu(up)
        mask = _sgu_mask()
        lane = lax.broadcasted_iota(jnp.int32, (SGU_BLOCK, LANES), 1)
        bst_v = bst_ref[...]
        dbst = jnp.zeros((SGU_BLOCK, LANES), F32)
        dvgn_parts, dua_parts = [], []
        for gi in range(groups):
            cols = slice(gi * SGU_GROUP_DIM, (gi + 1) * SGU_GROUP_DIM)
            wsg = jnp.where(mask, ws_ref[gi], 0.0)
            bias = _col_of(bst_v, lane, gi)
            dws = jnp.zeros((SGU_BLOCK, SGU_BLOCK), F32)
            dvgn_g, dua_g = [], []
            for win in range(tb // SGU_BLOCK):
                rows = slice(win * SGU_BLOCK, (win + 1) * SGU_BLOCK)
                vg_g, dyb_g = vgn[rows, cols], dyb_ref[rows, cols]
                sp = _dot(wsg, vg_g) + bias
                dsp = dyb_g * ua[rows, cols]
                dua_g.append(dyb_g * sp)
                dws = dws + _dot(dsp, vg_g, NT)
                dbst = dbst + jnp.where(lane == gi, jnp.sum(dsp, axis=1, keepdims=True), 0.0)
                dvgn_g.append(_dot(wsg, dsp, TN))
            dws_ref[gi] += jnp.where(mask, dws, 0.0)
            dvgn_parts.append(jnp.concatenate(dvgn_g, axis=0))
            dua_parts.append(jnp.concatenate(dua_g, axis=0))
        dbst_ref[...] += dbst
        dvgn = jnp.concatenate(dvgn_parts, axis=1)
        dua = jnp.concatenate(dua_parts, axis=1)
        dlng_ref[...] += _fold8(dvgn * xhat)
        dlnb_ref[...] += _fold8(dvgn)
        dvga = _ln_bwd(dvgn * lng_v, xhat, rstd)
        dp_ref[:, d:2 * d] = (dua * _gelu_grad(up)).astype(dp_ref.dtype)
        dp_ref[:, 2 * d:] = (dvga * _gelu_grad(vgp)).astype(dp_ref.dtype)

    res = pl.pallas_call(
        _carrying(body, 12, 7, 0, carried, (t // tb,)),
        name="gate_sgu_bwd_carrying" if carried else "gate_sgu_bwd", grid=(t // tb,),
        in_specs=[pl.BlockSpec((tb, d), lambda i: (i, 0)),
                  pl.BlockSpec((tb, d), lambda i: (i, 0)),
                  pl.BlockSpec((tb, d), lambda i: (i, 0)),
                  pl.BlockSpec((tb, d), lambda i: (i, 3)),
                  pl.BlockSpec((tb, d), lambda i: (i, 4)),
                  pl.BlockSpec((tb, d), lambda i: (i, 5)),
                  row_spec, row_spec, row_spec,
                  pl.BlockSpec((groups, SGU_BLOCK, SGU_BLOCK), lambda i: (0, 0, 0)),
                  pl.BlockSpec((SGU_BLOCK, LANES), lambda i: (0, 0)),
                  ANY] + [ANY] * len(extra.inputs),
        out_specs=[pl.BlockSpec((tb, d), lambda i: (i, 0)),
                   pl.BlockSpec((tb, 3 * d), lambda i: (i, 1)),
                   pl.BlockSpec((SUBLANES, DN_DK), lambda i: (0, 0)),
                   acc_row, acc_row,
                   pl.BlockSpec((groups, SGU_BLOCK, SGU_BLOCK), lambda i: (0, 0, 0)),
                   pl.BlockSpec((SGU_BLOCK, LANES), lambda i: (0, 0))] + [ANY] * len(extra.out_shapes),
        out_shape=[jax.ShapeDtypeStruct((t, d), F32),
                   jax.ShapeDtypeStruct(dprojm.shape, dprojm.dtype),
                   jax.ShapeDtypeStruct((SUBLANES, DN_DK), F32),
                   jax.ShapeDtypeStruct((SUBLANES, d), F32),
                   jax.ShapeDtypeStruct((SUBLANES, d), F32),
                   jax.ShapeDtypeStruct((groups, SGU_BLOCK, SGU_BLOCK), F32),
                   jax.ShapeDtypeStruct((SGU_BLOCK, LANES), F32)] + extra.out_shapes,
        input_output_aliases={11: 1},
        scratch_shapes=extra.scratch() if carried else [],
        compiler_params=_cparams(("arbitrary",)),
    )(dya, dyb, o, projm, projm, projm, onw, lng, lnb, ws, bst, dprojm, *extra.inputs)
    return res[:7], res[7:]


def _mix_fwd(ya, yb, projm, x, wpa, wpb, wo, g1, b1, d, tb):
    t = x.shape[0]
    blk = pl.BlockSpec((tb, d), lambda i: (i, 0))
    wspec = pl.BlockSpec((d, d), lambda i: (0, 0))
    row_spec = pl.BlockSpec((1, d), lambda i: (0, 0))

    def body(ya_ref, yb_ref, ga_ref, gb_ref, x_ref, wpa_ref, wpb_ref, wo_ref, g_ref, b_ref,
             pa_ref, pb_ref, m_ref, h_ref, x1_ref, x1b_ref):
        pa = _dot(ya_ref[...], wpa_ref[...])
        pb = _dot(yb_ref[...], wpb_ref[...])
        m = _sigmoid(ga_ref[...]) * pa + _sigmoid(gb_ref[...]) * pb
        hres = ALPHA * x_ref[...] + _dot(m, wo_ref[...])
        xhat, _ = _ln_hat(hres)
        x1 = xhat * g_ref[...] + b_ref[...]
        pa_ref[...] = pa.astype(pa_ref.dtype)
        pb_ref[...] = pb.astype(pb_ref.dtype)
        m_ref[...] = m.astype(m_ref.dtype)
        h_ref[...] = hres
        x1_ref[...] = x1
        x1b_ref[...] = x1.astype(x1b_ref.dtype)

    f32_out = jax.ShapeDtypeStruct((t, d), F32)
    bf_out = jax.ShapeDtypeStruct((t, d), ACT)
    return pl.pallas_call(
        body, name="mix_fwd", grid=(t // tb,),
        in_specs=[blk, blk, pl.BlockSpec((tb, d), lambda i: (i, 6)), pl.BlockSpec((tb, d), lambda i: (i, 7)),
                  blk, wspec, wspec, wspec, row_spec, row_spec],
        out_specs=[blk] * 6,
        out_shape=[bf_out, bf_out, bf_out, f32_out, f32_out, bf_out],
        compiler_params=_cparams(("parallel",)),
    )(ya, yb, projm, projm, x, wpa, wpb, wo, g1, b1)


def _mix_bwd(dmix, pa, pb, projm, wpa, wpb, wo, d, tb):
    t = dmix.shape[0]
    blk = pl.BlockSpec((tb, d), lambda i: (i, 0))
    wspec = pl.BlockSpec((d, d), lambda i: (0, 0))

    def body(dmix_ref, pa_ref, pb_ref, ga_ref, gb_ref, wpa_ref, wpb_ref, wo_ref,
             dpa_ref, dpb_ref, dya_ref, dyb_ref, dg_ref):
        dm = _dot(dmix_ref[...], wo_ref[...], NT)
        sa, sb = _sigmoid(ga_ref[...]), _sigmoid(gb_ref[...])
        dpa, dpb = dm * sa, dm * sb
        dpa_ref[...] = dpa.astype(dpa_ref.dtype)
        dpb_ref[...] = dpb.astype(dpb_ref.dtype)
        dg_ref[:, :d] = (dm * pa_ref[...].astype(F32) * sa * (1.0 - sa)).astype(dg_ref.dtype)
        dg_ref[:, d:] = (dm * pb_ref[...].astype(F32) * sb * (1.0 - sb)).astype(dg_ref.dtype)
        dya_ref[...] = _dot(dpa, wpa_ref[...], NT)
        dyb_ref[...] = _dot(dpb, wpb_ref[...], NT)

    return pl.pallas_call(
        body, name="mix_bwd", grid=(t // tb,),
        in_specs=[blk, blk, blk, pl.BlockSpec((tb, d), lambda i: (i, 6)), pl.BlockSpec((tb, d), lambda i: (i, 7)),
                  wspec, wspec, wspec],
        out_specs=[blk, blk, blk, blk, pl.BlockSpec((tb, 2 * d), lambda i: (i, 3))],
        out_shape=[jax.ShapeDtypeStruct((t, d), ACT), jax.ShapeDtypeStruct((t, d), ACT),
                   jax.ShapeDtypeStruct((t, d), F32), jax.ShapeDtypeStruct((t, d), F32),
                   jax.ShapeDtypeStruct((t, 8 * d), ACT)],
        compiler_params=_cparams(("parallel",)),
    )(dmix, pa, pb, projm, projm, wpa, wpb, wo)


def _ffn_tail_fwd(gu, wd, x1, g, b, tb):
    t, d = x1.shape
    f = wd.shape[0]
    fc = _tile(f, MM_TILE)
    blk = pl.BlockSpec((tb, d), lambda i: (i, 0))
    row_spec = pl.BlockSpec((1, d), lambda i: (0, 0))

    def body(gu_ref, wd_ref, x_ref, g_ref, b_ref, a_ref, h_ref, y_ref, yb_ref):
        ffn = jnp.zeros((tb, d), F32)
        for c in range(f // fc):
            gp = gu_ref[:, c * fc:(c + 1) * fc].astype(F32)
            act = (gp * _sigmoid(gp) * gu_ref[:, f + c * fc:f + (c + 1) * fc].astype(F32)).astype(a_ref.dtype)
            a_ref[:, c * fc:(c + 1) * fc] = act
            ffn = ffn + _dot(act, wd_ref[c * fc:(c + 1) * fc, :])
        hres = ALPHA * x_ref[...] + ffn
        xhat, _ = _ln_hat(hres)
        y = xhat * g_ref[...] + b_ref[...]
        h_ref[...] = hres
        y_ref[...] = y
        yb_ref[...] = y.astype(yb_ref.dtype)

    return pl.pallas_call(
        body, name="ffn_tail_fwd", grid=(t // tb,),
        in_specs=[pl.BlockSpec((tb, 2 * f), lambda i: (i, 0)), pl.BlockSpec((f, d), lambda i: (0, 0)),
                  blk, row_spec, row_spec],
        out_specs=[pl.BlockSpec((tb, f), lambda i: (i, 0)), blk, blk, blk],
        out_shape=[jax.ShapeDtypeStruct((t, f), ACT), jax.ShapeDtypeStruct((t, d), F32),
                   jax.ShapeDtypeStruct((t, d), F32), jax.ShapeDtypeStruct((t, d), ACT)],
        compiler_params=_cparams(("parallel",)),
    )(gu, wd, x1, g, b)


def _ffn_tail_bwd(dh, wd, gu, tb):
    t, d = dh.shape
    f = wd.shape[0]
    fc = _tile(f, MM_TILE)

    def body(dh_ref, wd_ref, gu_ref, dgu_ref):
        dh_v = dh_ref[...]
        for c in range(f // fc):
            da = _dot(dh_v, wd_ref[c * fc:(c + 1) * fc, :], NT)
            gp = gu_ref[:, c * fc:(c + 1) * fc].astype(F32)
            sg = _sigmoid(gp)
            dgu_ref[:, c * fc:(c + 1) * fc] = (
                da * gu_ref[:, f + c * fc:f + (c + 1) * fc].astype(F32) * sg * (1.0 + gp * (1.0 - sg))
            ).astype(dgu_ref.dtype)
            dgu_ref[:, f + c * fc:f + (c + 1) * fc] = (da * gp * sg).astype(dgu_ref.dtype)

    return pl.pallas_call(
        body, name="ffn_tail_bwd", grid=(t // tb,),
        in_specs=[pl.BlockSpec((tb, d), lambda i: (i, 0)), pl.BlockSpec((f, d), lambda i: (0, 0)),
                  pl.BlockSpec((tb, 2 * f), lambda i: (i, 0))],
        out_specs=pl.BlockSpec((tb, 2 * f), lambda i: (i, 0)),
        out_shape=jax.ShapeDtypeStruct((t, 2 * f), ACT),
        compiler_params=_cparams(("parallel",)),
    )(dh, wd, gu)


def _ffn_head_bwd(dgu, wgu, dh2, hres, g, tb):
    t, d = dh2.shape
    f2 = wgu.shape[1]
    blk = pl.BlockSpec((tb, d), lambda i: (i, 0))
    acc = pl.BlockSpec((SUBLANES, d), lambda i: (0, 0))

    def body(dgu_ref, w_ref, dh2_ref, h_ref, g_ref, dh_ref, dhb_ref, dg_ref, db_ref):
        @pl.when(pl.program_id(0) == 0)
        def _():
            dg_ref[...] = jnp.zeros_like(dg_ref)
            db_ref[...] = jnp.zeros_like(db_ref)

        dy_v = _dot(dgu_ref[...], w_ref[...], NT) + ALPHA * dh2_ref[...]
        xhat, r = _ln_hat(h_ref[...])
        dh = _ln_bwd(dy_v * g_ref[...], xhat, r)
        dh_ref[...] = dh
        dhb_ref[...] = dh.astype(dhb_ref.dtype)
        dg_ref[...] += _fold8(dy_v * xhat)
        db_ref[...] += _fold8(dy_v)

    return pl.pallas_call(
        body, name="ffn_head_bwd", grid=(t // tb,),
        in_specs=[pl.BlockSpec((tb, f2), lambda i: (i, 0)), pl.BlockSpec((d, f2), lambda i: (0, 0)),
                  blk, blk, pl.BlockSpec((1, d), lambda i: (0, 0))],
        out_specs=[blk, blk, acc, acc],
        out_shape=[jax.ShapeDtypeStruct((t, d), F32), jax.ShapeDtypeStruct((t, d), ACT),
                   jax.ShapeDtypeStruct((SUBLANES, d), F32), jax.ShapeDtypeStruct((SUBLANES, d), F32)],
        compiler_params=_cparams(("arbitrary",)),
    )(dgu, wgu, dh2, hres, g)


def _loss_ln_bwd(y, target, hres, g, tb):
    t, d = y.shape
    blk = pl.BlockSpec((tb, d), lambda i: (i, 0))
    acc = pl.BlockSpec((SUBLANES, d), lambda i: (0, 0))

    def body(y_ref, t_ref, h_ref, g_ref, dh_ref, dhb_ref, dg_ref, db_ref, l_ref):
        @pl.when(pl.program_id(0) == 0)
        def _():
            for r_ in (dg_ref, db_ref, l_ref):
                r_[...] = jnp.zeros_like(r_)

        err = y_ref[...] - t_ref[...]
        dy_v = err * (1.0 / d)
        sq = _fold8(err * err)
        part = sq[:, :LANES]
        for c in range(1, d // LANES):
            part = part + sq[:, c * LANES:(c + 1) * LANES]
        l_ref[...] += part
        xhat, r = _ln_hat(h_ref[...])
        dh = _ln_bwd(dy_v * g_ref[...], xhat, r)
        dh_ref[...] = dh
        dhb_ref[...] = dh.astype(dhb_ref.dtype)
        dg_ref[...] += _fold8(dy_v * xhat)
        db_ref[...] += _fold8(dy_v)

    res = pl.pallas_call(
        body, name="loss_ln_bwd", grid=(t // tb,),
        in_specs=[blk, blk, blk, pl.BlockSpec((1, d), lambda i: (0, 0))],
        out_specs=[blk, blk, acc, acc, pl.BlockSpec((SUBLANES, LANES), lambda i: (0, 0))],
        out_shape=[jax.ShapeDtypeStruct((t, d), F32), jax.ShapeDtypeStruct((t, d), ACT),
                   jax.ShapeDtypeStruct((SUBLANES, d), F32), jax.ShapeDtypeStruct((SUBLANES, d), F32),
                   jax.ShapeDtypeStruct((SUBLANES, LANES), F32)],
        compiler_params=_cparams(("arbitrary",)),
    )(y, target, hres, g)
    return res[:4], res[4]


def _adamw(w, g, m, v):
    shape = w.shape
    cols = shape[-1]
    w2, g2, m2, v2 = (a.reshape(-1, cols) for a in (w, g, m, v))
    rows = w2.shape[0]
    tr = _tile(rows, 256, SUBLANES)
    blk = pl.BlockSpec((tr, cols), lambda i: (i, 0))

    def body(w_ref, g_ref, m_ref, v_ref, d_ref, nm_ref, nv_ref):
        g_v = g_ref[...]
        nm = ADAM_B1 * m_ref[...] + (1.0 - ADAM_B1) * g_v
        nv = ADAM_B2 * v_ref[...] + (1.0 - ADAM_B2) * (g_v * g_v)
        m_hat = nm / (1.0 - ADAM_B1 ** ADAM_STEP)
        v_hat = nv / (1.0 - ADAM_B2 ** ADAM_STEP)
        d_ref[...] = -ADAM_LR * (m_hat / (jnp.sqrt(v_hat) + ADAM_EPS) + ADAM_WD * w_ref[...])
        nm_ref[...] = nm
        nv_ref[...] = nv

    out = jax.ShapeDtypeStruct((rows, cols), F32)
    res = pl.pallas_call(
        body, name="adamw", grid=(rows // tr,),
        in_specs=[blk] * 4, out_specs=[blk] * 3, out_shape=[out] * 3,
        compiler_params=_cparams(("parallel",)),
    )(w2, g2, m2, v2)
    return tuple(r.reshape(shape) for r in res)


def _place():
    x, y, c = lax.axis_index("x"), lax.axis_index("y"), lax.axis_index("c")
    return x, y, c, [(1 - x, y), (x, 1 - y), (1 - x, 1 - y)]


def _remote(src, dst, send_sems, recv_sems, k, to):
    return pltpu.make_async_remote_copy(src_ref=src, dst_ref=dst, send_sem=send_sems.at[k],
                                        recv_sem=recv_sems.at[k], device_id=to, device_id_type=MESH)


class _Carried:
    def __init__(self, inputs, out_shapes, n_sems, copies):
        self.inputs, self.out_shapes, self.n_sems, self.copies = list(inputs), list(out_shapes), n_sems, copies

    def scratch(self):
        return [pltpu.SemaphoreType.DMA((self.n_sems,)), pltpu.SemaphoreType.DMA((self.n_sems,))]


def _join_plans(first, second):
    ni, no, ns = len(first.inputs), len(first.out_shapes), first.n_sems

    def copies(in_refs, out_refs, send_sems, recv_sems):
        one = _phases(first.copies(in_refs[:ni], out_refs[:no], send_sems, recv_sems))
        two = _phases(second.copies(in_refs[ni:], out_refs[no:], send_sems.at[pl.ds(ns, second.n_sems)],
                                    recv_sems.at[pl.ds(ns, second.n_sems)]))

        def both(k):
            def run():
                one[k]()
                two[k]()
            return run

        return both(0), both(1), both(2)

    return _Carried(first.inputs + second.inputs, first.out_shapes + second.out_shapes, ns + second.n_sems, copies)


def _run_comm(name, plan):
    n_in, n_out = len(plan.inputs), len(plan.out_shapes)

    def body(*refs):
        for phase in _phases(plan.copies(refs[:n_in], refs[n_in:n_in + n_out], refs[-2], refs[-1])):
            phase()

    return pl.pallas_call(
        body, name=name, in_specs=[ANY] * n_in, out_specs=[ANY] * n_out, out_shape=plan.out_shapes,
        scratch_shapes=plan.scratch(),
    )(*plan.inputs)


def _half_rows(rows, core):
    if rows % (4 * SUBLANES):
        return None
    return pl.ds(pl.multiple_of(core * (rows // 2), 2 * SUBLANES), rows // 2)


def _all_gather_plan(shards):
    n = len(shards)

    def copies(x_refs, out_refs, send_sems, recv_sems):
        x, y, c, chips = _place()
        sibling = (x, y, 1 - c)
        mine = 2 * x + y
        split = [_half_rows(x_refs[t].shape[0], c) is not None for t in range(n)]

        def src(t):
            return x_refs[t].at[_half_rows(x_refs[t].shape[0], c)] if split[t] else x_refs[t]

        def slot(t, chip_idx, core):
            rows = _half_rows(x_refs[t].shape[0], core)
            return out_refs[t].at[chip_idx, rows] if split[t] else out_refs[t].at[chip_idx]

        def first():
            return [_remote(src(t), slot(t, mine, c), send_sems, recv_sems, 6 * t + j, (cx, cy, c))
                    for j, (cx, cy) in enumerate(chips) for t in range(n)]

        def start():
            for cp in first():
                cp.start()

        def passed():
            return [_remote(slot(t, 2 * cx + cy, c), slot(t, 2 * cx + cy, c), send_sems, recv_sems, 6 * t + 3 + j,
                            sibling) for j, (cx, cy) in enumerate(chips) for t in range(n) if split[t]]

        def middle():
            for j, (cx, cy) in enumerate(chips):
                for t in range(n):
                    theirs = slot(t, 2 * cx + cy, c)
                    _remote(theirs, theirs, send_sems, recv_sems, 6 * t + j, (cx, cy, c)).wait_recv()
            for cp in passed():
                cp.start()

        def finish():
            for j, (cx, cy) in enumerate(chips):
                for t in range(n):
                    if split[t]:
                        other = slot(t, 2 * cx + cy, 1 - c)
                        _remote(other, other, send_sems, recv_sems, 6 * t + 3 + j, sibling).wait_recv()
            for cp in first() + passed():
                cp.wait_send()

        return start, middle, finish

    return _Carried(shards, [jax.ShapeDtypeStruct((N_CHIPS,) + s.shape, s.dtype) for s in shards], 6 * n, copies)


def _sibling_exchange_plan(grads, small=None):
    n = len(grads)
    extra = [] if small is None else [small]

    def copies(in_refs, out_refs, send_sems, recv_sems):
        x, y, c, _ = _place()
        sibling = (x, y, 1 - c)

        def all_copies():
            cps = [_remote(in_refs[t].at[:, _half_rows(in_refs[t].shape[1], 1 - c), :], out_refs[t],
                           send_sems, recv_sems, t, sibling) for t in range(n)]
            if extra:
                cps.append(_remote(in_refs[n], out_refs[n], send_sems, recv_sems, n, sibling))
            return cps

        def start():
            for cp in all_copies():
                cp.start()

        def finish():
            for cp in all_copies():
                cp.wait()

        return start, finish

    shapes = [jax.ShapeDtypeStruct((g.shape[0], g.shape[1] // 2, g.shape[2]), g.dtype) for g in grads]
    shapes += [jax.ShapeDtypeStruct(s.shape, s.dtype) for s in extra]
    return _Carried(list(grads) + extra, shapes, n + 1, copies)


def _chip_exchange_plan(travel, small=None):
    n = len(travel)
    extra = [] if small is None else [small]

    def copies(in_refs, out_refs, send_sems, recv_sems):
        x, y, c, chips = _place()
        mine = 2 * x + y

        def all_copies():
            cps = []
            for j, (cx, cy) in enumerate(chips):
                to = (cx, cy, c)
                for t in range(n):
                    cps.append(_remote(in_refs[t].at[2 * cx + cy], out_refs[t].at[mine], send_sems, recv_sems,
                                       3 * t + j, to))
                if extra:
                    cps.append(_remote(in_refs[n], out_refs[n].at[mine], send_sems, recv_sems, 3 * n + j, to))
            return cps

        def start():
            for cp in all_copies():
                cp.start()

        def finish():
            for cp in all_copies():
                cp.wait()

        return start, finish

    shapes = [jax.ShapeDtypeStruct(g.shape, g.dtype) for g in travel]
    shapes += [jax.ShapeDtypeStruct((N_CHIPS,) + s.shape, s.dtype) for s in extra]
    return _Carried(list(travel) + extra, shapes, 3 * n + 3, copies)


def _sibling_merge_plan(reduced):
    n = len(reduced)

    def copies(in_refs, out_refs, send_sems, recv_sems):
        x, y, c, _ = _place()

        def all_copies():
            return [_remote(in_refs[t], out_refs[t], send_sems, recv_sems, t, (x, y, 1 - c)) for t in range(n)]

        def start():
            for cp in all_copies():
                cp.start()

        def finish():
            for cp in all_copies():
                cp.wait()

        return start, finish

    return _Carried(reduced, [jax.ShapeDtypeStruct(r.shape, r.dtype) for r in reduced], n, copies)


def _pair_sum(place, grad, land):
    n, r, c = grad.shape
    half = r // 2
    tr = _tile(half, 256, SUBLANES)
    nb = half // tr

    def body(place_ref, a_ref, b_ref, travel_ref, own_ref):
        total = a_ref[0] + b_ref[0]
        travel_ref[0] = total.astype(travel_ref.dtype)

        @pl.when(pl.program_id(1) == place_ref[1])
        def _():
            own_ref[...] = total

    return pl.pallas_call(
        body, name="grad_pair_sum",
        grid_spec=pltpu.PrefetchScalarGridSpec(
            num_scalar_prefetch=1, grid=(nb, n),
            in_specs=[pl.BlockSpec((1, tr, c), lambda i, s, p: (s, p[0] * nb + i, 0)),
                      pl.BlockSpec((1, tr, c), lambda i, s, p: (s, i, 0))],
            out_specs=[pl.BlockSpec((1, tr, c), lambda i, s, p: (s, i, 0)),
                       pl.BlockSpec((tr, c), lambda i, s, p: (i, 0))]),
        out_shape=[jax.ShapeDtypeStruct((n, half, c), BF16), jax.ShapeDtypeStruct((half, c), F32)],
        compiler_params=_cparams(("parallel", "arbitrary")),
    )(place, grad, land)


def _chip_sum(place, own, land, name):
    n, r, c = land.shape
    tr = _tile(r, 256, SUBLANES)

    def body(place_ref, own_ref, land_ref, o_ref):
        mine = place_ref[1]
        acc = jnp.zeros(o_ref.shape, F32)
        for s in range(n):
            acc = acc + jnp.where(mine == s, own_ref[...], land_ref[s].astype(F32))
        o_ref[...] = acc

    return pl.pallas_call(
        body, name=name,
        grid_spec=pltpu.PrefetchScalarGridSpec(
            num_scalar_prefetch=1, grid=(r // tr,),
            in_specs=[pl.BlockSpec((tr, c), lambda i, p: (i, 0)),
                      pl.BlockSpec((n, tr, c), lambda i, p: (0, i, 0))],
            out_specs=pl.BlockSpec((tr, c), lambda i, p: (i, 0))),
        out_shape=jax.ShapeDtypeStruct((r, c), F32),
        compiler_params=_cparams(("parallel",)),
    )(place, own, land)


def _add2(a, b):
    rows = a.shape[0]
    tr = _tile(rows, 256, SUBLANES)
    blk = pl.BlockSpec((tr, a.shape[1]), lambda i: (i, 0))

    def body(a_ref, b_ref, o_ref):
        o_ref[...] = a_ref[...] + b_ref[...]

    return pl.pallas_call(
        body, name="grad_small_pair_sum", grid=(rows // tr,), in_specs=[blk, blk], out_specs=blk,
        out_shape=jax.ShapeDtypeStruct(a.shape, F32), compiler_params=_cparams(("parallel",)),
    )(a, b)


def _merge_halves(place, mine, other):
    first_core = place[0] == 0
    return jnp.concatenate([jnp.where(first_core, mine, other), jnp.where(first_core, other, mine)], axis=0)


_BIG = (("w_in", 2), ("w_pa", 1), ("w_pb", 1), ("w_o", 1), ("w_ffn_gate", 2), ("w_ffn_up", 2),
        ("w_ffn_down", 1))
_SMALL = ("conv_w", "a_log", "dt_bias", "o_norm_w", "sgu_ln_g", "sgu_ln_b", "w_s", "b_s",
          "ln1_g", "ln1_b", "ln2_g", "ln2_b")


def _pack_small(arrays):
    pieces = []
    for a in arrays:
        if a.shape[-1] % LANES == 0:
            a2 = a.reshape(-1, LANES)
        else:
            a2 = jnp.pad(a.reshape(-1, a.shape[-1]), ((0, 0), (0, LANES - a.shape[-1])))
        pieces.append(jnp.pad(a2, ((0, -a2.shape[0] % SUBLANES), (0, 0))))
    return jnp.concatenate(pieces, axis=0)


def _unpack_small(buf, like):
    out, off = [], 0
    for a in like:
        if a.shape[-1] % LANES == 0:
            rows = a.size // LANES
            out.append(buf[off:off + rows].reshape(a.shape))
        else:
            rows = a.size // a.shape[-1]
            out.append(buf[off:off + rows, :a.shape[-1]].reshape(a.shape))
        off += -(-rows // SUBLANES) * SUBLANES
    return out


def _unshard(gathered, local, chip, axis):
    parts = [jnp.where(chip == s, local, gathered[s]) for s in range(N_CHIPS)]
    return jnp.concatenate(parts, axis=axis - 1)


def _to_shards(full, axis):
    l, r, c = full.shape
    if axis == 1:
        return full.reshape(l, N_CHIPS, r // N_CHIPS, c)
    return jnp.transpose(full.reshape(l, r, N_CHIPS, c // N_CHIPS), (0, 2, 1, 3))


def _row(v, width=None):
    v = v.reshape(1, -1).astype(F32)
    if width is not None and v.shape[1] < width:
        v = jnp.pad(v, ((0, 0), (0, width - v.shape[1])))
    return v


def _layer_consts(p, l, d):
    heads = d // DN_DK
    return dict(
        alog=_row(p["a_log"][l], LANES), dtb=_row(p["dt_bias"][l], LANES),
        onw=_row(jnp.tile(p["o_norm_w"][l], heads)),
        lng=_row(p["sgu_ln_g"][l]), lnb=_row(p["sgu_ln_b"][l]),
        ws=p["w_s"][l].astype(F32),
        bst=jnp.pad(p["b_s"][l].T, ((0, 0), (0, LANES - p["b_s"].shape[1]))),
        g1=_row(p["ln1_g"][l]), b1=_row(p["ln1_b"][l]), g2=_row(p["ln2_g"][l]), b2=_row(p["ln2_b"][l]))


class _NoComm:
    def with_proj_main(self):
        return None

    def after_proj_main(self, got):
        pass

    def weights(self, full):
        return full

    def with_dn_fwd(self):
        return None

    def after_dn_fwd(self, got):
        pass

    def with_ffn_in_dw(self):
        return None

    def after_ffn_in_dw(self, got):
        pass

    def after_branch_grads(self, g):
        pass

    def with_dn_bwd(self):
        return None

    def after_dn_bwd(self, got):
        pass

    def with_proj_main_dw(self):
        return None

    def after_proj_main_dw(self, got):
        pass

    def with_ffn_in(self):
        return None

    def after_ffn_in(self, got):
        pass

    def after_all_grads(self, g):
        pass

    def with_gate_sgu_bwd(self):
        return None

    def after_gate_sgu_bwd(self, got):
        pass

    def with_proj_gates_dx(self):
        return None

    def after_proj_gates_dx(self, got):
        pass

    def with_proj_main_dx(self):
        return None

    def after_proj_main_dx(self, got):
        pass


def _carry(carried, after, call, *args, **kw):
    if carried is None:
        return call(*args, **kw)
    out, got = call(*args, carried=carried, **kw)
    after(got)
    return out


def _in_proj_weights(w_in, d):
    heads, q4 = d // DN_DK, 4 * d
    wba = jnp.zeros((d, 2 * LANES), w_in.dtype)
    wba = wba.at[:, :heads].set(w_in[:, q4:q4 + heads])
    wba = wba.at[:, LANES:LANES + heads].set(w_in[:, q4 + heads:q4 + 2 * heads])
    return jnp.concatenate([w_in[:, :q4], w_in[:, q4 + 2 * heads:]], axis=1), wba


def _layer_fwd(x, xb, full, cl, d, tb, comm):
    wm, wba = _in_proj_weights(full["w_in"], d)
    projm = _carry(comm.with_proj_main(), comm.after_proj_main, _matmul, xb, wm, NN, "proj_main", tn=MM_WIDE)
    full = comm.weights(full)
    wl = dict(wm=wm, wba=wba, conv=full["conv_w"], wpa=full["w_pa"], wpb=full["w_pb"], wo=full["w_o"],
              wgu=jnp.concatenate([full["w_ffn_gate"], full["w_ffn_up"]], axis=1), wd=full["w_ffn_down"])
    ba = _matmul(xb, wba, NN, "proj_gates")
    qkv = _conv_fwd(projm, wl["conv"], d, _tile(x.shape[0], 2 * tb, SUBLANES))
    (o, states, ycors), got = _dn_fwd(qkv, ba, cl["alog"], cl["dtb"], d, comm.with_dn_fwd())
    comm.after_dn_fwd(got)
    ya, yb = _gate_sgu_fwd(o, projm, cl["onw"], cl["lng"], cl["lnb"], cl["ws"], cl["bst"], d)
    pa, pb, m, h1, x1, x1b = _mix_fwd(ya, yb, projm, x, wl["wpa"], wl["wpb"], wl["wo"], cl["g1"], cl["b1"], d, tb)
    gu = _carry(comm.with_ffn_in(), comm.after_ffn_in, _matmul, x1b, wl["wgu"], NN, "ffn_in", out_dtype=ACT,
                tn=2 * MM_TILE)
    act, h2, x2, x2b = _ffn_tail_fwd(gu, wl["wd"], x1, cl["g2"], cl["b2"], tb)
    saved = dict(xb=xb, projm=projm, ba=ba, qkv=qkv, o=o, states=states, ycors=ycors, ya=ya, yb=yb,
                 pa=pa, pb=pb, m=m, h1=h1, x1b=x1b, gu=gu, act=act, h2=h2)
    return x2, x2b, saved, wl


def _layer_bwd(sv, wl, cl, d, tb, comm, ln2_bwd, next_ln=None):
    g = {}
    dh2, dh2b, dg2, db2 = ln2_bwd
    g["ln2_g"], g["ln2_b"] = dg2.sum(0), db2.sum(0)
    g["wd"] = _matmul(sv["act"], dh2b, TN, "ffn_out_dw", tk=MM_WIDE)
    dgu = _ffn_tail_bwd(dh2b, wl["wd"], sv["gu"], tb)
    g["wgu"] = _carry(comm.with_ffn_in_dw(), comm.after_ffn_in_dw, _matmul, sv["x1b"], dgu, TN, "ffn_in_dw",
                      tk=MM_WIDE)
    dh1, dh1b, dg1, db1 = _ffn_head_bwd(dgu, wl["wgu"], dh2, sv["h1"], cl["g1"], tb)
    g["ln1_g"], g["ln1_b"] = dg1.sum(0), db1.sum(0)
    g["wo"] = _matmul(sv["m"], dh1b, TN, "wo_dw", tk=MM_WIDE)
    dpa, dpb, dya, dyb, dprojm = _mix_bwd(dh1b, sv["pa"], sv["pb"], sv["projm"], wl["wpa"], wl["wpb"], wl["wo"], d, tb)
    g["wpa"] = _matmul(sv["ya"], dpa, TN, "wpa_dw", tk=MM_WIDE)
    g["wpb"] = _matmul(sv["yb"], dpb, TN, "wpb_dw", tk=MM_WIDE)
    comm.after_branch_grads(g)
    (do, dprojm, donw, dlng, dlnb, dws, dbst), got = _gate_sgu_bwd(
        dya, dyb, sv["o"], sv["projm"], cl["onw"], cl["lng"], cl["lnb"], cl["ws"], cl["bst"], dprojm, d,
        comm.with_gate_sgu_bwd())
    comm.after_gate_sgu_bwd(got)
    heads, groups = d // DN_DK, d // SGU_GROUP_DIM
    g["o_norm_w"], g["sgu_ln_g"], g["sgu_ln_b"] = donw.sum(0), dlng.sum(0), dlnb.sum(0)
    g["w_s"], g["b_s"] = dws, dbst[:, :groups].T
    (dqkv, dba, dal, ddt), got = _dn_bwd(sv["qkv"], sv["ba"], cl["alog"], cl["dtb"], do, sv["states"],
                                         sv["ycors"], d, comm.with_dn_bwd())
    comm.after_dn_bwd(got)
    g["a_log"], g["dt_bias"] = dal.sum(0)[:heads], ddt.sum(0)[:heads]
    tbc = _tile(sv["xb"].shape[0], 2 * tb, SUBLANES)
    dy, dcw = _conv_bwd_dy(sv["projm"], wl["conv"], dqkv, d, tbc)
    g["conv_w"] = dcw.sum(1)
    dprojm = _conv_bwd_dx(dy, wl["conv"], dprojm, d, tbc)
    g["wba"] = _matmul(sv["xb"], dba, TN, "proj_gates_dw")
    g["wm"] = _carry(comm.with_proj_main_dw(), comm.after_proj_main_dw, _matmul, sv["xb"], dprojm, TN,
                     "proj_main_dw", tn=MM_WIDE)
    comm.after_all_grads(g)
    dx = _carry(comm.with_proj_gates_dx(), comm.after_proj_gates_dx, _matmul, dba, wl["wba"], NT, "proj_gates_dx",
                add=dh1, coef=ALPHA)
    if next_ln is not None:
        return _matmul(dprojm, wl["wm"], NT, "proj_main_dx", add=dx, tm=MM_TILE // 3, tk=MM_WIDE, ln=next_ln), g
    dx = _carry(comm.with_proj_main_dx(), comm.after_proj_main_dx, _matmul, dprojm, wl["wm"], NT, "proj_main_dx",
                add=dx, tk=MM_WIDE)
    return dx, g


_BRANCH = ("w_pa", "w_pb", "w_o", "w_ffn_gate", "w_ffn_up", "w_ffn_down")


def _grad_shards(g, d, keys):
    heads, q4 = d // DN_DK, 4 * d
    rows = lambda a: a.reshape(N_CHIPS, -1, a.shape[1])
    out = {}
    if "w_in" in keys:
        gm, gba, wsh = g["wm"], g["wba"], 2 * d + heads // 2
        out["w_in"] = jnp.stack([gm[:, :wsh],
                                 jnp.concatenate([gm[:, wsh:q4], gba[:, :heads]], axis=1),
                                 jnp.concatenate([gba[:, LANES:LANES + heads], gm[:, q4:q4 + wsh - heads]], axis=1),
                                 gm[:, q4 + wsh - heads:]])
    if "w_pa" in keys:
        ggu = g["wgu"]
        f = ggu.shape[1] // 2
        fs = f // N_CHIPS
        out.update({
            "w_pa": rows(g["wpa"]), "w_pb": rows(g["wpb"]), "w_o": rows(g["wo"]), "w_ffn_down": rows(g["wd"]),
            "w_ffn_gate": jnp.stack([ggu[:, s * fs:(s + 1) * fs] for s in range(N_CHIPS)]),
            "w_ffn_up": jnp.stack([ggu[:, f + s * fs:f + (s + 1) * fs] for s in range(N_CHIPS)])})
    return out


def _local_step(x, target, full0, full1_of, small_w, comm0=None):
    t, d = x.shape
    tb = _tile(t, 256, SUBLANES)
    comm0 = comm0 or _NoComm()
    consts = [_layer_consts(small_w, l, d) for l in range(DEPTH)]
    x1, x1b, sv0, w0 = _layer_fwd(x, x.astype(ACT), full0, consts[0], d, tb, comm0)
    x2, _, sv1, w1 = _layer_fwd(x1, x1b, full1_of(), consts[1], d, tb, _NoComm())
    ln2_bwd, loss_parts = _loss_ln_bwd(x2, target, sv1["h2"], consts[1]["g2"], tb)
    ln2_bwd, g1 = _layer_bwd(sv1, w1, consts[1], d, tb, _NoComm(), ln2_bwd, next_ln=(sv0["h2"], consts[0]["g2"]))
    comm0.layer1_grads = g1
    grad_x, g0 = _layer_bwd(sv0, w0, consts[0], d, tb, comm0, ln2_bwd)
    return loss_parts, grad_x, [g0, g1]


def kernel(x, w_in, conv_w, a_log, dt_bias, o_norm_w, sgu_ln_g, sgu_ln_b, w_s, b_s, w_pa, w_pb, w_o, ln1_g, ln1_b, w_ffn_gate, w_ffn_up, w_ffn_down, ln2_g, ln2_b, loss_target, m_w_in, m_conv_w, m_a_log, m_dt_bias, m_o_norm_w, m_sgu_ln_g, m_sgu_ln_b, m_w_s, m_b_s, m_w_pa, m_w_pb, m_w_o, m_ln1_g, m_ln1_b, m_w_ffn_gate, m_w_ffn_up, m_w_ffn_down, m_ln2_g, m_ln2_b, v_w_in, v_conv_w, v_a_log, v_dt_bias, v_o_norm_w, v_sgu_ln_g, v_sgu_ln_b, v_w_s, v_b_s, v_w_pa, v_w_pb, v_w_o, v_ln1_g, v_ln1_b, v_w_ffn_gate, v_w_ffn_up, v_w_ffn_down, v_ln2_g, v_ln2_b):
    names = ("w_in", "conv_w", "a_log", "dt_bias", "o_norm_w", "sgu_ln_g", "sgu_ln_b", "w_s", "b_s", "w_pa",
             "w_pb", "w_o", "ln1_g", "ln1_b", "w_ffn_gate", "w_ffn_up", "w_ffn_down", "ln2_g", "ln2_b")
    w = dict(zip(names, (w_in, conv_w, a_log, dt_bias, o_norm_w, sgu_ln_g, sgu_ln_b, w_s, b_s, w_pa, w_pb, w_o,
                         ln1_g, ln1_b, w_ffn_gate, w_ffn_up, w_ffn_down, ln2_g, ln2_b)))
    mom = dict(zip(names, (m_w_in, m_conv_w, m_a_log, m_dt_bias, m_o_norm_w, m_sgu_ln_g, m_sgu_ln_b, m_w_s, m_b_s,
                           m_w_pa, m_w_pb, m_w_o, m_ln1_g, m_ln1_b, m_w_ffn_gate, m_w_ffn_up, m_w_ffn_down,
                           m_ln2_g, m_ln2_b)))
    var = dict(zip(names, (v_w_in, v_conv_w, v_a_log, v_dt_bias, v_o_norm_w, v_sgu_ln_g, v_sgu_ln_b, v_w_s, v_b_s,
                           v_w_pa, v_w_pb, v_w_o, v_ln1_g, v_ln1_b, v_w_ffn_gate, v_w_ffn_up, v_w_ffn_down,
                           v_ln2_g, v_ln2_b)))
    chip = 2 * lax.axis_index("x") + lax.axis_index("y")
    place = jnp.stack([lax.axis_index("c"), chip]).astype(jnp.int32)

    big = [k for k, _ in _BIG]
    axis_of = dict(_BIG)
    local = {k: w[k].astype(BF16) for k in big}
    local["conv_w"] = conv_w

    def gather_plan(l, keys):
        return _all_gather_plan([local[k][l] for k in keys])

    def full_of(l, keys, gathered):
        return {k: _unshard(gt, local[k][l], chip, axis_of.get(k, 2)) for k, gt in zip(keys, gathered)}

    def pair_sums(grads_l, keys, lands):
        return [_pair_sum(place, grads_l[k], land) for k, land in zip(keys, lands)]

    def chip_sums(pairs, lands):
        return [_chip_sum(place, p[1], land, "grad_chip_sum") for p, land in zip(pairs, lands)]

    class Layer0Comm(_NoComm):
        def with_proj_main(self):
            return gather_plan(0, _BRANCH)

        def after_proj_main(self, got):
            self.rest = full_of(0, _BRANCH, got)

        def weights(self, full):
            return {**full, **self.rest}

        def with_dn_fwd(self):
            return gather_plan(1, mixer)

        def after_dn_fwd(self, got):
            self.full1 = full_of(1, mixer, got)

        def with_ffn_in(self):
            return gather_plan(1, ffn)

        def after_ffn_in(self, got):
            self.full1.update(full_of(1, ffn, got))

        def with_ffn_in_dw(self):
            self.g1 = _grad_shards(self.layer1_grads, x.shape[-1], big)
            return _sibling_exchange_plan([self.g1[k] for k in big])

        def after_ffn_in_dw(self, got):
            self.pairs1 = pair_sums(self.g1, big, got)

        def with_dn_bwd(self):
            return _chip_exchange_plan([p[0] for p in self.pairs1])

        def after_dn_bwd(self, got):
            self.red1 = chip_sums(self.pairs1, got)

        def after_branch_grads(self, g0):
            self.shards0 = _grad_shards(g0, x.shape[-1], _BRANCH)

        def with_gate_sgu_bwd(self):
            return _sibling_exchange_plan([self.shards0[k] for k in _BRANCH])

        def after_gate_sgu_bwd(self, got):
            self.pairs0 = pair_sums(self.shards0, _BRANCH, got)

        def with_proj_main_dw(self):
            return _chip_exchange_plan([p[0] for p in self.pairs0])

        def after_proj_main_dw(self, got):
            self.red0 = chip_sums(self.pairs0, got)

        def after_all_grads(self, g0):
            self.g_in = _grad_shards(g0, x.shape[-1], ["w_in"])["w_in"]
            self.small_g = {k: jnp.stack([g0[k], self.layer1_grads[k]]) for k in _SMALL}
            self.small = _pack_small([self.small_g[k] for k in _SMALL])

        def with_proj_gates_dx(self):
            return _sibling_exchange_plan([self.g_in], self.small)

        def after_proj_gates_dx(self, got):
            self.pair_in = _pair_sum(place, self.g_in, got[0])
            self.small_chip = _add2(self.small, got[1])

        def with_proj_main_dx(self):
            return _join_plans(_chip_exchange_plan([self.pair_in[0]], self.small_chip),
                               _sibling_merge_plan(self.red0 + self.red1))

        def after_proj_main_dx(self, got):
            self.red_in = _chip_sum(place, self.pair_in[1], got[0], "grad_chip_sum")
            self.small_total = _chip_sum(place, self.small_chip, got[1], "grad_small_chip_sum")
            self.others = got[2:]

    comm = Layer0Comm()
    first, mixer, ffn = ["w_in", "conv_w"], ["w_in", "conv_w", "w_pa", "w_pb", "w_o"], list(_BRANCH[3:])
    full0 = full_of(0, first, _run_comm("all_gather_weights", gather_plan(0, first)))
    small_w = {k: w[k] for k in _SMALL if k != "conv_w"}
    loss_parts, grad_x, g = _local_step(x[0], loss_target[0], full0, lambda: comm.full1, small_w, comm)

    reduced = [comm.red_in] + comm.red0 + comm.red1
    others = list(_run_comm("grad_sibling_merge", _sibling_merge_plan([comm.red_in]))) + list(comm.others)
    halves = [_merge_halves(place, mine, other) for mine, other in zip(reduced, others)]
    grads = {k: jnp.stack([halves[i], halves[len(big) + i]]) for i, k in enumerate(big)}
    grads.update(zip(_SMALL, _unpack_small(comm.small_total, [comm.small_g[k] for k in _SMALL])))
    grads["conv_w"] = lax.dynamic_index_in_dim(_to_shards(grads["conv_w"], 2), chip, 1, keepdims=False)

    delta, new_m, new_v = {}, {}, {}
    for k in [k for k, _ in _BIG] + ["conv_w"]:
        delta[k], new_m[k], new_v[k] = _adamw(w[k], grads[k], mom[k], var[k])
    rep = [k for k in _SMALL if k != "conv_w"]
    pack = lambda dct: _pack_small([dct[k] for k in rep])
    packed = _adamw(pack(w), pack(grads), pack(mom), pack(var))
    for dst, src in zip((delta, new_m, new_v), packed):
        dst.update(zip(rep, _unpack_small(src, [w[k] for k in rep])))

    loss = 0.5 * lax.psum(jnp.sum(loss_parts), ("x", "y", "c")) / x.shape[-1]
    return (loss, grad_x[None], *[grads[k] for k in names], *[delta[k] for k in names],
            *[new_m[k] for k in names], *[new_v[k] for k in names])
```

```python
import math

import jax
import jax.numpy as jnp
from jax import lax
from jax.experimental import pallas as pl
from jax.experimental.pallas import tpu as pltpu

F32 = jnp.float32
BF16 = jnp.bfloat16
MXU_DTYPE = jnp.bfloat16
ACT = jnp.bfloat16
HIGHEST = lax.Precision.HIGHEST

DEPTH = 2
CHUNK = 64
DN_GROUP = 2
DN_GROUP_FWD = 4
SGU_BLOCK = 128
SGU_WINDOWS = 4
CONV_K = 4
DN_DK = 128
SGU_GROUP_DIM = 128
LN_EPS = 1e-5
RMS_EPS = 1e-6
ALPHA = (2 * DEPTH) ** 0.25
ADAM_LR, ADAM_B1, ADAM_B2, ADAM_EPS, ADAM_WD, ADAM_STEP = 0.001, 0.9, 0.999, 1e-08, 0.01, 10

LANES = 128
SUBLANES = 8
VMEM_LIMIT = 52 * 2 ** 20
N_CHIPS = 4

NN = ((1,), (0,))
NT = ((1,), (1,))
TN = ((0,), (0,))
MESH = pl.DeviceIdType.MESH
ANY = pl.BlockSpec(memory_space=pl.ANY)


def _dot(a, b, dims=NN, prec=None):
    if prec is None:
        a = a.astype(MXU_DTYPE)
        b = b.astype(MXU_DTYPE)
    return lax.dot_general(a, b, (dims, ((), ())), preferred_element_type=F32, precision=prec)


def _cparams(sem=None):
    return pltpu.CompilerParams(dimension_semantics=sem, vmem_limit_bytes=VMEM_LIMIT)


def _tile(dim, pref, unit=LANES):
    t = (min(pref, dim) // unit) * unit
    while t >= unit:
        if dim % t == 0:
            return t
        t -= unit
    return dim


def _fold8(x):
    r, n = x.shape
    return x.reshape(r // SUBLANES, SUBLANES, n).sum(axis=0)


def _sigmoid(x):
    return 1.0 / (1.0 + jnp.exp(-x))


def _gelu(x):
    return 0.5 * x * (1.0 + lax.erf(x * (2.0 ** -0.5)))


def _gelu_grad(x):
    return 0.5 * (1.0 + lax.erf(x * (2.0 ** -0.5))) + x * jnp.exp(-0.5 * x * x) * (2.0 * math.pi) ** -0.5


def _ln_hat(h):
    mu = jnp.mean(h, axis=-1, keepdims=True)
    xc = h - mu
    var = jnp.mean(xc * xc, axis=-1, keepdims=True)
    r = lax.rsqrt(var + LN_EPS)
    return xc * r, r


def _ln_bwd(dxhat, xhat, r):
    return r * (dxhat - jnp.mean(dxhat, axis=-1, keepdims=True)
                - xhat * jnp.mean(dxhat * xhat, axis=-1, keepdims=True))


MM_TILE = 1536
MM_WIDE = 2048


def _matmul(a, b, dims, name, out_dtype=F32, add=None, coef=1.0, tm=MM_TILE, tn=MM_TILE, tk=MM_TILE, carried=None,
            ln=None):
    if dims == NN:
        (m, k), n = a.shape, b.shape[1]
    elif dims == NT:
        (m, k), n = a.shape, b.shape[0]
    else:
        (k, m), n = a.shape, b.shape[1]
    tm, tn, tk = _tile(m, tm), _tile(n, tn), _tile(k, tk)
    nk = k // tk
    a_spec = pl.BlockSpec((tk, tm), lambda j, i, q: (q, i)) if dims == TN else pl.BlockSpec((tm, tk), lambda j, i, q: (i, q))
    b_spec = pl.BlockSpec((tn, tk), lambda j, i, q: (j, q)) if dims == NT else pl.BlockSpec((tk, tn), lambda j, i, q: (q, j))
    o_spec = pl.BlockSpec((tm, tn), lambda j, i, q: (i, j))
    has_add = add is not None
    if ln is not None:
        assert n == tn and has_add and carried is None
        return _matmul_ln_bwd(a, b, dims, name, add, coef, ln, a_spec, b_spec, o_spec, (m, n, tm, tn, nk))

    def body(*refs):
        a_ref, b_ref = refs[0], refs[1]
        add_ref = refs[2] if has_add else None
        o_ref, acc_ref = refs[2 + has_add], refs[3 + has_add]
        q = pl.program_id(2)
        part = _dot(a_ref[...], b_ref[...], dims)

        def finish(r):
            if has_add:
                r = r + coef * add_ref[...]
            o_ref[...] = r.astype(out_dtype)

        if nk == 1:
            finish(part)
        else:
            @pl.when(q == 0)
            def _():
                acc_ref[...] = part

            @pl.when(q > 0)
            def _():
                acc_ref[...] += part

            @pl.when(q == nk - 1)
            def _():
                finish(acc_ref[...])

    ins = [a, b] + ([add] if has_add else [])
    in_specs = [a_spec, b_spec] + ([o_spec] if has_add else [])
    grid = (n // tn, m // tm, nk)
    acc = pltpu.VMEM((tm, tn) if nk > 1 else (SUBLANES, LANES), F32)
    out = jax.ShapeDtypeStruct((m, n), out_dtype)
    if carried is None:
        return pl.pallas_call(
            body, name=name, grid=grid, in_specs=in_specs, out_specs=o_spec, out_shape=out, scratch_shapes=[acc],
            compiler_params=_cparams(("parallel", "parallel", "arbitrary")),
        )(*ins)
    res = pl.pallas_call(
        _carrying(body, len(ins), 1, 1, carried, grid), name=name + "_carrying", grid=grid,
        in_specs=in_specs + [ANY] * len(carried.inputs), out_specs=[o_spec] + [ANY] * len(carried.out_shapes),
        out_shape=[out] + carried.out_shapes, scratch_shapes=[acc] + carried.scratch(),
        compiler_params=_cparams(("arbitrary", "arbitrary", "arbitrary")),
    )(*ins, *carried.inputs)
    return res[0], res[1:]


def _matmul_ln_bwd(a, b, dims, name, add, coef, ln, a_spec, b_spec, o_spec, sizes):
    m, n, tm, tn, nk = sizes
    hres, g = ln
    row = pl.BlockSpec((1, n), lambda j, i, q: (0, 0))
    sums = pl.BlockSpec((SUBLANES, n), lambda j, i, q: (0, 0))

    def body(a_ref, b_ref, add_ref, h_ref, g_ref, dh_ref, dhb_ref, dg_ref, db_ref, acc_ref):
        i, q = pl.program_id(1), pl.program_id(2)
        part = _dot(a_ref[...], b_ref[...], dims)

        @pl.when(jnp.logical_and(i == 0, q == 0))
        def _():
            dg_ref[...] = jnp.zeros_like(dg_ref)
            db_ref[...] = jnp.zeros_like(db_ref)

        @pl.when(q == 0)
        def _():
            acc_ref[...] = part

        @pl.when(q > 0)
        def _():
            acc_ref[...] += part

        @pl.when(q == nk - 1)
        def _():
            dy_v = acc_ref[...] + coef * add_ref[...]
            xhat, r = _ln_hat(h_ref[...])
            dh = _ln_bwd(dy_v * g_ref[...], xhat, r)
            dh_ref[...] = dh
            dhb_ref[...] = dh.astype(dhb_ref.dtype)
            dg_ref[...] += _fold8(dy_v * xhat)
            db_ref[...] += _fold8(dy_v)

    return pl.pallas_call(
        body, name=name + "_ln_bwd", grid=(1, m // tm, nk),
        in_specs=[a_spec, b_spec, o_spec, o_spec, row], out_specs=[o_spec, o_spec, sums, sums],
        out_shape=[jax.ShapeDtypeStruct((m, n), F32), jax.ShapeDtypeStruct((m, n), ACT),
                   jax.ShapeDtypeStruct((SUBLANES, n), F32), jax.ShapeDtypeStruct((SUBLANES, n), F32)],
        scratch_shapes=[pltpu.VMEM((tm, tn), F32)],
        compiler_params=_cparams(("arbitrary", "arbitrary", "arbitrary")),
    )(a, b, add, hres, g)


def _conv_taps(cur_ref, halo_ref, first):
    x = cur_ref[...]
    tb = x.shape[0]
    halo = jnp.where(first, 0.0, halo_ref[...])
    xc = jnp.concatenate([halo, x], axis=0)
    return [x] + [pltpu.roll(xc, s, 0)[SUBLANES:SUBLANES + tb] for s in range(1, CONV_K)]


def _conv_fwd(projm, conv_w, d, tb):
    t = projm.shape[0]
    heads = d // DN_DK
    hb = tb // SUBLANES

    def body(cur_ref, halo_ref, w_ref, o_ref):
        i, j = pl.program_id(0), pl.program_id(1)
        taps = _conv_taps(cur_ref, halo_ref, i == 0)
        y = taps[0] * w_ref[CONV_K - 1:CONV_K, :]
        for s in range(1, CONV_K):
            y = y + taps[s] * w_ref[CONV_K - 1 - s:CONV_K - s, :]
        act = y * _sigmoid(y)
        scale = jnp.where(j == 0, DN_DK ** -0.5, 1.0)
        for h in range(heads):
            seg = act[:, h * DN_DK:(h + 1) * DN_DK]
            r = lax.rsqrt(jnp.sum(seg * seg, axis=1, keepdims=True) + RMS_EPS) * scale
            o_ref[:, h * DN_DK:(h + 1) * DN_DK] = seg * jnp.where(j < 2, r, 1.0)

    blk = pl.BlockSpec((tb, d), lambda i, j: (i, j))
    return pl.pallas_call(
        body, name="conv_fwd", grid=(t // tb, 3),
        in_specs=[blk,
                  pl.BlockSpec((SUBLANES, d), lambda i, j: (jnp.maximum(i * hb - 1, 0), j)),
                  pl.BlockSpec((CONV_K, d), lambda i, j: (0, j))],
        out_specs=blk,
        out_shape=jax.ShapeDtypeStruct((t, 3 * d), F32),
        compiler_params=_cparams(("parallel", "parallel")),
    )(projm, projm, conv_w)


def _conv_bwd_dy(projm, conv_w, dqkv, d, tb):
    t = projm.shape[0]
    heads = d // DN_DK
    hb = tb // SUBLANES

    def body(cur_ref, halo_ref, w_ref, dout_ref, dy_ref, dw_ref):
        j, i = pl.program_id(0), pl.program_id(1)
        taps = _conv_taps(cur_ref, halo_ref, i == 0)
        y = taps[0] * w_ref[CONV_K - 1:CONV_K, :]
        for s in range(1, CONV_K):
            y = y + taps[s] * w_ref[CONV_K - 1 - s:CONV_K - s, :]
        sg = _sigmoid(y)
        act = y * sg
        dact = sg * (1.0 + y * (1.0 - sg))
        scale = jnp.where(j == 0, DN_DK ** -0.5, 1.0)
        for h in range(heads):
            cols = slice(h * DN_DK, (h + 1) * DN_DK)
            seg = act[:, cols]
            r = lax.rsqrt(jnp.sum(seg * seg, axis=1, keepdims=True) + RMS_EPS)
            nrm = seg * r
            dout = dout_ref[:, cols]
            ds = jnp.where(j < 2, (r * scale) * (dout - nrm * jnp.sum(dout * nrm, axis=1, keepdims=True)), dout)
            dy_ref[:, cols] = ds * dact[:, cols]
        dy = dy_ref[...]

        @pl.when(i == 0)
        def _():
            dw_ref[...] = jnp.zeros_like(dw_ref)

        for s in range(CONV_K):
            dw_ref[CONV_K - 1 - s] += _fold8(dy * taps[s])

    return pl.pallas_call(
        body, name="conv_bwd_dy", grid=(3, t // tb),
        in_specs=[pl.BlockSpec((tb, d), lambda j, i: (i, j)),
                  pl.BlockSpec((SUBLANES, d), lambda j, i: (jnp.maximum(i * hb - 1, 0), j)),
                  pl.BlockSpec((CONV_K, d), lambda j, i: (0, j)),
                  pl.BlockSpec((tb, d), lambda j, i: (i, j))],
        out_specs=[pl.BlockSpec((tb, d), lambda j, i: (i, j)),
                   pl.BlockSpec((CONV_K, SUBLANES, d), lambda j, i: (0, 0, j))],
        out_shape=[jax.ShapeDtypeStruct((t, 3 * d), F32),
                   jax.ShapeDtypeStruct((CONV_K, SUBLANES, 3 * d), F32)],
        compiler_params=_cparams(("parallel", "arbitrary")),
    )(projm, projm, conv_w, dqkv)


def _conv_bwd_dx(dy, conv_w, dprojm, d, tb):
    t = dy.shape[0]
    hb = tb // SUBLANES
    last = t // tb - 1

    def body(cur_ref, halo_ref, w_ref, alias_ref, o_ref):
        i = pl.program_id(0)
        cur = cur_ref[...]
        halo = jnp.where(i == last, 0.0, halo_ref[...])
        dc = jnp.concatenate([cur, halo], axis=0)
        acc = cur * w_ref[CONV_K - 1:CONV_K, :]
        for s in range(1, CONV_K):
            acc = acc + pltpu.roll(dc, tb + SUBLANES - s, 0)[:tb] * w_ref[CONV_K - 1 - s:CONV_K - s, :]
        o_ref[...] = acc.astype(o_ref.dtype)

    return pl.pallas_call(
        body, name="conv_bwd_dx", grid=(t // tb, 3),
        in_specs=[pl.BlockSpec((tb, d), lambda i, j: (i, j)),
                  pl.BlockSpec((SUBLANES, d), lambda i, j: (jnp.minimum((i + 1) * hb, t // SUBLANES - 1), j)),
                  pl.BlockSpec((CONV_K, d), lambda i, j: (0, j)),
                  ANY],
        out_specs=pl.BlockSpec((tb, d), lambda i, j: (i, j)),
        out_shape=jax.ShapeDtypeStruct(dprojm.shape, dprojm.dtype),
        input_output_aliases={3: 0},
        compiler_params=_cparams(("parallel", "parallel")),
    )(dy, dy, conv_w, dprojm)


def _beta_g(ba, alog, dtb):
    beta = _sigmoid(ba[:, :LANES])
    xa = ba[:, LANES:] + dtb
    softplus = jnp.maximum(xa, 0.0) + jnp.log(1.0 + jnp.exp(-jnp.abs(xa)))
    ea = jnp.exp(alog)
    return beta, -ea * softplus, ea, _sigmoid(xa)


def _inv_corrections(mats):
    ys = [-a for a in mats]
    ps = [_dot(a, a) for a in mats]
    steps = int(math.log2(CHUNK)) - 1
    for it in range(steps):
        ys = [y + p + _dot(y, p) for y, p in zip(ys, ps)]
        if it < steps - 1:
            ps = [_dot(p, p) for p in ps]
    return ys


def _chunk_masks():
    row = lax.broadcasted_iota(jnp.int32, (CHUNK, CHUNK), 0)
    col = lax.broadcasted_iota(jnp.int32, (CHUNK, CHUNK), 1)
    return row >= col, row > col, row <= col


def _col_of(mat, lane_idx, h):
    return jnp.sum(jnp.where(lane_idx == h, mat, 0.0), axis=1, keepdims=True)


def _row_of(mat, sub_idx, h):
    return jnp.sum(jnp.where(sub_idx == h, mat, 0.0), axis=0, keepdims=True)


def _phases(fns):
    return fns if len(fns) == 3 else (fns[0], lambda: None, fns[1])


def _carrying(compute, n_in, n_out, n_scratch, carried, grid):
    if carried is None:
        return compute
    ci, co = len(carried.inputs), len(carried.out_shapes)

    def body(*refs):
        ins, c_in = refs[:n_in], refs[n_in:n_in + ci]
        outs, c_out = refs[n_in + ci:n_in + ci + n_out], refs[n_in + ci + n_out:n_in + ci + n_out + co]
        scratch = refs[n_in + ci + n_out + co:]
        start, middle, finish = _phases(carried.copies(c_in, c_out, scratch[n_scratch], scratch[n_scratch + 1]))
        step, total = 0, 1
        for axis, steps in enumerate(grid):
            step = step * steps + pl.program_id(axis)
            total *= steps

        @pl.when(step == 0)
        def _():
            start()

        compute(*ins, *outs, *scratch[:n_scratch])

        @pl.when(step == (3 * total) // 4)
        def _():
            middle()

        @pl.when(step == total - 1)
        def _():
            finish()

    return body


def _dn_fwd(qkv, ba, alog, dtb, d, carried=None):
    t = qkv.shape[0]
    heads = d // DN_DK
    n_chunks = t // CHUNK
    grp = DN_GROUP_FWD if n_chunks % DN_GROUP_FWD == 0 else 1
    span = grp * CHUNK
    extra = carried or _Carried([], [], 0, None)

    def compute(qkv_ref, ba_ref, al_ref, dt_ref, o_ref, s_ref, y_ref, state):
        @pl.when(pl.program_id(0) == 0)
        def _():
            state[...] = jnp.zeros_like(state)

        tril, strict, _ = _chunk_masks()
        beta, g, _, _ = _beta_g(ba_ref[...], al_ref[...], dt_ref[...])
        lane = lax.broadcasted_iota(jnp.int32, (CHUNK, LANES), 1)
        sub = lax.broadcasted_iota(jnp.int32, (LANES, CHUNK), 0)
        rowc = lax.broadcasted_iota(jnp.int32, (CHUNK, 1), 0)
        hs = range(heads)
        units = [(c, h) for c in range(grp) for h in hs]
        un = range(len(units))
        rows = lambda c: slice(c * CHUNK, (c + 1) * CHUNK)
        gc = [_dot(jnp.where(tril, 1.0, 0.0), g[rows(c)], NN, HIGHEST) for c in range(grp)]
        gct = [m.T for m in gc]
        q = [qkv_ref[rows(c), h * DN_DK:(h + 1) * DN_DK] for c, h in units]
        k = [qkv_ref[rows(c), d + h * DN_DK:d + (h + 1) * DN_DK] for c, h in units]
        v = [qkv_ref[rows(c), 2 * d + h * DN_DK:2 * d + (h + 1) * DN_DK] for c, h in units]
        gch = [_col_of(gc[c], lane, h) for c, h in units]
        bh = [_col_of(beta[rows(c)], lane, h) for c, h in units]
        dec = [jnp.where(tril, jnp.exp(gch[n] - _row_of(gct[c], sub, h)), 0.0) for n, (c, h) in enumerate(units)]
        egc = [jnp.exp(gch[n]) for n in un]
        gl = [jnp.sum(jnp.where(rowc == CHUNK - 1, gch[n], 0.0), axis=0, keepdims=True) for n in un]
        kb = [k[n] * bh[n] for n in un]
        a = [jnp.where(strict, _dot(kb[n], k[n], NT) * dec[n], 0.0) for n in un]
        p = [_dot(q[n], k[n], NT) * dec[n] for n in un]
        ycor = _inv_corrections(a)
        rhs = [jnp.concatenate([v[n] * bh[n], kb[n] * egc[n]], axis=1) for n in un]
        sol = [rhs[n] + _dot(ycor[n], rhs[n]) for n in un]
        qg = [q[n] * egc[n] for n in un]
        kd = [k[n] * jnp.exp(gl[n] - gch[n]) for n in un]
        egl = [jnp.exp(gl[n]) for n in un]
        s_cur, s_in, o = [state[h] for h in hs], [], []
        for c in range(grp):
            ns = [c * heads + h for h in hs]
            vn = [sol[n][:, :DN_DK] - _dot(sol[n][:, DN_DK:], s_cur[h]) for h, n in enumerate(ns)]
            o += [_dot(qg[n], s_cur[h]) + _dot(p[n], vn[h]) for h, n in enumerate(ns)]
            s_in += s_cur
            s_cur = [s_cur[h] * egl[n] + _dot(kd[n], vn[h], TN) for h, n in enumerate(ns)]
        for n, (c, h) in enumerate(units):
            o_ref[rows(c), h * DN_DK:(h + 1) * DN_DK] = o[n]
            s_ref[c, h] = s_in[n]
            y_ref[h, rows(c), :] = ycor[n]
        for h in hs:
            state[h] = s_cur[h]

    res = pl.pallas_call(
        _carrying(compute, 4, 3, 1, carried, (n_chunks // grp,)),
        name="dn_fwd_carrying" if carried else "dn_fwd", grid=(n_chunks // grp,),
        in_specs=[pl.BlockSpec((span, 3 * d), lambda i: (i, 0)),
                  pl.BlockSpec((span, 2 * LANES), lambda i: (i, 0)),
                  pl.BlockSpec((1, LANES), lambda i: (0, 0)),
                  pl.BlockSpec((1, LANES), lambda i: (0, 0))] + [ANY] * len(extra.inputs),
        out_specs=[pl.BlockSpec((span, d), lambda i: (i, 0)),
                   pl.BlockSpec((grp, heads, DN_DK, DN_DK), lambda i: (i, 0, 0, 0)),
                   pl.BlockSpec((heads, span, CHUNK), lambda i: (0, i, 0))] + [ANY] * len(extra.out_shapes),
        out_shape=[jax.ShapeDtypeStruct((t, d), F32),
                   jax.ShapeDtypeStruct((n_chunks, heads, DN_DK, DN_DK), F32),
                   jax.ShapeDtypeStruct((heads, t, CHUNK), F32)] + extra.out_shapes,
        scratch_shapes=[pltpu.VMEM((heads, DN_DK, DN_DK), F32)] + (extra.scratch() if carried else []),
        compiler_params=_cparams(("arbitrary",)),
    )(qkv, ba, alog, dtb, *extra.inputs)
    return res[:3], res[3:]


def _dn_bwd(qkv, ba, alog, dtb, dout, states, ycors, d, carried=None):
    t = qkv.shape[0]
    heads = d // DN_DK
    n_chunks = t // CHUNK
    grp = DN_GROUP if n_chunks % DN_GROUP == 0 else 1
    span = grp * CHUNK
    rev = lambda i: n_chunks // grp - 1 - i
    extra = carried or _Carried([], [], 0, None)

    def compute(qkv_ref, ba_ref, al_ref, dt_ref, do_ref, s_ref, y_ref,
                dqkv_ref, dba_ref, dal_ref, ddt_ref, dstate):
        @pl.when(pl.program_id(0) == 0)
        def _():
            dstate[...] = jnp.zeros_like(dstate)
            dal_ref[...] = jnp.zeros_like(dal_ref)
            ddt_ref[...] = jnp.zeros_like(ddt_ref)

        tril, strict, triu = _chunk_masks()
        beta, g, ea, sig_a = _beta_g(ba_ref[...], al_ref[...], dt_ref[...])
        lane = lax.broadcasted_iota(jnp.int32, (CHUNK, LANES), 1)
        sub = lax.broadcasted_iota(jnp.int32, (LANES, CHUNK), 0)
        rowc = lax.broadcasted_iota(jnp.int32, (CHUNK, 1), 0)
        hs = range(heads)
        units = [(c, h) for c in range(grp) for h in hs]
        un = range(len(units))
        rows = lambda c: slice(c * CHUNK, (c + 1) * CHUNK)
        rsum = lambda x_: jnp.sum(x_, axis=1, keepdims=True)
        gc = [_dot(jnp.where(tril, 1.0, 0.0), g[rows(c)], NN, HIGHEST) for c in range(grp)]
        gct = [m.T for m in gc]
        q = [qkv_ref[rows(c), h * DN_DK:(h + 1) * DN_DK] for c, h in units]
        k = [qkv_ref[rows(c), d + h * DN_DK:d + (h + 1) * DN_DK] for c, h in units]
        v = [qkv_ref[rows(c), 2 * d + h * DN_DK:2 * d + (h + 1) * DN_DK] for c, h in units]
        dout_h = [do_ref[rows(c), h * DN_DK:(h + 1) * DN_DK] for c, h in units]
        s0 = [s_ref[c, h] for c, h in units]
        ycor = [y_ref[h, rows(c), :] for c, h in units]
        gch = [_col_of(gc[c], lane, h) for c, h in units]
        bh = [_col_of(beta[rows(c)], lane, h) for c, h in units]
        dec = [jnp.where(tril, jnp.exp(gch[n] - _row_of(gct[c], sub, h)), 0.0) for n, (c, h) in enumerate(units)]
        egc = [jnp.exp(gch[n]) for n in un]
        gl = [jnp.sum(jnp.where(rowc == CHUNK - 1, gch[n], 0.0), axis=0, keepdims=True) for n in un]
        egl = [jnp.exp(gl[n]) for n in un]
        ekd = [jnp.exp(gl[n] - gch[n]) for n in un]
        kb = [k[n] * bh[n] for n in un]
        kd = [k[n] * ekd[n] for n in un]
        qg = [q[n] * egc[n] for n in un]
        kbg = [kb[n] * egc[n] for n in un]
        a = [jnp.where(strict, _dot(kb[n], k[n], NT) * dec[n], 0.0) for n in un]
        p = [_dot(q[n], k[n], NT) * dec[n] for n in un]
        rhs = [jnp.concatenate([v[n] * bh[n], kbg[n]], axis=1) for n in un]
        sol = [rhs[n] + _dot(ycor[n], rhs[n]) for n in un]
        w = [sol[n][:, DN_DK:] for n in un]
        vn = [sol[n][:, :DN_DK] - _dot(w[n], s0[n]) for n in un]
        dqg = [_dot(dout_h[n], s0[n], NT) for n in un]
        dp = [jnp.where(tril, _dot(dout_h[n], vn[n], NT), 0.0) for n in un]
        pdo = [_dot(p[n], dout_h[n], TN) for n in un]
        qdo = [_dot(qg[n], dout_h[n], TN) for n in un]
        ds_cur = [dstate[h] for h in hs]
        dsn, dvn = [None] * len(units), [None] * len(units)
        for c in reversed(range(grp)):
            for h in hs:
                dsn[c * heads + h] = ds_cur[h]
            for h in hs:
                n = c * heads + h
                dvn[n] = pdo[n] + _dot(kd[n], ds_cur[h])
            ds_cur = [qdo[c * heads + h] + egl[c * heads + h] * ds_cur[h]
                      - _dot(w[c * heads + h], dvn[c * heads + h], TN) for h in hs]
        dkd = [_dot(vn[n], dsn[n], NT) for n in un]
        dw = [-_dot(dvn[n], s0[n], NT) for n in un]
        dgl = [jnp.sum(rsum(dsn[n] * s0[n]), axis=0, keepdims=True) * egl[n] for n in un]
        dsol = [jnp.concatenate([dvn[n], dw[n]], axis=1) for n in un]
        drhs = [dsol[n] + _dot(ycor[n], dsol[n], TN) for n in un]
        dvb = [drhs[n][:, :DN_DK] for n in un]
        dkbg = [drhs[n][:, DN_DK:] for n in un]
        da = [jnp.where(strict, -_dot(drhs[n], sol[n], NT), 0.0) for n in un]
        dma = [da[n] * dec[n] for n in un]
        dmp = [dp[n] * dec[n] for n in un]
        dkb = [_dot(dma[n], k[n]) + dkbg[n] * egc[n] for n in un]
        dq = [_dot(dmp[n], k[n]) + dqg[n] * egc[n] for n in un]
        dk = [_dot(dma[n], kb[n], TN) + _dot(dmp[n], q[n], TN) + dkd[n] * ekd[n] + dkb[n] * bh[n] for n in un]
        e = [da[n] * a[n] + dp[n] * p[n] for n in un]
        colsum = [jnp.sum(e[n], axis=0, keepdims=True) for n in un]
        tkd = [rsum(dkd[n] * kd[n]) for n in un]
        for n, (c, h) in enumerate(units):
            dqkv_ref[rows(c), h * DN_DK:(h + 1) * DN_DK] = dq[n]
            dqkv_ref[rows(c), d + h * DN_DK:d + (h + 1) * DN_DK] = dk[n]
            dqkv_ref[rows(c), 2 * d + h * DN_DK:2 * d + (h + 1) * DN_DK] = dvb[n] * bh[n]
        for h in hs:
            dstate[h] = ds_cur[h]
        valid = lane < heads
        dal_acc = jnp.zeros((SUBLANES, LANES), F32)
        ddt_acc = jnp.zeros((SUBLANES, LANES), F32)
        for c in range(grp):
            dgc_all = jnp.zeros((CHUNK, LANES), F32)
            dbeta_all = jnp.zeros((CHUNK, LANES), F32)
            colsums = jnp.zeros((LANES, CHUNK), F32)
            for h in hs:
                n = c * heads + h
                dgc = rsum(e[n]) + rsum(dqg[n] * qg[n]) - tkd[n] + rsum(dkbg[n] * kbg[n])
                dgc = dgc + jnp.where(rowc == CHUNK - 1, dgl[n] + jnp.sum(tkd[n], axis=0, keepdims=True), 0.0)
                dgc_all = dgc_all + jnp.where(lane == h, dgc, 0.0)
                colsums = colsums + jnp.where(sub == h, colsum[n], 0.0)
                dbeta_all = dbeta_all + jnp.where(lane == h, rsum(dkb[n] * k[n]) + rsum(dvb[n] * v[n]), 0.0)
            dg = _dot(jnp.where(triu, 1.0, 0.0), dgc_all - colsums.T, NN, HIGHEST)
            beta_c = beta[rows(c)]
            dbl = jnp.where(valid, dbeta_all * beta_c * (1.0 - beta_c), 0.0)
            dal = jnp.where(valid, -dg * ea * sig_a[rows(c)], 0.0)
            dba_ref[rows(c), :LANES] = dbl.astype(dba_ref.dtype)
            dba_ref[rows(c), LANES:] = dal.astype(dba_ref.dtype)
            dal_acc = dal_acc + _fold8(jnp.where(valid, dg * g[rows(c)], 0.0))
            ddt_acc = ddt_acc + _fold8(dal)
        dal_ref[...] += dal_acc
        ddt_ref[...] += ddt_acc

    res = pl.pallas_call(
        _carrying(compute, 7, 4, 1, carried, (n_chunks // grp,)),
        name="dn_bwd_carrying" if carried else "dn_bwd", grid=(n_chunks // grp,),
        in_specs=[pl.BlockSpec((span, 3 * d), lambda i: (rev(i), 0)),
                  pl.BlockSpec((span, 2 * LANES), lambda i: (rev(i), 0)),
                  pl.BlockSpec((1, LANES), lambda i: (0, 0)),
                  pl.BlockSpec((1, LANES), lambda i: (0, 0)),
                  pl.BlockSpec((span, d), lambda i: (rev(i), 0)),
                  pl.BlockSpec((grp, heads, DN_DK, DN_DK), lambda i: (rev(i), 0, 0, 0)),
                  pl.BlockSpec((heads, span, CHUNK), lambda i: (0, rev(i), 0))] + [ANY] * len(extra.inputs),
        out_specs=[pl.BlockSpec((span, 3 * d), lambda i: (rev(i), 0)),
                   pl.BlockSpec((span, 2 * LANES), lambda i: (rev(i), 0)),
                   pl.BlockSpec((SUBLANES, LANES), lambda i: (0, 0)),
                   pl.BlockSpec((SUBLANES, LANES), lambda i: (0, 0))] + [ANY] * len(extra.out_shapes),
        out_shape=[jax.ShapeDtypeStruct((t, 3 * d), F32),
                   jax.ShapeDtypeStruct((t, 2 * LANES), ACT),
                   jax.ShapeDtypeStruct((SUBLANES, LANES), F32),
                   jax.ShapeDtypeStruct((SUBLANES, LANES), F32)] + extra.out_shapes,
        scratch_shapes=[pltpu.VMEM((heads, DN_DK, DN_DK), F32)] + (extra.scratch() if carried else []),
        compiler_params=_cparams(("arbitrary",)),
    )(qkv, ba, alog, dtb, dout, states, ycors, *extra.inputs)
    return res[:4], res[4:]


def _sgu_mask():
    row = lax.broadcasted_iota(jnp.int32, (SGU_BLOCK, SGU_BLOCK), 0)
    col = lax.broadcasted_iota(jnp.int32, (SGU_BLOCK, SGU_BLOCK), 1)
    sh = int(math.log2(CHUNK))
    return lax.shift_right_logical(row, sh) >= lax.shift_right_logical(col, sh)


def _gate_sgu_fwd(o, projm, onw, lng, lnb, ws, bst, d):
    t = o.shape[0]
    heads, groups = d // DN_DK, d // SGU_GROUP_DIM
    tb = _tile(t, SGU_WINDOWS * SGU_BLOCK, SGU_BLOCK)
    row_spec = pl.BlockSpec((1, d), lambda i: (0, 0))

    def body(o_ref, z_ref, u_ref, v_ref, onw_ref, lng_ref, lnb_ref, ws_ref, bst_ref, ya_ref, yb_ref):
        for h in range(heads):
            cols = slice(h * DN_DK, (h + 1) * DN_DK)
            oh, zh = o_ref[:, cols], z_ref[:, cols]
            r = lax.rsqrt(jnp.mean(oh * oh, axis=1, keepdims=True) + RMS_EPS)
            ya_ref[:, cols] = (oh * r * onw_ref[:, cols] * (zh * _sigmoid(zh))).astype(ya_ref.dtype)
        xhat, _ = _ln_hat(_gelu(v_ref[...]))
        vgn = xhat * lng_ref[...] + lnb_ref[...]
        mask = _sgu_mask()
        lane = lax.broadcasted_iota(jnp.int32, (SGU_BLOCK, LANES), 1)
        bst_v = bst_ref[...]
        for gi in range(groups):
            cols = slice(gi * SGU_GROUP_DIM, (gi + 1) * SGU_GROUP_DIM)
            wsg = jnp.where(mask, ws_ref[gi], 0.0)
            bias = _col_of(bst_v, lane, gi)
            for win in range(tb // SGU_BLOCK):
                rows = slice(win * SGU_BLOCK, (win + 1) * SGU_BLOCK)
                sp = _dot(wsg, vgn[rows, cols]) + bias
                yb_ref[rows, cols] = (_gelu(u_ref[rows, cols]) * sp).astype(yb_ref.dtype)

    return pl.pallas_call(
        body, name="gate_sgu_fwd", grid=(t // tb,),
        in_specs=[pl.BlockSpec((tb, d), lambda i: (i, 0)),
                  pl.BlockSpec((tb, d), lambda i: (i, 3)),
                  pl.BlockSpec((tb, d), lambda i: (i, 4)),
                  pl.BlockSpec((tb, d), lambda i: (i, 5)),
                  row_spec, row_spec, row_spec,
                  pl.BlockSpec((groups, SGU_BLOCK, SGU_BLOCK), lambda i: (0, 0, 0)),
                  pl.BlockSpec((SGU_BLOCK, LANES), lambda i: (0, 0))],
        out_specs=[pl.BlockSpec((tb, d), lambda i: (i, 0)), pl.BlockSpec((tb, d), lambda i: (i, 0))],
        out_shape=[jax.ShapeDtypeStruct((t, d), ACT), jax.ShapeDtypeStruct((t, d), ACT)],
        compiler_params=_cparams(("parallel",)),
    )(o, projm, projm, projm, onw, lng, lnb, ws, bst)


def _gate_sgu_bwd(dya, dyb, o, projm, onw, lng, lnb, ws, bst, dprojm, d, carried=None):
    t = o.shape[0]
    heads, groups = d // DN_DK, d // SGU_GROUP_DIM
    tb = _tile(t, SGU_WINDOWS * SGU_BLOCK, SGU_BLOCK)
    extra = carried or _Carried([], [], 0, None)
    row_spec = pl.BlockSpec((1, d), lambda i: (0, 0))
    acc_row = pl.BlockSpec((SUBLANES, d), lambda i: (0, 0))

    def body(dya_ref, dyb_ref, o_ref, z_ref, u_ref, v_ref, onw_ref, lng_ref, lnb_ref, ws_ref, bst_ref, alias_ref,
             do_ref, dp_ref, donw_ref, dlng_ref, dlnb_ref, dws_ref, dbst_ref):
        @pl.when(pl.program_id(0) == 0)
        def _():
            for r_ in (donw_ref, dlng_ref, dlnb_ref, dws_ref, dbst_ref):
                r_[...] = jnp.zeros_like(r_)

        donw = jnp.zeros((SUBLANES, DN_DK), F32)
        for h in range(heads):
            cols = slice(h * DN_DK, (h + 1) * DN_DK)
            oh, zh, dyah, wh = o_ref[:, cols], z_ref[:, cols], dya_ref[:, cols], onw_ref[:, cols]
            r = lax.rsqrt(jnp.mean(oh * oh, axis=1, keepdims=True) + RMS_EPS)
            on = oh * r
            sz = _sigmoid(zh)
            silu_z = zh * sz
            don = dyah * wh * silu_z
            dp_ref[:, cols] = (dyah * on * wh * (sz * (1.0 + zh * (1.0 - sz)))).astype(dp_ref.dtype)
            donw = donw + _fold8(dyah * on * silu_z)
            do_ref[:, cols] = r * (don - on * jnp.mean(don * on, axis=1, keepdims=True))
        donw_ref[...] += donw

        vgp, up = v_ref[...], u_ref[...]
        xhat, rstd = _ln_hat(_gelu(vgp))
        lng_v = lng_ref[...]
        vgn = xhat * lng_v + lnb_ref[...]
        ua = _gelu(up)
        mask = _sgu_mask()
        lane = lax.broadcasted_iota(jnp.int32, (SGU_BLOCK, LANES), 1)
        bst_v = bst_ref[...]
        dbst = jnp.zeros((SGU_BLOCK, LANES), F32)
        dvgn_parts, dua_parts = [], []
        for gi in range(groups):
            cols = slice(gi * SGU_GROUP_DIM, (gi + 1) * SGU_GROUP_DIM)
            wsg = jnp.where(mask, ws_ref[gi], 0.0)
            bias = _col_of(bst_v, lane, gi)
            dws = jnp.zeros((SGU_BLOCK, SGU_BLOCK), F32)
            dvgn_g, dua_g = [], []
            for win in range(tb // SGU_BLOCK):
                rows = slice(win * SGU_BLOCK, (win + 1) * SGU_BLOCK)
                vg_g, dyb_g = vgn[rows, cols], dyb_ref[rows, cols]
                sp = _dot(wsg, vg_g) + bias
                dsp = dyb_g * ua[rows, cols]
                dua_g.append(dyb_g * sp)
                dws = dws + _dot(dsp, vg_g, NT)
                dbst = dbst + jnp.where(lane == gi, jnp.sum(dsp, axis=1, keepdims=True), 0.0)
                dvgn_g.append(_dot(wsg, dsp, TN))
            dws_ref[gi] += jnp.where(mask, dws, 0.0)
            dvgn_parts.append(jnp.concatenate(dvgn_g, axis=0))
            dua_parts.append(jnp.concatenate(dua_g, axis=0))
        dbst_ref[...] += dbst
        dvgn = jnp.concatenate(dvgn_parts, axis=1)
        dua = jnp.concatenate(dua_parts, axis=1)
        dlng_ref[...] += _fold8(dvgn * xhat)
        dlnb_ref[...] += _fold8(dvgn)
        dvga = _ln_bwd(dvgn * lng_v, xhat, rstd)
        dp_ref[:, d:2 * d] = (dua * _gelu_grad(up)).astype(dp_ref.dtype)
        dp_ref[:, 2 * d:] = (dvga * _gelu_grad(vgp)).astype(dp_ref.dtype)

    res = pl.pallas_call(
        _carrying(body, 12, 7, 0, carried, (t // tb,)),
        name="gate_sgu_bwd_carrying" if carried else "gate_sgu_bwd", grid=(t // tb,),
        in_specs=[pl.BlockSpec((tb, d), lambda i: (i, 0)),
                  pl.BlockSpec((tb, d), lambda i: (i, 0)),
                  pl.BlockSpec((tb, d), lambda i: (i, 0)),
                  pl.BlockSpec((tb, d), lambda i: (i, 3)),
                  pl.BlockSpec((tb, d), lambda i: (i, 4)),
                  pl.BlockSpec((tb, d), lambda i: (i, 5)),
                  row_spec, row_spec, row_spec,
                  pl.BlockSpec((groups, SGU_BLOCK, SGU_BLOCK), lambda i: (0, 0, 0)),
                  pl.BlockSpec((SGU_BLOCK, LANES), lambda i: (0, 0)),
                  ANY] + [ANY] * len(extra.inputs),
        out_specs=[pl.BlockSpec((tb, d), lambda i: (i, 0)),
                   pl.BlockSpec((tb, 3 * d), lambda i: (i, 1)),
                   pl.BlockSpec((SUBLANES, DN_DK), lambda i: (0, 0)),
                   acc_row, acc_row,
                   pl.BlockSpec((groups, SGU_BLOCK, SGU_BLOCK), lambda i: (0, 0, 0)),
                   pl.BlockSpec((SGU_BLOCK, LANES), lambda i: (0, 0))] + [ANY] * len(extra.out_shapes),
        out_shape=[jax.ShapeDtypeStruct((t, d), F32),
                   jax.ShapeDtypeStruct(dprojm.shape, dprojm.dtype),
                   jax.ShapeDtypeStruct((SUBLANES, DN_DK), F32),
                   jax.ShapeDtypeStruct((SUBLANES, d), F32),
                   jax.ShapeDtypeStruct((SUBLANES, d), F32),
                   jax.ShapeDtypeStruct((groups, SGU_BLOCK, SGU_BLOCK), F32),
                   jax.ShapeDtypeStruct((SGU_BLOCK, LANES), F32)] + extra.out_shapes,
        input_output_aliases={11: 1},
        scratch_shapes=extra.scratch() if carried else [],
        compiler_params=_cparams(("arbitrary",)),
    )(dya, dyb, o, projm, projm, projm, onw, lng, lnb, ws, bst, dprojm, *extra.inputs)
    return res[:7], res[7:]


def _mix_fwd(ya, yb, projm, x, wpa, wpb, wo, g1, b1, d, tb):
    t = x.shape[0]
    blk = pl.BlockSpec((tb, d), lambda i: (i, 0))
    wspec = pl.BlockSpec((d, d), lambda i: (0, 0))
    row_spec = pl.BlockSpec((1, d), lambda i: (0, 0))

    def body(ya_ref, yb_ref, ga_ref, gb_ref, x_ref, wpa_ref, wpb_ref, wo_ref, g_ref, b_ref,
             pa_ref, pb_ref, m_ref, h_ref, x1_ref, x1b_ref):
        pa = _dot(ya_ref[...], wpa_ref[...])
        pb = _dot(yb_ref[...], wpb_ref[...])
        m = _sigmoid(ga_ref[...]) * pa + _sigmoid(gb_ref[...]) * pb
        hres = ALPHA * x_ref[...] + _dot(m, wo_ref[...])
        xhat, _ = _ln_hat(hres)
        x1 = xhat * g_ref[...] + b_ref[...]
        pa_ref[...] = pa.astype(pa_ref.dtype)
        pb_ref[...] = pb.astype(pb_ref.dtype)
        m_ref[...] = m.astype(m_ref.dtype)
        h_ref[...] = hres
        x1_ref[...] = x1
        x1b_ref[...] = x1.astype(x1b_ref.dtype)

    f32_out = jax.ShapeDtypeStruct((t, d), F32)
    bf_out = jax.ShapeDtypeStruct((t, d), ACT)
    return pl.pallas_call(
        body, name="mix_fwd", grid=(t // tb,),
        in_specs=[blk, blk, pl.BlockSpec((tb, d), lambda i: (i, 6)), pl.BlockSpec((tb, d), lambda i: (i, 7)),
                  blk, wspec, wspec, wspec, row_spec, row_spec],
        out_specs=[blk] * 6,
        out_shape=[bf_out, bf_out, bf_out, f32_out, f32_out, bf_out],
        compiler_params=_cparams(("parallel",)),
    )(ya, yb, projm, projm, x, wpa, wpb, wo, g1, b1)


def _mix_bwd(dmix, pa, pb, projm, wpa, wpb, wo, d, tb):
    t = dmix.shape[0]
    blk = pl.BlockSpec((tb, d), lambda i: (i, 0))
    wspec = pl.BlockSpec((d, d), lambda i: (0, 0))

    def body(dmix_ref, pa_ref, pb_ref, ga_ref, gb_ref, wpa_ref, wpb_ref, wo_ref,
             dpa_ref, dpb_ref, dya_ref, dyb_ref, dg_ref):
        dm = _dot(dmix_ref[...], wo_ref[...], NT)
        sa, sb = _sigmoid(ga_ref[...]), _sigmoid(gb_ref[...])
        dpa, dpb = dm * sa, dm * sb
        dpa_ref[...] = dpa.astype(dpa_ref.dtype)
        dpb_ref[...] = dpb.astype(dpb_ref.dtype)
        dg_ref[:, :d] = (dm * pa_ref[...].astype(F32) * sa * (1.0 - sa)).astype(dg_ref.dtype)
        dg_ref[:, d:] = (dm * pb_ref[...].astype(F32) * sb * (1.0 - sb)).astype(dg_ref.dtype)
        dya_ref[...] = _dot(dpa, wpa_ref[...], NT)
        dyb_ref[...] = _dot(dpb, wpb_ref[...], NT)

    return pl.pallas_call(
        body, name="mix_bwd", grid=(t // tb,),
        in_specs=[blk, blk, blk, pl.BlockSpec((tb, d), lambda i: (i, 6)), pl.BlockSpec((tb, d), lambda i: (i, 7)),
                  wspec, wspec, wspec],
        out_specs=[blk, blk, blk, blk, pl.BlockSpec((tb, 2 * d), lambda i: (i, 3))],
        out_shape=[jax.ShapeDtypeStruct((t, d), ACT), jax.ShapeDtypeStruct((t, d), ACT),
                   jax.ShapeDtypeStruct((t, d), F32), jax.ShapeDtypeStruct((t, d), F32),
                   jax.ShapeDtypeStruct((t, 8 * d), ACT)],
        compiler_params=_cparams(("parallel",)),
    )(dmix, pa, pb, projm, projm, wpa, wpb, wo)


def _ffn_tail_fwd(gu, wd, x1, g, b, tb):
    t, d = x1.shape
    f = wd.shape[0]
    fc = _tile(f, MM_TILE)
    blk = pl.BlockSpec((tb, d), lambda i: (i, 0))
    row_spec = pl.BlockSpec((1, d), lambda i: (0, 0))

    def body(gu_ref, wd_ref, x_ref, g_ref, b_ref, a_ref, h_ref, y_ref, yb_ref):
        ffn = jnp.zeros((tb, d), F32)
        for c in range(f // fc):
            gp = gu_ref[:, c * fc:(c + 1) * fc].astype(F32)
            act = (gp * _sigmoid(gp) * gu_ref[:, f + c * fc:f + (c + 1) * fc].astype(F32)).astype(a_ref.dtype)
            a_ref[:, c * fc:(c + 1) * fc] = act
            ffn = ffn + _dot(act, wd_ref[c * fc:(c + 1) * fc, :])
        hres = ALPHA * x_ref[...] + ffn
        xhat, _ = _ln_hat(hres)
        y = xhat * g_ref[...] + b_ref[...]
        h_ref[...] = hres
        y_ref[...] = y
        yb_ref[...] = y.astype(yb_ref.dtype)

    return pl.pallas_call(
        body, name="ffn_tail_fwd", grid=(t // tb,),
        in_specs=[pl.BlockSpec((tb, 2 * f), lambda i: (i, 0)),
                  pl.BlockSpec((f, d), lambda i: (0, 0), pipeline_mode=pl.Buffered(1)),
                  blk, row_spec, row_spec],
        out_specs=[pl.BlockSpec((tb, f), lambda i: (i, 0)), blk, blk, blk],
        out_shape=[jax.ShapeDtypeStruct((t, f), ACT), jax.ShapeDtypeStruct((t, d), F32),
                   jax.ShapeDtypeStruct((t, d), F32), jax.ShapeDtypeStruct((t, d), ACT)],
        compiler_params=_cparams(("parallel",)),
    )(gu, wd, x1, g, b)


def _ffn_tail_bwd(dh, wd, gu, tb):
    t, d = dh.shape
    f = wd.shape[0]
    fc = _tile(f, MM_TILE)

    def body(dh_ref, wd_ref, gu_ref, dgu_ref):
        dh_v = dh_ref[...]
        for c in range(f // fc):
            da = _dot(dh_v, wd_ref[c * fc:(c + 1) * fc, :], NT)
            gp = gu_ref[:, c * fc:(c + 1) * fc].astype(F32)
            sg = _sigmoid(gp)
            dgu_ref[:, c * fc:(c + 1) * fc] = (
                da * gu_ref[:, f + c * fc:f + (c + 1) * fc].astype(F32) * sg * (1.0 + gp * (1.0 - sg))
            ).astype(dgu_ref.dtype)
            dgu_ref[:, f + c * fc:f + (c + 1) * fc] = (da * gp * sg).astype(dgu_ref.dtype)

    return pl.pallas_call(
        body, name="ffn_tail_bwd", grid=(t // tb,),
        in_specs=[pl.BlockSpec((tb, d), lambda i: (i, 0)),
                  pl.BlockSpec((f, d), lambda i: (0, 0), pipeline_mode=pl.Buffered(1)),
                  pl.BlockSpec((tb, 2 * f), lambda i: (i, 0))],
        out_specs=pl.BlockSpec((tb, 2 * f), lambda i: (i, 0)),
        out_shape=jax.ShapeDtypeStruct((t, 2 * f), ACT),
        compiler_params=_cparams(("parallel",)),
    )(dh, wd, gu)


def _ffn_head_bwd(dgu, wgu, dh2, hres, g, tb):
    t, d = dh2.shape
    f2 = wgu.shape[1]
    blk = pl.BlockSpec((tb, d), lambda i: (i, 0))
    acc = pl.BlockSpec((SUBLANES, d), lambda i: (0, 0))

    def body(dgu_ref, w_ref, dh2_ref, h_ref, g_ref, dh_ref, dhb_ref, dg_ref, db_ref):
        @pl.when(pl.program_id(0) == 0)
        def _():
            dg_ref[...] = jnp.zeros_like(dg_ref)
            db_ref[...] = jnp.zeros_like(db_ref)

        dy_v = _dot(dgu_ref[...], w_ref[...], NT) + ALPHA * dh2_ref[...]
        xhat, r = _ln_hat(h_ref[...])
        dh = _ln_bwd(dy_v * g_ref[...], xhat, r)
        dh_ref[...] = dh
        dhb_ref[...] = dh.astype(dhb_ref.dtype)
        dg_ref[...] += _fold8(dy_v * xhat)
        db_ref[...] += _fold8(dy_v)

    return pl.pallas_call(
        body, name="ffn_head_bwd", grid=(t // tb,),
        in_specs=[pl.BlockSpec((tb, f2), lambda i: (i, 0)),
                  pl.BlockSpec((d, f2), lambda i: (0, 0), pipeline_mode=pl.Buffered(1)),
                  blk, blk, pl.BlockSpec((1, d), lambda i: (0, 0))],
        out_specs=[blk, blk, acc, acc],
        out_shape=[jax.ShapeDtypeStruct((t, d), F32), jax.ShapeDtypeStruct((t, d), ACT),
                   jax.ShapeDtypeStruct((SUBLANES, d), F32), jax.ShapeDtypeStruct((SUBLANES, d), F32)],
        compiler_params=_cparams(("arbitrary",)),
    )(dgu, wgu, dh2, hres, g)


def _loss_ln_bwd(y, target, hres, g, tb):
    t, d = y.shape
    blk = pl.BlockSpec((tb, d), lambda i: (i, 0))
    acc = pl.BlockSpec((SUBLANES, d), lambda i: (0, 0))

    def body(y_ref, t_ref, h_ref, g_ref, dh_ref, dhb_ref, dg_ref, db_ref, l_ref):
        @pl.when(pl.program_id(0) == 0)
        def _():
            for r_ in (dg_ref, db_ref, l_ref):
                r_[...] = jnp.zeros_like(r_)

        err = y_ref[...] - t_ref[...]
        dy_v = err * (1.0 / d)
        sq = _fold8(err * err)
        part = sq[:, :LANES]
        for c in range(1, d // LANES):
            part = part + sq[:, c * LANES:(c + 1) * LANES]
        l_ref[...] += part
        xhat, r = _ln_hat(h_ref[...])
        dh = _ln_bwd(dy_v * g_ref[...], xhat, r)
        dh_ref[...] = dh
        dhb_ref[...] = dh.astype(dhb_ref.dtype)
        dg_ref[...] += _fold8(dy_v * xhat)
        db_ref[...] += _fold8(dy_v)

    res = pl.pallas_call(
        body, name="loss_ln_bwd", grid=(t // tb,),
        in_specs=[blk, blk, blk, pl.BlockSpec((1, d), lambda i: (0, 0))],
        out_specs=[blk, blk, acc, acc, pl.BlockSpec((SUBLANES, LANES), lambda i: (0, 0))],
        out_shape=[jax.ShapeDtypeStruct((t, d), F32), jax.ShapeDtypeStruct((t, d), ACT),
                   jax.ShapeDtypeStruct((SUBLANES, d), F32), jax.ShapeDtypeStruct((SUBLANES, d), F32),
                   jax.ShapeDtypeStruct((SUBLANES, LANES), F32)],
        compiler_params=_cparams(("arbitrary",)),
    )(y, target, hres, g)
    return res[:4], res[4]


def _adamw(w, g, m, v):
    shape = w.shape
    cols = shape[-1]
    w2, g2, m2, v2 = (a.reshape(-1, cols) for a in (w, g, m, v))
    rows = w2.shape[0]
    tr = _tile(rows, 256, SUBLANES)
    blk = pl.BlockSpec((tr, cols), lambda i: (i, 0))

    def body(w_ref, g_ref, m_ref, v_ref, d_ref, nm_ref, nv_ref):
        g_v = g_ref[...]
        nm = ADAM_B1 * m_ref[...] + (1.0 - ADAM_B1) * g_v
        nv = ADAM_B2 * v_ref[...] + (1.0 - ADAM_B2) * (g_v * g_v)
        m_hat = nm / (1.0 - ADAM_B1 ** ADAM_STEP)
        v_hat = nv / (1.0 - ADAM_B2 ** ADAM_STEP)
        d_ref[...] = -ADAM_LR * (m_hat / (jnp.sqrt(v_hat) + ADAM_EPS) + ADAM_WD * w_ref[...])
        nm_ref[...] = nm
        nv_ref[...] = nv

    out = jax.ShapeDtypeStruct((rows, cols), F32)
    res = pl.pallas_call(
        body, name="adamw", grid=(rows // tr,),
        in_specs=[blk] * 4, out_specs=[blk] * 3, out_shape=[out] * 3,
        compiler_params=_cparams(("parallel",)),
    )(w2, g2, m2, v2)
    return tuple(r.reshape(shape) for r in res)


def _place():
    x, y, c = lax.axis_index("x"), lax.axis_index("y"), lax.axis_index("c")
    return x, y, c, [(1 - x, y), (x, 1 - y), (1 - x, 1 - y)]


def _remote(src, dst, send_sems, recv_sems, k, to):
    return pltpu.make_async_remote_copy(src_ref=src, dst_ref=dst, send_sem=send_sems.at[k],
                                        recv_sem=recv_sems.at[k], device_id=to, device_id_type=MESH)


class _Carried:
    def __init__(self, inputs, out_shapes, n_sems, copies):
        self.inputs, self.out_shapes, self.n_sems, self.copies = list(inputs), list(out_shapes), n_sems, copies

    def scratch(self):
        return [pltpu.SemaphoreType.DMA((self.n_sems,)), pltpu.SemaphoreType.DMA((self.n_sems,))]


def _join_plans(first, second):
    ni, no, ns = len(first.inputs), len(first.out_shapes), first.n_sems

    def copies(in_refs, out_refs, send_sems, recv_sems):
        one = _phases(first.copies(in_refs[:ni], out_refs[:no], send_sems, recv_sems))
        two = _phases(second.copies(in_refs[ni:], out_refs[no:], send_sems.at[pl.ds(ns, second.n_sems)],
                                    recv_sems.at[pl.ds(ns, second.n_sems)]))

        def both(k):
            def run():
                one[k]()
                two[k]()
            return run

        return both(0), both(1), both(2)

    return _Carried(first.inputs + second.inputs, first.out_shapes + second.out_shapes, ns + second.n_sems, copies)


def _run_comm(name, plan):
    n_in, n_out = len(plan.inputs), len(plan.out_shapes)

    def body(*refs):
        for phase in _phases(plan.copies(refs[:n_in], refs[n_in:n_in + n_out], refs[-2], refs[-1])):
            phase()

    return pl.pallas_call(
        body, name=name, in_specs=[ANY] * n_in, out_specs=[ANY] * n_out, out_shape=plan.out_shapes,
        scratch_shapes=plan.scratch(),
    )(*plan.inputs)


def _half_rows(rows, core):
    if rows % (4 * SUBLANES):
        return None
    return pl.ds(pl.multiple_of(core * (rows // 2), 2 * SUBLANES), rows // 2)


def _all_gather_plan(shards):
    n = len(shards)

    def copies(x_refs, out_refs, send_sems, recv_sems):
        x, y, c, chips = _place()
        sibling = (x, y, 1 - c)
        mine = 2 * x + y
        split = [_half_rows(x_refs[t].shape[0], c) is not None for t in range(n)]

        def src(t):
            return x_refs[t].at[_half_rows(x_refs[t].shape[0], c)] if split[t] else x_refs[t]

        def slot(t, chip_idx, core):
            rows = _half_rows(x_refs[t].shape[0], core)
            return out_refs[t].at[chip_idx, rows] if split[t] else out_refs[t].at[chip_idx]

        def first():
            return [_remote(src(t), slot(t, mine, c), send_sems, recv_sems, 6 * t + j, (cx, cy, c))
                    for j, (cx, cy) in enumerate(chips) for t in range(n)]

        def start():
            for cp in first():
                cp.start()

        def passed():
            return [_remote(slot(t, 2 * cx + cy, c), slot(t, 2 * cx + cy, c), send_sems, recv_sems, 6 * t + 3 + j,
                            sibling) for j, (cx, cy) in enumerate(chips) for t in range(n) if split[t]]

        def middle():
            for j, (cx, cy) in enumerate(chips):
                for t in range(n):
                    theirs = slot(t, 2 * cx + cy, c)
                    _remote(theirs, theirs, send_sems, recv_sems, 6 * t + j, (cx, cy, c)).wait_recv()
            for cp in passed():
                cp.start()

        def finish():
            for j, (cx, cy) in enumerate(chips):
                for t in range(n):
                    if split[t]:
                        other = slot(t, 2 * cx + cy, 1 - c)
                        _remote(other, other, send_sems, recv_sems, 6 * t + 3 + j, sibling).wait_recv()
            for cp in first() + passed():
                cp.wait_send()

        return start, middle, finish

    return _Carried(shards, [jax.ShapeDtypeStruct((N_CHIPS,) + s.shape, s.dtype) for s in shards], 6 * n, copies)


def _sibling_exchange_plan(grads, small=None):
    n = len(grads)
    extra = [] if small is None else [small]

    def copies(in_refs, out_refs, send_sems, recv_sems):
        x, y, c, _ = _place()
        sibling = (x, y, 1 - c)

        def all_copies():
            cps = [_remote(in_refs[t].at[:, _half_rows(in_refs[t].shape[1], 1 - c), :], out_refs[t],
                           send_sems, recv_sems, t, sibling) for t in range(n)]
            if extra:
                cps.append(_remote(in_refs[n], out_refs[n], send_sems, recv_sems, n, sibling))
            return cps

        def start():
            for cp in all_copies():
                cp.start()

        def finish():
            for cp in all_copies():
                cp.wait()

        return start, finish

    shapes = [jax.ShapeDtypeStruct((g.shape[0], g.shape[1] // 2, g.shape[2]), g.dtype) for g in grads]
    shapes += [jax.ShapeDtypeStruct(s.shape, s.dtype) for s in extra]
    return _Carried(list(grads) + extra, shapes, n + 1, copies)


def _chip_exchange_plan(travel, small=None):
    n = len(travel)
    extra = [] if small is None else [small]

    def copies(in_refs, out_refs, send_sems, recv_sems):
        x, y, c, chips = _place()
        mine = 2 * x + y

        def all_copies():
            cps = []
            for j, (cx, cy) in enumerate(chips):
                to = (cx, cy, c)
                for t in range(n):
                    cps.append(_remote(in_refs[t].at[2 * cx + cy], out_refs[t].at[mine], send_sems, recv_sems,
                                       3 * t + j, to))
                if extra:
                    cps.append(_remote(in_refs[n], out_refs[n].at[mine], send_sems, recv_sems, 3 * n + j, to))
            return cps

        def start():
            for cp in all_copies():
                cp.start()

        def finish():
            for cp in all_copies():
                cp.wait()

        return start, finish

    shapes = [jax.ShapeDtypeStruct(g.shape, g.dtype) for g in travel]
    shapes += [jax.ShapeDtypeStruct((N_CHIPS,) + s.shape, s.dtype) for s in extra]
    return _Carried(list(travel) + extra, shapes, 3 * n + 3, copies)


def _sibling_merge_plan(reduced):
    n = len(reduced)

    def copies(in_refs, out_refs, send_sems, recv_sems):
        x, y, c, _ = _place()

        def all_copies():
            return [_remote(in_refs[t], out_refs[t], send_sems, recv_sems, t, (x, y, 1 - c)) for t in range(n)]

        def start():
            for cp in all_copies():
                cp.start()

        def finish():
            for cp in all_copies():
                cp.wait()

        return start, finish

    return _Carried(reduced, [jax.ShapeDtypeStruct(r.shape, r.dtype) for r in reduced], n, copies)


def _pair_sum(place, grad, land):
    n, r, c = grad.shape
    half = r // 2
    tr = _tile(half, 256, SUBLANES)
    nb = half // tr

    def body(place_ref, a_ref, b_ref, travel_ref, own_ref):
        total = a_ref[0] + b_ref[0]
        travel_ref[0] = total.astype(travel_ref.dtype)

        @pl.when(pl.program_id(1) == place_ref[1])
        def _():
            own_ref[...] = total

    return pl.pallas_call(
        body, name="grad_pair_sum",
        grid_spec=pltpu.PrefetchScalarGridSpec(
            num_scalar_prefetch=1, grid=(nb, n),
            in_specs=[pl.BlockSpec((1, tr, c), lambda i, s, p: (s, p[0] * nb + i, 0)),
                      pl.BlockSpec((1, tr, c), lambda i, s, p: (s, i, 0))],
            out_specs=[pl.BlockSpec((1, tr, c), lambda i, s, p: (s, i, 0)),
                       pl.BlockSpec((tr, c), lambda i, s, p: (i, 0))]),
        out_shape=[jax.ShapeDtypeStruct((n, half, c), BF16), jax.ShapeDtypeStruct((half, c), F32)],
        compiler_params=_cparams(("parallel", "arbitrary")),
    )(place, grad, land)


def _chip_sum(place, own, land, name):
    n, r, c = land.shape
    tr = _tile(r, 256, SUBLANES)

    def body(place_ref, own_ref, land_ref, o_ref):
        mine = place_ref[1]
        acc = jnp.zeros(o_ref.shape, F32)
        for s in range(n):
            acc = acc + jnp.where(mine == s, own_ref[...], land_ref[s].astype(F32))
        o_ref[...] = acc

    return pl.pallas_call(
        body, name=name,
        grid_spec=pltpu.PrefetchScalarGridSpec(
            num_scalar_prefetch=1, grid=(r // tr,),
            in_specs=[pl.BlockSpec((tr, c), lambda i, p: (i, 0)),
                      pl.BlockSpec((n, tr, c), lambda i, p: (0, i, 0))],
            out_specs=pl.BlockSpec((tr, c), lambda i, p: (i, 0))),
        out_shape=jax.ShapeDtypeStruct((r, c), F32),
        compiler_params=_cparams(("parallel",)),
    )(place, own, land)


def _add2(a, b):
    rows = a.shape[0]
    tr = _tile(rows, 256, SUBLANES)
    blk = pl.BlockSpec((tr, a.shape[1]), lambda i: (i, 0))

    def body(a_ref, b_ref, o_ref):
        o_ref[...] = a_ref[...] + b_ref[...]

    return pl.pallas_call(
        body, name="grad_small_pair_sum", grid=(rows // tr,), in_specs=[blk, blk], out_specs=blk,
        out_shape=jax.ShapeDtypeStruct(a.shape, F32), compiler_params=_cparams(("parallel",)),
    )(a, b)


def _merge_halves(place, mine, other):
    first_core = place[0] == 0
    return jnp.concatenate([jnp.where(first_core, mine, other), jnp.where(first_core, other, mine)], axis=0)


_BIG = (("w_in", 2), ("w_pa", 1), ("w_pb", 1), ("w_o", 1), ("w_ffn_gate", 2), ("w_ffn_up", 2),
        ("w_ffn_down", 1))
_SMALL = ("conv_w", "a_log", "dt_bias", "o_norm_w", "sgu_ln_g", "sgu_ln_b", "w_s", "b_s",
          "ln1_g", "ln1_b", "ln2_g", "ln2_b")


def _pack_small(arrays):
    pieces = []
    for a in arrays:
        if a.shape[-1] % LANES == 0:
            a2 = a.reshape(-1, LANES)
        else:
            a2 = jnp.pad(a.reshape(-1, a.shape[-1]), ((0, 0), (0, LANES - a.shape[-1])))
        pieces.append(jnp.pad(a2, ((0, -a2.shape[0] % SUBLANES), (0, 0))))
    return jnp.concatenate(pieces, axis=0)


def _unpack_small(buf, like):
    out, off = [], 0
    for a in like:
        if a.shape[-1] % LANES == 0:
            rows = a.size // LANES
            out.append(buf[off:off + rows].reshape(a.shape))
        else:
            rows = a.size // a.shape[-1]
            out.append(buf[off:off + rows, :a.shape[-1]].reshape(a.shape))
        off += -(-rows // SUBLANES) * SUBLANES
    return out


def _unshard(gathered, local, chip, axis):
    parts = [jnp.where(chip == s, local, gathered[s]) for s in range(N_CHIPS)]
    return jnp.concatenate(parts, axis=axis - 1)


def _to_shards(full, axis):
    l, r, c = full.shape
    if axis == 1:
        return full.reshape(l, N_CHIPS, r // N_CHIPS, c)
    return jnp.transpose(full.reshape(l, r, N_CHIPS, c // N_CHIPS), (0, 2, 1, 3))


def _row(v, width=None):
    v = v.reshape(1, -1).astype(F32)
    if width is not None and v.shape[1] < width:
        v = jnp.pad(v, ((0, 0), (0, width - v.shape[1])))
    return v


def _layer_consts(p, l, d):
    heads = d // DN_DK
    return dict(
        alog=_row(p["a_log"][l], LANES), dtb=_row(p["dt_bias"][l], LANES),
        onw=_row(jnp.tile(p["o_norm_w"][l], heads)),
        lng=_row(p["sgu_ln_g"][l]), lnb=_row(p["sgu_ln_b"][l]),
        ws=p["w_s"][l].astype(F32),
        bst=jnp.pad(p["b_s"][l].T, ((0, 0), (0, LANES - p["b_s"].shape[1]))),
        g1=_row(p["ln1_g"][l]), b1=_row(p["ln1_b"][l]), g2=_row(p["ln2_g"][l]), b2=_row(p["ln2_b"][l]))


class _NoComm:
    def with_proj_main(self):
        return None

    def after_proj_main(self, got):
        pass

    def weights(self, full):
        return full

    def with_dn_fwd(self):
        return None

    def after_dn_fwd(self, got):
        pass

    def with_ffn_in_dw(self):
        return None

    def after_ffn_in_dw(self, got):
        pass

    def after_branch_grads(self, g):
        pass

    def with_dn_bwd(self):
        return None

    def after_dn_bwd(self, got):
        pass

    def with_proj_main_dw(self):
        return None

    def after_proj_main_dw(self, got):
        pass

    def with_ffn_in(self):
        return None

    def after_ffn_in(self, got):
        pass

    def after_all_grads(self, g):
        pass

    def with_gate_sgu_bwd(self):
        return None

    def after_gate_sgu_bwd(self, got):
        pass

    def with_proj_gates_dx(self):
        return None

    def after_proj_gates_dx(self, got):
        pass

    def with_proj_main_dx(self):
        return None

    def after_proj_main_dx(self, got):
        pass


def _carry(carried, after, call, *args, **kw):
    if carried is None:
        return call(*args, **kw)
    out, got = call(*args, carried=carried, **kw)
    after(got)
    return out


def _in_proj_weights(w_in, d):
    heads, q4 = d // DN_DK, 4 * d
    wba = jnp.zeros((d, 2 * LANES), w_in.dtype)
    wba = wba.at[:, :heads].set(w_in[:, q4:q4 + heads])
    wba = wba.at[:, LANES:LANES + heads].set(w_in[:, q4 + heads:q4 + 2 * heads])
    return jnp.concatenate([w_in[:, :q4], w_in[:, q4 + 2 * heads:]], axis=1), wba


def _layer_fwd(x, xb, full, cl, d, tb, comm):
    wm, wba = _in_proj_weights(full["w_in"], d)
    projm = _carry(comm.with_proj_main(), comm.after_proj_main, _matmul, xb, wm, NN, "proj_main", tn=MM_WIDE)
    full = comm.weights(full)
    wl = dict(wm=wm, wba=wba, conv=full["conv_w"], wpa=full["w_pa"], wpb=full["w_pb"], wo=full["w_o"],
              wgu=jnp.concatenate([full["w_ffn_gate"], full["w_ffn_up"]], axis=1), wd=full["w_ffn_down"])
    ba = _matmul(xb, wba, NN, "proj_gates")
    qkv = _conv_fwd(projm, wl["conv"], d, _tile(x.shape[0], 2 * tb, SUBLANES))
    (o, states, ycors), got = _dn_fwd(qkv, ba, cl["alog"], cl["dtb"], d, comm.with_dn_fwd())
    comm.after_dn_fwd(got)
    ya, yb = _gate_sgu_fwd(o, projm, cl["onw"], cl["lng"], cl["lnb"], cl["ws"], cl["bst"], d)
    pa, pb, m, h1, x1, x1b = _mix_fwd(ya, yb, projm, x, wl["wpa"], wl["wpb"], wl["wo"], cl["g1"], cl["b1"], d, tb)
    gu = _carry(comm.with_ffn_in(), comm.after_ffn_in, _matmul, x1b, wl["wgu"], NN, "ffn_in", out_dtype=ACT,
                tn=2 * MM_TILE)
    act, h2, x2, x2b = _ffn_tail_fwd(gu, wl["wd"], x1, cl["g2"], cl["b2"], _tile(x.shape[0], 2 * tb, SUBLANES))
    saved = dict(xb=xb, projm=projm, ba=ba, qkv=qkv, o=o, states=states, ycors=ycors, ya=ya, yb=yb,
                 pa=pa, pb=pb, m=m, h1=h1, x1b=x1b, gu=gu, act=act, h2=h2)
    return x2, x2b, saved, wl


def _layer_bwd(sv, wl, cl, d, tb, comm, ln2_bwd, next_ln=None):
    g = {}
    dh2, dh2b, dg2, db2 = ln2_bwd
    g["ln2_g"], g["ln2_b"] = dg2.sum(0), db2.sum(0)
    g["wd"] = _matmul(sv["act"], dh2b, TN, "ffn_out_dw", tk=MM_WIDE)
    tbf = _tile(sv["xb"].shape[0], 2 * tb, SUBLANES)
    dgu = _ffn_tail_bwd(dh2b, wl["wd"], sv["gu"], tbf)
    g["wgu"] = _carry(comm.with_ffn_in_dw(), comm.after_ffn_in_dw, _matmul, sv["x1b"], dgu, TN, "ffn_in_dw",
                      tk=MM_WIDE)
    dh1, dh1b, dg1, db1 = _ffn_head_bwd(dgu, wl["wgu"], dh2, sv["h1"], cl["g1"], tbf)
    g["ln1_g"], g["ln1_b"] = dg1.sum(0), db1.sum(0)
    g["wo"] = _matmul(sv["m"], dh1b, TN, "wo_dw", tk=MM_WIDE)
    dpa, dpb, dya, dyb, dprojm = _mix_bwd(dh1b, sv["pa"], sv["pb"], sv["projm"], wl["wpa"], wl["wpb"], wl["wo"], d, tb)
    g["wpa"] = _matmul(sv["ya"], dpa, TN, "wpa_dw", tk=MM_WIDE)
    g["wpb"] = _matmul(sv["yb"], dpb, TN, "wpb_dw", tk=MM_WIDE)
    comm.after_branch_grads(g)
    (do, dprojm, donw, dlng, dlnb, dws, dbst), got = _gate_sgu_bwd(
        dya, dyb, sv["o"], sv["projm"], cl["onw"], cl["lng"], cl["lnb"], cl["ws"], cl["bst"], dprojm, d,
        comm.with_gate_sgu_bwd())
    comm.after_gate_sgu_bwd(got)
    heads, groups = d // DN_DK, d // SGU_GROUP_DIM
    g["o_norm_w"], g["sgu_ln_g"], g["sgu_ln_b"] = donw.sum(0), dlng.sum(0), dlnb.sum(0)
    g["w_s"], g["b_s"] = dws, dbst[:, :groups].T
    (dqkv, dba, dal, ddt), got = _dn_bwd(sv["qkv"], sv["ba"], cl["alog"], cl["dtb"], do, sv["states"],
                                         sv["ycors"], d, comm.with_dn_bwd())
    comm.after_dn_bwd(got)
    g["a_log"], g["dt_bias"] = dal.sum(0)[:heads], ddt.sum(0)[:heads]
    tbc = _tile(sv["xb"].shape[0], 2 * tb, SUBLANES)
    dy, dcw = _conv_bwd_dy(sv["projm"], wl["conv"], dqkv, d, tbc)
    g["conv_w"] = dcw.sum(1)
    dprojm = _conv_bwd_dx(dy, wl["conv"], dprojm, d, tbc)
    g["wba"] = _matmul(sv["xb"], dba, TN, "proj_gates_dw")
    g["wm"] = _carry(comm.with_proj_main_dw(), comm.after_proj_main_dw, _matmul, sv["xb"], dprojm, TN,
                     "proj_main_dw", tn=MM_WIDE)
    comm.after_all_grads(g)
    dx = _carry(comm.with_proj_gates_dx(), comm.after_proj_gates_dx, _matmul, dba, wl["wba"], NT, "proj_gates_dx",
                add=dh1, coef=ALPHA)
    if next_ln is not None:
        return _matmul(dprojm, wl["wm"], NT, "proj_main_dx", add=dx, tm=MM_TILE // 3, tk=MM_WIDE, ln=next_ln), g
    dx = _carry(comm.with_proj_main_dx(), comm.after_proj_main_dx, _matmul, dprojm, wl["wm"], NT, "proj_main_dx",
                add=dx, tk=MM_WIDE)
    return dx, g


_BRANCH = ("w_pa", "w_pb", "w_o", "w_ffn_gate", "w_ffn_up", "w_ffn_down")


def _grad_shards(g, d, keys):
    heads, q4 = d // DN_DK, 4 * d
    rows = lambda a: a.reshape(N_CHIPS, -1, a.shape[1])
    out = {}
    if "w_in" in keys:
        gm, gba, wsh = g["wm"], g["wba"], 2 * d + heads // 2
        out["w_in"] = jnp.stack([gm[:, :wsh],
                                 jnp.concatenate([gm[:, wsh:q4], gba[:, :heads]], axis=1),
                                 jnp.concatenate([gba[:, LANES:LANES + heads], gm[:, q4:q4 + wsh - heads]], axis=1),
                                 gm[:, q4 + wsh - heads:]])
    if "w_pa" in keys:
        ggu = g["wgu"]
        f = ggu.shape[1] // 2
        fs = f // N_CHIPS
        out.update({
            "w_pa": rows(g["wpa"]), "w_pb": rows(g["wpb"]), "w_o": rows(g["wo"]), "w_ffn_down": rows(g["wd"]),
            "w_ffn_gate": jnp.stack([ggu[:, s * fs:(s + 1) * fs] for s in range(N_CHIPS)]),
            "w_ffn_up": jnp.stack([ggu[:, f + s * fs:f + (s + 1) * fs] for s in range(N_CHIPS)])})
    return out


def _local_step(x, target, full0, full1_of, small_w, comm0=None):
    t, d = x.shape
    tb = _tile(t, 256, SUBLANES)
    comm0 = comm0 or _NoComm()
    consts = [_layer_consts(small_w, l, d) for l in range(DEPTH)]
    x1, x1b, sv0, w0 = _layer_fwd(x, x.astype(ACT), full0, consts[0], d, tb, comm0)
    x2, _, sv1, w1 = _layer_fwd(x1, x1b, full1_of(), consts[1], d, tb, _NoComm())
    ln2_bwd, loss_parts = _loss_ln_bwd(x2, target, sv1["h2"], consts[1]["g2"], tb)
    ln2_bwd, g1 = _layer_bwd(sv1, w1, consts[1], d, tb, _NoComm(), ln2_bwd, next_ln=(sv0["h2"], consts[0]["g2"]))
    comm0.layer1_grads = g1
    grad_x, g0 = _layer_bwd(sv0, w0, consts[0], d, tb, comm0, ln2_bwd)
    return loss_parts, grad_x, [g0, g1]


def kernel(x, w_in, conv_w, a_log, dt_bias, o_norm_w, sgu_ln_g, sgu_ln_b, w_s, b_s, w_pa, w_pb, w_o, ln1_g, ln1_b, w_ffn_gate, w_ffn_up, w_ffn_down, ln2_g, ln2_b, loss_target, m_w_in, m_conv_w, m_a_log, m_dt_bias, m_o_norm_w, m_sgu_ln_g, m_sgu_ln_b, m_w_s, m_b_s, m_w_pa, m_w_pb, m_w_o, m_ln1_g, m_ln1_b, m_w_ffn_gate, m_w_ffn_up, m_w_ffn_down, m_ln2_g, m_ln2_b, v_w_in, v_conv_w, v_a_log, v_dt_bias, v_o_norm_w, v_sgu_ln_g, v_sgu_ln_b, v_w_s, v_b_s, v_w_pa, v_w_pb, v_w_o, v_ln1_g, v_ln1_b, v_w_ffn_gate, v_w_ffn_up, v_w_ffn_down, v_ln2_g, v_ln2_b):
    names = ("w_in", "conv_w", "a_log", "dt_bias", "o_norm_w", "sgu_ln_g", "sgu_ln_b", "w_s", "b_s", "w_pa",
             "w_pb", "w_o", "ln1_g", "ln1_b", "w_ffn_gate", "w_ffn_up", "w_ffn_down", "ln2_g", "ln2_b")
    w = dict(zip(names, (w_in, conv_w, a_log, dt_bias, o_norm_w, sgu_ln_g, sgu_ln_b, w_s, b_s, w_pa, w_pb, w_o,
                         ln1_g, ln1_b, w_ffn_gate, w_ffn_up, w_ffn_down, ln2_g, ln2_b)))
    mom = dict(zip(names, (m_w_in, m_conv_w, m_a_log, m_dt_bias, m_o_norm_w, m_sgu_ln_g, m_sgu_ln_b, m_w_s, m_b_s,
                           m_w_pa, m_w_pb, m_w_o, m_ln1_g, m_ln1_b, m_w_ffn_gate, m_w_ffn_up, m_w_ffn_down,
                           m_ln2_g, m_ln2_b)))
    var = dict(zip(names, (v_w_in, v_conv_w, v_a_log, v_dt_bias, v_o_norm_w, v_sgu_ln_g, v_sgu_ln_b, v_w_s, v_b_s,
                           v_w_pa, v_w_pb, v_w_o, v_ln1_g, v_ln1_b, v_w_ffn_gate, v_w_ffn_up, v_w_ffn_down,
                           v_ln2_g, v_ln2_b)))
    chip = 2 * lax.axis_index("x") + lax.axis_index("y")
    place = jnp.stack([lax.axis_index("c"), chip]).astype(jnp.int32)

    big = [k for k, _ in _BIG]
    axis_of = dict(_BIG)
    local = {k: w[k].astype(BF16) for k in big}
    local["conv_w"] = conv_w

    def gather_plan(l, keys):
        return _all_gather_plan([local[k][l] for k in keys])

    def full_of(l, keys, gathered):
        return {k: _unshard(gt, local[k][l], chip, axis_of.get(k, 2)) for k, gt in zip(keys, gathered)}

    def pair_sums(grads_l, keys, lands):
        return [_pair_sum(place, grads_l[k], land) for k, land in zip(keys, lands)]

    def chip_sums(pairs, lands):
        return [_chip_sum(place, p[1], land, "grad_chip_sum") for p, land in zip(pairs, lands)]

    class Layer0Comm(_NoComm):
        def with_proj_main(self):
            return gather_plan(0, _BRANCH)

        def after_proj_main(self, got):
            self.rest = full_of(0, _BRANCH, got)

        def weights(self, full):
            return {**full, **self.rest}

        def with_dn_fwd(self):
            return gather_plan(1, mixer)

        def after_dn_fwd(self, got):
            self.full1 = full_of(1, mixer, got)

        def with_ffn_in(self):
            return gather_plan(1, ffn)

        def after_ffn_in(self, got):
            self.full1.update(full_of(1, ffn, got))

        def with_ffn_in_dw(self):
            self.g1 = _grad_shards(self.layer1_grads, x.shape[-1], big)
            return _sibling_exchange_plan([self.g1[k] for k in big])

        def after_ffn_in_dw(self, got):
            self.pairs1 = pair_sums(self.g1, big, got)

        def with_dn_bwd(self):
            return _chip_exchange_plan([p[0] for p in self.pairs1])

        def after_dn_bwd(self, got):
            self.red1 = chip_sums(self.pairs1, got)

        def after_branch_grads(self, g0):
            self.shards0 = _grad_shards(g0, x.shape[-1], _BRANCH)

        def with_gate_sgu_bwd(self):
            return _sibling_exchange_plan([self.shards0[k] for k in _BRANCH])

        def after_gate_sgu_bwd(self, got):
            self.pairs0 = pair_sums(self.shards0, _BRANCH, got)

        def with_proj_main_dw(self):
            return _chip_exchange_plan([p[0] for p in self.pairs0])

        def after_proj_main_dw(self, got):
            self.red0 = chip_sums(self.pairs0, got)

        def after_all_grads(self, g0):
            self.g_in = _grad_shards(g0, x.shape[-1], ["w_in"])["w_in"]
            self.small_g = {k: jnp.stack([g0[k], self.layer1_grads[k]]) for k in _SMALL}
            self.small = _pack_small([self.small_g[k] for k in _SMALL])

        def with_proj_gates_dx(self):
            return _sibling_exchange_plan([self.g_in], self.small)

        def after_proj_gates_dx(self, got):
            self.pair_in = _pair_sum(place, self.g_in, got[0])
            self.small_chip = _add2(self.small, got[1])

        def with_proj_main_dx(self):
            return _join_plans(_chip_exchange_plan([self.pair_in[0]], self.small_chip),
                               _sibling_merge_plan(self.red0 + self.red1))

        def after_proj_main_dx(self, got):
            self.red_in = _chip_sum(place, self.pair_in[1], got[0], "grad_chip_sum")
            self.small_total = _chip_sum(place, self.small_chip, got[1], "grad_small_chip_sum")
            self.others = got[2:]

    comm = Layer0Comm()
    first, mixer, ffn = ["w_in", "conv_w"], ["w_in", "conv_w", "w_pa", "w_pb", "w_o"], list(_BRANCH[3:])
    full0 = full_of(0, first, _run_comm("all_gather_weights", gather_plan(0, first)))
    small_w = {k: w[k] for k in _SMALL if k != "conv_w"}
    loss_parts, grad_x, g = _local_step(x[0], loss_target[0], full0, lambda: comm.full1, small_w, comm)

    reduced = [comm.red_in] + comm.red0 + comm.red1
    others = list(_run_comm("grad_sibling_merge", _sibling_merge_plan([comm.red_in]))) + list(comm.others)
    halves = [_merge_halves(place, mine, other) for mine, other in zip(reduced, others)]
    grads = {k: jnp.stack([halves[i], halves[len(big) + i]]) for i, k in enumerate(big)}
    grads.update(zip(_SMALL, _unpack_small(comm.small_total, [comm.small_g[k] for k in _SMALL])))
    grads["conv_w"] = lax.dynamic_index_in_dim(_to_shards(grads["conv_w"], 2), chip, 1, keepdims=False)

    delta, new_m, new_v = {}, {}, {}
    for k in [k for k, _ in _BIG] + ["conv_w"]:
        delta[k], new_m[k], new_v[k] = _adamw(w[k], grads[k], mom[k], var[k])
    rep = [k for k in _SMALL if k != "conv_w"]
    pack = lambda dct: _pack_small([dct[k] for k in rep])
    packed = _adamw(pack(w), pack(grads), pack(mom), pack(var))
    for dst, src in zip((delta, new_m, new_v), packed):
        dst.update(zip(rep, _unpack_small(src, [w[k] for k in rep])))

    loss = 0.5 * lax.psum(jnp.sum(loss_parts), ("x", "y", "c")) / x.shape[-1]
    return (loss, grad_x[None], *[grads[k] for k in names], *[delta[k] for k in names],
            *[new_m[k] for k in names], *[new_v[k] for k in names])
```
